```python
import jax
import jax.numpy as jnp
from jax import lax
import numpy as np

D_MODEL = 1024
BATCH = 8
SEQ = 4096
DEPTH = 1

GRID_W = 64
CTX_LEN = 256
D_MIX = D_MODEL
D_FOURIER = D_MIX // 2
FOURIER_GROUPS = 4
FOURIER_GD = D_FOURIER // FOURIER_GROUPS
D_RG = D_MIX - D_FOURIER
RG_HEADS = 8
RG_HD = D_RG // RG_HEADS
D_IN = D_FOURIER + 2 * D_RG
CONV_W = 4
CONV_PAD_LO = 2
CONV_PAD_HI = CONV_W - 1 - CONV_PAD_LO
RG_C = 8.0
N_EXPERTS = 32
TOP_K = 4
D_FF = D_MODEL
SWIGLU_LIMIT = 7.0
SWIGLU_ALPHA = 1.702
MOE_BLOCK = 256
N_MOD = 6
EPS = 1e-6

kernel_name = 'hybrid_fourier_rglru_moe_dit'


def rmsnorm(x, g):
    xf = x.astype(jnp.float32)
    y = xf * lax.rsqrt(jnp.mean(xf * xf, axis=-1, keepdims=True) + EPS)
    return (y * g.astype(jnp.float32)).astype(x.dtype)


def modulate(h, shift, scale):
    return h * (1 + scale) + shift


def adaln_params(cond, w_mod, b_mod):
    m = jax.nn.silu(cond) @ w_mod + b_mod
    return [t[..., None, :] for t in jnp.split(m, N_MOD, axis=-1)]


def serpentine(t):
    b, n, ch = t.shape
    rows = n // GRID_W
    g = t.reshape(b, rows, GRID_W, ch)
    odd = (jnp.arange(rows) % 2 == 1)[None, :, None, None]
    return jnp.where(odd, g[:, :, ::-1], g).reshape(b, n, ch)


def dwconv_centred(t, w, bias):
    n = t.shape[1]
    tp = jnp.pad(t, ((0, 0), (CONV_PAD_LO, CONV_PAD_HI), (0, 0)))
    out = tp[:, 0:n] * w[0]
    for k in range(1, CONV_W):
        out = out + tp[:, k:k + n] * w[k]
    return out + bias


def blockdiag(t, w, bias):
    b, n, _ = t.shape
    th = t.reshape(b, n, RG_HEADS, RG_HD)
    return jnp.einsum('bnhi,hij->bnhj', th, w).reshape(b, n, D_RG) + bias


def rglru_coeffs(xc, w_a, b_a, w_x, b_x, lam):
    f32 = jnp.float32
    r = jax.nn.sigmoid(blockdiag(xc, w_a.astype(f32), b_a.astype(f32)))
    i = jax.nn.sigmoid(blockdiag(xc, w_x.astype(f32), b_x.astype(f32)))
    log_a = -RG_C * r * jax.nn.softplus(-lam.astype(f32))
    a = jnp.exp(log_a)
    drive = jnp.sqrt(-jnp.expm1(2.0 * log_a)) * (i * xc)
    return a, drive


def linear_recurrence(a, drive, h0):
    drive = drive.at[:, 0].add(a[:, 0] * h0)

    def combine(left, right):
        return (left[0] * right[0], right[0] * left[1] + right[1])

    _, h = lax.associative_scan(combine, (a, drive), axis=1)
    return h


def rglru_bidirectional(xr_lat, xr_ctx, conv_w, conv_b, w_a, b_a, w_x, b_x, lam):
    xc_lat = dwconv_centred(xr_lat, conv_w, conv_b).astype(jnp.float32)
    xc_ctx = dwconv_centred(xr_ctx, conv_w, conv_b).astype(jnp.float32)
    h0 = jnp.zeros((xr_ctx.shape[0], D_RG), jnp.float32)
    ys_lat = []
    ys_ctx = []
    for d in range(2):
        flip = (lambda t: t[:, ::-1]) if d == 1 else (lambda t: t)
        a_c, u_c = rglru_coeffs(flip(xc_ctx), w_a[d], b_a[d], w_x[d], b_x[d], lam[d])
        a_l, u_l = rglru_coeffs(flip(xc_lat), w_a[d], b_a[d], w_x[d], b_x[d], lam[d])
        h_c = linear_recurrence(a_c, u_c, h0)
        h_l = linear_recurrence(a_l, u_l, h_c[:, -1])
        ys_ctx.append(flip(h_c))
        ys_lat.append(flip(h_l))
    return ys_lat[0] + ys_lat[1], ys_ctx[0] + ys_ctx[1]


def fourier_mix(u, w_f):
    b, n, _ = u.shape
    g = u.astype(jnp.float32).reshape(b, n, FOURIER_GROUPS, FOURIER_GD)
    f = jnp.fft.fft2(g, axes=(1, 3), norm='ortho').real
    out = jnp.einsum('bngi,gij->bngj', f, w_f.astype(jnp.float32))
    return out.reshape(b, n, D_FOURIER).astype(u.dtype)


def merge_heads(fourier_out, rg_out, g_out_f, g_out_r, w_out):
    cat = jnp.concatenate([rmsnorm(fourier_out, g_out_f), rmsnorm(rg_out, g_out_r)], axis=-1)
    return cat @ w_out


def token_mixer(h_lat, h_ctx, w_in, w_fourier, conv_w, conv_b, rg_w_a, rg_b_a, rg_w_x, rg_b_x,
                rg_lam, g_out_f, g_out_r, w_out, with_ctx_out):
    cuts = [D_FOURIER, D_FOURIER + D_RG]
    f_lat, xr_lat, gr_lat = jnp.split(h_lat @ w_in, cuts, axis=-1)
    f_ctx, xr_ctx, gr_ctx = jnp.split(h_ctx @ w_in, cuts, axis=-1)
    rg_lat_s, rg_ctx = rglru_bidirectional(serpentine(xr_lat), xr_ctx, conv_w, conv_b,
                                           rg_w_a, rg_b_a, rg_w_x, rg_b_x, rg_lam)
    rg_lat = serpentine(rg_lat_s).astype(h_lat.dtype) * jax.nn.gelu(gr_lat)
    out_lat = merge_heads(fourier_mix(f_lat, w_fourier), rg_lat, g_out_f, g_out_r, w_out)
    if not with_ctx_out:
        return out_lat, None
    rg_c = rg_ctx.astype(h_ctx.dtype) * jax.nn.gelu(gr_ctx)
    out_ctx = merge_heads(fourier_mix(f_ctx, w_fourier), rg_c, g_out_f, g_out_r, w_out)
    return out_lat, out_ctx


def moe_ffn(h, w_router, b_router, w_gu, b_gu, w_down, b_down):
    shp = h.shape
    hf = h.reshape(-1, D_MODEL)
    n_tok = hf.shape[0]
    logits = (hf @ w_router).astype(jnp.float32) + b_router.astype(jnp.float32)
    top_val, top_idx = lax.top_k(logits, TOP_K)
    gates = jax.nn.softmax(top_val, axis=-1)
    n_assign = n_tok * TOP_K
    e_flat = top_idx.reshape(-1).astype(jnp.int32)
    tok_flat = jnp.arange(n_assign, dtype=jnp.int32) // TOP_K
    g_flat = gates.reshape(-1)
    order = jnp.argsort(e_flat)
    e_sorted = e_flat[order]
    counts = jnp.bincount(e_flat, length=N_EXPERTS).astype(jnp.int32)
    padded = (counts + MOE_BLOCK - 1) // MOE_BLOCK * MOE_BLOCK
    pad_end = jnp.cumsum(padded)
    pad_start = pad_end - padded
    grp_start = jnp.cumsum(counts) - counts
    dest = pad_start[e_sorted] + jnp.arange(n_assign, dtype=jnp.int32) - grp_start[e_sorted]
    n_blocks = -(-n_assign // MOE_BLOCK) + N_EXPERTS
    cap = n_blocks * MOE_BLOCK
    row_tok = jnp.zeros((cap,), jnp.int32).at[dest].set(tok_flat[order])
    row_gate = jnp.zeros((cap,), jnp.float32).at[dest].set(g_flat[order])
    blk_start = jnp.arange(n_blocks, dtype=jnp.int32) * MOE_BLOCK
    blk_expert = jnp.minimum(jnp.searchsorted(pad_end, blk_start, side='right'), N_EXPERTS - 1)

    def run_block(args):
        tok, gate, e = args
        xb = hf[tok]
        gu = xb @ w_gu[e] + b_gu[e]
        gt = jnp.minimum(gu[:, :D_FF], SWIGLU_LIMIT)
        up = jnp.clip(gu[:, D_FF:], -SWIGLU_LIMIT, SWIGLU_LIMIT)
        act = (up + 1) * (gt * jax.nn.sigmoid(SWIGLU_ALPHA * gt))
        out = act @ w_down[e] + b_down[e]
        return (out * gate[:, None]).astype(hf.dtype)

    outs = lax.map(run_block, (row_tok.reshape(n_blocks, MOE_BLOCK),
                               row_gate.reshape(n_blocks, MOE_BLOCK), blk_expert))
    y = jnp.zeros_like(hf).at[row_tok].add(outs.reshape(cap, D_MODEL))
    return y.reshape(shp)


def setup_inputs(seed: int = 0) -> dict:
    key = jax.random.key(seed)
    ks = jax.random.split(key, 32)

    def nrm(k, shape, scale):
        return jax.random.normal(k, shape, jnp.float32) * scale

    u = jax.random.uniform(ks[14], (DEPTH, 2, D_RG), jnp.float32, 0.9, 0.999)
    a0 = u ** (1.0 / RG_C)
    rg_lam = jnp.log(a0) - jnp.log1p(-a0)
    return {
        'x': nrm(ks[0], (BATCH, SEQ, D_MODEL), 1.0),
        'c': nrm(ks[1], (BATCH, D_MODEL), 1.0),
        'ctx': nrm(ks[2], (BATCH, CTX_LEN, D_MODEL), 1.0),
        'c_ctx': nrm(ks[3], (D_MODEL,), 1.0),
        'w_mod': nrm(ks[4], (DEPTH, D_MODEL, N_MOD * D_MODEL), 0.5 * D_MODEL ** -0.5),
        'b_mod': nrm(ks[5], (DEPTH, N_MOD * D_MODEL), 0.02),
        'g_norm_mix': 1.0 + nrm(ks[6], (DEPTH, D_MODEL), 0.02),
        'g_norm_ffn': 1.0 + nrm(ks[7], (DEPTH, D_MODEL), 0.02),
        'w_in': nrm(ks[8], (DEPTH, D_MODEL, D_IN), D_MODEL ** -0.5),
        'w_fourier': nrm(ks[9], (DEPTH, FOURIER_GROUPS, FOURIER_GD, FOURIER_GD), FOURIER_GD ** -0.5),
        'conv_w': nrm(ks[10], (DEPTH, CONV_W, D_RG), CONV_W ** -0.5),
        'conv_b': nrm(ks[11], (DEPTH, D_RG), 0.02),
        'rg_w_a': nrm(ks[12], (DEPTH, 2, RG_HEADS, RG_HD, RG_HD), RG_HD ** -0.5),
        'rg_b_a': nrm(ks[13], (DEPTH, 2, D_RG), 0.02),
        'rg_w_x': nrm(ks[15], (DEPTH, 2, RG_HEADS, RG_HD, RG_HD), RG_HD ** -0.5),
        'rg_b_x': nrm(ks[16], (DEPTH, 2, D_RG), 0.02),
        'rg_lam': rg_lam,
        'g_out_fourier': 1.0 + nrm(ks[17], (DEPTH, D_FOURIER), 0.02),
        'g_out_rg': 1.0 + nrm(ks[18], (DEPTH, D_RG), 0.02),
        'w_out': nrm(ks[19], (DEPTH, D_MIX, D_MODEL), D_MIX ** -0.5),
        'w_router': nrm(ks[20], (DEPTH, D_MODEL, N_EXPERTS), D_MODEL ** -0.5),
        'b_router': nrm(ks[21], (DEPTH, N_EXPERTS), 0.01),
        'w_gate_up': nrm(ks[22], (DEPTH, N_EXPERTS, D_MODEL, 2 * D_FF), D_MODEL ** -0.5),
        'b_gate_up': nrm(ks[23], (DEPTH, N_EXPERTS, 2 * D_FF), 0.01),
        'w_down': nrm(ks[24], (DEPTH, N_EXPERTS, D_FF, D_MODEL), D_FF ** -0.5),
        'b_down': nrm(ks[25], (DEPTH, N_EXPERTS, D_MODEL), 0.01),
        'g_final': 1.0 + nrm(ks[26], (D_MODEL,), 0.02),
    }


def reference(x, c, ctx, c_ctx, w_mod, b_mod, g_norm_mix, g_norm_ffn, w_in, w_fourier, conv_w,
              conv_b, rg_w_a, rg_b_a, rg_w_x, rg_b_x, rg_lam, g_out_fourier, g_out_rg, w_out,
              w_router, b_router, w_gate_up, b_gate_up, w_down, b_down, g_final):
    y = ctx
    for i in range(DEPTH):
        last = i == DEPTH - 1
        sh_m, sc_m, gt_m, sh_f, sc_f, gt_f = adaln_params(c, w_mod[i], b_mod[i])
        csh_m, csc_m, cgt_m, csh_f, csc_f, cgt_f = adaln_params(c_ctx, w_mod[i], b_mod[i])
        h_lat = modulate(rmsnorm(x, g_norm_mix[i]), sh_m, sc_m)
        h_ctx = modulate(rmsnorm(y, g_norm_mix[i]), csh_m, csc_m)
        mix_lat, mix_ctx = token_mixer(h_lat, h_ctx, w_in[i], w_fourier[i], conv_w[i], conv_b[i],
                                       rg_w_a[i], rg_b_a[i], rg_w_x[i], rg_b_x[i], rg_lam[i],
                                       g_out_fourier[i], g_out_rg[i], w_out[i], not last)
        x = x + gt_m * mix_lat
        h = modulate(rmsnorm(x, g_norm_ffn[i]), sh_f, sc_f)
        x = x + gt_f * moe_ffn(h, w_router[i], b_router[i], w_gate_up[i], b_gate_up[i],
                               w_down[i], b_down[i])
        if not last:
            y = y + cgt_m * mix_ctx
            hc = modulate(rmsnorm(y, g_norm_ffn[i]), csh_f, csc_f)
            y = y + cgt_f * moe_ffn(hc, w_router[i], b_router[i], w_gate_up[i], b_gate_up[i],
                                    w_down[i], b_down[i])
    return rmsnorm(x, g_final)
```

```python
import functools

import numpy as np
import jax
import jax.numpy as jnp
from jax import lax
from jax.experimental import pallas as pl
from jax.experimental.pallas import tpu as pltpu

GRID_W = 64
FOURIER_GROUPS = 4
RG_HEADS = 8
CONV_W = 4
CONV_PAD_LO = 2
RG_C = 8.0
N_EXPERTS = 32
TOP_K = 4
SWIGLU_LIMIT = 7.0
SWIGLU_ALPHA = 1.702
N_MOD = 6
EPS = 1e-6

LANES = 128
SUBLANES = 8
VMEM_LIMIT_BYTES = 56 * 1024 * 1024

F32 = jnp.float32
BF16 = jnp.bfloat16


def _cparams(sem):
    return pltpu.CompilerParams(dimension_semantics=sem, vmem_limit_bytes=VMEM_LIMIT_BYTES)


def _split_bf16(a):
    hi = a.astype(BF16)
    lo = (a - hi.astype(F32)).astype(BF16)
    return hi, lo


def _dot3(a, b):
    ah, al = _split_bf16(a)
    bh, bl = _split_bf16(b)
    out = jnp.dot(ah, bh, preferred_element_type=F32)
    out += jnp.dot(ah, bl, preferred_element_type=F32)
    out += jnp.dot(al, bh, preferred_element_type=F32)
    return out


def _dot3_nt(a, b):
    dn = (((1,), (1,)), ((), ()))
    ah, al = _split_bf16(a)
    bh, bl = _split_bf16(b)
    out = lax.dot_general(ah, bh, dn, preferred_element_type=F32)
    out += lax.dot_general(ah, bl, dn, preferred_element_type=F32)
    out += lax.dot_general(al, bh, dn, preferred_element_type=F32)
    return out


def _gelu_tanh(x):
    return 0.5 * x * (1.0 + jnp.tanh(0.7978845608028654 * (x + 0.044715 * (x * x * x))))


def _rms(x, g):
    return x * lax.rsqrt(jnp.mean(x * x, axis=-1, keepdims=True) + EPS) * g


def _adaln_kernel(c_ref, w_ref, b_ref, o_ref):
    s = c_ref[...]
    s = s * jax.nn.sigmoid(s)
    o_ref[...] = _dot3(s, w_ref[...]) + b_ref[...]


def _adaln(cond, w_mod, b_mod):
    m, d = cond.shape
    n = w_mod.shape[1]
    tn = n // N_MOD
    return pl.pallas_call(
        _adaln_kernel,
        grid=(n // tn,),
        in_specs=[pl.BlockSpec((m, d), lambda i: (0, 0)),
                  pl.BlockSpec((d, tn), lambda i: (0, i)),
                  pl.BlockSpec((1, tn), lambda i: (0, i))],
        out_specs=pl.BlockSpec((m, tn), lambda i: (0, i)),
        out_shape=jax.ShapeDtypeStruct((m, n), F32),
        compiler_params=_cparams(("arbitrary",)),
        name="adaln",
    )(cond, w_mod, b_mod.reshape(1, n))


def _fold_kernel(c_ref, s_ref, w_ref, cw_ref, sw_ref):
    w = w_ref[0]
    cw_ref[0] = _dot3(c_ref[...], w)
    sw_ref[0] = _dot3(s_ref[...], w)


def _fold_fourier(cmat, smat, w_f):
    g, gd, _ = w_f.shape
    spec_m = pl.BlockSpec((gd, gd), lambda i: (0, 0))
    spec_w = pl.BlockSpec((1, gd, gd), lambda i: (i, 0, 0))
    return pl.pallas_call(
        _fold_kernel,
        grid=(g,),
        in_specs=[spec_m, spec_m, spec_w],
        out_specs=[spec_w, spec_w],
        out_shape=[jax.ShapeDtypeStruct((g, gd, gd), F32)] * 2,
        compiler_params=_cparams(("arbitrary",)),
        name="fold_fourier",
    )(cmat, smat, w_f)


def _stage_b_kernel(x_ref, sh_ref, sc_ref, g_ref, w_ref, j_ref, f_ref, xs_ref, gg_ref, *, df, dr):
    h = _rms(x_ref[0], g_ref[...]) * (1.0 + sc_ref[0]) + sh_ref[0]
    hb = h.astype(BF16)
    f_ref[0] = jnp.dot(hb, w_ref[:, :df], preferred_element_type=F32).astype(BF16)
    xr = jnp.dot(hb, w_ref[:, df:df + dr], preferred_element_type=F32).astype(BF16)
    tm = xr.shape[0]
    for r in range(tm // GRID_W):
        blk = xr[r * GRID_W:(r + 1) * GRID_W]
        if r % 2 == 1:
            blk = jnp.dot(j_ref[...], blk, preferred_element_type=F32).astype(BF16)
        xs_ref[0, r * GRID_W:(r + 1) * GRID_W, :] = blk
    gr = jnp.dot(hb, w_ref[:, df + dr:], preferred_element_type=F32)
    gg_ref[0] = _gelu_tanh(gr).astype(BF16)


def _stage_b(x, shift, scale, g, w_in_bf, jmat, df, dr, tm):
    b, s, d = x.shape
    n = w_in_bf.shape[1]
    vec = pl.BlockSpec((1, 1, d), lambda i, t: (i, 0, 0))
    out = pl.BlockSpec((1, tm, df), lambda i, t: (i, t, 0))
    return pl.pallas_call(
        functools.partial(_stage_b_kernel, df=df, dr=dr),
        grid=(b, s // tm),
        in_specs=[pl.BlockSpec((1, tm, d), lambda i, t: (i, t, 0)), vec, vec,
                  pl.BlockSpec((1, d), lambda i, t: (0, 0)),
                  pl.BlockSpec((d, n), lambda i, t: (0, 0)),
                  pl.BlockSpec((GRID_W, GRID_W), lambda i, t: (0, 0))],
        out_specs=[out, out, out],
        out_shape=[jax.ShapeDtypeStruct((b, s, df), BF16)] * 3,
        compiler_params=_cparams(("parallel", "arbitrary")),
        name="stage_b",
    )(x, shift, scale, g.reshape(1, d), w_in_bf, jmat)


def _stage_b_ctx_kernel(x_ref, sh_ref, sc_ref, g_ref, w_ref, xr_ref):
    h = _rms(x_ref[0], g_ref[...]) * (1.0 + sc_ref[0]) + sh_ref[0]
    xr_ref[0] = jnp.dot(h.astype(BF16), w_ref[...], preferred_element_type=F32).astype(BF16)


def _stage_b_ctx(ctx, shift, scale, g, w_xr_bf):
    b, s, d = ctx.shape
    dr = w_xr_bf.shape[1]
    vec = pl.BlockSpec((1, 1, d), lambda i: (0, 0, 0))
    return pl.pallas_call(
        _stage_b_ctx_kernel,
        grid=(b,),
        in_specs=[pl.BlockSpec((1, s, d), lambda i: (i, 0, 0)), vec, vec,
                  pl.BlockSpec((1, d), lambda i: (0, 0)),
                  pl.BlockSpec((d, dr), lambda i: (0, 0))],
        out_specs=pl.BlockSpec((1, s, dr), lambda i: (i, 0, 0)),
        out_shape=jax.ShapeDtypeStruct((b, s, dr), BF16),
        compiler_params=_cparams(("arbitrary",)),
        name="stage_b_ctx",
    )(ctx, shift, scale, g.reshape(1, d), w_xr_bf)


HALO = 16


def _rg_kernel(xs_ref, h0_ref, cw_ref, cb_ref, w_ref, b_ref, lam_ref, out_ref, hfin_ref,
               hf_s, a_s, u_s, hc_s, *, tc, nchunk, seq, dr):
    p = pl.program_id(1)
    j = pl.program_id(2)
    cidx = jnp.where(p == 0, j, nchunk - 1 - j)
    start = pl.multiple_of(cidx * tc, tc)

    cur = xs_ref[0, pl.ds(start, tc), :].astype(F32)
    pstart = pl.multiple_of(jnp.maximum(start - HALO, 0), HALO)
    nstart = pl.multiple_of(jnp.minimum(start + tc, seq - HALO), HALO)
    prev = xs_ref[0, pl.ds(pstart, HALO), :].astype(F32)
    nxt = xs_ref[0, pl.ds(nstart, HALO), :].astype(F32)
    prev = jnp.where(cidx > 0, prev, 0.0)
    nxt = jnp.where(cidx < nchunk - 1, nxt, 0.0)
    ext = jnp.concatenate([prev, cur, nxt], axis=0)
    xc = cb_ref[...] + cw_ref[0:1, :] * ext[HALO - CONV_PAD_LO:HALO - CONV_PAD_LO + tc]
    for k in range(1, CONV_W):
        o = HALO - CONV_PAD_LO + k
        xc = xc + cw_ref[k:k + 1, :] * ext[o:o + tc]

    gates = jnp.dot(xc.astype(BF16), w_ref[p], preferred_element_type=F32) + b_ref[p]
    r = jax.nn.sigmoid(gates[:, :dr])
    i = jax.nn.sigmoid(gates[:, dr:])
    log_a = (-RG_C) * r * jax.nn.softplus(-lam_ref[p])
    a = jnp.exp(log_a)
    a_s[...] = a
    u_s[...] = jnp.sqrt(-jnp.tanh(log_a) * (1.0 + a * a)) * (i * xc)

    @pl.when(jnp.logical_and(p == 0, j == 0))
    def _():
        hfin_ref[...] = jnp.zeros_like(hfin_ref)

    @pl.when(j == 0)
    def _():
        hc_s[0:1, :] = h0_ref[0, pl.ds(p, 1), :]

    @pl.when(p == 0)
    def _():
        def body(t, h):
            h = a_s[pl.ds(t, 1), :] * h + u_s[pl.ds(t, 1), :]
            hf_s[pl.ds(start + t, 1), :] = h
            return h
        hc_s[0:1, :] = lax.fori_loop(0, tc, body, hc_s[0:1, :], unroll=8)

    @pl.when(p == 1)
    def _():
        def body(q, h):
            t = tc - 1 - q
            h = a_s[pl.ds(t, 1), :] * h + u_s[pl.ds(t, 1), :]
            hf_s[pl.ds(start + t, 1), :] = hf_s[pl.ds(start + t, 1), :] + h
            return h
        hc_s[0:1, :] = lax.fori_loop(0, tc, body, hc_s[0:1, :], unroll=8)
        out_ref[0] = hf_s[pl.ds(start, tc), :].astype(BF16)

    @pl.when(j == nchunk - 1)
    def _():
        hfin_ref[0, pl.ds(p, 1), :] = hc_s[0:1, :]


def _rg_scan(xs, h0, conv_w, conv_b, wcat, bcat, lam, tc):
    b, s, dr = xs.shape
    nchunk = s // tc
    last = nchunk - 1
    full2 = lambda shape: pl.BlockSpec(shape, lambda i, p, j: (0,) * len(shape))
    return pl.pallas_call(
        functools.partial(_rg_kernel, tc=tc, nchunk=nchunk, seq=s, dr=dr),
        grid=(b, 2, nchunk),
        in_specs=[pl.BlockSpec((1, s, dr), lambda i, p, j: (i, 0, 0)),
                  pl.BlockSpec((1, SUBLANES, dr), lambda i, p, j: (i, 0, 0)),
                  full2((CONV_W, dr)), full2((1, dr)),
                  full2((2, dr, 2 * dr)), full2((2, 1, 2 * dr)), full2((2, 1, dr))],
        out_specs=[pl.BlockSpec((1, tc, dr), lambda i, p, j: (i, jnp.where(p == 0, last, last - j), 0)),
                   pl.BlockSpec((1, SUBLANES, dr), lambda i, p, j: (i, 0, 0))],
        out_shape=[jax.ShapeDtypeStruct((b, s, dr), BF16),
                   jax.ShapeDtypeStruct((b, SUBLANES, dr), F32)],
        scratch_shapes=[pltpu.VMEM((s, dr), F32), pltpu.VMEM((tc, dr), F32),
                        pltpu.VMEM((tc, dr), F32), pltpu.VMEM((SUBLANES, dr), F32)],
        compiler_params=_cparams(("arbitrary", "arbitrary", "arbitrary")),
        name="rg_scan",
    )(xs, h0, conv_w, conv_b.reshape(1, dr), wcat, bcat, lam)


def _f1_kernel(d_ref, x_ref, y_ref):
    y_ref[0] = jnp.dot(d_ref[...], x_ref[0], preferred_element_type=F32).astype(BF16)


def _fourier_stage1(fv, d2, tl):
    b, r, n = fv.shape
    return pl.pallas_call(
        _f1_kernel,
        grid=(b, n // tl),
        in_specs=[pl.BlockSpec((2 * r, r), lambda i, l: (0, 0)),
                  pl.BlockSpec((1, r, tl), lambda i, l: (i, 0, l))],
        out_specs=pl.BlockSpec((1, 2 * r, tl), lambda i, l: (i, 0, l)),
        out_shape=jax.ShapeDtypeStruct((b, 2 * r, n), BF16),
        compiler_params=_cparams(("parallel", "arbitrary")),
        name="fourier_stage1",
    )(d2, fv)


def _f2_kernel(y_ref, e_ref, bc_ref, bs_ref, g_ref, o_ref, *, kb, df):
    zr, zi = [], []
    for q in range(kb):
        yk = jnp.concatenate([y_ref[0, 0, q], y_ref[0, 1, q]], axis=0)
        z = jnp.dot(e_ref[q], yk, preferred_element_type=F32)
        zr.append(z[:GRID_W])
        zi.append(z[GRID_W:])
    zr = jnp.concatenate(zr, axis=0).astype(BF16)
    zi = jnp.concatenate(zi, axis=0).astype(BF16)
    o = jnp.dot(zr, bc_ref[...], preferred_element_type=F32)
    o += jnp.dot(zi, bs_ref[...], preferred_element_type=F32)
    on = _rms(o, g_ref[...]).astype(BF16)
    for q in range(kb):
        o_ref[0, :, q * df:(q + 1) * df] = on[q * GRID_W:(q + 1) * GRID_W]


def _fourier_stage2(y5, etab, bdc, bds, g, kb):
    b, _, r, w, df = y5.shape
    return pl.pallas_call(
        functools.partial(_f2_kernel, kb=kb, df=df),
        grid=(b, r // kb),
        in_specs=[pl.BlockSpec((1, 2, kb, w, df), lambda i, k: (i, 0, k, 0, 0)),
                  pl.BlockSpec((kb, 2 * w, 2 * w), lambda i, k: (k, 0, 0)),
                  pl.BlockSpec((df, df), lambda i, k: (0, 0)),
                  pl.BlockSpec((df, df), lambda i, k: (0, 0)),
                  pl.BlockSpec((1, df), lambda i, k: (0, 0))],
        out_specs=pl.BlockSpec((1, w, kb * df), lambda i, k: (i, 0, k)),
        out_shape=jax.ShapeDtypeStruct((b, w, r * df), BF16),
        compiler_params=_cparams(("parallel", "arbitrary")),
        name="fourier_stage2",
    )(y5, etab, bdc, bds, g.reshape(1, df))


def _stage_m_kernel(fn_ref, hs_ref, gg_ref, x_ref, gtm_ref, shf_ref, scf_ref, gr_ref, gffn_ref,
                    wo_ref, wr_ref, br_ref, j_ref, x1_ref, h2_ref, idx_ref, gate_ref, *, df):
    tm = x_ref.shape[1]
    hs = hs_ref[0]
    blocks = []
    for r in range(tm // GRID_W):
        blk = hs[r * GRID_W:(r + 1) * GRID_W]
        if r % 2 == 1:
            blk = jnp.dot(j_ref[...], blk, preferred_element_type=F32)
        blocks.append(blk.astype(F32))
    rg = jnp.concatenate(blocks, axis=0) * gg_ref[0].astype(F32)
    rgn = _rms(rg, gr_ref[...]).astype(BF16)
    mix = jnp.dot(fn_ref[0], wo_ref[:df, :], preferred_element_type=F32)
    mix += jnp.dot(rgn, wo_ref[df:, :], preferred_element_type=F32)
    x1 = x_ref[0] + gtm_ref[0] * mix
    x1_ref[0] = x1
    h2 = _rms(x1, gffn_ref[...]) * (1.0 + scf_ref[0]) + shf_ref[0]
    h2_ref[0] = h2.astype(BF16)

    logits = _dot3_nt(wr_ref[...], h2) + br_ref[...]
    eidx = lax.broadcasted_iota(jnp.int32, logits.shape, 0)
    vals, idxs = [], []
    for _ in range(TOP_K):
        m = jnp.max(logits, axis=0, keepdims=True)
        sel = jnp.min(jnp.where(logits == m, eidx, N_EXPERTS), axis=0, keepdims=True)
        vals.append(m)
        idxs.append(sel)
        logits = jnp.where(eidx == sel, -jnp.inf, logits)
    ex = [jnp.exp(v - vals[0]) for v in vals]
    den = ex[0] + ex[1] + ex[2] + ex[3]
    for k in range(TOP_K):
        gate_ref[k:k + 1, :] = ex[k] / den
        idx_ref[k:k + 1, :] = idxs[k]


def _stage_m(fn, hs, gg, x, gt_m, sh_f, sc_f, g_out_r, g_ffn, w_out_bf, w_router_t, b_router, jmat, tm):
    b, s, d = x.shape
    df = fn.shape[2]
    dr = hs.shape[2]
    nt = s // tm
    ne = w_router_t.shape[0]
    vec = pl.BlockSpec((1, 1, d), lambda i, t: (i, 0, 0))
    half = lambda dd: pl.BlockSpec((1, tm, dd), lambda i, t: (i, t, 0))
    full = lambda shape: pl.BlockSpec(shape, lambda i, t: (0,) * len(shape))
    tok = pl.BlockSpec((TOP_K, tm), lambda i, t: (0, i * nt + t))
    return pl.pallas_call(
        functools.partial(_stage_m_kernel, df=df),
        grid=(b, nt),
        in_specs=[half(df), half(dr), half(dr), half(d), vec, vec, vec,
                  full((1, dr)), full((1, d)), full((d, d)), full((ne, d)), full((ne, 1)),
                  full((GRID_W, GRID_W))],
        out_specs=[half(d), half(d), tok, tok],
        out_shape=[jax.ShapeDtypeStruct((b, s, d), F32), jax.ShapeDtypeStruct((b, s, d), BF16),
                   jax.ShapeDtypeStruct((TOP_K, b * s), jnp.int32),
                   jax.ShapeDtypeStruct((TOP_K, b * s), F32)],
        compiler_params=_cparams(("parallel", "arbitrary")),
        name="stage_m",
    )(fn, hs, gg, x, gt_m, sh_f, sc_f, g_out_r.reshape(1, dr), g_ffn.reshape(1, d), w_out_bf,
      w_router_t, b_router.reshape(ne, 1), jmat)


def _rank_kernel(idx_ref, tri_ref, rank_ref, cnt_ref, carry_s):
    c = pl.program_id(0)

    @pl.when(c == 0)
    def _():
        carry_s[...] = jnp.zeros_like(carry_s)

    l = idx_ref.shape[1]
    eidx = lax.broadcasted_iota(jnp.int32, (N_EXPERTS, l), 0)
    for k in range(TOP_K):
        onehot = eidx == idx_ref[k:k + 1, :]
        oh = jnp.where(onehot, 1.0, 0.0)
        prefix = jnp.dot(oh.astype(BF16), tri_ref[...], preferred_element_type=F32)
        carry = carry_s[:, 0:1]
        rank = jnp.sum(jnp.where(onehot, prefix - 1.0 + carry, 0.0), axis=0, keepdims=True)
        rank_ref[k:k + 1, :] = rank.astype(jnp.int32)
        carry_s[...] = carry_s[...] + jnp.sum(oh, axis=1, keepdims=True)
    cnt_ref[...] = carry_s[...].astype(jnp.int32)


def _dispatch_ranks(idx, tri, tl):
    k, t = idx.shape
    return pl.pallas_call(
        _rank_kernel,
        grid=(t // tl,),
        in_specs=[pl.BlockSpec((k, tl), lambda c: (0, c)),
                  pl.BlockSpec((tl, tl), lambda c: (0, 0))],
        out_specs=[pl.BlockSpec((k, tl), lambda c: (0, c)),
                   pl.BlockSpec((N_EXPERTS, LANES), lambda c: (0, 0))],
        out_shape=[jax.ShapeDtypeStruct((k, t), jnp.int32),
                   jax.ShapeDtypeStruct((N_EXPERTS, LANES), jnp.int32)],
        scratch_shapes=[pltpu.VMEM((N_EXPERTS, LANES), F32)],
        compiler_params=_cparams(("arbitrary",)),
        name="dispatch_ranks",
    )(idx, tri)


def _moe_kernel(be_ref, nu_ref, x_ref, gate_ref, wgu_ref, bgu_ref, wd_ref, bd_ref, o_ref, *, dff):
    i = pl.program_id(0)

    @pl.when(i < nu_ref[0])
    def _():
        gu = jnp.dot(x_ref[...], wgu_ref[0], preferred_element_type=F32) + bgu_ref[0]
        gt = jnp.minimum(gu[:, :dff], SWIGLU_LIMIT)
        up = jnp.clip(gu[:, dff:], -SWIGLU_LIMIT, SWIGLU_LIMIT)
        act = (up + 1.0) * (gt * jax.nn.sigmoid(SWIGLU_ALPHA * gt))
        out = jnp.dot(act.astype(BF16), wd_ref[0], preferred_element_type=F32) + bd_ref[0]
        o_ref[...] = (out * gate_ref[...]).astype(BF16)

    @pl.when(i >= nu_ref[0])
    def _():
        o_ref[...] = jnp.zeros_like(o_ref)


def _moe_experts(blk_expert, n_used, xs, row_gate, wgu_bf, bgu, wd_bf, bd, tmm):
    cap, d = xs.shape
    ne, _, dff2 = wgu_bf.shape
    dff = dff2 // 2
    grid_spec = pltpu.PrefetchScalarGridSpec(
        num_scalar_prefetch=2,
        grid=(cap // tmm,),
        in_specs=[pl.BlockSpec((tmm, d), lambda i, be, nu: (i, 0)),
                  pl.BlockSpec((tmm, 1), lambda i, be, nu: (i, 0)),
                  pl.BlockSpec((1, d, dff2), lambda i, be, nu: (be[i], 0, 0)),
                  pl.BlockSpec((1, 1, dff2), lambda i, be, nu: (be[i], 0, 0)),
                  pl.BlockSpec((1, dff, d), lambda i, be, nu: (be[i], 0, 0)),
                  pl.BlockSpec((1, 1, d), lambda i, be, nu: (be[i], 0, 0))],
        out_specs=pl.BlockSpec((tmm, d), lambda i, be, nu: (i, 0)),
    )
    return pl.pallas_call(
        functools.partial(_moe_kernel, dff=dff),
        grid_spec=grid_spec,
        out_shape=jax.ShapeDtypeStruct((cap, d), BF16),
        compiler_params=_cparams(("arbitrary",)),
        name="moe_experts",
    )(blk_expert, n_used, xs, row_gate, wgu_bf, bgu.reshape(ne, 1, dff2), wd_bf, bd.reshape(ne, 1, d))


def _combine_kernel(x1_ref, y_ref, gtf_ref, g_ref, o_ref):
    moe = y_ref[0, 0].astype(F32)
    for k in range(1, TOP_K):
        moe = moe + y_ref[k, 0].astype(F32)
    o_ref[0] = _rms(x1_ref[0] + gtf_ref[0] * moe, g_ref[...])


def _combine(x1, yk, gt_f, g_final, tm):
    b, s, d = x1.shape
    return pl.pallas_call(
        _combine_kernel,
        grid=(b, s // tm),
        in_specs=[pl.BlockSpec((1, tm, d), lambda i, t: (i, t, 0)),
                  pl.BlockSpec((TOP_K, 1, tm, d), lambda i, t: (0, i, t, 0)),
                  pl.BlockSpec((1, 1, d), lambda i, t: (i, 0, 0)),
                  pl.BlockSpec((1, d), lambda i, t: (0, 0))],
        out_specs=pl.BlockSpec((1, tm, d), lambda i, t: (i, t, 0)),
        out_shape=jax.ShapeDtypeStruct((b, s, d), F32),
        compiler_params=_cparams(("parallel", "arbitrary")),
        name="combine",
    )(x1, yk, gt_f, g_final.reshape(1, d))


def _dft_tables(rows, gd):
    seq = rows * GRID_W
    n = np.arange(rows)
    ang1 = 2.0 * np.pi * np.outer(n, n) / rows
    d2 = np.concatenate([np.cos(ang1), -np.sin(ang1)], axis=0)
    k1 = np.arange(rows)[:, None, None]
    k2 = np.arange(GRID_W)[None, :, None]
    n2 = np.arange(GRID_W)[None, None, :]
    ang2 = 2.0 * np.pi * ((n2 * (k1 + rows * k2)) % seq) / seq
    ec, es = np.cos(ang2), np.sin(ang2)
    etab = np.concatenate([np.concatenate([ec, es], axis=2),
                           np.concatenate([-es, ec], axis=2)], axis=1)
    c = np.arange(gd)
    angc = 2.0 * np.pi * np.outer(c, c) / gd
    scale = 1.0 / np.sqrt(seq * gd)
    return (jnp.asarray(d2, BF16), jnp.asarray(etab, BF16),
            jnp.asarray(np.cos(angc) * scale, F32), jnp.asarray(np.sin(angc) * scale, F32))


def _block_diag(w):
    h, i, o = w.shape
    eye = jnp.eye(h, dtype=w.dtype)
    return (eye[:, None, :, None] * w[:, :, None, :]).reshape(h * i, h * o)


def kernel(x, c, ctx, c_ctx, w_mod, b_mod, g_norm_mix, g_norm_ffn, w_in, w_fourier, conv_w, conv_b,
           rg_w_a, rg_b_a, rg_w_x, rg_b_x, rg_lam, g_out_fourier, g_out_rg, w_out, w_router, b_router,
           w_gate_up, b_gate_up, w_down, b_down, g_final):
    assert w_mod.shape[0] == 1, "single-layer stack only"
    b, s, d = x.shape
    df = w_fourier.shape[1] * w_fourier.shape[2]
    dr = conv_w.shape[2]
    gd = w_fourier.shape[2]
    rows = s // GRID_W
    t = b * s
    ne = w_router.shape[2]

    mrows = -(-(b + 1) // SUBLANES) * SUBLANES
    cond = jnp.zeros((mrows, d), F32).at[:b].set(c).at[b].set(c_ctx)
    mod = _adaln(cond, w_mod[0], b_mod[0])
    sh_m, sc_m, gt_m, sh_f, sc_f, gt_f = [mod[:b, k * d:(k + 1) * d].reshape(b, 1, d) for k in range(N_MOD)]
    csh_m = mod[b:b + 1, 0:d].reshape(1, 1, d)
    csc_m = mod[b:b + 1, d:2 * d].reshape(1, 1, d)

    d2, etab, cmat, smat = _dft_tables(rows, gd)
    jmat = jnp.asarray(np.eye(GRID_W)[::-1].copy(), BF16)

    w_in_bf = w_in[0].astype(BF16)
    f, xs, gg = _stage_b(x, sh_m, sc_m, g_norm_mix[0], w_in_bf, jmat, df, dr, tm=512)
    xr_ctx = _stage_b_ctx(ctx, csh_m, csc_m, g_norm_mix[0], w_in_bf[:, df:df + dr])

    wcat = jnp.stack([jnp.concatenate([_block_diag(rg_w_a[0, dd]), _block_diag(rg_w_x[0, dd])], axis=1)
                      for dd in range(2)]).astype(BF16)
    bcat = jnp.concatenate([rg_b_a[0], rg_b_x[0]], axis=1).reshape(2, 1, 2 * dr)
    lam = rg_lam[0].reshape(2, 1, dr)
    h0 = jnp.zeros((b, SUBLANES, dr), F32)
    _, hfin_ctx = _rg_scan(xr_ctx, h0, conv_w[0], conv_b[0], wcat, bcat, lam, tc=ctx.shape[1])
    hs, _ = _rg_scan(xs, hfin_ctx, conv_w[0], conv_b[0], wcat, bcat, lam, tc=512)

    cw, sw = _fold_fourier(cmat, smat, w_fourier[0])
    bdc = _block_diag(cw).astype(BF16)
    bds = _block_diag(sw).astype(BF16)
    y = _fourier_stage1(f.reshape(b, rows, GRID_W * df), d2, tl=4096)
    fn = _fourier_stage2(y.reshape(b, 2, rows, GRID_W, df), etab, bdc, bds, g_out_fourier[0],
                         kb=min(rows, 8))
    fn = fn.reshape(b, s, df)

    x1, h2, idx, gates = _stage_m(fn, hs, gg, x, gt_m, sh_f, sc_f, g_out_rg[0], g_norm_ffn[0],
                                  w_out[0].astype(BF16), w_router[0].T, b_router[0], jmat, tm=512)

    tl = 512
    tri = jnp.asarray(np.triu(np.ones((tl, tl))), BF16)
    rank, cnt = _dispatch_ranks(idx, tri, tl)
    counts = cnt[:, 0]
    tmm = 256
    padded = (counts + tmm - 1) // tmm * tmm
    pad_end = jnp.cumsum(padded)
    pad_start = pad_end - padded
    dest = pad_start[idx] + rank
    n_blocks = -(-(t * TOP_K) // tmm) + ne
    cap = n_blocks * tmm
    n_used = (pad_end[-1] // tmm).astype(jnp.int32).reshape(1)
    blk_start = jnp.arange(n_blocks, dtype=jnp.int32) * tmm
    blk_expert = jnp.minimum(jnp.searchsorted(pad_end, blk_start, side='right'), ne - 1).astype(jnp.int32)
    blk_expert = jnp.where(blk_start < pad_end[-1], blk_expert, blk_expert[jnp.maximum(n_used[0] - 1, 0)])

    tok = jnp.broadcast_to(jnp.arange(t, dtype=jnp.int32)[None, :], (TOP_K, t))
    row_tok = jnp.zeros((cap,), jnp.int32).at[dest.reshape(-1)].set(tok.reshape(-1))
    row_gate = jnp.zeros((cap,), F32).at[dest.reshape(-1)].set(gates.reshape(-1))
    x_sorted = h2.reshape(t, d)[row_tok]

    y_sorted = _moe_experts(blk_expert, n_used, x_sorted, row_gate.reshape(cap, 1),
                            w_gate_up[0].astype(BF16), b_gate_up[0], w_down[0].astype(BF16), b_down[0], tmm)
    yk = y_sorted[dest].reshape(TOP_K, b, s, d)
    return _combine(x1, yk, gt_f, g_final, tm=512)
```

```python
import functools

import numpy as np
import jax
import jax.numpy as jnp
from jax import lax
from jax.experimental import pallas as pl
from jax.experimental.pallas import tpu as pltpu
from jax.experimental.pallas import tpu_sc as plsc

GRID_W = 64
FOURIER_GROUPS = 4
RG_HEADS = 8
CONV_W = 4
CONV_PAD_LO = 2
RG_C = 8.0
N_EXPERTS = 32
TOP_K = 4
SWIGLU_LIMIT = 7.0
SWIGLU_ALPHA = 1.702
N_MOD = 6
EPS = 1e-6

LANES = 128
SUBLANES = 8
VMEM_LIMIT_BYTES = 56 * 1024 * 1024

F32 = jnp.float32
BF16 = jnp.bfloat16


def _cparams(sem):
    return pltpu.CompilerParams(dimension_semantics=sem, vmem_limit_bytes=VMEM_LIMIT_BYTES)


def _split_bf16(a):
    hi = a.astype(BF16)
    lo = (a - hi.astype(F32)).astype(BF16)
    return hi, lo


def _dot3(a, b):
    ah, al = _split_bf16(a)
    bh, bl = _split_bf16(b)
    out = jnp.dot(ah, bh, preferred_element_type=F32)
    out += jnp.dot(ah, bl, preferred_element_type=F32)
    out += jnp.dot(al, bh, preferred_element_type=F32)
    return out


def _dot3_nt(a, b):
    dn = (((1,), (1,)), ((), ()))
    ah, al = _split_bf16(a)
    bh, bl = _split_bf16(b)
    out = lax.dot_general(ah, bh, dn, preferred_element_type=F32)
    out += lax.dot_general(ah, bl, dn, preferred_element_type=F32)
    out += lax.dot_general(al, bh, dn, preferred_element_type=F32)
    return out


def _gelu_tanh(x):
    return 0.5 * x * (1.0 + jnp.tanh(0.7978845608028654 * (x + 0.044715 * (x * x * x))))


def _rms(x, g):
    return x * lax.rsqrt(jnp.mean(x * x, axis=-1, keepdims=True) + EPS) * g


def _pack_halves(v):
    h = v.shape[1] // 2
    hi = lax.bitcast_convert_type(v[:, :h].astype(BF16).astype(F32), jnp.uint32)
    lo = lax.bitcast_convert_type(v[:, h:].astype(BF16).astype(F32), jnp.uint32)
    return lax.bitcast_convert_type(hi | (lo >> 16), jnp.int32)


def _unpack_halves(w):
    u = lax.bitcast_convert_type(w, jnp.uint32)
    hi = lax.bitcast_convert_type(u & jnp.uint32(0xFFFF0000), F32)
    lo = lax.bitcast_convert_type(u << 16, F32)
    return hi, lo


def _adaln_kernel(c_ref, w_ref, b_ref, o_ref):
    s = c_ref[...]
    s = s * jax.nn.sigmoid(s)
    o_ref[...] = _dot3(s, w_ref[...]) + b_ref[...]


def _adaln(cond, w_mod, b_mod):
    m, d = cond.shape
    n = w_mod.shape[1]
    tn = n // N_MOD
    return pl.pallas_call(
        _adaln_kernel,
        grid=(n // tn,),
        in_specs=[pl.BlockSpec((m, d), lambda i: (0, 0)),
                  pl.BlockSpec((d, tn), lambda i: (0, i)),
                  pl.BlockSpec((1, tn), lambda i: (0, i))],
        out_specs=pl.BlockSpec((m, tn), lambda i: (0, i)),
        out_shape=jax.ShapeDtypeStruct((m, n), F32),
        compiler_params=_cparams(("arbitrary",)),
        name="adaln",
    )(cond, w_mod, b_mod.reshape(1, n))


def _fold_kernel(c_ref, s_ref, w_ref, cw_ref, sw_ref):
    w = w_ref[0]
    cw_ref[0] = _dot3(c_ref[...], w)
    sw_ref[0] = _dot3(s_ref[...], w)


def _fold_fourier(cmat, smat, w_f):
    g, gd, _ = w_f.shape
    spec_m = pl.BlockSpec((gd, gd), lambda i: (0, 0))
    spec_w = pl.BlockSpec((1, gd, gd), lambda i: (i, 0, 0))
    return pl.pallas_call(
        _fold_kernel,
        grid=(g,),
        in_specs=[spec_m, spec_m, spec_w],
        out_specs=[spec_w, spec_w],
        out_shape=[jax.ShapeDtypeStruct((g, gd, gd), F32)] * 2,
        compiler_params=_cparams(("arbitrary",)),
        name="fold_fourier",
    )(cmat, smat, w_f)


def _stage_b_kernel(x_ref, sh_ref, sc_ref, g_ref, w_ref, j_ref, f_ref, xs_ref, gg_ref, *, df, dr):
    h = _rms(x_ref[0], g_ref[...]) * (1.0 + sc_ref[0]) + sh_ref[0]
    hb = h.astype(BF16)
    f_ref[0] = jnp.dot(hb, w_ref[:, :df], preferred_element_type=F32).astype(BF16)
    xr = jnp.dot(hb, w_ref[:, df:df + dr], preferred_element_type=F32).astype(BF16)
    tm = xr.shape[0]
    for r in range(tm // GRID_W):
        blk = xr[r * GRID_W:(r + 1) * GRID_W]
        if r % 2 == 1:
            blk = jnp.dot(j_ref[...], blk, preferred_element_type=F32).astype(BF16)
        xs_ref[0, r * GRID_W:(r + 1) * GRID_W, :] = blk
    gr = jnp.dot(hb, w_ref[:, df + dr:], preferred_element_type=F32)
    gg_ref[0] = _gelu_tanh(gr).astype(BF16)


def _stage_b(x, shift, scale, g, w_in_bf, jmat, df, dr, tm):
    b, s, d = x.shape
    n = w_in_bf.shape[1]
    vec = pl.BlockSpec((1, 1, d), lambda i, t: (i, 0, 0))
    out = pl.BlockSpec((1, tm, df), lambda i, t: (i, t, 0))
    return pl.pallas_call(
        functools.partial(_stage_b_kernel, df=df, dr=dr),
        grid=(b, s // tm),
        in_specs=[pl.BlockSpec((1, tm, d), lambda i, t: (i, t, 0)), vec, vec,
                  pl.BlockSpec((1, d), lambda i, t: (0, 0)),
                  pl.BlockSpec((d, n), lambda i, t: (0, 0)),
                  pl.BlockSpec((GRID_W, GRID_W), lambda i, t: (0, 0))],
        out_specs=[out, out, out],
        out_shape=[jax.ShapeDtypeStruct((b, s, df), BF16)] * 3,
        compiler_params=_cparams(("parallel", "arbitrary")),
        name="stage_b",
    )(x, shift, scale, g.reshape(1, d), w_in_bf, jmat)


def _stage_b_ctx_kernel(x_ref, sh_ref, sc_ref, g_ref, w_ref, xr_ref):
    h = _rms(x_ref[0], g_ref[...]) * (1.0 + sc_ref[0]) + sh_ref[0]
    xr_ref[0] = jnp.dot(h.astype(BF16), w_ref[...], preferred_element_type=F32).astype(BF16)


def _stage_b_ctx(ctx, shift, scale, g, w_xr_bf):
    b, s, d = ctx.shape
    dr = w_xr_bf.shape[1]
    vec = pl.BlockSpec((1, 1, d), lambda i: (0, 0, 0))
    return pl.pallas_call(
        _stage_b_ctx_kernel,
        grid=(b,),
        in_specs=[pl.BlockSpec((1, s, d), lambda i: (i, 0, 0)), vec, vec,
                  pl.BlockSpec((1, d), lambda i: (0, 0)),
                  pl.BlockSpec((d, dr), lambda i: (0, 0))],
        out_specs=pl.BlockSpec((1, s, dr), lambda i: (i, 0, 0)),
        out_shape=jax.ShapeDtypeStruct((b, s, dr), BF16),
        compiler_params=_cparams(("arbitrary",)),
        name="stage_b_ctx",
    )(ctx, shift, scale, g.reshape(1, d), w_xr_bf)


HALO = 16


def _rg_kernel(xs_ref, h0_ref, cw_ref, cb_ref, w_ref, b_ref, lam_ref, out_ref, hfin_ref,
               hf_s, a_s, u_s, hc_s, *, tc, nchunk, seq, dr):
    p = pl.program_id(1)
    j = pl.program_id(2)
    cidx = jnp.where(p == 0, j, nchunk - 1 - j)
    start = pl.multiple_of(cidx * tc, tc)

    cur = xs_ref[0, pl.ds(start, tc), :].astype(F32)
    pstart = pl.multiple_of(jnp.maximum(start - HALO, 0), HALO)
    nstart = pl.multiple_of(jnp.minimum(start + tc, seq - HALO), HALO)
    prev = xs_ref[0, pl.ds(pstart, HALO), :].astype(F32)
    nxt = xs_ref[0, pl.ds(nstart, HALO), :].astype(F32)
    prev = jnp.where(cidx > 0, prev, 0.0)
    nxt = jnp.where(cidx < nchunk - 1, nxt, 0.0)
    ext = jnp.concatenate([prev, cur, nxt], axis=0)
    xc = cb_ref[...] + cw_ref[0:1, :] * ext[HALO - CONV_PAD_LO:HALO - CONV_PAD_LO + tc]
    for k in range(1, CONV_W):
        o = HALO - CONV_PAD_LO + k
        xc = xc + cw_ref[k:k + 1, :] * ext[o:o + tc]

    gates = jnp.dot(xc.astype(BF16), w_ref[p], preferred_element_type=F32) + b_ref[p]
    r = jax.nn.sigmoid(gates[:, :dr])
    i = jax.nn.sigmoid(gates[:, dr:])
    log_a = (-RG_C) * r * jax.nn.softplus(-lam_ref[p])
    a = jnp.exp(log_a)
    a_s[...] = a
    u_s[...] = jnp.sqrt(-jnp.tanh(log_a) * (1.0 + a * a)) * (i * xc)

    @pl.when(jnp.logical_and(p == 0, j == 0))
    def _():
        hfin_ref[...] = jnp.zeros_like(hfin_ref)

    @pl.when(j == 0)
    def _():
        hc_s[0:1, :] = h0_ref[0, pl.ds(p, 1), :]

    @pl.when(p == 0)
    def _():
        def body(t, h):
            h = a_s[pl.ds(t, 1), :] * h + u_s[pl.ds(t, 1), :]
            hf_s[pl.ds(start + t, 1), :] = h
            return h
        hc_s[0:1, :] = lax.fori_loop(0, tc, body, hc_s[0:1, :], unroll=8)

    @pl.when(p == 1)
    def _():
        def body(q, h):
            t = tc - 1 - q
            h = a_s[pl.ds(t, 1), :] * h + u_s[pl.ds(t, 1), :]
            hf_s[pl.ds(start + t, 1), :] = hf_s[pl.ds(start + t, 1), :] + h
            return h
        hc_s[0:1, :] = lax.fori_loop(0, tc, body, hc_s[0:1, :], unroll=8)
        out_ref[0] = hf_s[pl.ds(start, tc), :].astype(BF16)

    @pl.when(j == nchunk - 1)
    def _():
        hfin_ref[0, pl.ds(p, 1), :] = hc_s[0:1, :]


def _rg_scan(xs, h0, conv_w, conv_b, wcat, bcat, lam, tc):
    b, s, dr = xs.shape
    nchunk = s // tc
    last = nchunk - 1
    full2 = lambda shape: pl.BlockSpec(shape, lambda i, p, j: (0,) * len(shape))
    return pl.pallas_call(
        functools.partial(_rg_kernel, tc=tc, nchunk=nchunk, seq=s, dr=dr),
        grid=(b, 2, nchunk),
        in_specs=[pl.BlockSpec((1, s, dr), lambda i, p, j: (i, 0, 0)),
                  pl.BlockSpec((1, SUBLANES, dr), lambda i, p, j: (i, 0, 0)),
                  full2((CONV_W, dr)), full2((1, dr)),
                  full2((2, dr, 2 * dr)), full2((2, 1, 2 * dr)), full2((2, 1, dr))],
        out_specs=[pl.BlockSpec((1, tc, dr), lambda i, p, j: (i, jnp.where(p == 0, last, last - j), 0)),
                   pl.BlockSpec((1, SUBLANES, dr), lambda i, p, j: (i, 0, 0))],
        out_shape=[jax.ShapeDtypeStruct((b, s, dr), BF16),
                   jax.ShapeDtypeStruct((b, SUBLANES, dr), F32)],
        scratch_shapes=[pltpu.VMEM((s, dr), F32), pltpu.VMEM((tc, dr), F32),
                        pltpu.VMEM((tc, dr), F32), pltpu.VMEM((SUBLANES, dr), F32)],
        compiler_params=_cparams(("arbitrary", "arbitrary", "arbitrary")),
        name="rg_scan",
    )(xs, h0, conv_w, conv_b.reshape(1, dr), wcat, bcat, lam)


def _f1_kernel(d_ref, x_ref, y_ref):
    y_ref[0] = jnp.dot(d_ref[...], x_ref[0], preferred_element_type=F32).astype(BF16)


def _fourier_stage1(fv, d2, tl):
    b, r, n = fv.shape
    return pl.pallas_call(
        _f1_kernel,
        grid=(b, n // tl),
        in_specs=[pl.BlockSpec((2 * r, r), lambda i, l: (0, 0)),
                  pl.BlockSpec((1, r, tl), lambda i, l: (i, 0, l))],
        out_specs=pl.BlockSpec((1, 2 * r, tl), lambda i, l: (i, 0, l)),
        out_shape=jax.ShapeDtypeStruct((b, 2 * r, n), BF16),
        compiler_params=_cparams(("parallel", "arbitrary")),
        name="fourier_stage1",
    )(d2, fv)


def _f2_kernel(y_ref, e_ref, bc_ref, bs_ref, g_ref, o_ref, *, kb, df):
    zr, zi = [], []
    for q in range(kb):
        yk = jnp.concatenate([y_ref[0, 0, q], y_ref[0, 1, q]], axis=0)
        z = jnp.dot(e_ref[q], yk, preferred_element_type=F32)
        zr.append(z[:GRID_W])
        zi.append(z[GRID_W:])
    zr = jnp.concatenate(zr, axis=0).astype(BF16)
    zi = jnp.concatenate(zi, axis=0).astype(BF16)
    o = jnp.dot(zr, bc_ref[...], preferred_element_type=F32)
    o += jnp.dot(zi, bs_ref[...], preferred_element_type=F32)
    on = _rms(o, g_ref[...]).astype(BF16)
    for q in range(kb):
        o_ref[0, :, q * df:(q + 1) * df] = on[q * GRID_W:(q + 1) * GRID_W]


def _fourier_stage2(y5, etab, bdc, bds, g, kb):
    b, _, r, w, df = y5.shape
    return pl.pallas_call(
        functools.partial(_f2_kernel, kb=kb, df=df),
        grid=(b, r // kb),
        in_specs=[pl.BlockSpec((1, 2, kb, w, df), lambda i, k: (i, 0, k, 0, 0)),
                  pl.BlockSpec((kb, 2 * w, 2 * w), lambda i, k: (k, 0, 0)),
                  pl.BlockSpec((df, df), lambda i, k: (0, 0)),
                  pl.BlockSpec((df, df), lambda i, k: (0, 0)),
                  pl.BlockSpec((1, df), lambda i, k: (0, 0))],
        out_specs=pl.BlockSpec((1, w, kb * df), lambda i, k: (i, 0, k)),
        out_shape=jax.ShapeDtypeStruct((b, w, r * df), BF16),
        compiler_params=_cparams(("parallel", "arbitrary")),
        name="fourier_stage2",
    )(y5, etab, bdc, bds, g.reshape(1, df))


def _stage_m_kernel(fn_ref, hs_ref, gg_ref, x_ref, gtm_ref, shf_ref, scf_ref, gr_ref, gffn_ref,
                    wo_ref, wr_ref, br_ref, j_ref, x1_ref, h2_ref, idx_ref, gate_ref, *, df):
    tm = x_ref.shape[1]
    hs = hs_ref[0]
    blocks = []
    for r in range(tm // GRID_W):
        blk = hs[r * GRID_W:(r + 1) * GRID_W]
        if r % 2 == 1:
            blk = jnp.dot(j_ref[...], blk, preferred_element_type=F32)
        blocks.append(blk.astype(F32))
    rg = jnp.concatenate(blocks, axis=0) * gg_ref[0].astype(F32)
    rgn = _rms(rg, gr_ref[...]).astype(BF16)
    mix = jnp.dot(fn_ref[0], wo_ref[:df, :], preferred_element_type=F32)
    mix += jnp.dot(rgn, wo_ref[df:, :], preferred_element_type=F32)
    x1 = x_ref[0] + gtm_ref[0] * mix
    x1_ref[0] = x1
    h2 = _rms(x1, gffn_ref[...]) * (1.0 + scf_ref[0]) + shf_ref[0]
    h2_ref[0] = _pack_halves(h2)

    logits = _dot3_nt(wr_ref[...], h2) + br_ref[...]
    eidx = lax.broadcasted_iota(jnp.int32, logits.shape, 0)
    vals, idxs = [], []
    for _ in range(TOP_K):
        m = jnp.max(logits, axis=0, keepdims=True)
        sel = jnp.min(jnp.where(logits == m, eidx, N_EXPERTS), axis=0, keepdims=True)
        vals.append(m)
        idxs.append(sel)
        logits = jnp.where(eidx == sel, -jnp.inf, logits)
    ex = [jnp.exp(v - vals[0]) for v in vals]
    den = ex[0] + ex[1] + ex[2] + ex[3]
    for k in range(TOP_K):
        gate_ref[k:k + 1, :] = ex[k] / den
        idx_ref[k:k + 1, :] = idxs[k]


def _stage_m(fn, hs, gg, x, gt_m, sh_f, sc_f, g_out_r, g_ffn, w_out_bf, w_router_t, b_router, jmat, tm):
    b, s, d = x.shape
    df = fn.shape[2]
    dr = hs.shape[2]
    nt = s // tm
    ne = w_router_t.shape[0]
    vec = pl.BlockSpec((1, 1, d), lambda i, t: (i, 0, 0))
    half = lambda dd: pl.BlockSpec((1, tm, dd), lambda i, t: (i, t, 0))
    full = lambda shape: pl.BlockSpec(shape, lambda i, t: (0,) * len(shape))
    tok = pl.BlockSpec((TOP_K, tm), lambda i, t: (0, i * nt + t))
    return pl.pallas_call(
        functools.partial(_stage_m_kernel, df=df),
        grid=(b, nt),
        in_specs=[half(df), half(dr), half(dr), half(d), vec, vec, vec,
                  full((1, dr)), full((1, d)), full((d, d)), full((ne, d)), full((ne, 1)),
                  full((GRID_W, GRID_W))],
        out_specs=[half(d), half(d // 2), tok, tok],
        out_shape=[jax.ShapeDtypeStruct((b, s, d), F32), jax.ShapeDtypeStruct((b, s, d // 2), jnp.int32),
                   jax.ShapeDtypeStruct((TOP_K, b * s), jnp.int32),
                   jax.ShapeDtypeStruct((TOP_K, b * s), F32)],
        compiler_params=_cparams(("parallel", "arbitrary")),
        name="stage_m",
    )(fn, hs, gg, x, gt_m, sh_f, sc_f, g_out_r.reshape(1, dr), g_ffn.reshape(1, d), w_out_bf,
      w_router_t, b_router.reshape(ne, 1), jmat)


def _rank_kernel(idx_ref, tri_ref, rank_ref, cnt_ref, carry_s):
    c = pl.program_id(0)

    @pl.when(c == 0)
    def _():
        carry_s[...] = jnp.zeros_like(carry_s)

    l = idx_ref.shape[1]
    eidx = lax.broadcasted_iota(jnp.int32, (N_EXPERTS, l), 0)
    for k in range(TOP_K):
        onehot = eidx == idx_ref[k:k + 1, :]
        oh = jnp.where(onehot, 1.0, 0.0)
        prefix = jnp.dot(oh.astype(BF16), tri_ref[...], preferred_element_type=F32)
        carry = carry_s[:, 0:1]
        rank = jnp.sum(jnp.where(onehot, prefix - 1.0 + carry, 0.0), axis=0, keepdims=True)
        rank_ref[k:k + 1, :] = rank.astype(jnp.int32)
        carry_s[...] = carry_s[...] + jnp.sum(oh, axis=1, keepdims=True)
    cnt_ref[...] = carry_s[...].astype(jnp.int32)


def _dispatch_ranks(idx, tri, tl):
    k, t = idx.shape
    return pl.pallas_call(
        _rank_kernel,
        grid=(t // tl,),
        in_specs=[pl.BlockSpec((k, tl), lambda c: (0, c)),
                  pl.BlockSpec((tl, tl), lambda c: (0, 0))],
        out_specs=[pl.BlockSpec((k, tl), lambda c: (0, c)),
                   pl.BlockSpec((N_EXPERTS, LANES), lambda c: (0, 0))],
        out_shape=[jax.ShapeDtypeStruct((k, t), jnp.int32),
                   jax.ShapeDtypeStruct((N_EXPERTS, LANES), jnp.int32)],
        scratch_shapes=[pltpu.VMEM((N_EXPERTS, LANES), F32)],
        compiler_params=_cparams(("arbitrary",)),
        name="dispatch_ranks",
    )(idx, tri)


def _moe_kernel(be_ref, bv_ref, nu_ref, x_ref, wgu_ref, bgu_ref, wd_ref, bd_ref, o_ref, wgu_s, wd_s, *, dff):
    i = pl.program_id(0)
    h = x_ref.shape[1]

    @pl.when(jnp.logical_or(i == 0, be_ref[i] != be_ref[jnp.maximum(i - 1, 0)]))
    def _():
        wgu_s[...] = wgu_ref[0].astype(BF16)
        wd_s[...] = wd_ref[0].astype(BF16)

    @pl.when(i < nu_ref[0])
    def _():
        rows = lax.broadcasted_iota(jnp.int32, x_ref.shape, 0)
        xw = jnp.where(rows < bv_ref[i], x_ref[...], 0)
        xa, xb = _unpack_halves(xw)
        gu = jnp.dot(xa.astype(BF16), wgu_s[:h, :], preferred_element_type=F32)
        gu += jnp.dot(xb.astype(BF16), wgu_s[h:, :], preferred_element_type=F32)
        gu += bgu_ref[0]
        gt = jnp.minimum(gu[:, :dff], SWIGLU_LIMIT)
        up = jnp.clip(gu[:, dff:], -SWIGLU_LIMIT, SWIGLU_LIMIT)
        act = (up + 1.0) * (gt * jax.nn.sigmoid(SWIGLU_ALPHA * gt))
        out = jnp.dot(act.astype(BF16), wd_s[...], preferred_element_type=F32) + bd_ref[0]
        o_ref[...] = _pack_halves(out)


def _moe_experts(blk_expert, blk_valid, n_used, xs, wgu, bgu, wd, bd, tmm):
    cap, h = xs.shape
    ne, d, dff2 = wgu.shape
    dff = dff2 // 2
    row_blk = lambda i, be, bv, nu: (jnp.minimum(i, nu[0] - 1), 0)
    wsel = lambda i, be, bv, nu: (be[i], 0, 0)
    grid_spec = pltpu.PrefetchScalarGridSpec(
        num_scalar_prefetch=3,
        grid=(cap // tmm,),
        in_specs=[pl.BlockSpec((tmm, h), row_blk),
                  pl.BlockSpec((1, d, dff2), wsel),
                  pl.BlockSpec((1, 1, dff2), wsel),
                  pl.BlockSpec((1, dff, d), wsel),
                  pl.BlockSpec((1, 1, d), wsel)],
        out_specs=pl.BlockSpec((tmm, h), row_blk),
        scratch_shapes=[pltpu.VMEM((d, dff2), BF16), pltpu.VMEM((dff, d), BF16)],
    )
    return pl.pallas_call(
        functools.partial(_moe_kernel, dff=dff),
        grid_spec=grid_spec,
        out_shape=jax.ShapeDtypeStruct((cap, h), jnp.int32),
        compiler_params=_cparams(("arbitrary",)),
        name="moe_experts",
    )(blk_expert, blk_valid, n_used, xs, wgu, bgu.reshape(ne, 1, dff2), wd, bd.reshape(ne, 1, d))


SC_CHUNK = 64


def _sc_workers():
    info = plsc.get_sparse_core_info()
    return info.num_cores, info.num_subcores


def _sc_dispatch(rows, dest, cap):
    n, w = rows.shape
    topk = dest.shape[0] // n
    nc, ns = _sc_workers()
    per_w = n // (nc * ns)
    assert per_w % SC_CHUNK == 0
    mesh = plsc.VectorSubcoreMesh(core_axis_name="c", subcore_axis_name="s")

    @functools.partial(
        pl.kernel, mesh=mesh,
        out_type=jax.ShapeDtypeStruct((cap, w), jnp.int32),
        scratch_types=[pltpu.VMEM((SC_CHUNK,), jnp.int32),
                       pltpu.VMEM((SC_CHUNK, w), jnp.int32),
                       pltpu.SemaphoreType.DMA],
    )
    def scatter_rows(rows_hbm, dest_hbm, out_hbm, idx_v, rows_v, sem):
        base = (lax.axis_index("s") * nc + lax.axis_index("c")) * per_w

        @pl.loop(0, per_w // SC_CHUNK)
        def _(j):
            off = pl.multiple_of(base + j * SC_CHUNK, SC_CHUNK)
            pltpu.sync_copy(rows_hbm.at[pl.ds(off, SC_CHUNK)], rows_v)
            for k in range(topk):
                pltpu.sync_copy(dest_hbm.at[pl.ds(k * n + off, SC_CHUNK)], idx_v)
                pltpu.async_copy(rows_v, out_hbm.at[idx_v], sem).wait()

    return scatter_rows(rows, dest)


def _sc_gather(table, idx):
    n = idx.shape[0]
    w = table.shape[1]
    nc, ns = _sc_workers()
    per_w = n // (nc * ns)
    assert per_w % SC_CHUNK == 0
    mesh = plsc.VectorSubcoreMesh(core_axis_name="c", subcore_axis_name="s")

    @functools.partial(
        pl.kernel, mesh=mesh,
        out_type=jax.ShapeDtypeStruct((n, w), jnp.int32),
        scratch_types=[pltpu.VMEM((SC_CHUNK,), jnp.int32),
                       pltpu.VMEM((SC_CHUNK, w), jnp.int32),
                       pltpu.SemaphoreType.DMA],
    )
    def gather_rows(table_hbm, idx_hbm, out_hbm, idx_v, rows_v, sem):
        base = (lax.axis_index("s") * nc + lax.axis_index("c")) * per_w

        @pl.loop(0, per_w // SC_CHUNK)
        def _(j):
            off = pl.multiple_of(base + j * SC_CHUNK, SC_CHUNK)
            pltpu.sync_copy(idx_hbm.at[pl.ds(off, SC_CHUNK)], idx_v)
            pltpu.async_copy(table_hbm.at[idx_v], rows_v, sem).wait()
            pltpu.sync_copy(rows_v, out_hbm.at[pl.ds(off, SC_CHUNK)])

    return gather_rows(table, idx)


def _combine_kernel(x1_ref, y_ref, gate_ref, gtf_ref, g_ref, o_ref):
    h = y_ref.shape[3]
    gates = gate_ref[0]
    moe_a = moe_b = None
    for k in range(TOP_K):
        ya, yb = _unpack_halves(y_ref[k, 0])
        gk = gates[:, k:k + 1]
        moe_a = gk * ya if k == 0 else moe_a + gk * ya
        moe_b = gk * yb if k == 0 else moe_b + gk * yb
    za = x1_ref[0, :, :h] + gtf_ref[0, :, :h] * moe_a
    zb = x1_ref[0, :, h:] + gtf_ref[0, :, h:] * moe_b
    ms = (jnp.sum(za * za, axis=-1, keepdims=True) + jnp.sum(zb * zb, axis=-1, keepdims=True)) / (2 * h)
    inv = lax.rsqrt(ms + EPS)
    o_ref[0, :, :h] = za * inv * g_ref[:, :h]
    o_ref[0, :, h:] = zb * inv * g_ref[:, h:]


def _combine(x1, yk, gates_t, gt_f, g_final, tm):
    b, s, d = x1.shape
    h = yk.shape[3]
    return pl.pallas_call(
        _combine_kernel,
        grid=(b, s // tm),
        in_specs=[pl.BlockSpec((1, tm, d), lambda i, t: (i, t, 0)),
                  pl.BlockSpec((TOP_K, 1, tm, h), lambda i, t: (0, i, t, 0)),
                  pl.BlockSpec((1, tm, TOP_K), lambda i, t: (i, t, 0)),
                  pl.BlockSpec((1, 1, d), lambda i, t: (i, 0, 0)),
                  pl.BlockSpec((1, d), lambda i, t: (0, 0))],
        out_specs=pl.BlockSpec((1, tm, d), lambda i, t: (i, t, 0)),
        out_shape=jax.ShapeDtypeStruct((b, s, d), F32),
        compiler_params=_cparams(("parallel", "arbitrary")),
        name="combine",
    )(x1, yk, gates_t, gt_f, g_final.reshape(1, d))


def _dft_tables(rows, gd):
    seq = rows * GRID_W
    n = np.arange(rows)
    ang1 = 2.0 * np.pi * np.outer(n, n) / rows
    d2 = np.concatenate([np.cos(ang1), -np.sin(ang1)], axis=0)
    k1 = np.arange(rows)[:, None, None]
    k2 = np.arange(GRID_W)[None, :, None]
    n2 = np.arange(GRID_W)[None, None, :]
    ang2 = 2.0 * np.pi * ((n2 * (k1 + rows * k2)) % seq) / seq
    ec, es = np.cos(ang2), np.sin(ang2)
    etab = np.concatenate([np.concatenate([ec, es], axis=2),
                           np.concatenate([-es, ec], axis=2)], axis=1)
    c = np.arange(gd)
    angc = 2.0 * np.pi * np.outer(c, c) / gd
    scale = 1.0 / np.sqrt(seq * gd)
    return (jnp.asarray(d2, BF16), jnp.asarray(etab, BF16),
            jnp.asarray(np.cos(angc) * scale, F32), jnp.asarray(np.sin(angc) * scale, F32))


def _block_diag(w):
    h, i, o = w.shape
    eye = jnp.eye(h, dtype=w.dtype)
    return (eye[:, None, :, None] * w[:, :, None, :]).reshape(h * i, h * o)


def kernel(x, c, ctx, c_ctx, w_mod, b_mod, g_norm_mix, g_norm_ffn, w_in, w_fourier, conv_w, conv_b,
           rg_w_a, rg_b_a, rg_w_x, rg_b_x, rg_lam, g_out_fourier, g_out_rg, w_out, w_router, b_router,
           w_gate_up, b_gate_up, w_down, b_down, g_final):
    assert w_mod.shape[0] == 1, "single-layer stack only"
    b, s, d = x.shape
    df = w_fourier.shape[1] * w_fourier.shape[2]
    dr = conv_w.shape[2]
    gd = w_fourier.shape[2]
    rows = s // GRID_W
    t = b * s
    ne = w_router.shape[2]

    mrows = -(-(b + 1) // SUBLANES) * SUBLANES
    cond = jnp.zeros((mrows, d), F32).at[:b].set(c).at[b].set(c_ctx)
    mod = _adaln(cond, w_mod[0], b_mod[0])
    sh_m, sc_m, gt_m, sh_f, sc_f, gt_f = [mod[:b, k * d:(k + 1) * d].reshape(b, 1, d) for k in range(N_MOD)]
    csh_m = mod[b:b + 1, 0:d].reshape(1, 1, d)
    csc_m = mod[b:b + 1, d:2 * d].reshape(1, 1, d)

    d2, etab, cmat, smat = _dft_tables(rows, gd)
    jmat = jnp.asarray(np.eye(GRID_W)[::-1].copy(), BF16)

    w_in_bf = w_in[0].astype(BF16)
    f, xs, gg = _stage_b(x, sh_m, sc_m, g_norm_mix[0], w_in_bf, jmat, df, dr, tm=512)
    xr_ctx = _stage_b_ctx(ctx, csh_m, csc_m, g_norm_mix[0], w_in_bf[:, df:df + dr])

    wcat = jnp.stack([jnp.concatenate([_block_diag(rg_w_a[0, dd]), _block_diag(rg_w_x[0, dd])], axis=1)
                      for dd in range(2)]).astype(BF16)
    bcat = jnp.concatenate([rg_b_a[0], rg_b_x[0]], axis=1).reshape(2, 1, 2 * dr)
    lam = rg_lam[0].reshape(2, 1, dr)
    h0 = jnp.zeros((b, SUBLANES, dr), F32)
    _, hfin_ctx = _rg_scan(xr_ctx, h0, conv_w[0], conv_b[0], wcat, bcat, lam, tc=ctx.shape[1])
    hs, _ = _rg_scan(xs, hfin_ctx, conv_w[0], conv_b[0], wcat, bcat, lam, tc=512)

    cw, sw = _fold_fourier(cmat, smat, w_fourier[0])
    bdc = _block_diag(cw).astype(BF16)
    bds = _block_diag(sw).astype(BF16)
    y = _fourier_stage1(f.reshape(b, rows, GRID_W * df), d2, tl=4096)
    fn = _fourier_stage2(y.reshape(b, 2, rows, GRID_W, df), etab, bdc, bds, g_out_fourier[0],
                         kb=min(rows, 8))
    fn = fn.reshape(b, s, df)

    x1, h2, idx, gates = _stage_m(fn, hs, gg, x, gt_m, sh_f, sc_f, g_out_rg[0], g_norm_ffn[0],
                                  w_out[0].astype(BF16), w_router[0].T, b_router[0], jmat, tm=512)

    tl = 512
    tri = jnp.asarray(np.triu(np.ones((tl, tl))), BF16)
    rank, cnt = _dispatch_ranks(idx, tri, tl)
    counts = cnt[:, 0]
    tmm = 256
    padded = (counts + tmm - 1) // tmm * tmm
    pad_end = jnp.cumsum(padded)
    pad_start = pad_end - padded
    eids = jnp.arange(ne, dtype=jnp.int32)
    dest = rank + jnp.sum(jnp.where(idx[:, :, None] == eids, pad_start, 0), axis=-1)
    n_blocks = -(-(t * TOP_K) // tmm) + ne
    cap = n_blocks * tmm
    n_used = (pad_end[-1] // tmm).astype(jnp.int32).reshape(1)
    blk_start = jnp.arange(n_blocks, dtype=jnp.int32) * tmm
    blk_expert = jnp.sum(blk_start[:, None] >= pad_end[None, :], axis=1).astype(jnp.int32)
    last_expert = jnp.sum(pad_end[-1] - tmm >= pad_end).astype(jnp.int32)
    blk_expert = jnp.minimum(blk_expert, last_expert)
    sel = blk_expert[:, None] == eids
    blk_first = jnp.sum(jnp.where(sel, pad_start, 0), axis=1)
    blk_count = jnp.sum(jnp.where(sel, counts, 0), axis=1)
    blk_valid = jnp.clip(blk_count - (blk_start - blk_first), 0, tmm).astype(jnp.int32)

    x_sorted = _sc_dispatch(h2.reshape(t, d // 2), dest.reshape(-1), cap)
    y_sorted = _moe_experts(blk_expert, blk_valid, n_used, x_sorted,
                            w_gate_up[0], b_gate_up[0], w_down[0], b_down[0], tmm)
    yk = _sc_gather(y_sorted, dest.reshape(-1)).reshape(TOP_K, b, s, d // 2)
    return _combine(x1, yk, gates.T.reshape(b, s, TOP_K), gt_f, g_final, tm=512)
```

```python
import functools

import numpy as np
import jax
import jax.numpy as jnp
from jax import lax
from jax.experimental import pallas as pl
from jax.experimental.pallas import tpu as pltpu
from jax.experimental.pallas import tpu_sc as plsc

GRID_W = 64
FOURIER_GROUPS = 4
RG_HEADS = 8
CONV_W = 4
CONV_PAD_LO = 2
RG_C = 8.0
N_EXPERTS = 32
TOP_K = 4
SWIGLU_LIMIT = 7.0
SWIGLU_ALPHA = 1.702
N_MOD = 6
EPS = 1e-6

LANES = 128
SUBLANES = 8
VMEM_LIMIT_BYTES = 56 * 1024 * 1024

F32 = jnp.float32
BF16 = jnp.bfloat16


def _cparams(sem):
    return pltpu.CompilerParams(dimension_semantics=sem, vmem_limit_bytes=VMEM_LIMIT_BYTES)


def _split_bf16(a):
    hi = a.astype(BF16)
    lo = (a - hi.astype(F32)).astype(BF16)
    return hi, lo


def _dot3(a, b):
    ah, al = _split_bf16(a)
    bh, bl = _split_bf16(b)
    out = jnp.dot(ah, bh, preferred_element_type=F32)
    out += jnp.dot(ah, bl, preferred_element_type=F32)
    out += jnp.dot(al, bh, preferred_element_type=F32)
    return out


def _dot3_nt(a, b):
    dn = (((1,), (1,)), ((), ()))
    ah, al = _split_bf16(a)
    bh, bl = _split_bf16(b)
    out = lax.dot_general(ah, bh, dn, preferred_element_type=F32)
    out += lax.dot_general(ah, bl, dn, preferred_element_type=F32)
    out += lax.dot_general(al, bh, dn, preferred_element_type=F32)
    return out


def _gelu_tanh(x):
    return 0.5 * x * (1.0 + jnp.tanh(0.7978845608028654 * (x + 0.044715 * (x * x * x))))


def _rms(x, g):
    return x * lax.rsqrt(jnp.mean(x * x, axis=-1, keepdims=True) + EPS) * g


def _pack_halves(v):
    h = v.shape[1] // 2
    hi = lax.bitcast_convert_type(v[:, :h].astype(BF16).astype(F32), jnp.uint32)
    lo = lax.bitcast_convert_type(v[:, h:].astype(BF16).astype(F32), jnp.uint32)
    return lax.bitcast_convert_type(hi | (lo >> 16), jnp.int32)


def _unpack_halves(w):
    u = lax.bitcast_convert_type(w, jnp.uint32)
    hi = lax.bitcast_convert_type(u & jnp.uint32(0xFFFF0000), F32)
    lo = lax.bitcast_convert_type(u << 16, F32)
    return hi, lo


def _adaln_kernel(c_ref, w_ref, b_ref, o_ref):
    s = c_ref[...]
    s = s * jax.nn.sigmoid(s)
    o_ref[...] = _dot3(s, w_ref[...]) + b_ref[...]


def _adaln(cond, w_mod, b_mod):
    m, d = cond.shape
    n = w_mod.shape[1]
    tn = n // N_MOD
    return pl.pallas_call(
        _adaln_kernel,
        grid=(n // tn,),
        in_specs=[pl.BlockSpec((m, d), lambda i: (0, 0)),
                  pl.BlockSpec((d, tn), lambda i: (0, i)),
                  pl.BlockSpec((1, tn), lambda i: (0, i))],
        out_specs=pl.BlockSpec((m, tn), lambda i: (0, i)),
        out_shape=jax.ShapeDtypeStruct((m, n), F32),
        compiler_params=_cparams(("arbitrary",)),
        name="adaln",
    )(cond, w_mod, b_mod.reshape(1, n))


def _fold_kernel(c_ref, s_ref, w_ref, cw_ref, sw_ref):
    w = w_ref[0]
    cw_ref[0] = _dot3(c_ref[...], w)
    sw_ref[0] = _dot3(s_ref[...], w)


def _fold_fourier(cmat, smat, w_f):
    g, gd, _ = w_f.shape
    spec_m = pl.BlockSpec((gd, gd), lambda i: (0, 0))
    spec_w = pl.BlockSpec((1, gd, gd), lambda i: (i, 0, 0))
    return pl.pallas_call(
        _fold_kernel,
        grid=(g,),
        in_specs=[spec_m, spec_m, spec_w],
        out_specs=[spec_w, spec_w],
        out_shape=[jax.ShapeDtypeStruct((g, gd, gd), F32)] * 2,
        compiler_params=_cparams(("arbitrary",)),
        name="fold_fourier",
    )(cmat, smat, w_f)


def _stage_b_kernel(x_ref, sh_ref, sc_ref, g_ref, w_ref, j_ref, f_ref, xs_ref, gg_ref, *, df, dr):
    h = _rms(x_ref[0], g_ref[...]) * (1.0 + sc_ref[0]) + sh_ref[0]
    hb = h.astype(BF16)
    f_ref[0] = jnp.dot(hb, w_ref[:, :df], preferred_element_type=F32).astype(BF16)
    xr = jnp.dot(hb, w_ref[:, df:df + dr], preferred_element_type=F32).astype(BF16)
    tm = xr.shape[0]
    for r in range(tm // GRID_W):
        blk = xr[r * GRID_W:(r + 1) * GRID_W]
        if r % 2 == 1:
            blk = jnp.dot(j_ref[...], blk, preferred_element_type=F32).astype(BF16)
        xs_ref[0, r * GRID_W:(r + 1) * GRID_W, :] = blk
    gr = jnp.dot(hb, w_ref[:, df + dr:], preferred_element_type=F32)
    gg_ref[0] = _gelu_tanh(gr).astype(BF16)


def _stage_b(x, shift, scale, g, w_in_bf, jmat, df, dr, tm):
    b, s, d = x.shape
    n = w_in_bf.shape[1]
    vec = pl.BlockSpec((1, 1, d), lambda i, t: (i, 0, 0))
    out = pl.BlockSpec((1, tm, df), lambda i, t: (i, t, 0))
    return pl.pallas_call(
        functools.partial(_stage_b_kernel, df=df, dr=dr),
        grid=(b, s // tm),
        in_specs=[pl.BlockSpec((1, tm, d), lambda i, t: (i, t, 0)), vec, vec,
                  pl.BlockSpec((1, d), lambda i, t: (0, 0)),
                  pl.BlockSpec((d, n), lambda i, t: (0, 0)),
                  pl.BlockSpec((GRID_W, GRID_W), lambda i, t: (0, 0))],
        out_specs=[out, out, out],
        out_shape=[jax.ShapeDtypeStruct((b, s, df), BF16)] * 3,
        compiler_params=_cparams(("parallel", "arbitrary")),
        name="stage_b",
    )(x, shift, scale, g.reshape(1, d), w_in_bf, jmat)


def _stage_b_ctx_kernel(x_ref, sh_ref, sc_ref, g_ref, w_ref, xr_ref):
    h = _rms(x_ref[0], g_ref[...]) * (1.0 + sc_ref[0]) + sh_ref[0]
    xr_ref[0] = jnp.dot(h.astype(BF16), w_ref[...], preferred_element_type=F32).astype(BF16)


def _stage_b_ctx(ctx, shift, scale, g, w_xr_bf):
    b, s, d = ctx.shape
    dr = w_xr_bf.shape[1]
    vec = pl.BlockSpec((1, 1, d), lambda i: (0, 0, 0))
    return pl.pallas_call(
        _stage_b_ctx_kernel,
        grid=(b,),
        in_specs=[pl.BlockSpec((1, s, d), lambda i: (i, 0, 0)), vec, vec,
                  pl.BlockSpec((1, d), lambda i: (0, 0)),
                  pl.BlockSpec((d, dr), lambda i: (0, 0))],
        out_specs=pl.BlockSpec((1, s, dr), lambda i: (i, 0, 0)),
        out_shape=jax.ShapeDtypeStruct((b, s, dr), BF16),
        compiler_params=_cparams(("arbitrary",)),
        name="stage_b_ctx",
    )(ctx, shift, scale, g.reshape(1, d), w_xr_bf)


HALO = 16


def _seg_pitch(seg_len):
    n8 = seg_len // SUBLANES
    return SUBLANES * (n8 + 1 - n8 % 2)


def _sigmoid(x):
    return 0.5 * jnp.tanh(0.5 * x) + 0.5


def _rg_kernel(xs_ref, h0_ref, cw_ref, cb_ref, w_ref, b_ref, lam_ref, out_ref, hfin_ref,
               hf_s, xc_s, a_s, u_s, hl_s, p_s, hc_s, *, tc, nchunk, seq, dr):
    p = pl.program_id(1)
    j = pl.program_id(2)
    cidx = jnp.where(p == 0, j, nchunk - 1 - j)
    start = pl.multiple_of(cidx * tc, tc)
    nseg = SUBLANES
    sl = tc // nseg
    pitch = _seg_pitch(sl)
    nslab = dr // LANES

    @pl.when(p == 0)
    def _():
        cur = xs_ref[0, pl.ds(start, tc), :].astype(F32)
        pstart = pl.multiple_of(jnp.maximum(start - HALO, 0), HALO)
        nstart = pl.multiple_of(jnp.minimum(start + tc, seq - HALO), HALO)
        prev = xs_ref[0, pl.ds(pstart, HALO), :].astype(F32)
        nxt = xs_ref[0, pl.ds(nstart, HALO), :].astype(F32)
        prev = jnp.where(cidx > 0, prev, 0.0)
        nxt = jnp.where(cidx < nchunk - 1, nxt, 0.0)
        ext = jnp.concatenate([prev, cur, nxt], axis=0)
        xc = cb_ref[...] + cw_ref[0:1, :] * ext[HALO - CONV_PAD_LO:HALO - CONV_PAD_LO + tc]
        for k in range(1, CONV_W):
            o = HALO - CONV_PAD_LO + k
            xc = xc + cw_ref[k:k + 1, :] * ext[o:o + tc]
        xc_s[pl.ds(start, tc), :] = xc

    xc = xc_s[pl.ds(start, tc), :]
    gates = jnp.dot(xc.astype(BF16), w_ref[p], preferred_element_type=F32) + b_ref[p]
    r = _sigmoid(gates[:, :dr])
    i = _sigmoid(gates[:, dr:])
    log_a = (-RG_C) * r * jax.nn.softplus(-lam_ref[p])
    a = jnp.exp(log_a)
    u = jnp.sqrt(-jnp.tanh(log_a) * (1.0 + a * a)) * (i * xc)
    for g in range(nseg):
        for c in range(nslab):
            a_s[c, g * pitch:g * pitch + sl, :] = a[g * sl:(g + 1) * sl, c * LANES:(c + 1) * LANES]
            u_s[c, g * pitch:g * pitch + sl, :] = u[g * sl:(g + 1) * sl, c * LANES:(c + 1) * LANES]

    @pl.when(jnp.logical_and(p == 0, j == 0))
    def _():
        hfin_ref[...] = jnp.zeros_like(hfin_ref)

    @pl.when(j == 0)
    def _():
        hc_s[0:1, :] = h0_ref[0, pl.ds(p, 1), :]

    def local_scan(reverse):
        def body(q, carry):
            t = (sl - 1 - q) if reverse else q
            rows = pl.ds(t, nseg, stride=pitch)
            hs, ps = carry
            nh, npr = [], []
            for c in range(nslab):
                av = a_s[c, rows, :]
                h = av * hs[c] + u_s[c, rows, :]
                pr = av * ps[c]
                hl_s[c, rows, :] = h
                p_s[c, rows, :] = pr
                nh.append(h)
                npr.append(pr)
            return tuple(nh), tuple(npr)
        zero = tuple(jnp.zeros((nseg, LANES), F32) for _ in range(nslab))
        one = tuple(jnp.ones((nseg, LANES), F32) for _ in range(nslab))
        lax.fori_loop(0, sl, body, (zero, one), unroll=4)

    @pl.when(p == 0)
    def _():
        local_scan(False)
        for c in range(nslab):
            lanes = slice(c * LANES, (c + 1) * LANES)
            carry = hc_s[0:1, lanes]
            for g in range(nseg):
                seg = slice(g * pitch, g * pitch + sl)
                h = hl_s[c, seg, :] + p_s[c, seg, :] * carry
                hf_s[pl.ds(start + g * sl, sl), lanes] = h
                carry = h[sl - 1:sl, :]
            hc_s[0:1, lanes] = carry

    @pl.when(p == 1)
    def _():
        local_scan(True)
        for c in range(nslab):
            lanes = slice(c * LANES, (c + 1) * LANES)
            carry = hc_s[0:1, lanes]
            for g in range(nseg - 1, -1, -1):
                seg = slice(g * pitch, g * pitch + sl)
                h = hl_s[c, seg, :] + p_s[c, seg, :] * carry
                out_ref[0, g * sl:(g + 1) * sl, lanes] = (h + hf_s[pl.ds(start + g * sl, sl), lanes]).astype(BF16)
                carry = h[0:1, :]
            hc_s[0:1, lanes] = carry

    @pl.when(j == nchunk - 1)
    def _():
        hfin_ref[0, pl.ds(p, 1), :] = hc_s[0:1, :]


def _rg_scan(xs, h0, conv_w, conv_b, wcat, bcat, lam, tc):
    b, s, dr = xs.shape
    nchunk = s // tc
    last = nchunk - 1
    seg_rows = SUBLANES * _seg_pitch(tc // SUBLANES)
    seg_buf = pltpu.VMEM((dr // LANES, seg_rows, LANES), F32)
    full2 = lambda shape: pl.BlockSpec(shape, lambda i, p, j: (0,) * len(shape))
    return pl.pallas_call(
        functools.partial(_rg_kernel, tc=tc, nchunk=nchunk, seq=s, dr=dr),
        grid=(b, 2, nchunk),
        in_specs=[pl.BlockSpec((1, s, dr), lambda i, p, j: (i, 0, 0)),
                  pl.BlockSpec((1, SUBLANES, dr), lambda i, p, j: (i, 0, 0)),
                  full2((CONV_W, dr)), full2((1, dr)),
                  full2((2, dr, 2 * dr)), full2((2, 1, 2 * dr)), full2((2, 1, dr))],
        out_specs=[pl.BlockSpec((1, tc, dr), lambda i, p, j: (i, jnp.where(p == 0, last, last - j), 0)),
                   pl.BlockSpec((1, SUBLANES, dr), lambda i, p, j: (i, 0, 0))],
        out_shape=[jax.ShapeDtypeStruct((b, s, dr), BF16),
                   jax.ShapeDtypeStruct((b, SUBLANES, dr), F32)],
        scratch_shapes=[pltpu.VMEM((s, dr), F32), pltpu.VMEM((s, dr), F32),
                        seg_buf, seg_buf, seg_buf, seg_buf, pltpu.VMEM((SUBLANES, dr), F32)],
        compiler_params=_cparams(("arbitrary", "arbitrary", "arbitrary")),
        name="rg_scan",
    )(xs, h0, conv_w, conv_b.reshape(1, dr), wcat, bcat, lam)


def _f1_kernel(d_ref, x_ref, y_ref):
    y_ref[0] = jnp.dot(d_ref[...], x_ref[0], preferred_element_type=F32).astype(BF16)


def _fourier_stage1(fv, d2, tl):
    b, r, n = fv.shape
    return pl.pallas_call(
        _f1_kernel,
        grid=(b, n // tl),
        in_specs=[pl.BlockSpec((2 * r, r), lambda i, l: (0, 0)),
                  pl.BlockSpec((1, r, tl), lambda i, l: (i, 0, l))],
        out_specs=pl.BlockSpec((1, 2 * r, tl), lambda i, l: (i, 0, l)),
        out_shape=jax.ShapeDtypeStruct((b, 2 * r, n), BF16),
        compiler_params=_cparams(("parallel", "arbitrary")),
        name="fourier_stage1",
    )(d2, fv)


def _f2_kernel(y_ref, e_ref, bc_ref, bs_ref, g_ref, o_ref, *, kb, df):
    zr, zi = [], []
    for q in range(kb):
        yk = jnp.concatenate([y_ref[0, 0, q], y_ref[0, 1, q]], axis=0)
        z = jnp.dot(e_ref[q], yk, preferred_element_type=F32)
        zr.append(z[:GRID_W])
        zi.append(z[GRID_W:])
    zr = jnp.concatenate(zr, axis=0).astype(BF16)
    zi = jnp.concatenate(zi, axis=0).astype(BF16)
    o = jnp.dot(zr, bc_ref[...], preferred_element_type=F32)
    o += jnp.dot(zi, bs_ref[...], preferred_element_type=F32)
    on = _rms(o, g_ref[...]).astype(BF16)
    for q in range(kb):
        o_ref[0, :, q * df:(q + 1) * df] = on[q * GRID_W:(q + 1) * GRID_W]


def _fourier_stage2(y5, etab, bdc, bds, g, kb):
    b, _, r, w, df = y5.shape
    return pl.pallas_call(
        functools.partial(_f2_kernel, kb=kb, df=df),
        grid=(b, r // kb),
        in_specs=[pl.BlockSpec((1, 2, kb, w, df), lambda i, k: (i, 0, k, 0, 0)),
                  pl.BlockSpec((kb, 2 * w, 2 * w), lambda i, k: (k, 0, 0)),
                  pl.BlockSpec((df, df), lambda i, k: (0, 0)),
                  pl.BlockSpec((df, df), lambda i, k: (0, 0)),
                  pl.BlockSpec((1, df), lambda i, k: (0, 0))],
        out_specs=pl.BlockSpec((1, w, kb * df), lambda i, k: (i, 0, k)),
        out_shape=jax.ShapeDtypeStruct((b, w, r * df), BF16),
        compiler_params=_cparams(("parallel", "arbitrary")),
        name="fourier_stage2",
    )(y5, etab, bdc, bds, g.reshape(1, df))


def _stage_m_kernel(fn_ref, hs_ref, gg_ref, x_ref, gtm_ref, shf_ref, scf_ref, gr_ref, gffn_ref,
                    wo_ref, wr_ref, br_ref, j_ref, x1_ref, h2_ref, idx_ref, gate_ref, *, df):
    tm = x_ref.shape[1]
    hs = hs_ref[0]
    blocks = []
    for r in range(tm // GRID_W):
        blk = hs[r * GRID_W:(r + 1) * GRID_W]
        if r % 2 == 1:
            blk = jnp.dot(j_ref[...], blk, preferred_element_type=F32)
        blocks.append(blk.astype(F32))
    rg = jnp.concatenate(blocks, axis=0) * gg_ref[0].astype(F32)
    rgn = _rms(rg, gr_ref[...]).astype(BF16)
    mix = jnp.dot(fn_ref[0], wo_ref[:df, :], preferred_element_type=F32)
    mix += jnp.dot(rgn, wo_ref[df:, :], preferred_element_type=F32)
    x1 = x_ref[0] + gtm_ref[0] * mix
    x1_ref[0] = x1
    h2 = _rms(x1, gffn_ref[...]) * (1.0 + scf_ref[0]) + shf_ref[0]
    h2_ref[0] = _pack_halves(h2)

    logits = _dot3_nt(wr_ref[...], h2) + br_ref[...]
    eidx = lax.broadcasted_iota(jnp.int32, logits.shape, 0)
    vals, idxs = [], []
    for _ in range(TOP_K):
        m = jnp.max(logits, axis=0, keepdims=True)
        sel = jnp.min(jnp.where(logits == m, eidx, N_EXPERTS), axis=0, keepdims=True)
        vals.append(m)
        idxs.append(sel)
        logits = jnp.where(eidx == sel, -jnp.inf, logits)
    ex = [jnp.exp(v - vals[0]) for v in vals]
    den = ex[0] + ex[1] + ex[2] + ex[3]
    for k in range(TOP_K):
        gate_ref[k:k + 1, :] = ex[k] / den
        idx_ref[k:k + 1, :] = idxs[k]


def _stage_m(fn, hs, gg, x, gt_m, sh_f, sc_f, g_out_r, g_ffn, w_out_bf, w_router_t, b_router, jmat, tm):
    b, s, d = x.shape
    df = fn.shape[2]
    dr = hs.shape[2]
    nt = s // tm
    ne = w_router_t.shape[0]
    vec = pl.BlockSpec((1, 1, d), lambda i, t: (i, 0, 0))
    half = lambda dd: pl.BlockSpec((1, tm, dd), lambda i, t: (i, t, 0))
    full = lambda shape: pl.BlockSpec(shape, lambda i, t: (0,) * len(shape))
    tok = pl.BlockSpec((TOP_K, tm), lambda i, t: (0, i * nt + t))
    return pl.pallas_call(
        functools.partial(_stage_m_kernel, df=df),
        grid=(b, nt),
        in_specs=[half(df), half(dr), half(dr), half(d), vec, vec, vec,
                  full((1, dr)), full((1, d)), full((d, d)), full((ne, d)), full((ne, 1)),
                  full((GRID_W, GRID_W))],
        out_specs=[half(d), half(d // 2), tok, tok],
        out_shape=[jax.ShapeDtypeStruct((b, s, d), F32), jax.ShapeDtypeStruct((b, s, d // 2), jnp.int32),
                   jax.ShapeDtypeStruct((TOP_K, b * s), jnp.int32),
                   jax.ShapeDtypeStruct((TOP_K, b * s), F32)],
        compiler_params=_cparams(("parallel", "arbitrary")),
        name="stage_m",
    )(fn, hs, gg, x, gt_m, sh_f, sc_f, g_out_r.reshape(1, dr), g_ffn.reshape(1, d), w_out_bf,
      w_router_t, b_router.reshape(ne, 1), jmat)


def _rank_kernel(idx_ref, tri_ref, rank_ref, cnt_ref, carry_s):
    c = pl.program_id(0)

    @pl.when(c == 0)
    def _():
        carry_s[...] = jnp.zeros_like(carry_s)

    l = idx_ref.shape[1]
    eidx = lax.broadcasted_iota(jnp.int32, (N_EXPERTS, l), 0)
    for k in range(TOP_K):
        onehot = eidx == idx_ref[k:k + 1, :]
        oh = jnp.where(onehot, 1.0, 0.0)
        prefix = jnp.dot(oh.astype(BF16), tri_ref[...], preferred_element_type=F32)
        carry = carry_s[:, 0:1]
        rank = jnp.sum(jnp.where(onehot, prefix - 1.0 + carry, 0.0), axis=0, keepdims=True)
        rank_ref[k:k + 1, :] = rank.astype(jnp.int32)
        carry_s[...] = carry_s[...] + jnp.sum(oh, axis=1, keepdims=True)
    cnt_ref[...] = carry_s[...].astype(jnp.int32)


def _dispatch_ranks(idx, tri, tl):
    k, t = idx.shape
    return pl.pallas_call(
        _rank_kernel,
        grid=(t // tl,),
        in_specs=[pl.BlockSpec((k, tl), lambda c: (0, c)),
                  pl.BlockSpec((tl, tl), lambda c: (0, 0))],
        out_specs=[pl.BlockSpec((k, tl), lambda c: (0, c)),
                   pl.BlockSpec((N_EXPERTS, LANES), lambda c: (0, 0))],
        out_shape=[jax.ShapeDtypeStruct((k, t), jnp.int32),
                   jax.ShapeDtypeStruct((N_EXPERTS, LANES), jnp.int32)],
        scratch_shapes=[pltpu.VMEM((N_EXPERTS, LANES), F32)],
        compiler_params=_cparams(("arbitrary",)),
        name="dispatch_ranks",
    )(idx, tri)


def _moe_kernel(be_ref, bv_ref, nu_ref, x_ref, wgu_ref, bgu_ref, wd_ref, bd_ref, o_ref, wgu_s, wd_s, *, dff):
    i = pl.program_id(0)
    h = x_ref.shape[1]

    @pl.when(jnp.logical_or(i == 0, be_ref[i] != be_ref[jnp.maximum(i - 1, 0)]))
    def _():
        wgu_s[...] = wgu_ref[0].astype(BF16)
        wd_s[...] = wd_ref[0].astype(BF16)

    @pl.when(i < nu_ref[0])
    def _():
        rows = lax.broadcasted_iota(jnp.int32, x_ref.shape, 0)
        xw = jnp.where(rows < bv_ref[i], x_ref[...], 0)
        xa, xb = _unpack_halves(xw)
        gu = jnp.dot(xa.astype(BF16), wgu_s[:h, :], preferred_element_type=F32)
        gu += jnp.dot(xb.astype(BF16), wgu_s[h:, :], preferred_element_type=F32)
        gu += bgu_ref[0]
        gt = jnp.minimum(gu[:, :dff], SWIGLU_LIMIT)
        up = jnp.clip(gu[:, dff:], -SWIGLU_LIMIT, SWIGLU_LIMIT)
        act = (up + 1.0) * (gt * jax.nn.sigmoid(SWIGLU_ALPHA * gt))
        out = jnp.dot(act.astype(BF16), wd_s[...], preferred_element_type=F32) + bd_ref[0]
        o_ref[...] = _pack_halves(out)


def _moe_experts(blk_expert, blk_valid, n_used, xs, wgu, bgu, wd, bd, tmm):
    cap, h = xs.shape
    ne, d, dff2 = wgu.shape
    dff = dff2 // 2
    row_blk = lambda i, be, bv, nu: (jnp.minimum(i, nu[0] - 1), 0)
    wsel = lambda i, be, bv, nu: (be[i], 0, 0)
    grid_spec = pltpu.PrefetchScalarGridSpec(
        num_scalar_prefetch=3,
        grid=(cap // tmm,),
        in_specs=[pl.BlockSpec((tmm, h), row_blk),
                  pl.BlockSpec((1, d, dff2), wsel),
                  pl.BlockSpec((1, 1, dff2), wsel),
                  pl.BlockSpec((1, dff, d), wsel),
                  pl.BlockSpec((1, 1, d), wsel)],
        out_specs=pl.BlockSpec((tmm, h), row_blk),
        scratch_shapes=[pltpu.VMEM((d, dff2), BF16), pltpu.VMEM((dff, d), BF16)],
    )
    return pl.pallas_call(
        functools.partial(_moe_kernel, dff=dff),
        grid_spec=grid_spec,
        out_shape=jax.ShapeDtypeStruct((cap, h), jnp.int32),
        compiler_params=_cparams(("arbitrary",)),
        name="moe_experts",
    )(blk_expert, blk_valid, n_used, xs, wgu, bgu.reshape(ne, 1, dff2), wd, bd.reshape(ne, 1, d))


SC_CHUNK = 64


def _sc_workers():
    info = plsc.get_sparse_core_info()
    return info.num_cores, info.num_subcores


def _sc_dispatch(rows, dest, cap):
    n, w = rows.shape
    topk = dest.shape[0] // n
    nc, ns = _sc_workers()
    per_w = n // (nc * ns)
    assert per_w % SC_CHUNK == 0
    mesh = plsc.VectorSubcoreMesh(core_axis_name="c", subcore_axis_name="s")

    @functools.partial(
        pl.kernel, mesh=mesh,
        out_type=jax.ShapeDtypeStruct((cap, w), jnp.int32),
        scratch_types=[pltpu.VMEM((SC_CHUNK,), jnp.int32),
                       pltpu.VMEM((SC_CHUNK, w), jnp.int32),
                       pltpu.SemaphoreType.DMA],
    )
    def scatter_rows(rows_hbm, dest_hbm, out_hbm, idx_v, rows_v, sem):
        base = (lax.axis_index("s") * nc + lax.axis_index("c")) * per_w

        @pl.loop(0, per_w // SC_CHUNK)
        def _(j):
            off = pl.multiple_of(base + j * SC_CHUNK, SC_CHUNK)
            pltpu.sync_copy(rows_hbm.at[pl.ds(off, SC_CHUNK)], rows_v)
            for k in range(topk):
                pltpu.sync_copy(dest_hbm.at[pl.ds(k * n + off, SC_CHUNK)], idx_v)
                pltpu.async_copy(rows_v, out_hbm.at[idx_v], sem).wait()

    return scatter_rows(rows, dest)


def _sc_gather(table, idx):
    n = idx.shape[0]
    w = table.shape[1]
    nc, ns = _sc_workers()
    per_w = n // (nc * ns)
    assert per_w % SC_CHUNK == 0
    mesh = plsc.VectorSubcoreMesh(core_axis_name="c", subcore_axis_name="s")

    @functools.partial(
        pl.kernel, mesh=mesh,
        out_type=jax.ShapeDtypeStruct((n, w), jnp.int32),
        scratch_types=[pltpu.VMEM((SC_CHUNK,), jnp.int32),
                       pltpu.VMEM((SC_CHUNK, w), jnp.int32),
                       pltpu.SemaphoreType.DMA],
    )
    def gather_rows(table_hbm, idx_hbm, out_hbm, idx_v, rows_v, sem):
        base = (lax.axis_index("s") * nc + lax.axis_index("c")) * per_w

        @pl.loop(0, per_w // SC_CHUNK)
        def _(j):
            off = pl.multiple_of(base + j * SC_CHUNK, SC_CHUNK)
            pltpu.sync_copy(idx_hbm.at[pl.ds(off, SC_CHUNK)], idx_v)
            pltpu.async_copy(table_hbm.at[idx_v], rows_v, sem).wait()
            pltpu.sync_copy(rows_v, out_hbm.at[pl.ds(off, SC_CHUNK)])

    return gather_rows(table, idx)


def _combine_kernel(x1_ref, y_ref, gate_ref, gtf_ref, g_ref, o_ref):
    h = y_ref.shape[3]
    gates = gate_ref[0]
    moe_a = moe_b = None
    for k in range(TOP_K):
        ya, yb = _unpack_halves(y_ref[k, 0])
        gk = gates[:, k:k + 1]
        moe_a = gk * ya if k == 0 else moe_a + gk * ya
        moe_b = gk * yb if k == 0 else moe_b + gk * yb
    za = x1_ref[0, :, :h] + gtf_ref[0, :, :h] * moe_a
    zb = x1_ref[0, :, h:] + gtf_ref[0, :, h:] * moe_b
    ms = (jnp.sum(za * za, axis=-1, keepdims=True) + jnp.sum(zb * zb, axis=-1, keepdims=True)) / (2 * h)
    inv = lax.rsqrt(ms + EPS)
    o_ref[0, :, :h] = za * inv * g_ref[:, :h]
    o_ref[0, :, h:] = zb * inv * g_ref[:, h:]


def _combine(x1, yk, gates_t, gt_f, g_final, tm):
    b, s, d = x1.shape
    h = yk.shape[3]
    return pl.pallas_call(
        _combine_kernel,
        grid=(b, s // tm),
        in_specs=[pl.BlockSpec((1, tm, d), lambda i, t: (i, t, 0)),
                  pl.BlockSpec((TOP_K, 1, tm, h), lambda i, t: (0, i, t, 0)),
                  pl.BlockSpec((1, tm, TOP_K), lambda i, t: (i, t, 0)),
                  pl.BlockSpec((1, 1, d), lambda i, t: (i, 0, 0)),
                  pl.BlockSpec((1, d), lambda i, t: (0, 0))],
        out_specs=pl.BlockSpec((1, tm, d), lambda i, t: (i, t, 0)),
        out_shape=jax.ShapeDtypeStruct((b, s, d), F32),
        compiler_params=_cparams(("parallel", "arbitrary")),
        name="combine",
    )(x1, yk, gates_t, gt_f, g_final.reshape(1, d))


def _dft_tables(rows, gd):
    seq = rows * GRID_W
    n = np.arange(rows)
    ang1 = 2.0 * np.pi * np.outer(n, n) / rows
    d2 = np.concatenate([np.cos(ang1), -np.sin(ang1)], axis=0)
    k1 = np.arange(rows)[:, None, None]
    k2 = np.arange(GRID_W)[None, :, None]
    n2 = np.arange(GRID_W)[None, None, :]
    ang2 = 2.0 * np.pi * ((n2 * (k1 + rows * k2)) % seq) / seq
    ec, es = np.cos(ang2), np.sin(ang2)
    etab = np.concatenate([np.concatenate([ec, es], axis=2),
                           np.concatenate([-es, ec], axis=2)], axis=1)
    c = np.arange(gd)
    angc = 2.0 * np.pi * np.outer(c, c) / gd
    scale = 1.0 / np.sqrt(seq * gd)
    return (jnp.asarray(d2, BF16), jnp.asarray(etab, BF16),
            jnp.asarray(np.cos(angc) * scale, F32), jnp.asarray(np.sin(angc) * scale, F32))


def _block_diag(w):
    h, i, o = w.shape
    eye = jnp.eye(h, dtype=w.dtype)
    return (eye[:, None, :, None] * w[:, :, None, :]).reshape(h * i, h * o)


def kernel(x, c, ctx, c_ctx, w_mod, b_mod, g_norm_mix, g_norm_ffn, w_in, w_fourier, conv_w, conv_b,
           rg_w_a, rg_b_a, rg_w_x, rg_b_x, rg_lam, g_out_fourier, g_out_rg, w_out, w_router, b_router,
           w_gate_up, b_gate_up, w_down, b_down, g_final):
    assert w_mod.shape[0] == 1, "single-layer stack only"
    b, s, d = x.shape
    df = w_fourier.shape[1] * w_fourier.shape[2]
    dr = conv_w.shape[2]
    gd = w_fourier.shape[2]
    rows = s // GRID_W
    t = b * s
    ne = w_router.shape[2]

    mrows = -(-(b + 1) // SUBLANES) * SUBLANES
    cond = jnp.zeros((mrows, d), F32).at[:b].set(c).at[b].set(c_ctx)
    mod = _adaln(cond, w_mod[0], b_mod[0])
    sh_m, sc_m, gt_m, sh_f, sc_f, gt_f = [mod[:b, k * d:(k + 1) * d].reshape(b, 1, d) for k in range(N_MOD)]
    csh_m = mod[b:b + 1, 0:d].reshape(1, 1, d)
    csc_m = mod[b:b + 1, d:2 * d].reshape(1, 1, d)

    d2, etab, cmat, smat = _dft_tables(rows, gd)
    jmat = jnp.asarray(np.eye(GRID_W)[::-1].copy(), BF16)

    w_in_bf = w_in[0].astype(BF16)
    f, xs, gg = _stage_b(x, sh_m, sc_m, g_norm_mix[0], w_in_bf, jmat, df, dr, tm=512)
    xr_ctx = _stage_b_ctx(ctx, csh_m, csc_m, g_norm_mix[0], w_in_bf[:, df:df + dr])

    wcat = jnp.stack([jnp.concatenate([_block_diag(rg_w_a[0, dd]), _block_diag(rg_w_x[0, dd])], axis=1)
                      for dd in range(2)]).astype(BF16)
    bcat = jnp.concatenate([rg_b_a[0], rg_b_x[0]], axis=1).reshape(2, 1, 2 * dr)
    lam = rg_lam[0].reshape(2, 1, dr)
    h0 = jnp.zeros((b, SUBLANES, dr), F32)
    _, hfin_ctx = _rg_scan(xr_ctx, h0, conv_w[0], conv_b[0], wcat, bcat, lam, tc=ctx.shape[1])
    hs, _ = _rg_scan(xs, hfin_ctx, conv_w[0], conv_b[0], wcat, bcat, lam, tc=512)

    cw, sw = _fold_fourier(cmat, smat, w_fourier[0])
    bdc = _block_diag(cw).astype(BF16)
    bds = _block_diag(sw).astype(BF16)
    y = _fourier_stage1(f.reshape(b, rows, GRID_W * df), d2, tl=4096)
    fn = _fourier_stage2(y.reshape(b, 2, rows, GRID_W, df), etab, bdc, bds, g_out_fourier[0],
                         kb=min(rows, 8))
    fn = fn.reshape(b, s, df)

    x1, h2, idx, gates = _stage_m(fn, hs, gg, x, gt_m, sh_f, sc_f, g_out_rg[0], g_norm_ffn[0],
                                  w_out[0].astype(BF16), w_router[0].T, b_router[0], jmat, tm=512)

    tl = 512
    tri = jnp.asarray(np.triu(np.ones((tl, tl))), BF16)
    rank, cnt = _dispatch_ranks(idx, tri, tl)
    counts = cnt[:, 0]
    tmm = 512
    padded = (counts + tmm - 1) // tmm * tmm
    pad_end = jnp.cumsum(padded)
    pad_start = pad_end - padded
    eids = jnp.arange(ne, dtype=jnp.int32)
    dest = rank + jnp.sum(jnp.where(idx[:, :, None] == eids, pad_start, 0), axis=-1)
    n_blocks = -(-(t * TOP_K) // tmm) + ne
    cap = n_blocks * tmm
    n_used = (pad_end[-1] // tmm).astype(jnp.int32).reshape(1)
    blk_start = jnp.arange(n_blocks, dtype=jnp.int32) * tmm
    blk_expert = jnp.sum(blk_start[:, None] >= pad_end[None, :], axis=1).astype(jnp.int32)
    last_expert = jnp.sum(pad_end[-1] - tmm >= pad_end).astype(jnp.int32)
    blk_expert = jnp.minimum(blk_expert, last_expert)
    sel = blk_expert[:, None] == eids
    blk_first = jnp.sum(jnp.where(sel, pad_start, 0), axis=1)
    blk_count = jnp.sum(jnp.where(sel, counts, 0), axis=1)
    blk_valid = jnp.clip(blk_count - (blk_start - blk_first), 0, tmm).astype(jnp.int32)

    x_sorted = _sc_dispatch(h2.reshape(t, d // 2), dest.reshape(-1), cap)
    y_sorted = _moe_experts(blk_expert, blk_valid, n_used, x_sorted,
                            w_gate_up[0], b_gate_up[0], w_down[0], b_down[0], tmm)
    yk = _sc_gather(y_sorted, dest.reshape(-1)).reshape(TOP_K, b, s, d // 2)
    return _combine(x1, yk, gates.T.reshape(b, s, TOP_K), gt_f, g_final, tm=512)
```

```python
import functools

import numpy as np
import jax
import jax.numpy as jnp
from jax import lax
from jax.experimental import pallas as pl
from jax.experimental.pallas import tpu as pltpu
from jax.experimental.pallas import tpu_sc as plsc

GRID_W = 64
FOURIER_GROUPS = 4
RG_HEADS = 8
CONV_W = 4
CONV_PAD_LO = 2
RG_C = 8.0
N_EXPERTS = 32
TOP_K = 4
SWIGLU_LIMIT = 7.0
SWIGLU_ALPHA = 1.702
N_MOD = 6
EPS = 1e-6

LANES = 128
SUBLANES = 8
VMEM_LIMIT_BYTES = 56 * 1024 * 1024
TOKEN_TILE = 1024
RANK_TILE = 512
SCAN_CHUNK = 512
MOE_ROW_TILE = 512

F32 = jnp.float32
BF16 = jnp.bfloat16


def _cparams(sem):
    return pltpu.CompilerParams(dimension_semantics=sem, vmem_limit_bytes=VMEM_LIMIT_BYTES)


def _split_bf16(a):
    hi = a.astype(BF16)
    lo = (a - hi.astype(F32)).astype(BF16)
    return hi, lo


def _dot3(a, b):
    ah, al = _split_bf16(a)
    bh, bl = _split_bf16(b)
    out = jnp.dot(ah, bh, preferred_element_type=F32)
    out += jnp.dot(ah, bl, preferred_element_type=F32)
    out += jnp.dot(al, bh, preferred_element_type=F32)
    return out


def _dot3_nt(a, b):
    dn = (((1,), (1,)), ((), ()))
    ah, al = _split_bf16(a)
    bh, bl = _split_bf16(b)
    out = lax.dot_general(ah, bh, dn, preferred_element_type=F32)
    out += lax.dot_general(ah, bl, dn, preferred_element_type=F32)
    out += lax.dot_general(al, bh, dn, preferred_element_type=F32)
    return out


def _gelu_tanh(x):
    return 0.5 * x * (1.0 + jnp.tanh(0.7978845608028654 * (x + 0.044715 * (x * x * x))))


def _rms(x, g):
    return x * lax.rsqrt(jnp.mean(x * x, axis=-1, keepdims=True) + EPS) * g


def _pack_halves(v):
    h = v.shape[1] // 2
    hi = lax.bitcast_convert_type(v[:, :h].astype(BF16).astype(F32), jnp.uint32)
    lo = lax.bitcast_convert_type(v[:, h:].astype(BF16).astype(F32), jnp.uint32)
    return lax.bitcast_convert_type(hi | (lo >> 16), jnp.int32)


def _unpack_halves(w):
    u = lax.bitcast_convert_type(w, jnp.uint32)
    hi = lax.bitcast_convert_type(u & jnp.uint32(0xFFFF0000), F32)
    lo = lax.bitcast_convert_type(u << 16, F32)
    return hi, lo


def _adaln_kernel(c_ref, w_ref, b_ref, o_ref):
    s = c_ref[...]
    s = s * jax.nn.sigmoid(s)
    o_ref[...] = _dot3(s, w_ref[...]) + b_ref[...]


def _adaln(cond, w_mod, b_mod):
    m, d = cond.shape
    n = w_mod.shape[1]
    tn = n // N_MOD
    return pl.pallas_call(
        _adaln_kernel,
        grid=(n // tn,),
        in_specs=[pl.BlockSpec((m, d), lambda i: (0, 0)),
                  pl.BlockSpec((d, tn), lambda i: (0, i)),
                  pl.BlockSpec((1, tn), lambda i: (0, i))],
        out_specs=pl.BlockSpec((m, tn), lambda i: (0, i)),
        out_shape=jax.ShapeDtypeStruct((m, n), F32),
        compiler_params=_cparams(("arbitrary",)),
        name="adaln",
    )(cond, w_mod, b_mod.reshape(1, n))


def _fold_kernel(c_ref, s_ref, w_ref, cw_ref, sw_ref):
    w = w_ref[0]
    cw_ref[0] = _dot3(c_ref[...], w)
    sw_ref[0] = _dot3(s_ref[...], w)


def _fold_fourier(cmat, smat, w_f):
    g, gd, _ = w_f.shape
    spec_m = pl.BlockSpec((gd, gd), lambda i: (0, 0))
    spec_w = pl.BlockSpec((1, gd, gd), lambda i: (i, 0, 0))
    return pl.pallas_call(
        _fold_kernel,
        grid=(g,),
        in_specs=[spec_m, spec_m, spec_w],
        out_specs=[spec_w, spec_w],
        out_shape=[jax.ShapeDtypeStruct((g, gd, gd), F32)] * 2,
        compiler_params=_cparams(("arbitrary",)),
        name="fold_fourier",
    )(cmat, smat, w_f)


def _stage_b_kernel(x_ref, sh_ref, sc_ref, g_ref, w_ref, j_ref, f_ref, xs_ref, gg_ref, *, df, dr):
    h = _rms(x_ref[0], g_ref[...]) * (1.0 + sc_ref[0]) + sh_ref[0]
    hb = h.astype(BF16)
    f_ref[0] = jnp.dot(hb, w_ref[:, :df], preferred_element_type=F32).astype(BF16)
    xr = jnp.dot(hb, w_ref[:, df:df + dr], preferred_element_type=F32).astype(BF16)
    tm = xr.shape[0]
    for r in range(tm // GRID_W):
        blk = xr[r * GRID_W:(r + 1) * GRID_W]
        if r % 2 == 1:
            blk = jnp.dot(j_ref[...], blk, preferred_element_type=F32).astype(BF16)
        xs_ref[0, r * GRID_W:(r + 1) * GRID_W, :] = blk
    gr = jnp.dot(hb, w_ref[:, df + dr:], preferred_element_type=F32)
    gg_ref[0] = _gelu_tanh(gr).astype(BF16)


def _stage_b(x, shift, scale, g, w_in_bf, jmat, df, dr, tm):
    b, s, d = x.shape
    n = w_in_bf.shape[1]
    vec = pl.BlockSpec((1, 1, d), lambda i, t: (i, 0, 0))
    out = pl.BlockSpec((1, tm, df), lambda i, t: (i, t, 0))
    return pl.pallas_call(
        functools.partial(_stage_b_kernel, df=df, dr=dr),
        grid=(b, s // tm),
        in_specs=[pl.BlockSpec((1, tm, d), lambda i, t: (i, t, 0)), vec, vec,
                  pl.BlockSpec((1, d), lambda i, t: (0, 0)),
                  pl.BlockSpec((d, n), lambda i, t: (0, 0)),
                  pl.BlockSpec((GRID_W, GRID_W), lambda i, t: (0, 0))],
        out_specs=[out, out, out],
        out_shape=[jax.ShapeDtypeStruct((b, s, df), BF16)] * 3,
        compiler_params=_cparams(("parallel", "arbitrary")),
        name="stage_b",
    )(x, shift, scale, g.reshape(1, d), w_in_bf, jmat)


def _stage_b_ctx_kernel(x_ref, sh_ref, sc_ref, g_ref, w_ref, xr_ref):
    h = _rms(x_ref[0], g_ref[...]) * (1.0 + sc_ref[0]) + sh_ref[0]
    xr_ref[0] = jnp.dot(h.astype(BF16), w_ref[...], preferred_element_type=F32).astype(BF16)


def _stage_b_ctx(ctx, shift, scale, g, w_xr_bf):
    b, s, d = ctx.shape
    dr = w_xr_bf.shape[1]
    vec = pl.BlockSpec((1, 1, d), lambda i: (0, 0, 0))
    return pl.pallas_call(
        _stage_b_ctx_kernel,
        grid=(b,),
        in_specs=[pl.BlockSpec((1, s, d), lambda i: (i, 0, 0)), vec, vec,
                  pl.BlockSpec((1, d), lambda i: (0, 0)),
                  pl.BlockSpec((d, dr), lambda i: (0, 0))],
        out_specs=pl.BlockSpec((1, s, dr), lambda i: (i, 0, 0)),
        out_shape=jax.ShapeDtypeStruct((b, s, dr), BF16),
        compiler_params=_cparams(("arbitrary",)),
        name="stage_b_ctx",
    )(ctx, shift, scale, g.reshape(1, d), w_xr_bf)


HALO = 16


def _seg_pitch(seg_len):
    n8 = seg_len // SUBLANES
    return SUBLANES * (n8 + 1 - n8 % 2)


def _sigmoid(x):
    return 0.5 * jnp.tanh(0.5 * x) + 0.5


def _rg_kernel(xs_ref, h0_ref, cw_ref, cb_ref, w_ref, b_ref, lam_ref, out_ref, hfin_ref,
               hf_s, xc_s, a_s, u_s, hl_s, p_s, hc_s, *, tc, nchunk, seq, dr):
    p = pl.program_id(1)
    j = pl.program_id(2)
    cidx = jnp.where(p == 0, j, nchunk - 1 - j)
    start = pl.multiple_of(cidx * tc, tc)
    nseg = SUBLANES
    sl = tc // nseg
    pitch = _seg_pitch(sl)
    nslab = dr // LANES

    @pl.when(p == 0)
    def _():
        cur = xs_ref[0, pl.ds(start, tc), :].astype(F32)
        pstart = pl.multiple_of(jnp.maximum(start - HALO, 0), HALO)
        nstart = pl.multiple_of(jnp.minimum(start + tc, seq - HALO), HALO)
        prev = xs_ref[0, pl.ds(pstart, HALO), :].astype(F32)
        nxt = xs_ref[0, pl.ds(nstart, HALO), :].astype(F32)
        prev = jnp.where(cidx > 0, prev, 0.0)
        nxt = jnp.where(cidx < nchunk - 1, nxt, 0.0)
        ext = jnp.concatenate([prev, cur, nxt], axis=0)
        xc = cb_ref[...] + cw_ref[0:1, :] * ext[HALO - CONV_PAD_LO:HALO - CONV_PAD_LO + tc]
        for k in range(1, CONV_W):
            o = HALO - CONV_PAD_LO + k
            xc = xc + cw_ref[k:k + 1, :] * ext[o:o + tc]
        xc_s[pl.ds(start, tc), :] = xc

    xc = xc_s[pl.ds(start, tc), :]
    gates = jnp.dot(xc.astype(BF16), w_ref[p], preferred_element_type=F32) + b_ref[p]
    r = _sigmoid(gates[:, :dr])
    i = _sigmoid(gates[:, dr:])
    log_a = (-RG_C) * r * jax.nn.softplus(-lam_ref[p])
    a = jnp.exp(log_a)
    u = jnp.sqrt(-jnp.tanh(log_a) * (1.0 + a * a)) * (i * xc)
    for g in range(nseg):
        for c in range(nslab):
            a_s[c, g * pitch:g * pitch + sl, :] = a[g * sl:(g + 1) * sl, c * LANES:(c + 1) * LANES]
            u_s[c, g * pitch:g * pitch + sl, :] = u[g * sl:(g + 1) * sl, c * LANES:(c + 1) * LANES]

    @pl.when(jnp.logical_and(p == 0, j == 0))
    def _():
        hfin_ref[...] = jnp.zeros_like(hfin_ref)

    @pl.when(j == 0)
    def _():
        hc_s[0:1, :] = h0_ref[0, pl.ds(p, 1), :]

    def local_scan(reverse):
        def body(q, carry):
            t = (sl - 1 - q) if reverse else q
            rows = pl.ds(t, nseg, stride=pitch)
            hs, ps = carry
            nh, npr = [], []
            for c in range(nslab):
                av = a_s[c, rows, :]
                h = av * hs[c] + u_s[c, rows, :]
                pr = av * ps[c]
                hl_s[c, rows, :] = h
                p_s[c, rows, :] = pr
                nh.append(h)
                npr.append(pr)
            return tuple(nh), tuple(npr)
        zero = tuple(jnp.zeros((nseg, LANES), F32) for _ in range(nslab))
        one = tuple(jnp.ones((nseg, LANES), F32) for _ in range(nslab))
        lax.fori_loop(0, sl, body, (zero, one), unroll=4)

    @pl.when(p == 0)
    def _():
        local_scan(False)
        for c in range(nslab):
            lanes = slice(c * LANES, (c + 1) * LANES)
            carry = hc_s[0:1, lanes]
            for g in range(nseg):
                seg = slice(g * pitch, g * pitch + sl)
                h = hl_s[c, seg, :] + p_s[c, seg, :] * carry
                hf_s[pl.ds(start + g * sl, sl), lanes] = h
                carry = h[sl - 1:sl, :]
            hc_s[0:1, lanes] = carry

    @pl.when(p == 1)
    def _():
        local_scan(True)
        for c in range(nslab):
            lanes = slice(c * LANES, (c + 1) * LANES)
            carry = hc_s[0:1, lanes]
            for g in range(nseg - 1, -1, -1):
                seg = slice(g * pitch, g * pitch + sl)
                h = hl_s[c, seg, :] + p_s[c, seg, :] * carry
                out_ref[0, g * sl:(g + 1) * sl, lanes] = (h + hf_s[pl.ds(start + g * sl, sl), lanes]).astype(BF16)
                carry = h[0:1, :]
            hc_s[0:1, lanes] = carry

    @pl.when(j == nchunk - 1)
    def _():
        hfin_ref[0, pl.ds(p, 1), :] = hc_s[0:1, :]


def _rg_scan(xs, h0, conv_w, conv_b, wcat, bcat, lam, tc):
    b, s, dr = xs.shape
    nchunk = s // tc
    last = nchunk - 1
    seg_rows = SUBLANES * _seg_pitch(tc // SUBLANES)
    seg_buf = pltpu.VMEM((dr // LANES, seg_rows, LANES), F32)
    full2 = lambda shape: pl.BlockSpec(shape, lambda i, p, j: (0,) * len(shape))
    return pl.pallas_call(
        functools.partial(_rg_kernel, tc=tc, nchunk=nchunk, seq=s, dr=dr),
        grid=(b, 2, nchunk),
        in_specs=[pl.BlockSpec((1, s, dr), lambda i, p, j: (i, 0, 0)),
                  pl.BlockSpec((1, SUBLANES, dr), lambda i, p, j: (i, 0, 0)),
                  full2((CONV_W, dr)), full2((1, dr)),
                  full2((2, dr, 2 * dr)), full2((2, 1, 2 * dr)), full2((2, 1, dr))],
        out_specs=[pl.BlockSpec((1, tc, dr), lambda i, p, j: (i, jnp.where(p == 0, last, last - j), 0)),
                   pl.BlockSpec((1, SUBLANES, dr), lambda i, p, j: (i, 0, 0))],
        out_shape=[jax.ShapeDtypeStruct((b, s, dr), BF16),
                   jax.ShapeDtypeStruct((b, SUBLANES, dr), F32)],
        scratch_shapes=[pltpu.VMEM((s, dr), F32), pltpu.VMEM((s, dr), F32),
                        seg_buf, seg_buf, seg_buf, seg_buf, pltpu.VMEM((SUBLANES, dr), F32)],
        compiler_params=_cparams(("arbitrary", "arbitrary", "arbitrary")),
        name="rg_scan",
    )(xs, h0, conv_w, conv_b.reshape(1, dr), wcat, bcat, lam)


def _f1_kernel(d_ref, x_ref, y_ref):
    y_ref[0] = jnp.dot(d_ref[...], x_ref[0], preferred_element_type=F32).astype(BF16)


def _fourier_stage1(fv, d2, tl):
    b, r, n = fv.shape
    return pl.pallas_call(
        _f1_kernel,
        grid=(b, n // tl),
        in_specs=[pl.BlockSpec((2 * r, r), lambda i, l: (0, 0)),
                  pl.BlockSpec((1, r, tl), lambda i, l: (i, 0, l))],
        out_specs=pl.BlockSpec((1, 2 * r, tl), lambda i, l: (i, 0, l)),
        out_shape=jax.ShapeDtypeStruct((b, 2 * r, n), BF16),
        compiler_params=_cparams(("parallel", "arbitrary")),
        name="fourier_stage1",
    )(d2, fv)


def _f2_kernel(y_ref, e_ref, bc_ref, bs_ref, g_ref, o_ref, *, kb, df):
    zr, zi = [], []
    for q in range(kb):
        yk = jnp.concatenate([y_ref[0, 0, q], y_ref[0, 1, q]], axis=0)
        z = jnp.dot(e_ref[q], yk, preferred_element_type=F32)
        zr.append(z[:GRID_W])
        zi.append(z[GRID_W:])
    zr = jnp.concatenate(zr, axis=0).astype(BF16)
    zi = jnp.concatenate(zi, axis=0).astype(BF16)
    o = jnp.dot(zr, bc_ref[...], preferred_element_type=F32)
    o += jnp.dot(zi, bs_ref[...], preferred_element_type=F32)
    on = _rms(o, g_ref[...]).astype(BF16)
    for q in range(kb):
        o_ref[0, :, q * df:(q + 1) * df] = on[q * GRID_W:(q + 1) * GRID_W]


def _fourier_stage2(y5, etab, bdc, bds, g, kb):
    b, _, r, w, df = y5.shape
    return pl.pallas_call(
        functools.partial(_f2_kernel, kb=kb, df=df),
        grid=(b, r // kb),
        in_specs=[pl.BlockSpec((1, 2, kb, w, df), lambda i, k: (i, 0, k, 0, 0)),
                  pl.BlockSpec((kb, 2 * w, 2 * w), lambda i, k: (k, 0, 0)),
                  pl.BlockSpec((df, df), lambda i, k: (0, 0)),
                  pl.BlockSpec((df, df), lambda i, k: (0, 0)),
                  pl.BlockSpec((1, df), lambda i, k: (0, 0))],
        out_specs=pl.BlockSpec((1, w, kb * df), lambda i, k: (i, 0, k)),
        out_shape=jax.ShapeDtypeStruct((b, w, r * df), BF16),
        compiler_params=_cparams(("parallel", "arbitrary")),
        name="fourier_stage2",
    )(y5, etab, bdc, bds, g.reshape(1, df))


def _stage_m_kernel(fn_ref, hs_ref, gg_ref, x_ref, gtm_ref, shf_ref, scf_ref, gr_ref, gffn_ref,
                    wo_ref, wr_ref, br_ref, j_ref, x1_ref, h2_ref, idx_ref, gate_ref, *, df):
    tm = x_ref.shape[1]
    hs = hs_ref[0]
    blocks = []
    for r in range(tm // GRID_W):
        blk = hs[r * GRID_W:(r + 1) * GRID_W]
        if r % 2 == 1:
            blk = jnp.dot(j_ref[...], blk, preferred_element_type=F32)
        blocks.append(blk.astype(F32))
    rg = jnp.concatenate(blocks, axis=0) * gg_ref[0].astype(F32)
    rgn = _rms(rg, gr_ref[...]).astype(BF16)
    mix = jnp.dot(fn_ref[0], wo_ref[:df, :], preferred_element_type=F32)
    mix += jnp.dot(rgn, wo_ref[df:, :], preferred_element_type=F32)
    x1 = x_ref[0] + gtm_ref[0] * mix
    x1_ref[0] = x1
    h2 = _rms(x1, gffn_ref[...]) * (1.0 + scf_ref[0]) + shf_ref[0]
    h2_ref[0] = _pack_halves(h2)

    logits = _dot3_nt(wr_ref[...], h2) + br_ref[...]
    eidx = lax.broadcasted_iota(jnp.int32, logits.shape, 0)
    vals, idxs = [], []
    for _ in range(TOP_K):
        m = jnp.max(logits, axis=0, keepdims=True)
        sel = jnp.min(jnp.where(logits == m, eidx, N_EXPERTS), axis=0, keepdims=True)
        vals.append(m)
        idxs.append(sel)
        logits = jnp.where(eidx == sel, -jnp.inf, logits)
    ex = [jnp.exp(v - vals[0]) for v in vals]
    den = ex[0] + ex[1] + ex[2] + ex[3]
    for k in range(TOP_K):
        gate_ref[k:k + 1, :] = ex[k] / den
        idx_ref[k:k + 1, :] = idxs[k]


def _stage_m(fn, hs, gg, x, gt_m, sh_f, sc_f, g_out_r, g_ffn, w_out_bf, w_router_t, b_router, jmat, tm):
    b, s, d = x.shape
    df = fn.shape[2]
    dr = hs.shape[2]
    nt = s // tm
    ne = w_router_t.shape[0]
    vec = pl.BlockSpec((1, 1, d), lambda i, t: (i, 0, 0))
    half = lambda dd: pl.BlockSpec((1, tm, dd), lambda i, t: (i, t, 0))
    full = lambda shape: pl.BlockSpec(shape, lambda i, t: (0,) * len(shape))
    tok = pl.BlockSpec((TOP_K, tm), lambda i, t: (0, i * nt + t))
    return pl.pallas_call(
        functools.partial(_stage_m_kernel, df=df),
        grid=(b, nt),
        in_specs=[half(df), half(dr), half(dr), half(d), vec, vec, vec,
                  full((1, dr)), full((1, d)), full((d, d)), full((ne, d)), full((ne, 1)),
                  full((GRID_W, GRID_W))],
        out_specs=[half(d), half(d // 2), tok, tok],
        out_shape=[jax.ShapeDtypeStruct((b, s, d), F32), jax.ShapeDtypeStruct((b, s, d // 2), jnp.int32),
                   jax.ShapeDtypeStruct((TOP_K, b * s), jnp.int32),
                   jax.ShapeDtypeStruct((TOP_K, b * s), F32)],
        compiler_params=_cparams(("parallel", "arbitrary")),
        name="stage_m",
    )(fn, hs, gg, x, gt_m, sh_f, sc_f, g_out_r.reshape(1, dr), g_ffn.reshape(1, d), w_out_bf,
      w_router_t, b_router.reshape(ne, 1), jmat)


def _rank_kernel(idx_ref, tri_ref, rank_ref, cnt_ref, carry_s):
    c = pl.program_id(0)

    @pl.when(c == 0)
    def _():
        carry_s[...] = jnp.zeros_like(carry_s)

    l = idx_ref.shape[1]
    eidx = lax.broadcasted_iota(jnp.int32, (N_EXPERTS, l), 0)
    for k in range(TOP_K):
        onehot = eidx == idx_ref[k:k + 1, :]
        oh = jnp.where(onehot, 1.0, 0.0)
        prefix = jnp.dot(oh.astype(BF16), tri_ref[...], preferred_element_type=F32)
        carry = carry_s[:, 0:1]
        rank = jnp.sum(jnp.where(onehot, prefix - 1.0 + carry, 0.0), axis=0, keepdims=True)
        rank_ref[k:k + 1, :] = rank.astype(jnp.int32)
        carry_s[...] = carry_s[...] + jnp.sum(oh, axis=1, keepdims=True)
    cnt_ref[...] = carry_s[...].astype(jnp.int32)


def _dispatch_ranks(idx, tri, tl):
    k, t = idx.shape
    return pl.pallas_call(
        _rank_kernel,
        grid=(t // tl,),
        in_specs=[pl.BlockSpec((k, tl), lambda c: (0, c)),
                  pl.BlockSpec((tl, tl), lambda c: (0, 0))],
        out_specs=[pl.BlockSpec((k, tl), lambda c: (0, c)),
                   pl.BlockSpec((N_EXPERTS, LANES), lambda c: (0, 0))],
        out_shape=[jax.ShapeDtypeStruct((k, t), jnp.int32),
                   jax.ShapeDtypeStruct((N_EXPERTS, LANES), jnp.int32)],
        scratch_shapes=[pltpu.VMEM((N_EXPERTS, LANES), F32)],
        compiler_params=_cparams(("arbitrary",)),
        name="dispatch_ranks",
    )(idx, tri)


def _moe_kernel(be_ref, bv_ref, nu_ref, x_ref, wgu_ref, bgu_ref, wd_ref, bd_ref, o_ref, wgu_s, wd_s, *, dff):
    i = pl.program_id(0)
    h = x_ref.shape[1]

    @pl.when(jnp.logical_or(i == 0, be_ref[i] != be_ref[jnp.maximum(i - 1, 0)]))
    def _():
        wgu_s[...] = wgu_ref[0].astype(BF16)
        wd_s[...] = wd_ref[0].astype(BF16)

    @pl.when(i < nu_ref[0])
    def _():
        rows = lax.broadcasted_iota(jnp.int32, x_ref.shape, 0)
        xw = jnp.where(rows < bv_ref[i], x_ref[...], 0)
        xa, xb = _unpack_halves(xw)
        gu = jnp.dot(xa.astype(BF16), wgu_s[:h, :], preferred_element_type=F32)
        gu += jnp.dot(xb.astype(BF16), wgu_s[h:, :], preferred_element_type=F32)
        gu += bgu_ref[0]
        gt = jnp.minimum(gu[:, :dff], SWIGLU_LIMIT)
        up = jnp.clip(gu[:, dff:], -SWIGLU_LIMIT, SWIGLU_LIMIT)
        act = (up + 1.0) * (gt * _sigmoid(SWIGLU_ALPHA * gt))
        out = jnp.dot(act.astype(BF16), wd_s[...], preferred_element_type=F32) + bd_ref[0]
        o_ref[...] = _pack_halves(out)


def _moe_experts(blk_expert, blk_valid, n_used, xs, wgu, bgu, wd, bd, tmm):
    cap, h = xs.shape
    ne, d, dff2 = wgu.shape
    dff = dff2 // 2
    row_blk = lambda i, be, bv, nu: (jnp.minimum(i, nu[0] - 1), 0)
    wsel = lambda i, be, bv, nu: (be[i], 0, 0)
    grid_spec = pltpu.PrefetchScalarGridSpec(
        num_scalar_prefetch=3,
        grid=(cap // tmm,),
        in_specs=[pl.BlockSpec((tmm, h), row_blk),
                  pl.BlockSpec((1, d, dff2), wsel),
                  pl.BlockSpec((1, 1, dff2), wsel),
                  pl.BlockSpec((1, dff, d), wsel),
                  pl.BlockSpec((1, 1, d), wsel)],
        out_specs=pl.BlockSpec((tmm, h), row_blk),
        scratch_shapes=[pltpu.VMEM((d, dff2), BF16), pltpu.VMEM((dff, d), BF16)],
    )
    return pl.pallas_call(
        functools.partial(_moe_kernel, dff=dff),
        grid_spec=grid_spec,
        out_shape=jax.ShapeDtypeStruct((cap, h), jnp.int32),
        compiler_params=_cparams(("arbitrary",)),
        name="moe_experts",
    )(blk_expert, blk_valid, n_used, xs, wgu, bgu.reshape(ne, 1, dff2), wd, bd.reshape(ne, 1, d))


SC_CHUNK = 64


def _sc_workers():
    info = plsc.get_sparse_core_info()
    return info.num_cores, info.num_subcores


def _sc_dispatch(rows, dest, cap):
    n, w = rows.shape
    topk = dest.shape[0] // n
    nc, ns = _sc_workers()
    per_w = n // (nc * ns)
    assert per_w % SC_CHUNK == 0
    mesh = plsc.VectorSubcoreMesh(core_axis_name="c", subcore_axis_name="s")

    @functools.partial(
        pl.kernel, mesh=mesh,
        out_type=jax.ShapeDtypeStruct((cap, w), jnp.int32),
        scratch_types=[pltpu.VMEM((SC_CHUNK,), jnp.int32),
                       pltpu.VMEM((SC_CHUNK, w), jnp.int32),
                       pltpu.SemaphoreType.DMA],
    )
    def scatter_rows(rows_hbm, dest_hbm, out_hbm, idx_v, rows_v, sem):
        base = (lax.axis_index("s") * nc + lax.axis_index("c")) * per_w

        @pl.loop(0, per_w // SC_CHUNK)
        def _(j):
            off = pl.multiple_of(base + j * SC_CHUNK, SC_CHUNK)
            pltpu.sync_copy(rows_hbm.at[pl.ds(off, SC_CHUNK)], rows_v)
            for k in range(topk):
                pltpu.sync_copy(dest_hbm.at[pl.ds(k * n + off, SC_CHUNK)], idx_v)
                pltpu.async_copy(rows_v, out_hbm.at[idx_v], sem).wait()

    return scatter_rows(rows, dest)


def _sc_gather(table, idx):
    n = idx.shape[0]
    w = table.shape[1]
    nc, ns = _sc_workers()
    per_w = n // (nc * ns)
    assert per_w % SC_CHUNK == 0
    mesh = plsc.VectorSubcoreMesh(core_axis_name="c", subcore_axis_name="s")

    @functools.partial(
        pl.kernel, mesh=mesh,
        out_type=jax.ShapeDtypeStruct((n, w), jnp.int32),
        scratch_types=[pltpu.VMEM((SC_CHUNK,), jnp.int32),
                       pltpu.VMEM((SC_CHUNK, w), jnp.int32),
                       pltpu.SemaphoreType.DMA],
    )
    def gather_rows(table_hbm, idx_hbm, out_hbm, idx_v, rows_v, sem):
        base = (lax.axis_index("s") * nc + lax.axis_index("c")) * per_w

        @pl.loop(0, per_w // SC_CHUNK)
        def _(j):
            off = pl.multiple_of(base + j * SC_CHUNK, SC_CHUNK)
            pltpu.sync_copy(idx_hbm.at[pl.ds(off, SC_CHUNK)], idx_v)
            pltpu.async_copy(table_hbm.at[idx_v], rows_v, sem).wait()
            pltpu.sync_copy(rows_v, out_hbm.at[pl.ds(off, SC_CHUNK)])

    return gather_rows(table, idx)


def _combine_kernel(x1_ref, y_ref, gate_ref, gtf_ref, g_ref, o_ref):
    h = y_ref.shape[3]
    gates = gate_ref[0]
    moe_a = moe_b = None
    for k in range(TOP_K):
        ya, yb = _unpack_halves(y_ref[k, 0])
        gk = gates[:, k:k + 1]
        moe_a = gk * ya if k == 0 else moe_a + gk * ya
        moe_b = gk * yb if k == 0 else moe_b + gk * yb
    za = x1_ref[0, :, :h] + gtf_ref[0, :, :h] * moe_a
    zb = x1_ref[0, :, h:] + gtf_ref[0, :, h:] * moe_b
    ms = (jnp.sum(za * za, axis=-1, keepdims=True) + jnp.sum(zb * zb, axis=-1, keepdims=True)) / (2 * h)
    inv = lax.rsqrt(ms + EPS)
    o_ref[0, :, :h] = za * inv * g_ref[:, :h]
    o_ref[0, :, h:] = zb * inv * g_ref[:, h:]


def _combine(x1, yk, gates_t, gt_f, g_final, tm):
    b, s, d = x1.shape
    h = yk.shape[3]
    return pl.pallas_call(
        _combine_kernel,
        grid=(b, s // tm),
        in_specs=[pl.BlockSpec((1, tm, d), lambda i, t: (i, t, 0)),
                  pl.BlockSpec((TOP_K, 1, tm, h), lambda i, t: (0, i, t, 0)),
                  pl.BlockSpec((1, tm, TOP_K), lambda i, t: (i, t, 0)),
                  pl.BlockSpec((1, 1, d), lambda i, t: (i, 0, 0)),
                  pl.BlockSpec((1, d), lambda i, t: (0, 0))],
        out_specs=pl.BlockSpec((1, tm, d), lambda i, t: (i, t, 0)),
        out_shape=jax.ShapeDtypeStruct((b, s, d), F32),
        compiler_params=_cparams(("parallel", "arbitrary")),
        name="combine",
    )(x1, yk, gates_t, gt_f, g_final.reshape(1, d))


def _dft_tables(rows, gd):
    seq = rows * GRID_W
    n = np.arange(rows)
    ang1 = 2.0 * np.pi * np.outer(n, n) / rows
    d2 = np.concatenate([np.cos(ang1), -np.sin(ang1)], axis=0)
    k1 = np.arange(rows)[:, None, None]
    k2 = np.arange(GRID_W)[None, :, None]
    n2 = np.arange(GRID_W)[None, None, :]
    ang2 = 2.0 * np.pi * ((n2 * (k1 + rows * k2)) % seq) / seq
    ec, es = np.cos(ang2), np.sin(ang2)
    etab = np.concatenate([np.concatenate([ec, es], axis=2),
                           np.concatenate([-es, ec], axis=2)], axis=1)
    c = np.arange(gd)
    angc = 2.0 * np.pi * np.outer(c, c) / gd
    scale = 1.0 / np.sqrt(seq * gd)
    return (jnp.asarray(d2, BF16), jnp.asarray(etab, BF16),
            jnp.asarray(np.cos(angc) * scale, F32), jnp.asarray(np.sin(angc) * scale, F32))


def _block_diag(w):
    h, i, o = w.shape
    eye = jnp.eye(h, dtype=w.dtype)
    return (eye[:, None, :, None] * w[:, :, None, :]).reshape(h * i, h * o)


def kernel(x, c, ctx, c_ctx, w_mod, b_mod, g_norm_mix, g_norm_ffn, w_in, w_fourier, conv_w, conv_b,
           rg_w_a, rg_b_a, rg_w_x, rg_b_x, rg_lam, g_out_fourier, g_out_rg, w_out, w_router, b_router,
           w_gate_up, b_gate_up, w_down, b_down, g_final):
    assert w_mod.shape[0] == 1, "single-layer stack only"
    b, s, d = x.shape
    df = w_fourier.shape[1] * w_fourier.shape[2]
    dr = conv_w.shape[2]
    gd = w_fourier.shape[2]
    rows = s // GRID_W
    t = b * s
    ne = w_router.shape[2]

    mrows = -(-(b + 1) // SUBLANES) * SUBLANES
    cond = jnp.zeros((mrows, d), F32).at[:b].set(c).at[b].set(c_ctx)
    mod = _adaln(cond, w_mod[0], b_mod[0])
    sh_m, sc_m, gt_m, sh_f, sc_f, gt_f = [mod[:b, k * d:(k + 1) * d].reshape(b, 1, d) for k in range(N_MOD)]
    csh_m = mod[b:b + 1, 0:d].reshape(1, 1, d)
    csc_m = mod[b:b + 1, d:2 * d].reshape(1, 1, d)

    tm = min(s, TOKEN_TILE)
    d2, etab, cmat, smat = _dft_tables(rows, gd)
    jmat = jnp.asarray(np.eye(GRID_W)[::-1].copy(), BF16)

    w_in_bf = w_in[0].astype(BF16)
    f, xs, gg = _stage_b(x, sh_m, sc_m, g_norm_mix[0], w_in_bf, jmat, df, dr, tm=tm)
    xr_ctx = _stage_b_ctx(ctx, csh_m, csc_m, g_norm_mix[0], w_in_bf[:, df:df + dr])

    wcat = jnp.stack([jnp.concatenate([_block_diag(rg_w_a[0, dd]), _block_diag(rg_w_x[0, dd])], axis=1)
                      for dd in range(2)]).astype(BF16)
    bcat = jnp.concatenate([rg_b_a[0], rg_b_x[0]], axis=1).reshape(2, 1, 2 * dr)
    lam = rg_lam[0].reshape(2, 1, dr)
    h0 = jnp.zeros((b, SUBLANES, dr), F32)
    _, hfin_ctx = _rg_scan(xr_ctx, h0, conv_w[0], conv_b[0], wcat, bcat, lam, tc=ctx.shape[1])
    hs, _ = _rg_scan(xs, hfin_ctx, conv_w[0], conv_b[0], wcat, bcat, lam, tc=min(s, SCAN_CHUNK))

    cw, sw = _fold_fourier(cmat, smat, w_fourier[0])
    bdc = _block_diag(cw).astype(BF16)
    bds = _block_diag(sw).astype(BF16)
    y = _fourier_stage1(f.reshape(b, rows, GRID_W * df), d2, tl=min(GRID_W * df, 8192))
    fn = _fourier_stage2(y.reshape(b, 2, rows, GRID_W, df), etab, bdc, bds, g_out_fourier[0],
                         kb=min(rows, 16))
    fn = fn.reshape(b, s, df)

    x1, h2, idx, gates = _stage_m(fn, hs, gg, x, gt_m, sh_f, sc_f, g_out_rg[0], g_norm_ffn[0],
                                  w_out[0].astype(BF16), w_router[0].T, b_router[0], jmat, tm=tm)

    tl = min(t, RANK_TILE)
    tri = jnp.asarray(np.triu(np.ones((tl, tl))), BF16)
    rank, cnt = _dispatch_ranks(idx, tri, tl)
    counts = cnt[:, 0]
    tmm = MOE_ROW_TILE
    padded = (counts + tmm - 1) // tmm * tmm
    pad_end = jnp.cumsum(padded)
    pad_start = pad_end - padded
    eids = jnp.arange(ne, dtype=jnp.int32)
    dest = rank + jnp.sum(jnp.where(idx[:, :, None] == eids, pad_start, 0), axis=-1)
    n_blocks = -(-(t * TOP_K) // tmm) + ne
    cap = n_blocks * tmm
    n_used = (pad_end[-1] // tmm).astype(jnp.int32).reshape(1)
    blk_start = jnp.arange(n_blocks, dtype=jnp.int32) * tmm
    blk_expert = jnp.sum(blk_start[:, None] >= pad_end[None, :], axis=1).astype(jnp.int32)
    last_expert = jnp.sum(pad_end[-1] - tmm >= pad_end).astype(jnp.int32)
    blk_expert = jnp.minimum(blk_expert, last_expert)
    sel = blk_expert[:, None] == eids
    blk_first = jnp.sum(jnp.where(sel, pad_start, 0), axis=1)
    blk_count = jnp.sum(jnp.where(sel, counts, 0), axis=1)
    blk_valid = jnp.clip(blk_count - (blk_start - blk_first), 0, tmm).astype(jnp.int32)

    x_sorted = _sc_dispatch(h2.reshape(t, d // 2), dest.reshape(-1), cap)
    y_sorted = _moe_experts(blk_expert, blk_valid, n_used, x_sorted,
                            w_gate_up[0], b_gate_up[0], w_down[0], b_down[0], tmm)
    yk = _sc_gather(y_sorted, dest.reshape(-1)).reshape(TOP_K, b, s, d // 2)
    return _combine(x1, yk, gates.T.reshape(b, s, TOP_K), gt_f, g_final, tm=tm)
```

```python
import functools

import numpy as np
import jax
import jax.numpy as jnp
from jax import lax
from jax.experimental import pallas as pl
from jax.experimental.pallas import tpu as pltpu
from jax.experimental.pallas import tpu_sc as plsc

GRID_W = 64
FOURIER_GROUPS = 4
RG_HEADS = 8
CONV_W = 4
CONV_PAD_LO = 2
RG_C = 8.0
N_EXPERTS = 32
TOP_K = 4
SWIGLU_LIMIT = 7.0
SWIGLU_ALPHA = 1.702
N_MOD = 6
EPS = 1e-6

LANES = 128
SUBLANES = 8
VMEM_LIMIT_BYTES = 56 * 1024 * 1024
TOKEN_TILE = 1024
RANK_TILE = 512
SCAN_CHUNK = 512
MOE_ROW_TILE = 512
COMBINE_GROUPS = 4

F32 = jnp.float32
BF16 = jnp.bfloat16


def _cparams(sem):
    return pltpu.CompilerParams(dimension_semantics=sem, vmem_limit_bytes=VMEM_LIMIT_BYTES)


def _split_bf16(a):
    hi = a.astype(BF16)
    lo = (a - hi.astype(F32)).astype(BF16)
    return hi, lo


def _dot3(a, b):
    ah, al = _split_bf16(a)
    bh, bl = _split_bf16(b)
    out = jnp.dot(ah, bh, preferred_element_type=F32)
    out += jnp.dot(ah, bl, preferred_element_type=F32)
    out += jnp.dot(al, bh, preferred_element_type=F32)
    return out


def _dot3_nt(a, b):
    dn = (((1,), (1,)), ((), ()))
    ah, al = _split_bf16(a)
    bh, bl = _split_bf16(b)
    out = lax.dot_general(ah, bh, dn, preferred_element_type=F32)
    out += lax.dot_general(ah, bl, dn, preferred_element_type=F32)
    out += lax.dot_general(al, bh, dn, preferred_element_type=F32)
    return out


def _gelu_tanh(x):
    return 0.5 * x * (1.0 + jnp.tanh(0.7978845608028654 * (x + 0.044715 * (x * x * x))))


def _rms(x, g):
    return x * lax.rsqrt(jnp.mean(x * x, axis=-1, keepdims=True) + EPS) * g


def _pack_halves(v):
    h = v.shape[1] // 2
    hi = lax.bitcast_convert_type(v[:, :h].astype(BF16).astype(F32), jnp.uint32)
    lo = lax.bitcast_convert_type(v[:, h:].astype(BF16).astype(F32), jnp.uint32)
    return lax.bitcast_convert_type(hi | (lo >> 16), jnp.int32)


def _unpack_halves(w):
    u = lax.bitcast_convert_type(w, jnp.uint32)
    hi = lax.bitcast_convert_type(u & jnp.uint32(0xFFFF0000), F32)
    lo = lax.bitcast_convert_type(u << 16, F32)
    return hi, lo


def _adaln_kernel(c_ref, w_ref, b_ref, o_ref):
    s = c_ref[...]
    s = s * jax.nn.sigmoid(s)
    o_ref[...] = _dot3(s, w_ref[...]) + b_ref[...]


def _adaln(cond, w_mod, b_mod):
    m, d = cond.shape
    n = w_mod.shape[1]
    tn = n // N_MOD
    return pl.pallas_call(
        _adaln_kernel,
        grid=(n // tn,),
        in_specs=[pl.BlockSpec((m, d), lambda i: (0, 0)),
                  pl.BlockSpec((d, tn), lambda i: (0, i)),
                  pl.BlockSpec((1, tn), lambda i: (0, i))],
        out_specs=pl.BlockSpec((m, tn), lambda i: (0, i)),
        out_shape=jax.ShapeDtypeStruct((m, n), F32),
        compiler_params=_cparams(("arbitrary",)),
        name="adaln",
    )(cond, w_mod, b_mod.reshape(1, n))


def _fold_kernel(c_ref, s_ref, w_ref, cw_ref, sw_ref):
    w = w_ref[0]
    cw_ref[0] = _dot3(c_ref[...], w)
    sw_ref[0] = _dot3(s_ref[...], w)


def _fold_fourier(cmat, smat, w_f):
    g, gd, _ = w_f.shape
    spec_m = pl.BlockSpec((gd, gd), lambda i: (0, 0))
    spec_w = pl.BlockSpec((1, gd, gd), lambda i: (i, 0, 0))
    return pl.pallas_call(
        _fold_kernel,
        grid=(g,),
        in_specs=[spec_m, spec_m, spec_w],
        out_specs=[spec_w, spec_w],
        out_shape=[jax.ShapeDtypeStruct((g, gd, gd), F32)] * 2,
        compiler_params=_cparams(("arbitrary",)),
        name="fold_fourier",
    )(cmat, smat, w_f)


def _stage_b_kernel(x_ref, sh_ref, sc_ref, g_ref, w_ref, j_ref, f_ref, xs_ref, gg_ref, *, df, dr):
    h = _rms(x_ref[0], g_ref[...]) * (1.0 + sc_ref[0]) + sh_ref[0]
    hb = h.astype(BF16)
    f_ref[0] = jnp.dot(hb, w_ref[:, :df], preferred_element_type=F32).astype(BF16)
    xr = jnp.dot(hb, w_ref[:, df:df + dr], preferred_element_type=F32).astype(BF16)
    tm = xr.shape[0]
    for r in range(tm // GRID_W):
        blk = xr[r * GRID_W:(r + 1) * GRID_W]
        if r % 2 == 1:
            blk = jnp.dot(j_ref[...], blk, preferred_element_type=F32).astype(BF16)
        xs_ref[0, r * GRID_W:(r + 1) * GRID_W, :] = blk
    gr = jnp.dot(hb, w_ref[:, df + dr:], preferred_element_type=F32)
    gg_ref[0] = _gelu_tanh(gr).astype(BF16)


def _stage_b(x, shift, scale, g, w_in_bf, jmat, df, dr, tm):
    b, s, d = x.shape
    n = w_in_bf.shape[1]
    vec = pl.BlockSpec((1, 1, d), lambda i, t: (i, 0, 0))
    out = pl.BlockSpec((1, tm, df), lambda i, t: (i, t, 0))
    return pl.pallas_call(
        functools.partial(_stage_b_kernel, df=df, dr=dr),
        grid=(b, s // tm),
        in_specs=[pl.BlockSpec((1, tm, d), lambda i, t: (i, t, 0)), vec, vec,
                  pl.BlockSpec((1, d), lambda i, t: (0, 0)),
                  pl.BlockSpec((d, n), lambda i, t: (0, 0)),
                  pl.BlockSpec((GRID_W, GRID_W), lambda i, t: (0, 0))],
        out_specs=[out, out, out],
        out_shape=[jax.ShapeDtypeStruct((b, s, df), BF16)] * 3,
        compiler_params=_cparams(("parallel", "arbitrary")),
        name="stage_b",
    )(x, shift, scale, g.reshape(1, d), w_in_bf, jmat)


def _stage_b_ctx_kernel(x_ref, sh_ref, sc_ref, g_ref, w_ref, xr_ref):
    h = _rms(x_ref[0], g_ref[...]) * (1.0 + sc_ref[0]) + sh_ref[0]
    xr_ref[0] = jnp.dot(h.astype(BF16), w_ref[...], preferred_element_type=F32).astype(BF16)


def _stage_b_ctx(ctx, shift, scale, g, w_xr_bf):
    b, s, d = ctx.shape
    dr = w_xr_bf.shape[1]
    vec = pl.BlockSpec((1, 1, d), lambda i: (0, 0, 0))
    return pl.pallas_call(
        _stage_b_ctx_kernel,
        grid=(b,),
        in_specs=[pl.BlockSpec((1, s, d), lambda i: (i, 0, 0)), vec, vec,
                  pl.BlockSpec((1, d), lambda i: (0, 0)),
                  pl.BlockSpec((d, dr), lambda i: (0, 0))],
        out_specs=pl.BlockSpec((1, s, dr), lambda i: (i, 0, 0)),
        out_shape=jax.ShapeDtypeStruct((b, s, dr), BF16),
        compiler_params=_cparams(("arbitrary",)),
        name="stage_b_ctx",
    )(ctx, shift, scale, g.reshape(1, d), w_xr_bf)


HALO = 16


def _seg_pitch(seg_len):
    n8 = seg_len // SUBLANES
    return SUBLANES * (n8 + 1 - n8 % 2)


def _sigmoid(x):
    return 0.5 * jnp.tanh(0.5 * x) + 0.5


def _rg_kernel(xs_ref, h0_ref, cw_ref, cb_ref, w_ref, b_ref, lam_ref, out_ref, hfin_ref,
               hf_s, xc_s, a_s, u_s, hl_s, p_s, hc_s, *, tc, nchunk, seq, dr):
    p = pl.program_id(1)
    j = pl.program_id(2)
    cidx = jnp.where(p == 0, j, nchunk - 1 - j)
    start = pl.multiple_of(cidx * tc, tc)
    nseg = SUBLANES
    sl = tc // nseg
    pitch = _seg_pitch(sl)
    nslab = dr // LANES

    @pl.when(p == 0)
    def _():
        cur = xs_ref[0, pl.ds(start, tc), :].astype(F32)
        pstart = pl.multiple_of(jnp.maximum(start - HALO, 0), HALO)
        nstart = pl.multiple_of(jnp.minimum(start + tc, seq - HALO), HALO)
        prev = xs_ref[0, pl.ds(pstart, HALO), :].astype(F32)
        nxt = xs_ref[0, pl.ds(nstart, HALO), :].astype(F32)
        prev = jnp.where(cidx > 0, prev, 0.0)
        nxt = jnp.where(cidx < nchunk - 1, nxt, 0.0)
        ext = jnp.concatenate([prev, cur, nxt], axis=0)
        xc = cb_ref[...] + cw_ref[0:1, :] * ext[HALO - CONV_PAD_LO:HALO - CONV_PAD_LO + tc]
        for k in range(1, CONV_W):
            o = HALO - CONV_PAD_LO + k
            xc = xc + cw_ref[k:k + 1, :] * ext[o:o + tc]
        xc_s[pl.ds(start, tc), :] = xc

    xc = xc_s[pl.ds(start, tc), :]
    gates = jnp.dot(xc.astype(BF16), w_ref[p], preferred_element_type=F32) + b_ref[p]
    r = _sigmoid(gates[:, :dr])
    i = _sigmoid(gates[:, dr:])
    log_a = (-RG_C) * r * jax.nn.softplus(-lam_ref[p])
    a = jnp.exp(log_a)
    u = jnp.sqrt(-jnp.tanh(log_a) * (1.0 + a * a)) * (i * xc)
    for g in range(nseg):
        for c in range(nslab):
            a_s[c, g * pitch:g * pitch + sl, :] = a[g * sl:(g + 1) * sl, c * LANES:(c + 1) * LANES]
            u_s[c, g * pitch:g * pitch + sl, :] = u[g * sl:(g + 1) * sl, c * LANES:(c + 1) * LANES]

    @pl.when(jnp.logical_and(p == 0, j == 0))
    def _():
        hfin_ref[...] = jnp.zeros_like(hfin_ref)

    @pl.when(j == 0)
    def _():
        hc_s[0:1, :] = h0_ref[0, pl.ds(p, 1), :]

    def local_scan(reverse):
        def body(q, carry):
            t = (sl - 1 - q) if reverse else q
            rows = pl.ds(t, nseg, stride=pitch)
            hs, ps = carry
            nh, npr = [], []
            for c in range(nslab):
                av = a_s[c, rows, :]
                h = av * hs[c] + u_s[c, rows, :]
                pr = av * ps[c]
                hl_s[c, rows, :] = h
                p_s[c, rows, :] = pr
                nh.append(h)
                npr.append(pr)
            return tuple(nh), tuple(npr)
        zero = tuple(jnp.zeros((nseg, LANES), F32) for _ in range(nslab))
        one = tuple(jnp.ones((nseg, LANES), F32) for _ in range(nslab))
        lax.fori_loop(0, sl, body, (zero, one), unroll=4)

    @pl.when(p == 0)
    def _():
        local_scan(False)
        for c in range(nslab):
            lanes = slice(c * LANES, (c + 1) * LANES)
            carry = hc_s[0:1, lanes]
            for g in range(nseg):
                seg = slice(g * pitch, g * pitch + sl)
                h = hl_s[c, seg, :] + p_s[c, seg, :] * carry
                hf_s[pl.ds(start + g * sl, sl), lanes] = h
                carry = h[sl - 1:sl, :]
            hc_s[0:1, lanes] = carry

    @pl.when(p == 1)
    def _():
        local_scan(True)
        for c in range(nslab):
            lanes = slice(c * LANES, (c + 1) * LANES)
            carry = hc_s[0:1, lanes]
            for g in range(nseg - 1, -1, -1):
                seg = slice(g * pitch, g * pitch + sl)
                h = hl_s[c, seg, :] + p_s[c, seg, :] * carry
                out_ref[0, g * sl:(g + 1) * sl, lanes] = (h + hf_s[pl.ds(start + g * sl, sl), lanes]).astype(BF16)
                carry = h[0:1, :]
            hc_s[0:1, lanes] = carry

    @pl.when(j == nchunk - 1)
    def _():
        hfin_ref[0, pl.ds(p, 1), :] = hc_s[0:1, :]


def _rg_scan(xs, h0, conv_w, conv_b, wcat, bcat, lam, tc):
    b, s, dr = xs.shape
    nchunk = s // tc
    last = nchunk - 1
    seg_rows = SUBLANES * _seg_pitch(tc // SUBLANES)
    seg_buf = pltpu.VMEM((dr // LANES, seg_rows, LANES), F32)
    full2 = lambda shape: pl.BlockSpec(shape, lambda i, p, j: (0,) * len(shape))
    return pl.pallas_call(
        functools.partial(_rg_kernel, tc=tc, nchunk=nchunk, seq=s, dr=dr),
        grid=(b, 2, nchunk),
        in_specs=[pl.BlockSpec((1, s, dr), lambda i, p, j: (i, 0, 0)),
                  pl.BlockSpec((1, SUBLANES, dr), lambda i, p, j: (i, 0, 0)),
                  full2((CONV_W, dr)), full2((1, dr)),
                  full2((2, dr, 2 * dr)), full2((2, 1, 2 * dr)), full2((2, 1, dr))],
        out_specs=[pl.BlockSpec((1, tc, dr), lambda i, p, j: (i, jnp.where(p == 0, last, last - j), 0)),
                   pl.BlockSpec((1, SUBLANES, dr), lambda i, p, j: (i, 0, 0))],
        out_shape=[jax.ShapeDtypeStruct((b, s, dr), BF16),
                   jax.ShapeDtypeStruct((b, SUBLANES, dr), F32)],
        scratch_shapes=[pltpu.VMEM((s, dr), F32), pltpu.VMEM((s, dr), F32),
                        seg_buf, seg_buf, seg_buf, seg_buf, pltpu.VMEM((SUBLANES, dr), F32)],
        compiler_params=_cparams(("arbitrary", "arbitrary", "arbitrary")),
        name="rg_scan",
    )(xs, h0, conv_w, conv_b.reshape(1, dr), wcat, bcat, lam)


def _f1_kernel(d_ref, x_ref, y_ref):
    y_ref[0] = jnp.dot(d_ref[...], x_ref[0], preferred_element_type=F32).astype(BF16)


def _fourier_stage1(fv, d2, tl):
    b, r, n = fv.shape
    return pl.pallas_call(
        _f1_kernel,
        grid=(b, n // tl),
        in_specs=[pl.BlockSpec((2 * r, r), lambda i, l: (0, 0)),
                  pl.BlockSpec((1, r, tl), lambda i, l: (i, 0, l))],
        out_specs=pl.BlockSpec((1, 2 * r, tl), lambda i, l: (i, 0, l)),
        out_shape=jax.ShapeDtypeStruct((b, 2 * r, n), BF16),
        compiler_params=_cparams(("parallel", "arbitrary")),
        name="fourier_stage1",
    )(d2, fv)


def _f2_kernel(y_ref, e_ref, bc_ref, bs_ref, g_ref, o_ref, *, kb, df):
    zr, zi = [], []
    for q in range(kb):
        yk = jnp.concatenate([y_ref[0, 0, q], y_ref[0, 1, q]], axis=0)
        z = jnp.dot(e_ref[q], yk, preferred_element_type=F32)
        zr.append(z[:GRID_W])
        zi.append(z[GRID_W:])
    zr = jnp.concatenate(zr, axis=0).astype(BF16)
    zi = jnp.concatenate(zi, axis=0).astype(BF16)
    o = jnp.dot(zr, bc_ref[...], preferred_element_type=F32)
    o += jnp.dot(zi, bs_ref[...], preferred_element_type=F32)
    on = _rms(o, g_ref[...]).astype(BF16)
    for q in range(kb):
        o_ref[0, :, q * df:(q + 1) * df] = on[q * GRID_W:(q + 1) * GRID_W]


def _fourier_stage2(y5, etab, bdc, bds, g, kb):
    b, _, r, w, df = y5.shape
    return pl.pallas_call(
        functools.partial(_f2_kernel, kb=kb, df=df),
        grid=(b, r // kb),
        in_specs=[pl.BlockSpec((1, 2, kb, w, df), lambda i, k: (i, 0, k, 0, 0)),
                  pl.BlockSpec((kb, 2 * w, 2 * w), lambda i, k: (k, 0, 0)),
                  pl.BlockSpec((df, df), lambda i, k: (0, 0)),
                  pl.BlockSpec((df, df), lambda i, k: (0, 0)),
                  pl.BlockSpec((1, df), lambda i, k: (0, 0))],
        out_specs=pl.BlockSpec((1, w, kb * df), lambda i, k: (i, 0, k)),
        out_shape=jax.ShapeDtypeStruct((b, w, r * df), BF16),
        compiler_params=_cparams(("parallel", "arbitrary")),
        name="fourier_stage2",
    )(y5, etab, bdc, bds, g.reshape(1, df))


def _stage_m_kernel(fn_ref, hs_ref, gg_ref, x_ref, gtm_ref, shf_ref, scf_ref, gr_ref, gffn_ref,
                    wo_ref, wr_ref, br_ref, j_ref, x1_ref, h2_ref, idx_ref, gate_ref, *, df):
    tm = x_ref.shape[1]
    hs = hs_ref[0]
    blocks = []
    for r in range(tm // GRID_W):
        blk = hs[r * GRID_W:(r + 1) * GRID_W]
        if r % 2 == 1:
            blk = jnp.dot(j_ref[...], blk, preferred_element_type=F32)
        blocks.append(blk.astype(F32))
    rg = jnp.concatenate(blocks, axis=0) * gg_ref[0].astype(F32)
    rgn = _rms(rg, gr_ref[...]).astype(BF16)
    mix = jnp.dot(fn_ref[0], wo_ref[:df, :], preferred_element_type=F32)
    mix += jnp.dot(rgn, wo_ref[df:, :], preferred_element_type=F32)
    x1 = x_ref[0] + gtm_ref[0] * mix
    x1_ref[0] = x1
    h2 = _rms(x1, gffn_ref[...]) * (1.0 + scf_ref[0]) + shf_ref[0]
    h2_ref[0] = _pack_halves(h2)

    logits = _dot3_nt(wr_ref[...], h2) + br_ref[...]
    eidx = lax.broadcasted_iota(jnp.int32, logits.shape, 0)
    vals, idxs = [], []
    for _ in range(TOP_K):
        m = jnp.max(logits, axis=0, keepdims=True)
        sel = jnp.min(jnp.where(logits == m, eidx, N_EXPERTS), axis=0, keepdims=True)
        vals.append(m)
        idxs.append(sel)
        logits = jnp.where(eidx == sel, -jnp.inf, logits)
    ex = [jnp.exp(v - vals[0]) for v in vals]
    den = ex[0] + ex[1] + ex[2] + ex[3]
    for k in range(TOP_K):
        gate_ref[k:k + 1, :] = ex[k] / den
        idx_ref[k:k + 1, :] = idxs[k]


def _stage_m(fn, hs, gg, x, gt_m, sh_f, sc_f, g_out_r, g_ffn, w_out_bf, w_router_t, b_router, jmat, tm):
    b, s, d = x.shape
    df = fn.shape[2]
    dr = hs.shape[2]
    nt = s // tm
    ne = w_router_t.shape[0]
    vec = pl.BlockSpec((1, 1, d), lambda i, t: (i, 0, 0))
    half = lambda dd: pl.BlockSpec((1, tm, dd), lambda i, t: (i, t, 0))
    full = lambda shape: pl.BlockSpec(shape, lambda i, t: (0,) * len(shape))
    tok = pl.BlockSpec((TOP_K, tm), lambda i, t: (0, i * nt + t))
    return pl.pallas_call(
        functools.partial(_stage_m_kernel, df=df),
        grid=(b, nt),
        in_specs=[half(df), half(dr), half(dr), half(d), vec, vec, vec,
                  full((1, dr)), full((1, d)), full((d, d)), full((ne, d)), full((ne, 1)),
                  full((GRID_W, GRID_W))],
        out_specs=[half(d), half(d // 2), tok, tok],
        out_shape=[jax.ShapeDtypeStruct((b, s, d), F32), jax.ShapeDtypeStruct((b, s, d // 2), jnp.int32),
                   jax.ShapeDtypeStruct((TOP_K, b * s), jnp.int32),
                   jax.ShapeDtypeStruct((TOP_K, b * s), F32)],
        compiler_params=_cparams(("parallel", "arbitrary")),
        name="stage_m",
    )(fn, hs, gg, x, gt_m, sh_f, sc_f, g_out_r.reshape(1, dr), g_ffn.reshape(1, d), w_out_bf,
      w_router_t, b_router.reshape(ne, 1), jmat)


def _rank_kernel(idx_ref, tri_ref, rank_ref, cnt_ref, carry_s):
    c = pl.program_id(0)

    @pl.when(c == 0)
    def _():
        carry_s[...] = jnp.zeros_like(carry_s)

    l = idx_ref.shape[1]
    eidx = lax.broadcasted_iota(jnp.int32, (N_EXPERTS, l), 0)
    for k in range(TOP_K):
        onehot = eidx == idx_ref[k:k + 1, :]
        oh = jnp.where(onehot, 1.0, 0.0)
        prefix = jnp.dot(oh.astype(BF16), tri_ref[...], preferred_element_type=F32)
        carry = carry_s[:, 0:1]
        rank = jnp.sum(jnp.where(onehot, prefix - 1.0 + carry, 0.0), axis=0, keepdims=True)
        rank_ref[k:k + 1, :] = rank.astype(jnp.int32)
        carry_s[...] = carry_s[...] + jnp.sum(oh, axis=1, keepdims=True)
    cnt_ref[...] = carry_s[...].astype(jnp.int32)


def _dispatch_ranks(idx, tri, tl):
    k, t = idx.shape
    return pl.pallas_call(
        _rank_kernel,
        grid=(t // tl,),
        in_specs=[pl.BlockSpec((k, tl), lambda c: (0, c)),
                  pl.BlockSpec((tl, tl), lambda c: (0, 0))],
        out_specs=[pl.BlockSpec((k, tl), lambda c: (0, c)),
                   pl.BlockSpec((N_EXPERTS, LANES), lambda c: (0, 0))],
        out_shape=[jax.ShapeDtypeStruct((k, t), jnp.int32),
                   jax.ShapeDtypeStruct((N_EXPERTS, LANES), jnp.int32)],
        scratch_shapes=[pltpu.VMEM((N_EXPERTS, LANES), F32)],
        compiler_params=_cparams(("arbitrary",)),
        name="dispatch_ranks",
    )(idx, tri)


def _moe_kernel(be_ref, bv_ref, nu_ref, x_ref, wgu_ref, bgu_ref, wd_ref, bd_ref, o_ref, wgu_s, wd_s, *, dff):
    i = pl.program_id(0)
    h = x_ref.shape[1]

    @pl.when(jnp.logical_or(i == 0, be_ref[i] != be_ref[jnp.maximum(i - 1, 0)]))
    def _():
        wgu_s[...] = wgu_ref[0].astype(BF16)
        wd_s[...] = wd_ref[0].astype(BF16)

    @pl.when(i < nu_ref[0])
    def _():
        rows = lax.broadcasted_iota(jnp.int32, x_ref.shape, 0)
        xw = jnp.where(rows < bv_ref[i], x_ref[...], 0)
        xa, xb = _unpack_halves(xw)
        gu = jnp.dot(xa.astype(BF16), wgu_s[:h, :], preferred_element_type=F32)
        gu += jnp.dot(xb.astype(BF16), wgu_s[h:, :], preferred_element_type=F32)
        gu += bgu_ref[0]
        gt = jnp.minimum(gu[:, :dff], SWIGLU_LIMIT)
        up = jnp.clip(gu[:, dff:], -SWIGLU_LIMIT, SWIGLU_LIMIT)
        act = (up + 1.0) * (gt * _sigmoid(SWIGLU_ALPHA * gt))
        out = jnp.dot(act.astype(BF16), wd_s[...], preferred_element_type=F32) + bd_ref[0]
        o_ref[...] = _pack_halves(out)


def _moe_experts(blk_expert, blk_valid, n_used, xs, wgu, bgu, wd, bd, tmm):
    cap, h = xs.shape
    ne, d, dff2 = wgu.shape
    dff = dff2 // 2
    row_blk = lambda i, be, bv, nu: (jnp.minimum(i, nu[0] - 1), 0)
    wsel = lambda i, be, bv, nu: (be[i], 0, 0)
    grid_spec = pltpu.PrefetchScalarGridSpec(
        num_scalar_prefetch=3,
        grid=(cap // tmm,),
        in_specs=[pl.BlockSpec((tmm, h), row_blk),
                  pl.BlockSpec((1, d, dff2), wsel),
                  pl.BlockSpec((1, 1, dff2), wsel),
                  pl.BlockSpec((1, dff, d), wsel),
                  pl.BlockSpec((1, 1, d), wsel)],
        out_specs=pl.BlockSpec((tmm, h), row_blk),
        scratch_shapes=[pltpu.VMEM((d, dff2), BF16), pltpu.VMEM((dff, d), BF16)],
    )
    return pl.pallas_call(
        functools.partial(_moe_kernel, dff=dff),
        grid_spec=grid_spec,
        out_shape=jax.ShapeDtypeStruct((cap, h), jnp.int32),
        compiler_params=_cparams(("arbitrary",)),
        name="moe_experts",
    )(blk_expert, blk_valid, n_used, xs, wgu, bgu.reshape(ne, 1, dff2), wd, bd.reshape(ne, 1, d))


SC_CHUNK = 64


def _sc_workers():
    info = plsc.get_sparse_core_info()
    return info.num_cores, info.num_subcores


def _sc_dispatch(rows, dest, cap):
    n, w = rows.shape
    topk = dest.shape[0] // n
    nc, ns = _sc_workers()
    per_w = n // (nc * ns)
    assert per_w % SC_CHUNK == 0
    mesh = plsc.VectorSubcoreMesh(core_axis_name="c", subcore_axis_name="s")

    @functools.partial(
        pl.kernel, mesh=mesh,
        out_type=jax.ShapeDtypeStruct((cap, w), jnp.int32),
        scratch_types=[pltpu.VMEM((SC_CHUNK,), jnp.int32),
                       pltpu.VMEM((SC_CHUNK, w), jnp.int32),
                       pltpu.SemaphoreType.DMA],
    )
    def scatter_rows(rows_hbm, dest_hbm, out_hbm, idx_v, rows_v, sem):
        base = (lax.axis_index("s") * nc + lax.axis_index("c")) * per_w

        @pl.loop(0, per_w // SC_CHUNK)
        def _(j):
            off = pl.multiple_of(base + j * SC_CHUNK, SC_CHUNK)
            pltpu.sync_copy(rows_hbm.at[pl.ds(off, SC_CHUNK)], rows_v)
            for k in range(topk):
                pltpu.sync_copy(dest_hbm.at[pl.ds(k * n + off, SC_CHUNK)], idx_v)
                pltpu.async_copy(rows_v, out_hbm.at[idx_v], sem).wait()

    return scatter_rows(rows, dest)


def _sc_gather(table, idx):
    n = idx.shape[0]
    w = table.shape[1]
    nc, ns = _sc_workers()
    per_w = n // (nc * ns)
    assert per_w % SC_CHUNK == 0
    mesh = plsc.VectorSubcoreMesh(core_axis_name="c", subcore_axis_name="s")

    @functools.partial(
        pl.kernel, mesh=mesh,
        out_type=jax.ShapeDtypeStruct((n, w), jnp.int32),
        scratch_types=[pltpu.VMEM((SC_CHUNK,), jnp.int32),
                       pltpu.VMEM((SC_CHUNK, w), jnp.int32),
                       pltpu.SemaphoreType.DMA],
    )
    def gather_rows(table_hbm, idx_hbm, out_hbm, idx_v, rows_v, sem):
        base = (lax.axis_index("s") * nc + lax.axis_index("c")) * per_w

        @pl.loop(0, per_w // SC_CHUNK)
        def _(j):
            off = pl.multiple_of(base + j * SC_CHUNK, SC_CHUNK)
            pltpu.sync_copy(idx_hbm.at[pl.ds(off, SC_CHUNK)], idx_v)
            pltpu.async_copy(table_hbm.at[idx_v], rows_v, sem).wait()
            pltpu.sync_copy(rows_v, out_hbm.at[pl.ds(off, SC_CHUNK)])

    return gather_rows(table, idx)


def _combine_kernel(x1_ref, y_ref, gate_ref, gtf_ref, g_ref, *rest):
    o_ref = rest[-1]
    h = y_ref.shape[3]
    gates = gate_ref[0]
    moe_a = moe_b = None
    for k in range(TOP_K):
        ya, yb = _unpack_halves(y_ref[k, 0])
        gk = gates[:, k:k + 1]
        moe_a = gk * ya if k == 0 else moe_a + gk * ya
        moe_b = gk * yb if k == 0 else moe_b + gk * yb
    za = x1_ref[0, :, :h] + gtf_ref[0, :, :h] * moe_a
    zb = x1_ref[0, :, h:] + gtf_ref[0, :, h:] * moe_b
    ms = (jnp.sum(za * za, axis=-1, keepdims=True) + jnp.sum(zb * zb, axis=-1, keepdims=True)) / (2 * h)
    inv = lax.rsqrt(ms + EPS)
    o_ref[0, :, :h] = za * inv * g_ref[:, :h]
    o_ref[0, :, h:] = zb * inv * g_ref[:, h:]


def _combine(x1, yk, gates_t, gt_f, g_final, tm, b0, prev_out):
    b, s, d = x1.shape
    _, bp, _, h = yk.shape
    in_specs = [pl.BlockSpec((1, tm, d), lambda i, t: (i + b0, t, 0)),
                pl.BlockSpec((TOP_K, 1, tm, h), lambda i, t: (0, i, t, 0)),
                pl.BlockSpec((1, tm, TOP_K), lambda i, t: (i + b0, t, 0)),
                pl.BlockSpec((1, 1, d), lambda i, t: (i + b0, 0, 0)),
                pl.BlockSpec((1, d), lambda i, t: (0, 0))]
    args = [x1, yk, gates_t, gt_f, g_final.reshape(1, d)]
    aliases = {}
    if prev_out is not None:
        in_specs.append(pl.BlockSpec(memory_space=pl.ANY))
        args.append(prev_out)
        aliases = {len(args) - 1: 0}
    return pl.pallas_call(
        _combine_kernel,
        grid=(bp, s // tm),
        in_specs=in_specs,
        out_specs=pl.BlockSpec((1, tm, d), lambda i, t: (i + b0, t, 0)),
        out_shape=jax.ShapeDtypeStruct((b, s, d), F32),
        input_output_aliases=aliases,
        compiler_params=_cparams(("parallel", "arbitrary")),
        name="combine",
    )(*args)


def _dft_tables(rows, gd):
    seq = rows * GRID_W
    n = np.arange(rows)
    ang1 = 2.0 * np.pi * np.outer(n, n) / rows
    d2 = np.concatenate([np.cos(ang1), -np.sin(ang1)], axis=0)
    k1 = np.arange(rows)[:, None, None]
    k2 = np.arange(GRID_W)[None, :, None]
    n2 = np.arange(GRID_W)[None, None, :]
    ang2 = 2.0 * np.pi * ((n2 * (k1 + rows * k2)) % seq) / seq
    ec, es = np.cos(ang2), np.sin(ang2)
    etab = np.concatenate([np.concatenate([ec, es], axis=2),
                           np.concatenate([-es, ec], axis=2)], axis=1)
    c = np.arange(gd)
    angc = 2.0 * np.pi * np.outer(c, c) / gd
    scale = 1.0 / np.sqrt(seq * gd)
    return (jnp.asarray(d2, BF16), jnp.asarray(etab, BF16),
            jnp.asarray(np.cos(angc) * scale, F32), jnp.asarray(np.sin(angc) * scale, F32))


def _block_diag(w):
    h, i, o = w.shape
    eye = jnp.eye(h, dtype=w.dtype)
    return (eye[:, None, :, None] * w[:, :, None, :]).reshape(h * i, h * o)


def kernel(x, c, ctx, c_ctx, w_mod, b_mod, g_norm_mix, g_norm_ffn, w_in, w_fourier, conv_w, conv_b,
           rg_w_a, rg_b_a, rg_w_x, rg_b_x, rg_lam, g_out_fourier, g_out_rg, w_out, w_router, b_router,
           w_gate_up, b_gate_up, w_down, b_down, g_final):
    assert w_mod.shape[0] == 1, "single-layer stack only"
    b, s, d = x.shape
    df = w_fourier.shape[1] * w_fourier.shape[2]
    dr = conv_w.shape[2]
    gd = w_fourier.shape[2]
    rows = s // GRID_W
    t = b * s
    ne = w_router.shape[2]

    mrows = -(-(b + 1) // SUBLANES) * SUBLANES
    cond = jnp.zeros((mrows, d), F32).at[:b].set(c).at[b].set(c_ctx)
    mod = _adaln(cond, w_mod[0], b_mod[0])
    sh_m, sc_m, gt_m, sh_f, sc_f, gt_f = [mod[:b, k * d:(k + 1) * d].reshape(b, 1, d) for k in range(N_MOD)]
    csh_m = mod[b:b + 1, 0:d].reshape(1, 1, d)
    csc_m = mod[b:b + 1, d:2 * d].reshape(1, 1, d)

    tm = min(s, TOKEN_TILE)
    d2, etab, cmat, smat = _dft_tables(rows, gd)
    jmat = jnp.asarray(np.eye(GRID_W)[::-1].copy(), BF16)

    w_in_bf = w_in[0].astype(BF16)
    f, xs, gg = _stage_b(x, sh_m, sc_m, g_norm_mix[0], w_in_bf, jmat, df, dr, tm=tm)
    xr_ctx = _stage_b_ctx(ctx, csh_m, csc_m, g_norm_mix[0], w_in_bf[:, df:df + dr])

    wcat = jnp.stack([jnp.concatenate([_block_diag(rg_w_a[0, dd]), _block_diag(rg_w_x[0, dd])], axis=1)
                      for dd in range(2)]).astype(BF16)
    bcat = jnp.concatenate([rg_b_a[0], rg_b_x[0]], axis=1).reshape(2, 1, 2 * dr)
    lam = rg_lam[0].reshape(2, 1, dr)
    h0 = jnp.zeros((b, SUBLANES, dr), F32)
    _, hfin_ctx = _rg_scan(xr_ctx, h0, conv_w[0], conv_b[0], wcat, bcat, lam, tc=ctx.shape[1])
    hs, _ = _rg_scan(xs, hfin_ctx, conv_w[0], conv_b[0], wcat, bcat, lam, tc=min(s, SCAN_CHUNK))

    cw, sw = _fold_fourier(cmat, smat, w_fourier[0])
    bdc = _block_diag(cw).astype(BF16)
    bds = _block_diag(sw).astype(BF16)
    y = _fourier_stage1(f.reshape(b, rows, GRID_W * df), d2, tl=min(GRID_W * df, 8192))
    fn = _fourier_stage2(y.reshape(b, 2, rows, GRID_W, df), etab, bdc, bds, g_out_fourier[0],
                         kb=min(rows, 16))
    fn = fn.reshape(b, s, df)

    x1, h2, idx, gates = _stage_m(fn, hs, gg, x, gt_m, sh_f, sc_f, g_out_rg[0], g_norm_ffn[0],
                                  w_out[0].astype(BF16), w_router[0].T, b_router[0], jmat, tm=tm)

    tl = min(t, RANK_TILE)
    tri = jnp.asarray(np.triu(np.ones((tl, tl))), BF16)
    rank, cnt = _dispatch_ranks(idx, tri, tl)
    counts = cnt[:, 0]
    tmm = MOE_ROW_TILE
    padded = (counts + tmm - 1) // tmm * tmm
    pad_end = jnp.cumsum(padded)
    pad_start = pad_end - padded
    eids = jnp.arange(ne, dtype=jnp.int32)
    dest = rank + jnp.sum(jnp.where(idx[:, :, None] == eids, pad_start, 0), axis=-1)
    n_blocks = -(-(t * TOP_K) // tmm) + ne
    cap = n_blocks * tmm
    n_used = (pad_end[-1] // tmm).astype(jnp.int32).reshape(1)
    blk_start = jnp.arange(n_blocks, dtype=jnp.int32) * tmm
    blk_expert = jnp.sum(blk_start[:, None] >= pad_end[None, :], axis=1).astype(jnp.int32)
    last_expert = jnp.sum(pad_end[-1] - tmm >= pad_end).astype(jnp.int32)
    blk_expert = jnp.minimum(blk_expert, last_expert)
    sel = blk_expert[:, None] == eids
    blk_first = jnp.sum(jnp.where(sel, pad_start, 0), axis=1)
    blk_count = jnp.sum(jnp.where(sel, counts, 0), axis=1)
    blk_valid = jnp.clip(blk_count - (blk_start - blk_first), 0, tmm).astype(jnp.int32)

    x_sorted = _sc_dispatch(h2.reshape(t, d // 2), dest.reshape(-1), cap)
    y_sorted = _moe_experts(blk_expert, blk_valid, n_used, x_sorted,
                            w_gate_up[0], b_gate_up[0], w_down[0], b_down[0], tmm)
    gates_t = gates.T.reshape(b, s, TOP_K)
    dest_b = dest.reshape(TOP_K, b, s)
    bp = b // COMBINE_GROUPS if b % COMBINE_GROUPS == 0 else b
    out = None
    for b0 in range(0, b, bp):
        yk = _sc_gather(y_sorted, dest_b[:, b0:b0 + bp].reshape(-1)).reshape(TOP_K, bp, s, d // 2)
        out = _combine(x1, yk, gates_t, gt_f, g_final, tm, b0, out)
    return out
```

```python
import functools

import numpy as np
import jax
import jax.numpy as jnp
from jax import lax
from jax.experimental import pallas as pl
from jax.experimental.pallas import tpu as pltpu
from jax.experimental.pallas import tpu_sc as plsc

GRID_W = 64
FOURIER_GROUPS = 4
RG_HEADS = 8
CONV_W = 4
CONV_PAD_LO = 2
RG_C = 8.0
N_EXPERTS = 32
TOP_K = 4
SWIGLU_LIMIT = 7.0
SWIGLU_ALPHA = 1.702
N_MOD = 6
EPS = 1e-6

LANES = 128
SUBLANES = 8
VMEM_LIMIT_BYTES = 56 * 1024 * 1024
TOKEN_TILE = 1024
RANK_TILE = 512
SCAN_CHUNK = 512
MOE_ROW_TILE = 512
COMBINE_GROUPS = 4

F32 = jnp.float32
BF16 = jnp.bfloat16


def _cparams(sem):
    return pltpu.CompilerParams(dimension_semantics=sem, vmem_limit_bytes=VMEM_LIMIT_BYTES)


def _split_bf16(a):
    hi = a.astype(BF16)
    lo = (a - hi.astype(F32)).astype(BF16)
    return hi, lo


def _dot3(a, b):
    ah, al = _split_bf16(a)
    bh, bl = _split_bf16(b)
    out = jnp.dot(ah, bh, preferred_element_type=F32)
    out += jnp.dot(ah, bl, preferred_element_type=F32)
    out += jnp.dot(al, bh, preferred_element_type=F32)
    return out


def _dot3_nt(a, b):
    dn = (((1,), (1,)), ((), ()))
    ah, al = _split_bf16(a)
    bh, bl = _split_bf16(b)
    out = lax.dot_general(ah, bh, dn, preferred_element_type=F32)
    out += lax.dot_general(ah, bl, dn, preferred_element_type=F32)
    out += lax.dot_general(al, bh, dn, preferred_element_type=F32)
    return out


def _gelu_tanh(x):
    return 0.5 * x * (1.0 + jnp.tanh(0.7978845608028654 * (x + 0.044715 * (x * x * x))))


def _rms(x, g):
    return x * lax.rsqrt(jnp.mean(x * x, axis=-1, keepdims=True) + EPS) * g


def _pack_halves(v):
    h = v.shape[1] // 2
    hi = lax.bitcast_convert_type(v[:, :h].astype(BF16).astype(F32), jnp.uint32)
    lo = lax.bitcast_convert_type(v[:, h:].astype(BF16).astype(F32), jnp.uint32)
    return lax.bitcast_convert_type(hi | (lo >> 16), jnp.int32)


def _unpack_halves(w):
    u = lax.bitcast_convert_type(w, jnp.uint32)
    hi = lax.bitcast_convert_type(u & jnp.uint32(0xFFFF0000), F32)
    lo = lax.bitcast_convert_type(u << 16, F32)
    return hi, lo


def _adaln_kernel(c_ref, w_ref, b_ref, o_ref):
    s = c_ref[...]
    s = s * jax.nn.sigmoid(s)
    o_ref[...] = _dot3(s, w_ref[...]) + b_ref[...]


def _adaln(cond, w_mod, b_mod):
    m, d = cond.shape
    n = w_mod.shape[1]
    tn = n // N_MOD
    return pl.pallas_call(
        _adaln_kernel,
        grid=(n // tn,),
        in_specs=[pl.BlockSpec((m, d), lambda i: (0, 0)),
                  pl.BlockSpec((d, tn), lambda i: (0, i)),
                  pl.BlockSpec((1, tn), lambda i: (0, i))],
        out_specs=pl.BlockSpec((m, tn), lambda i: (0, i)),
        out_shape=jax.ShapeDtypeStruct((m, n), F32),
        compiler_params=_cparams(("arbitrary",)),
        name="adaln",
    )(cond, w_mod, b_mod.reshape(1, n))


def _fold_kernel(c_ref, s_ref, w_ref, cw_ref, sw_ref):
    w = w_ref[0]
    cw_ref[0] = _dot3(c_ref[...], w)
    sw_ref[0] = _dot3(s_ref[...], w)


def _fold_fourier(cmat, smat, w_f):
    g, gd, _ = w_f.shape
    spec_m = pl.BlockSpec((gd, gd), lambda i: (0, 0))
    spec_w = pl.BlockSpec((1, gd, gd), lambda i: (i, 0, 0))
    return pl.pallas_call(
        _fold_kernel,
        grid=(g,),
        in_specs=[spec_m, spec_m, spec_w],
        out_specs=[spec_w, spec_w],
        out_shape=[jax.ShapeDtypeStruct((g, gd, gd), F32)] * 2,
        compiler_params=_cparams(("arbitrary",)),
        name="fold_fourier",
    )(cmat, smat, w_f)


def _stage_b_kernel(x_ref, sh_ref, sc_ref, g_ref, w_ref, j_ref, f_ref, xs_ref, gg_ref, *, df, dr):
    h = _rms(x_ref[0], g_ref[...]) * (1.0 + sc_ref[0]) + sh_ref[0]
    hb = h.astype(BF16)
    f_ref[0] = jnp.dot(hb, w_ref[:, :df], preferred_element_type=F32).astype(BF16)
    xr = jnp.dot(hb, w_ref[:, df:df + dr], preferred_element_type=F32).astype(BF16)
    tm = xr.shape[0]
    for r in range(tm // GRID_W):
        blk = xr[r * GRID_W:(r + 1) * GRID_W]
        if r % 2 == 1:
            blk = jnp.dot(j_ref[...], blk, preferred_element_type=F32).astype(BF16)
        xs_ref[0, r * GRID_W:(r + 1) * GRID_W, :] = blk
    gr = jnp.dot(hb, w_ref[:, df + dr:], preferred_element_type=F32)
    gg_ref[0] = _gelu_tanh(gr).astype(BF16)


def _stage_b(x, shift, scale, g, w_in_bf, jmat, df, dr, tm):
    b, s, d = x.shape
    n = w_in_bf.shape[1]
    vec = pl.BlockSpec((1, 1, d), lambda i, t: (i, 0, 0))
    out = pl.BlockSpec((1, tm, df), lambda i, t: (i, t, 0))
    return pl.pallas_call(
        functools.partial(_stage_b_kernel, df=df, dr=dr),
        grid=(b, s // tm),
        in_specs=[pl.BlockSpec((1, tm, d), lambda i, t: (i, t, 0)), vec, vec,
                  pl.BlockSpec((1, d), lambda i, t: (0, 0)),
                  pl.BlockSpec((d, n), lambda i, t: (0, 0)),
                  pl.BlockSpec((GRID_W, GRID_W), lambda i, t: (0, 0))],
        out_specs=[out, out, out],
        out_shape=[jax.ShapeDtypeStruct((b, s, df), BF16)] * 3,
        compiler_params=_cparams(("parallel", "arbitrary")),
        name="stage_b",
    )(x, shift, scale, g.reshape(1, d), w_in_bf, jmat)


def _stage_b_ctx_kernel(x_ref, sh_ref, sc_ref, g_ref, w_ref, xr_ref):
    h = _rms(x_ref[0], g_ref[...]) * (1.0 + sc_ref[0]) + sh_ref[0]
    xr_ref[0] = jnp.dot(h.astype(BF16), w_ref[...], preferred_element_type=F32).astype(BF16)


def _stage_b_ctx(ctx, shift, scale, g, w_xr_bf):
    b, s, d = ctx.shape
    dr = w_xr_bf.shape[1]
    vec = pl.BlockSpec((1, 1, d), lambda i: (0, 0, 0))
    return pl.pallas_call(
        _stage_b_ctx_kernel,
        grid=(b,),
        in_specs=[pl.BlockSpec((1, s, d), lambda i: (i, 0, 0)), vec, vec,
                  pl.BlockSpec((1, d), lambda i: (0, 0)),
                  pl.BlockSpec((d, dr), lambda i: (0, 0))],
        out_specs=pl.BlockSpec((1, s, dr), lambda i: (i, 0, 0)),
        out_shape=jax.ShapeDtypeStruct((b, s, dr), BF16),
        compiler_params=_cparams(("arbitrary",)),
        name="stage_b_ctx",
    )(ctx, shift, scale, g.reshape(1, d), w_xr_bf)


HALO = 16


def _seg_pitch(seg_len):
    n8 = seg_len // SUBLANES
    return SUBLANES * (n8 + 1 - n8 % 2)


def _sigmoid(x):
    return 0.5 * jnp.tanh(0.5 * x) + 0.5


def _rg_kernel(xs_ref, h0_ref, cw_ref, cb_ref, w_ref, b_ref, lam_ref, out_ref, hfin_ref,
               hf_s, xc_s, a_s, u_s, hl_s, p_s, hc_s, *, tc, nchunk, seq, dr):
    p = pl.program_id(1)
    j = pl.program_id(2)
    cidx = jnp.where(p == 0, j, nchunk - 1 - j)
    start = pl.multiple_of(cidx * tc, tc)
    nseg = SUBLANES
    sl = tc // nseg
    pitch = _seg_pitch(sl)
    nslab = dr // LANES

    @pl.when(p == 0)
    def _():
        cur = xs_ref[0, pl.ds(start, tc), :].astype(F32)
        pstart = pl.multiple_of(jnp.maximum(start - HALO, 0), HALO)
        nstart = pl.multiple_of(jnp.minimum(start + tc, seq - HALO), HALO)
        prev = xs_ref[0, pl.ds(pstart, HALO), :].astype(F32)
        nxt = xs_ref[0, pl.ds(nstart, HALO), :].astype(F32)
        prev = jnp.where(cidx > 0, prev, 0.0)
        nxt = jnp.where(cidx < nchunk - 1, nxt, 0.0)
        ext = jnp.concatenate([prev, cur, nxt], axis=0)
        xc = cb_ref[...] + cw_ref[0:1, :] * ext[HALO - CONV_PAD_LO:HALO - CONV_PAD_LO + tc]
        for k in range(1, CONV_W):
            o = HALO - CONV_PAD_LO + k
            xc = xc + cw_ref[k:k + 1, :] * ext[o:o + tc]
        xc_s[pl.ds(start, tc), :] = xc

    xc = xc_s[pl.ds(start, tc), :]
    gates = jnp.dot(xc.astype(BF16), w_ref[p], preferred_element_type=F32) + b_ref[p]
    r = _sigmoid(gates[:, :dr])
    i = _sigmoid(gates[:, dr:])
    log_a = (-RG_C) * r * jax.nn.softplus(-lam_ref[p])
    a = jnp.exp(log_a)
    u = jnp.sqrt(-jnp.tanh(log_a) * (1.0 + a * a)) * (i * xc)
    for g in range(nseg):
        for c in range(nslab):
            a_s[c, g * pitch:g * pitch + sl, :] = a[g * sl:(g + 1) * sl, c * LANES:(c + 1) * LANES]
            u_s[c, g * pitch:g * pitch + sl, :] = u[g * sl:(g + 1) * sl, c * LANES:(c + 1) * LANES]

    @pl.when(jnp.logical_and(p == 0, j == 0))
    def _():
        hfin_ref[...] = jnp.zeros_like(hfin_ref)

    @pl.when(j == 0)
    def _():
        hc_s[0:1, :] = h0_ref[0, pl.ds(p, 1), :]

    def local_scan(reverse):
        def body(q, carry):
            t = (sl - 1 - q) if reverse else q
            rows = pl.ds(t, nseg, stride=pitch)
            hs, ps = carry
            nh, npr = [], []
            for c in range(nslab):
                av = a_s[c, rows, :]
                h = av * hs[c] + u_s[c, rows, :]
                pr = av * ps[c]
                hl_s[c, rows, :] = h
                p_s[c, rows, :] = pr
                nh.append(h)
                npr.append(pr)
            return tuple(nh), tuple(npr)
        zero = tuple(jnp.zeros((nseg, LANES), F32) for _ in range(nslab))
        one = tuple(jnp.ones((nseg, LANES), F32) for _ in range(nslab))
        lax.fori_loop(0, sl, body, (zero, one), unroll=4)

    @pl.when(p == 0)
    def _():
        local_scan(False)
        for c in range(nslab):
            lanes = slice(c * LANES, (c + 1) * LANES)
            carry = hc_s[0:1, lanes]
            for g in range(nseg):
                seg = slice(g * pitch, g * pitch + sl)
                h = hl_s[c, seg, :] + p_s[c, seg, :] * carry
                hf_s[pl.ds(start + g * sl, sl), lanes] = h
                carry = h[sl - 1:sl, :]
            hc_s[0:1, lanes] = carry

    @pl.when(p == 1)
    def _():
        local_scan(True)
        for c in range(nslab):
            lanes = slice(c * LANES, (c + 1) * LANES)
            carry = hc_s[0:1, lanes]
            for g in range(nseg - 1, -1, -1):
                seg = slice(g * pitch, g * pitch + sl)
                h = hl_s[c, seg, :] + p_s[c, seg, :] * carry
                out_ref[0, g * sl:(g + 1) * sl, lanes] = (h + hf_s[pl.ds(start + g * sl, sl), lanes]).astype(BF16)
                carry = h[0:1, :]
            hc_s[0:1, lanes] = carry

    @pl.when(j == nchunk - 1)
    def _():
        hfin_ref[0, pl.ds(p, 1), :] = hc_s[0:1, :]


def _rg_scan(xs, h0, conv_w, conv_b, wcat, bcat, lam, tc):
    b, s, dr = xs.shape
    nchunk = s // tc
    last = nchunk - 1
    seg_rows = SUBLANES * _seg_pitch(tc // SUBLANES)
    seg_buf = pltpu.VMEM((dr // LANES, seg_rows, LANES), F32)
    full2 = lambda shape: pl.BlockSpec(shape, lambda i, p, j: (0,) * len(shape))
    return pl.pallas_call(
        functools.partial(_rg_kernel, tc=tc, nchunk=nchunk, seq=s, dr=dr),
        grid=(b, 2, nchunk),
        in_specs=[pl.BlockSpec((1, s, dr), lambda i, p, j: (i, 0, 0)),
                  pl.BlockSpec((1, SUBLANES, dr), lambda i, p, j: (i, 0, 0)),
                  full2((CONV_W, dr)), full2((1, dr)),
                  full2((2, dr, 2 * dr)), full2((2, 1, 2 * dr)), full2((2, 1, dr))],
        out_specs=[pl.BlockSpec((1, tc, dr), lambda i, p, j: (i, jnp.where(p == 0, last, last - j), 0)),
                   pl.BlockSpec((1, SUBLANES, dr), lambda i, p, j: (i, 0, 0))],
        out_shape=[jax.ShapeDtypeStruct((b, s, dr), BF16),
                   jax.ShapeDtypeStruct((b, SUBLANES, dr), F32)],
        scratch_shapes=[pltpu.VMEM((s, dr), F32), pltpu.VMEM((s, dr), F32),
                        seg_buf, seg_buf, seg_buf, seg_buf, pltpu.VMEM((SUBLANES, dr), F32)],
        compiler_params=_cparams(("arbitrary", "arbitrary", "arbitrary")),
        name="rg_scan",
    )(xs, h0, conv_w, conv_b.reshape(1, dr), wcat, bcat, lam)


def _f1_kernel(d_ref, x_ref, y_ref):
    y_ref[0] = jnp.dot(d_ref[...], x_ref[0], preferred_element_type=F32).astype(BF16)


def _fourier_stage1(fv, d2, tl):
    b, r, n = fv.shape
    return pl.pallas_call(
        _f1_kernel,
        grid=(b, n // tl),
        in_specs=[pl.BlockSpec((2 * r, r), lambda i, l: (0, 0)),
                  pl.BlockSpec((1, r, tl), lambda i, l: (i, 0, l))],
        out_specs=pl.BlockSpec((1, 2 * r, tl), lambda i, l: (i, 0, l)),
        out_shape=jax.ShapeDtypeStruct((b, 2 * r, n), BF16),
        compiler_params=_cparams(("parallel", "arbitrary")),
        name="fourier_stage1",
    )(d2, fv)


def _f2_kernel(y_ref, e_ref, bc_ref, bs_ref, g_ref, o_ref, *, kb, df):
    zr, zi = [], []
    for q in range(kb):
        yk = jnp.concatenate([y_ref[0, 0, q], y_ref[0, 1, q]], axis=0)
        z = jnp.dot(e_ref[q], yk, preferred_element_type=F32)
        zr.append(z[:GRID_W])
        zi.append(z[GRID_W:])
    zr = jnp.concatenate(zr, axis=0).astype(BF16)
    zi = jnp.concatenate(zi, axis=0).astype(BF16)
    o = jnp.dot(zr, bc_ref[...], preferred_element_type=F32)
    o += jnp.dot(zi, bs_ref[...], preferred_element_type=F32)
    on = _rms(o, g_ref[...]).astype(BF16)
    for q in range(kb):
        o_ref[0, :, q * df:(q + 1) * df] = on[q * GRID_W:(q + 1) * GRID_W]


def _fourier_stage2(y5, etab, bdc, bds, g, kb):
    b, _, r, w, df = y5.shape
    return pl.pallas_call(
        functools.partial(_f2_kernel, kb=kb, df=df),
        grid=(b, r // kb),
        in_specs=[pl.BlockSpec((1, 2, kb, w, df), lambda i, k: (i, 0, k, 0, 0)),
                  pl.BlockSpec((kb, 2 * w, 2 * w), lambda i, k: (k, 0, 0)),
                  pl.BlockSpec((df, df), lambda i, k: (0, 0)),
                  pl.BlockSpec((df, df), lambda i, k: (0, 0)),
                  pl.BlockSpec((1, df), lambda i, k: (0, 0))],
        out_specs=pl.BlockSpec((1, w, kb * df), lambda i, k: (i, 0, k)),
        out_shape=jax.ShapeDtypeStruct((b, w, r * df), BF16),
        compiler_params=_cparams(("parallel", "arbitrary")),
        name="fourier_stage2",
    )(y5, etab, bdc, bds, g.reshape(1, df))


def _stage_m_kernel(fn_ref, hs_ref, gg_ref, x_ref, gtm_ref, shf_ref, scf_ref, gr_ref, gffn_ref,
                    wo_ref, wr_ref, br_ref, j_ref, x1_ref, h2_ref, idx_ref, gate_ref, *, df):
    tm = x_ref.shape[1]
    hs = hs_ref[0]
    blocks = []
    for r in range(tm // GRID_W):
        blk = hs[r * GRID_W:(r + 1) * GRID_W]
        if r % 2 == 1:
            blk = jnp.dot(j_ref[...], blk, preferred_element_type=F32)
        blocks.append(blk.astype(F32))
    rg = jnp.concatenate(blocks, axis=0) * gg_ref[0].astype(F32)
    rgn = _rms(rg, gr_ref[...]).astype(BF16)
    mix = jnp.dot(fn_ref[0], wo_ref[:df, :], preferred_element_type=F32)
    mix += jnp.dot(rgn, wo_ref[df:, :], preferred_element_type=F32)
    x1 = x_ref[0] + gtm_ref[0] * mix
    x1_ref[0] = x1
    h2 = _rms(x1, gffn_ref[...]) * (1.0 + scf_ref[0]) + shf_ref[0]
    h2_ref[0] = _pack_halves(h2)

    logits = _dot3_nt(wr_ref[...], h2) + br_ref[...]
    eidx = lax.broadcasted_iota(jnp.int32, logits.shape, 0)
    vals, idxs = [], []
    for _ in range(TOP_K):
        m = jnp.max(logits, axis=0, keepdims=True)
        sel = jnp.min(jnp.where(logits == m, eidx, N_EXPERTS), axis=0, keepdims=True)
        vals.append(m)
        idxs.append(sel)
        logits = jnp.where(eidx == sel, -jnp.inf, logits)
    ex = [jnp.exp(v - vals[0]) for v in vals]
    den = ex[0] + ex[1] + ex[2] + ex[3]
    for k in range(TOP_K):
        gate_ref[k:k + 1, :] = ex[k] / den
        idx_ref[k:k + 1, :] = idxs[k]


def _stage_m(fn, hs, gg, x, gt_m, sh_f, sc_f, g_out_r, g_ffn, w_out_bf, w_router_t, b_router, jmat, tm):
    b, s, d = x.shape
    df = fn.shape[2]
    dr = hs.shape[2]
    nt = s // tm
    ne = w_router_t.shape[0]
    vec = pl.BlockSpec((1, 1, d), lambda i, t: (i, 0, 0))
    half = lambda dd: pl.BlockSpec((1, tm, dd), lambda i, t: (i, t, 0))
    full = lambda shape: pl.BlockSpec(shape, lambda i, t: (0,) * len(shape))
    tok = pl.BlockSpec((TOP_K, tm), lambda i, t: (0, i * nt + t))
    return pl.pallas_call(
        functools.partial(_stage_m_kernel, df=df),
        grid=(b, nt),
        in_specs=[half(df), half(dr), half(dr), half(d), vec, vec, vec,
                  full((1, dr)), full((1, d)), full((d, d)), full((ne, d)), full((ne, 1)),
                  full((GRID_W, GRID_W))],
        out_specs=[half(d), half(d // 2), tok, tok],
        out_shape=[jax.ShapeDtypeStruct((b, s, d), F32), jax.ShapeDtypeStruct((b, s, d // 2), jnp.int32),
                   jax.ShapeDtypeStruct((TOP_K, b * s), jnp.int32),
                   jax.ShapeDtypeStruct((TOP_K, b * s), F32)],
        compiler_params=_cparams(("parallel", "arbitrary")),
        name="stage_m",
    )(fn, hs, gg, x, gt_m, sh_f, sc_f, g_out_r.reshape(1, dr), g_ffn.reshape(1, d), w_out_bf,
      w_router_t, b_router.reshape(ne, 1), jmat)


def _rank_kernel(idx_ref, tri_ref, rank_ref, cnt_ref, carry_s):
    c = pl.program_id(0)

    @pl.when(c == 0)
    def _():
        carry_s[...] = jnp.zeros_like(carry_s)

    l = idx_ref.shape[1]
    eidx = lax.broadcasted_iota(jnp.int32, (N_EXPERTS, l), 0)
    for k in range(TOP_K):
        onehot = eidx == idx_ref[k:k + 1, :]
        oh = jnp.where(onehot, 1.0, 0.0)
        prefix = jnp.dot(oh.astype(BF16), tri_ref[...], preferred_element_type=F32)
        carry = carry_s[:, 0:1]
        rank = jnp.sum(jnp.where(onehot, prefix - 1.0 + carry, 0.0), axis=0, keepdims=True)
        rank_ref[k:k + 1, :] = rank.astype(jnp.int32)
        carry_s[...] = carry_s[...] + jnp.sum(oh, axis=1, keepdims=True)
    cnt_ref[...] = carry_s[...].astype(jnp.int32)


def _dispatch_ranks(idx, tri, tl):
    k, t = idx.shape
    return pl.pallas_call(
        _rank_kernel,
        grid=(t // tl,),
        in_specs=[pl.BlockSpec((k, tl), lambda c: (0, c)),
                  pl.BlockSpec((tl, tl), lambda c: (0, 0))],
        out_specs=[pl.BlockSpec((k, tl), lambda c: (0, c)),
                   pl.BlockSpec((N_EXPERTS, LANES), lambda c: (0, 0))],
        out_shape=[jax.ShapeDtypeStruct((k, t), jnp.int32),
                   jax.ShapeDtypeStruct((N_EXPERTS, LANES), jnp.int32)],
        scratch_shapes=[pltpu.VMEM((N_EXPERTS, LANES), F32)],
        compiler_params=_cparams(("arbitrary",)),
        name="dispatch_ranks",
    )(idx, tri)


def _moe_kernel(be_ref, bs_ref, bv_ref, nu_ref, x_ref, wgu_ref, bgu_ref, wd_ref, bd_ref, o_ref, wgu_s, wd_s,
                *, dff, nblk):
    i = pl.program_id(0)
    h = x_ref.shape[1]
    nxt = jnp.minimum(i, nblk - 1)

    @pl.when(jnp.logical_and(i < nblk, jnp.logical_or(i == 0, be_ref[nxt] != be_ref[jnp.maximum(nxt - 1, 0)])))
    def _():
        slot = bs_ref[nxt]
        wgu_s[slot] = wgu_ref[0].astype(BF16)
        wd_s[slot] = wd_ref[0].astype(BF16)

    j = jnp.maximum(i - 1, 0)

    @pl.when(jnp.logical_and(i >= 1, j < nu_ref[0]))
    def _():
        slot = bs_ref[j]
        rows = lax.broadcasted_iota(jnp.int32, x_ref.shape, 0)
        xw = jnp.where(rows < bv_ref[j], x_ref[...], 0)
        xa, xb = _unpack_halves(xw)
        gu = jnp.dot(xa.astype(BF16), wgu_s[slot, :h, :], preferred_element_type=F32)
        gu += jnp.dot(xb.astype(BF16), wgu_s[slot, h:, :], preferred_element_type=F32)
        gu += bgu_ref[0]
        gt = jnp.minimum(gu[:, :dff], SWIGLU_LIMIT)
        up = jnp.clip(gu[:, dff:], -SWIGLU_LIMIT, SWIGLU_LIMIT)
        act = (up + 1.0) * (gt * _sigmoid(SWIGLU_ALPHA * gt))
        out = jnp.dot(act.astype(BF16), wd_s[slot], preferred_element_type=F32) + bd_ref[0]
        o_ref[...] = _pack_halves(out)


def _moe_experts(blk_expert, blk_slot, blk_valid, n_used, xs, wgu, bgu, wd, bd, tmm):
    cap, h = xs.shape
    ne, d, dff2 = wgu.shape
    dff = dff2 // 2
    nblk = cap // tmm
    row_blk = lambda i, be, bs, bv, nu: (jnp.minimum(jnp.maximum(i - 1, 0), nu[0] - 1), 0)
    wnext = lambda i, be, bs, bv, nu: (be[jnp.minimum(i, nblk - 1)], 0, 0)
    bcur = lambda i, be, bs, bv, nu: (be[jnp.maximum(i - 1, 0)], 0, 0)
    grid_spec = pltpu.PrefetchScalarGridSpec(
        num_scalar_prefetch=4,
        grid=(nblk + 1,),
        in_specs=[pl.BlockSpec((tmm, h), row_blk),
                  pl.BlockSpec((1, d, dff2), wnext),
                  pl.BlockSpec((1, 1, dff2), bcur),
                  pl.BlockSpec((1, dff, d), wnext),
                  pl.BlockSpec((1, 1, d), bcur)],
        out_specs=pl.BlockSpec((tmm, h), row_blk),
        scratch_shapes=[pltpu.VMEM((2, d, dff2), BF16), pltpu.VMEM((2, dff, d), BF16)],
    )
    return pl.pallas_call(
        functools.partial(_moe_kernel, dff=dff, nblk=nblk),
        grid_spec=grid_spec,
        out_shape=jax.ShapeDtypeStruct((cap, h), jnp.int32),
        compiler_params=_cparams(("arbitrary",)),
        name="moe_experts",
    )(blk_expert, blk_slot, blk_valid, n_used, xs, wgu, bgu.reshape(ne, 1, dff2), wd, bd.reshape(ne, 1, d))


SC_CHUNK = 64


def _sc_workers():
    info = plsc.get_sparse_core_info()
    return info.num_cores, info.num_subcores


def _sc_dispatch(rows, dest, cap):
    n, w = rows.shape
    topk = dest.shape[0] // n
    nc, ns = _sc_workers()
    per_w = n // (nc * ns)
    assert per_w % SC_CHUNK == 0
    mesh = plsc.VectorSubcoreMesh(core_axis_name="c", subcore_axis_name="s")

    @functools.partial(
        pl.kernel, mesh=mesh,
        out_type=jax.ShapeDtypeStruct((cap, w), jnp.int32),
        scratch_types=[pltpu.VMEM((SC_CHUNK,), jnp.int32),
                       pltpu.VMEM((SC_CHUNK, w), jnp.int32),
                       pltpu.SemaphoreType.DMA],
    )
    def scatter_rows(rows_hbm, dest_hbm, out_hbm, idx_v, rows_v, sem):
        base = (lax.axis_index("s") * nc + lax.axis_index("c")) * per_w

        @pl.loop(0, per_w // SC_CHUNK)
        def _(j):
            off = pl.multiple_of(base + j * SC_CHUNK, SC_CHUNK)
            pltpu.sync_copy(rows_hbm.at[pl.ds(off, SC_CHUNK)], rows_v)
            for k in range(topk):
                pltpu.sync_copy(dest_hbm.at[pl.ds(k * n + off, SC_CHUNK)], idx_v)
                pltpu.async_copy(rows_v, out_hbm.at[idx_v], sem).wait()

    return scatter_rows(rows, dest)


def _sc_gather(table, idx):
    n = idx.shape[0]
    w = table.shape[1]
    nc, ns = _sc_workers()
    per_w = n // (nc * ns)
    assert per_w % SC_CHUNK == 0
    mesh = plsc.VectorSubcoreMesh(core_axis_name="c", subcore_axis_name="s")

    @functools.partial(
        pl.kernel, mesh=mesh,
        out_type=jax.ShapeDtypeStruct((n, w), jnp.int32),
        scratch_types=[pltpu.VMEM((SC_CHUNK,), jnp.int32),
                       pltpu.VMEM((SC_CHUNK, w), jnp.int32),
                       pltpu.SemaphoreType.DMA],
    )
    def gather_rows(table_hbm, idx_hbm, out_hbm, idx_v, rows_v, sem):
        base = (lax.axis_index("s") * nc + lax.axis_index("c")) * per_w

        @pl.loop(0, per_w // SC_CHUNK)
        def _(j):
            off = pl.multiple_of(base + j * SC_CHUNK, SC_CHUNK)
            pltpu.sync_copy(idx_hbm.at[pl.ds(off, SC_CHUNK)], idx_v)
            pltpu.async_copy(table_hbm.at[idx_v], rows_v, sem).wait()
            pltpu.sync_copy(rows_v, out_hbm.at[pl.ds(off, SC_CHUNK)])

    return gather_rows(table, idx)


def _combine_kernel(x1_ref, y_ref, gate_ref, gtf_ref, g_ref, *rest):
    o_ref = rest[-1]
    h = y_ref.shape[3]
    gates = gate_ref[0]
    moe_a = moe_b = None
    for k in range(TOP_K):
        ya, yb = _unpack_halves(y_ref[k, 0])
        gk = gates[:, k:k + 1]
        moe_a = gk * ya if k == 0 else moe_a + gk * ya
        moe_b = gk * yb if k == 0 else moe_b + gk * yb
    za = x1_ref[0, :, :h] + gtf_ref[0, :, :h] * moe_a
    zb = x1_ref[0, :, h:] + gtf_ref[0, :, h:] * moe_b
    ms = (jnp.sum(za * za, axis=-1, keepdims=True) + jnp.sum(zb * zb, axis=-1, keepdims=True)) / (2 * h)
    inv = lax.rsqrt(ms + EPS)
    o_ref[0, :, :h] = za * inv * g_ref[:, :h]
    o_ref[0, :, h:] = zb * inv * g_ref[:, h:]


def _combine(x1, yk, gates_t, gt_f, g_final, tm, b0, prev_out):
    b, s, d = x1.shape
    _, bp, _, h = yk.shape
    in_specs = [pl.BlockSpec((1, tm, d), lambda i, t: (i + b0, t, 0)),
                pl.BlockSpec((TOP_K, 1, tm, h), lambda i, t: (0, i, t, 0)),
                pl.BlockSpec((1, tm, TOP_K), lambda i, t: (i + b0, t, 0)),
                pl.BlockSpec((1, 1, d), lambda i, t: (i + b0, 0, 0)),
                pl.BlockSpec((1, d), lambda i, t: (0, 0))]
    args = [x1, yk, gates_t, gt_f, g_final.reshape(1, d)]
    aliases = {}
    if prev_out is not None:
        in_specs.append(pl.BlockSpec(memory_space=pl.ANY))
        args.append(prev_out)
        aliases = {len(args) - 1: 0}
    return pl.pallas_call(
        _combine_kernel,
        grid=(bp, s // tm),
        in_specs=in_specs,
        out_specs=pl.BlockSpec((1, tm, d), lambda i, t: (i + b0, t, 0)),
        out_shape=jax.ShapeDtypeStruct((b, s, d), F32),
        input_output_aliases=aliases,
        compiler_params=_cparams(("parallel", "arbitrary")),
        name="combine",
    )(*args)


def _dft_tables(rows, gd):
    seq = rows * GRID_W
    n = np.arange(rows)
    ang1 = 2.0 * np.pi * np.outer(n, n) / rows
    d2 = np.concatenate([np.cos(ang1), -np.sin(ang1)], axis=0)
    k1 = np.arange(rows)[:, None, None]
    k2 = np.arange(GRID_W)[None, :, None]
    n2 = np.arange(GRID_W)[None, None, :]
    ang2 = 2.0 * np.pi * ((n2 * (k1 + rows * k2)) % seq) / seq
    ec, es = np.cos(ang2), np.sin(ang2)
    etab = np.concatenate([np.concatenate([ec, es], axis=2),
                           np.concatenate([-es, ec], axis=2)], axis=1)
    c = np.arange(gd)
    angc = 2.0 * np.pi * np.outer(c, c) / gd
    scale = 1.0 / np.sqrt(seq * gd)
    return (jnp.asarray(d2, BF16), jnp.asarray(etab, BF16),
            jnp.asarray(np.cos(angc) * scale, F32), jnp.asarray(np.sin(angc) * scale, F32))


def _block_diag(w):
    h, i, o = w.shape
    eye = jnp.eye(h, dtype=w.dtype)
    return (eye[:, None, :, None] * w[:, :, None, :]).reshape(h * i, h * o)


def kernel(x, c, ctx, c_ctx, w_mod, b_mod, g_norm_mix, g_norm_ffn, w_in, w_fourier, conv_w, conv_b,
           rg_w_a, rg_b_a, rg_w_x, rg_b_x, rg_lam, g_out_fourier, g_out_rg, w_out, w_router, b_router,
           w_gate_up, b_gate_up, w_down, b_down, g_final):
    assert w_mod.shape[0] == 1, "single-layer stack only"
    b, s, d = x.shape
    df = w_fourier.shape[1] * w_fourier.shape[2]
    dr = conv_w.shape[2]
    gd = w_fourier.shape[2]
    rows = s // GRID_W
    t = b * s
    ne = w_router.shape[2]

    mrows = -(-(b + 1) // SUBLANES) * SUBLANES
    cond = jnp.zeros((mrows, d), F32).at[:b].set(c).at[b].set(c_ctx)
    mod = _adaln(cond, w_mod[0], b_mod[0])
    sh_m, sc_m, gt_m, sh_f, sc_f, gt_f = [mod[:b, k * d:(k + 1) * d].reshape(b, 1, d) for k in range(N_MOD)]
    csh_m = mod[b:b + 1, 0:d].reshape(1, 1, d)
    csc_m = mod[b:b + 1, d:2 * d].reshape(1, 1, d)

    tm = min(s, TOKEN_TILE)
    d2, etab, cmat, smat = _dft_tables(rows, gd)
    jmat = jnp.asarray(np.eye(GRID_W)[::-1].copy(), BF16)

    w_in_bf = w_in[0].astype(BF16)
    f, xs, gg = _stage_b(x, sh_m, sc_m, g_norm_mix[0], w_in_bf, jmat, df, dr, tm=tm)
    xr_ctx = _stage_b_ctx(ctx, csh_m, csc_m, g_norm_mix[0], w_in_bf[:, df:df + dr])

    wcat = jnp.stack([jnp.concatenate([_block_diag(rg_w_a[0, dd]), _block_diag(rg_w_x[0, dd])], axis=1)
                      for dd in range(2)]).astype(BF16)
    bcat = jnp.concatenate([rg_b_a[0], rg_b_x[0]], axis=1).reshape(2, 1, 2 * dr)
    lam = rg_lam[0].reshape(2, 1, dr)
    h0 = jnp.zeros((b, SUBLANES, dr), F32)
    _, hfin_ctx = _rg_scan(xr_ctx, h0, conv_w[0], conv_b[0], wcat, bcat, lam, tc=ctx.shape[1])
    hs, _ = _rg_scan(xs, hfin_ctx, conv_w[0], conv_b[0], wcat, bcat, lam, tc=min(s, SCAN_CHUNK))

    cw, sw = _fold_fourier(cmat, smat, w_fourier[0])
    bdc = _block_diag(cw).astype(BF16)
    bds = _block_diag(sw).astype(BF16)
    y = _fourier_stage1(f.reshape(b, rows, GRID_W * df), d2, tl=min(GRID_W * df, 8192))
    fn = _fourier_stage2(y.reshape(b, 2, rows, GRID_W, df), etab, bdc, bds, g_out_fourier[0],
                         kb=min(rows, 16))
    fn = fn.reshape(b, s, df)

    x1, h2, idx, gates = _stage_m(fn, hs, gg, x, gt_m, sh_f, sc_f, g_out_rg[0], g_norm_ffn[0],
                                  w_out[0].astype(BF16), w_router[0].T, b_router[0], jmat, tm=tm)

    tl = min(t, RANK_TILE)
    tri = jnp.asarray(np.triu(np.ones((tl, tl))), BF16)
    rank, cnt = _dispatch_ranks(idx, tri, tl)
    counts = cnt[:, 0]
    tmm = MOE_ROW_TILE
    padded = (counts + tmm - 1) // tmm * tmm
    pad_end = jnp.cumsum(padded)
    pad_start = pad_end - padded
    eids = jnp.arange(ne, dtype=jnp.int32)
    dest = rank + jnp.sum(jnp.where(idx[:, :, None] == eids, pad_start, 0), axis=-1)
    n_blocks = -(-(t * TOP_K) // tmm) + ne
    cap = n_blocks * tmm
    n_used = (pad_end[-1] // tmm).astype(jnp.int32).reshape(1)
    blk_start = jnp.arange(n_blocks, dtype=jnp.int32) * tmm
    blk_expert = jnp.sum(blk_start[:, None] >= pad_end[None, :], axis=1).astype(jnp.int32)
    last_expert = jnp.sum(pad_end[-1] - tmm >= pad_end).astype(jnp.int32)
    blk_expert = jnp.minimum(blk_expert, last_expert)
    sel = blk_expert[:, None] == eids
    blk_first = jnp.sum(jnp.where(sel, pad_start, 0), axis=1)
    blk_count = jnp.sum(jnp.where(sel, counts, 0), axis=1)
    blk_valid = jnp.clip(blk_count - (blk_start - blk_first), 0, tmm).astype(jnp.int32)

    x_sorted = _sc_dispatch(h2.reshape(t, d // 2), dest.reshape(-1), cap)
    changed = jnp.concatenate([jnp.zeros((1,), jnp.int32), (blk_expert[1:] != blk_expert[:-1]).astype(jnp.int32)])
    blk_slot = jnp.cumsum(changed) % 2
    y_sorted = _moe_experts(blk_expert, blk_slot.astype(jnp.int32), blk_valid, n_used, x_sorted,
                            w_gate_up[0], b_gate_up[0], w_down[0], b_down[0], tmm)
    gates_t = gates.T.reshape(b, s, TOP_K)
    dest_b = dest.reshape(TOP_K, b, s)
    bp = b // COMBINE_GROUPS if b % COMBINE_GROUPS == 0 else b
    out = None
    for b0 in range(0, b, bp):
        yk = _sc_gather(y_sorted, dest_b[:, b0:b0 + bp].reshape(-1)).reshape(TOP_K, bp, s, d // 2)
        out = _combine(x1, yk, gates_t, gt_f, g_final, tm, b0, out)
    return out
```

```python
import functools

import numpy as np
import jax
import jax.numpy as jnp
from jax import lax
from jax.experimental import pallas as pl
from jax.experimental.pallas import tpu as pltpu
from jax.experimental.pallas import tpu_sc as plsc

GRID_W = 64
FOURIER_GROUPS = 4
RG_HEADS = 8
CONV_W = 4
CONV_PAD_LO = 2
RG_C = 8.0
N_EXPERTS = 32
TOP_K = 4
SWIGLU_LIMIT = 7.0
SWIGLU_ALPHA = 1.702
N_MOD = 6
EPS = 1e-6

LANES = 128
SUBLANES = 8
VMEM_LIMIT_BYTES = 56 * 1024 * 1024
TOKEN_TILE = 1024
RANK_TILE = 512
SCAN_CHUNK = 512
MOE_ROW_TILE = 512
COMBINE_GROUPS = 4

F32 = jnp.float32
BF16 = jnp.bfloat16


def _cparams(sem):
    return pltpu.CompilerParams(dimension_semantics=sem, vmem_limit_bytes=VMEM_LIMIT_BYTES)


def _split_bf16(a):
    hi = a.astype(BF16)
    lo = (a - hi.astype(F32)).astype(BF16)
    return hi, lo


def _dot3(a, b):
    ah, al = _split_bf16(a)
    bh, bl = _split_bf16(b)
    out = jnp.dot(ah, bh, preferred_element_type=F32)
    out += jnp.dot(ah, bl, preferred_element_type=F32)
    out += jnp.dot(al, bh, preferred_element_type=F32)
    return out


def _dot3_nt(a, b):
    dn = (((1,), (1,)), ((), ()))
    ah, al = _split_bf16(a)
    bh, bl = _split_bf16(b)
    out = lax.dot_general(ah, bh, dn, preferred_element_type=F32)
    out += lax.dot_general(ah, bl, dn, preferred_element_type=F32)
    out += lax.dot_general(al, bh, dn, preferred_element_type=F32)
    return out


def _gelu_tanh(x):
    return 0.5 * x * (1.0 + jnp.tanh(0.7978845608028654 * (x + 0.044715 * (x * x * x))))


def _rms(x, g):
    return x * lax.rsqrt(jnp.mean(x * x, axis=-1, keepdims=True) + EPS) * g


def _pack_halves(v):
    h = v.shape[1] // 2
    hi = lax.bitcast_convert_type(v[:, :h].astype(BF16).astype(F32), jnp.uint32)
    lo = lax.bitcast_convert_type(v[:, h:].astype(BF16).astype(F32), jnp.uint32)
    return lax.bitcast_convert_type(hi | (lo >> 16), jnp.int32)


def _unpack_halves(w):
    u = lax.bitcast_convert_type(w, jnp.uint32)
    hi = lax.bitcast_convert_type(u & jnp.uint32(0xFFFF0000), F32)
    lo = lax.bitcast_convert_type(u << 16, F32)
    return hi, lo


def _adaln_kernel(c_ref, w_ref, b_ref, o_ref):
    s = c_ref[...]
    s = s * jax.nn.sigmoid(s)
    o_ref[...] = _dot3(s, w_ref[...]) + b_ref[...]


def _adaln(cond, w_mod, b_mod):
    m, d = cond.shape
    n = w_mod.shape[1]
    tn = n // N_MOD
    return pl.pallas_call(
        _adaln_kernel,
        grid=(n // tn,),
        in_specs=[pl.BlockSpec((m, d), lambda i: (0, 0)),
                  pl.BlockSpec((d, tn), lambda i: (0, i)),
                  pl.BlockSpec((1, tn), lambda i: (0, i))],
        out_specs=pl.BlockSpec((m, tn), lambda i: (0, i)),
        out_shape=jax.ShapeDtypeStruct((m, n), F32),
        compiler_params=_cparams(("arbitrary",)),
        name="adaln",
    )(cond, w_mod, b_mod.reshape(1, n))


def _fold_kernel(c_ref, s_ref, w_ref, cw_ref, sw_ref):
    w = w_ref[0]
    cw_ref[0] = _dot3(c_ref[...], w)
    sw_ref[0] = _dot3(s_ref[...], w)


def _fold_fourier(cmat, smat, w_f):
    g, gd, _ = w_f.shape
    spec_m = pl.BlockSpec((gd, gd), lambda i: (0, 0))
    spec_w = pl.BlockSpec((1, gd, gd), lambda i: (i, 0, 0))
    return pl.pallas_call(
        _fold_kernel,
        grid=(g,),
        in_specs=[spec_m, spec_m, spec_w],
        out_specs=[spec_w, spec_w],
        out_shape=[jax.ShapeDtypeStruct((g, gd, gd), F32)] * 2,
        compiler_params=_cparams(("arbitrary",)),
        name="fold_fourier",
    )(cmat, smat, w_f)


def _seg_pitch(seg_len):
    n8 = seg_len // SUBLANES
    return SUBLANES * (n8 + 1 - n8 % 2)


def _transpose_row_blocks(v, buf, nb, store):
    pitch = _seg_pitch(GRID_W)
    nslab = v.shape[1] // LANES
    for r in range(nb):
        for c in range(nslab):
            buf[c, r * pitch:r * pitch + GRID_W, :] = v[r * GRID_W:(r + 1) * GRID_W, c * LANES:(c + 1) * LANES]
    for pos in range(GRID_W):
        for c in range(nslab):
            store(pos, c, buf[c, pl.ds(pos, nb, stride=pitch), :])


def _stage_b_kernel(x_ref, sh_ref, sc_ref, g_ref, w_ref, j_ref, f_ref, xs_ref, gg_ref, fbuf, *, df, dr):
    h = _rms(x_ref[0], g_ref[...]) * (1.0 + sc_ref[0]) + sh_ref[0]
    hb = h.astype(BF16)
    f = jnp.dot(hb, w_ref[:, :df], preferred_element_type=F32)

    def store_f(pos, c, tile):
        f_ref[0, :, pos * df + c * LANES:pos * df + (c + 1) * LANES] = tile.astype(BF16)
    _transpose_row_blocks(f, fbuf, f.shape[0] // GRID_W, store_f)
    xr = jnp.dot(hb, w_ref[:, df:df + dr], preferred_element_type=F32).astype(BF16)
    tm = xr.shape[0]
    for r in range(tm // GRID_W):
        blk = xr[r * GRID_W:(r + 1) * GRID_W]
        if r % 2 == 1:
            blk = jnp.dot(j_ref[...], blk, preferred_element_type=F32).astype(BF16)
        xs_ref[0, r * GRID_W:(r + 1) * GRID_W, :] = blk
    gr = jnp.dot(hb, w_ref[:, df + dr:], preferred_element_type=F32)
    gg_ref[0] = _gelu_tanh(gr).astype(BF16)


def _stage_b(x, shift, scale, g, w_in_bf, jmat, df, dr, tm):
    b, s, d = x.shape
    n = w_in_bf.shape[1]
    vec = pl.BlockSpec((1, 1, d), lambda i, t: (i, 0, 0))
    out = pl.BlockSpec((1, tm, df), lambda i, t: (i, t, 0))
    nb = tm // GRID_W
    tok = jax.ShapeDtypeStruct((b, s, df), BF16)
    return pl.pallas_call(
        functools.partial(_stage_b_kernel, df=df, dr=dr),
        grid=(b, s // tm),
        in_specs=[pl.BlockSpec((1, tm, d), lambda i, t: (i, t, 0)), vec, vec,
                  pl.BlockSpec((1, d), lambda i, t: (0, 0)),
                  pl.BlockSpec((d, n), lambda i, t: (0, 0)),
                  pl.BlockSpec((GRID_W, GRID_W), lambda i, t: (0, 0))],
        out_specs=[pl.BlockSpec((1, nb, GRID_W * df), lambda i, t: (i, t, 0)), out, out],
        out_shape=[jax.ShapeDtypeStruct((b, s // GRID_W, GRID_W * df), BF16), tok, tok],
        scratch_shapes=[pltpu.VMEM((df // LANES, nb * _seg_pitch(GRID_W), LANES), F32)],
        compiler_params=_cparams(("parallel", "arbitrary")),
        name="stage_b",
    )(x, shift, scale, g.reshape(1, d), w_in_bf, jmat)


def _stage_b_ctx_kernel(x_ref, sh_ref, sc_ref, g_ref, w_ref, xr_ref):
    h = _rms(x_ref[0], g_ref[...]) * (1.0 + sc_ref[0]) + sh_ref[0]
    xr_ref[0] = jnp.dot(h.astype(BF16), w_ref[...], preferred_element_type=F32).astype(BF16)


def _stage_b_ctx(ctx, shift, scale, g, w_xr_bf):
    b, s, d = ctx.shape
    dr = w_xr_bf.shape[1]
    vec = pl.BlockSpec((1, 1, d), lambda i: (0, 0, 0))
    return pl.pallas_call(
        _stage_b_ctx_kernel,
        grid=(b,),
        in_specs=[pl.BlockSpec((1, s, d), lambda i: (i, 0, 0)), vec, vec,
                  pl.BlockSpec((1, d), lambda i: (0, 0)),
                  pl.BlockSpec((d, dr), lambda i: (0, 0))],
        out_specs=pl.BlockSpec((1, s, dr), lambda i: (i, 0, 0)),
        out_shape=jax.ShapeDtypeStruct((b, s, dr), BF16),
        compiler_params=_cparams(("arbitrary",)),
        name="stage_b_ctx",
    )(ctx, shift, scale, g.reshape(1, d), w_xr_bf)


HALO = 16


def _sigmoid(x):
    return 0.5 * jnp.tanh(0.5 * x) + 0.5


def _rg_kernel(xs_ref, h0_ref, cw_ref, cb_ref, w_ref, b_ref, lam_ref, out_ref, hfin_ref,
               hf_s, xc_s, a_s, u_s, hl_s, p_s, hc_s, *, tc, nchunk, seq, dr):
    p = pl.program_id(1)
    j = pl.program_id(2)
    cidx = jnp.where(p == 0, j, nchunk - 1 - j)
    start = pl.multiple_of(cidx * tc, tc)
    nseg = SUBLANES
    sl = tc // nseg
    pitch = _seg_pitch(sl)
    nslab = dr // LANES

    @pl.when(p == 0)
    def _():
        cur = xs_ref[0, pl.ds(start, tc), :].astype(F32)
        pstart = pl.multiple_of(jnp.maximum(start - HALO, 0), HALO)
        nstart = pl.multiple_of(jnp.minimum(start + tc, seq - HALO), HALO)
        prev = xs_ref[0, pl.ds(pstart, HALO), :].astype(F32)
        nxt = xs_ref[0, pl.ds(nstart, HALO), :].astype(F32)
        prev = jnp.where(cidx > 0, prev, 0.0)
        nxt = jnp.where(cidx < nchunk - 1, nxt, 0.0)
        ext = jnp.concatenate([prev, cur, nxt], axis=0)
        xc = cb_ref[...] + cw_ref[0:1, :] * ext[HALO - CONV_PAD_LO:HALO - CONV_PAD_LO + tc]
        for k in range(1, CONV_W):
            o = HALO - CONV_PAD_LO + k
            xc = xc + cw_ref[k:k + 1, :] * ext[o:o + tc]
        xc_s[pl.ds(start, tc), :] = xc

    xc = xc_s[pl.ds(start, tc), :]
    gates = jnp.dot(xc.astype(BF16), w_ref[p], preferred_element_type=F32) + b_ref[p]
    r = _sigmoid(gates[:, :dr])
    i = _sigmoid(gates[:, dr:])
    log_a = (-RG_C) * r * jax.nn.softplus(-lam_ref[p])
    a = jnp.exp(log_a)
    u = jnp.sqrt(-jnp.tanh(log_a) * (1.0 + a * a)) * (i * xc)
    for g in range(nseg):
        for c in range(nslab):
            a_s[c, g * pitch:g * pitch + sl, :] = a[g * sl:(g + 1) * sl, c * LANES:(c + 1) * LANES]
            u_s[c, g * pitch:g * pitch + sl, :] = u[g * sl:(g + 1) * sl, c * LANES:(c + 1) * LANES]

    @pl.when(jnp.logical_and(p == 0, j == 0))
    def _():
        hfin_ref[...] = jnp.zeros_like(hfin_ref)

    @pl.when(j == 0)
    def _():
        hc_s[0:1, :] = h0_ref[0, pl.ds(p, 1), :]

    def local_scan(reverse):
        def body(q, carry):
            t = (sl - 1 - q) if reverse else q
            rows = pl.ds(t, nseg, stride=pitch)
            hs, ps = carry
            nh, npr = [], []
            for c in range(nslab):
                av = a_s[c, rows, :]
                h = av * hs[c] + u_s[c, rows, :]
                pr = av * ps[c]
                hl_s[c, rows, :] = h
                p_s[c, rows, :] = pr
                nh.append(h)
                npr.append(pr)
            return tuple(nh), tuple(npr)
        zero = tuple(jnp.zeros((nseg, LANES), F32) for _ in range(nslab))
        one = tuple(jnp.ones((nseg, LANES), F32) for _ in range(nslab))
        lax.fori_loop(0, sl, body, (zero, one), unroll=4)

    @pl.when(p == 0)
    def _():
        local_scan(False)
        for c in range(nslab):
            lanes = slice(c * LANES, (c + 1) * LANES)
            carry = hc_s[0:1, lanes]
            for g in range(nseg):
                seg = slice(g * pitch, g * pitch + sl)
                h = hl_s[c, seg, :] + p_s[c, seg, :] * carry
                hf_s[pl.ds(start + g * sl, sl), lanes] = h
                carry = h[sl - 1:sl, :]
            hc_s[0:1, lanes] = carry

    @pl.when(p == 1)
    def _():
        local_scan(True)
        for c in range(nslab):
            lanes = slice(c * LANES, (c + 1) * LANES)
            carry = hc_s[0:1, lanes]
            for g in range(nseg - 1, -1, -1):
                seg = slice(g * pitch, g * pitch + sl)
                h = hl_s[c, seg, :] + p_s[c, seg, :] * carry
                out_ref[0, g * sl:(g + 1) * sl, lanes] = (h + hf_s[pl.ds(start + g * sl, sl), lanes]).astype(BF16)
                carry = h[0:1, :]
            hc_s[0:1, lanes] = carry

    @pl.when(j == nchunk - 1)
    def _():
        hfin_ref[0, pl.ds(p, 1), :] = hc_s[0:1, :]


def _rg_scan(xs, h0, conv_w, conv_b, wcat, bcat, lam, tc):
    b, s, dr = xs.shape
    nchunk = s // tc
    last = nchunk - 1
    seg_rows = SUBLANES * _seg_pitch(tc // SUBLANES)
    seg_buf = pltpu.VMEM((dr // LANES, seg_rows, LANES), F32)
    full2 = lambda shape: pl.BlockSpec(shape, lambda i, p, j: (0,) * len(shape))
    return pl.pallas_call(
        functools.partial(_rg_kernel, tc=tc, nchunk=nchunk, seq=s, dr=dr),
        grid=(b, 2, nchunk),
        in_specs=[pl.BlockSpec((1, s, dr), lambda i, p, j: (i, 0, 0)),
                  pl.BlockSpec((1, SUBLANES, dr), lambda i, p, j: (i, 0, 0)),
                  full2((CONV_W, dr)), full2((1, dr)),
                  full2((2, dr, 2 * dr)), full2((2, 1, 2 * dr)), full2((2, 1, dr))],
        out_specs=[pl.BlockSpec((1, tc, dr), lambda i, p, j: (i, jnp.where(p == 0, last, last - j), 0)),
                   pl.BlockSpec((1, SUBLANES, dr), lambda i, p, j: (i, 0, 0))],
        out_shape=[jax.ShapeDtypeStruct((b, s, dr), BF16),
                   jax.ShapeDtypeStruct((b, SUBLANES, dr), F32)],
        scratch_shapes=[pltpu.VMEM((s, dr), F32), pltpu.VMEM((s, dr), F32),
                        seg_buf, seg_buf, seg_buf, seg_buf, pltpu.VMEM((SUBLANES, dr), F32)],
        compiler_params=_cparams(("arbitrary", "arbitrary", "arbitrary")),
        name="rg_scan",
    )(xs, h0, conv_w, conv_b.reshape(1, dr), wcat, bcat, lam)


def _f1_kernel(d_ref, x_ref, y_ref):
    y_ref[0] = jnp.dot(d_ref[...], x_ref[0], preferred_element_type=F32).astype(BF16)


def _fourier_stage1(fv, d2, tl):
    b, r, n = fv.shape
    return pl.pallas_call(
        _f1_kernel,
        grid=(b, n // tl),
        in_specs=[pl.BlockSpec((2 * r, r), lambda i, l: (0, 0)),
                  pl.BlockSpec((1, r, tl), lambda i, l: (i, 0, l))],
        out_specs=pl.BlockSpec((1, 2 * r, tl), lambda i, l: (i, 0, l)),
        out_shape=jax.ShapeDtypeStruct((b, 2 * r, n), BF16),
        compiler_params=_cparams(("parallel", "arbitrary")),
        name="fourier_stage1",
    )(d2, fv)


def _f2_kernel(y_ref, e_ref, bc_ref, bs_ref, g_ref, o_ref, obuf, *, kb, df):
    zr, zi = [], []
    for q in range(kb):
        yk = jnp.concatenate([y_ref[0, 0, q], y_ref[0, 1, q]], axis=0)
        z = jnp.dot(e_ref[q], yk, preferred_element_type=F32)
        zr.append(z[:GRID_W])
        zi.append(z[GRID_W:])
    zr = jnp.concatenate(zr, axis=0).astype(BF16)
    zi = jnp.concatenate(zi, axis=0).astype(BF16)
    o = jnp.dot(zr, bc_ref[...], preferred_element_type=F32)
    o += jnp.dot(zi, bs_ref[...], preferred_element_type=F32)
    on = _rms(o, g_ref[...])

    def store_o(pos, c, tile):
        o_ref[0, pos, :, c * LANES:(c + 1) * LANES] = tile.astype(BF16)
    _transpose_row_blocks(on, obuf, kb, store_o)


def _fourier_stage2(y5, etab, bdc, bds, g, kb):
    b, _, r, w, df = y5.shape
    return pl.pallas_call(
        functools.partial(_f2_kernel, kb=kb, df=df),
        grid=(b, r // kb),
        in_specs=[pl.BlockSpec((1, 2, kb, w, df), lambda i, k: (i, 0, k, 0, 0)),
                  pl.BlockSpec((kb, 2 * w, 2 * w), lambda i, k: (k, 0, 0)),
                  pl.BlockSpec((df, df), lambda i, k: (0, 0)),
                  pl.BlockSpec((df, df), lambda i, k: (0, 0)),
                  pl.BlockSpec((1, df), lambda i, k: (0, 0))],
        out_specs=pl.BlockSpec((1, w, kb, df), lambda i, k: (i, 0, k, 0)),
        out_shape=jax.ShapeDtypeStruct((b, w, r, df), BF16),
        scratch_shapes=[pltpu.VMEM((df // LANES, kb * _seg_pitch(GRID_W), LANES), F32)],
        compiler_params=_cparams(("parallel", "arbitrary")),
        name="fourier_stage2",
    )(y5, etab, bdc, bds, g.reshape(1, df))


def _stage_m_kernel(fn_ref, hs_ref, gg_ref, x_ref, gtm_ref, shf_ref, scf_ref, gr_ref, gffn_ref,
                    wo_ref, wr_ref, br_ref, j_ref, x1_ref, h2_ref, idx_ref, gate_ref, *, df):
    tm = x_ref.shape[1]
    hs = hs_ref[0]
    blocks = []
    for r in range(tm // GRID_W):
        blk = hs[r * GRID_W:(r + 1) * GRID_W]
        if r % 2 == 1:
            blk = jnp.dot(j_ref[...], blk, preferred_element_type=F32)
        blocks.append(blk.astype(F32))
    rg = jnp.concatenate(blocks, axis=0) * gg_ref[0].astype(F32)
    rgn = _rms(rg, gr_ref[...]).astype(BF16)
    mix = jnp.dot(fn_ref[0], wo_ref[:df, :], preferred_element_type=F32)
    mix += jnp.dot(rgn, wo_ref[df:, :], preferred_element_type=F32)
    x1 = x_ref[0] + gtm_ref[0] * mix
    x1_ref[0] = x1
    h2 = _rms(x1, gffn_ref[...]) * (1.0 + scf_ref[0]) + shf_ref[0]
    h2_ref[0] = _pack_halves(h2)

    logits = _dot3_nt(wr_ref[...], h2) + br_ref[...]
    eidx = lax.broadcasted_iota(jnp.int32, logits.shape, 0)
    vals, idxs = [], []
    for _ in range(TOP_K):
        m = jnp.max(logits, axis=0, keepdims=True)
        sel = jnp.min(jnp.where(logits == m, eidx, N_EXPERTS), axis=0, keepdims=True)
        vals.append(m)
        idxs.append(sel)
        logits = jnp.where(eidx == sel, -jnp.inf, logits)
    ex = [jnp.exp(v - vals[0]) for v in vals]
    den = ex[0] + ex[1] + ex[2] + ex[3]
    for k in range(TOP_K):
        gate_ref[k:k + 1, :] = ex[k] / den
        idx_ref[k:k + 1, :] = idxs[k]


def _stage_m(fn, hs, gg, x, gt_m, sh_f, sc_f, g_out_r, g_ffn, w_out_bf, w_router_t, b_router, jmat, tm):
    b, s, d = x.shape
    df = fn.shape[2]
    dr = hs.shape[2]
    nt = s // tm
    ne = w_router_t.shape[0]
    vec = pl.BlockSpec((1, 1, d), lambda i, t: (i, 0, 0))
    half = lambda dd: pl.BlockSpec((1, tm, dd), lambda i, t: (i, t, 0))
    full = lambda shape: pl.BlockSpec(shape, lambda i, t: (0,) * len(shape))
    tok = pl.BlockSpec((TOP_K, tm), lambda i, t: (0, i * nt + t))
    return pl.pallas_call(
        functools.partial(_stage_m_kernel, df=df),
        grid=(b, nt),
        in_specs=[half(df), half(dr), half(dr), half(d), vec, vec, vec,
                  full((1, dr)), full((1, d)), full((d, d)), full((ne, d)), full((ne, 1)),
                  full((GRID_W, GRID_W))],
        out_specs=[half(d), half(d // 2), tok, tok],
        out_shape=[jax.ShapeDtypeStruct((b, s, d), F32), jax.ShapeDtypeStruct((b, s, d // 2), jnp.int32),
                   jax.ShapeDtypeStruct((TOP_K, b * s), jnp.int32),
                   jax.ShapeDtypeStruct((TOP_K, b * s), F32)],
        compiler_params=_cparams(("parallel", "arbitrary")),
        name="stage_m",
    )(fn, hs, gg, x, gt_m, sh_f, sc_f, g_out_r.reshape(1, dr), g_ffn.reshape(1, d), w_out_bf,
      w_router_t, b_router.reshape(ne, 1), jmat)


def _rank_kernel(idx_ref, tri_ref, rank_ref, cnt_ref, carry_s):
    c = pl.program_id(0)

    @pl.when(c == 0)
    def _():
        carry_s[...] = jnp.zeros_like(carry_s)

    l = idx_ref.shape[1]
    eidx = lax.broadcasted_iota(jnp.int32, (N_EXPERTS, l), 0)
    for k in range(TOP_K):
        onehot = eidx == idx_ref[k:k + 1, :]
        oh = jnp.where(onehot, 1.0, 0.0)
        prefix = jnp.dot(oh.astype(BF16), tri_ref[...], preferred_element_type=F32)
        carry = carry_s[:, 0:1]
        rank = jnp.sum(jnp.where(onehot, prefix - 1.0 + carry, 0.0), axis=0, keepdims=True)
        rank_ref[k:k + 1, :] = rank.astype(jnp.int32)
        carry_s[...] = carry_s[...] + jnp.sum(oh, axis=1, keepdims=True)
    cnt_ref[...] = carry_s[...].astype(jnp.int32)


def _dispatch_ranks(idx, tri, tl):
    k, t = idx.shape
    return pl.pallas_call(
        _rank_kernel,
        grid=(t // tl,),
        in_specs=[pl.BlockSpec((k, tl), lambda c: (0, c)),
                  pl.BlockSpec((tl, tl), lambda c: (0, 0))],
        out_specs=[pl.BlockSpec((k, tl), lambda c: (0, c)),
                   pl.BlockSpec((N_EXPERTS, LANES), lambda c: (0, 0))],
        out_shape=[jax.ShapeDtypeStruct((k, t), jnp.int32),
                   jax.ShapeDtypeStruct((N_EXPERTS, LANES), jnp.int32)],
        scratch_shapes=[pltpu.VMEM((N_EXPERTS, LANES), F32)],
        compiler_params=_cparams(("arbitrary",)),
        name="dispatch_ranks",
    )(idx, tri)


def _moe_kernel(be_ref, bv_ref, nu_ref, x_ref, wgu_ref, bgu_ref, wd_ref, bd_ref, o_ref, wgu_s, wd_s, *, dff):
    i = pl.program_id(0)
    h = x_ref.shape[1]

    @pl.when(jnp.logical_or(i == 0, be_ref[i] != be_ref[jnp.maximum(i - 1, 0)]))
    def _():
        wgu_s[...] = wgu_ref[0].astype(BF16)
        wd_s[...] = wd_ref[0].astype(BF16)

    @pl.when(i < nu_ref[0])
    def _():
        rows = lax.broadcasted_iota(jnp.int32, x_ref.shape, 0)
        xw = jnp.where(rows < bv_ref[i], x_ref[...], 0)
        xa, xb = _unpack_halves(xw)
        gu = jnp.dot(xa.astype(BF16), wgu_s[:h, :], preferred_element_type=F32)
        gu += jnp.dot(xb.astype(BF16), wgu_s[h:, :], preferred_element_type=F32)
        gu += bgu_ref[0]
        gt = jnp.minimum(gu[:, :dff], SWIGLU_LIMIT)
        up = jnp.clip(gu[:, dff:], -SWIGLU_LIMIT, SWIGLU_LIMIT)
        act = (up + 1.0) * (gt * _sigmoid(SWIGLU_ALPHA * gt))
        out = jnp.dot(act.astype(BF16), wd_s[...], preferred_element_type=F32) + bd_ref[0]
        o_ref[...] = _pack_halves(out)


def _moe_experts(blk_expert, blk_valid, n_used, xs, wgu, bgu, wd, bd, tmm):
    cap, h = xs.shape
    ne, d, dff2 = wgu.shape
    dff = dff2 // 2
    row_blk = lambda i, be, bv, nu: (jnp.minimum(i, nu[0] - 1), 0)
    wsel = lambda i, be, bv, nu: (be[i], 0, 0)
    grid_spec = pltpu.PrefetchScalarGridSpec(
        num_scalar_prefetch=3,
        grid=(cap // tmm,),
        in_specs=[pl.BlockSpec((tmm, h), row_blk),
                  pl.BlockSpec((1, d, dff2), wsel),
                  pl.BlockSpec((1, 1, dff2), wsel),
                  pl.BlockSpec((1, dff, d), wsel),
                  pl.BlockSpec((1, 1, d), wsel)],
        out_specs=pl.BlockSpec((tmm, h), row_blk),
        scratch_shapes=[pltpu.VMEM((d, dff2), BF16), pltpu.VMEM((dff, d), BF16)],
    )
    return pl.pallas_call(
        functools.partial(_moe_kernel, dff=dff),
        grid_spec=grid_spec,
        out_shape=jax.ShapeDtypeStruct((cap, h), jnp.int32),
        compiler_params=_cparams(("arbitrary",)),
        name="moe_experts",
    )(blk_expert, blk_valid, n_used, xs, wgu, bgu.reshape(ne, 1, dff2), wd, bd.reshape(ne, 1, d))


SC_CHUNK = 64


def _sc_workers():
    info = plsc.get_sparse_core_info()
    return info.num_cores, info.num_subcores


def _sc_dispatch(rows, dest, cap):
    n, w = rows.shape
    topk = dest.shape[0] // n
    nc, ns = _sc_workers()
    per_w = n // (nc * ns)
    assert per_w % SC_CHUNK == 0
    mesh = plsc.VectorSubcoreMesh(core_axis_name="c", subcore_axis_name="s")

    @functools.partial(
        pl.kernel, mesh=mesh,
        out_type=jax.ShapeDtypeStruct((cap, w), jnp.int32),
        scratch_types=[pltpu.VMEM((SC_CHUNK,), jnp.int32),
                       pltpu.VMEM((SC_CHUNK, w), jnp.int32),
                       pltpu.SemaphoreType.DMA],
    )
    def scatter_rows(rows_hbm, dest_hbm, out_hbm, idx_v, rows_v, sem):
        base = (lax.axis_index("s") * nc + lax.axis_index("c")) * per_w

        @pl.loop(0, per_w // SC_CHUNK)
        def _(j):
            off = pl.multiple_of(base + j * SC_CHUNK, SC_CHUNK)
            pltpu.sync_copy(rows_hbm.at[pl.ds(off, SC_CHUNK)], rows_v)
            for k in range(topk):
                pltpu.sync_copy(dest_hbm.at[pl.ds(k * n + off, SC_CHUNK)], idx_v)
                pltpu.async_copy(rows_v, out_hbm.at[idx_v], sem).wait()

    return scatter_rows(rows, dest)


def _sc_gather(table, idx):
    n = idx.shape[0]
    w = table.shape[1]
    nc, ns = _sc_workers()
    per_w = n // (nc * ns)
    assert per_w % SC_CHUNK == 0
    mesh = plsc.VectorSubcoreMesh(core_axis_name="c", subcore_axis_name="s")

    @functools.partial(
        pl.kernel, mesh=mesh,
        out_type=jax.ShapeDtypeStruct((n, w), jnp.int32),
        scratch_types=[pltpu.VMEM((SC_CHUNK,), jnp.int32),
                       pltpu.VMEM((SC_CHUNK, w), jnp.int32),
                       pltpu.SemaphoreType.DMA],
    )
    def gather_rows(table_hbm, idx_hbm, out_hbm, idx_v, rows_v, sem):
        base = (lax.axis_index("s") * nc + lax.axis_index("c")) * per_w

        @pl.loop(0, per_w // SC_CHUNK)
        def _(j):
            off = pl.multiple_of(base + j * SC_CHUNK, SC_CHUNK)
            pltpu.sync_copy(idx_hbm.at[pl.ds(off, SC_CHUNK)], idx_v)
            pltpu.async_copy(table_hbm.at[idx_v], rows_v, sem).wait()
            pltpu.sync_copy(rows_v, out_hbm.at[pl.ds(off, SC_CHUNK)])

    return gather_rows(table, idx)


def _combine_kernel(x1_ref, y_ref, gate_ref, gtf_ref, g_ref, *rest):
    o_ref = rest[-1]
    h = y_ref.shape[3]
    gates = gate_ref[0]
    moe_a = moe_b = None
    for k in range(TOP_K):
        ya, yb = _unpack_halves(y_ref[k, 0])
        gk = gates[:, k:k + 1]
        moe_a = gk * ya if k == 0 else moe_a + gk * ya
        moe_b = gk * yb if k == 0 else moe_b + gk * yb
    za = x1_ref[0, :, :h] + gtf_ref[0, :, :h] * moe_a
    zb = x1_ref[0, :, h:] + gtf_ref[0, :, h:] * moe_b
    ms = (jnp.sum(za * za, axis=-1, keepdims=True) + jnp.sum(zb * zb, axis=-1, keepdims=True)) / (2 * h)
    inv = lax.rsqrt(ms + EPS)
    o_ref[0, :, :h] = za * inv * g_ref[:, :h]
    o_ref[0, :, h:] = zb * inv * g_ref[:, h:]


def _combine(x1, yk, gates_t, gt_f, g_final, tm, b0, prev_out):
    b, s, d = x1.shape
    _, bp, _, h = yk.shape
    in_specs = [pl.BlockSpec((1, tm, d), lambda i, t: (i + b0, t, 0)),
                pl.BlockSpec((TOP_K, 1, tm, h), lambda i, t: (0, i, t, 0)),
                pl.BlockSpec((1, tm, TOP_K), lambda i, t: (i + b0, t, 0)),
                pl.BlockSpec((1, 1, d), lambda i, t: (i + b0, 0, 0)),
                pl.BlockSpec((1, d), lambda i, t: (0, 0))]
    args = [x1, yk, gates_t, gt_f, g_final.reshape(1, d)]
    aliases = {}
    if prev_out is not None:
        in_specs.append(pl.BlockSpec(memory_space=pl.ANY))
        args.append(prev_out)
        aliases = {len(args) - 1: 0}
    return pl.pallas_call(
        _combine_kernel,
        grid=(bp, s // tm),
        in_specs=in_specs,
        out_specs=pl.BlockSpec((1, tm, d), lambda i, t: (i + b0, t, 0)),
        out_shape=jax.ShapeDtypeStruct((b, s, d), F32),
        input_output_aliases=aliases,
        compiler_params=_cparams(("parallel", "arbitrary")),
        name="combine",
    )(*args)


def _dft_tables(rows, gd):
    seq = rows * GRID_W
    n = np.arange(rows)
    ang1 = 2.0 * np.pi * np.outer(n, n) / rows
    d2 = np.concatenate([np.cos(ang1), -np.sin(ang1)], axis=0)
    k1 = np.arange(rows)[:, None, None]
    k2 = np.arange(GRID_W)[None, :, None]
    n2 = np.arange(GRID_W)[None, None, :]
    ang2 = 2.0 * np.pi * ((n2 * (k1 + rows * k2)) % seq) / seq
    ec, es = np.cos(ang2), np.sin(ang2)
    etab = np.concatenate([np.concatenate([ec, es], axis=2),
                           np.concatenate([-es, ec], axis=2)], axis=1)
    c = np.arange(gd)
    angc = 2.0 * np.pi * np.outer(c, c) / gd
    scale = 1.0 / np.sqrt(seq * gd)
    return (jnp.asarray(d2, BF16), jnp.asarray(etab, BF16),
            jnp.asarray(np.cos(angc) * scale, F32), jnp.asarray(np.sin(angc) * scale, F32))


def _block_diag(w):
    h, i, o = w.shape
    eye = jnp.eye(h, dtype=w.dtype)
    return (eye[:, None, :, None] * w[:, :, None, :]).reshape(h * i, h * o)


def kernel(x, c, ctx, c_ctx, w_mod, b_mod, g_norm_mix, g_norm_ffn, w_in, w_fourier, conv_w, conv_b,
           rg_w_a, rg_b_a, rg_w_x, rg_b_x, rg_lam, g_out_fourier, g_out_rg, w_out, w_router, b_router,
           w_gate_up, b_gate_up, w_down, b_down, g_final):
    assert w_mod.shape[0] == 1, "single-layer stack only"
    b, s, d = x.shape
    df = w_fourier.shape[1] * w_fourier.shape[2]
    dr = conv_w.shape[2]
    gd = w_fourier.shape[2]
    rows = s // GRID_W
    t = b * s
    ne = w_router.shape[2]

    mrows = -(-(b + 1) // SUBLANES) * SUBLANES
    cond = jnp.zeros((mrows, d), F32).at[:b].set(c).at[b].set(c_ctx)
    mod = _adaln(cond, w_mod[0], b_mod[0])
    sh_m, sc_m, gt_m, sh_f, sc_f, gt_f = [mod[:b, k * d:(k + 1) * d].reshape(b, 1, d) for k in range(N_MOD)]
    csh_m = mod[b:b + 1, 0:d].reshape(1, 1, d)
    csc_m = mod[b:b + 1, d:2 * d].reshape(1, 1, d)

    tm = min(s, TOKEN_TILE)
    d2, etab, cmat, smat = _dft_tables(rows, gd)
    jmat = jnp.asarray(np.eye(GRID_W)[::-1].copy(), BF16)

    w_in_bf = w_in[0].astype(BF16)
    f, xs, gg = _stage_b(x, sh_m, sc_m, g_norm_mix[0], w_in_bf, jmat, df, dr, tm=tm)
    xr_ctx = _stage_b_ctx(ctx, csh_m, csc_m, g_norm_mix[0], w_in_bf[:, df:df + dr])

    wcat = jnp.stack([jnp.concatenate([_block_diag(rg_w_a[0, dd]), _block_diag(rg_w_x[0, dd])], axis=1)
                      for dd in range(2)]).astype(BF16)
    bcat = jnp.concatenate([rg_b_a[0], rg_b_x[0]], axis=1).reshape(2, 1, 2 * dr)
    lam = rg_lam[0].reshape(2, 1, dr)
    h0 = jnp.zeros((b, SUBLANES, dr), F32)
    _, hfin_ctx = _rg_scan(xr_ctx, h0, conv_w[0], conv_b[0], wcat, bcat, lam, tc=ctx.shape[1])
    hs, _ = _rg_scan(xs, hfin_ctx, conv_w[0], conv_b[0], wcat, bcat, lam, tc=min(s, SCAN_CHUNK))

    cw, sw = _fold_fourier(cmat, smat, w_fourier[0])
    bdc = _block_diag(cw).astype(BF16)
    bds = _block_diag(sw).astype(BF16)
    y = _fourier_stage1(f, d2, tl=min(GRID_W * df, 8192))
    fn = _fourier_stage2(y.reshape(b, 2, rows, GRID_W, df), etab, bdc, bds, g_out_fourier[0],
                         kb=min(rows, 16))
    fn = fn.reshape(b, s, df)

    x1, h2, idx, gates = _stage_m(fn, hs, gg, x, gt_m, sh_f, sc_f, g_out_rg[0], g_norm_ffn[0],
                                  w_out[0].astype(BF16), w_router[0].T, b_router[0], jmat, tm=tm)

    tl = min(t, RANK_TILE)
    tri = jnp.asarray(np.triu(np.ones((tl, tl))), BF16)
    rank, cnt = _dispatch_ranks(idx, tri, tl)
    counts = cnt[:, 0]
    tmm = MOE_ROW_TILE
    padded = (counts + tmm - 1) // tmm * tmm
    pad_end = jnp.cumsum(padded)
    pad_start = pad_end - padded
    eids = jnp.arange(ne, dtype=jnp.int32)
    dest = rank + jnp.sum(jnp.where(idx[:, :, None] == eids, pad_start, 0), axis=-1)
    n_blocks = -(-(t * TOP_K) // tmm) + ne
    cap = n_blocks * tmm
    n_used = (pad_end[-1] // tmm).astype(jnp.int32).reshape(1)
    blk_start = jnp.arange(n_blocks, dtype=jnp.int32) * tmm
    blk_expert = jnp.sum(blk_start[:, None] >= pad_end[None, :], axis=1).astype(jnp.int32)
    last_expert = jnp.sum(pad_end[-1] - tmm >= pad_end).astype(jnp.int32)
    blk_expert = jnp.minimum(blk_expert, last_expert)
    sel = blk_expert[:, None] == eids
    blk_first = jnp.sum(jnp.where(sel, pad_start, 0), axis=1)
    blk_count = jnp.sum(jnp.where(sel, counts, 0), axis=1)
    blk_valid = jnp.clip(blk_count - (blk_start - blk_first), 0, tmm).astype(jnp.int32)

    x_sorted = _sc_dispatch(h2.reshape(t, d // 2), dest.reshape(-1), cap)
    y_sorted = _moe_experts(blk_expert, blk_valid, n_used, x_sorted,
                            w_gate_up[0], b_gate_up[0], w_down[0], b_down[0], tmm)
    gates_t = gates.T.reshape(b, s, TOP_K)
    dest_b = dest.reshape(TOP_K, b, s)
    bp = b // COMBINE_GROUPS if b % COMBINE_GROUPS == 0 else b
    out = None
    for b0 in range(0, b, bp):
        yk = _sc_gather(y_sorted, dest_b[:, b0:b0 + bp].reshape(-1)).reshape(TOP_K, bp, s, d // 2)
        out = _combine(x1, yk, gates_t, gt_f, g_final, tm, b0, out)
    return out
```

```python
import functools

import numpy as np
import jax
import jax.numpy as jnp
from jax import lax
from jax.experimental import pallas as pl
from jax.experimental.pallas import tpu as pltpu
from jax.experimental.pallas import tpu_sc as plsc

GRID_W = 64
FOURIER_GROUPS = 4
RG_HEADS = 8
CONV_W = 4
CONV_PAD_LO = 2
RG_C = 8.0
N_EXPERTS = 32
TOP_K = 4
SWIGLU_LIMIT = 7.0
SWIGLU_ALPHA = 1.702
N_MOD = 6
EPS = 1e-6

LANES = 128
SUBLANES = 8
VMEM_LIMIT_BYTES = 56 * 1024 * 1024
TOKEN_TILE = 1024
RANK_TILE = 512
SCAN_CHUNK = 512
MOE_ROW_TILE = 512
COMBINE_GROUPS = 4

F32 = jnp.float32
BF16 = jnp.bfloat16


def _cparams(sem):
    return pltpu.CompilerParams(dimension_semantics=sem, vmem_limit_bytes=VMEM_LIMIT_BYTES)


def _split_bf16(a):
    hi = a.astype(BF16)
    lo = (a - hi.astype(F32)).astype(BF16)
    return hi, lo


def _dot3(a, b):
    ah, al = _split_bf16(a)
    bh, bl = _split_bf16(b)
    out = jnp.dot(ah, bh, preferred_element_type=F32)
    out += jnp.dot(ah, bl, preferred_element_type=F32)
    out += jnp.dot(al, bh, preferred_element_type=F32)
    return out


def _dot3_nt(a, b):
    dn = (((1,), (1,)), ((), ()))
    ah, al = _split_bf16(a)
    bh, bl = _split_bf16(b)
    out = lax.dot_general(ah, bh, dn, preferred_element_type=F32)
    out += lax.dot_general(ah, bl, dn, preferred_element_type=F32)
    out += lax.dot_general(al, bh, dn, preferred_element_type=F32)
    return out


def _gelu_tanh(x):
    return 0.5 * x * (1.0 + jnp.tanh(0.7978845608028654 * (x + 0.044715 * (x * x * x))))


def _rms(x, g):
    return x * lax.rsqrt(jnp.mean(x * x, axis=-1, keepdims=True) + EPS) * g


def _pack_halves(v):
    h = v.shape[1] // 2
    hi = lax.bitcast_convert_type(v[:, :h].astype(BF16).astype(F32), jnp.uint32)
    lo = lax.bitcast_convert_type(v[:, h:].astype(BF16).astype(F32), jnp.uint32)
    return lax.bitcast_convert_type(hi | (lo >> 16), jnp.int32)


def _unpack_halves(w):
    u = lax.bitcast_convert_type(w, jnp.uint32)
    hi = lax.bitcast_convert_type(u & jnp.uint32(0xFFFF0000), F32)
    lo = lax.bitcast_convert_type(u << 16, F32)
    return hi, lo


def _adaln_kernel(c_ref, w_ref, b_ref, o_ref):
    s = c_ref[...]
    s = s * jax.nn.sigmoid(s)
    o_ref[...] = _dot3(s, w_ref[...]) + b_ref[...]


def _adaln(cond, w_mod, b_mod):
    m, d = cond.shape
    n = w_mod.shape[1]
    tn = n // N_MOD
    return pl.pallas_call(
        _adaln_kernel,
        grid=(n // tn,),
        in_specs=[pl.BlockSpec((m, d), lambda i: (0, 0)),
                  pl.BlockSpec((d, tn), lambda i: (0, i)),
                  pl.BlockSpec((1, tn), lambda i: (0, i))],
        out_specs=pl.BlockSpec((m, tn), lambda i: (0, i)),
        out_shape=jax.ShapeDtypeStruct((m, n), F32),
        compiler_params=_cparams(("arbitrary",)),
        name="adaln",
    )(cond, w_mod, b_mod.reshape(1, n))


def _fold_kernel(c_ref, s_ref, w_ref, cw_ref, sw_ref):
    w = w_ref[0]
    cw_ref[0] = _dot3(c_ref[...], w)
    sw_ref[0] = _dot3(s_ref[...], w)


def _fold_fourier(cmat, smat, w_f):
    g, gd, _ = w_f.shape
    spec_m = pl.BlockSpec((gd, gd), lambda i: (0, 0))
    spec_w = pl.BlockSpec((1, gd, gd), lambda i: (i, 0, 0))
    return pl.pallas_call(
        _fold_kernel,
        grid=(g,),
        in_specs=[spec_m, spec_m, spec_w],
        out_specs=[spec_w, spec_w],
        out_shape=[jax.ShapeDtypeStruct((g, gd, gd), F32)] * 2,
        compiler_params=_cparams(("arbitrary",)),
        name="fold_fourier",
    )(cmat, smat, w_f)


def _seg_pitch(seg_len):
    n8 = seg_len // SUBLANES
    return SUBLANES * (n8 + 1 - n8 % 2)


def _transpose_row_blocks(v, buf, nb, store):
    pitch = _seg_pitch(GRID_W)
    nslab = v.shape[1] // LANES
    for r in range(nb):
        for c in range(nslab):
            buf[c, r * pitch:r * pitch + GRID_W, :] = v[r * GRID_W:(r + 1) * GRID_W, c * LANES:(c + 1) * LANES]
    for pos in range(GRID_W):
        for c in range(nslab):
            store(pos, c, buf[c, pl.ds(pos, nb, stride=pitch), :])


def _stage_b_kernel(x_ref, sh_ref, sc_ref, g_ref, w_ref, j_ref, f_ref, xs_ref, gg_ref, fbuf, *, df, dr):
    h = _rms(x_ref[0], g_ref[...]) * (1.0 + sc_ref[0]) + sh_ref[0]
    hb = h.astype(BF16)
    f = jnp.dot(hb, w_ref[:, :df], preferred_element_type=F32)

    def store_f(pos, c, tile):
        f_ref[0, :, pos * df + c * LANES:pos * df + (c + 1) * LANES] = tile.astype(BF16)
    _transpose_row_blocks(f, fbuf, f.shape[0] // GRID_W, store_f)
    xr = jnp.dot(hb, w_ref[:, df:df + dr], preferred_element_type=F32).astype(BF16)
    tm = xr.shape[0]
    for r in range(tm // GRID_W):
        blk = xr[r * GRID_W:(r + 1) * GRID_W]
        if r % 2 == 1:
            blk = jnp.dot(j_ref[...], blk, preferred_element_type=F32).astype(BF16)
        xs_ref[0, r * GRID_W:(r + 1) * GRID_W, :] = blk
    gr = jnp.dot(hb, w_ref[:, df + dr:], preferred_element_type=F32)
    gg_ref[0] = _gelu_tanh(gr).astype(BF16)


def _stage_b(x, shift, scale, g, w_in_bf, jmat, df, dr, tm):
    b, s, d = x.shape
    n = w_in_bf.shape[1]
    vec = pl.BlockSpec((1, 1, d), lambda i, t: (i, 0, 0))
    out = pl.BlockSpec((1, tm, df), lambda i, t: (i, t, 0))
    nb = tm // GRID_W
    tok = jax.ShapeDtypeStruct((b, s, df), BF16)
    return pl.pallas_call(
        functools.partial(_stage_b_kernel, df=df, dr=dr),
        grid=(b, s // tm),
        in_specs=[pl.BlockSpec((1, tm, d), lambda i, t: (i, t, 0)), vec, vec,
                  pl.BlockSpec((1, d), lambda i, t: (0, 0)),
                  pl.BlockSpec((d, n), lambda i, t: (0, 0)),
                  pl.BlockSpec((GRID_W, GRID_W), lambda i, t: (0, 0))],
        out_specs=[pl.BlockSpec((1, nb, GRID_W * df), lambda i, t: (i, t, 0)), out, out],
        out_shape=[jax.ShapeDtypeStruct((b, s // GRID_W, GRID_W * df), BF16), tok, tok],
        scratch_shapes=[pltpu.VMEM((df // LANES, nb * _seg_pitch(GRID_W), LANES), F32)],
        compiler_params=_cparams(("parallel", "arbitrary")),
        name="stage_b",
    )(x, shift, scale, g.reshape(1, d), w_in_bf, jmat)


def _stage_b_ctx_kernel(x_ref, sh_ref, sc_ref, g_ref, w_ref, xr_ref):
    h = _rms(x_ref[0], g_ref[...]) * (1.0 + sc_ref[0]) + sh_ref[0]
    xr_ref[0] = jnp.dot(h.astype(BF16), w_ref[...], preferred_element_type=F32).astype(BF16)


def _stage_b_ctx(ctx, shift, scale, g, w_xr_bf):
    b, s, d = ctx.shape
    dr = w_xr_bf.shape[1]
    vec = pl.BlockSpec((1, 1, d), lambda i: (0, 0, 0))
    return pl.pallas_call(
        _stage_b_ctx_kernel,
        grid=(b,),
        in_specs=[pl.BlockSpec((1, s, d), lambda i: (i, 0, 0)), vec, vec,
                  pl.BlockSpec((1, d), lambda i: (0, 0)),
                  pl.BlockSpec((d, dr), lambda i: (0, 0))],
        out_specs=pl.BlockSpec((1, s, dr), lambda i: (i, 0, 0)),
        out_shape=jax.ShapeDtypeStruct((b, s, dr), BF16),
        compiler_params=_cparams(("arbitrary",)),
        name="stage_b_ctx",
    )(ctx, shift, scale, g.reshape(1, d), w_xr_bf)


HALO = 16


def _sigmoid(x):
    return 0.5 * jnp.tanh(0.5 * x) + 0.5


def _rg_kernel(xs_ref, h0_ref, cw_ref, cb_ref, w_ref, b_ref, lam_ref, pm_ref, pmt_ref, out_ref, hfin_ref,
               hf_s, xc_s, a_s, u_s, hl_s, p_s, c_s, hc_s, *, tc, nchunk, seq, dr):
    p = pl.program_id(1)
    j = pl.program_id(2)
    cidx = jnp.where(p == 0, j, nchunk - 1 - j)
    start = pl.multiple_of(cidx * tc, tc)
    nseg = SUBLANES
    sl = tc // nseg
    sub = lax.broadcasted_iota(jnp.int32, (nseg, dr), 0)

    @pl.when(p == 0)
    def _():
        xp = jnp.dot(pm_ref[...], xs_ref[0, pl.ds(start, tc), :], preferred_element_type=F32)
        pstart = pl.multiple_of(jnp.maximum(start - HALO, 0), HALO)
        nstart = pl.multiple_of(jnp.minimum(start + tc, seq - HALO), HALO)
        prev = xs_ref[0, pl.ds(pstart, HALO), :].astype(F32)
        nxt = xs_ref[0, pl.ds(nstart, HALO), :].astype(F32)
        prev = jnp.where(cidx > 0, prev, 0.0)
        nxt = jnp.where(cidx < nchunk - 1, nxt, 0.0)
        tm2 = jnp.where(sub == 0, prev[HALO - 2:HALO - 1], pltpu.roll(xp[(sl - 2) * nseg:(sl - 1) * nseg], 1, 0))
        tm1 = jnp.where(sub == 0, prev[HALO - 1:HALO], pltpu.roll(xp[(sl - 1) * nseg:sl * nseg], 1, 0))
        tp1 = jnp.where(sub == nseg - 1, nxt[0:1], pltpu.roll(xp[0:nseg], nseg - 1, 0))
        ext = jnp.concatenate([tm2, tm1, xp, tp1], axis=0)
        xc = cb_ref[...] + cw_ref[0:1, :] * ext[0:tc]
        for k in range(1, CONV_W):
            xc = xc + cw_ref[k:k + 1, :] * ext[k * nseg:k * nseg + tc]
        xc_s[pl.ds(start, tc), :] = xc

    xc = xc_s[pl.ds(start, tc), :]
    gates = jnp.dot(xc.astype(BF16), w_ref[p], preferred_element_type=F32) + b_ref[p]
    r = _sigmoid(gates[:, :dr])
    i = _sigmoid(gates[:, dr:])
    log_a = (-RG_C) * r * jax.nn.softplus(-lam_ref[p])
    a = jnp.exp(log_a)
    a_s[...] = a
    u_s[...] = jnp.sqrt(-jnp.tanh(log_a) * (1.0 + a * a)) * (i * xc)

    @pl.when(jnp.logical_and(p == 0, j == 0))
    def _():
        hfin_ref[...] = jnp.zeros_like(hfin_ref)

    @pl.when(j == 0)
    def _():
        hc_s[0:1, :] = h0_ref[0, pl.ds(p, 1), :]

    def segment_scan(reverse):
        def body(q, carry):
            t = (sl - 1 - q) if reverse else q
            rows = pl.ds(pl.multiple_of(t * nseg, nseg), nseg)
            h, pr = carry
            av = a_s[rows, :]
            h = av * h + u_s[rows, :]
            pr = av * pr
            hl_s[rows, :] = h
            p_s[rows, :] = pr
            return h, pr
        h_end, p_end = lax.fori_loop(0, sl, body, (jnp.zeros((nseg, dr), F32), jnp.ones((nseg, dr), F32)),
                                     unroll=4)
        carry = hc_s[0:1, :]
        for g in (range(nseg - 1, -1, -1) if reverse else range(nseg)):
            c_s[g:g + 1, :] = carry
            carry = h_end[g:g + 1, :] + p_end[g:g + 1, :] * carry
        hc_s[0:1, :] = carry
        return c_s[...]

    def corrected(cin):
        h = hl_s[...].reshape(sl, nseg, dr) + p_s[...].reshape(sl, nseg, dr) * cin[None]
        return h.reshape(tc, dr)

    @pl.when(p == 0)
    def _():
        hf_s[pl.ds(start, tc), :] = corrected(segment_scan(False))

    @pl.when(p == 1)
    def _():
        tot = corrected(segment_scan(True)) + hf_s[pl.ds(start, tc), :]
        out_ref[0] = jnp.dot(pmt_ref[...], tot.astype(BF16), preferred_element_type=F32).astype(BF16)

    @pl.when(j == nchunk - 1)
    def _():
        hfin_ref[0, pl.ds(p, 1), :] = hc_s[0:1, :]


def _rg_scan(xs, h0, conv_w, conv_b, wcat, bcat, lam, tc):
    b, s, dr = xs.shape
    nchunk = s // tc
    last = nchunk - 1
    sl = tc // SUBLANES
    src = (np.arange(tc) % SUBLANES) * sl + np.arange(tc) // SUBLANES
    pm = np.zeros((tc, tc), np.float32)
    pm[np.arange(tc), src] = 1.0
    chunk_buf = pltpu.VMEM((tc, dr), F32)
    full2 = lambda shape: pl.BlockSpec(shape, lambda i, p, j: (0,) * len(shape))
    return pl.pallas_call(
        functools.partial(_rg_kernel, tc=tc, nchunk=nchunk, seq=s, dr=dr),
        grid=(b, 2, nchunk),
        in_specs=[pl.BlockSpec((1, s, dr), lambda i, p, j: (i, 0, 0)),
                  pl.BlockSpec((1, SUBLANES, dr), lambda i, p, j: (i, 0, 0)),
                  full2((CONV_W, dr)), full2((1, dr)),
                  full2((2, dr, 2 * dr)), full2((2, 1, 2 * dr)), full2((2, 1, dr)),
                  full2((tc, tc)), full2((tc, tc))],
        out_specs=[pl.BlockSpec((1, tc, dr), lambda i, p, j: (i, jnp.where(p == 0, last, last - j), 0)),
                   pl.BlockSpec((1, SUBLANES, dr), lambda i, p, j: (i, 0, 0))],
        out_shape=[jax.ShapeDtypeStruct((b, s, dr), BF16),
                   jax.ShapeDtypeStruct((b, SUBLANES, dr), F32)],
        scratch_shapes=[pltpu.VMEM((s, dr), F32), pltpu.VMEM((s, dr), F32),
                        chunk_buf, chunk_buf, chunk_buf, chunk_buf,
                        pltpu.VMEM((SUBLANES, dr), F32), pltpu.VMEM((SUBLANES, dr), F32)],
        compiler_params=_cparams(("arbitrary", "arbitrary", "arbitrary")),
        name="rg_scan",
    )(xs, h0, conv_w, conv_b.reshape(1, dr), wcat, bcat, lam, jnp.asarray(pm, BF16), jnp.asarray(pm.T, BF16))


def _f1_kernel(d_ref, x_ref, y_ref):
    y_ref[0] = jnp.dot(d_ref[...], x_ref[0], preferred_element_type=F32).astype(BF16)


def _fourier_stage1(fv, d2, tl):
    b, r, n = fv.shape
    return pl.pallas_call(
        _f1_kernel,
        grid=(b, n // tl),
        in_specs=[pl.BlockSpec((2 * r, r), lambda i, l: (0, 0)),
                  pl.BlockSpec((1, r, tl), lambda i, l: (i, 0, l))],
        out_specs=pl.BlockSpec((1, 2 * r, tl), lambda i, l: (i, 0, l)),
        out_shape=jax.ShapeDtypeStruct((b, 2 * r, n), BF16),
        compiler_params=_cparams(("parallel", "arbitrary")),
        name="fourier_stage1",
    )(d2, fv)


def _f2_kernel(y_ref, e_ref, bc_ref, bs_ref, g_ref, o_ref, obuf, *, kb, df):
    zr, zi = [], []
    for q in range(kb):
        yk = jnp.concatenate([y_ref[0, 0, q], y_ref[0, 1, q]], axis=0)
        z = jnp.dot(e_ref[q], yk, preferred_element_type=F32)
        zr.append(z[:GRID_W])
        zi.append(z[GRID_W:])
    zr = jnp.concatenate(zr, axis=0).astype(BF16)
    zi = jnp.concatenate(zi, axis=0).astype(BF16)
    o = jnp.dot(zr, bc_ref[...], preferred_element_type=F32)
    o += jnp.dot(zi, bs_ref[...], preferred_element_type=F32)
    on = _rms(o, g_ref[...])

    def store_o(pos, c, tile):
        o_ref[0, pos, :, c * LANES:(c + 1) * LANES] = tile.astype(BF16)
    _transpose_row_blocks(on, obuf, kb, store_o)


def _fourier_stage2(y5, etab, bdc, bds, g, kb):
    b, _, r, w, df = y5.shape
    return pl.pallas_call(
        functools.partial(_f2_kernel, kb=kb, df=df),
        grid=(b, r // kb),
        in_specs=[pl.BlockSpec((1, 2, kb, w, df), lambda i, k: (i, 0, k, 0, 0)),
                  pl.BlockSpec((kb, 2 * w, 2 * w), lambda i, k: (k, 0, 0)),
                  pl.BlockSpec((df, df), lambda i, k: (0, 0)),
                  pl.BlockSpec((df, df), lambda i, k: (0, 0)),
                  pl.BlockSpec((1, df), lambda i, k: (0, 0))],
        out_specs=pl.BlockSpec((1, w, kb, df), lambda i, k: (i, 0, k, 0)),
        out_shape=jax.ShapeDtypeStruct((b, w, r, df), BF16),
        scratch_shapes=[pltpu.VMEM((df // LANES, kb * _seg_pitch(GRID_W), LANES), F32)],
        compiler_params=_cparams(("parallel", "arbitrary")),
        name="fourier_stage2",
    )(y5, etab, bdc, bds, g.reshape(1, df))


def _stage_m_kernel(fn_ref, hs_ref, gg_ref, x_ref, gtm_ref, shf_ref, scf_ref, gr_ref, gffn_ref,
                    wo_ref, wr_ref, br_ref, j_ref, x1_ref, h2_ref, idx_ref, gate_ref, *, df):
    tm = x_ref.shape[1]
    hs = hs_ref[0]
    blocks = []
    for r in range(tm // GRID_W):
        blk = hs[r * GRID_W:(r + 1) * GRID_W]
        if r % 2 == 1:
            blk = jnp.dot(j_ref[...], blk, preferred_element_type=F32)
        blocks.append(blk.astype(F32))
    rg = jnp.concatenate(blocks, axis=0) * gg_ref[0].astype(F32)
    rgn = _rms(rg, gr_ref[...]).astype(BF16)
    mix = jnp.dot(fn_ref[0], wo_ref[:df, :], preferred_element_type=F32)
    mix += jnp.dot(rgn, wo_ref[df:, :], preferred_element_type=F32)
    x1 = x_ref[0] + gtm_ref[0] * mix
    x1_ref[0] = x1
    h2 = _rms(x1, gffn_ref[...]) * (1.0 + scf_ref[0]) + shf_ref[0]
    h2_ref[0] = _pack_halves(h2)

    logits = _dot3_nt(wr_ref[...], h2) + br_ref[...]
    eidx = lax.broadcasted_iota(jnp.int32, logits.shape, 0)
    vals, idxs = [], []
    for _ in range(TOP_K):
        m = jnp.max(logits, axis=0, keepdims=True)
        sel = jnp.min(jnp.where(logits == m, eidx, N_EXPERTS), axis=0, keepdims=True)
        vals.append(m)
        idxs.append(sel)
        logits = jnp.where(eidx == sel, -jnp.inf, logits)
    ex = [jnp.exp(v - vals[0]) for v in vals]
    den = ex[0] + ex[1] + ex[2] + ex[3]
    for k in range(TOP_K):
        gate_ref[k:k + 1, :] = ex[k] / den
        idx_ref[k:k + 1, :] = idxs[k]


def _stage_m(fn, hs, gg, x, gt_m, sh_f, sc_f, g_out_r, g_ffn, w_out_bf, w_router_t, b_router, jmat, tm):
    b, s, d = x.shape
    df = fn.shape[2]
    dr = hs.shape[2]
    nt = s // tm
    ne = w_router_t.shape[0]
    vec = pl.BlockSpec((1, 1, d), lambda i, t: (i, 0, 0))
    half = lambda dd: pl.BlockSpec((1, tm, dd), lambda i, t: (i, t, 0))
    full = lambda shape: pl.BlockSpec(shape, lambda i, t: (0,) * len(shape))
    tok = pl.BlockSpec((TOP_K, tm), lambda i, t: (0, i * nt + t))
    return pl.pallas_call(
        functools.partial(_stage_m_kernel, df=df),
        grid=(b, nt),
        in_specs=[half(df), half(dr), half(dr), half(d), vec, vec, vec,
                  full((1, dr)), full((1, d)), full((d, d)), full((ne, d)), full((ne, 1)),
                  full((GRID_W, GRID_W))],
        out_specs=[half(d), half(d // 2), tok, tok],
        out_shape=[jax.ShapeDtypeStruct((b, s, d), F32), jax.ShapeDtypeStruct((b, s, d // 2), jnp.int32),
                   jax.ShapeDtypeStruct((TOP_K, b * s), jnp.int32),
                   jax.ShapeDtypeStruct((TOP_K, b * s), F32)],
        compiler_params=_cparams(("parallel", "arbitrary")),
        name="stage_m",
    )(fn, hs, gg, x, gt_m, sh_f, sc_f, g_out_r.reshape(1, dr), g_ffn.reshape(1, d), w_out_bf,
      w_router_t, b_router.reshape(ne, 1), jmat)


def _rank_kernel(idx_ref, tri_ref, rank_ref, cnt_ref, carry_s):
    c = pl.program_id(0)

    @pl.when(c == 0)
    def _():
        carry_s[...] = jnp.zeros_like(carry_s)

    l = idx_ref.shape[1]
    eidx = lax.broadcasted_iota(jnp.int32, (N_EXPERTS, l), 0)
    for k in range(TOP_K):
        onehot = eidx == idx_ref[k:k + 1, :]
        oh = jnp.where(onehot, 1.0, 0.0)
        prefix = jnp.dot(oh.astype(BF16), tri_ref[...], preferred_element_type=F32)
        carry = carry_s[:, 0:1]
        rank = jnp.sum(jnp.where(onehot, prefix - 1.0 + carry, 0.0), axis=0, keepdims=True)
        rank_ref[k:k + 1, :] = rank.astype(jnp.int32)
        carry_s[...] = carry_s[...] + jnp.sum(oh, axis=1, keepdims=True)
    cnt_ref[...] = carry_s[...].astype(jnp.int32)


def _dispatch_ranks(idx, tri, tl):
    k, t = idx.shape
    return pl.pallas_call(
        _rank_kernel,
        grid=(t // tl,),
        in_specs=[pl.BlockSpec((k, tl), lambda c: (0, c)),
                  pl.BlockSpec((tl, tl), lambda c: (0, 0))],
        out_specs=[pl.BlockSpec((k, tl), lambda c: (0, c)),
                   pl.BlockSpec((N_EXPERTS, LANES), lambda c: (0, 0))],
        out_shape=[jax.ShapeDtypeStruct((k, t), jnp.int32),
                   jax.ShapeDtypeStruct((N_EXPERTS, LANES), jnp.int32)],
        scratch_shapes=[pltpu.VMEM((N_EXPERTS, LANES), F32)],
        compiler_params=_cparams(("arbitrary",)),
        name="dispatch_ranks",
    )(idx, tri)


def _moe_kernel(be_ref, bv_ref, nu_ref, x_ref, wgu_ref, bgu_ref, wd_ref, bd_ref, o_ref, wgu_s, wd_s, *, dff):
    i = pl.program_id(0)
    h = x_ref.shape[1]

    @pl.when(jnp.logical_or(i == 0, be_ref[i] != be_ref[jnp.maximum(i - 1, 0)]))
    def _():
        wgu_s[...] = wgu_ref[0].astype(BF16)
        wd_s[...] = wd_ref[0].astype(BF16)

    @pl.when(i < nu_ref[0])
    def _():
        rows = lax.broadcasted_iota(jnp.int32, x_ref.shape, 0)
        xw = jnp.where(rows < bv_ref[i], x_ref[...], 0)
        xa, xb = _unpack_halves(xw)
        gu = jnp.dot(xa.astype(BF16), wgu_s[:h, :], preferred_element_type=F32)
        gu += jnp.dot(xb.astype(BF16), wgu_s[h:, :], preferred_element_type=F32)
        gu += bgu_ref[0]
        gt = jnp.minimum(gu[:, :dff], SWIGLU_LIMIT)
        up = jnp.clip(gu[:, dff:], -SWIGLU_LIMIT, SWIGLU_LIMIT)
        act = (up + 1.0) * (gt * _sigmoid(SWIGLU_ALPHA * gt))
        out = jnp.dot(act.astype(BF16), wd_s[...], preferred_element_type=F32) + bd_ref[0]
        o_ref[...] = _pack_halves(out)


def _moe_experts(blk_expert, blk_valid, n_used, xs, wgu, bgu, wd, bd, tmm):
    cap, h = xs.shape
    ne, d, dff2 = wgu.shape
    dff = dff2 // 2
    row_blk = lambda i, be, bv, nu: (jnp.minimum(i, nu[0] - 1), 0)
    wsel = lambda i, be, bv, nu: (be[i], 0, 0)
    grid_spec = pltpu.PrefetchScalarGridSpec(
        num_scalar_prefetch=3,
        grid=(cap // tmm,),
        in_specs=[pl.BlockSpec((tmm, h), row_blk),
                  pl.BlockSpec((1, d, dff2), wsel),
                  pl.BlockSpec((1, 1, dff2), wsel),
                  pl.BlockSpec((1, dff, d), wsel),
                  pl.BlockSpec((1, 1, d), wsel)],
        out_specs=pl.BlockSpec((tmm, h), row_blk),
        scratch_shapes=[pltpu.VMEM((d, dff2), BF16), pltpu.VMEM((dff, d), BF16)],
    )
    return pl.pallas_call(
        functools.partial(_moe_kernel, dff=dff),
        grid_spec=grid_spec,
        out_shape=jax.ShapeDtypeStruct((cap, h), jnp.int32),
        compiler_params=_cparams(("arbitrary",)),
        name="moe_experts",
    )(blk_expert, blk_valid, n_used, xs, wgu, bgu.reshape(ne, 1, dff2), wd, bd.reshape(ne, 1, d))


SC_CHUNK = 64


def _sc_workers():
    info = plsc.get_sparse_core_info()
    return info.num_cores, info.num_subcores


def _sc_dispatch(rows, dest, cap):
    n, w = rows.shape
    topk = dest.shape[0] // n
    nc, ns = _sc_workers()
    per_w = n // (nc * ns)
    assert per_w % SC_CHUNK == 0
    mesh = plsc.VectorSubcoreMesh(core_axis_name="c", subcore_axis_name="s")

    @functools.partial(
        pl.kernel, mesh=mesh,
        out_type=jax.ShapeDtypeStruct((cap, w), jnp.int32),
        scratch_types=[pltpu.VMEM((SC_CHUNK,), jnp.int32),
                       pltpu.VMEM((SC_CHUNK, w), jnp.int32),
                       pltpu.SemaphoreType.DMA],
    )
    def scatter_rows(rows_hbm, dest_hbm, out_hbm, idx_v, rows_v, sem):
        base = (lax.axis_index("s") * nc + lax.axis_index("c")) * per_w

        @pl.loop(0, per_w // SC_CHUNK)
        def _(j):
            off = pl.multiple_of(base + j * SC_CHUNK, SC_CHUNK)
            pltpu.sync_copy(rows_hbm.at[pl.ds(off, SC_CHUNK)], rows_v)
            for k in range(topk):
                pltpu.sync_copy(dest_hbm.at[pl.ds(k * n + off, SC_CHUNK)], idx_v)
                pltpu.async_copy(rows_v, out_hbm.at[idx_v], sem).wait()

    return scatter_rows(rows, dest)


def _sc_gather(table, idx):
    n = idx.shape[0]
    w = table.shape[1]
    nc, ns = _sc_workers()
    per_w = n // (nc * ns)
    assert per_w % SC_CHUNK == 0
    mesh = plsc.VectorSubcoreMesh(core_axis_name="c", subcore_axis_name="s")

    @functools.partial(
        pl.kernel, mesh=mesh,
        out_type=jax.ShapeDtypeStruct((n, w), jnp.int32),
        scratch_types=[pltpu.VMEM((SC_CHUNK,), jnp.int32),
                       pltpu.VMEM((SC_CHUNK, w), jnp.int32),
                       pltpu.SemaphoreType.DMA],
    )
    def gather_rows(table_hbm, idx_hbm, out_hbm, idx_v, rows_v, sem):
        base = (lax.axis_index("s") * nc + lax.axis_index("c")) * per_w

        @pl.loop(0, per_w // SC_CHUNK)
        def _(j):
            off = pl.multiple_of(base + j * SC_CHUNK, SC_CHUNK)
            pltpu.sync_copy(idx_hbm.at[pl.ds(off, SC_CHUNK)], idx_v)
            pltpu.async_copy(table_hbm.at[idx_v], rows_v, sem).wait()
            pltpu.sync_copy(rows_v, out_hbm.at[pl.ds(off, SC_CHUNK)])

    return gather_rows(table, idx)


def _combine_kernel(x1_ref, y_ref, gate_ref, gtf_ref, g_ref, *rest):
    o_ref = rest[-1]
    h = y_ref.shape[3]
    gates = gate_ref[0]
    moe_a = moe_b = None
    for k in range(TOP_K):
        ya, yb = _unpack_halves(y_ref[k, 0])
        gk = gates[:, k:k + 1]
        moe_a = gk * ya if k == 0 else moe_a + gk * ya
        moe_b = gk * yb if k == 0 else moe_b + gk * yb
    za = x1_ref[0, :, :h] + gtf_ref[0, :, :h] * moe_a
    zb = x1_ref[0, :, h:] + gtf_ref[0, :, h:] * moe_b
    ms = (jnp.sum(za * za, axis=-1, keepdims=True) + jnp.sum(zb * zb, axis=-1, keepdims=True)) / (2 * h)
    inv = lax.rsqrt(ms + EPS)
    o_ref[0, :, :h] = za * inv * g_ref[:, :h]
    o_ref[0, :, h:] = zb * inv * g_ref[:, h:]


def _combine(x1, yk, gates_t, gt_f, g_final, tm, b0, prev_out):
    b, s, d = x1.shape
    _, bp, _, h = yk.shape
    in_specs = [pl.BlockSpec((1, tm, d), lambda i, t: (i + b0, t, 0)),
                pl.BlockSpec((TOP_K, 1, tm, h), lambda i, t: (0, i, t, 0)),
                pl.BlockSpec((1, tm, TOP_K), lambda i, t: (i + b0, t, 0)),
                pl.BlockSpec((1, 1, d), lambda i, t: (i + b0, 0, 0)),
                pl.BlockSpec((1, d), lambda i, t: (0, 0))]
    args = [x1, yk, gates_t, gt_f, g_final.reshape(1, d)]
    aliases = {}
    if prev_out is not None:
        in_specs.append(pl.BlockSpec(memory_space=pl.ANY))
        args.append(prev_out)
        aliases = {len(args) - 1: 0}
    return pl.pallas_call(
        _combine_kernel,
        grid=(bp, s // tm),
        in_specs=in_specs,
        out_specs=pl.BlockSpec((1, tm, d), lambda i, t: (i + b0, t, 0)),
        out_shape=jax.ShapeDtypeStruct((b, s, d), F32),
        input_output_aliases=aliases,
        compiler_params=_cparams(("parallel", "arbitrary")),
        name="combine",
    )(*args)


def _dft_tables(rows, gd):
    seq = rows * GRID_W
    n = np.arange(rows)
    ang1 = 2.0 * np.pi * np.outer(n, n) / rows
    d2 = np.concatenate([np.cos(ang1), -np.sin(ang1)], axis=0)
    k1 = np.arange(rows)[:, None, None]
    k2 = np.arange(GRID_W)[None, :, None]
    n2 = np.arange(GRID_W)[None, None, :]
    ang2 = 2.0 * np.pi * ((n2 * (k1 + rows * k2)) % seq) / seq
    ec, es = np.cos(ang2), np.sin(ang2)
    etab = np.concatenate([np.concatenate([ec, es], axis=2),
                           np.concatenate([-es, ec], axis=2)], axis=1)
    c = np.arange(gd)
    angc = 2.0 * np.pi * np.outer(c, c) / gd
    scale = 1.0 / np.sqrt(seq * gd)
    return (jnp.asarray(d2, BF16), jnp.asarray(etab, BF16),
            jnp.asarray(np.cos(angc) * scale, F32), jnp.asarray(np.sin(angc) * scale, F32))


def _block_diag(w):
    h, i, o = w.shape
    eye = jnp.eye(h, dtype=w.dtype)
    return (eye[:, None, :, None] * w[:, :, None, :]).reshape(h * i, h * o)


def kernel(x, c, ctx, c_ctx, w_mod, b_mod, g_norm_mix, g_norm_ffn, w_in, w_fourier, conv_w, conv_b,
           rg_w_a, rg_b_a, rg_w_x, rg_b_x, rg_lam, g_out_fourier, g_out_rg, w_out, w_router, b_router,
           w_gate_up, b_gate_up, w_down, b_down, g_final):
    assert w_mod.shape[0] == 1, "single-layer stack only"
    b, s, d = x.shape
    df = w_fourier.shape[1] * w_fourier.shape[2]
    dr = conv_w.shape[2]
    gd = w_fourier.shape[2]
    rows = s // GRID_W
    t = b * s
    ne = w_router.shape[2]

    mrows = -(-(b + 1) // SUBLANES) * SUBLANES
    cond = jnp.zeros((mrows, d), F32).at[:b].set(c).at[b].set(c_ctx)
    mod = _adaln(cond, w_mod[0], b_mod[0])
    sh_m, sc_m, gt_m, sh_f, sc_f, gt_f = [mod[:b, k * d:(k + 1) * d].reshape(b, 1, d) for k in range(N_MOD)]
    csh_m = mod[b:b + 1, 0:d].reshape(1, 1, d)
    csc_m = mod[b:b + 1, d:2 * d].reshape(1, 1, d)

    tm = min(s, TOKEN_TILE)
    d2, etab, cmat, smat = _dft_tables(rows, gd)
    jmat = jnp.asarray(np.eye(GRID_W)[::-1].copy(), BF16)

    w_in_bf = w_in[0].astype(BF16)
    f, xs, gg = _stage_b(x, sh_m, sc_m, g_norm_mix[0], w_in_bf, jmat, df, dr, tm=tm)
    xr_ctx = _stage_b_ctx(ctx, csh_m, csc_m, g_norm_mix[0], w_in_bf[:, df:df + dr])

    wcat = jnp.stack([jnp.concatenate([_block_diag(rg_w_a[0, dd]), _block_diag(rg_w_x[0, dd])], axis=1)
                      for dd in range(2)]).astype(BF16)
    bcat = jnp.concatenate([rg_b_a[0], rg_b_x[0]], axis=1).reshape(2, 1, 2 * dr)
    lam = rg_lam[0].reshape(2, 1, dr)
    h0 = jnp.zeros((b, SUBLANES, dr), F32)
    _, hfin_ctx = _rg_scan(xr_ctx, h0, conv_w[0], conv_b[0], wcat, bcat, lam, tc=ctx.shape[1])
    hs, _ = _rg_scan(xs, hfin_ctx, conv_w[0], conv_b[0], wcat, bcat, lam, tc=min(s, SCAN_CHUNK))

    cw, sw = _fold_fourier(cmat, smat, w_fourier[0])
    bdc = _block_diag(cw).astype(BF16)
    bds = _block_diag(sw).astype(BF16)
    y = _fourier_stage1(f, d2, tl=min(GRID_W * df, 8192))
    fn = _fourier_stage2(y.reshape(b, 2, rows, GRID_W, df), etab, bdc, bds, g_out_fourier[0],
                         kb=min(rows, 16))
    fn = fn.reshape(b, s, df)

    x1, h2, idx, gates = _stage_m(fn, hs, gg, x, gt_m, sh_f, sc_f, g_out_rg[0], g_norm_ffn[0],
                                  w_out[0].astype(BF16), w_router[0].T, b_router[0], jmat, tm=tm)

    tl = min(t, RANK_TILE)
    tri = jnp.asarray(np.triu(np.ones((tl, tl))), BF16)
    rank, cnt = _dispatch_ranks(idx, tri, tl)
    counts = cnt[:, 0]
    tmm = MOE_ROW_TILE
    padded = (counts + tmm - 1) // tmm * tmm
    pad_end = jnp.cumsum(padded)
    pad_start = pad_end - padded
    eids = jnp.arange(ne, dtype=jnp.int32)
    dest = rank + jnp.sum(jnp.where(idx[:, :, None] == eids, pad_start, 0), axis=-1)
    n_blocks = -(-(t * TOP_K) // tmm) + ne
    cap = n_blocks * tmm
    n_used = (pad_end[-1] // tmm).astype(jnp.int32).reshape(1)
    blk_start = jnp.arange(n_blocks, dtype=jnp.int32) * tmm
    blk_expert = jnp.sum(blk_start[:, None] >= pad_end[None, :], axis=1).astype(jnp.int32)
    last_expert = jnp.sum(pad_end[-1] - tmm >= pad_end).astype(jnp.int32)
    blk_expert = jnp.minimum(blk_expert, last_expert)
    sel = blk_expert[:, None] == eids
    blk_first = jnp.sum(jnp.where(sel, pad_start, 0), axis=1)
    blk_count = jnp.sum(jnp.where(sel, counts, 0), axis=1)
    blk_valid = jnp.clip(blk_count - (blk_start - blk_first), 0, tmm).astype(jnp.int32)

    x_sorted = _sc_dispatch(h2.reshape(t, d // 2), dest.reshape(-1), cap)
    y_sorted = _moe_experts(blk_expert, blk_valid, n_used, x_sorted,
                            w_gate_up[0], b_gate_up[0], w_down[0], b_down[0], tmm)
    gates_t = gates.T.reshape(b, s, TOP_K)
    dest_b = dest.reshape(TOP_K, b, s)
    bp = b // COMBINE_GROUPS if b % COMBINE_GROUPS == 0 else b
    out = None
    for b0 in range(0, b, bp):
        yk = _sc_gather(y_sorted, dest_b[:, b0:b0 + bp].reshape(-1)).reshape(TOP_K, bp, s, d // 2)
        out = _combine(x1, yk, gates_t, gt_f, g_final, tm, b0, out)
    return out
```

```python
import functools

import numpy as np
import jax
import jax.numpy as jnp
from jax import lax
from jax.experimental import pallas as pl
from jax.experimental.pallas import tpu as pltpu
from jax.experimental.pallas import tpu_sc as plsc

GRID_W = 64
FOURIER_GROUPS = 4
RG_HEADS = 8
CONV_W = 4
CONV_PAD_LO = 2
RG_C = 8.0
N_EXPERTS = 32
TOP_K = 4
SWIGLU_LIMIT = 7.0
SWIGLU_ALPHA = 1.702
N_MOD = 6
EPS = 1e-6

LANES = 128
SUBLANES = 8
VMEM_LIMIT_BYTES = 56 * 1024 * 1024
TOKEN_TILE = 1024
RANK_TILE = 512
SCAN_CHUNK = 512
MOE_ROW_TILE = 1024
MOE_ROW_GROUP = 512
COMBINE_GROUPS = 4

F32 = jnp.float32
BF16 = jnp.bfloat16


def _cparams(sem):
    return pltpu.CompilerParams(dimension_semantics=sem, vmem_limit_bytes=VMEM_LIMIT_BYTES)


def _split_bf16(a):
    hi = a.astype(BF16)
    lo = (a - hi.astype(F32)).astype(BF16)
    return hi, lo


def _dot3(a, b):
    ah, al = _split_bf16(a)
    bh, bl = _split_bf16(b)
    out = jnp.dot(ah, bh, preferred_element_type=F32)
    out += jnp.dot(ah, bl, preferred_element_type=F32)
    out += jnp.dot(al, bh, preferred_element_type=F32)
    return out


def _dot3_nt(a, b):
    dn = (((1,), (1,)), ((), ()))
    ah, al = _split_bf16(a)
    bh, bl = _split_bf16(b)
    out = lax.dot_general(ah, bh, dn, preferred_element_type=F32)
    out += lax.dot_general(ah, bl, dn, preferred_element_type=F32)
    out += lax.dot_general(al, bh, dn, preferred_element_type=F32)
    return out


def _gelu_tanh(x):
    return 0.5 * x * (1.0 + jnp.tanh(0.7978845608028654 * (x + 0.044715 * (x * x * x))))


def _rms(x, g):
    return x * lax.rsqrt(jnp.mean(x * x, axis=-1, keepdims=True) + EPS) * g


def _pack_halves(v):
    h = v.shape[1] // 2
    hi = lax.bitcast_convert_type(v[:, :h].astype(BF16).astype(F32), jnp.uint32)
    lo = lax.bitcast_convert_type(v[:, h:].astype(BF16).astype(F32), jnp.uint32)
    return lax.bitcast_convert_type(hi | (lo >> 16), jnp.int32)


def _unpack_halves(w):
    u = lax.bitcast_convert_type(w, jnp.uint32)
    hi = lax.bitcast_convert_type(u & jnp.uint32(0xFFFF0000), F32)
    lo = lax.bitcast_convert_type(u << 16, F32)
    return hi, lo


def _adaln_kernel(c_ref, w_ref, b_ref, o_ref):
    s = c_ref[...]
    s = s * jax.nn.sigmoid(s)
    o_ref[...] = _dot3(s, w_ref[...]) + b_ref[...]


def _adaln(cond, w_mod, b_mod):
    m, d = cond.shape
    n = w_mod.shape[1]
    tn = n // N_MOD
    return pl.pallas_call(
        _adaln_kernel,
        grid=(n // tn,),
        in_specs=[pl.BlockSpec((m, d), lambda i: (0, 0)),
                  pl.BlockSpec((d, tn), lambda i: (0, i)),
                  pl.BlockSpec((1, tn), lambda i: (0, i))],
        out_specs=pl.BlockSpec((m, tn), lambda i: (0, i)),
        out_shape=jax.ShapeDtypeStruct((m, n), F32),
        compiler_params=_cparams(("arbitrary",)),
        name="adaln",
    )(cond, w_mod, b_mod.reshape(1, n))


def _fold_kernel(c_ref, s_ref, w_ref, cw_ref, sw_ref):
    w = w_ref[0]
    cw_ref[0] = _dot3(c_ref[...], w)
    sw_ref[0] = _dot3(s_ref[...], w)


def _fold_fourier(cmat, smat, w_f):
    g, gd, _ = w_f.shape
    spec_m = pl.BlockSpec((gd, gd), lambda i: (0, 0))
    spec_w = pl.BlockSpec((1, gd, gd), lambda i: (i, 0, 0))
    return pl.pallas_call(
        _fold_kernel,
        grid=(g,),
        in_specs=[spec_m, spec_m, spec_w],
        out_specs=[spec_w, spec_w],
        out_shape=[jax.ShapeDtypeStruct((g, gd, gd), F32)] * 2,
        compiler_params=_cparams(("arbitrary",)),
        name="fold_fourier",
    )(cmat, smat, w_f)


def _seg_pitch(seg_len):
    n8 = seg_len // SUBLANES
    return SUBLANES * (n8 + 1 - n8 % 2)


def _pitched_store(v, buf, blk0):
    pitch = _seg_pitch(GRID_W)
    for r in range(v.shape[0] // GRID_W):
        for c in range(v.shape[1] // LANES):
            buf[c, (blk0 + r) * pitch:(blk0 + r) * pitch + GRID_W, :] = (
                v[r * GRID_W:(r + 1) * GRID_W, c * LANES:(c + 1) * LANES])


def _pitched_gather(buf, nb, store):
    pitch = _seg_pitch(GRID_W)
    for pos in range(GRID_W):
        for c in range(buf.shape[0]):
            store(pos, c, buf[c, pl.ds(pos, nb, stride=pitch), :])


def _stage_b_kernel(x_ref, sh_ref, sc_ref, g_ref, w_ref, j_ref, f_ref, xs_ref, gg_ref, fbuf, *, df, dr):
    tm = x_ref.shape[1]
    h = _rms(x_ref[0], g_ref[...]) * (1.0 + sc_ref[0]) + sh_ref[0]
    hb = h.astype(BF16)
    _pitched_store(jnp.dot(hb, w_ref[:, :df], preferred_element_type=F32), fbuf, 0)

    def store_f(pos, c, tile):
        f_ref[0, :, pos * df + c * LANES:pos * df + (c + 1) * LANES] = tile.astype(BF16)
    _pitched_gather(fbuf, tm // GRID_W, store_f)
    xr = jnp.dot(hb, w_ref[:, df:df + dr], preferred_element_type=F32).astype(BF16)
    for r in range(tm // GRID_W):
        blk = xr[r * GRID_W:(r + 1) * GRID_W]
        if r % 2 == 1:
            blk = jnp.dot(j_ref[...], blk, preferred_element_type=F32).astype(BF16)
        xs_ref[0, r * GRID_W:(r + 1) * GRID_W, :] = blk
    gr = jnp.dot(hb, w_ref[:, df + dr:], preferred_element_type=F32)
    gg_ref[0] = _gelu_tanh(gr).astype(BF16)


def _stage_b(x, shift, scale, g, w_in_bf, jmat, df, dr, tm):
    b, s, d = x.shape
    n = w_in_bf.shape[1]
    vec = pl.BlockSpec((1, 1, d), lambda i, t: (i, 0, 0))
    out = pl.BlockSpec((1, tm, df), lambda i, t: (i, t, 0))
    nb = tm // GRID_W
    tok = jax.ShapeDtypeStruct((b, s, df), BF16)
    return pl.pallas_call(
        functools.partial(_stage_b_kernel, df=df, dr=dr),
        grid=(b, s // tm),
        in_specs=[pl.BlockSpec((1, tm, d), lambda i, t: (i, t, 0)), vec, vec,
                  pl.BlockSpec((1, d), lambda i, t: (0, 0)),
                  pl.BlockSpec((d, n), lambda i, t: (0, 0)),
                  pl.BlockSpec((GRID_W, GRID_W), lambda i, t: (0, 0))],
        out_specs=[pl.BlockSpec((1, nb, GRID_W * df), lambda i, t: (i, t, 0)), out, out],
        out_shape=[jax.ShapeDtypeStruct((b, s // GRID_W, GRID_W * df), BF16), tok, tok],
        scratch_shapes=[pltpu.VMEM((df // LANES, nb * _seg_pitch(GRID_W), LANES), F32)],
        compiler_params=_cparams(("parallel", "arbitrary")),
        name="stage_b",
    )(x, shift, scale, g.reshape(1, d), w_in_bf, jmat)


def _stage_b_ctx_kernel(x_ref, sh_ref, sc_ref, g_ref, w_ref, xr_ref):
    h = _rms(x_ref[0], g_ref[...]) * (1.0 + sc_ref[0]) + sh_ref[0]
    xr_ref[0] = jnp.dot(h.astype(BF16), w_ref[...], preferred_element_type=F32).astype(BF16)


def _stage_b_ctx(ctx, shift, scale, g, w_xr_bf):
    b, s, d = ctx.shape
    dr = w_xr_bf.shape[1]
    vec = pl.BlockSpec((1, 1, d), lambda i: (0, 0, 0))
    return pl.pallas_call(
        _stage_b_ctx_kernel,
        grid=(b,),
        in_specs=[pl.BlockSpec((1, s, d), lambda i: (i, 0, 0)), vec, vec,
                  pl.BlockSpec((1, d), lambda i: (0, 0)),
                  pl.BlockSpec((d, dr), lambda i: (0, 0))],
        out_specs=pl.BlockSpec((1, s, dr), lambda i: (i, 0, 0)),
        out_shape=jax.ShapeDtypeStruct((b, s, dr), BF16),
        compiler_params=_cparams(("arbitrary",)),
        name="stage_b_ctx",
    )(ctx, shift, scale, g.reshape(1, d), w_xr_bf)


HALO = 16


def _sigmoid(x):
    return 0.5 * jnp.tanh(0.5 * x) + 0.5


def _rg_kernel(xs_ref, h0_ref, cw_ref, cb_ref, w_ref, b_ref, lam_ref, pm_ref, pmt_ref, out_ref, hfin_ref,
               hf_s, xc_s, a_s, u_s, hl_s, p_s, c_s, hc_s, *, tc, nchunk, seq, dr):
    p = pl.program_id(1)
    j = pl.program_id(2)
    cidx = jnp.where(p == 0, j, nchunk - 1 - j)
    start = pl.multiple_of(cidx * tc, tc)
    nseg = SUBLANES
    sl = tc // nseg
    sub = lax.broadcasted_iota(jnp.int32, (nseg, dr), 0)

    @pl.when(p == 0)
    def _():
        xp = jnp.dot(pm_ref[...], xs_ref[0, pl.ds(start, tc), :], preferred_element_type=F32)
        pstart = pl.multiple_of(jnp.maximum(start - HALO, 0), HALO)
        nstart = pl.multiple_of(jnp.minimum(start + tc, seq - HALO), HALO)
        prev = xs_ref[0, pl.ds(pstart, HALO), :].astype(F32)
        nxt = xs_ref[0, pl.ds(nstart, HALO), :].astype(F32)
        prev = jnp.where(cidx > 0, prev, 0.0)
        nxt = jnp.where(cidx < nchunk - 1, nxt, 0.0)
        tm2 = jnp.where(sub == 0, prev[HALO - 2:HALO - 1], pltpu.roll(xp[(sl - 2) * nseg:(sl - 1) * nseg], 1, 0))
        tm1 = jnp.where(sub == 0, prev[HALO - 1:HALO], pltpu.roll(xp[(sl - 1) * nseg:sl * nseg], 1, 0))
        tp1 = jnp.where(sub == nseg - 1, nxt[0:1], pltpu.roll(xp[0:nseg], nseg - 1, 0))
        ext = jnp.concatenate([tm2, tm1, xp, tp1], axis=0)
        xc = cb_ref[...] + cw_ref[0:1, :] * ext[0:tc]
        for k in range(1, CONV_W):
            xc = xc + cw_ref[k:k + 1, :] * ext[k * nseg:k * nseg + tc]
        xc_s[pl.ds(start, tc), :] = xc

    xc = xc_s[pl.ds(start, tc), :]
    gates = jnp.dot(xc.astype(BF16), w_ref[p], preferred_element_type=F32) + b_ref[p]
    r = _sigmoid(gates[:, :dr])
    i = _sigmoid(gates[:, dr:])
    log_a = (-RG_C) * r * jax.nn.softplus(-lam_ref[p])
    a = jnp.exp(log_a)
    a_s[...] = a
    u_s[...] = jnp.sqrt(-jnp.tanh(log_a) * (1.0 + a * a)) * (i * xc)

    @pl.when(jnp.logical_and(p == 0, j == 0))
    def _():
        hfin_ref[...] = jnp.zeros_like(hfin_ref)

    @pl.when(j == 0)
    def _():
        hc_s[0:1, :] = h0_ref[0, pl.ds(p, 1), :]

    def segment_scan(reverse):
        def body(q, carry):
            t = (sl - 1 - q) if reverse else q
            rows = pl.ds(pl.multiple_of(t * nseg, nseg), nseg)
            h, pr = carry
            av = a_s[rows, :]
            h = av * h + u_s[rows, :]
            pr = av * pr
            hl_s[rows, :] = h
            p_s[rows, :] = pr
            return h, pr
        h_end, p_end = lax.fori_loop(0, sl, body, (jnp.zeros((nseg, dr), F32), jnp.ones((nseg, dr), F32)),
                                     unroll=4)
        carry = hc_s[0:1, :]
        for g in (range(nseg - 1, -1, -1) if reverse else range(nseg)):
            c_s[g:g + 1, :] = carry
            carry = h_end[g:g + 1, :] + p_end[g:g + 1, :] * carry
        hc_s[0:1, :] = carry
        return c_s[...]

    def corrected(cin):
        h = hl_s[...].reshape(sl, nseg, dr) + p_s[...].reshape(sl, nseg, dr) * cin[None]
        return h.reshape(tc, dr)

    @pl.when(p == 0)
    def _():
        hf_s[pl.ds(start, tc), :] = corrected(segment_scan(False))

    @pl.when(p == 1)
    def _():
        tot = corrected(segment_scan(True)) + hf_s[pl.ds(start, tc), :]
        out_ref[0] = jnp.dot(pmt_ref[...], tot.astype(BF16), preferred_element_type=F32).astype(BF16)

    @pl.when(j == nchunk - 1)
    def _():
        hfin_ref[0, pl.ds(p, 1), :] = hc_s[0:1, :]


def _rg_scan(xs, h0, conv_w, conv_b, wcat, bcat, lam, tc):
    b, s, dr = xs.shape
    nchunk = s // tc
    last = nchunk - 1
    sl = tc // SUBLANES
    src = (np.arange(tc) % SUBLANES) * sl + np.arange(tc) // SUBLANES
    pm = np.zeros((tc, tc), np.float32)
    pm[np.arange(tc), src] = 1.0
    chunk_buf = pltpu.VMEM((tc, dr), F32)
    full2 = lambda shape: pl.BlockSpec(shape, lambda i, p, j: (0,) * len(shape))
    return pl.pallas_call(
        functools.partial(_rg_kernel, tc=tc, nchunk=nchunk, seq=s, dr=dr),
        grid=(b, 2, nchunk),
        in_specs=[pl.BlockSpec((1, s, dr), lambda i, p, j: (i, 0, 0)),
                  pl.BlockSpec((1, SUBLANES, dr), lambda i, p, j: (i, 0, 0)),
                  full2((CONV_W, dr)), full2((1, dr)),
                  full2((2, dr, 2 * dr)), full2((2, 1, 2 * dr)), full2((2, 1, dr)),
                  full2((tc, tc)), full2((tc, tc))],
        out_specs=[pl.BlockSpec((1, tc, dr), lambda i, p, j: (i, jnp.where(p == 0, last, last - j), 0)),
                   pl.BlockSpec((1, SUBLANES, dr), lambda i, p, j: (i, 0, 0))],
        out_shape=[jax.ShapeDtypeStruct((b, s, dr), BF16),
                   jax.ShapeDtypeStruct((b, SUBLANES, dr), F32)],
        scratch_shapes=[pltpu.VMEM((s, dr), F32), pltpu.VMEM((s, dr), F32),
                        chunk_buf, chunk_buf, chunk_buf, chunk_buf,
                        pltpu.VMEM((SUBLANES, dr), F32), pltpu.VMEM((SUBLANES, dr), F32)],
        compiler_params=_cparams(("arbitrary", "arbitrary", "arbitrary")),
        name="rg_scan",
    )(xs, h0, conv_w, conv_b.reshape(1, dr), wcat, bcat, lam, jnp.asarray(pm, BF16), jnp.asarray(pm.T, BF16))


def _f1_kernel(d_ref, x_ref, y_ref):
    y_ref[0] = jnp.dot(d_ref[...], x_ref[0], preferred_element_type=F32).astype(BF16)


def _fourier_stage1(fv, d2, tl):
    b, r, n = fv.shape
    return pl.pallas_call(
        _f1_kernel,
        grid=(b, n // tl),
        in_specs=[pl.BlockSpec((2 * r, r), lambda i, l: (0, 0)),
                  pl.BlockSpec((1, r, tl), lambda i, l: (i, 0, l))],
        out_specs=pl.BlockSpec((1, 2 * r, tl), lambda i, l: (i, 0, l)),
        out_shape=jax.ShapeDtypeStruct((b, 2 * r, n), BF16),
        compiler_params=_cparams(("parallel", "arbitrary")),
        name="fourier_stage1",
    )(d2, fv)


def _f2_kernel(y_ref, e_ref, bc_ref, bs_ref, g_ref, o_ref, obuf, *, kb, df):
    zr, zi = [], []
    for q in range(kb):
        yk = jnp.concatenate([y_ref[0, 0, q], y_ref[0, 1, q]], axis=0)
        z = jnp.dot(e_ref[q], yk, preferred_element_type=F32)
        zr.append(z[:GRID_W])
        zi.append(z[GRID_W:])
    zr = jnp.concatenate(zr, axis=0).astype(BF16)
    zi = jnp.concatenate(zi, axis=0).astype(BF16)
    o = jnp.dot(zr, bc_ref[...], preferred_element_type=F32)
    o += jnp.dot(zi, bs_ref[...], preferred_element_type=F32)
    on = _rms(o, g_ref[...])

    def store_o(pos, c, tile):
        o_ref[0, pos, :, c * LANES:(c + 1) * LANES] = tile.astype(BF16)
    _pitched_store(on, obuf, 0)
    _pitched_gather(obuf, kb, store_o)


def _fourier_stage2(y5, etab, bdc, bds, g, kb):
    b, _, r, w, df = y5.shape
    return pl.pallas_call(
        functools.partial(_f2_kernel, kb=kb, df=df),
        grid=(b, r // kb),
        in_specs=[pl.BlockSpec((1, 2, kb, w, df), lambda i, k: (i, 0, k, 0, 0)),
                  pl.BlockSpec((kb, 2 * w, 2 * w), lambda i, k: (k, 0, 0)),
                  pl.BlockSpec((df, df), lambda i, k: (0, 0)),
                  pl.BlockSpec((df, df), lambda i, k: (0, 0)),
                  pl.BlockSpec((1, df), lambda i, k: (0, 0))],
        out_specs=pl.BlockSpec((1, w, kb, df), lambda i, k: (i, 0, k, 0)),
        out_shape=jax.ShapeDtypeStruct((b, w, r, df), BF16),
        scratch_shapes=[pltpu.VMEM((df // LANES, kb * _seg_pitch(GRID_W), LANES), F32)],
        compiler_params=_cparams(("parallel", "arbitrary")),
        name="fourier_stage2",
    )(y5, etab, bdc, bds, g.reshape(1, df))


def _stage_m_kernel(fn_ref, hs_ref, gg_ref, x_ref, gtm_ref, shf_ref, scf_ref, gr_ref, gffn_ref,
                    wo_ref, wr_ref, br_ref, j_ref, x1_ref, h2_ref, idx_ref, gate_ref, *, df):
    tm = x_ref.shape[1]
    hs = hs_ref[0]
    blocks = []
    for r in range(tm // GRID_W):
        blk = hs[r * GRID_W:(r + 1) * GRID_W]
        if r % 2 == 1:
            blk = jnp.dot(j_ref[...], blk, preferred_element_type=F32)
        blocks.append(blk.astype(F32))
    rg = jnp.concatenate(blocks, axis=0) * gg_ref[0].astype(F32)
    rgn = _rms(rg, gr_ref[...]).astype(BF16)
    mix = jnp.dot(fn_ref[0], wo_ref[:df, :], preferred_element_type=F32)
    mix += jnp.dot(rgn, wo_ref[df:, :], preferred_element_type=F32)
    x1 = x_ref[0] + gtm_ref[0] * mix
    x1_ref[0] = x1
    h2 = _rms(x1, gffn_ref[...]) * (1.0 + scf_ref[0]) + shf_ref[0]
    h2_ref[0] = _pack_halves(h2)

    logits = _dot3_nt(wr_ref[...], h2) + br_ref[...]
    eidx = lax.broadcasted_iota(jnp.int32, logits.shape, 0)
    vals, idxs = [], []
    for _ in range(TOP_K):
        m = jnp.max(logits, axis=0, keepdims=True)
        sel = jnp.min(jnp.where(logits == m, eidx, N_EXPERTS), axis=0, keepdims=True)
        vals.append(m)
        idxs.append(sel)
        logits = jnp.where(eidx == sel, -jnp.inf, logits)
    ex = [jnp.exp(v - vals[0]) for v in vals]
    den = ex[0] + ex[1] + ex[2] + ex[3]
    for k in range(TOP_K):
        gate_ref[k:k + 1, :] = ex[k] / den
        idx_ref[k:k + 1, :] = idxs[k]


def _stage_m(fn, hs, gg, x, gt_m, sh_f, sc_f, g_out_r, g_ffn, w_out_bf, w_router_t, b_router, jmat, tm):
    b, s, d = x.shape
    df = fn.shape[2]
    dr = hs.shape[2]
    nt = s // tm
    ne = w_router_t.shape[0]
    vec = pl.BlockSpec((1, 1, d), lambda i, t: (i, 0, 0))
    half = lambda dd: pl.BlockSpec((1, tm, dd), lambda i, t: (i, t, 0))
    full = lambda shape: pl.BlockSpec(shape, lambda i, t: (0,) * len(shape))
    tok = pl.BlockSpec((TOP_K, tm), lambda i, t: (0, i * nt + t))
    return pl.pallas_call(
        functools.partial(_stage_m_kernel, df=df),
        grid=(b, nt),
        in_specs=[half(df), half(dr), half(dr), half(d), vec, vec, vec,
                  full((1, dr)), full((1, d)), full((d, d)), full((ne, d)), full((ne, 1)),
                  full((GRID_W, GRID_W))],
        out_specs=[half(d), half(d // 2), tok, tok],
        out_shape=[jax.ShapeDtypeStruct((b, s, d), F32), jax.ShapeDtypeStruct((b, s, d // 2), jnp.int32),
                   jax.ShapeDtypeStruct((TOP_K, b * s), jnp.int32),
                   jax.ShapeDtypeStruct((TOP_K, b * s), F32)],
        compiler_params=_cparams(("parallel", "arbitrary")),
        name="stage_m",
    )(fn, hs, gg, x, gt_m, sh_f, sc_f, g_out_r.reshape(1, dr), g_ffn.reshape(1, d), w_out_bf,
      w_router_t, b_router.reshape(ne, 1), jmat)


def _rank_kernel(idx_ref, tri_ref, rank_ref, cnt_ref, carry_s):
    c = pl.program_id(0)

    @pl.when(c == 0)
    def _():
        carry_s[...] = jnp.zeros_like(carry_s)

    l = idx_ref.shape[1]
    eidx = lax.broadcasted_iota(jnp.int32, (N_EXPERTS, l), 0)
    for k in range(TOP_K):
        onehot = eidx == idx_ref[k:k + 1, :]
        oh = jnp.where(onehot, 1.0, 0.0)
        prefix = jnp.dot(oh.astype(BF16), tri_ref[...], preferred_element_type=F32)
        carry = carry_s[:, 0:1]
        rank = jnp.sum(jnp.where(onehot, prefix - 1.0 + carry, 0.0), axis=0, keepdims=True)
        rank_ref[k:k + 1, :] = rank.astype(jnp.int32)
        carry_s[...] = carry_s[...] + jnp.sum(oh, axis=1, keepdims=True)
    cnt_ref[...] = carry_s[...].astype(jnp.int32)


def _dispatch_ranks(idx, tri, tl):
    k, t = idx.shape
    return pl.pallas_call(
        _rank_kernel,
        grid=(t // tl,),
        in_specs=[pl.BlockSpec((k, tl), lambda c: (0, c)),
                  pl.BlockSpec((tl, tl), lambda c: (0, 0))],
        out_specs=[pl.BlockSpec((k, tl), lambda c: (0, c)),
                   pl.BlockSpec((N_EXPERTS, LANES), lambda c: (0, 0))],
        out_shape=[jax.ShapeDtypeStruct((k, t), jnp.int32),
                   jax.ShapeDtypeStruct((N_EXPERTS, LANES), jnp.int32)],
        scratch_shapes=[pltpu.VMEM((N_EXPERTS, LANES), F32)],
        compiler_params=_cparams(("arbitrary",)),
        name="dispatch_ranks",
    )(idx, tri)


def _moe_kernel(be_ref, bv_ref, nu_ref, x_ref, wgu_ref, bgu_ref, wd_ref, bd_ref, o_ref, wgu_s, wd_s, *, dff):
    i = pl.program_id(0)
    h = x_ref.shape[1]

    @pl.when(jnp.logical_or(i == 0, be_ref[i] != be_ref[jnp.maximum(i - 1, 0)]))
    def _():
        wgu_s[...] = wgu_ref[0].astype(BF16)
        wd_s[...] = wd_ref[0].astype(BF16)

    def swiglu_rows(m):
        rs = slice(m * MOE_ROW_GROUP, (m + 1) * MOE_ROW_GROUP)
        rows = lax.broadcasted_iota(jnp.int32, (MOE_ROW_GROUP, h), 0) + m * MOE_ROW_GROUP
        xw = jnp.where(rows < bv_ref[i], x_ref[rs, :], 0)
        xa, xb = _unpack_halves(xw)
        gu = jnp.dot(xa.astype(BF16), wgu_s[:h, :], preferred_element_type=F32)
        gu += jnp.dot(xb.astype(BF16), wgu_s[h:, :], preferred_element_type=F32)
        gu += bgu_ref[0]
        gt = jnp.minimum(gu[:, :dff], SWIGLU_LIMIT)
        up = jnp.clip(gu[:, dff:], -SWIGLU_LIMIT, SWIGLU_LIMIT)
        act = (up + 1.0) * (gt * _sigmoid(SWIGLU_ALPHA * gt))
        out = jnp.dot(act.astype(BF16), wd_s[...], preferred_element_type=F32) + bd_ref[0]
        o_ref[rs, :] = _pack_halves(out)

    for m in range(x_ref.shape[0] // MOE_ROW_GROUP):
        live = jnp.logical_and(i < nu_ref[0], bv_ref[i] > m * MOE_ROW_GROUP)

        @pl.when(live)
        def _():
            swiglu_rows(m)

        @pl.when(jnp.logical_and(i < nu_ref[0], jnp.logical_not(live)))
        def _():
            o_ref[m * MOE_ROW_GROUP:(m + 1) * MOE_ROW_GROUP, :] = jnp.zeros((MOE_ROW_GROUP, h), jnp.int32)


def _moe_experts(blk_expert, blk_valid, n_used, xs, wgu, bgu, wd, bd, tmm):
    cap, h = xs.shape
    ne, d, dff2 = wgu.shape
    dff = dff2 // 2
    row_blk = lambda i, be, bv, nu: (jnp.minimum(i, nu[0] - 1), 0)
    wsel = lambda i, be, bv, nu: (be[i], 0, 0)
    grid_spec = pltpu.PrefetchScalarGridSpec(
        num_scalar_prefetch=3,
        grid=(cap // tmm,),
        in_specs=[pl.BlockSpec((tmm, h), row_blk),
                  pl.BlockSpec((1, d, dff2), wsel),
                  pl.BlockSpec((1, 1, dff2), wsel),
                  pl.BlockSpec((1, dff, d), wsel),
                  pl.BlockSpec((1, 1, d), wsel)],
        out_specs=pl.BlockSpec((tmm, h), row_blk),
        scratch_shapes=[pltpu.VMEM((d, dff2), BF16), pltpu.VMEM((dff, d), BF16)],
    )
    return pl.pallas_call(
        functools.partial(_moe_kernel, dff=dff),
        grid_spec=grid_spec,
        out_shape=jax.ShapeDtypeStruct((cap, h), jnp.int32),
        compiler_params=_cparams(("arbitrary",)),
        name="moe_experts",
    )(blk_expert, blk_valid, n_used, xs, wgu, bgu.reshape(ne, 1, dff2), wd, bd.reshape(ne, 1, d))


SC_CHUNK = 64


def _sc_workers():
    info = plsc.get_sparse_core_info()
    return info.num_cores, info.num_subcores


def _sc_dispatch(rows, dest, cap):
    n, w = rows.shape
    topk = dest.shape[0] // n
    nc, ns = _sc_workers()
    per_w = n // (nc * ns)
    assert per_w % SC_CHUNK == 0
    mesh = plsc.VectorSubcoreMesh(core_axis_name="c", subcore_axis_name="s")

    @functools.partial(
        pl.kernel, mesh=mesh,
        out_type=jax.ShapeDtypeStruct((cap, w), jnp.int32),
        scratch_types=[pltpu.VMEM((SC_CHUNK,), jnp.int32),
                       pltpu.VMEM((SC_CHUNK, w), jnp.int32),
                       pltpu.SemaphoreType.DMA],
    )
    def scatter_rows(rows_hbm, dest_hbm, out_hbm, idx_v, rows_v, sem):
        base = (lax.axis_index("s") * nc + lax.axis_index("c")) * per_w

        @pl.loop(0, per_w // SC_CHUNK)
        def _(j):
            off = pl.multiple_of(base + j * SC_CHUNK, SC_CHUNK)
            pltpu.sync_copy(rows_hbm.at[pl.ds(off, SC_CHUNK)], rows_v)
            for k in range(topk):
                pltpu.sync_copy(dest_hbm.at[pl.ds(k * n + off, SC_CHUNK)], idx_v)
                pltpu.async_copy(rows_v, out_hbm.at[idx_v], sem).wait()

    return scatter_rows(rows, dest)


def _sc_gather(table, idx):
    n = idx.shape[0]
    w = table.shape[1]
    nc, ns = _sc_workers()
    per_w = n // (nc * ns)
    assert per_w % SC_CHUNK == 0
    mesh = plsc.VectorSubcoreMesh(core_axis_name="c", subcore_axis_name="s")

    @functools.partial(
        pl.kernel, mesh=mesh,
        out_type=jax.ShapeDtypeStruct((n, w), jnp.int32),
        scratch_types=[pltpu.VMEM((SC_CHUNK,), jnp.int32),
                       pltpu.VMEM((SC_CHUNK, w), jnp.int32),
                       pltpu.SemaphoreType.DMA],
    )
    def gather_rows(table_hbm, idx_hbm, out_hbm, idx_v, rows_v, sem):
        base = (lax.axis_index("s") * nc + lax.axis_index("c")) * per_w

        @pl.loop(0, per_w // SC_CHUNK)
        def _(j):
            off = pl.multiple_of(base + j * SC_CHUNK, SC_CHUNK)
            pltpu.sync_copy(idx_hbm.at[pl.ds(off, SC_CHUNK)], idx_v)
            pltpu.async_copy(table_hbm.at[idx_v], rows_v, sem).wait()
            pltpu.sync_copy(rows_v, out_hbm.at[pl.ds(off, SC_CHUNK)])

    return gather_rows(table, idx)


def _combine_kernel(x1_ref, y_ref, gate_ref, gtf_ref, g_ref, *rest):
    o_ref = rest[-1]
    h = y_ref.shape[3]
    gates = gate_ref[0]
    moe_a = moe_b = None
    for k in range(TOP_K):
        ya, yb = _unpack_halves(y_ref[k, 0])
        gk = gates[:, k:k + 1]
        moe_a = gk * ya if k == 0 else moe_a + gk * ya
        moe_b = gk * yb if k == 0 else moe_b + gk * yb
    za = x1_ref[0, :, :h] + gtf_ref[0, :, :h] * moe_a
    zb = x1_ref[0, :, h:] + gtf_ref[0, :, h:] * moe_b
    ms = (jnp.sum(za * za, axis=-1, keepdims=True) + jnp.sum(zb * zb, axis=-1, keepdims=True)) / (2 * h)
    inv = lax.rsqrt(ms + EPS)
    o_ref[0, :, :h] = za * inv * g_ref[:, :h]
    o_ref[0, :, h:] = zb * inv * g_ref[:, h:]


def _combine(x1, yk, gates_t, gt_f, g_final, tm, b0, prev_out):
    b, s, d = x1.shape
    _, bp, _, h = yk.shape
    in_specs = [pl.BlockSpec((1, tm, d), lambda i, t: (i + b0, t, 0)),
                pl.BlockSpec((TOP_K, 1, tm, h), lambda i, t: (0, i, t, 0)),
                pl.BlockSpec((1, tm, TOP_K), lambda i, t: (i + b0, t, 0)),
                pl.BlockSpec((1, 1, d), lambda i, t: (i + b0, 0, 0)),
                pl.BlockSpec((1, d), lambda i, t: (0, 0))]
    args = [x1, yk, gates_t, gt_f, g_final.reshape(1, d)]
    aliases = {}
    if prev_out is not None:
        in_specs.append(pl.BlockSpec(memory_space=pl.ANY))
        args.append(prev_out)
        aliases = {len(args) - 1: 0}
    return pl.pallas_call(
        _combine_kernel,
        grid=(bp, s // tm),
        in_specs=in_specs,
        out_specs=pl.BlockSpec((1, tm, d), lambda i, t: (i + b0, t, 0)),
        out_shape=jax.ShapeDtypeStruct((b, s, d), F32),
        input_output_aliases=aliases,
        compiler_params=_cparams(("parallel", "arbitrary")),
        name="combine",
    )(*args)


def _dft_tables(rows, gd):
    seq = rows * GRID_W
    n = np.arange(rows)
    ang1 = 2.0 * np.pi * np.outer(n, n) / rows
    d2 = np.concatenate([np.cos(ang1), -np.sin(ang1)], axis=0)
    k1 = np.arange(rows)[:, None, None]
    k2 = np.arange(GRID_W)[None, :, None]
    n2 = np.arange(GRID_W)[None, None, :]
    ang2 = 2.0 * np.pi * ((n2 * (k1 + rows * k2)) % seq) / seq
    ec, es = np.cos(ang2), np.sin(ang2)
    etab = np.concatenate([np.concatenate([ec, es], axis=2),
                           np.concatenate([-es, ec], axis=2)], axis=1)
    c = np.arange(gd)
    angc = 2.0 * np.pi * np.outer(c, c) / gd
    scale = 1.0 / np.sqrt(seq * gd)
    return (jnp.asarray(d2, BF16), jnp.asarray(etab, BF16),
            jnp.asarray(np.cos(angc) * scale, F32), jnp.asarray(np.sin(angc) * scale, F32))


def _block_diag(w):
    h, i, o = w.shape
    eye = jnp.eye(h, dtype=w.dtype)
    return (eye[:, None, :, None] * w[:, :, None, :]).reshape(h * i, h * o)


def kernel(x, c, ctx, c_ctx, w_mod, b_mod, g_norm_mix, g_norm_ffn, w_in, w_fourier, conv_w, conv_b,
           rg_w_a, rg_b_a, rg_w_x, rg_b_x, rg_lam, g_out_fourier, g_out_rg, w_out, w_router, b_router,
           w_gate_up, b_gate_up, w_down, b_down, g_final):
    assert w_mod.shape[0] == 1, "single-layer stack only"
    b, s, d = x.shape
    df = w_fourier.shape[1] * w_fourier.shape[2]
    dr = conv_w.shape[2]
    gd = w_fourier.shape[2]
    rows = s // GRID_W
    t = b * s
    ne = w_router.shape[2]

    mrows = -(-(b + 1) // SUBLANES) * SUBLANES
    cond = jnp.zeros((mrows, d), F32).at[:b].set(c).at[b].set(c_ctx)
    mod = _adaln(cond, w_mod[0], b_mod[0])
    sh_m, sc_m, gt_m, sh_f, sc_f, gt_f = [mod[:b, k * d:(k + 1) * d].reshape(b, 1, d) for k in range(N_MOD)]
    csh_m = mod[b:b + 1, 0:d].reshape(1, 1, d)
    csc_m = mod[b:b + 1, d:2 * d].reshape(1, 1, d)

    tm = min(s, TOKEN_TILE)
    d2, etab, cmat, smat = _dft_tables(rows, gd)
    jmat = jnp.asarray(np.eye(GRID_W)[::-1].copy(), BF16)

    w_in_bf = w_in[0].astype(BF16)
    f, xs, gg = _stage_b(x, sh_m, sc_m, g_norm_mix[0], w_in_bf, jmat, df, dr, tm=tm)
    xr_ctx = _stage_b_ctx(ctx, csh_m, csc_m, g_norm_mix[0], w_in_bf[:, df:df + dr])

    wcat = jnp.stack([jnp.concatenate([_block_diag(rg_w_a[0, dd]), _block_diag(rg_w_x[0, dd])], axis=1)
                      for dd in range(2)]).astype(BF16)
    bcat = jnp.concatenate([rg_b_a[0], rg_b_x[0]], axis=1).reshape(2, 1, 2 * dr)
    lam = rg_lam[0].reshape(2, 1, dr)
    h0 = jnp.zeros((b, SUBLANES, dr), F32)
    _, hfin_ctx = _rg_scan(xr_ctx, h0, conv_w[0], conv_b[0], wcat, bcat, lam, tc=ctx.shape[1])
    hs, _ = _rg_scan(xs, hfin_ctx, conv_w[0], conv_b[0], wcat, bcat, lam, tc=min(s, SCAN_CHUNK))

    cw, sw = _fold_fourier(cmat, smat, w_fourier[0])
    bdc = _block_diag(cw).astype(BF16)
    bds = _block_diag(sw).astype(BF16)
    y = _fourier_stage1(f, d2, tl=min(GRID_W * df, 8192))
    fn = _fourier_stage2(y.reshape(b, 2, rows, GRID_W, df), etab, bdc, bds, g_out_fourier[0],
                         kb=min(rows, 16))
    fn = fn.reshape(b, s, df)

    x1, h2, idx, gates = _stage_m(fn, hs, gg, x, gt_m, sh_f, sc_f, g_out_rg[0], g_norm_ffn[0],
                                  w_out[0].astype(BF16), w_router[0].T, b_router[0], jmat, tm=tm)

    tl = min(t, RANK_TILE)
    tri = jnp.asarray(np.triu(np.ones((tl, tl))), BF16)
    rank, cnt = _dispatch_ranks(idx, tri, tl)
    counts = cnt[:, 0]
    tmm = MOE_ROW_TILE
    padded = (counts + tmm - 1) // tmm * tmm
    pad_end = jnp.cumsum(padded)
    pad_start = pad_end - padded
    eids = jnp.arange(ne, dtype=jnp.int32)
    dest = rank + jnp.sum(jnp.where(idx[:, :, None] == eids, pad_start, 0), axis=-1)
    n_blocks = -(-(t * TOP_K) // tmm) + ne
    cap = n_blocks * tmm
    n_used = (pad_end[-1] // tmm).astype(jnp.int32).reshape(1)
    blk_start = jnp.arange(n_blocks, dtype=jnp.int32) * tmm
    blk_expert = jnp.sum(blk_start[:, None] >= pad_end[None, :], axis=1).astype(jnp.int32)
    last_expert = jnp.sum(pad_end[-1] - tmm >= pad_end).astype(jnp.int32)
    blk_expert = jnp.minimum(blk_expert, last_expert)
    sel = blk_expert[:, None] == eids
    blk_first = jnp.sum(jnp.where(sel, pad_start, 0), axis=1)
    blk_count = jnp.sum(jnp.where(sel, counts, 0), axis=1)
    blk_valid = jnp.clip(blk_count - (blk_start - blk_first), 0, tmm).astype(jnp.int32)

    x_sorted = _sc_dispatch(h2.reshape(t, d // 2), dest.reshape(-1), cap)
    y_sorted = _moe_experts(blk_expert, blk_valid, n_used, x_sorted,
                            w_gate_up[0], b_gate_up[0], w_down[0], b_down[0], tmm)
    gates_t = gates.T.reshape(b, s, TOP_K)
    dest_b = dest.reshape(TOP_K, b, s)
    bp = b // COMBINE_GROUPS if b % COMBINE_GROUPS == 0 else b
    out = None
    for b0 in range(0, b, bp):
        yk = _sc_gather(y_sorted, dest_b[:, b0:b0 + bp].reshape(-1)).reshape(TOP_K, bp, s, d // 2)
        out = _combine(x1, yk, gates_t, gt_f, g_final, tm, b0, out)
    return out
```

```python
import functools

import numpy as np
import jax
import jax.numpy as jnp
from jax import lax
from jax.experimental import pallas as pl
from jax.experimental.pallas import tpu as pltpu
from jax.experimental.pallas import tpu_sc as plsc

GRID_W = 64
FOURIER_GROUPS = 4
RG_HEADS = 8
CONV_W = 4
CONV_PAD_LO = 2
RG_C = 8.0
N_EXPERTS = 32
TOP_K = 4
SWIGLU_LIMIT = 7.0
SWIGLU_ALPHA = 1.702
N_MOD = 6
EPS = 1e-6

LANES = 128
SUBLANES = 8
VMEM_LIMIT_BYTES = 56 * 1024 * 1024
TOKEN_TILE = 1024
RANK_TILE = 512
SCAN_CHUNK = 512
MOE_ROW_TILE = 512
MOE_FF_CHUNK = 512
COMBINE_GROUPS = 4

F32 = jnp.float32
BF16 = jnp.bfloat16


def _cparams(sem):
    return pltpu.CompilerParams(dimension_semantics=sem, vmem_limit_bytes=VMEM_LIMIT_BYTES)


def _split_bf16(a):
    hi = a.astype(BF16)
    lo = (a - hi.astype(F32)).astype(BF16)
    return hi, lo


def _dot3(a, b):
    ah, al = _split_bf16(a)
    bh, bl = _split_bf16(b)
    out = jnp.dot(ah, bh, preferred_element_type=F32)
    out += jnp.dot(ah, bl, preferred_element_type=F32)
    out += jnp.dot(al, bh, preferred_element_type=F32)
    return out


def _dot3_nt(a, b):
    dn = (((1,), (1,)), ((), ()))
    ah, al = _split_bf16(a)
    bh, bl = _split_bf16(b)
    out = lax.dot_general(ah, bh, dn, preferred_element_type=F32)
    out += lax.dot_general(ah, bl, dn, preferred_element_type=F32)
    out += lax.dot_general(al, bh, dn, preferred_element_type=F32)
    return out


def _gelu_tanh(x):
    return 0.5 * x * (1.0 + jnp.tanh(0.7978845608028654 * (x + 0.044715 * (x * x * x))))


def _rms(x, g):
    return x * lax.rsqrt(jnp.mean(x * x, axis=-1, keepdims=True) + EPS) * g


def _pack_halves(v):
    h = v.shape[1] // 2
    hi = lax.bitcast_convert_type(v[:, :h].astype(BF16).astype(F32), jnp.uint32)
    lo = lax.bitcast_convert_type(v[:, h:].astype(BF16).astype(F32), jnp.uint32)
    return lax.bitcast_convert_type(hi | (lo >> 16), jnp.int32)


def _unpack_halves(w):
    u = lax.bitcast_convert_type(w, jnp.uint32)
    hi = lax.bitcast_convert_type(u & jnp.uint32(0xFFFF0000), F32)
    lo = lax.bitcast_convert_type(u << 16, F32)
    return hi, lo


def _adaln_kernel(c_ref, w_ref, b_ref, o_ref):
    s = c_ref[...]
    s = s * jax.nn.sigmoid(s)
    o_ref[...] = _dot3(s, w_ref[...]) + b_ref[...]


def _adaln(cond, w_mod, b_mod):
    m, d = cond.shape
    n = w_mod.shape[1]
    tn = n // N_MOD
    return pl.pallas_call(
        _adaln_kernel,
        grid=(n // tn,),
        in_specs=[pl.BlockSpec((m, d), lambda i: (0, 0)),
                  pl.BlockSpec((d, tn), lambda i: (0, i)),
                  pl.BlockSpec((1, tn), lambda i: (0, i))],
        out_specs=pl.BlockSpec((m, tn), lambda i: (0, i)),
        out_shape=jax.ShapeDtypeStruct((m, n), F32),
        compiler_params=_cparams(("arbitrary",)),
        name="adaln",
    )(cond, w_mod, b_mod.reshape(1, n))


def _fold_kernel(c_ref, s_ref, w_ref, cw_ref, sw_ref):
    w = w_ref[0]
    cw_ref[0] = _dot3(c_ref[...], w)
    sw_ref[0] = _dot3(s_ref[...], w)


def _fold_fourier(cmat, smat, w_f):
    g, gd, _ = w_f.shape
    spec_m = pl.BlockSpec((gd, gd), lambda i: (0, 0))
    spec_w = pl.BlockSpec((1, gd, gd), lambda i: (i, 0, 0))
    return pl.pallas_call(
        _fold_kernel,
        grid=(g,),
        in_specs=[spec_m, spec_m, spec_w],
        out_specs=[spec_w, spec_w],
        out_shape=[jax.ShapeDtypeStruct((g, gd, gd), F32)] * 2,
        compiler_params=_cparams(("arbitrary",)),
        name="fold_fourier",
    )(cmat, smat, w_f)


def _seg_pitch(seg_len):
    n8 = seg_len // SUBLANES
    return SUBLANES * (n8 + 1 - n8 % 2)


def _pitched_store(v, buf, blk0):
    pitch = _seg_pitch(GRID_W)
    for r in range(v.shape[0] // GRID_W):
        for c in range(v.shape[1] // LANES):
            buf[c, (blk0 + r) * pitch:(blk0 + r) * pitch + GRID_W, :] = (
                v[r * GRID_W:(r + 1) * GRID_W, c * LANES:(c + 1) * LANES])


def _pitched_gather(buf, nb, store):
    pitch = _seg_pitch(GRID_W)
    for pos in range(GRID_W):
        for c in range(buf.shape[0]):
            store(pos, c, buf[c, pl.ds(pos, nb, stride=pitch), :])


def _stage_b_kernel(x_ref, sh_ref, sc_ref, g_ref, w_ref, j_ref, f_ref, xs_ref, gg_ref, fbuf, *, df, dr):
    tm = x_ref.shape[1]
    h = _rms(x_ref[0], g_ref[...]) * (1.0 + sc_ref[0]) + sh_ref[0]
    hb = h.astype(BF16)
    _pitched_store(jnp.dot(hb, w_ref[:, :df], preferred_element_type=F32), fbuf, 0)

    def store_f(pos, c, tile):
        f_ref[0, :, pos * df + c * LANES:pos * df + (c + 1) * LANES] = tile.astype(BF16)
    _pitched_gather(fbuf, tm // GRID_W, store_f)
    xr = jnp.dot(hb, w_ref[:, df:df + dr], preferred_element_type=F32).astype(BF16)
    for r in range(tm // GRID_W):
        blk = xr[r * GRID_W:(r + 1) * GRID_W]
        if r % 2 == 1:
            blk = jnp.dot(j_ref[...], blk, preferred_element_type=F32).astype(BF16)
        xs_ref[0, r * GRID_W:(r + 1) * GRID_W, :] = blk
    gr = jnp.dot(hb, w_ref[:, df + dr:], preferred_element_type=F32)
    gg_ref[0] = _gelu_tanh(gr).astype(BF16)


def _stage_b(x, shift, scale, g, w_in_bf, jmat, df, dr, tm):
    b, s, d = x.shape
    n = w_in_bf.shape[1]
    vec = pl.BlockSpec((1, 1, d), lambda i, t: (i, 0, 0))
    out = pl.BlockSpec((1, tm, df), lambda i, t: (i, t, 0))
    nb = tm // GRID_W
    tok = jax.ShapeDtypeStruct((b, s, df), BF16)
    return pl.pallas_call(
        functools.partial(_stage_b_kernel, df=df, dr=dr),
        grid=(b, s // tm),
        in_specs=[pl.BlockSpec((1, tm, d), lambda i, t: (i, t, 0)), vec, vec,
                  pl.BlockSpec((1, d), lambda i, t: (0, 0)),
                  pl.BlockSpec((d, n), lambda i, t: (0, 0)),
                  pl.BlockSpec((GRID_W, GRID_W), lambda i, t: (0, 0))],
        out_specs=[pl.BlockSpec((1, nb, GRID_W * df), lambda i, t: (i, t, 0)), out, out],
        out_shape=[jax.ShapeDtypeStruct((b, s // GRID_W, GRID_W * df), BF16), tok, tok],
        scratch_shapes=[pltpu.VMEM((df // LANES, nb * _seg_pitch(GRID_W), LANES), F32)],
        compiler_params=_cparams(("parallel", "arbitrary")),
        name="stage_b",
    )(x, shift, scale, g.reshape(1, d), w_in_bf, jmat)


def _stage_b_ctx_kernel(x_ref, sh_ref, sc_ref, g_ref, w_ref, xr_ref):
    h = _rms(x_ref[0], g_ref[...]) * (1.0 + sc_ref[0]) + sh_ref[0]
    xr_ref[0] = jnp.dot(h.astype(BF16), w_ref[...], preferred_element_type=F32).astype(BF16)


def _stage_b_ctx(ctx, shift, scale, g, w_xr_bf):
    b, s, d = ctx.shape
    dr = w_xr_bf.shape[1]
    vec = pl.BlockSpec((1, 1, d), lambda i: (0, 0, 0))
    return pl.pallas_call(
        _stage_b_ctx_kernel,
        grid=(b,),
        in_specs=[pl.BlockSpec((1, s, d), lambda i: (i, 0, 0)), vec, vec,
                  pl.BlockSpec((1, d), lambda i: (0, 0)),
                  pl.BlockSpec((d, dr), lambda i: (0, 0))],
        out_specs=pl.BlockSpec((1, s, dr), lambda i: (i, 0, 0)),
        out_shape=jax.ShapeDtypeStruct((b, s, dr), BF16),
        compiler_params=_cparams(("arbitrary",)),
        name="stage_b_ctx",
    )(ctx, shift, scale, g.reshape(1, d), w_xr_bf)


HALO = 16


def _sigmoid(x):
    return 0.5 * jnp.tanh(0.5 * x) + 0.5


def _rg_kernel(xs_ref, h0_ref, cw_ref, cb_ref, w_ref, b_ref, lam_ref, pm_ref, pmt_ref, out_ref, hfin_ref,
               hf_s, xc_s, a_s, u_s, hl_s, p_s, c_s, hc_s, *, tc, nchunk, seq, dr):
    p = pl.program_id(1)
    j = pl.program_id(2)
    cidx = jnp.where(p == 0, j, nchunk - 1 - j)
    start = pl.multiple_of(cidx * tc, tc)
    nseg = SUBLANES
    sl = tc // nseg
    sub = lax.broadcasted_iota(jnp.int32, (nseg, dr), 0)

    @pl.when(p == 0)
    def _():
        xp = jnp.dot(pm_ref[...], xs_ref[0, pl.ds(start, tc), :], preferred_element_type=F32)
        pstart = pl.multiple_of(jnp.maximum(start - HALO, 0), HALO)
        nstart = pl.multiple_of(jnp.minimum(start + tc, seq - HALO), HALO)
        prev = xs_ref[0, pl.ds(pstart, HALO), :].astype(F32)
        nxt = xs_ref[0, pl.ds(nstart, HALO), :].astype(F32)
        prev = jnp.where(cidx > 0, prev, 0.0)
        nxt = jnp.where(cidx < nchunk - 1, nxt, 0.0)
        tm2 = jnp.where(sub == 0, prev[HALO - 2:HALO - 1], pltpu.roll(xp[(sl - 2) * nseg:(sl - 1) * nseg], 1, 0))
        tm1 = jnp.where(sub == 0, prev[HALO - 1:HALO], pltpu.roll(xp[(sl - 1) * nseg:sl * nseg], 1, 0))
        tp1 = jnp.where(sub == nseg - 1, nxt[0:1], pltpu.roll(xp[0:nseg], nseg - 1, 0))
        ext = jnp.concatenate([tm2, tm1, xp, tp1], axis=0)
        xc = cb_ref[...] + cw_ref[0:1, :] * ext[0:tc]
        for k in range(1, CONV_W):
            xc = xc + cw_ref[k:k + 1, :] * ext[k * nseg:k * nseg + tc]
        xc_s[pl.ds(start, tc), :] = xc

    xc = xc_s[pl.ds(start, tc), :]
    gates = jnp.dot(xc.astype(BF16), w_ref[p], preferred_element_type=F32) + b_ref[p]
    r = _sigmoid(gates[:, :dr])
    i = _sigmoid(gates[:, dr:])
    log_a = (-RG_C) * r * jax.nn.softplus(-lam_ref[p])
    a = jnp.exp(log_a)
    a_s[...] = a
    u_s[...] = jnp.sqrt(-jnp.tanh(log_a) * (1.0 + a * a)) * (i * xc)

    @pl.when(jnp.logical_and(p == 0, j == 0))
    def _():
        hfin_ref[...] = jnp.zeros_like(hfin_ref)

    @pl.when(j == 0)
    def _():
        hc_s[0:1, :] = h0_ref[0, pl.ds(p, 1), :]

    def segment_scan(reverse):
        def body(q, carry):
            t = (sl - 1 - q) if reverse else q
            rows = pl.ds(pl.multiple_of(t * nseg, nseg), nseg)
            h, pr = carry
            av = a_s[rows, :]
            h = av * h + u_s[rows, :]
            pr = av * pr
            hl_s[rows, :] = h
            p_s[rows, :] = pr
            return h, pr
        h_end, p_end = lax.fori_loop(0, sl, body, (jnp.zeros((nseg, dr), F32), jnp.ones((nseg, dr), F32)),
                                     unroll=4)
        carry = hc_s[0:1, :]
        for g in (range(nseg - 1, -1, -1) if reverse else range(nseg)):
            c_s[g:g + 1, :] = carry
            carry = h_end[g:g + 1, :] + p_end[g:g + 1, :] * carry
        hc_s[0:1, :] = carry
        return c_s[...]

    def corrected(cin):
        h = hl_s[...].reshape(sl, nseg, dr) + p_s[...].reshape(sl, nseg, dr) * cin[None]
        return h.reshape(tc, dr)

    @pl.when(p == 0)
    def _():
        hf_s[pl.ds(start, tc), :] = corrected(segment_scan(False))

    @pl.when(p == 1)
    def _():
        tot = corrected(segment_scan(True)) + hf_s[pl.ds(start, tc), :]
        out_ref[0] = jnp.dot(pmt_ref[...], tot.astype(BF16), preferred_element_type=F32).astype(BF16)

    @pl.when(j == nchunk - 1)
    def _():
        hfin_ref[0, pl.ds(p, 1), :] = hc_s[0:1, :]


def _rg_scan(xs, h0, conv_w, conv_b, wcat, bcat, lam, tc):
    b, s, dr = xs.shape
    nchunk = s // tc
    last = nchunk - 1
    sl = tc // SUBLANES
    src = (np.arange(tc) % SUBLANES) * sl + np.arange(tc) // SUBLANES
    pm = np.zeros((tc, tc), np.float32)
    pm[np.arange(tc), src] = 1.0
    chunk_buf = pltpu.VMEM((tc, dr), F32)
    full2 = lambda shape: pl.BlockSpec(shape, lambda i, p, j: (0,) * len(shape))
    return pl.pallas_call(
        functools.partial(_rg_kernel, tc=tc, nchunk=nchunk, seq=s, dr=dr),
        grid=(b, 2, nchunk),
        in_specs=[pl.BlockSpec((1, s, dr), lambda i, p, j: (i, 0, 0)),
                  pl.BlockSpec((1, SUBLANES, dr), lambda i, p, j: (i, 0, 0)),
                  full2((CONV_W, dr)), full2((1, dr)),
                  full2((2, dr, 2 * dr)), full2((2, 1, 2 * dr)), full2((2, 1, dr)),
                  full2((tc, tc)), full2((tc, tc))],
        out_specs=[pl.BlockSpec((1, tc, dr), lambda i, p, j: (i, jnp.where(p == 0, last, last - j), 0)),
                   pl.BlockSpec((1, SUBLANES, dr), lambda i, p, j: (i, 0, 0))],
        out_shape=[jax.ShapeDtypeStruct((b, s, dr), BF16),
                   jax.ShapeDtypeStruct((b, SUBLANES, dr), F32)],
        scratch_shapes=[pltpu.VMEM((s, dr), F32), pltpu.VMEM((s, dr), F32),
                        chunk_buf, chunk_buf, chunk_buf, chunk_buf,
                        pltpu.VMEM((SUBLANES, dr), F32), pltpu.VMEM((SUBLANES, dr), F32)],
        compiler_params=_cparams(("arbitrary", "arbitrary", "arbitrary")),
        name="rg_scan",
    )(xs, h0, conv_w, conv_b.reshape(1, dr), wcat, bcat, lam, jnp.asarray(pm, BF16), jnp.asarray(pm.T, BF16))


def _f1_kernel(d_ref, x_ref, y_ref):
    y_ref[0] = jnp.dot(d_ref[...], x_ref[0], preferred_element_type=F32).astype(BF16)


def _fourier_stage1(fv, d2, tl):
    b, r, n = fv.shape
    return pl.pallas_call(
        _f1_kernel,
        grid=(b, n // tl),
        in_specs=[pl.BlockSpec((2 * r, r), lambda i, l: (0, 0)),
                  pl.BlockSpec((1, r, tl), lambda i, l: (i, 0, l))],
        out_specs=pl.BlockSpec((1, 2 * r, tl), lambda i, l: (i, 0, l)),
        out_shape=jax.ShapeDtypeStruct((b, 2 * r, n), BF16),
        compiler_params=_cparams(("parallel", "arbitrary")),
        name="fourier_stage1",
    )(d2, fv)


def _f2_kernel(y_ref, e_ref, bc_ref, bs_ref, g_ref, o_ref, obuf, *, kb, df):
    zr, zi = [], []
    for q in range(kb):
        yk = jnp.concatenate([y_ref[0, 0, q], y_ref[0, 1, q]], axis=0)
        z = jnp.dot(e_ref[q], yk, preferred_element_type=F32)
        zr.append(z[:GRID_W])
        zi.append(z[GRID_W:])
    zr = jnp.concatenate(zr, axis=0).astype(BF16)
    zi = jnp.concatenate(zi, axis=0).astype(BF16)
    o = jnp.dot(zr, bc_ref[...], preferred_element_type=F32)
    o += jnp.dot(zi, bs_ref[...], preferred_element_type=F32)
    on = _rms(o, g_ref[...])

    def store_o(pos, c, tile):
        o_ref[0, pos, :, c * LANES:(c + 1) * LANES] = tile.astype(BF16)
    _pitched_store(on, obuf, 0)
    _pitched_gather(obuf, kb, store_o)


def _fourier_stage2(y5, etab, bdc, bds, g, kb):
    b, _, r, w, df = y5.shape
    return pl.pallas_call(
        functools.partial(_f2_kernel, kb=kb, df=df),
        grid=(b, r // kb),
        in_specs=[pl.BlockSpec((1, 2, kb, w, df), lambda i, k: (i, 0, k, 0, 0)),
                  pl.BlockSpec((kb, 2 * w, 2 * w), lambda i, k: (k, 0, 0)),
                  pl.BlockSpec((df, df), lambda i, k: (0, 0)),
                  pl.BlockSpec((df, df), lambda i, k: (0, 0)),
                  pl.BlockSpec((1, df), lambda i, k: (0, 0))],
        out_specs=pl.BlockSpec((1, w, kb, df), lambda i, k: (i, 0, k, 0)),
        out_shape=jax.ShapeDtypeStruct((b, w, r, df), BF16),
        scratch_shapes=[pltpu.VMEM((df // LANES, kb * _seg_pitch(GRID_W), LANES), F32)],
        compiler_params=_cparams(("parallel", "arbitrary")),
        name="fourier_stage2",
    )(y5, etab, bdc, bds, g.reshape(1, df))


def _stage_m_kernel(fn_ref, hs_ref, gg_ref, x_ref, gtm_ref, shf_ref, scf_ref, gr_ref, gffn_ref,
                    wo_ref, wr_ref, br_ref, j_ref, x1_ref, h2_ref, idx_ref, gate_ref, *, df):
    tm = x_ref.shape[1]
    hs = hs_ref[0]
    blocks = []
    for r in range(tm // GRID_W):
        blk = hs[r * GRID_W:(r + 1) * GRID_W]
        if r % 2 == 1:
            blk = jnp.dot(j_ref[...], blk, preferred_element_type=F32)
        blocks.append(blk.astype(F32))
    rg = jnp.concatenate(blocks, axis=0) * gg_ref[0].astype(F32)
    rgn = _rms(rg, gr_ref[...]).astype(BF16)
    mix = jnp.dot(fn_ref[0], wo_ref[:df, :], preferred_element_type=F32)
    mix += jnp.dot(rgn, wo_ref[df:, :], preferred_element_type=F32)
    x1 = x_ref[0] + gtm_ref[0] * mix
    x1_ref[0] = x1
    h2 = _rms(x1, gffn_ref[...]) * (1.0 + scf_ref[0]) + shf_ref[0]
    h2_ref[0] = _pack_halves(h2)

    logits = _dot3_nt(wr_ref[...], h2) + br_ref[...]
    eidx = lax.broadcasted_iota(jnp.int32, logits.shape, 0)
    vals, idxs = [], []
    for _ in range(TOP_K):
        m = jnp.max(logits, axis=0, keepdims=True)
        sel = jnp.min(jnp.where(logits == m, eidx, N_EXPERTS), axis=0, keepdims=True)
        vals.append(m)
        idxs.append(sel)
        logits = jnp.where(eidx == sel, -jnp.inf, logits)
    ex = [jnp.exp(v - vals[0]) for v in vals]
    den = ex[0] + ex[1] + ex[2] + ex[3]
    for k in range(TOP_K):
        gate_ref[k:k + 1, :] = ex[k] / den
        idx_ref[k:k + 1, :] = idxs[k]


def _stage_m(fn, hs, gg, x, gt_m, sh_f, sc_f, g_out_r, g_ffn, w_out_bf, w_router_t, b_router, jmat, tm):
    b, s, d = x.shape
    df = fn.shape[2]
    dr = hs.shape[2]
    nt = s // tm
    ne = w_router_t.shape[0]
    vec = pl.BlockSpec((1, 1, d), lambda i, t: (i, 0, 0))
    half = lambda dd: pl.BlockSpec((1, tm, dd), lambda i, t: (i, t, 0))
    full = lambda shape: pl.BlockSpec(shape, lambda i, t: (0,) * len(shape))
    tok = pl.BlockSpec((TOP_K, tm), lambda i, t: (0, i * nt + t))
    return pl.pallas_call(
        functools.partial(_stage_m_kernel, df=df),
        grid=(b, nt),
        in_specs=[half(df), half(dr), half(dr), half(d), vec, vec, vec,
                  full((1, dr)), full((1, d)), full((d, d)), full((ne, d)), full((ne, 1)),
                  full((GRID_W, GRID_W))],
        out_specs=[half(d), half(d // 2), tok, tok],
        out_shape=[jax.ShapeDtypeStruct((b, s, d), F32), jax.ShapeDtypeStruct((b, s, d // 2), jnp.int32),
                   jax.ShapeDtypeStruct((TOP_K, b * s), jnp.int32),
                   jax.ShapeDtypeStruct((TOP_K, b * s), F32)],
        compiler_params=_cparams(("parallel", "arbitrary")),
        name="stage_m",
    )(fn, hs, gg, x, gt_m, sh_f, sc_f, g_out_r.reshape(1, dr), g_ffn.reshape(1, d), w_out_bf,
      w_router_t, b_router.reshape(ne, 1), jmat)


def _rank_kernel(idx_ref, tri_ref, rank_ref, cnt_ref, carry_s):
    c = pl.program_id(0)

    @pl.when(c == 0)
    def _():
        carry_s[...] = jnp.zeros_like(carry_s)

    l = idx_ref.shape[1]
    eidx = lax.broadcasted_iota(jnp.int32, (N_EXPERTS, l), 0)
    for k in range(TOP_K):
        onehot = eidx == idx_ref[k:k + 1, :]
        oh = jnp.where(onehot, 1.0, 0.0)
        prefix = jnp.dot(oh.astype(BF16), tri_ref[...], preferred_element_type=F32)
        carry = carry_s[:, 0:1]
        rank = jnp.sum(jnp.where(onehot, prefix - 1.0 + carry, 0.0), axis=0, keepdims=True)
        rank_ref[k:k + 1, :] = rank.astype(jnp.int32)
        carry_s[...] = carry_s[...] + jnp.sum(oh, axis=1, keepdims=True)
    cnt_ref[...] = carry_s[...].astype(jnp.int32)


def _dispatch_ranks(idx, tri, tl):
    k, t = idx.shape
    return pl.pallas_call(
        _rank_kernel,
        grid=(t // tl,),
        in_specs=[pl.BlockSpec((k, tl), lambda c: (0, c)),
                  pl.BlockSpec((tl, tl), lambda c: (0, 0))],
        out_specs=[pl.BlockSpec((k, tl), lambda c: (0, c)),
                   pl.BlockSpec((N_EXPERTS, LANES), lambda c: (0, 0))],
        out_shape=[jax.ShapeDtypeStruct((k, t), jnp.int32),
                   jax.ShapeDtypeStruct((N_EXPERTS, LANES), jnp.int32)],
        scratch_shapes=[pltpu.VMEM((N_EXPERTS, LANES), F32)],
        compiler_params=_cparams(("arbitrary",)),
        name="dispatch_ranks",
    )(idx, tri)


def _moe_kernel(be_ref, bv_ref, nu_ref, x_ref, wgu_ref, bgu_ref, wd_ref, bd_ref, o_ref, wgu_s, wd_s, *, dff):
    i = pl.program_id(0)
    h = x_ref.shape[1]

    @pl.when(jnp.logical_or(i == 0, be_ref[i] != be_ref[jnp.maximum(i - 1, 0)]))
    def _():
        wgu_s[...] = wgu_ref[0].astype(BF16)
        wd_s[...] = wd_ref[0].astype(BF16)

    @pl.when(i < nu_ref[0])
    def _():
        rows = lax.broadcasted_iota(jnp.int32, x_ref.shape, 0)
        xw = jnp.where(rows < bv_ref[i], x_ref[...], 0)
        xa, xb = _unpack_halves(xw)
        xa = xa.astype(BF16)
        xb = xb.astype(BF16)
        acc = None
        for c in range(dff // MOE_FF_CHUNK):
            gs = slice(c * MOE_FF_CHUNK, (c + 1) * MOE_FF_CHUNK)
            us = slice(dff + c * MOE_FF_CHUNK, dff + (c + 1) * MOE_FF_CHUNK)
            g = jnp.dot(xa, wgu_s[:h, gs], preferred_element_type=F32)
            g += jnp.dot(xb, wgu_s[h:, gs], preferred_element_type=F32)
            u = jnp.dot(xa, wgu_s[:h, us], preferred_element_type=F32)
            u += jnp.dot(xb, wgu_s[h:, us], preferred_element_type=F32)
            gt = jnp.minimum(g + bgu_ref[0, :, gs], SWIGLU_LIMIT)
            up = jnp.clip(u + bgu_ref[0, :, us], -SWIGLU_LIMIT, SWIGLU_LIMIT)
            act = (up + 1.0) * (gt * _sigmoid(SWIGLU_ALPHA * gt))
            part = jnp.dot(act.astype(BF16), wd_s[gs, :], preferred_element_type=F32)
            acc = part if acc is None else acc + part
        o_ref[...] = _pack_halves(acc + bd_ref[0])


def _moe_experts(blk_expert, blk_valid, n_used, xs, wgu, bgu, wd, bd, tmm):
    cap, h = xs.shape
    ne, d, dff2 = wgu.shape
    dff = dff2 // 2
    row_blk = lambda i, be, bv, nu: (jnp.minimum(i, nu[0] - 1), 0)
    wsel = lambda i, be, bv, nu: (be[i], 0, 0)
    grid_spec = pltpu.PrefetchScalarGridSpec(
        num_scalar_prefetch=3,
        grid=(cap // tmm,),
        in_specs=[pl.BlockSpec((tmm, h), row_blk),
                  pl.BlockSpec((1, d, dff2), wsel),
                  pl.BlockSpec((1, 1, dff2), wsel),
                  pl.BlockSpec((1, dff, d), wsel),
                  pl.BlockSpec((1, 1, d), wsel)],
        out_specs=pl.BlockSpec((tmm, h), row_blk),
        scratch_shapes=[pltpu.VMEM((d, dff2), BF16), pltpu.VMEM((dff, d), BF16)],
    )
    return pl.pallas_call(
        functools.partial(_moe_kernel, dff=dff),
        grid_spec=grid_spec,
        out_shape=jax.ShapeDtypeStruct((cap, h), jnp.int32),
        compiler_params=_cparams(("arbitrary",)),
        name="moe_experts",
    )(blk_expert, blk_valid, n_used, xs, wgu, bgu.reshape(ne, 1, dff2), wd, bd.reshape(ne, 1, d))


SC_CHUNK = 64


def _sc_workers():
    info = plsc.get_sparse_core_info()
    return info.num_cores, info.num_subcores


def _sc_dispatch(rows, dest, cap):
    n, w = rows.shape
    topk = dest.shape[0] // n
    nc, ns = _sc_workers()
    per_w = n // (nc * ns)
    assert per_w % SC_CHUNK == 0
    mesh = plsc.VectorSubcoreMesh(core_axis_name="c", subcore_axis_name="s")

    @functools.partial(
        pl.kernel, mesh=mesh,
        out_type=jax.ShapeDtypeStruct((cap, w), jnp.int32),
        scratch_types=[pltpu.VMEM((SC_CHUNK,), jnp.int32),
                       pltpu.VMEM((SC_CHUNK, w), jnp.int32),
                       pltpu.SemaphoreType.DMA],
    )
    def scatter_rows(rows_hbm, dest_hbm, out_hbm, idx_v, rows_v, sem):
        base = (lax.axis_index("s") * nc + lax.axis_index("c")) * per_w

        @pl.loop(0, per_w // SC_CHUNK)
        def _(j):
            off = pl.multiple_of(base + j * SC_CHUNK, SC_CHUNK)
            pltpu.sync_copy(rows_hbm.at[pl.ds(off, SC_CHUNK)], rows_v)
            for k in range(topk):
                pltpu.sync_copy(dest_hbm.at[pl.ds(k * n + off, SC_CHUNK)], idx_v)
                pltpu.async_copy(rows_v, out_hbm.at[idx_v], sem).wait()

    return scatter_rows(rows, dest)


def _sc_gather(table, idx):
    n = idx.shape[0]
    w = table.shape[1]
    nc, ns = _sc_workers()
    per_w = n // (nc * ns)
    assert per_w % SC_CHUNK == 0
    mesh = plsc.VectorSubcoreMesh(core_axis_name="c", subcore_axis_name="s")

    @functools.partial(
        pl.kernel, mesh=mesh,
        out_type=jax.ShapeDtypeStruct((n, w), jnp.int32),
        scratch_types=[pltpu.VMEM((SC_CHUNK,), jnp.int32),
                       pltpu.VMEM((SC_CHUNK, w), jnp.int32),
                       pltpu.SemaphoreType.DMA],
    )
    def gather_rows(table_hbm, idx_hbm, out_hbm, idx_v, rows_v, sem):
        base = (lax.axis_index("s") * nc + lax.axis_index("c")) * per_w

        @pl.loop(0, per_w // SC_CHUNK)
        def _(j):
            off = pl.multiple_of(base + j * SC_CHUNK, SC_CHUNK)
            pltpu.sync_copy(idx_hbm.at[pl.ds(off, SC_CHUNK)], idx_v)
            pltpu.async_copy(table_hbm.at[idx_v], rows_v, sem).wait()
            pltpu.sync_copy(rows_v, out_hbm.at[pl.ds(off, SC_CHUNK)])

    return gather_rows(table, idx)


def _combine_kernel(x1_ref, y_ref, gate_ref, gtf_ref, g_ref, *rest):
    o_ref = rest[-1]
    h = y_ref.shape[3]
    gates = gate_ref[0]
    moe_a = moe_b = None
    for k in range(TOP_K):
        ya, yb = _unpack_halves(y_ref[k, 0])
        gk = gates[:, k:k + 1]
        moe_a = gk * ya if k == 0 else moe_a + gk * ya
        moe_b = gk * yb if k == 0 else moe_b + gk * yb
    za = x1_ref[0, :, :h] + gtf_ref[0, :, :h] * moe_a
    zb = x1_ref[0, :, h:] + gtf_ref[0, :, h:] * moe_b
    ms = (jnp.sum(za * za, axis=-1, keepdims=True) + jnp.sum(zb * zb, axis=-1, keepdims=True)) / (2 * h)
    inv = lax.rsqrt(ms + EPS)
    o_ref[0, :, :h] = za * inv * g_ref[:, :h]
    o_ref[0, :, h:] = zb * inv * g_ref[:, h:]


def _combine(x1, yk, gates_t, gt_f, g_final, tm, b0, prev_out):
    b, s, d = x1.shape
    _, bp, _, h = yk.shape
    in_specs = [pl.BlockSpec((1, tm, d), lambda i, t: (i + b0, t, 0)),
                pl.BlockSpec((TOP_K, 1, tm, h), lambda i, t: (0, i, t, 0)),
                pl.BlockSpec((1, tm, TOP_K), lambda i, t: (i + b0, t, 0)),
                pl.BlockSpec((1, 1, d), lambda i, t: (i + b0, 0, 0)),
                pl.BlockSpec((1, d), lambda i, t: (0, 0))]
    args = [x1, yk, gates_t, gt_f, g_final.reshape(1, d)]
    aliases = {}
    if prev_out is not None:
        in_specs.append(pl.BlockSpec(memory_space=pl.ANY))
        args.append(prev_out)
        aliases = {len(args) - 1: 0}
    return pl.pallas_call(
        _combine_kernel,
        grid=(bp, s // tm),
        in_specs=in_specs,
        out_specs=pl.BlockSpec((1, tm, d), lambda i, t: (i + b0, t, 0)),
        out_shape=jax.ShapeDtypeStruct((b, s, d), F32),
        input_output_aliases=aliases,
        compiler_params=_cparams(("parallel", "arbitrary")),
        name="combine",
    )(*args)


def _dft_tables(rows, gd):
    seq = rows * GRID_W
    n = np.arange(rows)
    ang1 = 2.0 * np.pi * np.outer(n, n) / rows
    d2 = np.concatenate([np.cos(ang1), -np.sin(ang1)], axis=0)
    k1 = np.arange(rows)[:, None, None]
    k2 = np.arange(GRID_W)[None, :, None]
    n2 = np.arange(GRID_W)[None, None, :]
    ang2 = 2.0 * np.pi * ((n2 * (k1 + rows * k2)) % seq) / seq
    ec, es = np.cos(ang2), np.sin(ang2)
    etab = np.concatenate([np.concatenate([ec, es], axis=2),
                           np.concatenate([-es, ec], axis=2)], axis=1)
    c = np.arange(gd)
    angc = 2.0 * np.pi * np.outer(c, c) / gd
    scale = 1.0 / np.sqrt(seq * gd)
    return (jnp.asarray(d2, BF16), jnp.asarray(etab, BF16),
            jnp.asarray(np.cos(angc) * scale, F32), jnp.asarray(np.sin(angc) * scale, F32))


def _block_diag(w):
    h, i, o = w.shape
    eye = jnp.eye(h, dtype=w.dtype)
    return (eye[:, None, :, None] * w[:, :, None, :]).reshape(h * i, h * o)


def kernel(x, c, ctx, c_ctx, w_mod, b_mod, g_norm_mix, g_norm_ffn, w_in, w_fourier, conv_w, conv_b,
           rg_w_a, rg_b_a, rg_w_x, rg_b_x, rg_lam, g_out_fourier, g_out_rg, w_out, w_router, b_router,
           w_gate_up, b_gate_up, w_down, b_down, g_final):
    assert w_mod.shape[0] == 1, "single-layer stack only"
    b, s, d = x.shape
    df = w_fourier.shape[1] * w_fourier.shape[2]
    dr = conv_w.shape[2]
    gd = w_fourier.shape[2]
    rows = s // GRID_W
    t = b * s
    ne = w_router.shape[2]

    mrows = -(-(b + 1) // SUBLANES) * SUBLANES
    cond = jnp.zeros((mrows, d), F32).at[:b].set(c).at[b].set(c_ctx)
    mod = _adaln(cond, w_mod[0], b_mod[0])
    sh_m, sc_m, gt_m, sh_f, sc_f, gt_f = [mod[:b, k * d:(k + 1) * d].reshape(b, 1, d) for k in range(N_MOD)]
    csh_m = mod[b:b + 1, 0:d].reshape(1, 1, d)
    csc_m = mod[b:b + 1, d:2 * d].reshape(1, 1, d)

    tm = min(s, TOKEN_TILE)
    d2, etab, cmat, smat = _dft_tables(rows, gd)
    jmat = jnp.asarray(np.eye(GRID_W)[::-1].copy(), BF16)

    w_in_bf = w_in[0].astype(BF16)
    f, xs, gg = _stage_b(x, sh_m, sc_m, g_norm_mix[0], w_in_bf, jmat, df, dr, tm=tm)
    xr_ctx = _stage_b_ctx(ctx, csh_m, csc_m, g_norm_mix[0], w_in_bf[:, df:df + dr])

    wcat = jnp.stack([jnp.concatenate([_block_diag(rg_w_a[0, dd]), _block_diag(rg_w_x[0, dd])], axis=1)
                      for dd in range(2)]).astype(BF16)
    bcat = jnp.concatenate([rg_b_a[0], rg_b_x[0]], axis=1).reshape(2, 1, 2 * dr)
    lam = rg_lam[0].reshape(2, 1, dr)
    h0 = jnp.zeros((b, SUBLANES, dr), F32)
    _, hfin_ctx = _rg_scan(xr_ctx, h0, conv_w[0], conv_b[0], wcat, bcat, lam, tc=ctx.shape[1])
    hs, _ = _rg_scan(xs, hfin_ctx, conv_w[0], conv_b[0], wcat, bcat, lam, tc=min(s, SCAN_CHUNK))

    cw, sw = _fold_fourier(cmat, smat, w_fourier[0])
    bdc = _block_diag(cw).astype(BF16)
    bds = _block_diag(sw).astype(BF16)
    y = _fourier_stage1(f, d2, tl=min(GRID_W * df, 8192))
    fn = _fourier_stage2(y.reshape(b, 2, rows, GRID_W, df), etab, bdc, bds, g_out_fourier[0],
                         kb=min(rows, 16))
    fn = fn.reshape(b, s, df)

    x1, h2, idx, gates = _stage_m(fn, hs, gg, x, gt_m, sh_f, sc_f, g_out_rg[0], g_norm_ffn[0],
                                  w_out[0].astype(BF16), w_router[0].T, b_router[0], jmat, tm=tm)

    tl = min(t, RANK_TILE)
    tri = jnp.asarray(np.triu(np.ones((tl, tl))), BF16)
    rank, cnt = _dispatch_ranks(idx, tri, tl)
    counts = cnt[:, 0]
    tmm = MOE_ROW_TILE
    padded = (counts + tmm - 1) // tmm * tmm
    pad_end = jnp.cumsum(padded)
    pad_start = pad_end - padded
    eids = jnp.arange(ne, dtype=jnp.int32)
    dest = rank + jnp.sum(jnp.where(idx[:, :, None] == eids, pad_start, 0), axis=-1)
    n_blocks = -(-(t * TOP_K) // tmm) + ne
    cap = n_blocks * tmm
    n_used = (pad_end[-1] // tmm).astype(jnp.int32).reshape(1)
    blk_start = jnp.arange(n_blocks, dtype=jnp.int32) * tmm
    blk_expert = jnp.sum(blk_start[:, None] >= pad_end[None, :], axis=1).astype(jnp.int32)
    last_expert = jnp.sum(pad_end[-1] - tmm >= pad_end).astype(jnp.int32)
    blk_expert = jnp.minimum(blk_expert, last_expert)
    sel = blk_expert[:, None] == eids
    blk_first = jnp.sum(jnp.where(sel, pad_start, 0), axis=1)
    blk_count = jnp.sum(jnp.where(sel, counts, 0), axis=1)
    blk_valid = jnp.clip(blk_count - (blk_start - blk_first), 0, tmm).astype(jnp.int32)

    x_sorted = _sc_dispatch(h2.reshape(t, d // 2), dest.reshape(-1), cap)
    y_sorted = _moe_experts(blk_expert, blk_valid, n_used, x_sorted,
                            w_gate_up[0], b_gate_up[0], w_down[0], b_down[0], tmm)
    gates_t = gates.T.reshape(b, s, TOP_K)
    dest_b = dest.reshape(TOP_K, b, s)
    bp = b // COMBINE_GROUPS if b % COMBINE_GROUPS == 0 else b
    out = None
    for b0 in range(0, b, bp):
        yk = _sc_gather(y_sorted, dest_b[:, b0:b0 + bp].reshape(-1)).reshape(TOP_K, bp, s, d // 2)
        out = _combine(x1, yk, gates_t, gt_f, g_final, tm, b0, out)
    return out
```

```python
import functools

import numpy as np
import jax
import jax.numpy as jnp
from jax import lax
from jax.experimental import pallas as pl
from jax.experimental.pallas import tpu as pltpu
from jax.experimental.pallas import tpu_sc as plsc

GRID_W = 64
FOURIER_GROUPS = 4
RG_HEADS = 8
CONV_W = 4
CONV_PAD_LO = 2
RG_C = 8.0
N_EXPERTS = 32
TOP_K = 4
SWIGLU_LIMIT = 7.0
SWIGLU_ALPHA = 1.702
N_MOD = 6
EPS = 1e-6

LANES = 128
SUBLANES = 8
VMEM_LIMIT_BYTES = 56 * 1024 * 1024
TOKEN_TILE = 1024
RANK_TILE = 512
SCAN_CHUNK = 512
MOE_ROW_TILE = 512
MOE_FF_CHUNK = 512
MOE_PIPE = 4

F32 = jnp.float32
BF16 = jnp.bfloat16


def _cparams(sem):
    return pltpu.CompilerParams(dimension_semantics=sem, vmem_limit_bytes=VMEM_LIMIT_BYTES)


def _split_bf16(a):
    hi = a.astype(BF16)
    lo = (a - hi.astype(F32)).astype(BF16)
    return hi, lo


def _dot3(a, b):
    ah, al = _split_bf16(a)
    bh, bl = _split_bf16(b)
    out = jnp.dot(ah, bh, preferred_element_type=F32)
    out += jnp.dot(ah, bl, preferred_element_type=F32)
    out += jnp.dot(al, bh, preferred_element_type=F32)
    return out


def _dot3_nt(a, b):
    dn = (((1,), (1,)), ((), ()))
    ah, al = _split_bf16(a)
    bh, bl = _split_bf16(b)
    out = lax.dot_general(ah, bh, dn, preferred_element_type=F32)
    out += lax.dot_general(ah, bl, dn, preferred_element_type=F32)
    out += lax.dot_general(al, bh, dn, preferred_element_type=F32)
    return out


def _gelu_tanh(x):
    return 0.5 * x * (1.0 + jnp.tanh(0.7978845608028654 * (x + 0.044715 * (x * x * x))))


def _rms(x, g):
    return x * lax.rsqrt(jnp.mean(x * x, axis=-1, keepdims=True) + EPS) * g


def _pack_halves(v):
    h = v.shape[1] // 2
    hi = lax.bitcast_convert_type(v[:, :h].astype(BF16).astype(F32), jnp.uint32)
    lo = lax.bitcast_convert_type(v[:, h:].astype(BF16).astype(F32), jnp.uint32)
    return lax.bitcast_convert_type(hi | (lo >> 16), jnp.int32)


def _unpack_halves(w):
    u = lax.bitcast_convert_type(w, jnp.uint32)
    hi = lax.bitcast_convert_type(u & jnp.uint32(0xFFFF0000), F32)
    lo = lax.bitcast_convert_type(u << 16, F32)
    return hi, lo


def _adaln_kernel(c_ref, w_ref, b_ref, o_ref):
    s = c_ref[...]
    s = s * jax.nn.sigmoid(s)
    o_ref[...] = _dot3(s, w_ref[...]) + b_ref[...]


def _adaln(cond, w_mod, b_mod):
    m, d = cond.shape
    n = w_mod.shape[1]
    tn = n // N_MOD
    return pl.pallas_call(
        _adaln_kernel,
        grid=(n // tn,),
        in_specs=[pl.BlockSpec((m, d), lambda i: (0, 0)),
                  pl.BlockSpec((d, tn), lambda i: (0, i)),
                  pl.BlockSpec((1, tn), lambda i: (0, i))],
        out_specs=pl.BlockSpec((m, tn), lambda i: (0, i)),
        out_shape=jax.ShapeDtypeStruct((m, n), F32),
        compiler_params=_cparams(("arbitrary",)),
        name="adaln",
    )(cond, w_mod, b_mod.reshape(1, n))


def _fold_kernel(c_ref, s_ref, w_ref, cw_ref, sw_ref):
    w = w_ref[0]
    cw_ref[0] = _dot3(c_ref[...], w)
    sw_ref[0] = _dot3(s_ref[...], w)


def _fold_fourier(cmat, smat, w_f):
    g, gd, _ = w_f.shape
    spec_m = pl.BlockSpec((gd, gd), lambda i: (0, 0))
    spec_w = pl.BlockSpec((1, gd, gd), lambda i: (i, 0, 0))
    return pl.pallas_call(
        _fold_kernel,
        grid=(g,),
        in_specs=[spec_m, spec_m, spec_w],
        out_specs=[spec_w, spec_w],
        out_shape=[jax.ShapeDtypeStruct((g, gd, gd), F32)] * 2,
        compiler_params=_cparams(("arbitrary",)),
        name="fold_fourier",
    )(cmat, smat, w_f)


def _seg_pitch(seg_len):
    n8 = seg_len // SUBLANES
    return SUBLANES * (n8 + 1 - n8 % 2)


def _pitched_store(v, buf, blk0):
    pitch = _seg_pitch(GRID_W)
    for r in range(v.shape[0] // GRID_W):
        for c in range(v.shape[1] // LANES):
            buf[c, (blk0 + r) * pitch:(blk0 + r) * pitch + GRID_W, :] = (
                v[r * GRID_W:(r + 1) * GRID_W, c * LANES:(c + 1) * LANES])


def _pitched_gather(buf, nb, store):
    pitch = _seg_pitch(GRID_W)
    for pos in range(GRID_W):
        for c in range(buf.shape[0]):
            store(pos, c, buf[c, pl.ds(pos, nb, stride=pitch), :])


def _stage_b_kernel(x_ref, sh_ref, sc_ref, g_ref, w_ref, j_ref, f_ref, xs_ref, gg_ref, fbuf, *, df, dr):
    tm = x_ref.shape[1]
    h = _rms(x_ref[0], g_ref[...]) * (1.0 + sc_ref[0]) + sh_ref[0]
    hb = h.astype(BF16)
    _pitched_store(jnp.dot(hb, w_ref[:, :df], preferred_element_type=F32), fbuf, 0)

    def store_f(pos, c, tile):
        f_ref[0, :, pos * df + c * LANES:pos * df + (c + 1) * LANES] = tile.astype(BF16)
    _pitched_gather(fbuf, tm // GRID_W, store_f)
    xr = jnp.dot(hb, w_ref[:, df:df + dr], preferred_element_type=F32).astype(BF16)
    for r in range(tm // GRID_W):
        blk = xr[r * GRID_W:(r + 1) * GRID_W]
        if r % 2 == 1:
            blk = jnp.dot(j_ref[...], blk, preferred_element_type=F32).astype(BF16)
        xs_ref[0, r * GRID_W:(r + 1) * GRID_W, :] = blk
    gr = jnp.dot(hb, w_ref[:, df + dr:], preferred_element_type=F32)
    gg_ref[0] = _gelu_tanh(gr).astype(BF16)


def _stage_b(x, shift, scale, g, w_in_bf, jmat, df, dr, tm):
    b, s, d = x.shape
    n = w_in_bf.shape[1]
    vec = pl.BlockSpec((1, 1, d), lambda i, t: (i, 0, 0))
    out = pl.BlockSpec((1, tm, df), lambda i, t: (i, t, 0))
    nb = tm // GRID_W
    tok = jax.ShapeDtypeStruct((b, s, df), BF16)
    return pl.pallas_call(
        functools.partial(_stage_b_kernel, df=df, dr=dr),
        grid=(b, s // tm),
        in_specs=[pl.BlockSpec((1, tm, d), lambda i, t: (i, t, 0)), vec, vec,
                  pl.BlockSpec((1, d), lambda i, t: (0, 0)),
                  pl.BlockSpec((d, n), lambda i, t: (0, 0)),
                  pl.BlockSpec((GRID_W, GRID_W), lambda i, t: (0, 0))],
        out_specs=[pl.BlockSpec((1, nb, GRID_W * df), lambda i, t: (i, t, 0)), out, out],
        out_shape=[jax.ShapeDtypeStruct((b, s // GRID_W, GRID_W * df), BF16), tok, tok],
        scratch_shapes=[pltpu.VMEM((df // LANES, nb * _seg_pitch(GRID_W), LANES), F32)],
        compiler_params=_cparams(("parallel", "arbitrary")),
        name="stage_b",
    )(x, shift, scale, g.reshape(1, d), w_in_bf, jmat)


def _stage_b_ctx_kernel(x_ref, sh_ref, sc_ref, g_ref, w_ref, xr_ref):
    h = _rms(x_ref[0], g_ref[...]) * (1.0 + sc_ref[0]) + sh_ref[0]
    xr_ref[0] = jnp.dot(h.astype(BF16), w_ref[...], preferred_element_type=F32).astype(BF16)


def _stage_b_ctx(ctx, shift, scale, g, w_xr_bf):
    b, s, d = ctx.shape
    dr = w_xr_bf.shape[1]
    vec = pl.BlockSpec((1, 1, d), lambda i: (0, 0, 0))
    return pl.pallas_call(
        _stage_b_ctx_kernel,
        grid=(b,),
        in_specs=[pl.BlockSpec((1, s, d), lambda i: (i, 0, 0)), vec, vec,
                  pl.BlockSpec((1, d), lambda i: (0, 0)),
                  pl.BlockSpec((d, dr), lambda i: (0, 0))],
        out_specs=pl.BlockSpec((1, s, dr), lambda i: (i, 0, 0)),
        out_shape=jax.ShapeDtypeStruct((b, s, dr), BF16),
        compiler_params=_cparams(("arbitrary",)),
        name="stage_b_ctx",
    )(ctx, shift, scale, g.reshape(1, d), w_xr_bf)


HALO = 16


def _sigmoid(x):
    return 0.5 * jnp.tanh(0.5 * x) + 0.5


def _rg_kernel(xs_ref, h0_ref, cw_ref, cb_ref, w_ref, b_ref, lam_ref, pm_ref, pmt_ref, out_ref, hfin_ref,
               hf_s, xc_s, a_s, u_s, hl_s, p_s, c_s, hc_s, *, tc, nchunk, seq, dr):
    p = pl.program_id(1)
    j = pl.program_id(2)
    cidx = jnp.where(p == 0, j, nchunk - 1 - j)
    start = pl.multiple_of(cidx * tc, tc)
    nseg = SUBLANES
    sl = tc // nseg
    sub = lax.broadcasted_iota(jnp.int32, (nseg, dr), 0)

    @pl.when(p == 0)
    def _():
        xp = jnp.dot(pm_ref[...], xs_ref[0, pl.ds(start, tc), :], preferred_element_type=F32)
        pstart = pl.multiple_of(jnp.maximum(start - HALO, 0), HALO)
        nstart = pl.multiple_of(jnp.minimum(start + tc, seq - HALO), HALO)
        prev = xs_ref[0, pl.ds(pstart, HALO), :].astype(F32)
        nxt = xs_ref[0, pl.ds(nstart, HALO), :].astype(F32)
        prev = jnp.where(cidx > 0, prev, 0.0)
        nxt = jnp.where(cidx < nchunk - 1, nxt, 0.0)
        tm2 = jnp.where(sub == 0, prev[HALO - 2:HALO - 1], pltpu.roll(xp[(sl - 2) * nseg:(sl - 1) * nseg], 1, 0))
        tm1 = jnp.where(sub == 0, prev[HALO - 1:HALO], pltpu.roll(xp[(sl - 1) * nseg:sl * nseg], 1, 0))
        tp1 = jnp.where(sub == nseg - 1, nxt[0:1], pltpu.roll(xp[0:nseg], nseg - 1, 0))
        ext = jnp.concatenate([tm2, tm1, xp, tp1], axis=0)
        xc = cb_ref[...] + cw_ref[0:1, :] * ext[0:tc]
        for k in range(1, CONV_W):
            xc = xc + cw_ref[k:k + 1, :] * ext[k * nseg:k * nseg + tc]
        xc_s[pl.ds(start, tc), :] = xc

    xc = xc_s[pl.ds(start, tc), :]
    gates = jnp.dot(xc.astype(BF16), w_ref[p], preferred_element_type=F32) + b_ref[p]
    r = _sigmoid(gates[:, :dr])
    i = _sigmoid(gates[:, dr:])
    log_a = (-RG_C) * r * jax.nn.softplus(-lam_ref[p])
    a = jnp.exp(log_a)
    a_s[...] = a
    u_s[...] = jnp.sqrt(-jnp.tanh(log_a) * (1.0 + a * a)) * (i * xc)

    @pl.when(jnp.logical_and(p == 0, j == 0))
    def _():
        hfin_ref[...] = jnp.zeros_like(hfin_ref)

    @pl.when(j == 0)
    def _():
        hc_s[0:1, :] = h0_ref[0, pl.ds(p, 1), :]

    def segment_scan(reverse):
        def body(q, carry):
            t = (sl - 1 - q) if reverse else q
            rows = pl.ds(pl.multiple_of(t * nseg, nseg), nseg)
            h, pr = carry
            av = a_s[rows, :]
            h = av * h + u_s[rows, :]
            pr = av * pr
            hl_s[rows, :] = h
            p_s[rows, :] = pr
            return h, pr
        h_end, p_end = lax.fori_loop(0, sl, body, (jnp.zeros((nseg, dr), F32), jnp.ones((nseg, dr), F32)),
                                     unroll=4)
        carry = hc_s[0:1, :]
        for g in (range(nseg - 1, -1, -1) if reverse else range(nseg)):
            c_s[g:g + 1, :] = carry
            carry = h_end[g:g + 1, :] + p_end[g:g + 1, :] * carry
        hc_s[0:1, :] = carry
        return c_s[...]

    def corrected(cin):
        h = hl_s[...].reshape(sl, nseg, dr) + p_s[...].reshape(sl, nseg, dr) * cin[None]
        return h.reshape(tc, dr)

    @pl.when(p == 0)
    def _():
        hf_s[pl.ds(start, tc), :] = corrected(segment_scan(False))

    @pl.when(p == 1)
    def _():
        tot = corrected(segment_scan(True)) + hf_s[pl.ds(start, tc), :]
        out_ref[0] = jnp.dot(pmt_ref[...], tot.astype(BF16), preferred_element_type=F32).astype(BF16)

    @pl.when(j == nchunk - 1)
    def _():
        hfin_ref[0, pl.ds(p, 1), :] = hc_s[0:1, :]


def _rg_scan(xs, h0, conv_w, conv_b, wcat, bcat, lam, tc):
    b, s, dr = xs.shape
    nchunk = s // tc
    last = nchunk - 1
    sl = tc // SUBLANES
    src = (np.arange(tc) % SUBLANES) * sl + np.arange(tc) // SUBLANES
    pm = np.zeros((tc, tc), np.float32)
    pm[np.arange(tc), src] = 1.0
    chunk_buf = pltpu.VMEM((tc, dr), F32)
    full2 = lambda shape: pl.BlockSpec(shape, lambda i, p, j: (0,) * len(shape))
    return pl.pallas_call(
        functools.partial(_rg_kernel, tc=tc, nchunk=nchunk, seq=s, dr=dr),
        grid=(b, 2, nchunk),
        in_specs=[pl.BlockSpec((1, s, dr), lambda i, p, j: (i, 0, 0)),
                  pl.BlockSpec((1, SUBLANES, dr), lambda i, p, j: (i, 0, 0)),
                  full2((CONV_W, dr)), full2((1, dr)),
                  full2((2, dr, 2 * dr)), full2((2, 1, 2 * dr)), full2((2, 1, dr)),
                  full2((tc, tc)), full2((tc, tc))],
        out_specs=[pl.BlockSpec((1, tc, dr), lambda i, p, j: (i, jnp.where(p == 0, last, last - j), 0)),
                   pl.BlockSpec((1, SUBLANES, dr), lambda i, p, j: (i, 0, 0))],
        out_shape=[jax.ShapeDtypeStruct((b, s, dr), BF16),
                   jax.ShapeDtypeStruct((b, SUBLANES, dr), F32)],
        scratch_shapes=[pltpu.VMEM((s, dr), F32), pltpu.VMEM((s, dr), F32),
                        chunk_buf, chunk_buf, chunk_buf, chunk_buf,
                        pltpu.VMEM((SUBLANES, dr), F32), pltpu.VMEM((SUBLANES, dr), F32)],
        compiler_params=_cparams(("arbitrary", "arbitrary", "arbitrary")),
        name="rg_scan",
    )(xs, h0, conv_w, conv_b.reshape(1, dr), wcat, bcat, lam, jnp.asarray(pm, BF16), jnp.asarray(pm.T, BF16))


def _f1_kernel(d_ref, x_ref, y_ref):
    y_ref[0] = jnp.dot(d_ref[...], x_ref[0], preferred_element_type=F32).astype(BF16)


def _fourier_stage1(fv, d2, tl):
    b, r, n = fv.shape
    return pl.pallas_call(
        _f1_kernel,
        grid=(b, n // tl),
        in_specs=[pl.BlockSpec((2 * r, r), lambda i, l: (0, 0)),
                  pl.BlockSpec((1, r, tl), lambda i, l: (i, 0, l))],
        out_specs=pl.BlockSpec((1, 2 * r, tl), lambda i, l: (i, 0, l)),
        out_shape=jax.ShapeDtypeStruct((b, 2 * r, n), BF16),
        compiler_params=_cparams(("parallel", "arbitrary")),
        name="fourier_stage1",
    )(d2, fv)


def _f2_kernel(y_ref, e_ref, bc_ref, bs_ref, g_ref, o_ref, obuf, *, kb, df):
    zr, zi = [], []
    for q in range(kb):
        yk = jnp.concatenate([y_ref[0, 0, q], y_ref[0, 1, q]], axis=0)
        z = jnp.dot(e_ref[q], yk, preferred_element_type=F32)
        zr.append(z[:GRID_W])
        zi.append(z[GRID_W:])
    zr = jnp.concatenate(zr, axis=0).astype(BF16)
    zi = jnp.concatenate(zi, axis=0).astype(BF16)
    o = jnp.dot(zr, bc_ref[...], preferred_element_type=F32)
    o += jnp.dot(zi, bs_ref[...], preferred_element_type=F32)
    on = _rms(o, g_ref[...])

    def store_o(pos, c, tile):
        o_ref[0, pos, :, c * LANES:(c + 1) * LANES] = tile.astype(BF16)
    _pitched_store(on, obuf, 0)
    _pitched_gather(obuf, kb, store_o)


def _fourier_stage2(y5, etab, bdc, bds, g, kb):
    b, _, r, w, df = y5.shape
    return pl.pallas_call(
        functools.partial(_f2_kernel, kb=kb, df=df),
        grid=(b, r // kb),
        in_specs=[pl.BlockSpec((1, 2, kb, w, df), lambda i, k: (i, 0, k, 0, 0)),
                  pl.BlockSpec((kb, 2 * w, 2 * w), lambda i, k: (k, 0, 0)),
                  pl.BlockSpec((df, df), lambda i, k: (0, 0)),
                  pl.BlockSpec((df, df), lambda i, k: (0, 0)),
                  pl.BlockSpec((1, df), lambda i, k: (0, 0))],
        out_specs=pl.BlockSpec((1, w, kb, df), lambda i, k: (i, 0, k, 0)),
        out_shape=jax.ShapeDtypeStruct((b, w, r, df), BF16),
        scratch_shapes=[pltpu.VMEM((df // LANES, kb * _seg_pitch(GRID_W), LANES), F32)],
        compiler_params=_cparams(("parallel", "arbitrary")),
        name="fourier_stage2",
    )(y5, etab, bdc, bds, g.reshape(1, df))


def _stage_m_kernel(fn_ref, hs_ref, gg_ref, x_ref, gtm_ref, shf_ref, scf_ref, gr_ref, gffn_ref,
                    wo_ref, wr_ref, br_ref, j_ref, x1_ref, h2_ref, idx_ref, gate_ref, *, df):
    tm = x_ref.shape[1]
    hs = hs_ref[0]
    blocks = []
    for r in range(tm // GRID_W):
        blk = hs[r * GRID_W:(r + 1) * GRID_W]
        if r % 2 == 1:
            blk = jnp.dot(j_ref[...], blk, preferred_element_type=F32)
        blocks.append(blk.astype(F32))
    rg = jnp.concatenate(blocks, axis=0) * gg_ref[0].astype(F32)
    rgn = _rms(rg, gr_ref[...]).astype(BF16)
    mix = jnp.dot(fn_ref[0], wo_ref[:df, :], preferred_element_type=F32)
    mix += jnp.dot(rgn, wo_ref[df:, :], preferred_element_type=F32)
    x1 = x_ref[0] + gtm_ref[0] * mix
    x1_ref[0] = x1
    h2 = _rms(x1, gffn_ref[...]) * (1.0 + scf_ref[0]) + shf_ref[0]
    h2_ref[0] = _pack_halves(h2)

    logits = _dot3_nt(wr_ref[...], h2) + br_ref[...]
    eidx = lax.broadcasted_iota(jnp.int32, logits.shape, 0)
    vals, idxs = [], []
    for _ in range(TOP_K):
        m = jnp.max(logits, axis=0, keepdims=True)
        sel = jnp.min(jnp.where(logits == m, eidx, N_EXPERTS), axis=0, keepdims=True)
        vals.append(m)
        idxs.append(sel)
        logits = jnp.where(eidx == sel, -jnp.inf, logits)
    ex = [jnp.exp(v - vals[0]) for v in vals]
    den = ex[0] + ex[1] + ex[2] + ex[3]
    for k in range(TOP_K):
        gate_ref[k:k + 1, :] = ex[k] / den
        idx_ref[k:k + 1, :] = idxs[k]


def _stage_m(fn, hs, gg, x, gt_m, sh_f, sc_f, g_out_r, g_ffn, w_out_bf, w_router_t, b_router, jmat, tm):
    b, s, d = x.shape
    df = fn.shape[2]
    dr = hs.shape[2]
    nt = s // tm
    ne = w_router_t.shape[0]
    vec = pl.BlockSpec((1, 1, d), lambda i, t: (i, 0, 0))
    half = lambda dd: pl.BlockSpec((1, tm, dd), lambda i, t: (i, t, 0))
    full = lambda shape: pl.BlockSpec(shape, lambda i, t: (0,) * len(shape))
    tok = pl.BlockSpec((TOP_K, tm), lambda i, t: (0, i * nt + t))
    return pl.pallas_call(
        functools.partial(_stage_m_kernel, df=df),
        grid=(b, nt),
        in_specs=[half(df), half(dr), half(dr), half(d), vec, vec, vec,
                  full((1, dr)), full((1, d)), full((d, d)), full((ne, d)), full((ne, 1)),
                  full((GRID_W, GRID_W))],
        out_specs=[half(d), half(d // 2), tok, tok],
        out_shape=[jax.ShapeDtypeStruct((b, s, d), F32), jax.ShapeDtypeStruct((b, s, d // 2), jnp.int32),
                   jax.ShapeDtypeStruct((TOP_K, b * s), jnp.int32),
                   jax.ShapeDtypeStruct((TOP_K, b * s), F32)],
        compiler_params=_cparams(("parallel", "arbitrary")),
        name="stage_m",
    )(fn, hs, gg, x, gt_m, sh_f, sc_f, g_out_r.reshape(1, dr), g_ffn.reshape(1, d), w_out_bf,
      w_router_t, b_router.reshape(ne, 1), jmat)


def _rank_kernel(idx_ref, tri_ref, rank_ref, cnt_ref, carry_s):
    c = pl.program_id(0)

    @pl.when(c == 0)
    def _():
        carry_s[...] = jnp.zeros_like(carry_s)

    l = idx_ref.shape[1]
    eidx = lax.broadcasted_iota(jnp.int32, (N_EXPERTS, l), 0)
    for k in range(TOP_K):
        onehot = eidx == idx_ref[k:k + 1, :]
        oh = jnp.where(onehot, 1.0, 0.0)
        prefix = jnp.dot(oh.astype(BF16), tri_ref[...], preferred_element_type=F32)
        carry = carry_s[:, 0:1]
        rank = jnp.sum(jnp.where(onehot, prefix - 1.0 + carry, 0.0), axis=0, keepdims=True)
        rank_ref[k:k + 1, :] = rank.astype(jnp.int32)
        carry_s[...] = carry_s[...] + jnp.sum(oh, axis=1, keepdims=True)
    cnt_ref[...] = carry_s[...].astype(jnp.int32)


def _dispatch_ranks(idx, tri, tl):
    k, t = idx.shape
    return pl.pallas_call(
        _rank_kernel,
        grid=(t // tl,),
        in_specs=[pl.BlockSpec((k, tl), lambda c: (0, c)),
                  pl.BlockSpec((tl, tl), lambda c: (0, 0))],
        out_specs=[pl.BlockSpec((k, tl), lambda c: (0, c)),
                   pl.BlockSpec((N_EXPERTS, LANES), lambda c: (0, 0))],
        out_shape=[jax.ShapeDtypeStruct((k, t), jnp.int32),
                   jax.ShapeDtypeStruct((N_EXPERTS, LANES), jnp.int32)],
        scratch_shapes=[pltpu.VMEM((N_EXPERTS, LANES), F32)],
        compiler_params=_cparams(("arbitrary",)),
        name="dispatch_ranks",
    )(idx, tri)


def _moe_kernel(be_ref, bv_ref, nu_ref, x_ref, wgu_ref, bgu_ref, wd_ref, bd_ref, o_ref, wgu_s, wd_s, *, dff):
    i = pl.program_id(0)
    h = x_ref.shape[1]

    @pl.when(jnp.logical_or(i == 0, be_ref[i] != be_ref[jnp.maximum(i - 1, 0)]))
    def _():
        wgu_s[...] = wgu_ref[0].astype(BF16)
        wd_s[...] = wd_ref[0].astype(BF16)

    @pl.when(i < nu_ref[0])
    def _():
        rows = lax.broadcasted_iota(jnp.int32, x_ref.shape, 0)
        xw = jnp.where(rows < bv_ref[i], x_ref[...], 0)
        xa, xb = _unpack_halves(xw)
        xa = xa.astype(BF16)
        xb = xb.astype(BF16)
        acc = None
        for c in range(dff // MOE_FF_CHUNK):
            gs = slice(c * MOE_FF_CHUNK, (c + 1) * MOE_FF_CHUNK)
            us = slice(dff + c * MOE_FF_CHUNK, dff + (c + 1) * MOE_FF_CHUNK)
            g = jnp.dot(xa, wgu_s[:h, gs], preferred_element_type=F32)
            g += jnp.dot(xb, wgu_s[h:, gs], preferred_element_type=F32)
            u = jnp.dot(xa, wgu_s[:h, us], preferred_element_type=F32)
            u += jnp.dot(xb, wgu_s[h:, us], preferred_element_type=F32)
            gt = jnp.minimum(g + bgu_ref[0, :, gs], SWIGLU_LIMIT)
            up = jnp.clip(u + bgu_ref[0, :, us], -SWIGLU_LIMIT, SWIGLU_LIMIT)
            act = (up + 1.0) * (gt * _sigmoid(SWIGLU_ALPHA * gt))
            part = jnp.dot(act.astype(BF16), wd_s[gs, :], preferred_element_type=F32)
            acc = part if acc is None else acc + part
        o_ref[...] = _pack_halves(acc + bd_ref[0])


def _moe_experts(blk_expert, blk_valid, n_used, xs, wgu, bgu, wd, bd, tmm):
    cap, h = xs.shape
    ne, d, dff2 = wgu.shape
    dff = dff2 // 2
    row_blk = lambda i, be, bv, nu: (jnp.maximum(jnp.minimum(i, nu[0] - 1), 0), 0)
    wsel = lambda i, be, bv, nu: (be[i], 0, 0)
    grid_spec = pltpu.PrefetchScalarGridSpec(
        num_scalar_prefetch=3,
        grid=(cap // tmm,),
        in_specs=[pl.BlockSpec((tmm, h), row_blk),
                  pl.BlockSpec((1, d, dff2), wsel),
                  pl.BlockSpec((1, 1, dff2), wsel),
                  pl.BlockSpec((1, dff, d), wsel),
                  pl.BlockSpec((1, 1, d), wsel)],
        out_specs=pl.BlockSpec((tmm, h), row_blk),
        scratch_shapes=[pltpu.VMEM((d, dff2), BF16), pltpu.VMEM((dff, d), BF16)],
    )
    return pl.pallas_call(
        functools.partial(_moe_kernel, dff=dff),
        grid_spec=grid_spec,
        out_shape=jax.ShapeDtypeStruct((cap, h), jnp.int32),
        compiler_params=_cparams(("arbitrary",)),
        name="moe_experts",
    )(blk_expert, blk_valid, n_used, xs, wgu, bgu.reshape(ne, 1, dff2), wd, bd.reshape(ne, 1, d))


SC_CHUNK = 64


def _sc_workers():
    info = plsc.get_sparse_core_info()
    return info.num_cores, info.num_subcores


def _sc_scatter(rows, dest, out_rows=None, out_ref=None):
    n, w = rows.shape
    nc, ns = _sc_workers()
    per_w = n // (nc * ns)
    assert per_w % SC_CHUNK == 0
    mesh = plsc.VectorSubcoreMesh(core_axis_name="c", subcore_axis_name="s")
    out_type = () if out_ref is not None else jax.ShapeDtypeStruct((out_rows, w), jnp.int32)

    @functools.partial(
        pl.kernel, mesh=mesh, out_type=out_type,
        scratch_types=[pltpu.VMEM((SC_CHUNK,), jnp.int32),
                       pltpu.VMEM((SC_CHUNK, w), jnp.int32),
                       pltpu.SemaphoreType.DMA],
    )
    def scatter_rows(rows_hbm, dest_hbm, out_hbm, idx_v, rows_v, sem):
        base = (lax.axis_index("s") * nc + lax.axis_index("c")) * per_w

        @pl.loop(0, per_w // SC_CHUNK)
        def _(j):
            off = pl.multiple_of(base + j * SC_CHUNK, SC_CHUNK)
            pltpu.sync_copy(rows_hbm.at[pl.ds(off, SC_CHUNK)], rows_v)
            pltpu.sync_copy(dest_hbm.at[pl.ds(off, SC_CHUNK)], idx_v)
            pltpu.async_copy(rows_v, out_hbm.at[idx_v], sem).wait()

    if out_ref is not None:
        scatter_rows(rows, dest, out_ref)
        return None
    return scatter_rows(rows, dest)


def _sc_scatter_ids(dest, out_rows):
    n = dest.shape[0]
    info = plsc.get_sparse_core_info()
    nc, ns, nl = info.num_cores, info.num_subcores, info.num_lanes
    per_w = n // (nc * ns)
    assert per_w % SC_CHUNK == 0
    mesh = plsc.VectorSubcoreMesh(core_axis_name="c", subcore_axis_name="s")

    @functools.partial(
        pl.kernel, mesh=mesh,
        out_type=jax.ShapeDtypeStruct((out_rows, LANES), jnp.int32),
        scratch_types=[pltpu.VMEM((SC_CHUNK,), jnp.int32),
                       pltpu.VMEM((SC_CHUNK, LANES), jnp.int32),
                       pltpu.SemaphoreType.DMA],
    )
    def scatter_ids(dest_hbm, out_hbm, idx_v, rows_v, sem):
        base = (lax.axis_index("s") * nc + lax.axis_index("c")) * per_w

        @pl.loop(0, per_w // SC_CHUNK)
        def _(j):
            off = pl.multiple_of(base + j * SC_CHUNK, SC_CHUNK)
            for r in range(SC_CHUNK):
                rows_v[r, pl.ds(0, nl)] = jnp.zeros((nl,), jnp.int32) + (off + r)
            pltpu.sync_copy(dest_hbm.at[pl.ds(off, SC_CHUNK)], idx_v)
            pltpu.async_copy(rows_v, out_hbm.at[idx_v], sem).wait()

    return scatter_ids(dest)


def _sc_gather(table, idx):
    n = idx.shape[0]
    w = table.shape[1]
    nc, ns = _sc_workers()
    per_w = n // (nc * ns)
    assert per_w % SC_CHUNK == 0
    mesh = plsc.VectorSubcoreMesh(core_axis_name="c", subcore_axis_name="s")

    @functools.partial(
        pl.kernel, mesh=mesh,
        out_type=jax.ShapeDtypeStruct((n, w), jnp.int32),
        scratch_types=[pltpu.VMEM((SC_CHUNK,), jnp.int32),
                       pltpu.VMEM((SC_CHUNK, w), jnp.int32),
                       pltpu.SemaphoreType.DMA],
    )
    def gather_rows(table_hbm, idx_hbm, out_hbm, idx_v, rows_v, sem):
        base = (lax.axis_index("s") * nc + lax.axis_index("c")) * per_w

        @pl.loop(0, per_w // SC_CHUNK)
        def _(j):
            off = pl.multiple_of(base + j * SC_CHUNK, SC_CHUNK)
            pltpu.sync_copy(idx_hbm.at[pl.ds(off, SC_CHUNK)], idx_v)
            pltpu.async_copy(table_hbm.at[idx_v], rows_v, sem).wait()
            pltpu.sync_copy(rows_v, out_hbm.at[pl.ds(off, SC_CHUNK)])

    return gather_rows(table, idx)


def _combine_kernel(x1_ref, y_ref, gate_ref, gtf_ref, g_ref, o_ref):
    h = y_ref.shape[3]
    gates = gate_ref[0]
    moe_a = moe_b = None
    for k in range(TOP_K):
        ya, yb = _unpack_halves(y_ref[k, 0])
        gk = gates[:, k:k + 1]
        moe_a = gk * ya if k == 0 else moe_a + gk * ya
        moe_b = gk * yb if k == 0 else moe_b + gk * yb
    za = x1_ref[0, :, :h] + gtf_ref[0, :, :h] * moe_a
    zb = x1_ref[0, :, h:] + gtf_ref[0, :, h:] * moe_b
    ms = (jnp.sum(za * za, axis=-1, keepdims=True) + jnp.sum(zb * zb, axis=-1, keepdims=True)) / (2 * h)
    inv = lax.rsqrt(ms + EPS)
    o_ref[0, :, :h] = za * inv * g_ref[:, :h]
    o_ref[0, :, h:] = zb * inv * g_ref[:, h:]


def _combine(x1, yk, gates_t, gt_f, g_final, tm):
    b, s, d = x1.shape
    h = yk.shape[3]
    return pl.pallas_call(
        _combine_kernel,
        grid=(b, s // tm),
        in_specs=[pl.BlockSpec((1, tm, d), lambda i, t: (i, t, 0)),
                  pl.BlockSpec((TOP_K, 1, tm, h), lambda i, t: (0, i, t, 0)),
                  pl.BlockSpec((1, tm, TOP_K), lambda i, t: (i, t, 0)),
                  pl.BlockSpec((1, 1, d), lambda i, t: (i, 0, 0)),
                  pl.BlockSpec((1, d), lambda i, t: (0, 0))],
        out_specs=pl.BlockSpec((1, tm, d), lambda i, t: (i, t, 0)),
        out_shape=jax.ShapeDtypeStruct((b, s, d), F32),
        compiler_params=_cparams(("parallel", "arbitrary")),
        name="combine",
    )(x1, yk, gates_t, gt_f, g_final.reshape(1, d))


def _dft_tables(rows, gd):
    seq = rows * GRID_W
    n = np.arange(rows)
    ang1 = 2.0 * np.pi * np.outer(n, n) / rows
    d2 = np.concatenate([np.cos(ang1), -np.sin(ang1)], axis=0)
    k1 = np.arange(rows)[:, None, None]
    k2 = np.arange(GRID_W)[None, :, None]
    n2 = np.arange(GRID_W)[None, None, :]
    ang2 = 2.0 * np.pi * ((n2 * (k1 + rows * k2)) % seq) / seq
    ec, es = np.cos(ang2), np.sin(ang2)
    etab = np.concatenate([np.concatenate([ec, es], axis=2),
                           np.concatenate([-es, ec], axis=2)], axis=1)
    c = np.arange(gd)
    angc = 2.0 * np.pi * np.outer(c, c) / gd
    scale = 1.0 / np.sqrt(seq * gd)
    return (jnp.asarray(d2, BF16), jnp.asarray(etab, BF16),
            jnp.asarray(np.cos(angc) * scale, F32), jnp.asarray(np.sin(angc) * scale, F32))


def _block_diag(w):
    h, i, o = w.shape
    eye = jnp.eye(h, dtype=w.dtype)
    return (eye[:, None, :, None] * w[:, :, None, :]).reshape(h * i, h * o)


def kernel(x, c, ctx, c_ctx, w_mod, b_mod, g_norm_mix, g_norm_ffn, w_in, w_fourier, conv_w, conv_b,
           rg_w_a, rg_b_a, rg_w_x, rg_b_x, rg_lam, g_out_fourier, g_out_rg, w_out, w_router, b_router,
           w_gate_up, b_gate_up, w_down, b_down, g_final):
    assert w_mod.shape[0] == 1, "single-layer stack only"
    b, s, d = x.shape
    df = w_fourier.shape[1] * w_fourier.shape[2]
    dr = conv_w.shape[2]
    gd = w_fourier.shape[2]
    rows = s // GRID_W
    t = b * s
    ne = w_router.shape[2]

    mrows = -(-(b + 1) // SUBLANES) * SUBLANES
    cond = jnp.zeros((mrows, d), F32).at[:b].set(c).at[b].set(c_ctx)
    mod = _adaln(cond, w_mod[0], b_mod[0])
    sh_m, sc_m, gt_m, sh_f, sc_f, gt_f = [mod[:b, k * d:(k + 1) * d].reshape(b, 1, d) for k in range(N_MOD)]
    csh_m = mod[b:b + 1, 0:d].reshape(1, 1, d)
    csc_m = mod[b:b + 1, d:2 * d].reshape(1, 1, d)

    tm = min(s, TOKEN_TILE)
    d2, etab, cmat, smat = _dft_tables(rows, gd)
    jmat = jnp.asarray(np.eye(GRID_W)[::-1].copy(), BF16)

    w_in_bf = w_in[0].astype(BF16)
    f, xs, gg = _stage_b(x, sh_m, sc_m, g_norm_mix[0], w_in_bf, jmat, df, dr, tm=tm)
    xr_ctx = _stage_b_ctx(ctx, csh_m, csc_m, g_norm_mix[0], w_in_bf[:, df:df + dr])

    wcat = jnp.stack([jnp.concatenate([_block_diag(rg_w_a[0, dd]), _block_diag(rg_w_x[0, dd])], axis=1)
                      for dd in range(2)]).astype(BF16)
    bcat = jnp.concatenate([rg_b_a[0], rg_b_x[0]], axis=1).reshape(2, 1, 2 * dr)
    lam = rg_lam[0].reshape(2, 1, dr)
    h0 = jnp.zeros((b, SUBLANES, dr), F32)
    _, hfin_ctx = _rg_scan(xr_ctx, h0, conv_w[0], conv_b[0], wcat, bcat, lam, tc=ctx.shape[1])
    hs, _ = _rg_scan(xs, hfin_ctx, conv_w[0], conv_b[0], wcat, bcat, lam, tc=min(s, SCAN_CHUNK))

    cw, sw = _fold_fourier(cmat, smat, w_fourier[0])
    bdc = _block_diag(cw).astype(BF16)
    bds = _block_diag(sw).astype(BF16)
    y = _fourier_stage1(f, d2, tl=min(GRID_W * df, 8192))
    fn = _fourier_stage2(y.reshape(b, 2, rows, GRID_W, df), etab, bdc, bds, g_out_fourier[0],
                         kb=min(rows, 16))
    fn = fn.reshape(b, s, df)

    x1, h2, idx, gates = _stage_m(fn, hs, gg, x, gt_m, sh_f, sc_f, g_out_rg[0], g_norm_ffn[0],
                                  w_out[0].astype(BF16), w_router[0].T, b_router[0], jmat, tm=tm)

    tl = min(t, RANK_TILE)
    tri = jnp.asarray(np.triu(np.ones((tl, tl))), BF16)
    rank, cnt = _dispatch_ranks(idx, tri, tl)
    counts = cnt[:, 0]
    tmm = MOE_ROW_TILE
    padded = (counts + tmm - 1) // tmm * tmm
    pad_end = jnp.cumsum(padded)
    pad_start = pad_end - padded
    eids = jnp.arange(ne, dtype=jnp.int32)
    dest = rank + jnp.sum(jnp.where(idx[:, :, None] == eids, pad_start, 0), axis=-1)
    n_blocks = -(-(t * TOP_K) // tmm) + ne
    cap = n_blocks * tmm
    n_used = (pad_end[-1] // tmm).astype(jnp.int32).reshape(1)
    blk_start = jnp.arange(n_blocks, dtype=jnp.int32) * tmm
    blk_expert = jnp.sum(blk_start[:, None] >= pad_end[None, :], axis=1).astype(jnp.int32)
    last_expert = jnp.sum(pad_end[-1] - tmm >= pad_end).astype(jnp.int32)
    blk_expert = jnp.minimum(blk_expert, last_expert)
    sel = blk_expert[:, None] == eids
    blk_first = jnp.sum(jnp.where(sel, pad_start, 0), axis=1)
    blk_count = jnp.sum(jnp.where(sel, counts, 0), axis=1)
    blk_valid = jnp.clip(blk_count - (blk_start - blk_first), 0, tmm).astype(jnp.int32)

    na = TOP_K * t
    inv = _sc_scatter_ids(dest.reshape(-1), cap)[:, 0].reshape(n_blocks, tmm)
    live = jnp.arange(tmm, dtype=jnp.int32)[None, :] < blk_valid[:, None]
    src_tok = jnp.where(live, inv % t, 0).reshape(-1)
    dst_row = jnp.where(live, inv, na).reshape(-1)

    h2_rows = h2.reshape(t, d // 2)
    y_all = jax.empty_ref(jax.ShapeDtypeStruct(((TOP_K + 1) * t, d // 2), jnp.int32))
    assert n_blocks % MOE_PIPE == 0
    nq = n_blocks // MOE_PIPE
    for q in range(MOE_PIPE):
        blocks = slice(q * nq, (q + 1) * nq)
        rows = slice(q * nq * tmm, (q + 1) * nq * tmm)
        x_q = _sc_gather(h2_rows, src_tok[rows])
        nu_q = jnp.clip(n_used - q * nq, 0, nq).astype(jnp.int32)
        y_q = _moe_experts(blk_expert[blocks], blk_valid[blocks], nu_q, x_q,
                           w_gate_up[0], b_gate_up[0], w_down[0], b_down[0], tmm)
        _sc_scatter(y_q, dst_row[rows], out_ref=y_all)
    yk = y_all[...].reshape(TOP_K + 1, b, s, d // 2)
    return _combine(x1, yk, gates.T.reshape(b, s, TOP_K), gt_f, g_final, tm)
```

```python
import functools

import numpy as np
import jax
import jax.numpy as jnp
from jax import lax
from jax.experimental import pallas as pl
from jax.experimental.pallas import tpu as pltpu
from jax.experimental.pallas import tpu_sc as plsc

GRID_W = 64
FOURIER_GROUPS = 4
RG_HEADS = 8
CONV_W = 4
CONV_PAD_LO = 2
RG_C = 8.0
N_EXPERTS = 32
TOP_K = 4
SWIGLU_LIMIT = 7.0
SWIGLU_ALPHA = 1.702
N_MOD = 6
EPS = 1e-6

LANES = 128
SUBLANES = 8
VMEM_LIMIT_BYTES = 56 * 1024 * 1024
TOKEN_TILE = 1024
RANK_TILE = 512
SCAN_CHUNK = 512
MOE_ROW_TILE = 512
MOE_FF_CHUNK = 512
MOE_PIPE = 4

F32 = jnp.float32
BF16 = jnp.bfloat16


def _cparams(sem):
    return pltpu.CompilerParams(dimension_semantics=sem, vmem_limit_bytes=VMEM_LIMIT_BYTES)


def _split_bf16(a):
    hi = a.astype(BF16)
    lo = (a - hi.astype(F32)).astype(BF16)
    return hi, lo


def _dot3(a, b):
    ah, al = _split_bf16(a)
    bh, bl = _split_bf16(b)
    out = jnp.dot(ah, bh, preferred_element_type=F32)
    out += jnp.dot(ah, bl, preferred_element_type=F32)
    out += jnp.dot(al, bh, preferred_element_type=F32)
    return out


def _dot3_nt(a, b):
    dn = (((1,), (1,)), ((), ()))
    ah, al = _split_bf16(a)
    bh, bl = _split_bf16(b)
    out = lax.dot_general(ah, bh, dn, preferred_element_type=F32)
    out += lax.dot_general(ah, bl, dn, preferred_element_type=F32)
    out += lax.dot_general(al, bh, dn, preferred_element_type=F32)
    return out


def _gelu_tanh(x):
    return 0.5 * x * (1.0 + jnp.tanh(0.7978845608028654 * (x + 0.044715 * (x * x * x))))


def _rms(x, g):
    return x * lax.rsqrt(jnp.mean(x * x, axis=-1, keepdims=True) + EPS) * g


def _pack_halves(v):
    h = v.shape[1] // 2
    hi = lax.bitcast_convert_type(v[:, :h].astype(BF16).astype(F32), jnp.uint32)
    lo = lax.bitcast_convert_type(v[:, h:].astype(BF16).astype(F32), jnp.uint32)
    return lax.bitcast_convert_type(hi | (lo >> 16), jnp.int32)


def _unpack_halves(w):
    u = lax.bitcast_convert_type(w, jnp.uint32)
    hi = lax.bitcast_convert_type(u & jnp.uint32(0xFFFF0000), F32)
    lo = lax.bitcast_convert_type(u << 16, F32)
    return hi, lo


def _adaln_kernel(c_ref, w_ref, b_ref, o_ref):
    s = c_ref[...]
    s = s * jax.nn.sigmoid(s)
    o_ref[...] = _dot3(s, w_ref[...]) + b_ref[...]


def _adaln(cond, w_mod, b_mod):
    m, d = cond.shape
    n = w_mod.shape[1]
    tn = n // N_MOD
    return pl.pallas_call(
        _adaln_kernel,
        grid=(n // tn,),
        in_specs=[pl.BlockSpec((m, d), lambda i: (0, 0)),
                  pl.BlockSpec((d, tn), lambda i: (0, i)),
                  pl.BlockSpec((1, tn), lambda i: (0, i))],
        out_specs=pl.BlockSpec((m, tn), lambda i: (0, i)),
        out_shape=jax.ShapeDtypeStruct((m, n), F32),
        compiler_params=_cparams(("arbitrary",)),
        name="adaln",
    )(cond, w_mod, b_mod.reshape(1, n))


def _fold_kernel(c_ref, s_ref, w_ref, cw_ref, sw_ref):
    w = w_ref[0]
    cw_ref[0] = _dot3(c_ref[...], w)
    sw_ref[0] = _dot3(s_ref[...], w)


def _fold_fourier(cmat, smat, w_f):
    g, gd, _ = w_f.shape
    spec_m = pl.BlockSpec((gd, gd), lambda i: (0, 0))
    spec_w = pl.BlockSpec((1, gd, gd), lambda i: (i, 0, 0))
    return pl.pallas_call(
        _fold_kernel,
        grid=(g,),
        in_specs=[spec_m, spec_m, spec_w],
        out_specs=[spec_w, spec_w],
        out_shape=[jax.ShapeDtypeStruct((g, gd, gd), F32)] * 2,
        compiler_params=_cparams(("arbitrary",)),
        name="fold_fourier",
    )(cmat, smat, w_f)


def _seg_pitch(seg_len):
    n8 = seg_len // SUBLANES
    return SUBLANES * (n8 + 1 - n8 % 2)


def _pitched_store(v, buf, blk0):
    pitch = _seg_pitch(GRID_W)
    for r in range(v.shape[0] // GRID_W):
        for c in range(v.shape[1] // LANES):
            buf[c, (blk0 + r) * pitch:(blk0 + r) * pitch + GRID_W, :] = (
                v[r * GRID_W:(r + 1) * GRID_W, c * LANES:(c + 1) * LANES])


def _pitched_gather(buf, nb, store):
    pitch = _seg_pitch(GRID_W)
    for pos in range(GRID_W):
        for c in range(buf.shape[0]):
            store(pos, c, buf[c, pl.ds(pos, nb, stride=pitch), :])


def _stage_b_kernel(x_ref, sh_ref, sc_ref, g_ref, w_ref, j_ref, f_ref, xs_ref, gg_ref, fbuf, *, df, dr):
    tm = x_ref.shape[1]
    h = _rms(x_ref[0], g_ref[...]) * (1.0 + sc_ref[0]) + sh_ref[0]
    hb = h.astype(BF16)
    _pitched_store(jnp.dot(hb, w_ref[:, :df], preferred_element_type=F32), fbuf, 0)

    def store_f(pos, c, tile):
        f_ref[0, :, pos * df + c * LANES:pos * df + (c + 1) * LANES] = tile.astype(BF16)
    _pitched_gather(fbuf, tm // GRID_W, store_f)
    xr = jnp.dot(hb, w_ref[:, df:df + dr], preferred_element_type=F32).astype(BF16)
    for r in range(tm // GRID_W):
        blk = xr[r * GRID_W:(r + 1) * GRID_W]
        if r % 2 == 1:
            blk = jnp.dot(j_ref[...], blk, preferred_element_type=F32).astype(BF16)
        xs_ref[0, r * GRID_W:(r + 1) * GRID_W, :] = blk
    gr = jnp.dot(hb, w_ref[:, df + dr:], preferred_element_type=F32)
    gg_ref[0] = _gelu_tanh(gr).astype(BF16)


def _stage_b(x, shift, scale, g, w_in_bf, jmat, df, dr, tm):
    b, s, d = x.shape
    n = w_in_bf.shape[1]
    vec = pl.BlockSpec((1, 1, d), lambda i, t: (i, 0, 0))
    out = pl.BlockSpec((1, tm, df), lambda i, t: (i, t, 0))
    nb = tm // GRID_W
    tok = jax.ShapeDtypeStruct((b, s, df), BF16)
    return pl.pallas_call(
        functools.partial(_stage_b_kernel, df=df, dr=dr),
        grid=(b, s // tm),
        in_specs=[pl.BlockSpec((1, tm, d), lambda i, t: (i, t, 0)), vec, vec,
                  pl.BlockSpec((1, d), lambda i, t: (0, 0)),
                  pl.BlockSpec((d, n), lambda i, t: (0, 0)),
                  pl.BlockSpec((GRID_W, GRID_W), lambda i, t: (0, 0))],
        out_specs=[pl.BlockSpec((1, nb, GRID_W * df), lambda i, t: (i, t, 0)), out, out],
        out_shape=[jax.ShapeDtypeStruct((b, s // GRID_W, GRID_W * df), BF16), tok, tok],
        scratch_shapes=[pltpu.VMEM((df // LANES, nb * _seg_pitch(GRID_W), LANES), F32)],
        compiler_params=_cparams(("parallel", "arbitrary")),
        name="stage_b",
    )(x, shift, scale, g.reshape(1, d), w_in_bf, jmat)


def _stage_b_ctx_kernel(x_ref, sh_ref, sc_ref, g_ref, w_ref, xr_ref):
    h = _rms(x_ref[0], g_ref[...]) * (1.0 + sc_ref[0]) + sh_ref[0]
    xr_ref[0] = jnp.dot(h.astype(BF16), w_ref[...], preferred_element_type=F32).astype(BF16)


def _stage_b_ctx(ctx, shift, scale, g, w_xr_bf):
    b, s, d = ctx.shape
    dr = w_xr_bf.shape[1]
    vec = pl.BlockSpec((1, 1, d), lambda i: (0, 0, 0))
    return pl.pallas_call(
        _stage_b_ctx_kernel,
        grid=(b,),
        in_specs=[pl.BlockSpec((1, s, d), lambda i: (i, 0, 0)), vec, vec,
                  pl.BlockSpec((1, d), lambda i: (0, 0)),
                  pl.BlockSpec((d, dr), lambda i: (0, 0))],
        out_specs=pl.BlockSpec((1, s, dr), lambda i: (i, 0, 0)),
        out_shape=jax.ShapeDtypeStruct((b, s, dr), BF16),
        compiler_params=_cparams(("arbitrary",)),
        name="stage_b_ctx",
    )(ctx, shift, scale, g.reshape(1, d), w_xr_bf)


HALO = 16


def _sigmoid(x):
    return 0.5 * jnp.tanh(0.5 * x) + 0.5


def _rg_kernel(xs_ref, h0_ref, cw_ref, cb_ref, w_ref, b_ref, lam_ref, pm_ref, pmt_ref, out_ref, hfin_ref,
               hf_s, xc_s, a_s, u_s, hl_s, p_s, c_s, hc_s, *, tc, nchunk, seq, dr):
    p = pl.program_id(1)
    j = pl.program_id(2)
    cidx = jnp.where(p == 0, j, nchunk - 1 - j)
    start = pl.multiple_of(cidx * tc, tc)
    nseg = SUBLANES
    sl = tc // nseg
    sub = lax.broadcasted_iota(jnp.int32, (nseg, dr), 0)

    @pl.when(p == 0)
    def _():
        xp = jnp.dot(pm_ref[...], xs_ref[0, pl.ds(start, tc), :], preferred_element_type=F32)
        pstart = pl.multiple_of(jnp.maximum(start - HALO, 0), HALO)
        nstart = pl.multiple_of(jnp.minimum(start + tc, seq - HALO), HALO)
        prev = xs_ref[0, pl.ds(pstart, HALO), :].astype(F32)
        nxt = xs_ref[0, pl.ds(nstart, HALO), :].astype(F32)
        prev = jnp.where(cidx > 0, prev, 0.0)
        nxt = jnp.where(cidx < nchunk - 1, nxt, 0.0)
        tm2 = jnp.where(sub == 0, prev[HALO - 2:HALO - 1], pltpu.roll(xp[(sl - 2) * nseg:(sl - 1) * nseg], 1, 0))
        tm1 = jnp.where(sub == 0, prev[HALO - 1:HALO], pltpu.roll(xp[(sl - 1) * nseg:sl * nseg], 1, 0))
        tp1 = jnp.where(sub == nseg - 1, nxt[0:1], pltpu.roll(xp[0:nseg], nseg - 1, 0))
        ext = jnp.concatenate([tm2, tm1, xp, tp1], axis=0)
        xc = cb_ref[...] + cw_ref[0:1, :] * ext[0:tc]
        for k in range(1, CONV_W):
            xc = xc + cw_ref[k:k + 1, :] * ext[k * nseg:k * nseg + tc]
        xc_s[pl.ds(start, tc), :] = xc

    xc = xc_s[pl.ds(start, tc), :]
    gates = jnp.dot(xc.astype(BF16), w_ref[p], preferred_element_type=F32) + b_ref[p]
    r = _sigmoid(gates[:, :dr])
    i = _sigmoid(gates[:, dr:])
    log_a = (-RG_C) * r * jax.nn.softplus(-lam_ref[p])
    a = jnp.exp(log_a)
    a_s[...] = a
    u_s[...] = jnp.sqrt(-jnp.tanh(log_a) * (1.0 + a * a)) * (i * xc)

    @pl.when(jnp.logical_and(p == 0, j == 0))
    def _():
        hfin_ref[...] = jnp.zeros_like(hfin_ref)

    @pl.when(j == 0)
    def _():
        hc_s[0:1, :] = h0_ref[0, pl.ds(p, 1), :]

    def segment_scan(reverse):
        def body(q, carry):
            t = (sl - 1 - q) if reverse else q
            rows = pl.ds(pl.multiple_of(t * nseg, nseg), nseg)
            h, pr = carry
            av = a_s[rows, :]
            h = av * h + u_s[rows, :]
            pr = av * pr
            hl_s[rows, :] = h
            p_s[rows, :] = pr
            return h, pr
        h_end, p_end = lax.fori_loop(0, sl, body, (jnp.zeros((nseg, dr), F32), jnp.ones((nseg, dr), F32)),
                                     unroll=4)
        carry = hc_s[0:1, :]
        for g in (range(nseg - 1, -1, -1) if reverse else range(nseg)):
            c_s[g:g + 1, :] = carry
            carry = h_end[g:g + 1, :] + p_end[g:g + 1, :] * carry
        hc_s[0:1, :] = carry
        return c_s[...]

    def corrected(cin):
        h = hl_s[...].reshape(sl, nseg, dr) + p_s[...].reshape(sl, nseg, dr) * cin[None]
        return h.reshape(tc, dr)

    @pl.when(p == 0)
    def _():
        hf_s[pl.ds(start, tc), :] = corrected(segment_scan(False))

    @pl.when(p == 1)
    def _():
        tot = corrected(segment_scan(True)) + hf_s[pl.ds(start, tc), :]
        out_ref[0] = jnp.dot(pmt_ref[...], tot.astype(BF16), preferred_element_type=F32).astype(BF16)

    @pl.when(j == nchunk - 1)
    def _():
        hfin_ref[0, pl.ds(p, 1), :] = hc_s[0:1, :]


def _rg_scan(xs, h0, conv_w, conv_b, wcat, bcat, lam, tc):
    b, s, dr = xs.shape
    nchunk = s // tc
    last = nchunk - 1
    sl = tc // SUBLANES
    src = (np.arange(tc) % SUBLANES) * sl + np.arange(tc) // SUBLANES
    pm = np.zeros((tc, tc), np.float32)
    pm[np.arange(tc), src] = 1.0
    chunk_buf = pltpu.VMEM((tc, dr), F32)
    full2 = lambda shape: pl.BlockSpec(shape, lambda i, p, j: (0,) * len(shape))
    return pl.pallas_call(
        functools.partial(_rg_kernel, tc=tc, nchunk=nchunk, seq=s, dr=dr),
        grid=(b, 2, nchunk),
        in_specs=[pl.BlockSpec((1, s, dr), lambda i, p, j: (i, 0, 0)),
                  pl.BlockSpec((1, SUBLANES, dr), lambda i, p, j: (i, 0, 0)),
                  full2((CONV_W, dr)), full2((1, dr)),
                  full2((2, dr, 2 * dr)), full2((2, 1, 2 * dr)), full2((2, 1, dr)),
                  full2((tc, tc)), full2((tc, tc))],
        out_specs=[pl.BlockSpec((1, tc, dr), lambda i, p, j: (i, jnp.where(p == 0, last, last - j), 0)),
                   pl.BlockSpec((1, SUBLANES, dr), lambda i, p, j: (i, 0, 0))],
        out_shape=[jax.ShapeDtypeStruct((b, s, dr), BF16),
                   jax.ShapeDtypeStruct((b, SUBLANES, dr), F32)],
        scratch_shapes=[pltpu.VMEM((s, dr), F32), pltpu.VMEM((s, dr), F32),
                        chunk_buf, chunk_buf, chunk_buf, chunk_buf,
                        pltpu.VMEM((SUBLANES, dr), F32), pltpu.VMEM((SUBLANES, dr), F32)],
        compiler_params=_cparams(("arbitrary", "arbitrary", "arbitrary")),
        name="rg_scan",
    )(xs, h0, conv_w, conv_b.reshape(1, dr), wcat, bcat, lam, jnp.asarray(pm, BF16), jnp.asarray(pm.T, BF16))


def _f1_kernel(d_ref, x_ref, y_ref):
    y_ref[0] = jnp.dot(d_ref[...], x_ref[0], preferred_element_type=F32).astype(BF16)


def _fourier_stage1(fv, d2, tl):
    b, r, n = fv.shape
    return pl.pallas_call(
        _f1_kernel,
        grid=(b, n // tl),
        in_specs=[pl.BlockSpec((2 * r, r), lambda i, l: (0, 0)),
                  pl.BlockSpec((1, r, tl), lambda i, l: (i, 0, l))],
        out_specs=pl.BlockSpec((1, 2 * r, tl), lambda i, l: (i, 0, l)),
        out_shape=jax.ShapeDtypeStruct((b, 2 * r, n), BF16),
        compiler_params=_cparams(("parallel", "arbitrary")),
        name="fourier_stage1",
    )(d2, fv)


def _f2_kernel(y_ref, e_ref, bc_ref, bs_ref, g_ref, o_ref, obuf, *, kb, df):
    zr, zi = [], []
    for q in range(kb):
        yk = jnp.concatenate([y_ref[0, 0, q], y_ref[0, 1, q]], axis=0)
        z = jnp.dot(e_ref[q], yk, preferred_element_type=F32)
        zr.append(z[:GRID_W])
        zi.append(z[GRID_W:])
    zr = jnp.concatenate(zr, axis=0).astype(BF16)
    zi = jnp.concatenate(zi, axis=0).astype(BF16)
    o = jnp.dot(zr, bc_ref[...], preferred_element_type=F32)
    o += jnp.dot(zi, bs_ref[...], preferred_element_type=F32)
    on = _rms(o, g_ref[...])

    def store_o(pos, c, tile):
        o_ref[0, pos, :, c * LANES:(c + 1) * LANES] = tile.astype(BF16)
    _pitched_store(on, obuf, 0)
    _pitched_gather(obuf, kb, store_o)


def _fourier_stage2(y5, etab, bdc, bds, g, kb):
    b, _, r, w, df = y5.shape
    return pl.pallas_call(
        functools.partial(_f2_kernel, kb=kb, df=df),
        grid=(b, r // kb),
        in_specs=[pl.BlockSpec((1, 2, kb, w, df), lambda i, k: (i, 0, k, 0, 0)),
                  pl.BlockSpec((kb, 2 * w, 2 * w), lambda i, k: (k, 0, 0)),
                  pl.BlockSpec((df, df), lambda i, k: (0, 0)),
                  pl.BlockSpec((df, df), lambda i, k: (0, 0)),
                  pl.BlockSpec((1, df), lambda i, k: (0, 0))],
        out_specs=pl.BlockSpec((1, w, kb, df), lambda i, k: (i, 0, k, 0)),
        out_shape=jax.ShapeDtypeStruct((b, w, r, df), BF16),
        scratch_shapes=[pltpu.VMEM((df // LANES, kb * _seg_pitch(GRID_W), LANES), F32)],
        compiler_params=_cparams(("parallel", "arbitrary")),
        name="fourier_stage2",
    )(y5, etab, bdc, bds, g.reshape(1, df))


def _stage_m_kernel(fn_ref, hs_ref, gg_ref, x_ref, gtm_ref, shf_ref, scf_ref, gr_ref, gffn_ref,
                    wo_ref, wr_ref, br_ref, j_ref, x1_ref, h2_ref, idx_ref, gate_ref, *, df):
    tm = x_ref.shape[1]
    hs = hs_ref[0]
    blocks = []
    for r in range(tm // GRID_W):
        blk = hs[r * GRID_W:(r + 1) * GRID_W]
        if r % 2 == 1:
            blk = jnp.dot(j_ref[...], blk, preferred_element_type=F32)
        blocks.append(blk.astype(F32))
    rg = jnp.concatenate(blocks, axis=0) * gg_ref[0].astype(F32)
    rgn = _rms(rg, gr_ref[...]).astype(BF16)
    mix = jnp.dot(fn_ref[0], wo_ref[:df, :], preferred_element_type=F32)
    mix += jnp.dot(rgn, wo_ref[df:, :], preferred_element_type=F32)
    x1 = x_ref[0] + gtm_ref[0] * mix
    x1_ref[0] = x1
    h2 = _rms(x1, gffn_ref[...]) * (1.0 + scf_ref[0]) + shf_ref[0]
    h2_ref[0] = _pack_halves(h2)

    logits = _dot3_nt(wr_ref[...], h2) + br_ref[...]
    eidx = lax.broadcasted_iota(jnp.int32, logits.shape, 0)
    vals, idxs = [], []
    for _ in range(TOP_K):
        m = jnp.max(logits, axis=0, keepdims=True)
        sel = jnp.min(jnp.where(logits == m, eidx, N_EXPERTS), axis=0, keepdims=True)
        vals.append(m)
        idxs.append(sel)
        logits = jnp.where(eidx == sel, -jnp.inf, logits)
    ex = [jnp.exp(v - vals[0]) for v in vals]
    den = ex[0] + ex[1] + ex[2] + ex[3]
    for k in range(TOP_K):
        gate_ref[k:k + 1, :] = ex[k] / den
        idx_ref[k:k + 1, :] = idxs[k]


def _stage_m(fn, hs, gg, x, gt_m, sh_f, sc_f, g_out_r, g_ffn, w_out_bf, w_router_t, b_router, jmat, tm):
    b, s, d = x.shape
    df = fn.shape[2]
    dr = hs.shape[2]
    nt = s // tm
    ne = w_router_t.shape[0]
    vec = pl.BlockSpec((1, 1, d), lambda i, t: (i, 0, 0))
    half = lambda dd: pl.BlockSpec((1, tm, dd), lambda i, t: (i, t, 0))
    full = lambda shape: pl.BlockSpec(shape, lambda i, t: (0,) * len(shape))
    tok = pl.BlockSpec((TOP_K, tm), lambda i, t: (0, i * nt + t))
    return pl.pallas_call(
        functools.partial(_stage_m_kernel, df=df),
        grid=(b, nt),
        in_specs=[half(df), half(dr), half(dr), half(d), vec, vec, vec,
                  full((1, dr)), full((1, d)), full((d, d)), full((ne, d)), full((ne, 1)),
                  full((GRID_W, GRID_W))],
        out_specs=[half(d), half(d // 2), tok, tok],
        out_shape=[jax.ShapeDtypeStruct((b, s, d), F32), jax.ShapeDtypeStruct((b, s, d // 2), jnp.int32),
                   jax.ShapeDtypeStruct((TOP_K, b * s), jnp.int32),
                   jax.ShapeDtypeStruct((TOP_K, b * s), F32)],
        compiler_params=_cparams(("parallel", "arbitrary")),
        name="stage_m",
    )(fn, hs, gg, x, gt_m, sh_f, sc_f, g_out_r.reshape(1, dr), g_ffn.reshape(1, d), w_out_bf,
      w_router_t, b_router.reshape(ne, 1), jmat)


def _rank_kernel(idx_ref, tri_ref, rank_ref, cnt_ref, carry_s):
    c = pl.program_id(0)

    @pl.when(c == 0)
    def _():
        carry_s[...] = jnp.zeros_like(carry_s)

    l = idx_ref.shape[1]
    eidx = lax.broadcasted_iota(jnp.int32, (N_EXPERTS, l), 0)
    for k in range(TOP_K):
        onehot = eidx == idx_ref[k:k + 1, :]
        oh = jnp.where(onehot, 1.0, 0.0)
        prefix = jnp.dot(oh.astype(BF16), tri_ref[...], preferred_element_type=F32)
        carry = carry_s[:, 0:1]
        rank = jnp.sum(jnp.where(onehot, prefix - 1.0 + carry, 0.0), axis=0, keepdims=True)
        rank_ref[k:k + 1, :] = rank.astype(jnp.int32)
        carry_s[...] = carry_s[...] + jnp.sum(oh, axis=1, keepdims=True)
    cnt_ref[...] = carry_s[...].astype(jnp.int32)


def _dispatch_ranks(idx, tri, tl):
    k, t = idx.shape
    return pl.pallas_call(
        _rank_kernel,
        grid=(t // tl,),
        in_specs=[pl.BlockSpec((k, tl), lambda c: (0, c)),
                  pl.BlockSpec((tl, tl), lambda c: (0, 0))],
        out_specs=[pl.BlockSpec((k, tl), lambda c: (0, c)),
                   pl.BlockSpec((N_EXPERTS, LANES), lambda c: (0, 0))],
        out_shape=[jax.ShapeDtypeStruct((k, t), jnp.int32),
                   jax.ShapeDtypeStruct((N_EXPERTS, LANES), jnp.int32)],
        scratch_shapes=[pltpu.VMEM((N_EXPERTS, LANES), F32)],
        compiler_params=_cparams(("arbitrary",)),
        name="dispatch_ranks",
    )(idx, tri)


def _moe_kernel(be_ref, bv_ref, nu_ref, x_ref, wgu_ref, bgu_ref, wd_ref, bd_ref, o_ref, wgu_s, wd_s, *, dff):
    i = pl.program_id(0)
    h = x_ref.shape[1]

    @pl.when(jnp.logical_or(i == 0, be_ref[i] != be_ref[jnp.maximum(i - 1, 0)]))
    def _():
        wgu_s[...] = wgu_ref[0].astype(BF16)
        wd_s[...] = wd_ref[0].astype(BF16)

    @pl.when(i < nu_ref[0])
    def _():
        rows = lax.broadcasted_iota(jnp.int32, x_ref.shape, 0)
        xw = jnp.where(rows < bv_ref[i], x_ref[...], 0)
        xa, xb = _unpack_halves(xw)
        xa = xa.astype(BF16)
        xb = xb.astype(BF16)
        acc = None
        for c in range(dff // MOE_FF_CHUNK):
            gs = slice(c * MOE_FF_CHUNK, (c + 1) * MOE_FF_CHUNK)
            us = slice(dff + c * MOE_FF_CHUNK, dff + (c + 1) * MOE_FF_CHUNK)
            g = jnp.dot(xa, wgu_s[:h, gs], preferred_element_type=F32)
            g += jnp.dot(xb, wgu_s[h:, gs], preferred_element_type=F32)
            u = jnp.dot(xa, wgu_s[:h, us], preferred_element_type=F32)
            u += jnp.dot(xb, wgu_s[h:, us], preferred_element_type=F32)
            gt = jnp.minimum(g + bgu_ref[0, :, gs], SWIGLU_LIMIT)
            up = jnp.clip(u + bgu_ref[0, :, us], -SWIGLU_LIMIT, SWIGLU_LIMIT)
            act = (up + 1.0) * (gt * _sigmoid(SWIGLU_ALPHA * gt))
            part = jnp.dot(act.astype(BF16), wd_s[gs, :], preferred_element_type=F32)
            acc = part if acc is None else acc + part
        o_ref[...] = _pack_halves(acc + bd_ref[0])


def _moe_experts(blk_expert, blk_valid, n_used, xs, wgu, bgu, wd, bd, tmm):
    cap, h = xs.shape
    ne, d, dff2 = wgu.shape
    dff = dff2 // 2
    row_blk = lambda i, be, bv, nu: (jnp.maximum(jnp.minimum(i, nu[0] - 1), 0), 0)
    wsel = lambda i, be, bv, nu: (be[i], 0, 0)
    grid_spec = pltpu.PrefetchScalarGridSpec(
        num_scalar_prefetch=3,
        grid=(cap // tmm,),
        in_specs=[pl.BlockSpec((tmm, h), row_blk),
                  pl.BlockSpec((1, d, dff2), wsel),
                  pl.BlockSpec((1, 1, dff2), wsel),
                  pl.BlockSpec((1, dff, d), wsel),
                  pl.BlockSpec((1, 1, d), wsel)],
        out_specs=pl.BlockSpec((tmm, h), row_blk),
        scratch_shapes=[pltpu.VMEM((d, dff2), BF16), pltpu.VMEM((dff, d), BF16)],
    )
    return pl.pallas_call(
        functools.partial(_moe_kernel, dff=dff),
        grid_spec=grid_spec,
        out_shape=jax.ShapeDtypeStruct((cap, h), jnp.int32),
        compiler_params=_cparams(("arbitrary",)),
        name="moe_experts",
    )(blk_expert, blk_valid, n_used, xs, wgu, bgu.reshape(ne, 1, dff2), wd, bd.reshape(ne, 1, d))


SC_CHUNK = 64


def _sc_workers():
    info = plsc.get_sparse_core_info()
    return info.num_cores, info.num_subcores


def _sc_scatter(rows, dest, out_rows=None, out_ref=None):
    n, w = rows.shape
    nc, ns = _sc_workers()
    per_w = n // (nc * ns)
    assert per_w % SC_CHUNK == 0
    mesh = plsc.VectorSubcoreMesh(core_axis_name="c", subcore_axis_name="s")
    out_type = () if out_ref is not None else jax.ShapeDtypeStruct((out_rows, w), jnp.int32)

    @functools.partial(
        pl.kernel, mesh=mesh, out_type=out_type,
        scratch_types=[pltpu.VMEM((SC_CHUNK,), jnp.int32),
                       pltpu.VMEM((SC_CHUNK, w), jnp.int32),
                       pltpu.SemaphoreType.DMA],
    )
    def scatter_rows(rows_hbm, dest_hbm, out_hbm, idx_v, rows_v, sem):
        base = (lax.axis_index("s") * nc + lax.axis_index("c")) * per_w

        @pl.loop(0, per_w // SC_CHUNK)
        def _(j):
            off = pl.multiple_of(base + j * SC_CHUNK, SC_CHUNK)
            pltpu.sync_copy(rows_hbm.at[pl.ds(off, SC_CHUNK)], rows_v)
            pltpu.sync_copy(dest_hbm.at[pl.ds(off, SC_CHUNK)], idx_v)
            pltpu.async_copy(rows_v, out_hbm.at[idx_v], sem).wait()

    if out_ref is not None:
        scatter_rows(rows, dest, out_ref)
        return None
    return scatter_rows(rows, dest)


def _sc_scatter_ids(dest, out_rows):
    n = dest.shape[0]
    info = plsc.get_sparse_core_info()
    nc, ns, nl = info.num_cores, info.num_subcores, info.num_lanes
    per_w = n // (nc * ns)
    assert per_w % SC_CHUNK == 0
    mesh = plsc.VectorSubcoreMesh(core_axis_name="c", subcore_axis_name="s")

    @functools.partial(
        pl.kernel, mesh=mesh,
        out_type=jax.ShapeDtypeStruct((out_rows, LANES), jnp.int32),
        scratch_types=[pltpu.VMEM((SC_CHUNK,), jnp.int32),
                       pltpu.VMEM((SC_CHUNK, LANES), jnp.int32),
                       pltpu.SemaphoreType.DMA],
    )
    def scatter_ids(dest_hbm, out_hbm, idx_v, rows_v, sem):
        base = (lax.axis_index("s") * nc + lax.axis_index("c")) * per_w

        @pl.loop(0, per_w // SC_CHUNK)
        def _(j):
            off = pl.multiple_of(base + j * SC_CHUNK, SC_CHUNK)
            for r in range(SC_CHUNK):
                rows_v[r, pl.ds(0, nl)] = jnp.zeros((nl,), jnp.int32) + (off + r)
            pltpu.sync_copy(dest_hbm.at[pl.ds(off, SC_CHUNK)], idx_v)
            pltpu.async_copy(rows_v, out_hbm.at[idx_v], sem).wait()

    return scatter_ids(dest)


def _sc_gather(table, idx):
    n = idx.shape[0]
    w = table.shape[1]
    nc, ns = _sc_workers()
    per_w = n // (nc * ns)
    assert per_w % SC_CHUNK == 0
    mesh = plsc.VectorSubcoreMesh(core_axis_name="c", subcore_axis_name="s")

    @functools.partial(
        pl.kernel, mesh=mesh,
        out_type=jax.ShapeDtypeStruct((n, w), jnp.int32),
        scratch_types=[pltpu.VMEM((SC_CHUNK,), jnp.int32),
                       pltpu.VMEM((SC_CHUNK, w), jnp.int32),
                       pltpu.SemaphoreType.DMA],
    )
    def gather_rows(table_hbm, idx_hbm, out_hbm, idx_v, rows_v, sem):
        base = (lax.axis_index("s") * nc + lax.axis_index("c")) * per_w

        @pl.loop(0, per_w // SC_CHUNK)
        def _(j):
            off = pl.multiple_of(base + j * SC_CHUNK, SC_CHUNK)
            pltpu.sync_copy(idx_hbm.at[pl.ds(off, SC_CHUNK)], idx_v)
            pltpu.async_copy(table_hbm.at[idx_v], rows_v, sem).wait()
            pltpu.sync_copy(rows_v, out_hbm.at[pl.ds(off, SC_CHUNK)])

    return gather_rows(table, idx)


def _combine_kernel(x1_ref, y_ref, gate_ref, gtf_ref, g_ref, o_ref):
    h = y_ref.shape[3]
    gates = gate_ref[0]
    moe_a = moe_b = None
    for k in range(TOP_K):
        ya, yb = _unpack_halves(y_ref[k, 0])
        gk = gates[:, k:k + 1]
        moe_a = gk * ya if k == 0 else moe_a + gk * ya
        moe_b = gk * yb if k == 0 else moe_b + gk * yb
    za = x1_ref[0, :, :h] + gtf_ref[0, :, :h] * moe_a
    zb = x1_ref[0, :, h:] + gtf_ref[0, :, h:] * moe_b
    ms = (jnp.sum(za * za, axis=-1, keepdims=True) + jnp.sum(zb * zb, axis=-1, keepdims=True)) / (2 * h)
    inv = lax.rsqrt(ms + EPS)
    o_ref[0, :, :h] = za * inv * g_ref[:, :h]
    o_ref[0, :, h:] = zb * inv * g_ref[:, h:]


def _combine(x1, yk, gates_t, gt_f, g_final, tm):
    b, s, d = x1.shape
    h = yk.shape[3]
    return pl.pallas_call(
        _combine_kernel,
        grid=(b, s // tm),
        in_specs=[pl.BlockSpec((1, tm, d), lambda i, t: (i, t, 0)),
                  pl.BlockSpec((TOP_K, 1, tm, h), lambda i, t: (0, i, t, 0)),
                  pl.BlockSpec((1, tm, TOP_K), lambda i, t: (i, t, 0)),
                  pl.BlockSpec((1, 1, d), lambda i, t: (i, 0, 0)),
                  pl.BlockSpec((1, d), lambda i, t: (0, 0))],
        out_specs=pl.BlockSpec((1, tm, d), lambda i, t: (i, t, 0)),
        out_shape=jax.ShapeDtypeStruct((b, s, d), F32),
        compiler_params=_cparams(("parallel", "arbitrary")),
        name="combine",
    )(x1, yk, gates_t, gt_f, g_final.reshape(1, d))


def _dft_tables(rows, gd):
    seq = rows * GRID_W
    n = np.arange(rows)
    ang1 = 2.0 * np.pi * np.outer(n, n) / rows
    d2 = np.concatenate([np.cos(ang1), -np.sin(ang1)], axis=0)
    k1 = np.arange(rows)[:, None, None]
    k2 = np.arange(GRID_W)[None, :, None]
    n2 = np.arange(GRID_W)[None, None, :]
    ang2 = 2.0 * np.pi * ((n2 * (k1 + rows * k2)) % seq) / seq
    ec, es = np.cos(ang2), np.sin(ang2)
    etab = np.concatenate([np.concatenate([ec, es], axis=2),
                           np.concatenate([-es, ec], axis=2)], axis=1)
    c = np.arange(gd)
    angc = 2.0 * np.pi * np.outer(c, c) / gd
    scale = 1.0 / np.sqrt(seq * gd)
    return (jnp.asarray(d2, BF16), jnp.asarray(etab, BF16),
            jnp.asarray(np.cos(angc) * scale, F32), jnp.asarray(np.sin(angc) * scale, F32))


def _block_diag(w):
    h, i, o = w.shape
    eye = jnp.eye(h, dtype=w.dtype)
    return (eye[:, None, :, None] * w[:, :, None, :]).reshape(h * i, h * o)


def kernel(x, c, ctx, c_ctx, w_mod, b_mod, g_norm_mix, g_norm_ffn, w_in, w_fourier, conv_w, conv_b,
           rg_w_a, rg_b_a, rg_w_x, rg_b_x, rg_lam, g_out_fourier, g_out_rg, w_out, w_router, b_router,
           w_gate_up, b_gate_up, w_down, b_down, g_final):
    assert w_mod.shape[0] == 1, "single-layer stack only"
    b, s, d = x.shape
    df = w_fourier.shape[1] * w_fourier.shape[2]
    dr = conv_w.shape[2]
    gd = w_fourier.shape[2]
    rows = s // GRID_W
    t = b * s
    ne = w_router.shape[2]

    mrows = -(-(b + 1) // SUBLANES) * SUBLANES
    cond = jnp.zeros((mrows, d), F32).at[:b].set(c).at[b].set(c_ctx)
    mod = _adaln(cond, w_mod[0], b_mod[0])
    sh_m, sc_m, gt_m, sh_f, sc_f, gt_f = [mod[:b, k * d:(k + 1) * d].reshape(b, 1, d) for k in range(N_MOD)]
    csh_m = mod[b:b + 1, 0:d].reshape(1, 1, d)
    csc_m = mod[b:b + 1, d:2 * d].reshape(1, 1, d)

    tm = min(s, TOKEN_TILE)
    d2, etab, cmat, smat = _dft_tables(rows, gd)
    jmat = jnp.asarray(np.eye(GRID_W)[::-1].copy(), BF16)

    w_in_bf = w_in[0].astype(BF16)
    f, xs, gg = _stage_b(x, sh_m, sc_m, g_norm_mix[0], w_in_bf, jmat, df, dr, tm=tm)
    xr_ctx = _stage_b_ctx(ctx, csh_m, csc_m, g_norm_mix[0], w_in_bf[:, df:df + dr])

    wcat = jnp.stack([jnp.concatenate([_block_diag(rg_w_a[0, dd]), _block_diag(rg_w_x[0, dd])], axis=1)
                      for dd in range(2)]).astype(BF16)
    bcat = jnp.concatenate([rg_b_a[0], rg_b_x[0]], axis=1).reshape(2, 1, 2 * dr)
    lam = rg_lam[0].reshape(2, 1, dr)
    h0 = jnp.zeros((b, SUBLANES, dr), F32)
    _, hfin_ctx = _rg_scan(xr_ctx, h0, conv_w[0], conv_b[0], wcat, bcat, lam, tc=ctx.shape[1])
    hs, _ = _rg_scan(xs, hfin_ctx, conv_w[0], conv_b[0], wcat, bcat, lam, tc=min(s, SCAN_CHUNK))

    cw, sw = _fold_fourier(cmat, smat, w_fourier[0])
    bdc = _block_diag(cw).astype(BF16)
    bds = _block_diag(sw).astype(BF16)
    y = _fourier_stage1(f, d2, tl=min(GRID_W * df, 8192))
    fn = _fourier_stage2(y.reshape(b, 2, rows, GRID_W, df), etab, bdc, bds, g_out_fourier[0],
                         kb=min(rows, 16))
    fn = fn.reshape(b, s, df)

    x1, h2, idx, gates = _stage_m(fn, hs, gg, x, gt_m, sh_f, sc_f, g_out_rg[0], g_norm_ffn[0],
                                  w_out[0].astype(BF16), w_router[0].T, b_router[0], jmat, tm=tm)

    tl = min(t, RANK_TILE)
    tri = jnp.asarray(np.triu(np.ones((tl, tl))), BF16)
    rank, cnt = _dispatch_ranks(idx, tri, tl)
    counts = cnt[:, 0]
    tmm = MOE_ROW_TILE
    padded = (counts + tmm - 1) // tmm * tmm
    pad_end = jnp.cumsum(padded)
    pad_start = pad_end - padded
    eids = jnp.arange(ne, dtype=jnp.int32)
    dest = rank + jnp.sum(jnp.where(idx[:, :, None] == eids, pad_start, 0), axis=-1)
    n_blocks = -(-(t * TOP_K) // tmm) + ne
    cap = n_blocks * tmm
    n_used = (pad_end[-1] // tmm).astype(jnp.int32).reshape(1)
    blk_start = jnp.arange(n_blocks, dtype=jnp.int32) * tmm
    blk_expert = jnp.sum(blk_start[:, None] >= pad_end[None, :], axis=1).astype(jnp.int32)
    last_expert = jnp.sum(pad_end[-1] - tmm >= pad_end).astype(jnp.int32)
    blk_expert = jnp.minimum(blk_expert, last_expert)
    sel = blk_expert[:, None] == eids
    blk_first = jnp.sum(jnp.where(sel, pad_start, 0), axis=1)
    blk_count = jnp.sum(jnp.where(sel, counts, 0), axis=1)
    blk_valid = jnp.clip(blk_count - (blk_start - blk_first), 0, tmm).astype(jnp.int32)

    na = TOP_K * t
    inv = _sc_scatter_ids(dest.reshape(-1), cap)[:, 0].reshape(n_blocks, tmm)
    live = jnp.arange(tmm, dtype=jnp.int32)[None, :] < blk_valid[:, None]
    spread = jnp.arange(cap, dtype=jnp.int32).reshape(n_blocks, tmm) % t
    src_tok = jnp.where(live, inv % t, spread).reshape(-1)
    dst_row = jnp.where(live, inv, na + spread).reshape(-1)

    h2_rows = h2.reshape(t, d // 2)
    y_all = jax.empty_ref(jax.ShapeDtypeStruct(((TOP_K + 1) * t, d // 2), jnp.int32))
    assert n_blocks % MOE_PIPE == 0
    nq = n_blocks // MOE_PIPE
    for q in range(MOE_PIPE):
        blocks = slice(q * nq, (q + 1) * nq)
        rows = slice(q * nq * tmm, (q + 1) * nq * tmm)
        x_q = _sc_gather(h2_rows, src_tok[rows])
        nu_q = jnp.clip(n_used - q * nq, 0, nq).astype(jnp.int32)
        y_q = _moe_experts(blk_expert[blocks], blk_valid[blocks], nu_q, x_q,
                           w_gate_up[0], b_gate_up[0], w_down[0], b_down[0], tmm)
        _sc_scatter(y_q, dst_row[rows], out_ref=y_all)
    yk = y_all[...].reshape(TOP_K + 1, b, s, d // 2)
    return _combine(x1, yk, gates.T.reshape(b, s, TOP_K), gt_f, g_final, tm)
```

```python
import functools

import numpy as np
import jax
import jax.numpy as jnp
from jax import lax
from jax.experimental import pallas as pl
from jax.experimental.pallas import tpu as pltpu
from jax.experimental.pallas import tpu_sc as plsc

GRID_W = 64
FOURIER_GROUPS = 4
RG_HEADS = 8
CONV_W = 4
CONV_PAD_LO = 2
RG_C = 8.0
N_EXPERTS = 32
TOP_K = 4
SWIGLU_LIMIT = 7.0
SWIGLU_ALPHA = 1.702
N_MOD = 6
EPS = 1e-6

LANES = 128
SUBLANES = 8
VMEM_LIMIT_BYTES = 56 * 1024 * 1024
TOKEN_TILE = 1024
RANK_TILE = 512
SCAN_CHUNK = 512
MOE_ROW_TILE = 512
MOE_FF_CHUNK = 512
MOE_PIPE = (1, 3, 3, 1)

F32 = jnp.float32
BF16 = jnp.bfloat16


def _cparams(sem):
    return pltpu.CompilerParams(dimension_semantics=sem, vmem_limit_bytes=VMEM_LIMIT_BYTES)


def _split_bf16(a):
    hi = a.astype(BF16)
    lo = (a - hi.astype(F32)).astype(BF16)
    return hi, lo


def _dot3(a, b):
    ah, al = _split_bf16(a)
    bh, bl = _split_bf16(b)
    out = jnp.dot(ah, bh, preferred_element_type=F32)
    out += jnp.dot(ah, bl, preferred_element_type=F32)
    out += jnp.dot(al, bh, preferred_element_type=F32)
    return out


def _dot3_nt(a, b):
    dn = (((1,), (1,)), ((), ()))
    ah, al = _split_bf16(a)
    bh, bl = _split_bf16(b)
    out = lax.dot_general(ah, bh, dn, preferred_element_type=F32)
    out += lax.dot_general(ah, bl, dn, preferred_element_type=F32)
    out += lax.dot_general(al, bh, dn, preferred_element_type=F32)
    return out


def _gelu_tanh(x):
    return 0.5 * x * (1.0 + jnp.tanh(0.7978845608028654 * (x + 0.044715 * (x * x * x))))


def _rms(x, g):
    return x * lax.rsqrt(jnp.mean(x * x, axis=-1, keepdims=True) + EPS) * g


def _pack_halves(v):
    h = v.shape[1] // 2
    hi = lax.bitcast_convert_type(v[:, :h].astype(BF16).astype(F32), jnp.uint32)
    lo = lax.bitcast_convert_type(v[:, h:].astype(BF16).astype(F32), jnp.uint32)
    return lax.bitcast_convert_type(hi | (lo >> 16), jnp.int32)


def _unpack_halves(w):
    u = lax.bitcast_convert_type(w, jnp.uint32)
    hi = lax.bitcast_convert_type(u & jnp.uint32(0xFFFF0000), F32)
    lo = lax.bitcast_convert_type(u << 16, F32)
    return hi, lo


def _adaln_kernel(c_ref, w_ref, b_ref, o_ref):
    s = c_ref[...]
    s = s * jax.nn.sigmoid(s)
    o_ref[...] = _dot3(s, w_ref[...]) + b_ref[...]


def _adaln(cond, w_mod, b_mod):
    m, d = cond.shape
    n = w_mod.shape[1]
    tn = n // N_MOD
    return pl.pallas_call(
        _adaln_kernel,
        grid=(n // tn,),
        in_specs=[pl.BlockSpec((m, d), lambda i: (0, 0)),
                  pl.BlockSpec((d, tn), lambda i: (0, i)),
                  pl.BlockSpec((1, tn), lambda i: (0, i))],
        out_specs=pl.BlockSpec((m, tn), lambda i: (0, i)),
        out_shape=jax.ShapeDtypeStruct((m, n), F32),
        compiler_params=_cparams(("arbitrary",)),
        name="adaln",
    )(cond, w_mod, b_mod.reshape(1, n))


def _fold_kernel(c_ref, s_ref, w_ref, cw_ref, sw_ref):
    w = w_ref[0]
    cw_ref[0] = _dot3(c_ref[...], w)
    sw_ref[0] = _dot3(s_ref[...], w)


def _fold_fourier(cmat, smat, w_f):
    g, gd, _ = w_f.shape
    spec_m = pl.BlockSpec((gd, gd), lambda i: (0, 0))
    spec_w = pl.BlockSpec((1, gd, gd), lambda i: (i, 0, 0))
    return pl.pallas_call(
        _fold_kernel,
        grid=(g,),
        in_specs=[spec_m, spec_m, spec_w],
        out_specs=[spec_w, spec_w],
        out_shape=[jax.ShapeDtypeStruct((g, gd, gd), F32)] * 2,
        compiler_params=_cparams(("arbitrary",)),
        name="fold_fourier",
    )(cmat, smat, w_f)


def _seg_pitch(seg_len):
    n8 = seg_len // SUBLANES
    return SUBLANES * (n8 + 1 - n8 % 2)


def _pitched_store(v, buf, blk0):
    pitch = _seg_pitch(GRID_W)
    for r in range(v.shape[0] // GRID_W):
        for c in range(v.shape[1] // LANES):
            buf[c, (blk0 + r) * pitch:(blk0 + r) * pitch + GRID_W, :] = (
                v[r * GRID_W:(r + 1) * GRID_W, c * LANES:(c + 1) * LANES])


def _pitched_gather(buf, nb, store):
    pitch = _seg_pitch(GRID_W)
    for pos in range(GRID_W):
        for c in range(buf.shape[0]):
            store(pos, c, buf[c, pl.ds(pos, nb, stride=pitch), :])


def _stage_b_kernel(x_ref, sh_ref, sc_ref, g_ref, w_ref, j_ref, f_ref, xs_ref, gg_ref, fbuf, *, df, dr):
    tm = x_ref.shape[1]
    h = _rms(x_ref[0], g_ref[...]) * (1.0 + sc_ref[0]) + sh_ref[0]
    hb = h.astype(BF16)
    _pitched_store(jnp.dot(hb, w_ref[:, :df], preferred_element_type=F32), fbuf, 0)

    def store_f(pos, c, tile):
        f_ref[0, :, pos * df + c * LANES:pos * df + (c + 1) * LANES] = tile.astype(BF16)
    _pitched_gather(fbuf, tm // GRID_W, store_f)
    xr = jnp.dot(hb, w_ref[:, df:df + dr], preferred_element_type=F32).astype(BF16)
    for r in range(tm // GRID_W):
        blk = xr[r * GRID_W:(r + 1) * GRID_W]
        if r % 2 == 1:
            blk = jnp.dot(j_ref[...], blk, preferred_element_type=F32).astype(BF16)
        xs_ref[0, r * GRID_W:(r + 1) * GRID_W, :] = blk
    gr = jnp.dot(hb, w_ref[:, df + dr:], preferred_element_type=F32)
    gg_ref[0] = _gelu_tanh(gr).astype(BF16)


def _stage_b(x, shift, scale, g, w_in_bf, jmat, df, dr, tm):
    b, s, d = x.shape
    n = w_in_bf.shape[1]
    vec = pl.BlockSpec((1, 1, d), lambda i, t: (i, 0, 0))
    out = pl.BlockSpec((1, tm, df), lambda i, t: (i, t, 0))
    nb = tm // GRID_W
    tok = jax.ShapeDtypeStruct((b, s, df), BF16)
    return pl.pallas_call(
        functools.partial(_stage_b_kernel, df=df, dr=dr),
        grid=(b, s // tm),
        in_specs=[pl.BlockSpec((1, tm, d), lambda i, t: (i, t, 0)), vec, vec,
                  pl.BlockSpec((1, d), lambda i, t: (0, 0)),
                  pl.BlockSpec((d, n), lambda i, t: (0, 0)),
                  pl.BlockSpec((GRID_W, GRID_W), lambda i, t: (0, 0))],
        out_specs=[pl.BlockSpec((1, nb, GRID_W * df), lambda i, t: (i, t, 0)), out, out],
        out_shape=[jax.ShapeDtypeStruct((b, s // GRID_W, GRID_W * df), BF16), tok, tok],
        scratch_shapes=[pltpu.VMEM((df // LANES, nb * _seg_pitch(GRID_W), LANES), F32)],
        compiler_params=_cparams(("parallel", "arbitrary")),
        name="stage_b",
    )(x, shift, scale, g.reshape(1, d), w_in_bf, jmat)


def _stage_b_ctx_kernel(x_ref, sh_ref, sc_ref, g_ref, w_ref, xr_ref):
    h = _rms(x_ref[0], g_ref[...]) * (1.0 + sc_ref[0]) + sh_ref[0]
    xr_ref[0] = jnp.dot(h.astype(BF16), w_ref[...], preferred_element_type=F32).astype(BF16)


def _stage_b_ctx(ctx, shift, scale, g, w_xr_bf):
    b, s, d = ctx.shape
    dr = w_xr_bf.shape[1]
    vec = pl.BlockSpec((1, 1, d), lambda i: (0, 0, 0))
    return pl.pallas_call(
        _stage_b_ctx_kernel,
        grid=(b,),
        in_specs=[pl.BlockSpec((1, s, d), lambda i: (i, 0, 0)), vec, vec,
                  pl.BlockSpec((1, d), lambda i: (0, 0)),
                  pl.BlockSpec((d, dr), lambda i: (0, 0))],
        out_specs=pl.BlockSpec((1, s, dr), lambda i: (i, 0, 0)),
        out_shape=jax.ShapeDtypeStruct((b, s, dr), BF16),
        compiler_params=_cparams(("arbitrary",)),
        name="stage_b_ctx",
    )(ctx, shift, scale, g.reshape(1, d), w_xr_bf)


HALO = 16


def _sigmoid(x):
    return 0.5 * jnp.tanh(0.5 * x) + 0.5


def _rg_kernel(xs_ref, h0_ref, cw_ref, cb_ref, w_ref, b_ref, lam_ref, pm_ref, pmt_ref, out_ref, hfin_ref,
               hf_s, xc_s, a_s, u_s, hl_s, p_s, c_s, hc_s, *, tc, nchunk, seq, dr):
    p = pl.program_id(1)
    j = pl.program_id(2)
    cidx = jnp.where(p == 0, j, nchunk - 1 - j)
    start = pl.multiple_of(cidx * tc, tc)
    nseg = SUBLANES
    sl = tc // nseg
    sub = lax.broadcasted_iota(jnp.int32, (nseg, dr), 0)

    @pl.when(p == 0)
    def _():
        xp = jnp.dot(pm_ref[...], xs_ref[0, pl.ds(start, tc), :], preferred_element_type=F32)
        pstart = pl.multiple_of(jnp.maximum(start - HALO, 0), HALO)
        nstart = pl.multiple_of(jnp.minimum(start + tc, seq - HALO), HALO)
        prev = xs_ref[0, pl.ds(pstart, HALO), :].astype(F32)
        nxt = xs_ref[0, pl.ds(nstart, HALO), :].astype(F32)
        prev = jnp.where(cidx > 0, prev, 0.0)
        nxt = jnp.where(cidx < nchunk - 1, nxt, 0.0)
        tm2 = jnp.where(sub == 0, prev[HALO - 2:HALO - 1], pltpu.roll(xp[(sl - 2) * nseg:(sl - 1) * nseg], 1, 0))
        tm1 = jnp.where(sub == 0, prev[HALO - 1:HALO], pltpu.roll(xp[(sl - 1) * nseg:sl * nseg], 1, 0))
        tp1 = jnp.where(sub == nseg - 1, nxt[0:1], pltpu.roll(xp[0:nseg], nseg - 1, 0))
        ext = jnp.concatenate([tm2, tm1, xp, tp1], axis=0)
        xc = cb_ref[...] + cw_ref[0:1, :] * ext[0:tc]
        for k in range(1, CONV_W):
            xc = xc + cw_ref[k:k + 1, :] * ext[k * nseg:k * nseg + tc]
        xc_s[pl.ds(start, tc), :] = xc

    xc = xc_s[pl.ds(start, tc), :]
    gates = jnp.dot(xc.astype(BF16), w_ref[p], preferred_element_type=F32) + b_ref[p]
    r = _sigmoid(gates[:, :dr])
    i = _sigmoid(gates[:, dr:])
    log_a = (-RG_C) * r * jax.nn.softplus(-lam_ref[p])
    a = jnp.exp(log_a)
    a_s[...] = a
    u_s[...] = jnp.sqrt(-jnp.tanh(log_a) * (1.0 + a * a)) * (i * xc)

    @pl.when(jnp.logical_and(p == 0, j == 0))
    def _():
        hfin_ref[...] = jnp.zeros_like(hfin_ref)

    @pl.when(j == 0)
    def _():
        hc_s[0:1, :] = h0_ref[0, pl.ds(p, 1), :]

    def segment_scan(reverse):
        def body(q, carry):
            t = (sl - 1 - q) if reverse else q
            rows = pl.ds(pl.multiple_of(t * nseg, nseg), nseg)
            h, pr = carry
            av = a_s[rows, :]
            h = av * h + u_s[rows, :]
            pr = av * pr
            hl_s[rows, :] = h
            p_s[rows, :] = pr
            return h, pr
        h_end, p_end = lax.fori_loop(0, sl, body, (jnp.zeros((nseg, dr), F32), jnp.ones((nseg, dr), F32)),
                                     unroll=4)
        carry = hc_s[0:1, :]
        for g in (range(nseg - 1, -1, -1) if reverse else range(nseg)):
            c_s[g:g + 1, :] = carry
            carry = h_end[g:g + 1, :] + p_end[g:g + 1, :] * carry
        hc_s[0:1, :] = carry
        return c_s[...]

    def corrected(cin):
        h = hl_s[...].reshape(sl, nseg, dr) + p_s[...].reshape(sl, nseg, dr) * cin[None]
        return h.reshape(tc, dr)

    @pl.when(p == 0)
    def _():
        hf_s[pl.ds(start, tc), :] = corrected(segment_scan(False))

    @pl.when(p == 1)
    def _():
        tot = corrected(segment_scan(True)) + hf_s[pl.ds(start, tc), :]
        out_ref[0] = jnp.dot(pmt_ref[...], tot.astype(BF16), preferred_element_type=F32).astype(BF16)

    @pl.when(j == nchunk - 1)
    def _():
        hfin_ref[0, pl.ds(p, 1), :] = hc_s[0:1, :]


def _rg_scan(xs, h0, conv_w, conv_b, wcat, bcat, lam, tc):
    b, s, dr = xs.shape
    nchunk = s // tc
    last = nchunk - 1
    sl = tc // SUBLANES
    src = (np.arange(tc) % SUBLANES) * sl + np.arange(tc) // SUBLANES
    pm = np.zeros((tc, tc), np.float32)
    pm[np.arange(tc), src] = 1.0
    chunk_buf = pltpu.VMEM((tc, dr), F32)
    full2 = lambda shape: pl.BlockSpec(shape, lambda i, p, j: (0,) * len(shape))
    return pl.pallas_call(
        functools.partial(_rg_kernel, tc=tc, nchunk=nchunk, seq=s, dr=dr),
        grid=(b, 2, nchunk),
        in_specs=[pl.BlockSpec((1, s, dr), lambda i, p, j: (i, 0, 0)),
                  pl.BlockSpec((1, SUBLANES, dr), lambda i, p, j: (i, 0, 0)),
                  full2((CONV_W, dr)), full2((1, dr)),
                  full2((2, dr, 2 * dr)), full2((2, 1, 2 * dr)), full2((2, 1, dr)),
                  full2((tc, tc)), full2((tc, tc))],
        out_specs=[pl.BlockSpec((1, tc, dr), lambda i, p, j: (i, jnp.where(p == 0, last, last - j), 0)),
                   pl.BlockSpec((1, SUBLANES, dr), lambda i, p, j: (i, 0, 0))],
        out_shape=[jax.ShapeDtypeStruct((b, s, dr), BF16),
                   jax.ShapeDtypeStruct((b, SUBLANES, dr), F32)],
        scratch_shapes=[pltpu.VMEM((s, dr), F32), pltpu.VMEM((s, dr), F32),
                        chunk_buf, chunk_buf, chunk_buf, chunk_buf,
                        pltpu.VMEM((SUBLANES, dr), F32), pltpu.VMEM((SUBLANES, dr), F32)],
        compiler_params=_cparams(("arbitrary", "arbitrary", "arbitrary")),
        name="rg_scan",
    )(xs, h0, conv_w, conv_b.reshape(1, dr), wcat, bcat, lam, jnp.asarray(pm, BF16), jnp.asarray(pm.T, BF16))


def _f1_kernel(d_ref, x_ref, y_ref):
    y_ref[0] = jnp.dot(d_ref[...], x_ref[0], preferred_element_type=F32).astype(BF16)


def _fourier_stage1(fv, d2, tl):
    b, r, n = fv.shape
    return pl.pallas_call(
        _f1_kernel,
        grid=(b, n // tl),
        in_specs=[pl.BlockSpec((2 * r, r), lambda i, l: (0, 0)),
                  pl.BlockSpec((1, r, tl), lambda i, l: (i, 0, l))],
        out_specs=pl.BlockSpec((1, 2 * r, tl), lambda i, l: (i, 0, l)),
        out_shape=jax.ShapeDtypeStruct((b, 2 * r, n), BF16),
        compiler_params=_cparams(("parallel", "arbitrary")),
        name="fourier_stage1",
    )(d2, fv)


def _f2_kernel(y_ref, e_ref, bc_ref, bs_ref, g_ref, o_ref, obuf, *, kb, df):
    zr, zi = [], []
    for q in range(kb):
        yk = jnp.concatenate([y_ref[0, 0, q], y_ref[0, 1, q]], axis=0)
        z = jnp.dot(e_ref[q], yk, preferred_element_type=F32)
        zr.append(z[:GRID_W])
        zi.append(z[GRID_W:])
    zr = jnp.concatenate(zr, axis=0).astype(BF16)
    zi = jnp.concatenate(zi, axis=0).astype(BF16)
    o = jnp.dot(zr, bc_ref[...], preferred_element_type=F32)
    o += jnp.dot(zi, bs_ref[...], preferred_element_type=F32)
    on = _rms(o, g_ref[...])

    def store_o(pos, c, tile):
        o_ref[0, pos, :, c * LANES:(c + 1) * LANES] = tile.astype(BF16)
    _pitched_store(on, obuf, 0)
    _pitched_gather(obuf, kb, store_o)


def _fourier_stage2(y5, etab, bdc, bds, g, kb):
    b, _, r, w, df = y5.shape
    return pl.pallas_call(
        functools.partial(_f2_kernel, kb=kb, df=df),
        grid=(b, r // kb),
        in_specs=[pl.BlockSpec((1, 2, kb, w, df), lambda i, k: (i, 0, k, 0, 0)),
                  pl.BlockSpec((kb, 2 * w, 2 * w), lambda i, k: (k, 0, 0)),
                  pl.BlockSpec((df, df), lambda i, k: (0, 0)),
                  pl.BlockSpec((df, df), lambda i, k: (0, 0)),
                  pl.BlockSpec((1, df), lambda i, k: (0, 0))],
        out_specs=pl.BlockSpec((1, w, kb, df), lambda i, k: (i, 0, k, 0)),
        out_shape=jax.ShapeDtypeStruct((b, w, r, df), BF16),
        scratch_shapes=[pltpu.VMEM((df // LANES, kb * _seg_pitch(GRID_W), LANES), F32)],
        compiler_params=_cparams(("parallel", "arbitrary")),
        name="fourier_stage2",
    )(y5, etab, bdc, bds, g.reshape(1, df))


def _stage_m_kernel(fn_ref, hs_ref, gg_ref, x_ref, gtm_ref, shf_ref, scf_ref, gr_ref, gffn_ref,
                    wo_ref, wr_ref, br_ref, j_ref, x1_ref, h2_ref, idx_ref, gate_ref, *, df):
    tm = x_ref.shape[1]
    hs = hs_ref[0]
    blocks = []
    for r in range(tm // GRID_W):
        blk = hs[r * GRID_W:(r + 1) * GRID_W]
        if r % 2 == 1:
            blk = jnp.dot(j_ref[...], blk, preferred_element_type=F32)
        blocks.append(blk.astype(F32))
    rg = jnp.concatenate(blocks, axis=0) * gg_ref[0].astype(F32)
    rgn = _rms(rg, gr_ref[...]).astype(BF16)
    mix = jnp.dot(fn_ref[0], wo_ref[:df, :], preferred_element_type=F32)
    mix += jnp.dot(rgn, wo_ref[df:, :], preferred_element_type=F32)
    x1 = x_ref[0] + gtm_ref[0] * mix
    x1_ref[0] = x1
    h2 = _rms(x1, gffn_ref[...]) * (1.0 + scf_ref[0]) + shf_ref[0]
    h2_ref[0] = _pack_halves(h2)

    logits = _dot3_nt(wr_ref[...], h2) + br_ref[...]
    eidx = lax.broadcasted_iota(jnp.int32, logits.shape, 0)
    vals, idxs = [], []
    for _ in range(TOP_K):
        m = jnp.max(logits, axis=0, keepdims=True)
        sel = jnp.min(jnp.where(logits == m, eidx, N_EXPERTS), axis=0, keepdims=True)
        vals.append(m)
        idxs.append(sel)
        logits = jnp.where(eidx == sel, -jnp.inf, logits)
    ex = [jnp.exp(v - vals[0]) for v in vals]
    den = ex[0] + ex[1] + ex[2] + ex[3]
    for k in range(TOP_K):
        gate_ref[k:k + 1, :] = ex[k] / den
        idx_ref[k:k + 1, :] = idxs[k]


def _stage_m(fn, hs, gg, x, gt_m, sh_f, sc_f, g_out_r, g_ffn, w_out_bf, w_router_t, b_router, jmat, tm):
    b, s, d = x.shape
    df = fn.shape[2]
    dr = hs.shape[2]
    nt = s // tm
    ne = w_router_t.shape[0]
    vec = pl.BlockSpec((1, 1, d), lambda i, t: (i, 0, 0))
    half = lambda dd: pl.BlockSpec((1, tm, dd), lambda i, t: (i, t, 0))
    full = lambda shape: pl.BlockSpec(shape, lambda i, t: (0,) * len(shape))
    tok = pl.BlockSpec((TOP_K, tm), lambda i, t: (0, i * nt + t))
    return pl.pallas_call(
        functools.partial(_stage_m_kernel, df=df),
        grid=(b, nt),
        in_specs=[half(df), half(dr), half(dr), half(d), vec, vec, vec,
                  full((1, dr)), full((1, d)), full((d, d)), full((ne, d)), full((ne, 1)),
                  full((GRID_W, GRID_W))],
        out_specs=[half(d), half(d // 2), tok, tok],
        out_shape=[jax.ShapeDtypeStruct((b, s, d), F32), jax.ShapeDtypeStruct((b, s, d // 2), jnp.int32),
                   jax.ShapeDtypeStruct((TOP_K, b * s), jnp.int32),
                   jax.ShapeDtypeStruct((TOP_K, b * s), F32)],
        compiler_params=_cparams(("parallel", "arbitrary")),
        name="stage_m",
    )(fn, hs, gg, x, gt_m, sh_f, sc_f, g_out_r.reshape(1, dr), g_ffn.reshape(1, d), w_out_bf,
      w_router_t, b_router.reshape(ne, 1), jmat)


def _rank_kernel(idx_ref, tri_ref, rank_ref, cnt_ref, carry_s):
    c = pl.program_id(0)

    @pl.when(c == 0)
    def _():
        carry_s[...] = jnp.zeros_like(carry_s)

    l = idx_ref.shape[1]
    eidx = lax.broadcasted_iota(jnp.int32, (N_EXPERTS, l), 0)
    for k in range(TOP_K):
        onehot = eidx == idx_ref[k:k + 1, :]
        oh = jnp.where(onehot, 1.0, 0.0)
        prefix = jnp.dot(oh.astype(BF16), tri_ref[...], preferred_element_type=F32)
        carry = carry_s[:, 0:1]
        rank = jnp.sum(jnp.where(onehot, prefix - 1.0 + carry, 0.0), axis=0, keepdims=True)
        rank_ref[k:k + 1, :] = rank.astype(jnp.int32)
        carry_s[...] = carry_s[...] + jnp.sum(oh, axis=1, keepdims=True)
    cnt_ref[...] = carry_s[...].astype(jnp.int32)


def _dispatch_ranks(idx, tri, tl):
    k, t = idx.shape
    return pl.pallas_call(
        _rank_kernel,
        grid=(t // tl,),
        in_specs=[pl.BlockSpec((k, tl), lambda c: (0, c)),
                  pl.BlockSpec((tl, tl), lambda c: (0, 0))],
        out_specs=[pl.BlockSpec((k, tl), lambda c: (0, c)),
                   pl.BlockSpec((N_EXPERTS, LANES), lambda c: (0, 0))],
        out_shape=[jax.ShapeDtypeStruct((k, t), jnp.int32),
                   jax.ShapeDtypeStruct((N_EXPERTS, LANES), jnp.int32)],
        scratch_shapes=[pltpu.VMEM((N_EXPERTS, LANES), F32)],
        compiler_params=_cparams(("arbitrary",)),
        name="dispatch_ranks",
    )(idx, tri)


def _moe_kernel(be_ref, bv_ref, nu_ref, x_ref, wgu_ref, bgu_ref, wd_ref, bd_ref, o_ref, wgu_s, wd_s, *, dff):
    i = pl.program_id(0)
    h = x_ref.shape[1]

    @pl.when(jnp.logical_or(i == 0, be_ref[i] != be_ref[jnp.maximum(i - 1, 0)]))
    def _():
        wgu_s[...] = wgu_ref[0].astype(BF16)
        wd_s[...] = wd_ref[0].astype(BF16)

    @pl.when(i < nu_ref[0])
    def _():
        rows = lax.broadcasted_iota(jnp.int32, x_ref.shape, 0)
        xw = jnp.where(rows < bv_ref[i], x_ref[...], 0)
        xa, xb = _unpack_halves(xw)
        xa = xa.astype(BF16)
        xb = xb.astype(BF16)
        acc = None
        for c in range(dff // MOE_FF_CHUNK):
            gs = slice(c * MOE_FF_CHUNK, (c + 1) * MOE_FF_CHUNK)
            us = slice(dff + c * MOE_FF_CHUNK, dff + (c + 1) * MOE_FF_CHUNK)
            g = jnp.dot(xa, wgu_s[:h, gs], preferred_element_type=F32)
            g += jnp.dot(xb, wgu_s[h:, gs], preferred_element_type=F32)
            u = jnp.dot(xa, wgu_s[:h, us], preferred_element_type=F32)
            u += jnp.dot(xb, wgu_s[h:, us], preferred_element_type=F32)
            gt = jnp.minimum(g + bgu_ref[0, :, gs], SWIGLU_LIMIT)
            up = jnp.clip(u + bgu_ref[0, :, us], -SWIGLU_LIMIT, SWIGLU_LIMIT)
            act = (up + 1.0) * (gt * _sigmoid(SWIGLU_ALPHA * gt))
            part = jnp.dot(act.astype(BF16), wd_s[gs, :], preferred_element_type=F32)
            acc = part if acc is None else acc + part
        o_ref[...] = _pack_halves(acc + bd_ref[0])


def _moe_experts(blk_expert, blk_valid, n_used, xs, wgu, bgu, wd, bd, tmm):
    cap, h = xs.shape
    ne, d, dff2 = wgu.shape
    dff = dff2 // 2
    row_blk = lambda i, be, bv, nu: (jnp.maximum(jnp.minimum(i, nu[0] - 1), 0), 0)
    wsel = lambda i, be, bv, nu: (be[i], 0, 0)
    grid_spec = pltpu.PrefetchScalarGridSpec(
        num_scalar_prefetch=3,
        grid=(cap // tmm,),
        in_specs=[pl.BlockSpec((tmm, h), row_blk),
                  pl.BlockSpec((1, d, dff2), wsel),
                  pl.BlockSpec((1, 1, dff2), wsel),
                  pl.BlockSpec((1, dff, d), wsel),
                  pl.BlockSpec((1, 1, d), wsel)],
        out_specs=pl.BlockSpec((tmm, h), row_blk),
        scratch_shapes=[pltpu.VMEM((d, dff2), BF16), pltpu.VMEM((dff, d), BF16)],
    )
    return pl.pallas_call(
        functools.partial(_moe_kernel, dff=dff),
        grid_spec=grid_spec,
        out_shape=jax.ShapeDtypeStruct((cap, h), jnp.int32),
        compiler_params=_cparams(("arbitrary",)),
        name="moe_experts",
    )(blk_expert, blk_valid, n_used, xs, wgu, bgu.reshape(ne, 1, dff2), wd, bd.reshape(ne, 1, d))


SC_CHUNK = 64


def _sc_workers():
    info = plsc.get_sparse_core_info()
    return info.num_cores, info.num_subcores


def _sc_scatter(rows, dest, out_rows=None, out_ref=None):
    n, w = rows.shape
    nc, ns = _sc_workers()
    per_w = n // (nc * ns)
    assert per_w % SC_CHUNK == 0
    mesh = plsc.VectorSubcoreMesh(core_axis_name="c", subcore_axis_name="s")
    out_type = () if out_ref is not None else jax.ShapeDtypeStruct((out_rows, w), jnp.int32)

    @functools.partial(
        pl.kernel, mesh=mesh, out_type=out_type,
        scratch_types=[pltpu.VMEM((SC_CHUNK,), jnp.int32),
                       pltpu.VMEM((SC_CHUNK, w), jnp.int32),
                       pltpu.SemaphoreType.DMA],
    )
    def scatter_rows(rows_hbm, dest_hbm, out_hbm, idx_v, rows_v, sem):
        base = (lax.axis_index("s") * nc + lax.axis_index("c")) * per_w

        @pl.loop(0, per_w // SC_CHUNK)
        def _(j):
            off = pl.multiple_of(base + j * SC_CHUNK, SC_CHUNK)
            pltpu.sync_copy(rows_hbm.at[pl.ds(off, SC_CHUNK)], rows_v)
            pltpu.sync_copy(dest_hbm.at[pl.ds(off, SC_CHUNK)], idx_v)
            pltpu.async_copy(rows_v, out_hbm.at[idx_v], sem).wait()

    if out_ref is not None:
        scatter_rows(rows, dest, out_ref)
        return None
    return scatter_rows(rows, dest)


def _sc_scatter_ids(dest, out_rows):
    n = dest.shape[0]
    info = plsc.get_sparse_core_info()
    nc, ns, nl = info.num_cores, info.num_subcores, info.num_lanes
    per_w = n // (nc * ns)
    assert per_w % SC_CHUNK == 0
    mesh = plsc.VectorSubcoreMesh(core_axis_name="c", subcore_axis_name="s")

    @functools.partial(
        pl.kernel, mesh=mesh,
        out_type=jax.ShapeDtypeStruct((out_rows, LANES), jnp.int32),
        scratch_types=[pltpu.VMEM((SC_CHUNK,), jnp.int32),
                       pltpu.VMEM((SC_CHUNK, LANES), jnp.int32),
                       pltpu.SemaphoreType.DMA],
    )
    def scatter_ids(dest_hbm, out_hbm, idx_v, rows_v, sem):
        base = (lax.axis_index("s") * nc + lax.axis_index("c")) * per_w

        @pl.loop(0, per_w // SC_CHUNK)
        def _(j):
            off = pl.multiple_of(base + j * SC_CHUNK, SC_CHUNK)
            for r in range(SC_CHUNK):
                rows_v[r, pl.ds(0, nl)] = jnp.zeros((nl,), jnp.int32) + (off + r)
            pltpu.sync_copy(dest_hbm.at[pl.ds(off, SC_CHUNK)], idx_v)
            pltpu.async_copy(rows_v, out_hbm.at[idx_v], sem).wait()

    return scatter_ids(dest)


def _sc_gather(table, idx):
    n = idx.shape[0]
    w = table.shape[1]
    nc, ns = _sc_workers()
    per_w = n // (nc * ns)
    assert per_w % SC_CHUNK == 0
    mesh = plsc.VectorSubcoreMesh(core_axis_name="c", subcore_axis_name="s")

    @functools.partial(
        pl.kernel, mesh=mesh,
        out_type=jax.ShapeDtypeStruct((n, w), jnp.int32),
        scratch_types=[pltpu.VMEM((SC_CHUNK,), jnp.int32),
                       pltpu.VMEM((SC_CHUNK, w), jnp.int32),
                       pltpu.SemaphoreType.DMA],
    )
    def gather_rows(table_hbm, idx_hbm, out_hbm, idx_v, rows_v, sem):
        base = (lax.axis_index("s") * nc + lax.axis_index("c")) * per_w

        @pl.loop(0, per_w // SC_CHUNK)
        def _(j):
            off = pl.multiple_of(base + j * SC_CHUNK, SC_CHUNK)
            pltpu.sync_copy(idx_hbm.at[pl.ds(off, SC_CHUNK)], idx_v)
            pltpu.async_copy(table_hbm.at[idx_v], rows_v, sem).wait()
            pltpu.sync_copy(rows_v, out_hbm.at[pl.ds(off, SC_CHUNK)])

    return gather_rows(table, idx)


def _combine_kernel(x1_ref, y_ref, gate_ref, gtf_ref, g_ref, o_ref):
    h = y_ref.shape[3]
    gates = gate_ref[0]
    moe_a = moe_b = None
    for k in range(TOP_K):
        ya, yb = _unpack_halves(y_ref[k, 0])
        gk = gates[:, k:k + 1]
        moe_a = gk * ya if k == 0 else moe_a + gk * ya
        moe_b = gk * yb if k == 0 else moe_b + gk * yb
    za = x1_ref[0, :, :h] + gtf_ref[0, :, :h] * moe_a
    zb = x1_ref[0, :, h:] + gtf_ref[0, :, h:] * moe_b
    ms = (jnp.sum(za * za, axis=-1, keepdims=True) + jnp.sum(zb * zb, axis=-1, keepdims=True)) / (2 * h)
    inv = lax.rsqrt(ms + EPS)
    o_ref[0, :, :h] = za * inv * g_ref[:, :h]
    o_ref[0, :, h:] = zb * inv * g_ref[:, h:]


def _combine(x1, yk, gates_t, gt_f, g_final, tm):
    b, s, d = x1.shape
    h = yk.shape[3]
    return pl.pallas_call(
        _combine_kernel,
        grid=(b, s // tm),
        in_specs=[pl.BlockSpec((1, tm, d), lambda i, t: (i, t, 0)),
                  pl.BlockSpec((TOP_K, 1, tm, h), lambda i, t: (0, i, t, 0)),
                  pl.BlockSpec((1, tm, TOP_K), lambda i, t: (i, t, 0)),
                  pl.BlockSpec((1, 1, d), lambda i, t: (i, 0, 0)),
                  pl.BlockSpec((1, d), lambda i, t: (0, 0))],
        out_specs=pl.BlockSpec((1, tm, d), lambda i, t: (i, t, 0)),
        out_shape=jax.ShapeDtypeStruct((b, s, d), F32),
        compiler_params=_cparams(("parallel", "arbitrary")),
        name="combine",
    )(x1, yk, gates_t, gt_f, g_final.reshape(1, d))


def _dft_tables(rows, gd):
    seq = rows * GRID_W
    n = np.arange(rows)
    ang1 = 2.0 * np.pi * np.outer(n, n) / rows
    d2 = np.concatenate([np.cos(ang1), -np.sin(ang1)], axis=0)
    k1 = np.arange(rows)[:, None, None]
    k2 = np.arange(GRID_W)[None, :, None]
    n2 = np.arange(GRID_W)[None, None, :]
    ang2 = 2.0 * np.pi * ((n2 * (k1 + rows * k2)) % seq) / seq
    ec, es = np.cos(ang2), np.sin(ang2)
    etab = np.concatenate([np.concatenate([ec, es], axis=2),
                           np.concatenate([-es, ec], axis=2)], axis=1)
    c = np.arange(gd)
    angc = 2.0 * np.pi * np.outer(c, c) / gd
    scale = 1.0 / np.sqrt(seq * gd)
    return (jnp.asarray(d2, BF16), jnp.asarray(etab, BF16),
            jnp.asarray(np.cos(angc) * scale, F32), jnp.asarray(np.sin(angc) * scale, F32))


def _block_diag(w):
    h, i, o = w.shape
    eye = jnp.eye(h, dtype=w.dtype)
    return (eye[:, None, :, None] * w[:, :, None, :]).reshape(h * i, h * o)


def kernel(x, c, ctx, c_ctx, w_mod, b_mod, g_norm_mix, g_norm_ffn, w_in, w_fourier, conv_w, conv_b,
           rg_w_a, rg_b_a, rg_w_x, rg_b_x, rg_lam, g_out_fourier, g_out_rg, w_out, w_router, b_router,
           w_gate_up, b_gate_up, w_down, b_down, g_final):
    assert w_mod.shape[0] == 1, "single-layer stack only"
    b, s, d = x.shape
    df = w_fourier.shape[1] * w_fourier.shape[2]
    dr = conv_w.shape[2]
    gd = w_fourier.shape[2]
    rows = s // GRID_W
    t = b * s
    ne = w_router.shape[2]

    mrows = -(-(b + 1) // SUBLANES) * SUBLANES
    cond = jnp.zeros((mrows, d), F32).at[:b].set(c).at[b].set(c_ctx)
    mod = _adaln(cond, w_mod[0], b_mod[0])
    sh_m, sc_m, gt_m, sh_f, sc_f, gt_f = [mod[:b, k * d:(k + 1) * d].reshape(b, 1, d) for k in range(N_MOD)]
    csh_m = mod[b:b + 1, 0:d].reshape(1, 1, d)
    csc_m = mod[b:b + 1, d:2 * d].reshape(1, 1, d)

    tm = min(s, TOKEN_TILE)
    d2, etab, cmat, smat = _dft_tables(rows, gd)
    jmat = jnp.asarray(np.eye(GRID_W)[::-1].copy(), BF16)

    w_in_bf = w_in[0].astype(BF16)
    f, xs, gg = _stage_b(x, sh_m, sc_m, g_norm_mix[0], w_in_bf, jmat, df, dr, tm=tm)
    xr_ctx = _stage_b_ctx(ctx, csh_m, csc_m, g_norm_mix[0], w_in_bf[:, df:df + dr])

    wcat = jnp.stack([jnp.concatenate([_block_diag(rg_w_a[0, dd]), _block_diag(rg_w_x[0, dd])], axis=1)
                      for dd in range(2)]).astype(BF16)
    bcat = jnp.concatenate([rg_b_a[0], rg_b_x[0]], axis=1).reshape(2, 1, 2 * dr)
    lam = rg_lam[0].reshape(2, 1, dr)
    h0 = jnp.zeros((b, SUBLANES, dr), F32)
    _, hfin_ctx = _rg_scan(xr_ctx, h0, conv_w[0], conv_b[0], wcat, bcat, lam, tc=ctx.shape[1])
    hs, _ = _rg_scan(xs, hfin_ctx, conv_w[0], conv_b[0], wcat, bcat, lam, tc=min(s, SCAN_CHUNK))

    cw, sw = _fold_fourier(cmat, smat, w_fourier[0])
    bdc = _block_diag(cw).astype(BF16)
    bds = _block_diag(sw).astype(BF16)
    y = _fourier_stage1(f, d2, tl=min(GRID_W * df, 8192))
    fn = _fourier_stage2(y.reshape(b, 2, rows, GRID_W, df), etab, bdc, bds, g_out_fourier[0],
                         kb=min(rows, 16))
    fn = fn.reshape(b, s, df)

    x1, h2, idx, gates = _stage_m(fn, hs, gg, x, gt_m, sh_f, sc_f, g_out_rg[0], g_norm_ffn[0],
                                  w_out[0].astype(BF16), w_router[0].T, b_router[0], jmat, tm=tm)

    tl = min(t, RANK_TILE)
    tri = jnp.asarray(np.triu(np.ones((tl, tl))), BF16)
    rank, cnt = _dispatch_ranks(idx, tri, tl)
    counts = cnt[:, 0]
    tmm = MOE_ROW_TILE
    padded = (counts + tmm - 1) // tmm * tmm
    pad_end = jnp.cumsum(padded)
    pad_start = pad_end - padded
    eids = jnp.arange(ne, dtype=jnp.int32)
    dest = rank + jnp.sum(jnp.where(idx[:, :, None] == eids, pad_start, 0), axis=-1)
    n_blocks = -(-(t * TOP_K) // tmm) + ne
    cap = n_blocks * tmm
    n_used = (pad_end[-1] // tmm).astype(jnp.int32).reshape(1)
    blk_start = jnp.arange(n_blocks, dtype=jnp.int32) * tmm
    blk_expert = jnp.sum(blk_start[:, None] >= pad_end[None, :], axis=1).astype(jnp.int32)
    last_expert = jnp.sum(pad_end[-1] - tmm >= pad_end).astype(jnp.int32)
    blk_expert = jnp.minimum(blk_expert, last_expert)
    sel = blk_expert[:, None] == eids
    blk_first = jnp.sum(jnp.where(sel, pad_start, 0), axis=1)
    blk_count = jnp.sum(jnp.where(sel, counts, 0), axis=1)
    blk_valid = jnp.clip(blk_count - (blk_start - blk_first), 0, tmm).astype(jnp.int32)

    na = TOP_K * t
    inv = _sc_scatter_ids(dest.reshape(-1), cap)[:, 0].reshape(n_blocks, tmm)
    live = jnp.arange(tmm, dtype=jnp.int32)[None, :] < blk_valid[:, None]
    spread = jnp.arange(cap, dtype=jnp.int32).reshape(n_blocks, tmm) % t
    src_tok = jnp.where(live, inv % t, spread).reshape(-1)
    dst_row = jnp.where(live, inv, na + spread).reshape(-1)

    h2_rows = h2.reshape(t, d // 2)
    y_all = jax.empty_ref(jax.ShapeDtypeStruct(((TOP_K + 1) * t, d // 2), jnp.int32))
    unit = n_blocks // sum(MOE_PIPE)
    assert unit * sum(MOE_PIPE) == n_blocks
    blk0 = 0
    for parts in MOE_PIPE:
        nq = parts * unit
        blocks = slice(blk0, blk0 + nq)
        rows = slice(blk0 * tmm, (blk0 + nq) * tmm)
        x_q = _sc_gather(h2_rows, src_tok[rows])
        nu_q = jnp.clip(n_used - blk0, 0, nq).astype(jnp.int32)
        blk0 += nq
        y_q = _moe_experts(blk_expert[blocks], blk_valid[blocks], nu_q, x_q,
                           w_gate_up[0], b_gate_up[0], w_down[0], b_down[0], tmm)
        _sc_scatter(y_q, dst_row[rows], out_ref=y_all)
    yk = y_all[...].reshape(TOP_K + 1, b, s, d // 2)
    return _combine(x1, yk, gates.T.reshape(b, s, TOP_K), gt_f, g_final, tm)
```

```python
import functools

import numpy as np
import jax
import jax.numpy as jnp
from jax import lax
from jax.experimental import pallas as pl
from jax.experimental.pallas import tpu as pltpu
from jax.experimental.pallas import tpu_sc as plsc

GRID_W = 64
FOURIER_GROUPS = 4
RG_HEADS = 8
CONV_W = 4
CONV_PAD_LO = 2
RG_C = 8.0
N_EXPERTS = 32
TOP_K = 4
SWIGLU_LIMIT = 7.0
SWIGLU_ALPHA = 1.702
N_MOD = 6
EPS = 1e-6

LANES = 128
SUBLANES = 8
VMEM_LIMIT_BYTES = 56 * 1024 * 1024
TOKEN_TILE = 1024
RANK_TILE = 512
SCAN_CHUNK = 512
MOE_ROW_TILE = 512
MOE_FF_CHUNK = 512
MOE_PIPE = (1, 3, 3, 1)

F32 = jnp.float32
BF16 = jnp.bfloat16


def _cparams(sem):
    return pltpu.CompilerParams(dimension_semantics=sem, vmem_limit_bytes=VMEM_LIMIT_BYTES)


def _split_bf16(a):
    hi = a.astype(BF16)
    lo = (a - hi.astype(F32)).astype(BF16)
    return hi, lo


def _dot3(a, b):
    ah, al = _split_bf16(a)
    bh, bl = _split_bf16(b)
    out = jnp.dot(ah, bh, preferred_element_type=F32)
    out += jnp.dot(ah, bl, preferred_element_type=F32)
    out += jnp.dot(al, bh, preferred_element_type=F32)
    return out


def _dot3_nt(a, b):
    dn = (((1,), (1,)), ((), ()))
    ah, al = _split_bf16(a)
    bh, bl = _split_bf16(b)
    out = lax.dot_general(ah, bh, dn, preferred_element_type=F32)
    out += lax.dot_general(ah, bl, dn, preferred_element_type=F32)
    out += lax.dot_general(al, bh, dn, preferred_element_type=F32)
    return out


def _gelu_tanh(x):
    return 0.5 * x * (1.0 + jnp.tanh(0.7978845608028654 * (x + 0.044715 * (x * x * x))))


def _rms(x, g):
    return x * lax.rsqrt(jnp.mean(x * x, axis=-1, keepdims=True) + EPS) * g


def _pack_halves(v):
    h = v.shape[1] // 2
    hi = lax.bitcast_convert_type(v[:, :h].astype(BF16).astype(F32), jnp.uint32)
    lo = lax.bitcast_convert_type(v[:, h:].astype(BF16).astype(F32), jnp.uint32)
    return lax.bitcast_convert_type(hi | (lo >> 16), jnp.int32)


def _unpack_halves(w):
    u = lax.bitcast_convert_type(w, jnp.uint32)
    hi = lax.bitcast_convert_type(u & jnp.uint32(0xFFFF0000), F32)
    lo = lax.bitcast_convert_type(u << 16, F32)
    return hi, lo


def _adaln_kernel(c_ref, w_ref, b_ref, o_ref):
    s = c_ref[...]
    s = s * jax.nn.sigmoid(s)
    o_ref[...] = _dot3(s, w_ref[...]) + b_ref[...]


def _adaln(cond, w_mod, b_mod):
    m, d = cond.shape
    n = w_mod.shape[1]
    tn = n // N_MOD
    return pl.pallas_call(
        _adaln_kernel,
        grid=(n // tn,),
        in_specs=[pl.BlockSpec((m, d), lambda i: (0, 0)),
                  pl.BlockSpec((d, tn), lambda i: (0, i)),
                  pl.BlockSpec((1, tn), lambda i: (0, i))],
        out_specs=pl.BlockSpec((m, tn), lambda i: (0, i)),
        out_shape=jax.ShapeDtypeStruct((m, n), F32),
        compiler_params=_cparams(("arbitrary",)),
        name="adaln",
    )(cond, w_mod, b_mod.reshape(1, n))


def _fold_kernel(c_ref, s_ref, w_ref, cw_ref, sw_ref):
    w = w_ref[0]
    cw_ref[0] = _dot3(c_ref[...], w)
    sw_ref[0] = _dot3(s_ref[...], w)


def _fold_fourier(cmat, smat, w_f):
    g, gd, _ = w_f.shape
    spec_m = pl.BlockSpec((gd, gd), lambda i: (0, 0))
    spec_w = pl.BlockSpec((1, gd, gd), lambda i: (i, 0, 0))
    return pl.pallas_call(
        _fold_kernel,
        grid=(g,),
        in_specs=[spec_m, spec_m, spec_w],
        out_specs=[spec_w, spec_w],
        out_shape=[jax.ShapeDtypeStruct((g, gd, gd), F32)] * 2,
        compiler_params=_cparams(("arbitrary",)),
        name="fold_fourier",
    )(cmat, smat, w_f)


def _seg_pitch(seg_len):
    n8 = seg_len // SUBLANES
    return SUBLANES * (n8 + 1 - n8 % 2)


def _pitched_store(v, buf, blk0):
    pitch = _seg_pitch(GRID_W)
    for r in range(v.shape[0] // GRID_W):
        for c in range(v.shape[1] // LANES):
            buf[c, (blk0 + r) * pitch:(blk0 + r) * pitch + GRID_W, :] = (
                v[r * GRID_W:(r + 1) * GRID_W, c * LANES:(c + 1) * LANES])


def _pitched_gather(buf, nb, store):
    pitch = _seg_pitch(GRID_W)
    for pos in range(GRID_W):
        for c in range(buf.shape[0]):
            store(pos, c, buf[c, pl.ds(pos, nb, stride=pitch), :])


def _stage_b_kernel(x_ref, sh_ref, sc_ref, g_ref, w_ref, j_ref, f_ref, xs_ref, gg_ref, fbuf, *, df, dr):
    tm = x_ref.shape[1]
    h = _rms(x_ref[0], g_ref[...]) * (1.0 + sc_ref[0]) + sh_ref[0]
    hb = h.astype(BF16)
    _pitched_store(jnp.dot(hb, w_ref[:, :df], preferred_element_type=F32), fbuf, 0)

    def store_f(pos, c, tile):
        f_ref[0, :, pos * df + c * LANES:pos * df + (c + 1) * LANES] = tile.astype(BF16)
    _pitched_gather(fbuf, tm // GRID_W, store_f)
    xr = jnp.dot(hb, w_ref[:, df:df + dr], preferred_element_type=F32).astype(BF16)
    for r in range(tm // GRID_W):
        blk = xr[r * GRID_W:(r + 1) * GRID_W]
        if r % 2 == 1:
            blk = jnp.dot(j_ref[...], blk, preferred_element_type=F32).astype(BF16)
        xs_ref[0, r * GRID_W:(r + 1) * GRID_W, :] = blk
    gr = jnp.dot(hb, w_ref[:, df + dr:], preferred_element_type=F32)
    gg_ref[0] = _gelu_tanh(gr).astype(BF16)


def _stage_b(x, shift, scale, g, w_in_bf, jmat, df, dr, tm):
    b, s, d = x.shape
    n = w_in_bf.shape[1]
    vec = pl.BlockSpec((1, 1, d), lambda i, t: (i, 0, 0))
    out = pl.BlockSpec((1, tm, df), lambda i, t: (i, t, 0))
    nb = tm // GRID_W
    tok = jax.ShapeDtypeStruct((b, s, df), BF16)
    return pl.pallas_call(
        functools.partial(_stage_b_kernel, df=df, dr=dr),
        grid=(b, s // tm),
        in_specs=[pl.BlockSpec((1, tm, d), lambda i, t: (i, t, 0)), vec, vec,
                  pl.BlockSpec((1, d), lambda i, t: (0, 0)),
                  pl.BlockSpec((d, n), lambda i, t: (0, 0)),
                  pl.BlockSpec((GRID_W, GRID_W), lambda i, t: (0, 0))],
        out_specs=[pl.BlockSpec((1, nb, GRID_W * df), lambda i, t: (i, t, 0)), out, out],
        out_shape=[jax.ShapeDtypeStruct((b, s // GRID_W, GRID_W * df), BF16), tok, tok],
        scratch_shapes=[pltpu.VMEM((df // LANES, nb * _seg_pitch(GRID_W), LANES), F32)],
        compiler_params=_cparams(("parallel", "arbitrary")),
        name="stage_b",
    )(x, shift, scale, g.reshape(1, d), w_in_bf, jmat)


def _stage_b_ctx_kernel(x_ref, sh_ref, sc_ref, g_ref, w_ref, xr_ref):
    h = _rms(x_ref[0], g_ref[...]) * (1.0 + sc_ref[0]) + sh_ref[0]
    xr_ref[0] = jnp.dot(h.astype(BF16), w_ref[...], preferred_element_type=F32).astype(BF16)


def _stage_b_ctx(ctx, shift, scale, g, w_xr_bf):
    b, s, d = ctx.shape
    dr = w_xr_bf.shape[1]
    vec = pl.BlockSpec((1, 1, d), lambda i: (0, 0, 0))
    return pl.pallas_call(
        _stage_b_ctx_kernel,
        grid=(b,),
        in_specs=[pl.BlockSpec((1, s, d), lambda i: (i, 0, 0)), vec, vec,
                  pl.BlockSpec((1, d), lambda i: (0, 0)),
                  pl.BlockSpec((d, dr), lambda i: (0, 0))],
        out_specs=pl.BlockSpec((1, s, dr), lambda i: (i, 0, 0)),
        out_shape=jax.ShapeDtypeStruct((b, s, dr), BF16),
        compiler_params=_cparams(("arbitrary",)),
        name="stage_b_ctx",
    )(ctx, shift, scale, g.reshape(1, d), w_xr_bf)


HALO = 16


def _sigmoid(x):
    return 0.5 * jnp.tanh(0.5 * x) + 0.5


def _rg_kernel(xs_ref, h0_ref, cw_ref, cb_ref, w_ref, b_ref, lam_ref, pm_ref, pmt_ref, out_ref, hfin_ref,
               hf_s, xc_s, a_s, u_s, hl_s, p_s, c_s, hc_s, *, tc, nchunk, seq, dr):
    p = pl.program_id(1)
    j = pl.program_id(2)
    cidx = jnp.where(p == 0, j, nchunk - 1 - j)
    start = pl.multiple_of(cidx * tc, tc)
    nseg = SUBLANES
    sl = tc // nseg
    sub = lax.broadcasted_iota(jnp.int32, (nseg, dr), 0)

    @pl.when(p == 0)
    def _():
        xp = jnp.dot(pm_ref[...], xs_ref[0, pl.ds(start, tc), :], preferred_element_type=F32)
        pstart = pl.multiple_of(jnp.maximum(start - HALO, 0), HALO)
        nstart = pl.multiple_of(jnp.minimum(start + tc, seq - HALO), HALO)
        prev = xs_ref[0, pl.ds(pstart, HALO), :].astype(F32)
        nxt = xs_ref[0, pl.ds(nstart, HALO), :].astype(F32)
        prev = jnp.where(cidx > 0, prev, 0.0)
        nxt = jnp.where(cidx < nchunk - 1, nxt, 0.0)
        tm2 = jnp.where(sub == 0, prev[HALO - 2:HALO - 1], pltpu.roll(xp[(sl - 2) * nseg:(sl - 1) * nseg], 1, 0))
        tm1 = jnp.where(sub == 0, prev[HALO - 1:HALO], pltpu.roll(xp[(sl - 1) * nseg:sl * nseg], 1, 0))
        tp1 = jnp.where(sub == nseg - 1, nxt[0:1], pltpu.roll(xp[0:nseg], nseg - 1, 0))
        ext = jnp.concatenate([tm2, tm1, xp, tp1], axis=0)
        xc = cb_ref[...] + cw_ref[0:1, :] * ext[0:tc]
        for k in range(1, CONV_W):
            xc = xc + cw_ref[k:k + 1, :] * ext[k * nseg:k * nseg + tc]
        xc_s[pl.ds(start, tc), :] = xc

    xc = xc_s[pl.ds(start, tc), :]
    gates = jnp.dot(xc.astype(BF16), w_ref[p], preferred_element_type=F32) + b_ref[p]
    i = _sigmoid(gates[:, dr:])
    half_c = (-0.5 * RG_C) * jax.nn.softplus(-lam_ref[p])
    log_a = half_c * jnp.tanh(0.5 * gates[:, :dr]) + half_c
    a = jnp.exp(log_a)
    a_s[...] = a
    w = -jnp.tanh(log_a) * (1.0 + a * a)
    u_s[...] = jnp.where(w > 0.0, w * lax.rsqrt(w), 0.0) * (i * xc)

    @pl.when(jnp.logical_and(p == 0, j == 0))
    def _():
        hfin_ref[...] = jnp.zeros_like(hfin_ref)

    @pl.when(j == 0)
    def _():
        hc_s[0:1, :] = h0_ref[0, pl.ds(p, 1), :]

    def segment_scan(reverse):
        def body(q, carry):
            t = (sl - 1 - q) if reverse else q
            rows = pl.ds(pl.multiple_of(t * nseg, nseg), nseg)
            h, pr = carry
            av = a_s[rows, :]
            h = av * h + u_s[rows, :]
            pr = av * pr
            hl_s[rows, :] = h
            p_s[rows, :] = pr
            return h, pr
        h_end, p_end = lax.fori_loop(0, sl, body, (jnp.zeros((nseg, dr), F32), jnp.ones((nseg, dr), F32)),
                                     unroll=4)
        carry = hc_s[0:1, :]
        for g in (range(nseg - 1, -1, -1) if reverse else range(nseg)):
            c_s[g:g + 1, :] = carry
            carry = h_end[g:g + 1, :] + p_end[g:g + 1, :] * carry
        hc_s[0:1, :] = carry
        return c_s[...]

    def corrected(cin):
        h = hl_s[...].reshape(sl, nseg, dr) + p_s[...].reshape(sl, nseg, dr) * cin[None]
        return h.reshape(tc, dr)

    @pl.when(p == 0)
    def _():
        hf_s[pl.ds(start, tc), :] = corrected(segment_scan(False))

    @pl.when(p == 1)
    def _():
        tot = corrected(segment_scan(True)) + hf_s[pl.ds(start, tc), :]
        out_ref[0] = jnp.dot(pmt_ref[...], tot.astype(BF16), preferred_element_type=F32).astype(BF16)

    @pl.when(j == nchunk - 1)
    def _():
        hfin_ref[0, pl.ds(p, 1), :] = hc_s[0:1, :]


def _rg_scan(xs, h0, conv_w, conv_b, wcat, bcat, lam, tc):
    b, s, dr = xs.shape
    nchunk = s // tc
    last = nchunk - 1
    sl = tc // SUBLANES
    src = (np.arange(tc) % SUBLANES) * sl + np.arange(tc) // SUBLANES
    pm = np.zeros((tc, tc), np.float32)
    pm[np.arange(tc), src] = 1.0
    chunk_buf = pltpu.VMEM((tc, dr), F32)
    full2 = lambda shape: pl.BlockSpec(shape, lambda i, p, j: (0,) * len(shape))
    return pl.pallas_call(
        functools.partial(_rg_kernel, tc=tc, nchunk=nchunk, seq=s, dr=dr),
        grid=(b, 2, nchunk),
        in_specs=[pl.BlockSpec((1, s, dr), lambda i, p, j: (i, 0, 0)),
                  pl.BlockSpec((1, SUBLANES, dr), lambda i, p, j: (i, 0, 0)),
                  full2((CONV_W, dr)), full2((1, dr)),
                  full2((2, dr, 2 * dr)), full2((2, 1, 2 * dr)), full2((2, 1, dr)),
                  full2((tc, tc)), full2((tc, tc))],
        out_specs=[pl.BlockSpec((1, tc, dr), lambda i, p, j: (i, jnp.where(p == 0, last, last - j), 0)),
                   pl.BlockSpec((1, SUBLANES, dr), lambda i, p, j: (i, 0, 0))],
        out_shape=[jax.ShapeDtypeStruct((b, s, dr), BF16),
                   jax.ShapeDtypeStruct((b, SUBLANES, dr), F32)],
        scratch_shapes=[pltpu.VMEM((s, dr), F32), pltpu.VMEM((s, dr), F32),
                        chunk_buf, chunk_buf, chunk_buf, chunk_buf,
                        pltpu.VMEM((SUBLANES, dr), F32), pltpu.VMEM((SUBLANES, dr), F32)],
        compiler_params=_cparams(("arbitrary", "arbitrary", "arbitrary")),
        name="rg_scan",
    )(xs, h0, conv_w, conv_b.reshape(1, dr), wcat, bcat, lam, jnp.asarray(pm, BF16), jnp.asarray(pm.T, BF16))


def _f1_kernel(d_ref, x_ref, y_ref):
    y_ref[0] = jnp.dot(d_ref[...], x_ref[0], preferred_element_type=F32).astype(BF16)


def _fourier_stage1(fv, d2, tl):
    b, r, n = fv.shape
    return pl.pallas_call(
        _f1_kernel,
        grid=(b, n // tl),
        in_specs=[pl.BlockSpec((2 * r, r), lambda i, l: (0, 0)),
                  pl.BlockSpec((1, r, tl), lambda i, l: (i, 0, l))],
        out_specs=pl.BlockSpec((1, 2 * r, tl), lambda i, l: (i, 0, l)),
        out_shape=jax.ShapeDtypeStruct((b, 2 * r, n), BF16),
        compiler_params=_cparams(("parallel", "arbitrary")),
        name="fourier_stage1",
    )(d2, fv)


def _f2_kernel(y_ref, e_ref, bc_ref, bs_ref, g_ref, o_ref, obuf, *, kb, df):
    zr, zi = [], []
    for q in range(kb):
        yk = jnp.concatenate([y_ref[0, 0, q], y_ref[0, 1, q]], axis=0)
        z = jnp.dot(e_ref[q], yk, preferred_element_type=F32)
        zr.append(z[:GRID_W])
        zi.append(z[GRID_W:])
    zr = jnp.concatenate(zr, axis=0).astype(BF16)
    zi = jnp.concatenate(zi, axis=0).astype(BF16)
    o = jnp.dot(zr, bc_ref[...], preferred_element_type=F32)
    o += jnp.dot(zi, bs_ref[...], preferred_element_type=F32)
    on = _rms(o, g_ref[...])

    def store_o(pos, c, tile):
        o_ref[0, pos, :, c * LANES:(c + 1) * LANES] = tile.astype(BF16)
    _pitched_store(on, obuf, 0)
    _pitched_gather(obuf, kb, store_o)


def _fourier_stage2(y5, etab, bdc, bds, g, kb):
    b, _, r, w, df = y5.shape
    return pl.pallas_call(
        functools.partial(_f2_kernel, kb=kb, df=df),
        grid=(b, r // kb),
        in_specs=[pl.BlockSpec((1, 2, kb, w, df), lambda i, k: (i, 0, k, 0, 0)),
                  pl.BlockSpec((kb, 2 * w, 2 * w), lambda i, k: (k, 0, 0)),
                  pl.BlockSpec((df, df), lambda i, k: (0, 0)),
                  pl.BlockSpec((df, df), lambda i, k: (0, 0)),
                  pl.BlockSpec((1, df), lambda i, k: (0, 0))],
        out_specs=pl.BlockSpec((1, w, kb, df), lambda i, k: (i, 0, k, 0)),
        out_shape=jax.ShapeDtypeStruct((b, w, r, df), BF16),
        scratch_shapes=[pltpu.VMEM((df // LANES, kb * _seg_pitch(GRID_W), LANES), F32)],
        compiler_params=_cparams(("parallel", "arbitrary")),
        name="fourier_stage2",
    )(y5, etab, bdc, bds, g.reshape(1, df))


def _stage_m_kernel(fn_ref, hs_ref, gg_ref, x_ref, gtm_ref, shf_ref, scf_ref, gr_ref, gffn_ref,
                    wo_ref, wr_ref, br_ref, j_ref, x1_ref, h2_ref, idx_ref, gate_ref, *, df):
    tm = x_ref.shape[1]
    hs = hs_ref[0]
    blocks = []
    for r in range(tm // GRID_W):
        blk = hs[r * GRID_W:(r + 1) * GRID_W]
        if r % 2 == 1:
            blk = jnp.dot(j_ref[...], blk, preferred_element_type=F32)
        blocks.append(blk.astype(F32))
    rg = jnp.concatenate(blocks, axis=0) * gg_ref[0].astype(F32)
    rgn = _rms(rg, gr_ref[...]).astype(BF16)
    mix = jnp.dot(fn_ref[0], wo_ref[:df, :], preferred_element_type=F32)
    mix += jnp.dot(rgn, wo_ref[df:, :], preferred_element_type=F32)
    x1 = x_ref[0] + gtm_ref[0] * mix
    x1_ref[0] = x1
    h2 = _rms(x1, gffn_ref[...]) * (1.0 + scf_ref[0]) + shf_ref[0]
    h2_ref[0] = _pack_halves(h2)

    logits = _dot3_nt(wr_ref[...], h2) + br_ref[...]
    eidx = lax.broadcasted_iota(jnp.int32, logits.shape, 0)
    vals, idxs = [], []
    for _ in range(TOP_K):
        m = jnp.max(logits, axis=0, keepdims=True)
        sel = jnp.min(jnp.where(logits == m, eidx, N_EXPERTS), axis=0, keepdims=True)
        vals.append(m)
        idxs.append(sel)
        logits = jnp.where(eidx == sel, -jnp.inf, logits)
    ex = [jnp.exp(v - vals[0]) for v in vals]
    den = ex[0] + ex[1] + ex[2] + ex[3]
    for k in range(TOP_K):
        gate_ref[k:k + 1, :] = ex[k] / den
        idx_ref[k:k + 1, :] = idxs[k]


def _stage_m(fn, hs, gg, x, gt_m, sh_f, sc_f, g_out_r, g_ffn, w_out_bf, w_router_t, b_router, jmat, tm):
    b, s, d = x.shape
    df = fn.shape[2]
    dr = hs.shape[2]
    nt = s // tm
    ne = w_router_t.shape[0]
    vec = pl.BlockSpec((1, 1, d), lambda i, t: (i, 0, 0))
    half = lambda dd: pl.BlockSpec((1, tm, dd), lambda i, t: (i, t, 0))
    full = lambda shape: pl.BlockSpec(shape, lambda i, t: (0,) * len(shape))
    tok = pl.BlockSpec((TOP_K, tm), lambda i, t: (0, i * nt + t))
    return pl.pallas_call(
        functools.partial(_stage_m_kernel, df=df),
        grid=(b, nt),
        in_specs=[half(df), half(dr), half(dr), half(d), vec, vec, vec,
                  full((1, dr)), full((1, d)), full((d, d)), full((ne, d)), full((ne, 1)),
                  full((GRID_W, GRID_W))],
        out_specs=[half(d), half(d // 2), tok, tok],
        out_shape=[jax.ShapeDtypeStruct((b, s, d), F32), jax.ShapeDtypeStruct((b, s, d // 2), jnp.int32),
                   jax.ShapeDtypeStruct((TOP_K, b * s), jnp.int32),
                   jax.ShapeDtypeStruct((TOP_K, b * s), F32)],
        compiler_params=_cparams(("parallel", "arbitrary")),
        name="stage_m",
    )(fn, hs, gg, x, gt_m, sh_f, sc_f, g_out_r.reshape(1, dr), g_ffn.reshape(1, d), w_out_bf,
      w_router_t, b_router.reshape(ne, 1), jmat)


def _rank_kernel(idx_ref, tri_ref, rank_ref, cnt_ref, carry_s):
    c = pl.program_id(0)

    @pl.when(c == 0)
    def _():
        carry_s[...] = jnp.zeros_like(carry_s)

    l = idx_ref.shape[1]
    eidx = lax.broadcasted_iota(jnp.int32, (N_EXPERTS, l), 0)
    for k in range(TOP_K):
        onehot = eidx == idx_ref[k:k + 1, :]
        oh = jnp.where(onehot, 1.0, 0.0)
        prefix = jnp.dot(oh.astype(BF16), tri_ref[...], preferred_element_type=F32)
        carry = carry_s[:, 0:1]
        rank = jnp.sum(jnp.where(onehot, prefix - 1.0 + carry, 0.0), axis=0, keepdims=True)
        rank_ref[k:k + 1, :] = rank.astype(jnp.int32)
        carry_s[...] = carry_s[...] + jnp.sum(oh, axis=1, keepdims=True)
    cnt_ref[...] = carry_s[...].astype(jnp.int32)


def _dispatch_ranks(idx, tri, tl):
    k, t = idx.shape
    return pl.pallas_call(
        _rank_kernel,
        grid=(t // tl,),
        in_specs=[pl.BlockSpec((k, tl), lambda c: (0, c)),
                  pl.BlockSpec((tl, tl), lambda c: (0, 0))],
        out_specs=[pl.BlockSpec((k, tl), lambda c: (0, c)),
                   pl.BlockSpec((N_EXPERTS, LANES), lambda c: (0, 0))],
        out_shape=[jax.ShapeDtypeStruct((k, t), jnp.int32),
                   jax.ShapeDtypeStruct((N_EXPERTS, LANES), jnp.int32)],
        scratch_shapes=[pltpu.VMEM((N_EXPERTS, LANES), F32)],
        compiler_params=_cparams(("arbitrary",)),
        name="dispatch_ranks",
    )(idx, tri)


def _moe_kernel(be_ref, bv_ref, nu_ref, x_ref, wgu_ref, bgu_ref, wd_ref, bd_ref, o_ref, wgu_s, wd_s, *, dff):
    i = pl.program_id(0)
    h = x_ref.shape[1]

    @pl.when(jnp.logical_or(i == 0, be_ref[i] != be_ref[jnp.maximum(i - 1, 0)]))
    def _():
        wgu_s[...] = wgu_ref[0].astype(BF16)
        wd_s[...] = wd_ref[0].astype(BF16)

    @pl.when(i < nu_ref[0])
    def _():
        rows = lax.broadcasted_iota(jnp.int32, x_ref.shape, 0)
        xw = jnp.where(rows < bv_ref[i], x_ref[...], 0)
        xa, xb = _unpack_halves(xw)
        xa = xa.astype(BF16)
        xb = xb.astype(BF16)
        acc = None
        for c in range(dff // MOE_FF_CHUNK):
            gs = slice(c * MOE_FF_CHUNK, (c + 1) * MOE_FF_CHUNK)
            us = slice(dff + c * MOE_FF_CHUNK, dff + (c + 1) * MOE_FF_CHUNK)
            g = jnp.dot(xa, wgu_s[:h, gs], preferred_element_type=F32)
            g += jnp.dot(xb, wgu_s[h:, gs], preferred_element_type=F32)
            u = jnp.dot(xa, wgu_s[:h, us], preferred_element_type=F32)
            u += jnp.dot(xb, wgu_s[h:, us], preferred_element_type=F32)
            gt = jnp.minimum(g + bgu_ref[0, :, gs], SWIGLU_LIMIT)
            up = jnp.clip(u + bgu_ref[0, :, us], -SWIGLU_LIMIT, SWIGLU_LIMIT)
            act = (up + 1.0) * (gt * _sigmoid(SWIGLU_ALPHA * gt))
            part = jnp.dot(act.astype(BF16), wd_s[gs, :], preferred_element_type=F32)
            acc = part if acc is None else acc + part
        o_ref[...] = _pack_halves(acc + bd_ref[0])


def _moe_experts(blk_expert, blk_valid, n_used, xs, wgu, bgu, wd, bd, tmm):
    cap, h = xs.shape
    ne, d, dff2 = wgu.shape
    dff = dff2 // 2
    row_blk = lambda i, be, bv, nu: (jnp.maximum(jnp.minimum(i, nu[0] - 1), 0), 0)
    wsel = lambda i, be, bv, nu: (be[i], 0, 0)
    grid_spec = pltpu.PrefetchScalarGridSpec(
        num_scalar_prefetch=3,
        grid=(cap // tmm,),
        in_specs=[pl.BlockSpec((tmm, h), row_blk),
                  pl.BlockSpec((1, d, dff2), wsel),
                  pl.BlockSpec((1, 1, dff2), wsel),
                  pl.BlockSpec((1, dff, d), wsel),
                  pl.BlockSpec((1, 1, d), wsel)],
        out_specs=pl.BlockSpec((tmm, h), row_blk),
        scratch_shapes=[pltpu.VMEM((d, dff2), BF16), pltpu.VMEM((dff, d), BF16)],
    )
    return pl.pallas_call(
        functools.partial(_moe_kernel, dff=dff),
        grid_spec=grid_spec,
        out_shape=jax.ShapeDtypeStruct((cap, h), jnp.int32),
        compiler_params=_cparams(("arbitrary",)),
        name="moe_experts",
    )(blk_expert, blk_valid, n_used, xs, wgu, bgu.reshape(ne, 1, dff2), wd, bd.reshape(ne, 1, d))


SC_CHUNK = 64
SC_ID_CHUNK = 128


def _sc_workers():
    info = plsc.get_sparse_core_info()
    return info.num_cores, info.num_subcores


def _sc_scatter(rows, dest, out_rows=None, out_ref=None):
    n, w = rows.shape
    nc, ns = _sc_workers()
    per_w = n // (nc * ns)
    assert per_w % SC_CHUNK == 0
    mesh = plsc.VectorSubcoreMesh(core_axis_name="c", subcore_axis_name="s")
    out_type = () if out_ref is not None else jax.ShapeDtypeStruct((out_rows, w), jnp.int32)

    @functools.partial(
        pl.kernel, mesh=mesh, out_type=out_type,
        scratch_types=[pltpu.VMEM((SC_CHUNK,), jnp.int32),
                       pltpu.VMEM((SC_CHUNK, w), jnp.int32),
                       pltpu.SemaphoreType.DMA],
    )
    def scatter_rows(rows_hbm, dest_hbm, out_hbm, idx_v, rows_v, sem):
        base = (lax.axis_index("s") * nc + lax.axis_index("c")) * per_w

        @pl.loop(0, per_w // SC_CHUNK)
        def _(j):
            off = pl.multiple_of(base + j * SC_CHUNK, SC_CHUNK)
            pltpu.sync_copy(rows_hbm.at[pl.ds(off, SC_CHUNK)], rows_v)
            pltpu.sync_copy(dest_hbm.at[pl.ds(off, SC_CHUNK)], idx_v)
            pltpu.async_copy(rows_v, out_hbm.at[idx_v], sem).wait()

    if out_ref is not None:
        scatter_rows(rows, dest, out_ref)
        return None
    return scatter_rows(rows, dest)


def _sc_scatter_ids(dest, out_rows):
    n = dest.shape[0]
    info = plsc.get_sparse_core_info()
    nc, ns, nl = info.num_cores, info.num_subcores, info.num_lanes
    per_w = n // (nc * ns)
    chunk = SC_ID_CHUNK
    assert per_w % (2 * chunk) == 0
    mesh = plsc.VectorSubcoreMesh(core_axis_name="c", subcore_axis_name="s")
    idx_t = pltpu.VMEM((chunk,), jnp.int32)
    rows_t = pltpu.VMEM((chunk, LANES), jnp.int32)

    @functools.partial(
        pl.kernel, mesh=mesh,
        out_type=jax.ShapeDtypeStruct((out_rows, LANES), jnp.int32),
        scratch_types=[idx_t, idx_t, rows_t, rows_t, pltpu.SemaphoreType.DMA, pltpu.SemaphoreType.DMA],
    )
    def scatter_ids(dest_hbm, out_hbm, idx_a, idx_b, rows_a, rows_b, sem_a, sem_b):
        base = (lax.axis_index("s") * nc + lax.axis_index("c")) * per_w

        def start(off, idx_v, rows_v, sem):
            for r in range(chunk):
                rows_v[r, pl.ds(0, nl)] = jnp.zeros((nl,), jnp.int32) + (off + r)
            pltpu.sync_copy(dest_hbm.at[pl.ds(off, chunk)], idx_v)
            return pltpu.async_copy(rows_v, out_hbm.at[idx_v], sem)

        @pl.loop(0, per_w // (2 * chunk))
        def _(j):
            off = pl.multiple_of(base + j * (2 * chunk), 2 * chunk)
            copy_a = start(off, idx_a, rows_a, sem_a)
            copy_b = start(off + chunk, idx_b, rows_b, sem_b)
            copy_a.wait()
            copy_b.wait()

    return scatter_ids(dest)


def _sc_gather(table, idx):
    n = idx.shape[0]
    w = table.shape[1]
    nc, ns = _sc_workers()
    per_w = n // (nc * ns)
    assert per_w % SC_CHUNK == 0
    mesh = plsc.VectorSubcoreMesh(core_axis_name="c", subcore_axis_name="s")

    @functools.partial(
        pl.kernel, mesh=mesh,
        out_type=jax.ShapeDtypeStruct((n, w), jnp.int32),
        scratch_types=[pltpu.VMEM((SC_CHUNK,), jnp.int32),
                       pltpu.VMEM((SC_CHUNK, w), jnp.int32),
                       pltpu.SemaphoreType.DMA],
    )
    def gather_rows(table_hbm, idx_hbm, out_hbm, idx_v, rows_v, sem):
        base = (lax.axis_index("s") * nc + lax.axis_index("c")) * per_w

        @pl.loop(0, per_w // SC_CHUNK)
        def _(j):
            off = pl.multiple_of(base + j * SC_CHUNK, SC_CHUNK)
            pltpu.sync_copy(idx_hbm.at[pl.ds(off, SC_CHUNK)], idx_v)
            pltpu.async_copy(table_hbm.at[idx_v], rows_v, sem).wait()
            pltpu.sync_copy(rows_v, out_hbm.at[pl.ds(off, SC_CHUNK)])

    return gather_rows(table, idx)


def _combine_kernel(x1_ref, y_ref, gate_ref, gtf_ref, g_ref, o_ref):
    h = y_ref.shape[3]
    gates = gate_ref[0]
    moe_a = moe_b = None
    for k in range(TOP_K):
        ya, yb = _unpack_halves(y_ref[k, 0])
        gk = gates[:, k:k + 1]
        moe_a = gk * ya if k == 0 else moe_a + gk * ya
        moe_b = gk * yb if k == 0 else moe_b + gk * yb
    za = x1_ref[0, :, :h] + gtf_ref[0, :, :h] * moe_a
    zb = x1_ref[0, :, h:] + gtf_ref[0, :, h:] * moe_b
    ms = (jnp.sum(za * za, axis=-1, keepdims=True) + jnp.sum(zb * zb, axis=-1, keepdims=True)) / (2 * h)
    inv = lax.rsqrt(ms + EPS)
    o_ref[0, :, :h] = za * inv * g_ref[:, :h]
    o_ref[0, :, h:] = zb * inv * g_ref[:, h:]


def _combine(x1, yk, gates_t, gt_f, g_final, tm):
    b, s, d = x1.shape
    h = yk.shape[3]
    return pl.pallas_call(
        _combine_kernel,
        grid=(b, s // tm),
        in_specs=[pl.BlockSpec((1, tm, d), lambda i, t: (i, t, 0)),
                  pl.BlockSpec((TOP_K, 1, tm, h), lambda i, t: (0, i, t, 0)),
                  pl.BlockSpec((1, tm, TOP_K), lambda i, t: (i, t, 0)),
                  pl.BlockSpec((1, 1, d), lambda i, t: (i, 0, 0)),
                  pl.BlockSpec((1, d), lambda i, t: (0, 0))],
        out_specs=pl.BlockSpec((1, tm, d), lambda i, t: (i, t, 0)),
        out_shape=jax.ShapeDtypeStruct((b, s, d), F32),
        compiler_params=_cparams(("parallel", "arbitrary")),
        name="combine",
    )(x1, yk, gates_t, gt_f, g_final.reshape(1, d))


def _dft_tables(rows, gd):
    seq = rows * GRID_W
    n = np.arange(rows)
    ang1 = 2.0 * np.pi * np.outer(n, n) / rows
    d2 = np.concatenate([np.cos(ang1), -np.sin(ang1)], axis=0)
    k1 = np.arange(rows)[:, None, None]
    k2 = np.arange(GRID_W)[None, :, None]
    n2 = np.arange(GRID_W)[None, None, :]
    ang2 = 2.0 * np.pi * ((n2 * (k1 + rows * k2)) % seq) / seq
    ec, es = np.cos(ang2), np.sin(ang2)
    etab = np.concatenate([np.concatenate([ec, es], axis=2),
                           np.concatenate([-es, ec], axis=2)], axis=1)
    c = np.arange(gd)
    angc = 2.0 * np.pi * np.outer(c, c) / gd
    scale = 1.0 / np.sqrt(seq * gd)
    return (jnp.asarray(d2, BF16), jnp.asarray(etab, BF16),
            jnp.asarray(np.cos(angc) * scale, F32), jnp.asarray(np.sin(angc) * scale, F32))


def _block_diag(w):
    h, i, o = w.shape
    eye = jnp.eye(h, dtype=w.dtype)
    return (eye[:, None, :, None] * w[:, :, None, :]).reshape(h * i, h * o)


def kernel(x, c, ctx, c_ctx, w_mod, b_mod, g_norm_mix, g_norm_ffn, w_in, w_fourier, conv_w, conv_b,
           rg_w_a, rg_b_a, rg_w_x, rg_b_x, rg_lam, g_out_fourier, g_out_rg, w_out, w_router, b_router,
           w_gate_up, b_gate_up, w_down, b_down, g_final):
    assert w_mod.shape[0] == 1, "single-layer stack only"
    b, s, d = x.shape
    df = w_fourier.shape[1] * w_fourier.shape[2]
    dr = conv_w.shape[2]
    gd = w_fourier.shape[2]
    rows = s // GRID_W
    t = b * s
    ne = w_router.shape[2]

    mrows = -(-(b + 1) // SUBLANES) * SUBLANES
    cond = jnp.zeros((mrows, d), F32).at[:b].set(c).at[b].set(c_ctx)
    mod = _adaln(cond, w_mod[0], b_mod[0])
    sh_m, sc_m, gt_m, sh_f, sc_f, gt_f = [mod[:b, k * d:(k + 1) * d].reshape(b, 1, d) for k in range(N_MOD)]
    csh_m = mod[b:b + 1, 0:d].reshape(1, 1, d)
    csc_m = mod[b:b + 1, d:2 * d].reshape(1, 1, d)

    tm = min(s, TOKEN_TILE)
    d2, etab, cmat, smat = _dft_tables(rows, gd)
    jmat = jnp.asarray(np.eye(GRID_W)[::-1].copy(), BF16)

    w_in_bf = w_in[0].astype(BF16)
    f, xs, gg = _stage_b(x, sh_m, sc_m, g_norm_mix[0], w_in_bf, jmat, df, dr, tm=tm)
    xr_ctx = _stage_b_ctx(ctx, csh_m, csc_m, g_norm_mix[0], w_in_bf[:, df:df + dr])

    wcat = jnp.stack([jnp.concatenate([_block_diag(rg_w_a[0, dd]), _block_diag(rg_w_x[0, dd])], axis=1)
                      for dd in range(2)]).astype(BF16)
    bcat = jnp.concatenate([rg_b_a[0], rg_b_x[0]], axis=1).reshape(2, 1, 2 * dr)
    lam = rg_lam[0].reshape(2, 1, dr)
    h0 = jnp.zeros((b, SUBLANES, dr), F32)
    _, hfin_ctx = _rg_scan(xr_ctx, h0, conv_w[0], conv_b[0], wcat, bcat, lam, tc=ctx.shape[1])
    hs, _ = _rg_scan(xs, hfin_ctx, conv_w[0], conv_b[0], wcat, bcat, lam, tc=min(s, SCAN_CHUNK))

    cw, sw = _fold_fourier(cmat, smat, w_fourier[0])
    bdc = _block_diag(cw).astype(BF16)
    bds = _block_diag(sw).astype(BF16)
    y = _fourier_stage1(f, d2, tl=min(GRID_W * df, 8192))
    fn = _fourier_stage2(y.reshape(b, 2, rows, GRID_W, df), etab, bdc, bds, g_out_fourier[0],
                         kb=min(rows, 16))
    fn = fn.reshape(b, s, df)

    x1, h2, idx, gates = _stage_m(fn, hs, gg, x, gt_m, sh_f, sc_f, g_out_rg[0], g_norm_ffn[0],
                                  w_out[0].astype(BF16), w_router[0].T, b_router[0], jmat, tm=tm)

    tl = min(t, RANK_TILE)
    tri = jnp.asarray(np.triu(np.ones((tl, tl))), BF16)
    rank, cnt = _dispatch_ranks(idx, tri, tl)
    counts = cnt[:, 0]
    tmm = MOE_ROW_TILE
    padded = (counts + tmm - 1) // tmm * tmm
    pad_end = jnp.cumsum(padded)
    pad_start = pad_end - padded
    eids = jnp.arange(ne, dtype=jnp.int32)
    dest = rank + jnp.sum(jnp.where(idx[:, :, None] == eids, pad_start, 0), axis=-1)
    n_blocks = -(-(t * TOP_K) // tmm) + ne
    cap = n_blocks * tmm
    n_used = (pad_end[-1] // tmm).astype(jnp.int32).reshape(1)
    blk_start = jnp.arange(n_blocks, dtype=jnp.int32) * tmm
    blk_expert = jnp.sum(blk_start[:, None] >= pad_end[None, :], axis=1).astype(jnp.int32)
    last_expert = jnp.sum(pad_end[-1] - tmm >= pad_end).astype(jnp.int32)
    blk_expert = jnp.minimum(blk_expert, last_expert)
    sel = blk_expert[:, None] == eids
    blk_first = jnp.sum(jnp.where(sel, pad_start, 0), axis=1)
    blk_count = jnp.sum(jnp.where(sel, counts, 0), axis=1)
    blk_valid = jnp.clip(blk_count - (blk_start - blk_first), 0, tmm).astype(jnp.int32)

    na = TOP_K * t
    inv = _sc_scatter_ids(dest.reshape(-1), cap)[:, 0].reshape(n_blocks, tmm)
    live = jnp.arange(tmm, dtype=jnp.int32)[None, :] < blk_valid[:, None]
    spread = jnp.arange(cap, dtype=jnp.int32).reshape(n_blocks, tmm) % t
    src_tok = jnp.where(live, inv % t, spread).reshape(-1)
    dst_row = jnp.where(live, inv, na + spread).reshape(-1)

    h2_rows = h2.reshape(t, d // 2)
    y_all = jax.empty_ref(jax.ShapeDtypeStruct(((TOP_K + 1) * t, d // 2), jnp.int32))
    unit = n_blocks // sum(MOE_PIPE)
    assert unit * sum(MOE_PIPE) == n_blocks
    blk0 = 0
    for parts in MOE_PIPE:
        nq = parts * unit
        blocks = slice(blk0, blk0 + nq)
        rows = slice(blk0 * tmm, (blk0 + nq) * tmm)
        x_q = _sc_gather(h2_rows, src_tok[rows])
        nu_q = jnp.clip(n_used - blk0, 0, nq).astype(jnp.int32)
        blk0 += nq
        y_q = _moe_experts(blk_expert[blocks], blk_valid[blocks], nu_q, x_q,
                           w_gate_up[0], b_gate_up[0], w_down[0], b_down[0], tmm)
        _sc_scatter(y_q, dst_row[rows], out_ref=y_all)
    yk = y_all[...].reshape(TOP_K + 1, b, s, d // 2)
    return _combine(x1, yk, gates.T.reshape(b, s, TOP_K), gt_f, g_final, tm)
```

```python
import functools

import numpy as np
import jax
import jax.numpy as jnp
from jax import lax
from jax.experimental import pallas as pl
from jax.experimental.pallas import tpu as pltpu
from jax.experimental.pallas import tpu_sc as plsc

GRID_W = 64
FOURIER_GROUPS = 4
RG_HEADS = 8
CONV_W = 4
CONV_PAD_LO = 2
RG_C = 8.0
N_EXPERTS = 32
TOP_K = 4
SWIGLU_LIMIT = 7.0
SWIGLU_ALPHA = 1.702
N_MOD = 6
EPS = 1e-6

LANES = 128
SUBLANES = 8
VMEM_LIMIT_BYTES = 56 * 1024 * 1024
TOKEN_TILE = 1024
RANK_TILE = 512
RANK_STEP = 2048
SCAN_CHUNK = 512
MOE_ROW_TILE = 512
MOE_FF_CHUNK = 512
MOE_PIPE = (1, 3, 3, 1)

F32 = jnp.float32
BF16 = jnp.bfloat16


def _cparams(sem):
    return pltpu.CompilerParams(dimension_semantics=sem, vmem_limit_bytes=VMEM_LIMIT_BYTES)


def _split_bf16(a):
    hi = a.astype(BF16)
    lo = (a - hi.astype(F32)).astype(BF16)
    return hi, lo


def _dot3(a, b):
    ah, al = _split_bf16(a)
    bh, bl = _split_bf16(b)
    out = jnp.dot(ah, bh, preferred_element_type=F32)
    out += jnp.dot(ah, bl, preferred_element_type=F32)
    out += jnp.dot(al, bh, preferred_element_type=F32)
    return out


def _dot3_nt(a, b):
    dn = (((1,), (1,)), ((), ()))
    ah, al = _split_bf16(a)
    bh, bl = _split_bf16(b)
    out = lax.dot_general(ah, bh, dn, preferred_element_type=F32)
    out += lax.dot_general(ah, bl, dn, preferred_element_type=F32)
    out += lax.dot_general(al, bh, dn, preferred_element_type=F32)
    return out


def _gelu_tanh(x):
    return 0.5 * x * (1.0 + jnp.tanh(0.7978845608028654 * (x + 0.044715 * (x * x * x))))


def _rms(x, g):
    return x * lax.rsqrt(jnp.mean(x * x, axis=-1, keepdims=True) + EPS) * g


def _pack_halves(v):
    h = v.shape[1] // 2
    hi = lax.bitcast_convert_type(v[:, :h].astype(BF16).astype(F32), jnp.uint32)
    lo = lax.bitcast_convert_type(v[:, h:].astype(BF16).astype(F32), jnp.uint32)
    return lax.bitcast_convert_type(hi | (lo >> 16), jnp.int32)


def _unpack_halves(w):
    u = lax.bitcast_convert_type(w, jnp.uint32)
    hi = lax.bitcast_convert_type(u & jnp.uint32(0xFFFF0000), F32)
    lo = lax.bitcast_convert_type(u << 16, F32)
    return hi, lo


def _adaln_kernel(c_ref, w_ref, b_ref, o_ref):
    s = c_ref[...]
    s = s * jax.nn.sigmoid(s)
    o_ref[...] = _dot3(s, w_ref[...]) + b_ref[...]


def _adaln(cond, w_mod, b_mod):
    m, d = cond.shape
    n = w_mod.shape[1]
    tn = n // N_MOD
    return pl.pallas_call(
        _adaln_kernel,
        grid=(n // tn,),
        in_specs=[pl.BlockSpec((m, d), lambda i: (0, 0)),
                  pl.BlockSpec((d, tn), lambda i: (0, i)),
                  pl.BlockSpec((1, tn), lambda i: (0, i))],
        out_specs=pl.BlockSpec((m, tn), lambda i: (0, i)),
        out_shape=jax.ShapeDtypeStruct((m, n), F32),
        compiler_params=_cparams(("arbitrary",)),
        name="adaln",
    )(cond, w_mod, b_mod.reshape(1, n))


def _fold_kernel(c_ref, s_ref, w_ref, cw_ref, sw_ref):
    w = w_ref[0]
    cw_ref[0] = _dot3(c_ref[...], w)
    sw_ref[0] = _dot3(s_ref[...], w)


def _fold_fourier(cmat, smat, w_f):
    g, gd, _ = w_f.shape
    spec_m = pl.BlockSpec((gd, gd), lambda i: (0, 0))
    spec_w = pl.BlockSpec((1, gd, gd), lambda i: (i, 0, 0))
    return pl.pallas_call(
        _fold_kernel,
        grid=(g,),
        in_specs=[spec_m, spec_m, spec_w],
        out_specs=[spec_w, spec_w],
        out_shape=[jax.ShapeDtypeStruct((g, gd, gd), F32)] * 2,
        compiler_params=_cparams(("arbitrary",)),
        name="fold_fourier",
    )(cmat, smat, w_f)


def _seg_pitch(seg_len):
    n8 = seg_len // SUBLANES
    return SUBLANES * (n8 + 1 - n8 % 2)


def _pitched_store(v, buf, blk0):
    pitch = _seg_pitch(GRID_W)
    for r in range(v.shape[0] // GRID_W):
        for c in range(v.shape[1] // LANES):
            buf[c, (blk0 + r) * pitch:(blk0 + r) * pitch + GRID_W, :] = (
                v[r * GRID_W:(r + 1) * GRID_W, c * LANES:(c + 1) * LANES])


def _pitched_gather(buf, nb, store):
    pitch = _seg_pitch(GRID_W)
    for pos in range(GRID_W):
        for c in range(buf.shape[0]):
            store(pos, c, buf[c, pl.ds(pos, nb, stride=pitch), :])


def _stage_b_kernel(x_ref, sh_ref, sc_ref, g_ref, w_ref, j_ref, f_ref, xs_ref, gg_ref, fbuf, *, df, dr):
    tm = x_ref.shape[1]
    h = _rms(x_ref[0], g_ref[...]) * (1.0 + sc_ref[0]) + sh_ref[0]
    hb = h.astype(BF16)
    _pitched_store(jnp.dot(hb, w_ref[:, :df], preferred_element_type=F32), fbuf, 0)

    def store_f(pos, c, tile):
        f_ref[0, :, pos * df + c * LANES:pos * df + (c + 1) * LANES] = tile.astype(BF16)
    _pitched_gather(fbuf, tm // GRID_W, store_f)
    xr = jnp.dot(hb, w_ref[:, df:df + dr], preferred_element_type=F32).astype(BF16)
    for r in range(tm // GRID_W):
        blk = xr[r * GRID_W:(r + 1) * GRID_W]
        if r % 2 == 1:
            blk = jnp.dot(j_ref[...], blk, preferred_element_type=F32).astype(BF16)
        xs_ref[0, r * GRID_W:(r + 1) * GRID_W, :] = blk
    gr = jnp.dot(hb, w_ref[:, df + dr:], preferred_element_type=F32)
    gg_ref[0] = _gelu_tanh(gr).astype(BF16)


def _stage_b(x, shift, scale, g, w_in_bf, jmat, df, dr, tm):
    b, s, d = x.shape
    n = w_in_bf.shape[1]
    vec = pl.BlockSpec((1, 1, d), lambda i, t: (i, 0, 0))
    out = pl.BlockSpec((1, tm, df), lambda i, t: (i, t, 0))
    nb = tm // GRID_W
    tok = jax.ShapeDtypeStruct((b, s, df), BF16)
    return pl.pallas_call(
        functools.partial(_stage_b_kernel, df=df, dr=dr),
        grid=(b, s // tm),
        in_specs=[pl.BlockSpec((1, tm, d), lambda i, t: (i, t, 0)), vec, vec,
                  pl.BlockSpec((1, d), lambda i, t: (0, 0)),
                  pl.BlockSpec((d, n), lambda i, t: (0, 0)),
                  pl.BlockSpec((GRID_W, GRID_W), lambda i, t: (0, 0))],
        out_specs=[pl.BlockSpec((1, nb, GRID_W * df), lambda i, t: (i, t, 0)), out, out],
        out_shape=[jax.ShapeDtypeStruct((b, s // GRID_W, GRID_W * df), BF16), tok, tok],
        scratch_shapes=[pltpu.VMEM((df // LANES, nb * _seg_pitch(GRID_W), LANES), F32)],
        compiler_params=_cparams(("parallel", "arbitrary")),
        name="stage_b",
    )(x, shift, scale, g.reshape(1, d), w_in_bf, jmat)


def _stage_b_ctx_kernel(x_ref, sh_ref, sc_ref, g_ref, w_ref, xr_ref):
    h = _rms(x_ref[0], g_ref[...]) * (1.0 + sc_ref[0]) + sh_ref[0]
    xr_ref[0] = jnp.dot(h.astype(BF16), w_ref[...], preferred_element_type=F32).astype(BF16)


def _stage_b_ctx(ctx, shift, scale, g, w_xr_bf):
    b, s, d = ctx.shape
    dr = w_xr_bf.shape[1]
    vec = pl.BlockSpec((1, 1, d), lambda i: (0, 0, 0))
    return pl.pallas_call(
        _stage_b_ctx_kernel,
        grid=(b,),
        in_specs=[pl.BlockSpec((1, s, d), lambda i: (i, 0, 0)), vec, vec,
                  pl.BlockSpec((1, d), lambda i: (0, 0)),
                  pl.BlockSpec((d, dr), lambda i: (0, 0))],
        out_specs=pl.BlockSpec((1, s, dr), lambda i: (i, 0, 0)),
        out_shape=jax.ShapeDtypeStruct((b, s, dr), BF16),
        compiler_params=_cparams(("arbitrary",)),
        name="stage_b_ctx",
    )(ctx, shift, scale, g.reshape(1, d), w_xr_bf)


HALO = 16


def _sigmoid(x):
    return 0.5 * jnp.tanh(0.5 * x) + 0.5


def _rg_kernel(xs_ref, h0_ref, cw_ref, cb_ref, w_ref, b_ref, lam_ref, pm_ref, pmt_ref, out_ref, hfin_ref,
               hf_s, xc_s, a_s, u_s, hl_s, p_s, c_s, hc_s, *, tc, nchunk, seq, dr):
    p = pl.program_id(1)
    j = pl.program_id(2)
    cidx = jnp.where(p == 0, j, nchunk - 1 - j)
    start = pl.multiple_of(cidx * tc, tc)
    nseg = SUBLANES
    sl = tc // nseg
    sub = lax.broadcasted_iota(jnp.int32, (nseg, dr), 0)

    @pl.when(p == 0)
    def _():
        xp = jnp.dot(pm_ref[...], xs_ref[0, pl.ds(start, tc), :], preferred_element_type=F32)
        pstart = pl.multiple_of(jnp.maximum(start - HALO, 0), HALO)
        nstart = pl.multiple_of(jnp.minimum(start + tc, seq - HALO), HALO)
        prev = xs_ref[0, pl.ds(pstart, HALO), :].astype(F32)
        nxt = xs_ref[0, pl.ds(nstart, HALO), :].astype(F32)
        prev = jnp.where(cidx > 0, prev, 0.0)
        nxt = jnp.where(cidx < nchunk - 1, nxt, 0.0)
        tm2 = jnp.where(sub == 0, prev[HALO - 2:HALO - 1], pltpu.roll(xp[(sl - 2) * nseg:(sl - 1) * nseg], 1, 0))
        tm1 = jnp.where(sub == 0, prev[HALO - 1:HALO], pltpu.roll(xp[(sl - 1) * nseg:sl * nseg], 1, 0))
        tp1 = jnp.where(sub == nseg - 1, nxt[0:1], pltpu.roll(xp[0:nseg], nseg - 1, 0))
        ext = jnp.concatenate([tm2, tm1, xp, tp1], axis=0)
        xc = cb_ref[...] + cw_ref[0:1, :] * ext[0:tc]
        for k in range(1, CONV_W):
            xc = xc + cw_ref[k:k + 1, :] * ext[k * nseg:k * nseg + tc]
        xc_s[pl.ds(start, tc), :] = xc

    xc = xc_s[pl.ds(start, tc), :]
    gates = jnp.dot(xc.astype(BF16), w_ref[p], preferred_element_type=F32) + b_ref[p]
    i = _sigmoid(gates[:, dr:])
    half_c = (-0.5 * RG_C) * jax.nn.softplus(-lam_ref[p])
    log_a = half_c * jnp.tanh(0.5 * gates[:, :dr]) + half_c
    a = jnp.exp(log_a)
    a_s[...] = a
    w = -jnp.tanh(log_a) * (1.0 + a * a)
    u_s[...] = jnp.where(w > 0.0, w * lax.rsqrt(w), 0.0) * (i * xc)

    @pl.when(jnp.logical_and(p == 0, j == 0))
    def _():
        hfin_ref[...] = jnp.zeros_like(hfin_ref)

    @pl.when(j == 0)
    def _():
        hc_s[0:1, :] = h0_ref[0, pl.ds(p, 1), :]

    def segment_scan(reverse):
        def body(q, carry):
            t = (sl - 1 - q) if reverse else q
            rows = pl.ds(pl.multiple_of(t * nseg, nseg), nseg)
            h, pr = carry
            av = a_s[rows, :]
            h = av * h + u_s[rows, :]
            pr = av * pr
            hl_s[rows, :] = h
            p_s[rows, :] = pr
            return h, pr
        h_end, p_end = lax.fori_loop(0, sl, body, (jnp.zeros((nseg, dr), F32), jnp.ones((nseg, dr), F32)),
                                     unroll=4)
        carry = hc_s[0:1, :]
        for g in (range(nseg - 1, -1, -1) if reverse else range(nseg)):
            c_s[g:g + 1, :] = carry
            carry = h_end[g:g + 1, :] + p_end[g:g + 1, :] * carry
        hc_s[0:1, :] = carry
        return c_s[...]

    def corrected(cin):
        h = hl_s[...].reshape(sl, nseg, dr) + p_s[...].reshape(sl, nseg, dr) * cin[None]
        return h.reshape(tc, dr)

    @pl.when(p == 0)
    def _():
        hf_s[pl.ds(start, tc), :] = corrected(segment_scan(False))

    @pl.when(p == 1)
    def _():
        tot = corrected(segment_scan(True)) + hf_s[pl.ds(start, tc), :]
        out_ref[0] = jnp.dot(pmt_ref[...], tot.astype(BF16), preferred_element_type=F32).astype(BF16)

    @pl.when(j == nchunk - 1)
    def _():
        hfin_ref[0, pl.ds(p, 1), :] = hc_s[0:1, :]


def _rg_scan(xs, h0, conv_w, conv_b, wcat, bcat, lam, tc):
    b, s, dr = xs.shape
    nchunk = s // tc
    last = nchunk - 1
    sl = tc // SUBLANES
    src = (np.arange(tc) % SUBLANES) * sl + np.arange(tc) // SUBLANES
    pm = np.zeros((tc, tc), np.float32)
    pm[np.arange(tc), src] = 1.0
    chunk_buf = pltpu.VMEM((tc, dr), F32)
    full2 = lambda shape: pl.BlockSpec(shape, lambda i, p, j: (0,) * len(shape))
    return pl.pallas_call(
        functools.partial(_rg_kernel, tc=tc, nchunk=nchunk, seq=s, dr=dr),
        grid=(b, 2, nchunk),
        in_specs=[pl.BlockSpec((1, s, dr), lambda i, p, j: (i, 0, 0)),
                  pl.BlockSpec((1, SUBLANES, dr), lambda i, p, j: (i, 0, 0)),
                  full2((CONV_W, dr)), full2((1, dr)),
                  full2((2, dr, 2 * dr)), full2((2, 1, 2 * dr)), full2((2, 1, dr)),
                  full2((tc, tc)), full2((tc, tc))],
        out_specs=[pl.BlockSpec((1, tc, dr), lambda i, p, j: (i, jnp.where(p == 0, last, last - j), 0)),
                   pl.BlockSpec((1, SUBLANES, dr), lambda i, p, j: (i, 0, 0))],
        out_shape=[jax.ShapeDtypeStruct((b, s, dr), BF16),
                   jax.ShapeDtypeStruct((b, SUBLANES, dr), F32)],
        scratch_shapes=[pltpu.VMEM((s, dr), F32), pltpu.VMEM((s, dr), F32),
                        chunk_buf, chunk_buf, chunk_buf, chunk_buf,
                        pltpu.VMEM((SUBLANES, dr), F32), pltpu.VMEM((SUBLANES, dr), F32)],
        compiler_params=_cparams(("arbitrary", "arbitrary", "arbitrary")),
        name="rg_scan",
    )(xs, h0, conv_w, conv_b.reshape(1, dr), wcat, bcat, lam, jnp.asarray(pm, BF16), jnp.asarray(pm.T, BF16))


def _f1_kernel(d_ref, x_ref, y_ref):
    y_ref[0] = jnp.dot(d_ref[...], x_ref[0], preferred_element_type=F32).astype(BF16)


def _fourier_stage1(fv, d2, tl):
    b, r, n = fv.shape
    return pl.pallas_call(
        _f1_kernel,
        grid=(b, n // tl),
        in_specs=[pl.BlockSpec((2 * r, r), lambda i, l: (0, 0)),
                  pl.BlockSpec((1, r, tl), lambda i, l: (i, 0, l))],
        out_specs=pl.BlockSpec((1, 2 * r, tl), lambda i, l: (i, 0, l)),
        out_shape=jax.ShapeDtypeStruct((b, 2 * r, n), BF16),
        compiler_params=_cparams(("parallel", "arbitrary")),
        name="fourier_stage1",
    )(d2, fv)


def _f2_kernel(y_ref, e_ref, bc_ref, bs_ref, g_ref, o_ref, obuf, *, kb, df):
    zr, zi = [], []
    for q in range(kb):
        yk = jnp.concatenate([y_ref[0, 0, q], y_ref[0, 1, q]], axis=0)
        z = jnp.dot(e_ref[q], yk, preferred_element_type=F32)
        zr.append(z[:GRID_W])
        zi.append(z[GRID_W:])
    zr = jnp.concatenate(zr, axis=0).astype(BF16)
    zi = jnp.concatenate(zi, axis=0).astype(BF16)
    o = jnp.dot(zr, bc_ref[...], preferred_element_type=F32)
    o += jnp.dot(zi, bs_ref[...], preferred_element_type=F32)
    on = _rms(o, g_ref[...])

    def store_o(pos, c, tile):
        o_ref[0, pos, :, c * LANES:(c + 1) * LANES] = tile.astype(BF16)
    _pitched_store(on, obuf, 0)
    _pitched_gather(obuf, kb, store_o)


def _fourier_stage2(y5, etab, bdc, bds, g, kb):
    b, _, r, w, df = y5.shape
    return pl.pallas_call(
        functools.partial(_f2_kernel, kb=kb, df=df),
        grid=(b, r // kb),
        in_specs=[pl.BlockSpec((1, 2, kb, w, df), lambda i, k: (i, 0, k, 0, 0)),
                  pl.BlockSpec((kb, 2 * w, 2 * w), lambda i, k: (k, 0, 0)),
                  pl.BlockSpec((df, df), lambda i, k: (0, 0)),
                  pl.BlockSpec((df, df), lambda i, k: (0, 0)),
                  pl.BlockSpec((1, df), lambda i, k: (0, 0))],
        out_specs=pl.BlockSpec((1, w, kb, df), lambda i, k: (i, 0, k, 0)),
        out_shape=jax.ShapeDtypeStruct((b, w, r, df), BF16),
        scratch_shapes=[pltpu.VMEM((df // LANES, kb * _seg_pitch(GRID_W), LANES), F32)],
        compiler_params=_cparams(("parallel", "arbitrary")),
        name="fourier_stage2",
    )(y5, etab, bdc, bds, g.reshape(1, df))


def _stage_m_kernel(fn_ref, hs_ref, gg_ref, x_ref, gtm_ref, shf_ref, scf_ref, gr_ref, gffn_ref,
                    wo_ref, wr_ref, br_ref, j_ref, x1_ref, h2_ref, idx_ref, gate_ref, *, df):
    tm = x_ref.shape[1]
    hs = hs_ref[0]
    blocks = []
    for r in range(tm // GRID_W):
        blk = hs[r * GRID_W:(r + 1) * GRID_W]
        if r % 2 == 1:
            blk = jnp.dot(j_ref[...], blk, preferred_element_type=F32)
        blocks.append(blk.astype(F32))
    rg = jnp.concatenate(blocks, axis=0) * gg_ref[0].astype(F32)
    rgn = _rms(rg, gr_ref[...]).astype(BF16)
    mix = jnp.dot(fn_ref[0], wo_ref[:df, :], preferred_element_type=F32)
    mix += jnp.dot(rgn, wo_ref[df:, :], preferred_element_type=F32)
    x1 = x_ref[0] + gtm_ref[0] * mix
    x1_ref[0] = x1
    h2 = _rms(x1, gffn_ref[...]) * (1.0 + scf_ref[0]) + shf_ref[0]
    h2_ref[0] = _pack_halves(h2)

    logits = _dot3_nt(wr_ref[...], h2) + br_ref[...]
    eidx = lax.broadcasted_iota(jnp.int32, logits.shape, 0)
    vals, idxs = [], []
    for _ in range(TOP_K):
        m = jnp.max(logits, axis=0, keepdims=True)
        sel = jnp.min(jnp.where(logits == m, eidx, N_EXPERTS), axis=0, keepdims=True)
        vals.append(m)
        idxs.append(sel)
        logits = jnp.where(eidx == sel, -jnp.inf, logits)
    ex = [jnp.exp(v - vals[0]) for v in vals]
    den = ex[0] + ex[1] + ex[2] + ex[3]
    for k in range(TOP_K):
        gate_ref[k:k + 1, :] = ex[k] / den
        idx_ref[k:k + 1, :] = idxs[k]


def _stage_m(fn, hs, gg, x, gt_m, sh_f, sc_f, g_out_r, g_ffn, w_out_bf, w_router_t, b_router, jmat, tm):
    b, s, d = x.shape
    df = fn.shape[2]
    dr = hs.shape[2]
    nt = s // tm
    ne = w_router_t.shape[0]
    vec = pl.BlockSpec((1, 1, d), lambda i, t: (i, 0, 0))
    half = lambda dd: pl.BlockSpec((1, tm, dd), lambda i, t: (i, t, 0))
    full = lambda shape: pl.BlockSpec(shape, lambda i, t: (0,) * len(shape))
    tok = pl.BlockSpec((TOP_K, tm), lambda i, t: (0, i * nt + t))
    return pl.pallas_call(
        functools.partial(_stage_m_kernel, df=df),
        grid=(b, nt),
        in_specs=[half(df), half(dr), half(dr), half(d), vec, vec, vec,
                  full((1, dr)), full((1, d)), full((d, d)), full((ne, d)), full((ne, 1)),
                  full((GRID_W, GRID_W))],
        out_specs=[half(d), half(d // 2), tok, tok],
        out_shape=[jax.ShapeDtypeStruct((b, s, d), F32), jax.ShapeDtypeStruct((b, s, d // 2), jnp.int32),
                   jax.ShapeDtypeStruct((TOP_K, b * s), jnp.int32),
                   jax.ShapeDtypeStruct((TOP_K, b * s), F32)],
        compiler_params=_cparams(("parallel", "arbitrary")),
        name="stage_m",
    )(fn, hs, gg, x, gt_m, sh_f, sc_f, g_out_r.reshape(1, dr), g_ffn.reshape(1, d), w_out_bf,
      w_router_t, b_router.reshape(ne, 1), jmat)


def _rank_kernel(idx_ref, tri_ref, rank_ref, cnt_ref, carry_s):
    c = pl.program_id(0)

    @pl.when(c == 0)
    def _():
        carry_s[...] = jnp.zeros_like(carry_s)

    l = tri_ref.shape[0]
    eidx = lax.broadcasted_iota(jnp.int32, (N_EXPERTS, l), 0)
    for sub in range(idx_ref.shape[1] // l):
        lanes = slice(sub * l, (sub + 1) * l)
        for k in range(TOP_K):
            onehot = eidx == idx_ref[k:k + 1, lanes]
            oh = jnp.where(onehot, 1.0, 0.0)
            prefix = jnp.dot(oh.astype(BF16), tri_ref[...], preferred_element_type=F32)
            carry = carry_s[:, 0:1]
            rank = jnp.sum(jnp.where(onehot, prefix - 1.0 + carry, 0.0), axis=0, keepdims=True)
            rank_ref[k:k + 1, lanes] = rank.astype(jnp.int32)
            carry_s[...] = carry_s[...] + jnp.sum(oh, axis=1, keepdims=True)
    cnt_ref[...] = carry_s[...].astype(jnp.int32)


def _dispatch_ranks(idx, tri, tl):
    k, t = idx.shape
    return pl.pallas_call(
        _rank_kernel,
        grid=(t // tl,),
        in_specs=[pl.BlockSpec((k, tl), lambda c: (0, c)),
                  pl.BlockSpec(tri.shape, lambda c: (0, 0))],
        out_specs=[pl.BlockSpec((k, tl), lambda c: (0, c)),
                   pl.BlockSpec((N_EXPERTS, LANES), lambda c: (0, 0))],
        out_shape=[jax.ShapeDtypeStruct((k, t), jnp.int32),
                   jax.ShapeDtypeStruct((N_EXPERTS, LANES), jnp.int32)],
        scratch_shapes=[pltpu.VMEM((N_EXPERTS, LANES), F32)],
        compiler_params=_cparams(("arbitrary",)),
        name="dispatch_ranks",
    )(idx, tri)


def _moe_kernel(be_ref, bv_ref, nu_ref, x_ref, wgu_ref, bgu_ref, wd_ref, bd_ref, o_ref, wgu_s, wd_s, *, dff):
    i = pl.program_id(0)
    h = x_ref.shape[1]

    @pl.when(jnp.logical_or(i == 0, be_ref[i] != be_ref[jnp.maximum(i - 1, 0)]))
    def _():
        wgu_s[...] = wgu_ref[0].astype(BF16)
        wd_s[...] = wd_ref[0].astype(BF16)

    @pl.when(i < nu_ref[0])
    def _():
        rows = lax.broadcasted_iota(jnp.int32, x_ref.shape, 0)
        xw = jnp.where(rows < bv_ref[i], x_ref[...], 0)
        xa, xb = _unpack_halves(xw)
        xa = xa.astype(BF16)
        xb = xb.astype(BF16)
        acc = None
        for c in range(dff // MOE_FF_CHUNK):
            gs = slice(c * MOE_FF_CHUNK, (c + 1) * MOE_FF_CHUNK)
            us = slice(dff + c * MOE_FF_CHUNK, dff + (c + 1) * MOE_FF_CHUNK)
            g = jnp.dot(xa, wgu_s[:h, gs], preferred_element_type=F32)
            g += jnp.dot(xb, wgu_s[h:, gs], preferred_element_type=F32)
            u = jnp.dot(xa, wgu_s[:h, us], preferred_element_type=F32)
            u += jnp.dot(xb, wgu_s[h:, us], preferred_element_type=F32)
            gt = jnp.minimum(g + bgu_ref[0, :, gs], SWIGLU_LIMIT)
            up = jnp.clip(u + bgu_ref[0, :, us], -SWIGLU_LIMIT, SWIGLU_LIMIT)
            act = (up + 1.0) * (gt * _sigmoid(SWIGLU_ALPHA * gt))
            part = jnp.dot(act.astype(BF16), wd_s[gs, :], preferred_element_type=F32)
            acc = part if acc is None else acc + part
        o_ref[...] = _pack_halves(acc + bd_ref[0])


def _moe_experts(blk_expert, blk_valid, n_used, xs, wgu, bgu, wd, bd, tmm):
    cap, h = xs.shape
    ne, d, dff2 = wgu.shape
    dff = dff2 // 2
    row_blk = lambda i, be, bv, nu: (jnp.maximum(jnp.minimum(i, nu[0] - 1), 0), 0)
    wsel = lambda i, be, bv, nu: (be[i], 0, 0)
    grid_spec = pltpu.PrefetchScalarGridSpec(
        num_scalar_prefetch=3,
        grid=(cap // tmm,),
        in_specs=[pl.BlockSpec((tmm, h), row_blk),
                  pl.BlockSpec((1, d, dff2), wsel),
                  pl.BlockSpec((1, 1, dff2), wsel),
                  pl.BlockSpec((1, dff, d), wsel),
                  pl.BlockSpec((1, 1, d), wsel)],
        out_specs=pl.BlockSpec((tmm, h), row_blk),
        scratch_shapes=[pltpu.VMEM((d, dff2), BF16), pltpu.VMEM((dff, d), BF16)],
    )
    return pl.pallas_call(
        functools.partial(_moe_kernel, dff=dff),
        grid_spec=grid_spec,
        out_shape=jax.ShapeDtypeStruct((cap, h), jnp.int32),
        compiler_params=_cparams(("arbitrary",)),
        name="moe_experts",
    )(blk_expert, blk_valid, n_used, xs, wgu, bgu.reshape(ne, 1, dff2), wd, bd.reshape(ne, 1, d))


SC_CHUNK = 64
SC_ID_CHUNK = 128


def _sc_workers():
    info = plsc.get_sparse_core_info()
    return info.num_cores, info.num_subcores


def _sc_scatter(rows, dest, out_rows=None, out_ref=None):
    n, w = rows.shape
    nc, ns = _sc_workers()
    per_w = n // (nc * ns)
    assert per_w % SC_CHUNK == 0
    mesh = plsc.VectorSubcoreMesh(core_axis_name="c", subcore_axis_name="s")
    out_type = () if out_ref is not None else jax.ShapeDtypeStruct((out_rows, w), jnp.int32)

    @functools.partial(
        pl.kernel, mesh=mesh, out_type=out_type,
        scratch_types=[pltpu.VMEM((SC_CHUNK,), jnp.int32),
                       pltpu.VMEM((SC_CHUNK, w), jnp.int32),
                       pltpu.SemaphoreType.DMA],
    )
    def scatter_rows(rows_hbm, dest_hbm, out_hbm, idx_v, rows_v, sem):
        base = (lax.axis_index("s") * nc + lax.axis_index("c")) * per_w

        @pl.loop(0, per_w // SC_CHUNK)
        def _(j):
            off = pl.multiple_of(base + j * SC_CHUNK, SC_CHUNK)
            pltpu.sync_copy(rows_hbm.at[pl.ds(off, SC_CHUNK)], rows_v)
            pltpu.sync_copy(dest_hbm.at[pl.ds(off, SC_CHUNK)], idx_v)
            pltpu.async_copy(rows_v, out_hbm.at[idx_v], sem).wait()

    if out_ref is not None:
        scatter_rows(rows, dest, out_ref)
        return None
    return scatter_rows(rows, dest)


def _sc_scatter_ids(dest, out_rows):
    n = dest.shape[0]
    info = plsc.get_sparse_core_info()
    nc, ns, nl = info.num_cores, info.num_subcores, info.num_lanes
    per_w = n // (nc * ns)
    chunk = SC_ID_CHUNK
    assert per_w % (2 * chunk) == 0
    mesh = plsc.VectorSubcoreMesh(core_axis_name="c", subcore_axis_name="s")
    idx_t = pltpu.VMEM((chunk,), jnp.int32)
    rows_t = pltpu.VMEM((chunk, LANES), jnp.int32)

    @functools.partial(
        pl.kernel, mesh=mesh,
        out_type=jax.ShapeDtypeStruct((out_rows, LANES), jnp.int32),
        scratch_types=[idx_t, idx_t, rows_t, rows_t, pltpu.SemaphoreType.DMA, pltpu.SemaphoreType.DMA],
    )
    def scatter_ids(dest_hbm, out_hbm, idx_a, idx_b, rows_a, rows_b, sem_a, sem_b):
        base = (lax.axis_index("s") * nc + lax.axis_index("c")) * per_w

        def start(off, idx_v, rows_v, sem):
            for r in range(chunk):
                rows_v[r, pl.ds(0, nl)] = jnp.zeros((nl,), jnp.int32) + (off + r)
            pltpu.sync_copy(dest_hbm.at[pl.ds(off, chunk)], idx_v)
            return pltpu.async_copy(rows_v, out_hbm.at[idx_v], sem)

        @pl.loop(0, per_w // (2 * chunk))
        def _(j):
            off = pl.multiple_of(base + j * (2 * chunk), 2 * chunk)
            copy_a = start(off, idx_a, rows_a, sem_a)
            copy_b = start(off + chunk, idx_b, rows_b, sem_b)
            copy_a.wait()
            copy_b.wait()

    return scatter_ids(dest)


def _sc_gather(table, idx):
    n = idx.shape[0]
    w = table.shape[1]
    nc, ns = _sc_workers()
    per_w = n // (nc * ns)
    assert per_w % SC_CHUNK == 0
    mesh = plsc.VectorSubcoreMesh(core_axis_name="c", subcore_axis_name="s")

    @functools.partial(
        pl.kernel, mesh=mesh,
        out_type=jax.ShapeDtypeStruct((n, w), jnp.int32),
        scratch_types=[pltpu.VMEM((SC_CHUNK,), jnp.int32),
                       pltpu.VMEM((SC_CHUNK, w), jnp.int32),
                       pltpu.SemaphoreType.DMA],
    )
    def gather_rows(table_hbm, idx_hbm, out_hbm, idx_v, rows_v, sem):
        base = (lax.axis_index("s") * nc + lax.axis_index("c")) * per_w

        @pl.loop(0, per_w // SC_CHUNK)
        def _(j):
            off = pl.multiple_of(base + j * SC_CHUNK, SC_CHUNK)
            pltpu.sync_copy(idx_hbm.at[pl.ds(off, SC_CHUNK)], idx_v)
            pltpu.async_copy(table_hbm.at[idx_v], rows_v, sem).wait()
            pltpu.sync_copy(rows_v, out_hbm.at[pl.ds(off, SC_CHUNK)])

    return gather_rows(table, idx)


def _combine_kernel(x1_ref, y_ref, gate_ref, gtf_ref, g_ref, o_ref):
    h = y_ref.shape[3]
    gates = gate_ref[0]
    moe_a = moe_b = None
    for k in range(TOP_K):
        ya, yb = _unpack_halves(y_ref[k, 0])
        gk = gates[:, k:k + 1]
        moe_a = gk * ya if k == 0 else moe_a + gk * ya
        moe_b = gk * yb if k == 0 else moe_b + gk * yb
    za = x1_ref[0, :, :h] + gtf_ref[0, :, :h] * moe_a
    zb = x1_ref[0, :, h:] + gtf_ref[0, :, h:] * moe_b
    ms = (jnp.sum(za * za, axis=-1, keepdims=True) + jnp.sum(zb * zb, axis=-1, keepdims=True)) / (2 * h)
    inv = lax.rsqrt(ms + EPS)
    o_ref[0, :, :h] = za * inv * g_ref[:, :h]
    o_ref[0, :, h:] = zb * inv * g_ref[:, h:]


def _combine(x1, yk, gates_t, gt_f, g_final, tm):
    b, s, d = x1.shape
    h = yk.shape[3]
    return pl.pallas_call(
        _combine_kernel,
        grid=(b, s // tm),
        in_specs=[pl.BlockSpec((1, tm, d), lambda i, t: (i, t, 0)),
                  pl.BlockSpec((TOP_K, 1, tm, h), lambda i, t: (0, i, t, 0)),
                  pl.BlockSpec((1, tm, TOP_K), lambda i, t: (i, t, 0)),
                  pl.BlockSpec((1, 1, d), lambda i, t: (i, 0, 0)),
                  pl.BlockSpec((1, d), lambda i, t: (0, 0))],
        out_specs=pl.BlockSpec((1, tm, d), lambda i, t: (i, t, 0)),
        out_shape=jax.ShapeDtypeStruct((b, s, d), F32),
        compiler_params=_cparams(("parallel", "arbitrary")),
        name="combine",
    )(x1, yk, gates_t, gt_f, g_final.reshape(1, d))


def _dft_tables(rows, gd):
    seq = rows * GRID_W
    n = np.arange(rows)
    ang1 = 2.0 * np.pi * np.outer(n, n) / rows
    d2 = np.concatenate([np.cos(ang1), -np.sin(ang1)], axis=0)
    k1 = np.arange(rows)[:, None, None]
    k2 = np.arange(GRID_W)[None, :, None]
    n2 = np.arange(GRID_W)[None, None, :]
    ang2 = 2.0 * np.pi * ((n2 * (k1 + rows * k2)) % seq) / seq
    ec, es = np.cos(ang2), np.sin(ang2)
    etab = np.concatenate([np.concatenate([ec, es], axis=2),
                           np.concatenate([-es, ec], axis=2)], axis=1)
    c = np.arange(gd)
    angc = 2.0 * np.pi * np.outer(c, c) / gd
    scale = 1.0 / np.sqrt(seq * gd)
    return (jnp.asarray(d2, BF16), jnp.asarray(etab, BF16),
            jnp.asarray(np.cos(angc) * scale, F32), jnp.asarray(np.sin(angc) * scale, F32))


def _block_diag(w):
    h, i, o = w.shape
    eye = jnp.eye(h, dtype=w.dtype)
    return (eye[:, None, :, None] * w[:, :, None, :]).reshape(h * i, h * o)


def kernel(x, c, ctx, c_ctx, w_mod, b_mod, g_norm_mix, g_norm_ffn, w_in, w_fourier, conv_w, conv_b,
           rg_w_a, rg_b_a, rg_w_x, rg_b_x, rg_lam, g_out_fourier, g_out_rg, w_out, w_router, b_router,
           w_gate_up, b_gate_up, w_down, b_down, g_final):
    assert w_mod.shape[0] == 1, "single-layer stack only"
    b, s, d = x.shape
    df = w_fourier.shape[1] * w_fourier.shape[2]
    dr = conv_w.shape[2]
    gd = w_fourier.shape[2]
    rows = s // GRID_W
    t = b * s
    ne = w_router.shape[2]

    mrows = -(-(b + 1) // SUBLANES) * SUBLANES
    cond = jnp.zeros((mrows, d), F32).at[:b].set(c).at[b].set(c_ctx)
    mod = _adaln(cond, w_mod[0], b_mod[0])
    sh_m, sc_m, gt_m, sh_f, sc_f, gt_f = [mod[:b, k * d:(k + 1) * d].reshape(b, 1, d) for k in range(N_MOD)]
    csh_m = mod[b:b + 1, 0:d].reshape(1, 1, d)
    csc_m = mod[b:b + 1, d:2 * d].reshape(1, 1, d)

    tm = min(s, TOKEN_TILE)
    d2, etab, cmat, smat = _dft_tables(rows, gd)
    jmat = jnp.asarray(np.eye(GRID_W)[::-1].copy(), BF16)

    w_in_bf = w_in[0].astype(BF16)
    f, xs, gg = _stage_b(x, sh_m, sc_m, g_norm_mix[0], w_in_bf, jmat, df, dr, tm=tm)
    xr_ctx = _stage_b_ctx(ctx, csh_m, csc_m, g_norm_mix[0], w_in_bf[:, df:df + dr])

    wcat = jnp.stack([jnp.concatenate([_block_diag(rg_w_a[0, dd]), _block_diag(rg_w_x[0, dd])], axis=1)
                      for dd in range(2)]).astype(BF16)
    bcat = jnp.concatenate([rg_b_a[0], rg_b_x[0]], axis=1).reshape(2, 1, 2 * dr)
    lam = rg_lam[0].reshape(2, 1, dr)
    h0 = jnp.zeros((b, SUBLANES, dr), F32)
    _, hfin_ctx = _rg_scan(xr_ctx, h0, conv_w[0], conv_b[0], wcat, bcat, lam, tc=ctx.shape[1])
    hs, _ = _rg_scan(xs, hfin_ctx, conv_w[0], conv_b[0], wcat, bcat, lam, tc=min(s, SCAN_CHUNK))

    cw, sw = _fold_fourier(cmat, smat, w_fourier[0])
    bdc = _block_diag(cw).astype(BF16)
    bds = _block_diag(sw).astype(BF16)
    y = _fourier_stage1(f, d2, tl=min(GRID_W * df, 8192))
    fn = _fourier_stage2(y.reshape(b, 2, rows, GRID_W, df), etab, bdc, bds, g_out_fourier[0],
                         kb=min(rows, 16))
    fn = fn.reshape(b, s, df)

    x1, h2, idx, gates = _stage_m(fn, hs, gg, x, gt_m, sh_f, sc_f, g_out_rg[0], g_norm_ffn[0],
                                  w_out[0].astype(BF16), w_router[0].T, b_router[0], jmat, tm=tm)

    tl = min(t, RANK_TILE)
    tri = jnp.asarray(np.triu(np.ones((tl, tl))), BF16)
    rank, cnt = _dispatch_ranks(idx, tri, min(t, RANK_STEP))
    counts = cnt[:, 0]
    tmm = MOE_ROW_TILE
    padded = (counts + tmm - 1) // tmm * tmm
    pad_end = jnp.cumsum(padded)
    pad_start = pad_end - padded
    eids = jnp.arange(ne, dtype=jnp.int32)
    dest = rank + jnp.sum(jnp.where(idx[:, :, None] == eids, pad_start, 0), axis=-1)
    n_blocks = -(-(t * TOP_K) // tmm) + ne
    cap = n_blocks * tmm
    n_used = (pad_end[-1] // tmm).astype(jnp.int32).reshape(1)
    blk_start = jnp.arange(n_blocks, dtype=jnp.int32) * tmm
    blk_expert = jnp.sum(blk_start[:, None] >= pad_end[None, :], axis=1).astype(jnp.int32)
    last_expert = jnp.sum(pad_end[-1] - tmm >= pad_end).astype(jnp.int32)
    blk_expert = jnp.minimum(blk_expert, last_expert)
    sel = blk_expert[:, None] == eids
    blk_first = jnp.sum(jnp.where(sel, pad_start, 0), axis=1)
    blk_count = jnp.sum(jnp.where(sel, counts, 0), axis=1)
    blk_valid = jnp.clip(blk_count - (blk_start - blk_first), 0, tmm).astype(jnp.int32)

    na = TOP_K * t
    inv = _sc_scatter_ids(dest.reshape(-1), cap)[:, 0].reshape(n_blocks, tmm)
    live = jnp.arange(tmm, dtype=jnp.int32)[None, :] < blk_valid[:, None]
    spread = jnp.arange(cap, dtype=jnp.int32).reshape(n_blocks, tmm) % t
    src_tok = jnp.where(live, inv % t, spread).reshape(-1)
    dst_row = jnp.where(live, inv, na + spread).reshape(-1)

    h2_rows = h2.reshape(t, d // 2)
    y_all = jax.empty_ref(jax.ShapeDtypeStruct(((TOP_K + 1) * t, d // 2), jnp.int32))
    unit = n_blocks // sum(MOE_PIPE)
    assert unit * sum(MOE_PIPE) == n_blocks
    blk0 = 0
    for parts in MOE_PIPE:
        nq = parts * unit
        blocks = slice(blk0, blk0 + nq)
        rows = slice(blk0 * tmm, (blk0 + nq) * tmm)
        x_q = _sc_gather(h2_rows, src_tok[rows])
        nu_q = jnp.clip(n_used - blk0, 0, nq).astype(jnp.int32)
        blk0 += nq
        y_q = _moe_experts(blk_expert[blocks], blk_valid[blocks], nu_q, x_q,
                           w_gate_up[0], b_gate_up[0], w_down[0], b_down[0], tmm)
        _sc_scatter(y_q, dst_row[rows], out_ref=y_all)
    yk = y_all[...].reshape(TOP_K + 1, b, s, d // 2)
    return _combine(x1, yk, gates.T.reshape(b, s, TOP_K), gt_f, g_final, tm)
```

```python
import functools

import numpy as np
import jax
import jax.numpy as jnp
from jax import lax
from jax.experimental import pallas as pl
from jax.experimental.pallas import tpu as pltpu
from jax.experimental.pallas import tpu_sc as plsc

GRID_W = 64
FOURIER_GROUPS = 4
RG_HEADS = 8
CONV_W = 4
CONV_PAD_LO = 2
RG_C = 8.0
N_EXPERTS = 32
TOP_K = 4
SWIGLU_LIMIT = 7.0
SWIGLU_ALPHA = 1.702
N_MOD = 6
EPS = 1e-6

LANES = 128
SUBLANES = 8
VMEM_LIMIT_BYTES = 56 * 1024 * 1024
TOKEN_TILE = 1024
RANK_TILE = 512
RANK_STEP = 2048
SCAN_CHUNK = 512
MOE_ROW_TILE = 512
MOE_FF_CHUNK = 512
MOE_PIPE = (1, 3, 3, 1)

F32 = jnp.float32
BF16 = jnp.bfloat16


def _cparams(sem):
    return pltpu.CompilerParams(dimension_semantics=sem, vmem_limit_bytes=VMEM_LIMIT_BYTES)


def _split_bf16(a):
    hi = a.astype(BF16)
    lo = (a - hi.astype(F32)).astype(BF16)
    return hi, lo


def _dot3(a, b):
    ah, al = _split_bf16(a)
    bh, bl = _split_bf16(b)
    out = jnp.dot(ah, bh, preferred_element_type=F32)
    out += jnp.dot(ah, bl, preferred_element_type=F32)
    out += jnp.dot(al, bh, preferred_element_type=F32)
    return out


def _dot3_nt(a, b):
    dn = (((1,), (1,)), ((), ()))
    ah, al = _split_bf16(a)
    bh, bl = _split_bf16(b)
    out = lax.dot_general(ah, bh, dn, preferred_element_type=F32)
    out += lax.dot_general(ah, bl, dn, preferred_element_type=F32)
    out += lax.dot_general(al, bh, dn, preferred_element_type=F32)
    return out


def _gelu_tanh(x):
    return 0.5 * x * (1.0 + jnp.tanh(0.7978845608028654 * (x + 0.044715 * (x * x * x))))


def _rms(x, g):
    return x * lax.rsqrt(jnp.mean(x * x, axis=-1, keepdims=True) + EPS) * g


def _pack_halves(v):
    h = v.shape[1] // 2
    hi = lax.bitcast_convert_type(v[:, :h].astype(BF16).astype(F32), jnp.uint32)
    lo = lax.bitcast_convert_type(v[:, h:].astype(BF16).astype(F32), jnp.uint32)
    return lax.bitcast_convert_type(hi | (lo >> 16), jnp.int32)


def _unpack_halves(w):
    u = lax.bitcast_convert_type(w, jnp.uint32)
    hi = lax.bitcast_convert_type(u & jnp.uint32(0xFFFF0000), F32)
    lo = lax.bitcast_convert_type(u << 16, F32)
    return hi, lo


def _adaln_kernel(c_ref, w_ref, b_ref, o_ref):
    s = c_ref[...]
    s = s * jax.nn.sigmoid(s)
    o_ref[...] = _dot3(s, w_ref[...]) + b_ref[...]


def _adaln(cond, w_mod, b_mod):
    m, d = cond.shape
    n = w_mod.shape[1]
    tn = n // N_MOD
    return pl.pallas_call(
        _adaln_kernel,
        grid=(n // tn,),
        in_specs=[pl.BlockSpec((m, d), lambda i: (0, 0)),
                  pl.BlockSpec((d, tn), lambda i: (0, i)),
                  pl.BlockSpec((1, tn), lambda i: (0, i))],
        out_specs=pl.BlockSpec((m, tn), lambda i: (0, i)),
        out_shape=jax.ShapeDtypeStruct((m, n), F32),
        compiler_params=_cparams(("arbitrary",)),
        name="adaln",
    )(cond, w_mod, b_mod.reshape(1, n))


def _fold_kernel(c_ref, s_ref, w_ref, cw_ref, sw_ref):
    w = w_ref[0]
    cw_ref[0] = _dot3(c_ref[...], w)
    sw_ref[0] = _dot3(s_ref[...], w)


def _fold_fourier(cmat, smat, w_f):
    g, gd, _ = w_f.shape
    spec_m = pl.BlockSpec((gd, gd), lambda i: (0, 0))
    spec_w = pl.BlockSpec((1, gd, gd), lambda i: (i, 0, 0))
    return pl.pallas_call(
        _fold_kernel,
        grid=(g,),
        in_specs=[spec_m, spec_m, spec_w],
        out_specs=[spec_w, spec_w],
        out_shape=[jax.ShapeDtypeStruct((g, gd, gd), F32)] * 2,
        compiler_params=_cparams(("arbitrary",)),
        name="fold_fourier",
    )(cmat, smat, w_f)


def _seg_pitch(seg_len):
    n8 = seg_len // SUBLANES
    return SUBLANES * (n8 + 1 - n8 % 2)


def _pitched_store(v, buf, blk0):
    pitch = _seg_pitch(GRID_W)
    for r in range(v.shape[0] // GRID_W):
        for c in range(v.shape[1] // LANES):
            buf[c, (blk0 + r) * pitch:(blk0 + r) * pitch + GRID_W, :] = (
                v[r * GRID_W:(r + 1) * GRID_W, c * LANES:(c + 1) * LANES])


def _pitched_gather(buf, nb, store):
    pitch = _seg_pitch(GRID_W)
    for pos in range(GRID_W):
        for c in range(buf.shape[0]):
            store(pos, c, buf[c, pl.ds(pos, nb, stride=pitch), :])


def _stage_b_kernel(x_ref, sh_ref, sc_ref, g_ref, w_ref, j_ref, f_ref, xs_ref, gg_ref, fbuf, *, df, dr):
    tm = x_ref.shape[1]
    h = _rms(x_ref[0], g_ref[...]) * (1.0 + sc_ref[0]) + sh_ref[0]
    hb = h.astype(BF16)
    _pitched_store(jnp.dot(hb, w_ref[:, :df], preferred_element_type=F32), fbuf, 0)

    def store_f(pos, c, tile):
        f_ref[0, :, pos * df + c * LANES:pos * df + (c + 1) * LANES] = tile.astype(BF16)
    _pitched_gather(fbuf, tm // GRID_W, store_f)
    xr = jnp.dot(hb, w_ref[:, df:df + dr], preferred_element_type=F32).astype(BF16)
    for r in range(tm // GRID_W):
        blk = xr[r * GRID_W:(r + 1) * GRID_W]
        if r % 2 == 1:
            blk = jnp.dot(j_ref[...], blk, preferred_element_type=F32).astype(BF16)
        xs_ref[0, r * GRID_W:(r + 1) * GRID_W, :] = blk
    gr = jnp.dot(hb, w_ref[:, df + dr:], preferred_element_type=F32)
    gg_ref[0] = _gelu_tanh(gr).astype(BF16)


def _stage_b(x, shift, scale, g, w_in_bf, jmat, df, dr, tm):
    b, s, d = x.shape
    n = w_in_bf.shape[1]
    vec = pl.BlockSpec((1, 1, d), lambda i, t: (i, 0, 0))
    out = pl.BlockSpec((1, tm, df), lambda i, t: (i, t, 0))
    nb = tm // GRID_W
    tok = jax.ShapeDtypeStruct((b, s, df), BF16)
    return pl.pallas_call(
        functools.partial(_stage_b_kernel, df=df, dr=dr),
        grid=(b, s // tm),
        in_specs=[pl.BlockSpec((1, tm, d), lambda i, t: (i, t, 0)), vec, vec,
                  pl.BlockSpec((1, d), lambda i, t: (0, 0)),
                  pl.BlockSpec((d, n), lambda i, t: (0, 0)),
                  pl.BlockSpec((GRID_W, GRID_W), lambda i, t: (0, 0))],
        out_specs=[pl.BlockSpec((1, nb, GRID_W * df), lambda i, t: (i, t, 0)), out, out],
        out_shape=[jax.ShapeDtypeStruct((b, s // GRID_W, GRID_W * df), BF16), tok, tok],
        scratch_shapes=[pltpu.VMEM((df // LANES, nb * _seg_pitch(GRID_W), LANES), F32)],
        compiler_params=_cparams(("parallel", "arbitrary")),
        name="stage_b",
    )(x, shift, scale, g.reshape(1, d), w_in_bf, jmat)


def _stage_b_ctx_kernel(x_ref, sh_ref, sc_ref, g_ref, w_ref, xr_ref):
    h = _rms(x_ref[0], g_ref[...]) * (1.0 + sc_ref[0]) + sh_ref[0]
    xr_ref[0] = jnp.dot(h.astype(BF16), w_ref[...], preferred_element_type=F32).astype(BF16)


def _stage_b_ctx(ctx, shift, scale, g, w_xr_bf):
    b, s, d = ctx.shape
    dr = w_xr_bf.shape[1]
    vec = pl.BlockSpec((1, 1, d), lambda i: (0, 0, 0))
    return pl.pallas_call(
        _stage_b_ctx_kernel,
        grid=(b,),
        in_specs=[pl.BlockSpec((1, s, d), lambda i: (i, 0, 0)), vec, vec,
                  pl.BlockSpec((1, d), lambda i: (0, 0)),
                  pl.BlockSpec((d, dr), lambda i: (0, 0))],
        out_specs=pl.BlockSpec((1, s, dr), lambda i: (i, 0, 0)),
        out_shape=jax.ShapeDtypeStruct((b, s, dr), BF16),
        compiler_params=_cparams(("arbitrary",)),
        name="stage_b_ctx",
    )(ctx, shift, scale, g.reshape(1, d), w_xr_bf)


HALO = 16


def _sigmoid(x):
    return 0.5 * jnp.tanh(0.5 * x) + 0.5


def _rg_kernel(xs_ref, h0_ref, cw_ref, cb_ref, w_ref, b_ref, lam_ref, pm_ref, pmt_ref, out_ref, hfin_ref,
               hf_s, xc_s, a_s, u_s, hl_s, p_s, c_s, hc_s, *, tc, nchunk, seq, dr):
    p = pl.program_id(1)
    j = pl.program_id(2)
    cidx = jnp.where(p == 0, j, nchunk - 1 - j)
    start = pl.multiple_of(cidx * tc, tc)
    nseg = SUBLANES
    sl = tc // nseg
    sub = lax.broadcasted_iota(jnp.int32, (nseg, dr), 0)

    @pl.when(p == 0)
    def _():
        xp = jnp.dot(pm_ref[...], xs_ref[0, pl.ds(start, tc), :], preferred_element_type=F32)
        pstart = pl.multiple_of(jnp.maximum(start - HALO, 0), HALO)
        nstart = pl.multiple_of(jnp.minimum(start + tc, seq - HALO), HALO)
        prev = xs_ref[0, pl.ds(pstart, HALO), :].astype(F32)
        nxt = xs_ref[0, pl.ds(nstart, HALO), :].astype(F32)
        prev = jnp.where(cidx > 0, prev, 0.0)
        nxt = jnp.where(cidx < nchunk - 1, nxt, 0.0)
        tm2 = jnp.where(sub == 0, prev[HALO - 2:HALO - 1], pltpu.roll(xp[(sl - 2) * nseg:(sl - 1) * nseg], 1, 0))
        tm1 = jnp.where(sub == 0, prev[HALO - 1:HALO], pltpu.roll(xp[(sl - 1) * nseg:sl * nseg], 1, 0))
        tp1 = jnp.where(sub == nseg - 1, nxt[0:1], pltpu.roll(xp[0:nseg], nseg - 1, 0))
        ext = jnp.concatenate([tm2, tm1, xp, tp1], axis=0)
        xc = cb_ref[...] + cw_ref[0:1, :] * ext[0:tc]
        for k in range(1, CONV_W):
            xc = xc + cw_ref[k:k + 1, :] * ext[k * nseg:k * nseg + tc]
        xc_s[pl.ds(start, tc), :] = xc

    xc = xc_s[pl.ds(start, tc), :]
    gates = jnp.dot(xc.astype(BF16), w_ref[p], preferred_element_type=F32) + b_ref[p]
    i = _sigmoid(gates[:, dr:])
    half_c = (-0.5 * RG_C) * jax.nn.softplus(-lam_ref[p])
    log_a = half_c * jnp.tanh(0.5 * gates[:, :dr]) + half_c
    a = jnp.exp(log_a)
    a_s[...] = a
    w = -jnp.tanh(log_a) * (1.0 + a * a)
    u_s[...] = jnp.where(w > 0.0, w * lax.rsqrt(w), 0.0) * (i * xc)

    @pl.when(jnp.logical_and(p == 0, j == 0))
    def _():
        hfin_ref[...] = jnp.zeros_like(hfin_ref)

    @pl.when(j == 0)
    def _():
        hc_s[0:1, :] = h0_ref[0, pl.ds(p, 1), :]

    def segment_scan(reverse):
        def body(q, carry):
            t = (sl - 1 - q) if reverse else q
            rows = pl.ds(pl.multiple_of(t * nseg, nseg), nseg)
            h, pr = carry
            av = a_s[rows, :]
            h = av * h + u_s[rows, :]
            pr = av * pr
            hl_s[rows, :] = h
            p_s[rows, :] = pr
            return h, pr
        h_end, p_end = lax.fori_loop(0, sl, body, (jnp.zeros((nseg, dr), F32), jnp.ones((nseg, dr), F32)),
                                     unroll=4)
        carry = hc_s[0:1, :]
        for g in (range(nseg - 1, -1, -1) if reverse else range(nseg)):
            c_s[g:g + 1, :] = carry
            carry = h_end[g:g + 1, :] + p_end[g:g + 1, :] * carry
        hc_s[0:1, :] = carry
        return c_s[...]

    def corrected(cin):
        h = hl_s[...].reshape(sl, nseg, dr) + p_s[...].reshape(sl, nseg, dr) * cin[None]
        return h.reshape(tc, dr)

    @pl.when(p == 0)
    def _():
        hf_s[pl.ds(start, tc), :] = corrected(segment_scan(False))

    @pl.when(p == 1)
    def _():
        tot = corrected(segment_scan(True)) + hf_s[pl.ds(start, tc), :]
        out_ref[0] = jnp.dot(pmt_ref[...], tot.astype(BF16), preferred_element_type=F32).astype(BF16)

    @pl.when(j == nchunk - 1)
    def _():
        hfin_ref[0, pl.ds(p, 1), :] = hc_s[0:1, :]


def _rg_scan(xs, h0, conv_w, conv_b, wcat, bcat, lam, tc):
    b, s, dr = xs.shape
    nchunk = s // tc
    last = nchunk - 1
    sl = tc // SUBLANES
    src = (np.arange(tc) % SUBLANES) * sl + np.arange(tc) // SUBLANES
    pm = np.zeros((tc, tc), np.float32)
    pm[np.arange(tc), src] = 1.0
    chunk_buf = pltpu.VMEM((tc, dr), F32)
    full2 = lambda shape: pl.BlockSpec(shape, lambda i, p, j: (0,) * len(shape))
    return pl.pallas_call(
        functools.partial(_rg_kernel, tc=tc, nchunk=nchunk, seq=s, dr=dr),
        grid=(b, 2, nchunk),
        in_specs=[pl.BlockSpec((1, s, dr), lambda i, p, j: (i, 0, 0)),
                  pl.BlockSpec((1, SUBLANES, dr), lambda i, p, j: (i, 0, 0)),
                  full2((CONV_W, dr)), full2((1, dr)),
                  full2((2, dr, 2 * dr)), full2((2, 1, 2 * dr)), full2((2, 1, dr)),
                  full2((tc, tc)), full2((tc, tc))],
        out_specs=[pl.BlockSpec((1, tc, dr), lambda i, p, j: (i, jnp.where(p == 0, last, last - j), 0)),
                   pl.BlockSpec((1, SUBLANES, dr), lambda i, p, j: (i, 0, 0))],
        out_shape=[jax.ShapeDtypeStruct((b, s, dr), BF16),
                   jax.ShapeDtypeStruct((b, SUBLANES, dr), F32)],
        scratch_shapes=[pltpu.VMEM((s, dr), F32), pltpu.VMEM((s, dr), F32),
                        chunk_buf, chunk_buf, chunk_buf, chunk_buf,
                        pltpu.VMEM((SUBLANES, dr), F32), pltpu.VMEM((SUBLANES, dr), F32)],
        compiler_params=_cparams(("arbitrary", "arbitrary", "arbitrary")),
        name="rg_scan",
    )(xs, h0, conv_w, conv_b.reshape(1, dr), wcat, bcat, lam, jnp.asarray(pm, BF16), jnp.asarray(pm.T, BF16))


def _f1_kernel(d_ref, x_ref, y_ref):
    y_ref[0] = jnp.dot(d_ref[...], x_ref[0], preferred_element_type=F32).astype(BF16)


def _fourier_stage1(fv, d2, tl):
    b, r, n = fv.shape
    return pl.pallas_call(
        _f1_kernel,
        grid=(b, n // tl),
        in_specs=[pl.BlockSpec((2 * r, r), lambda i, l: (0, 0)),
                  pl.BlockSpec((1, r, tl), lambda i, l: (i, 0, l))],
        out_specs=pl.BlockSpec((1, 2 * r, tl), lambda i, l: (i, 0, l)),
        out_shape=jax.ShapeDtypeStruct((b, 2 * r, n), BF16),
        compiler_params=_cparams(("parallel", "arbitrary")),
        name="fourier_stage1",
    )(d2, fv)


def _f2_kernel(y_ref, e_ref, wcs_ref, g_ref, o_ref, obuf, *, kb, df):
    zr, zi = [], []
    for q in range(kb):
        yk = jnp.concatenate([y_ref[0, 0, q], y_ref[0, 1, q]], axis=0)
        z = jnp.dot(e_ref[q], yk, preferred_element_type=F32)
        zr.append(z[:GRID_W])
        zi.append(z[GRID_W:])
    zr = jnp.concatenate(zr, axis=0).astype(BF16)
    zi = jnp.concatenate(zi, axis=0).astype(BF16)
    gd = wcs_ref.shape[2]
    o = jnp.concatenate(
        [jnp.dot(jnp.concatenate([zr[:, g * gd:(g + 1) * gd], zi[:, g * gd:(g + 1) * gd]], axis=1), wcs_ref[g],
                 preferred_element_type=F32) for g in range(df // gd)], axis=1)
    on = _rms(o, g_ref[...])

    def store_o(pos, c, tile):
        o_ref[0, pos, :, c * LANES:(c + 1) * LANES] = tile.astype(BF16)
    _pitched_store(on, obuf, 0)
    _pitched_gather(obuf, kb, store_o)


def _fourier_stage2(y5, etab, wcs, g, kb):
    b, _, r, w, df = y5.shape
    return pl.pallas_call(
        functools.partial(_f2_kernel, kb=kb, df=df),
        grid=(b, r // kb),
        in_specs=[pl.BlockSpec((1, 2, kb, w, df), lambda i, k: (i, 0, k, 0, 0)),
                  pl.BlockSpec((kb, 2 * w, 2 * w), lambda i, k: (k, 0, 0)),
                  pl.BlockSpec(wcs.shape, lambda i, k: (0, 0, 0)),
                  pl.BlockSpec((1, df), lambda i, k: (0, 0))],
        out_specs=pl.BlockSpec((1, w, kb, df), lambda i, k: (i, 0, k, 0)),
        out_shape=jax.ShapeDtypeStruct((b, w, r, df), BF16),
        scratch_shapes=[pltpu.VMEM((df // LANES, kb * _seg_pitch(GRID_W), LANES), F32)],
        compiler_params=_cparams(("parallel", "arbitrary")),
        name="fourier_stage2",
    )(y5, etab, wcs, g.reshape(1, df))


def _stage_m_kernel(fn_ref, hs_ref, gg_ref, x_ref, gtm_ref, shf_ref, scf_ref, gr_ref, gffn_ref,
                    wo_ref, wr_ref, br_ref, j_ref, x1_ref, h2_ref, idx_ref, gate_ref, *, df):
    tm = x_ref.shape[1]
    hs = hs_ref[0]
    blocks = []
    for r in range(tm // GRID_W):
        blk = hs[r * GRID_W:(r + 1) * GRID_W]
        if r % 2 == 1:
            blk = jnp.dot(j_ref[...], blk, preferred_element_type=F32)
        blocks.append(blk.astype(F32))
    rg = jnp.concatenate(blocks, axis=0) * gg_ref[0].astype(F32)
    rgn = _rms(rg, gr_ref[...]).astype(BF16)
    mix = jnp.dot(fn_ref[0], wo_ref[:df, :], preferred_element_type=F32)
    mix += jnp.dot(rgn, wo_ref[df:, :], preferred_element_type=F32)
    x1 = x_ref[0] + gtm_ref[0] * mix
    x1_ref[0] = x1
    h2 = _rms(x1, gffn_ref[...]) * (1.0 + scf_ref[0]) + shf_ref[0]
    h2_ref[0] = _pack_halves(h2)

    logits = _dot3_nt(wr_ref[...], h2) + br_ref[...]
    eidx = lax.broadcasted_iota(jnp.int32, logits.shape, 0)
    vals, idxs = [], []
    for _ in range(TOP_K):
        m = jnp.max(logits, axis=0, keepdims=True)
        sel = jnp.min(jnp.where(logits == m, eidx, N_EXPERTS), axis=0, keepdims=True)
        vals.append(m)
        idxs.append(sel)
        logits = jnp.where(eidx == sel, -jnp.inf, logits)
    ex = [jnp.exp(v - vals[0]) for v in vals]
    den = ex[0] + ex[1] + ex[2] + ex[3]
    for k in range(TOP_K):
        gate_ref[k:k + 1, :] = ex[k] / den
        idx_ref[k:k + 1, :] = idxs[k]


def _stage_m(fn, hs, gg, x, gt_m, sh_f, sc_f, g_out_r, g_ffn, w_out_bf, w_router_t, b_router, jmat, tm):
    b, s, d = x.shape
    df = fn.shape[2]
    dr = hs.shape[2]
    nt = s // tm
    ne = w_router_t.shape[0]
    vec = pl.BlockSpec((1, 1, d), lambda i, t: (i, 0, 0))
    half = lambda dd: pl.BlockSpec((1, tm, dd), lambda i, t: (i, t, 0))
    full = lambda shape: pl.BlockSpec(shape, lambda i, t: (0,) * len(shape))
    tok = pl.BlockSpec((TOP_K, tm), lambda i, t: (0, i * nt + t))
    return pl.pallas_call(
        functools.partial(_stage_m_kernel, df=df),
        grid=(b, nt),
        in_specs=[half(df), half(dr), half(dr), half(d), vec, vec, vec,
                  full((1, dr)), full((1, d)), full((d, d)), full((ne, d)), full((ne, 1)),
                  full((GRID_W, GRID_W))],
        out_specs=[half(d), half(d // 2), tok, tok],
        out_shape=[jax.ShapeDtypeStruct((b, s, d), F32), jax.ShapeDtypeStruct((b, s, d // 2), jnp.int32),
                   jax.ShapeDtypeStruct((TOP_K, b * s), jnp.int32),
                   jax.ShapeDtypeStruct((TOP_K, b * s), F32)],
        compiler_params=_cparams(("parallel", "arbitrary")),
        name="stage_m",
    )(fn, hs, gg, x, gt_m, sh_f, sc_f, g_out_r.reshape(1, dr), g_ffn.reshape(1, d), w_out_bf,
      w_router_t, b_router.reshape(ne, 1), jmat)


def _rank_kernel(idx_ref, tri_ref, rank_ref, cnt_ref, carry_s):
    c = pl.program_id(0)

    @pl.when(c == 0)
    def _():
        carry_s[...] = jnp.zeros_like(carry_s)

    l = tri_ref.shape[0]
    eidx = lax.broadcasted_iota(jnp.int32, (N_EXPERTS, l), 0)
    for sub in range(idx_ref.shape[1] // l):
        lanes = slice(sub * l, (sub + 1) * l)
        for k in range(TOP_K):
            onehot = eidx == idx_ref[k:k + 1, lanes]
            oh = jnp.where(onehot, 1.0, 0.0)
            prefix = jnp.dot(oh.astype(BF16), tri_ref[...], preferred_element_type=F32)
            carry = carry_s[:, 0:1]
            rank = jnp.sum(jnp.where(onehot, prefix - 1.0 + carry, 0.0), axis=0, keepdims=True)
            rank_ref[k:k + 1, lanes] = rank.astype(jnp.int32)
            carry_s[...] = carry_s[...] + jnp.sum(oh, axis=1, keepdims=True)
    cnt_ref[...] = carry_s[...].astype(jnp.int32)


def _dispatch_ranks(idx, tri, tl):
    k, t = idx.shape
    return pl.pallas_call(
        _rank_kernel,
        grid=(t // tl,),
        in_specs=[pl.BlockSpec((k, tl), lambda c: (0, c)),
                  pl.BlockSpec(tri.shape, lambda c: (0, 0))],
        out_specs=[pl.BlockSpec((k, tl), lambda c: (0, c)),
                   pl.BlockSpec((N_EXPERTS, LANES), lambda c: (0, 0))],
        out_shape=[jax.ShapeDtypeStruct((k, t), jnp.int32),
                   jax.ShapeDtypeStruct((N_EXPERTS, LANES), jnp.int32)],
        scratch_shapes=[pltpu.VMEM((N_EXPERTS, LANES), F32)],
        compiler_params=_cparams(("arbitrary",)),
        name="dispatch_ranks",
    )(idx, tri)


def _moe_kernel(be_ref, bv_ref, nu_ref, x_ref, wgu_ref, bgu_ref, wd_ref, bd_ref, o_ref, wgu_s, wd_s, *, dff):
    i = pl.program_id(0)
    h = x_ref.shape[1]

    @pl.when(jnp.logical_or(i == 0, be_ref[i] != be_ref[jnp.maximum(i - 1, 0)]))
    def _():
        wgu_s[...] = wgu_ref[0].astype(BF16)
        wd_s[...] = wd_ref[0].astype(BF16)

    @pl.when(i < nu_ref[0])
    def _():
        rows = lax.broadcasted_iota(jnp.int32, x_ref.shape, 0)
        xw = jnp.where(rows < bv_ref[i], x_ref[...], 0)
        xa, xb = _unpack_halves(xw)
        xa = xa.astype(BF16)
        xb = xb.astype(BF16)
        acc = None
        for c in range(dff // MOE_FF_CHUNK):
            gs = slice(c * MOE_FF_CHUNK, (c + 1) * MOE_FF_CHUNK)
            us = slice(dff + c * MOE_FF_CHUNK, dff + (c + 1) * MOE_FF_CHUNK)
            g = jnp.dot(xa, wgu_s[:h, gs], preferred_element_type=F32)
            g += jnp.dot(xb, wgu_s[h:, gs], preferred_element_type=F32)
            u = jnp.dot(xa, wgu_s[:h, us], preferred_element_type=F32)
            u += jnp.dot(xb, wgu_s[h:, us], preferred_element_type=F32)
            gt = jnp.minimum(g + bgu_ref[0, :, gs], SWIGLU_LIMIT)
            up = jnp.clip(u + bgu_ref[0, :, us], -SWIGLU_LIMIT, SWIGLU_LIMIT)
            act = (up + 1.0) * (gt * _sigmoid(SWIGLU_ALPHA * gt))
            part = jnp.dot(act.astype(BF16), wd_s[gs, :], preferred_element_type=F32)
            acc = part if acc is None else acc + part
        o_ref[...] = _pack_halves(acc + bd_ref[0])


def _moe_experts(blk_expert, blk_valid, n_used, xs, wgu, bgu, wd, bd, tmm):
    cap, h = xs.shape
    ne, d, dff2 = wgu.shape
    dff = dff2 // 2
    row_blk = lambda i, be, bv, nu: (jnp.maximum(jnp.minimum(i, nu[0] - 1), 0), 0)
    wsel = lambda i, be, bv, nu: (be[i], 0, 0)
    grid_spec = pltpu.PrefetchScalarGridSpec(
        num_scalar_prefetch=3,
        grid=(cap // tmm,),
        in_specs=[pl.BlockSpec((tmm, h), row_blk),
                  pl.BlockSpec((1, d, dff2), wsel),
                  pl.BlockSpec((1, 1, dff2), wsel),
                  pl.BlockSpec((1, dff, d), wsel),
                  pl.BlockSpec((1, 1, d), wsel)],
        out_specs=pl.BlockSpec((tmm, h), row_blk),
        scratch_shapes=[pltpu.VMEM((d, dff2), BF16), pltpu.VMEM((dff, d), BF16)],
    )
    return pl.pallas_call(
        functools.partial(_moe_kernel, dff=dff),
        grid_spec=grid_spec,
        out_shape=jax.ShapeDtypeStruct((cap, h), jnp.int32),
        compiler_params=_cparams(("arbitrary",)),
        name="moe_experts",
    )(blk_expert, blk_valid, n_used, xs, wgu, bgu.reshape(ne, 1, dff2), wd, bd.reshape(ne, 1, d))


SC_CHUNK = 64
SC_ID_CHUNK = 128


def _sc_workers():
    info = plsc.get_sparse_core_info()
    return info.num_cores, info.num_subcores


def _sc_scatter(rows, dest, out_rows=None, out_ref=None):
    n, w = rows.shape
    nc, ns = _sc_workers()
    per_w = n // (nc * ns)
    assert per_w % SC_CHUNK == 0
    mesh = plsc.VectorSubcoreMesh(core_axis_name="c", subcore_axis_name="s")
    out_type = () if out_ref is not None else jax.ShapeDtypeStruct((out_rows, w), jnp.int32)

    @functools.partial(
        pl.kernel, mesh=mesh, out_type=out_type,
        scratch_types=[pltpu.VMEM((SC_CHUNK,), jnp.int32),
                       pltpu.VMEM((SC_CHUNK, w), jnp.int32),
                       pltpu.SemaphoreType.DMA],
    )
    def scatter_rows(rows_hbm, dest_hbm, out_hbm, idx_v, rows_v, sem):
        base = (lax.axis_index("s") * nc + lax.axis_index("c")) * per_w

        @pl.loop(0, per_w // SC_CHUNK)
        def _(j):
            off = pl.multiple_of(base + j * SC_CHUNK, SC_CHUNK)
            pltpu.sync_copy(rows_hbm.at[pl.ds(off, SC_CHUNK)], rows_v)
            pltpu.sync_copy(dest_hbm.at[pl.ds(off, SC_CHUNK)], idx_v)
            pltpu.async_copy(rows_v, out_hbm.at[idx_v], sem).wait()

    if out_ref is not None:
        scatter_rows(rows, dest, out_ref)
        return None
    return scatter_rows(rows, dest)


def _sc_scatter_ids(dest, out_rows):
    n = dest.shape[0]
    info = plsc.get_sparse_core_info()
    nc, ns, nl = info.num_cores, info.num_subcores, info.num_lanes
    per_w = n // (nc * ns)
    chunk = SC_ID_CHUNK
    assert per_w % (2 * chunk) == 0
    mesh = plsc.VectorSubcoreMesh(core_axis_name="c", subcore_axis_name="s")
    idx_t = pltpu.VMEM((chunk,), jnp.int32)
    rows_t = pltpu.VMEM((chunk, LANES), jnp.int32)

    @functools.partial(
        pl.kernel, mesh=mesh,
        out_type=jax.ShapeDtypeStruct((out_rows, LANES), jnp.int32),
        scratch_types=[idx_t, idx_t, rows_t, rows_t, pltpu.SemaphoreType.DMA, pltpu.SemaphoreType.DMA],
    )
    def scatter_ids(dest_hbm, out_hbm, idx_a, idx_b, rows_a, rows_b, sem_a, sem_b):
        base = (lax.axis_index("s") * nc + lax.axis_index("c")) * per_w

        def start(off, idx_v, rows_v, sem):
            for r in range(chunk):
                rows_v[r, pl.ds(0, nl)] = jnp.zeros((nl,), jnp.int32) + (off + r)
            pltpu.sync_copy(dest_hbm.at[pl.ds(off, chunk)], idx_v)
            return pltpu.async_copy(rows_v, out_hbm.at[idx_v], sem)

        @pl.loop(0, per_w // (2 * chunk))
        def _(j):
            off = pl.multiple_of(base + j * (2 * chunk), 2 * chunk)
            copy_a = start(off, idx_a, rows_a, sem_a)
            copy_b = start(off + chunk, idx_b, rows_b, sem_b)
            copy_a.wait()
            copy_b.wait()

    return scatter_ids(dest)


def _sc_gather(table, idx):
    n = idx.shape[0]
    w = table.shape[1]
    nc, ns = _sc_workers()
    per_w = n // (nc * ns)
    assert per_w % SC_CHUNK == 0
    mesh = plsc.VectorSubcoreMesh(core_axis_name="c", subcore_axis_name="s")

    @functools.partial(
        pl.kernel, mesh=mesh,
        out_type=jax.ShapeDtypeStruct((n, w), jnp.int32),
        scratch_types=[pltpu.VMEM((SC_CHUNK,), jnp.int32),
                       pltpu.VMEM((SC_CHUNK, w), jnp.int32),
                       pltpu.SemaphoreType.DMA],
    )
    def gather_rows(table_hbm, idx_hbm, out_hbm, idx_v, rows_v, sem):
        base = (lax.axis_index("s") * nc + lax.axis_index("c")) * per_w

        @pl.loop(0, per_w // SC_CHUNK)
        def _(j):
            off = pl.multiple_of(base + j * SC_CHUNK, SC_CHUNK)
            pltpu.sync_copy(idx_hbm.at[pl.ds(off, SC_CHUNK)], idx_v)
            pltpu.async_copy(table_hbm.at[idx_v], rows_v, sem).wait()
            pltpu.sync_copy(rows_v, out_hbm.at[pl.ds(off, SC_CHUNK)])

    return gather_rows(table, idx)


def _combine_kernel(x1_ref, y_ref, gate_ref, gtf_ref, g_ref, o_ref):
    h = y_ref.shape[3]
    gates = gate_ref[0]
    moe_a = moe_b = None
    for k in range(TOP_K):
        ya, yb = _unpack_halves(y_ref[k, 0])
        gk = gates[:, k:k + 1]
        moe_a = gk * ya if k == 0 else moe_a + gk * ya
        moe_b = gk * yb if k == 0 else moe_b + gk * yb
    za = x1_ref[0, :, :h] + gtf_ref[0, :, :h] * moe_a
    zb = x1_ref[0, :, h:] + gtf_ref[0, :, h:] * moe_b
    ms = (jnp.sum(za * za, axis=-1, keepdims=True) + jnp.sum(zb * zb, axis=-1, keepdims=True)) / (2 * h)
    inv = lax.rsqrt(ms + EPS)
    o_ref[0, :, :h] = za * inv * g_ref[:, :h]
    o_ref[0, :, h:] = zb * inv * g_ref[:, h:]


def _combine(x1, yk, gates_t, gt_f, g_final, tm):
    b, s, d = x1.shape
    h = yk.shape[3]
    return pl.pallas_call(
        _combine_kernel,
        grid=(b, s // tm),
        in_specs=[pl.BlockSpec((1, tm, d), lambda i, t: (i, t, 0)),
                  pl.BlockSpec((TOP_K, 1, tm, h), lambda i, t: (0, i, t, 0)),
                  pl.BlockSpec((1, tm, TOP_K), lambda i, t: (i, t, 0)),
                  pl.BlockSpec((1, 1, d), lambda i, t: (i, 0, 0)),
                  pl.BlockSpec((1, d), lambda i, t: (0, 0))],
        out_specs=pl.BlockSpec((1, tm, d), lambda i, t: (i, t, 0)),
        out_shape=jax.ShapeDtypeStruct((b, s, d), F32),
        compiler_params=_cparams(("parallel", "arbitrary")),
        name="combine",
    )(x1, yk, gates_t, gt_f, g_final.reshape(1, d))


def _dft_tables(rows, gd):
    seq = rows * GRID_W
    n = np.arange(rows)
    ang1 = 2.0 * np.pi * np.outer(n, n) / rows
    d2 = np.concatenate([np.cos(ang1), -np.sin(ang1)], axis=0)
    k1 = np.arange(rows)[:, None, None]
    k2 = np.arange(GRID_W)[None, :, None]
    n2 = np.arange(GRID_W)[None, None, :]
    ang2 = 2.0 * np.pi * ((n2 * (k1 + rows * k2)) % seq) / seq
    ec, es = np.cos(ang2), np.sin(ang2)
    etab = np.concatenate([np.concatenate([ec, es], axis=2),
                           np.concatenate([-es, ec], axis=2)], axis=1)
    c = np.arange(gd)
    angc = 2.0 * np.pi * np.outer(c, c) / gd
    scale = 1.0 / np.sqrt(seq * gd)
    return (jnp.asarray(d2, BF16), jnp.asarray(etab, BF16),
            jnp.asarray(np.cos(angc) * scale, F32), jnp.asarray(np.sin(angc) * scale, F32))


def _block_diag(w):
    h, i, o = w.shape
    eye = jnp.eye(h, dtype=w.dtype)
    return (eye[:, None, :, None] * w[:, :, None, :]).reshape(h * i, h * o)


def kernel(x, c, ctx, c_ctx, w_mod, b_mod, g_norm_mix, g_norm_ffn, w_in, w_fourier, conv_w, conv_b,
           rg_w_a, rg_b_a, rg_w_x, rg_b_x, rg_lam, g_out_fourier, g_out_rg, w_out, w_router, b_router,
           w_gate_up, b_gate_up, w_down, b_down, g_final):
    assert w_mod.shape[0] == 1, "single-layer stack only"
    b, s, d = x.shape
    df = w_fourier.shape[1] * w_fourier.shape[2]
    dr = conv_w.shape[2]
    gd = w_fourier.shape[2]
    rows = s // GRID_W
    t = b * s
    ne = w_router.shape[2]

    mrows = -(-(b + 1) // SUBLANES) * SUBLANES
    cond = jnp.zeros((mrows, d), F32).at[:b].set(c).at[b].set(c_ctx)
    mod = _adaln(cond, w_mod[0], b_mod[0])
    sh_m, sc_m, gt_m, sh_f, sc_f, gt_f = [mod[:b, k * d:(k + 1) * d].reshape(b, 1, d) for k in range(N_MOD)]
    csh_m = mod[b:b + 1, 0:d].reshape(1, 1, d)
    csc_m = mod[b:b + 1, d:2 * d].reshape(1, 1, d)

    tm = min(s, TOKEN_TILE)
    d2, etab, cmat, smat = _dft_tables(rows, gd)
    jmat = jnp.asarray(np.eye(GRID_W)[::-1].copy(), BF16)

    w_in_bf = w_in[0].astype(BF16)
    f, xs, gg = _stage_b(x, sh_m, sc_m, g_norm_mix[0], w_in_bf, jmat, df, dr, tm=tm)
    xr_ctx = _stage_b_ctx(ctx, csh_m, csc_m, g_norm_mix[0], w_in_bf[:, df:df + dr])

    wcat = jnp.stack([jnp.concatenate([_block_diag(rg_w_a[0, dd]), _block_diag(rg_w_x[0, dd])], axis=1)
                      for dd in range(2)]).astype(BF16)
    bcat = jnp.concatenate([rg_b_a[0], rg_b_x[0]], axis=1).reshape(2, 1, 2 * dr)
    lam = rg_lam[0].reshape(2, 1, dr)
    h0 = jnp.zeros((b, SUBLANES, dr), F32)
    _, hfin_ctx = _rg_scan(xr_ctx, h0, conv_w[0], conv_b[0], wcat, bcat, lam, tc=ctx.shape[1])
    hs, _ = _rg_scan(xs, hfin_ctx, conv_w[0], conv_b[0], wcat, bcat, lam, tc=min(s, SCAN_CHUNK))

    cw, sw = _fold_fourier(cmat, smat, w_fourier[0])
    wcs = jnp.concatenate([cw, sw], axis=1).astype(BF16)
    y = _fourier_stage1(f, d2, tl=min(GRID_W * df, 16384))
    fn = _fourier_stage2(y.reshape(b, 2, rows, GRID_W, df), etab, wcs, g_out_fourier[0],
                         kb=min(rows, 16))
    fn = fn.reshape(b, s, df)

    x1, h2, idx, gates = _stage_m(fn, hs, gg, x, gt_m, sh_f, sc_f, g_out_rg[0], g_norm_ffn[0],
                                  w_out[0].astype(BF16), w_router[0].T, b_router[0], jmat, tm=tm)

    tl = min(t, RANK_TILE)
    tri = jnp.asarray(np.triu(np.ones((tl, tl))), BF16)
    rank, cnt = _dispatch_ranks(idx, tri, min(t, RANK_STEP))
    counts = cnt[:, 0]
    tmm = MOE_ROW_TILE
    padded = (counts + tmm - 1) // tmm * tmm
    pad_end = jnp.cumsum(padded)
    pad_start = pad_end - padded
    eids = jnp.arange(ne, dtype=jnp.int32)
    dest = rank + jnp.sum(jnp.where(idx[:, :, None] == eids, pad_start, 0), axis=-1)
    n_blocks = -(-(t * TOP_K) // tmm) + ne
    cap = n_blocks * tmm
    n_used = (pad_end[-1] // tmm).astype(jnp.int32).reshape(1)
    blk_start = jnp.arange(n_blocks, dtype=jnp.int32) * tmm
    blk_expert = jnp.sum(blk_start[:, None] >= pad_end[None, :], axis=1).astype(jnp.int32)
    last_expert = jnp.sum(pad_end[-1] - tmm >= pad_end).astype(jnp.int32)
    blk_expert = jnp.minimum(blk_expert, last_expert)
    sel = blk_expert[:, None] == eids
    blk_first = jnp.sum(jnp.where(sel, pad_start, 0), axis=1)
    blk_count = jnp.sum(jnp.where(sel, counts, 0), axis=1)
    blk_valid = jnp.clip(blk_count - (blk_start - blk_first), 0, tmm).astype(jnp.int32)

    na = TOP_K * t
    inv = _sc_scatter_ids(dest.reshape(-1), cap)[:, 0].reshape(n_blocks, tmm)
    live = jnp.arange(tmm, dtype=jnp.int32)[None, :] < blk_valid[:, None]
    spread = jnp.arange(cap, dtype=jnp.int32).reshape(n_blocks, tmm) % t
    src_tok = jnp.where(live, inv % t, spread).reshape(-1)
    dst_row = jnp.where(live, inv, na + spread).reshape(-1)

    h2_rows = h2.reshape(t, d // 2)
    y_all = jax.empty_ref(jax.ShapeDtypeStruct(((TOP_K + 1) * t, d // 2), jnp.int32))
    unit = n_blocks // sum(MOE_PIPE)
    assert unit * sum(MOE_PIPE) == n_blocks
    blk0 = 0
    for parts in MOE_PIPE:
        nq = parts * unit
        blocks = slice(blk0, blk0 + nq)
        rows = slice(blk0 * tmm, (blk0 + nq) * tmm)
        x_q = _sc_gather(h2_rows, src_tok[rows])
        nu_q = jnp.clip(n_used - blk0, 0, nq).astype(jnp.int32)
        blk0 += nq
        y_q = _moe_experts(blk_expert[blocks], blk_valid[blocks], nu_q, x_q,
                           w_gate_up[0], b_gate_up[0], w_down[0], b_down[0], tmm)
        _sc_scatter(y_q, dst_row[rows], out_ref=y_all)
    yk = y_all[...].reshape(TOP_K + 1, b, s, d // 2)
    return _combine(x1, yk, gates.T.reshape(b, s, TOP_K), gt_f, g_final, tm)
```

```python
import functools

import numpy as np
import jax
import jax.numpy as jnp
from jax import lax
from jax.experimental import pallas as pl
from jax.experimental.pallas import tpu as pltpu
from jax.experimental.pallas import tpu_sc as plsc

GRID_W = 64
FOURIER_GROUPS = 4
RG_HEADS = 8
CONV_W = 4
CONV_PAD_LO = 2
RG_C = 8.0
N_EXPERTS = 32
TOP_K = 4
SWIGLU_LIMIT = 7.0
SWIGLU_ALPHA = 1.702
N_MOD = 6
EPS = 1e-6

LANES = 128
SUBLANES = 8
VMEM_LIMIT_BYTES = 56 * 1024 * 1024
TOKEN_TILE = 1024
RANK_TILE = 512
RANK_STEP = 2048
SCAN_CHUNK = 512
MOE_ROW_TILE = 512
MOE_FF_CHUNK = 512
MOE_PIPE = (1, 3, 3, 1)

F32 = jnp.float32
BF16 = jnp.bfloat16


def _cparams(sem):
    return pltpu.CompilerParams(dimension_semantics=sem, vmem_limit_bytes=VMEM_LIMIT_BYTES)


def _split_bf16(a):
    hi = a.astype(BF16)
    lo = (a - hi.astype(F32)).astype(BF16)
    return hi, lo


def _dot3(a, b):
    ah, al = _split_bf16(a)
    bh, bl = _split_bf16(b)
    out = jnp.dot(ah, bh, preferred_element_type=F32)
    out += jnp.dot(ah, bl, preferred_element_type=F32)
    out += jnp.dot(al, bh, preferred_element_type=F32)
    return out


def _dot3_nt(a, b):
    dn = (((1,), (1,)), ((), ()))
    ah, al = _split_bf16(a)
    bh, bl = _split_bf16(b)
    out = lax.dot_general(ah, bh, dn, preferred_element_type=F32)
    out += lax.dot_general(ah, bl, dn, preferred_element_type=F32)
    out += lax.dot_general(al, bh, dn, preferred_element_type=F32)
    return out


def _gelu_tanh(x):
    return 0.5 * x * (1.0 + jnp.tanh(0.7978845608028654 * (x + 0.044715 * (x * x * x))))


def _rms(x, g):
    return x * lax.rsqrt(jnp.mean(x * x, axis=-1, keepdims=True) + EPS) * g


def _pack_halves(v):
    h = v.shape[1] // 2
    hi = lax.bitcast_convert_type(v[:, :h].astype(BF16).astype(F32), jnp.uint32)
    lo = lax.bitcast_convert_type(v[:, h:].astype(BF16).astype(F32), jnp.uint32)
    return lax.bitcast_convert_type(hi | (lo >> 16), jnp.int32)


def _unpack_halves(w):
    u = lax.bitcast_convert_type(w, jnp.uint32)
    hi = lax.bitcast_convert_type(u & jnp.uint32(0xFFFF0000), F32)
    lo = lax.bitcast_convert_type(u << 16, F32)
    return hi, lo


def _adaln_kernel(c_ref, w_ref, b_ref, o_ref):
    s = c_ref[...]
    s = s * jax.nn.sigmoid(s)
    o_ref[...] = _dot3(s, w_ref[...]) + b_ref[...]


def _adaln(cond, w_mod, b_mod):
    m, d = cond.shape
    n = w_mod.shape[1]
    tn = n // N_MOD
    return pl.pallas_call(
        _adaln_kernel,
        grid=(n // tn,),
        in_specs=[pl.BlockSpec((m, d), lambda i: (0, 0)),
                  pl.BlockSpec((d, tn), lambda i: (0, i)),
                  pl.BlockSpec((1, tn), lambda i: (0, i))],
        out_specs=pl.BlockSpec((m, tn), lambda i: (0, i)),
        out_shape=jax.ShapeDtypeStruct((m, n), F32),
        compiler_params=_cparams(("arbitrary",)),
        name="adaln",
    )(cond, w_mod, b_mod.reshape(1, n))


def _fold_kernel(c_ref, s_ref, w_ref, cw_ref, sw_ref):
    w = w_ref[0]
    cw_ref[0] = _dot3(c_ref[...], w)
    sw_ref[0] = _dot3(s_ref[...], w)


def _fold_fourier(cmat, smat, w_f):
    g, gd, _ = w_f.shape
    spec_m = pl.BlockSpec((gd, gd), lambda i: (0, 0))
    spec_w = pl.BlockSpec((1, gd, gd), lambda i: (i, 0, 0))
    return pl.pallas_call(
        _fold_kernel,
        grid=(g,),
        in_specs=[spec_m, spec_m, spec_w],
        out_specs=[spec_w, spec_w],
        out_shape=[jax.ShapeDtypeStruct((g, gd, gd), F32)] * 2,
        compiler_params=_cparams(("arbitrary",)),
        name="fold_fourier",
    )(cmat, smat, w_f)


def _seg_pitch(seg_len):
    n8 = seg_len // SUBLANES
    return SUBLANES * (n8 + 1 - n8 % 2)


def _pitched_store(v, buf, blk0):
    pitch = _seg_pitch(GRID_W)
    for r in range(v.shape[0] // GRID_W):
        for c in range(v.shape[1] // LANES):
            buf[c, (blk0 + r) * pitch:(blk0 + r) * pitch + GRID_W, :] = (
                v[r * GRID_W:(r + 1) * GRID_W, c * LANES:(c + 1) * LANES])


def _pitched_gather(buf, nb, store):
    pitch = _seg_pitch(GRID_W)
    for pos in range(GRID_W):
        for c in range(buf.shape[0]):
            store(pos, c, buf[c, pl.ds(pos, nb, stride=pitch), :])


def _stage_b_kernel(x_ref, sh_ref, sc_ref, g_ref, w_ref, j_ref, f_ref, xs_ref, gg_ref, fbuf, *, df, dr):
    tm = x_ref.shape[1]
    h = _rms(x_ref[0], g_ref[...]) * (1.0 + sc_ref[0]) + sh_ref[0]
    hb = h.astype(BF16)
    gr = jnp.dot(hb, w_ref[:, df + dr:], preferred_element_type=F32)
    gg_ref[0] = _gelu_tanh(gr).astype(BF16)
    _pitched_store(jnp.dot(hb, w_ref[:, :df], preferred_element_type=F32), fbuf, 0)

    def store_f(pos, c, tile):
        f_ref[0, :, pos * df + c * LANES:pos * df + (c + 1) * LANES] = tile.astype(BF16)
    _pitched_gather(fbuf, tm // GRID_W, store_f)
    xr = jnp.dot(hb, w_ref[:, df:df + dr], preferred_element_type=F32).astype(BF16)
    for r in range(tm // GRID_W):
        blk = xr[r * GRID_W:(r + 1) * GRID_W]
        if r % 2 == 1:
            blk = jnp.dot(j_ref[...], blk, preferred_element_type=F32).astype(BF16)
        xs_ref[0, r * GRID_W:(r + 1) * GRID_W, :] = blk


def _stage_b(x, shift, scale, g, w_in_bf, jmat, df, dr, tm):
    b, s, d = x.shape
    n = w_in_bf.shape[1]
    vec = pl.BlockSpec((1, 1, d), lambda i, t: (i, 0, 0))
    out = pl.BlockSpec((1, tm, df), lambda i, t: (i, t, 0))
    nb = tm // GRID_W
    tok = jax.ShapeDtypeStruct((b, s, df), BF16)
    return pl.pallas_call(
        functools.partial(_stage_b_kernel, df=df, dr=dr),
        grid=(b, s // tm),
        in_specs=[pl.BlockSpec((1, tm, d), lambda i, t: (i, t, 0)), vec, vec,
                  pl.BlockSpec((1, d), lambda i, t: (0, 0)),
                  pl.BlockSpec((d, n), lambda i, t: (0, 0)),
                  pl.BlockSpec((GRID_W, GRID_W), lambda i, t: (0, 0))],
        out_specs=[pl.BlockSpec((1, nb, GRID_W * df), lambda i, t: (i, t, 0)), out, out],
        out_shape=[jax.ShapeDtypeStruct((b, s // GRID_W, GRID_W * df), BF16), tok, tok],
        scratch_shapes=[pltpu.VMEM((df // LANES, nb * _seg_pitch(GRID_W), LANES), F32)],
        compiler_params=_cparams(("parallel", "arbitrary")),
        name="stage_b",
    )(x, shift, scale, g.reshape(1, d), w_in_bf, jmat)


def _stage_b_ctx_kernel(x_ref, sh_ref, sc_ref, g_ref, w_ref, xr_ref):
    h = _rms(x_ref[0], g_ref[...]) * (1.0 + sc_ref[0]) + sh_ref[0]
    xr_ref[0] = jnp.dot(h.astype(BF16), w_ref[...], preferred_element_type=F32).astype(BF16)


def _stage_b_ctx(ctx, shift, scale, g, w_xr_bf):
    b, s, d = ctx.shape
    dr = w_xr_bf.shape[1]
    vec = pl.BlockSpec((1, 1, d), lambda i: (0, 0, 0))
    return pl.pallas_call(
        _stage_b_ctx_kernel,
        grid=(b,),
        in_specs=[pl.BlockSpec((1, s, d), lambda i: (i, 0, 0)), vec, vec,
                  pl.BlockSpec((1, d), lambda i: (0, 0)),
                  pl.BlockSpec((d, dr), lambda i: (0, 0))],
        out_specs=pl.BlockSpec((1, s, dr), lambda i: (i, 0, 0)),
        out_shape=jax.ShapeDtypeStruct((b, s, dr), BF16),
        compiler_params=_cparams(("arbitrary",)),
        name="stage_b_ctx",
    )(ctx, shift, scale, g.reshape(1, d), w_xr_bf)


HALO = 16


def _sigmoid(x):
    return 0.5 * jnp.tanh(0.5 * x) + 0.5


def _rg_kernel(xs_ref, h0_ref, cw_ref, cb_ref, w_ref, b_ref, lam_ref, pm_ref, pmt_ref, out_ref, hfin_ref,
               hf_s, xc_s, a_s, u_s, hl_s, p_s, c_s, hc_s, *, tc, nchunk, seq, dr):
    p = pl.program_id(1)
    j = pl.program_id(2)
    cidx = jnp.where(p == 0, j, nchunk - 1 - j)
    start = pl.multiple_of(cidx * tc, tc)
    nseg = SUBLANES
    sl = tc // nseg
    sub = lax.broadcasted_iota(jnp.int32, (nseg, dr), 0)

    @pl.when(p == 0)
    def _():
        xp = jnp.dot(pm_ref[...], xs_ref[0, pl.ds(start, tc), :], preferred_element_type=F32)
        pstart = pl.multiple_of(jnp.maximum(start - HALO, 0), HALO)
        nstart = pl.multiple_of(jnp.minimum(start + tc, seq - HALO), HALO)
        prev = xs_ref[0, pl.ds(pstart, HALO), :].astype(F32)
        nxt = xs_ref[0, pl.ds(nstart, HALO), :].astype(F32)
        prev = jnp.where(cidx > 0, prev, 0.0)
        nxt = jnp.where(cidx < nchunk - 1, nxt, 0.0)
        tm2 = jnp.where(sub == 0, prev[HALO - 2:HALO - 1], pltpu.roll(xp[(sl - 2) * nseg:(sl - 1) * nseg], 1, 0))
        tm1 = jnp.where(sub == 0, prev[HALO - 1:HALO], pltpu.roll(xp[(sl - 1) * nseg:sl * nseg], 1, 0))
        tp1 = jnp.where(sub == nseg - 1, nxt[0:1], pltpu.roll(xp[0:nseg], nseg - 1, 0))
        ext = jnp.concatenate([tm2, tm1, xp, tp1], axis=0)
        xc = cb_ref[...] + cw_ref[0:1, :] * ext[0:tc]
        for k in range(1, CONV_W):
            xc = xc + cw_ref[k:k + 1, :] * ext[k * nseg:k * nseg + tc]
        xc_s[pl.ds(start, tc), :] = xc

    xc = xc_s[pl.ds(start, tc), :]
    gates = jnp.dot(xc.astype(BF16), w_ref[p], preferred_element_type=F32) + b_ref[p]
    i = _sigmoid(gates[:, dr:])
    half_c = (-0.5 * RG_C) * jax.nn.softplus(-lam_ref[p])
    log_a = half_c * jnp.tanh(0.5 * gates[:, :dr]) + half_c
    a = jnp.exp(log_a)
    a_s[...] = a
    w = -jnp.tanh(log_a) * (1.0 + a * a)
    u_s[...] = jnp.where(w > 0.0, w * lax.rsqrt(w), 0.0) * (i * xc)

    @pl.when(jnp.logical_and(p == 0, j == 0))
    def _():
        hfin_ref[...] = jnp.zeros_like(hfin_ref)

    @pl.when(j == 0)
    def _():
        hc_s[0:1, :] = h0_ref[0, pl.ds(p, 1), :]

    def segment_scan(reverse):
        def body(q, carry):
            t = (sl - 1 - q) if reverse else q
            rows = pl.ds(pl.multiple_of(t * nseg, nseg), nseg)
            h, pr = carry
            av = a_s[rows, :]
            h = av * h + u_s[rows, :]
            pr = av * pr
            hl_s[rows, :] = h
            p_s[rows, :] = pr
            return h, pr
        h_end, p_end = lax.fori_loop(0, sl, body, (jnp.zeros((nseg, dr), F32), jnp.ones((nseg, dr), F32)),
                                     unroll=4)
        carry = hc_s[0:1, :]
        for g in (range(nseg - 1, -1, -1) if reverse else range(nseg)):
            c_s[g:g + 1, :] = carry
            carry = h_end[g:g + 1, :] + p_end[g:g + 1, :] * carry
        hc_s[0:1, :] = carry
        return c_s[...]

    def corrected(cin):
        h = hl_s[...].reshape(sl, nseg, dr) + p_s[...].reshape(sl, nseg, dr) * cin[None]
        return h.reshape(tc, dr)

    @pl.when(p == 0)
    def _():
        hf_s[pl.ds(start, tc), :] = corrected(segment_scan(False))

    @pl.when(p == 1)
    def _():
        tot = corrected(segment_scan(True)) + hf_s[pl.ds(start, tc), :]
        out_ref[0] = jnp.dot(pmt_ref[...], tot.astype(BF16), preferred_element_type=F32).astype(BF16)

    @pl.when(j == nchunk - 1)
    def _():
        hfin_ref[0, pl.ds(p, 1), :] = hc_s[0:1, :]


def _rg_scan(xs, h0, conv_w, conv_b, wcat, bcat, lam, tc):
    b, s, dr = xs.shape
    nchunk = s // tc
    last = nchunk - 1
    sl = tc // SUBLANES
    src = (np.arange(tc) % SUBLANES) * sl + np.arange(tc) // SUBLANES
    pm = np.zeros((tc, tc), np.float32)
    pm[np.arange(tc), src] = 1.0
    chunk_buf = pltpu.VMEM((tc, dr), F32)
    full2 = lambda shape: pl.BlockSpec(shape, lambda i, p, j: (0,) * len(shape))
    return pl.pallas_call(
        functools.partial(_rg_kernel, tc=tc, nchunk=nchunk, seq=s, dr=dr),
        grid=(b, 2, nchunk),
        in_specs=[pl.BlockSpec((1, s, dr), lambda i, p, j: (i, 0, 0)),
                  pl.BlockSpec((1, SUBLANES, dr), lambda i, p, j: (i, 0, 0)),
                  full2((CONV_W, dr)), full2((1, dr)),
                  full2((2, dr, 2 * dr)), full2((2, 1, 2 * dr)), full2((2, 1, dr)),
                  full2((tc, tc)), full2((tc, tc))],
        out_specs=[pl.BlockSpec((1, tc, dr), lambda i, p, j: (i, jnp.where(p == 0, last, last - j), 0)),
                   pl.BlockSpec((1, SUBLANES, dr), lambda i, p, j: (i, 0, 0))],
        out_shape=[jax.ShapeDtypeStruct((b, s, dr), BF16),
                   jax.ShapeDtypeStruct((b, SUBLANES, dr), F32)],
        scratch_shapes=[pltpu.VMEM((s, dr), F32), pltpu.VMEM((s, dr), F32),
                        chunk_buf, chunk_buf, chunk_buf, chunk_buf,
                        pltpu.VMEM((SUBLANES, dr), F32), pltpu.VMEM((SUBLANES, dr), F32)],
        compiler_params=_cparams(("arbitrary", "arbitrary", "arbitrary")),
        name="rg_scan",
    )(xs, h0, conv_w, conv_b.reshape(1, dr), wcat, bcat, lam, jnp.asarray(pm, BF16), jnp.asarray(pm.T, BF16))


def _f1_kernel(d_ref, x_ref, y_ref):
    y_ref[0] = jnp.dot(d_ref[...], x_ref[0], preferred_element_type=F32).astype(BF16)


def _fourier_stage1(fv, d2, tl):
    b, r, n = fv.shape
    return pl.pallas_call(
        _f1_kernel,
        grid=(b, n // tl),
        in_specs=[pl.BlockSpec((2 * r, r), lambda i, l: (0, 0)),
                  pl.BlockSpec((1, r, tl), lambda i, l: (i, 0, l))],
        out_specs=pl.BlockSpec((1, 2 * r, tl), lambda i, l: (i, 0, l)),
        out_shape=jax.ShapeDtypeStruct((b, 2 * r, n), BF16),
        compiler_params=_cparams(("parallel", "arbitrary")),
        name="fourier_stage1",
    )(d2, fv)


def _f2_kernel(y_ref, e_ref, wcs_ref, g_ref, o_ref, obuf, *, kb, df):
    zr, zi = [], []
    for q in range(kb):
        yk = jnp.concatenate([y_ref[0, 0, q], y_ref[0, 1, q]], axis=0)
        z = jnp.dot(e_ref[q], yk, preferred_element_type=F32)
        zr.append(z[:GRID_W])
        zi.append(z[GRID_W:])
    zr = jnp.concatenate(zr, axis=0).astype(BF16)
    zi = jnp.concatenate(zi, axis=0).astype(BF16)
    gd = wcs_ref.shape[2]
    o = jnp.concatenate(
        [jnp.dot(jnp.concatenate([zr[:, g * gd:(g + 1) * gd], zi[:, g * gd:(g + 1) * gd]], axis=1), wcs_ref[g],
                 preferred_element_type=F32) for g in range(df // gd)], axis=1)
    on = _rms(o, g_ref[...])

    def store_o(pos, c, tile):
        o_ref[0, pos, :, c * LANES:(c + 1) * LANES] = tile.astype(BF16)
    _pitched_store(on, obuf, 0)
    _pitched_gather(obuf, kb, store_o)


def _fourier_stage2(y5, etab, wcs, g, kb):
    b, _, r, w, df = y5.shape
    return pl.pallas_call(
        functools.partial(_f2_kernel, kb=kb, df=df),
        grid=(b, r // kb),
        in_specs=[pl.BlockSpec((1, 2, kb, w, df), lambda i, k: (i, 0, k, 0, 0)),
                  pl.BlockSpec((kb, 2 * w, 2 * w), lambda i, k: (k, 0, 0)),
                  pl.BlockSpec(wcs.shape, lambda i, k: (0, 0, 0)),
                  pl.BlockSpec((1, df), lambda i, k: (0, 0))],
        out_specs=pl.BlockSpec((1, w, kb, df), lambda i, k: (i, 0, k, 0)),
        out_shape=jax.ShapeDtypeStruct((b, w, r, df), BF16),
        scratch_shapes=[pltpu.VMEM((df // LANES, kb * _seg_pitch(GRID_W), LANES), F32)],
        compiler_params=_cparams(("parallel", "arbitrary")),
        name="fourier_stage2",
    )(y5, etab, wcs, g.reshape(1, df))


def _stage_m_kernel(fn_ref, hs_ref, gg_ref, x_ref, gtm_ref, shf_ref, scf_ref, gr_ref, gffn_ref,
                    wo_ref, wr_ref, br_ref, j_ref, x1_ref, h2_ref, idx_ref, gate_ref, *, df):
    tm = x_ref.shape[1]
    hs = hs_ref[0]
    blocks = []
    for r in range(tm // GRID_W):
        blk = hs[r * GRID_W:(r + 1) * GRID_W]
        if r % 2 == 1:
            blk = jnp.dot(j_ref[...], blk, preferred_element_type=F32)
        blocks.append(blk.astype(F32))
    rg = jnp.concatenate(blocks, axis=0) * gg_ref[0].astype(F32)
    rgn = _rms(rg, gr_ref[...]).astype(BF16)
    mix = jnp.dot(fn_ref[0], wo_ref[:df, :], preferred_element_type=F32)
    mix += jnp.dot(rgn, wo_ref[df:, :], preferred_element_type=F32)
    x1 = x_ref[0] + gtm_ref[0] * mix
    x1_ref[0] = x1
    h2 = _rms(x1, gffn_ref[...]) * (1.0 + scf_ref[0]) + shf_ref[0]
    h2_ref[0] = _pack_halves(h2)

    logits = _dot3_nt(wr_ref[...], h2) + br_ref[...]
    eidx = lax.broadcasted_iota(jnp.int32, logits.shape, 0)
    vals, idxs = [], []
    for _ in range(TOP_K):
        m = jnp.max(logits, axis=0, keepdims=True)
        sel = jnp.min(jnp.where(logits == m, eidx, N_EXPERTS), axis=0, keepdims=True)
        vals.append(m)
        idxs.append(sel)
        logits = jnp.where(eidx == sel, -jnp.inf, logits)
    ex = [jnp.exp(v - vals[0]) for v in vals]
    den = ex[0] + ex[1] + ex[2] + ex[3]
    for k in range(TOP_K):
        gate_ref[k:k + 1, :] = ex[k] / den
        idx_ref[k:k + 1, :] = idxs[k]


def _stage_m(fn, hs, gg, x, gt_m, sh_f, sc_f, g_out_r, g_ffn, w_out_bf, w_router_t, b_router, jmat, tm):
    b, s, d = x.shape
    df = fn.shape[2]
    dr = hs.shape[2]
    nt = s // tm
    ne = w_router_t.shape[0]
    vec = pl.BlockSpec((1, 1, d), lambda i, t: (i, 0, 0))
    half = lambda dd: pl.BlockSpec((1, tm, dd), lambda i, t: (i, t, 0))
    full = lambda shape: pl.BlockSpec(shape, lambda i, t: (0,) * len(shape))
    tok = pl.BlockSpec((TOP_K, tm), lambda i, t: (0, i * nt + t))
    return pl.pallas_call(
        functools.partial(_stage_m_kernel, df=df),
        grid=(b, nt),
        in_specs=[half(df), half(dr), half(dr), half(d), vec, vec, vec,
                  full((1, dr)), full((1, d)), full((d, d)), full((ne, d)), full((ne, 1)),
                  full((GRID_W, GRID_W))],
        out_specs=[half(d), half(d // 2), tok, tok],
        out_shape=[jax.ShapeDtypeStruct((b, s, d), F32), jax.ShapeDtypeStruct((b, s, d // 2), jnp.int32),
                   jax.ShapeDtypeStruct((TOP_K, b * s), jnp.int32),
                   jax.ShapeDtypeStruct((TOP_K, b * s), F32)],
        compiler_params=_cparams(("parallel", "arbitrary")),
        name="stage_m",
    )(fn, hs, gg, x, gt_m, sh_f, sc_f, g_out_r.reshape(1, dr), g_ffn.reshape(1, d), w_out_bf,
      w_router_t, b_router.reshape(ne, 1), jmat)


def _rank_kernel(idx_ref, tri_ref, rank_ref, cnt_ref, carry_s):
    c = pl.program_id(0)

    @pl.when(c == 0)
    def _():
        carry_s[...] = jnp.zeros_like(carry_s)

    l = tri_ref.shape[0]
    eidx = lax.broadcasted_iota(jnp.int32, (N_EXPERTS, l), 0)
    for sub in range(idx_ref.shape[1] // l):
        lanes = slice(sub * l, (sub + 1) * l)
        for k in range(TOP_K):
            onehot = eidx == idx_ref[k:k + 1, lanes]
            oh = jnp.where(onehot, 1.0, 0.0)
            prefix = jnp.dot(oh.astype(BF16), tri_ref[...], preferred_element_type=F32)
            carry = carry_s[:, 0:1]
            rank = jnp.sum(jnp.where(onehot, prefix - 1.0 + carry, 0.0), axis=0, keepdims=True)
            rank_ref[k:k + 1, lanes] = rank.astype(jnp.int32)
            carry_s[...] = carry_s[...] + jnp.sum(oh, axis=1, keepdims=True)
    cnt_ref[...] = carry_s[...].astype(jnp.int32)


def _dispatch_ranks(idx, tri, tl):
    k, t = idx.shape
    return pl.pallas_call(
        _rank_kernel,
        grid=(t // tl,),
        in_specs=[pl.BlockSpec((k, tl), lambda c: (0, c)),
                  pl.BlockSpec(tri.shape, lambda c: (0, 0))],
        out_specs=[pl.BlockSpec((k, tl), lambda c: (0, c)),
                   pl.BlockSpec((N_EXPERTS, LANES), lambda c: (0, 0))],
        out_shape=[jax.ShapeDtypeStruct((k, t), jnp.int32),
                   jax.ShapeDtypeStruct((N_EXPERTS, LANES), jnp.int32)],
        scratch_shapes=[pltpu.VMEM((N_EXPERTS, LANES), F32)],
        compiler_params=_cparams(("arbitrary",)),
        name="dispatch_ranks",
    )(idx, tri)


def _moe_kernel(be_ref, bv_ref, nu_ref, x_ref, wgu_ref, bgu_ref, wd_ref, bd_ref, o_ref, wgu_s, wd_s, *, dff):
    i = pl.program_id(0)
    h = x_ref.shape[1]

    @pl.when(jnp.logical_or(i == 0, be_ref[i] != be_ref[jnp.maximum(i - 1, 0)]))
    def _():
        wgu_s[...] = wgu_ref[0].astype(BF16)
        wd_s[...] = wd_ref[0].astype(BF16)

    @pl.when(i < nu_ref[0])
    def _():
        rows = lax.broadcasted_iota(jnp.int32, x_ref.shape, 0)
        xw = jnp.where(rows < bv_ref[i], x_ref[...], 0)
        xa, xb = _unpack_halves(xw)
        xa = xa.astype(BF16)
        xb = xb.astype(BF16)
        acc = None
        for c in range(dff // MOE_FF_CHUNK):
            gs = slice(c * MOE_FF_CHUNK, (c + 1) * MOE_FF_CHUNK)
            us = slice(dff + c * MOE_FF_CHUNK, dff + (c + 1) * MOE_FF_CHUNK)
            g = jnp.dot(xa, wgu_s[:h, gs], preferred_element_type=F32)
            g += jnp.dot(xb, wgu_s[h:, gs], preferred_element_type=F32)
            u = jnp.dot(xa, wgu_s[:h, us], preferred_element_type=F32)
            u += jnp.dot(xb, wgu_s[h:, us], preferred_element_type=F32)
            gt = jnp.minimum(g + bgu_ref[0, :, gs], SWIGLU_LIMIT)
            up = jnp.clip(u + bgu_ref[0, :, us], -SWIGLU_LIMIT, SWIGLU_LIMIT)
            act = (up + 1.0) * (gt * _sigmoid(SWIGLU_ALPHA * gt))
            part = jnp.dot(act.astype(BF16), wd_s[gs, :], preferred_element_type=F32)
            acc = part if acc is None else acc + part
        o_ref[...] = _pack_halves(acc + bd_ref[0])


def _moe_experts(blk_expert, blk_valid, n_used, xs, wgu, bgu, wd, bd, tmm):
    cap, h = xs.shape
    ne, d, dff2 = wgu.shape
    dff = dff2 // 2
    row_blk = lambda i, be, bv, nu: (jnp.maximum(jnp.minimum(i, nu[0] - 1), 0), 0)
    wsel = lambda i, be, bv, nu: (be[i], 0, 0)
    grid_spec = pltpu.PrefetchScalarGridSpec(
        num_scalar_prefetch=3,
        grid=(cap // tmm,),
        in_specs=[pl.BlockSpec((tmm, h), row_blk),
                  pl.BlockSpec((1, d, dff2), wsel),
                  pl.BlockSpec((1, 1, dff2), wsel),
                  pl.BlockSpec((1, dff, d), wsel),
                  pl.BlockSpec((1, 1, d), wsel)],
        out_specs=pl.BlockSpec((tmm, h), row_blk),
        scratch_shapes=[pltpu.VMEM((d, dff2), BF16), pltpu.VMEM((dff, d), BF16)],
    )
    return pl.pallas_call(
        functools.partial(_moe_kernel, dff=dff),
        grid_spec=grid_spec,
        out_shape=jax.ShapeDtypeStruct((cap, h), jnp.int32),
        compiler_params=_cparams(("arbitrary",)),
        name="moe_experts",
    )(blk_expert, blk_valid, n_used, xs, wgu, bgu.reshape(ne, 1, dff2), wd, bd.reshape(ne, 1, d))


SC_CHUNK = 64
SC_ID_CHUNK = 128


def _sc_workers():
    info = plsc.get_sparse_core_info()
    return info.num_cores, info.num_subcores


def _sc_scatter(rows, dest, out_rows=None, out_ref=None):
    n, w = rows.shape
    nc, ns = _sc_workers()
    per_w = n // (nc * ns)
    assert per_w % SC_CHUNK == 0
    mesh = plsc.VectorSubcoreMesh(core_axis_name="c", subcore_axis_name="s")
    out_type = () if out_ref is not None else jax.ShapeDtypeStruct((out_rows, w), jnp.int32)

    @functools.partial(
        pl.kernel, mesh=mesh, out_type=out_type,
        scratch_types=[pltpu.VMEM((SC_CHUNK,), jnp.int32),
                       pltpu.VMEM((SC_CHUNK, w), jnp.int32),
                       pltpu.SemaphoreType.DMA],
    )
    def scatter_rows(rows_hbm, dest_hbm, out_hbm, idx_v, rows_v, sem):
        base = (lax.axis_index("s") * nc + lax.axis_index("c")) * per_w

        @pl.loop(0, per_w // SC_CHUNK)
        def _(j):
            off = pl.multiple_of(base + j * SC_CHUNK, SC_CHUNK)
            pltpu.sync_copy(rows_hbm.at[pl.ds(off, SC_CHUNK)], rows_v)
            pltpu.sync_copy(dest_hbm.at[pl.ds(off, SC_CHUNK)], idx_v)
            pltpu.async_copy(rows_v, out_hbm.at[idx_v], sem).wait()

    if out_ref is not None:
        scatter_rows(rows, dest, out_ref)
        return None
    return scatter_rows(rows, dest)


def _sc_scatter_ids(dest, out_rows):
    n = dest.shape[0]
    info = plsc.get_sparse_core_info()
    nc, ns, nl = info.num_cores, info.num_subcores, info.num_lanes
    per_w = n // (nc * ns)
    chunk = SC_ID_CHUNK
    assert per_w % (2 * chunk) == 0
    mesh = plsc.VectorSubcoreMesh(core_axis_name="c", subcore_axis_name="s")
    idx_t = pltpu.VMEM((chunk,), jnp.int32)
    rows_t = pltpu.VMEM((chunk, LANES), jnp.int32)

    @functools.partial(
        pl.kernel, mesh=mesh,
        out_type=jax.ShapeDtypeStruct((out_rows, LANES), jnp.int32),
        scratch_types=[idx_t, idx_t, rows_t, rows_t, pltpu.SemaphoreType.DMA, pltpu.SemaphoreType.DMA],
    )
    def scatter_ids(dest_hbm, out_hbm, idx_a, idx_b, rows_a, rows_b, sem_a, sem_b):
        base = (lax.axis_index("s") * nc + lax.axis_index("c")) * per_w

        def start(off, idx_v, rows_v, sem):
            for r in range(chunk):
                rows_v[r, pl.ds(0, nl)] = jnp.zeros((nl,), jnp.int32) + (off + r)
            pltpu.sync_copy(dest_hbm.at[pl.ds(off, chunk)], idx_v)
            return pltpu.async_copy(rows_v, out_hbm.at[idx_v], sem)

        @pl.loop(0, per_w // (2 * chunk))
        def _(j):
            off = pl.multiple_of(base + j * (2 * chunk), 2 * chunk)
            copy_a = start(off, idx_a, rows_a, sem_a)
            copy_b = start(off + chunk, idx_b, rows_b, sem_b)
            copy_a.wait()
            copy_b.wait()

    return scatter_ids(dest)


def _sc_gather(table, idx):
    n = idx.shape[0]
    w = table.shape[1]
    nc, ns = _sc_workers()
    per_w = n // (nc * ns)
    assert per_w % SC_CHUNK == 0
    mesh = plsc.VectorSubcoreMesh(core_axis_name="c", subcore_axis_name="s")

    @functools.partial(
        pl.kernel, mesh=mesh,
        out_type=jax.ShapeDtypeStruct((n, w), jnp.int32),
        scratch_types=[pltpu.VMEM((SC_CHUNK,), jnp.int32),
                       pltpu.VMEM((SC_CHUNK, w), jnp.int32),
                       pltpu.SemaphoreType.DMA],
    )
    def gather_rows(table_hbm, idx_hbm, out_hbm, idx_v, rows_v, sem):
        base = (lax.axis_index("s") * nc + lax.axis_index("c")) * per_w

        @pl.loop(0, per_w // SC_CHUNK)
        def _(j):
            off = pl.multiple_of(base + j * SC_CHUNK, SC_CHUNK)
            pltpu.sync_copy(idx_hbm.at[pl.ds(off, SC_CHUNK)], idx_v)
            pltpu.async_copy(table_hbm.at[idx_v], rows_v, sem).wait()
            pltpu.sync_copy(rows_v, out_hbm.at[pl.ds(off, SC_CHUNK)])

    return gather_rows(table, idx)


def _combine_kernel(x1_ref, y_ref, gate_ref, gtf_ref, g_ref, o_ref):
    h = y_ref.shape[3]
    gates = gate_ref[0]
    moe_a = moe_b = None
    for k in range(TOP_K):
        ya, yb = _unpack_halves(y_ref[k, 0])
        gk = gates[:, k:k + 1]
        moe_a = gk * ya if k == 0 else moe_a + gk * ya
        moe_b = gk * yb if k == 0 else moe_b + gk * yb
    za = x1_ref[0, :, :h] + gtf_ref[0, :, :h] * moe_a
    zb = x1_ref[0, :, h:] + gtf_ref[0, :, h:] * moe_b
    ms = (jnp.sum(za * za, axis=-1, keepdims=True) + jnp.sum(zb * zb, axis=-1, keepdims=True)) / (2 * h)
    inv = lax.rsqrt(ms + EPS)
    o_ref[0, :, :h] = za * inv * g_ref[:, :h]
    o_ref[0, :, h:] = zb * inv * g_ref[:, h:]


def _combine(x1, yk, gates_t, gt_f, g_final, tm):
    b, s, d = x1.shape
    h = yk.shape[3]
    return pl.pallas_call(
        _combine_kernel,
        grid=(b, s // tm),
        in_specs=[pl.BlockSpec((1, tm, d), lambda i, t: (i, t, 0)),
                  pl.BlockSpec((TOP_K, 1, tm, h), lambda i, t: (0, i, t, 0)),
                  pl.BlockSpec((1, tm, TOP_K), lambda i, t: (i, t, 0)),
                  pl.BlockSpec((1, 1, d), lambda i, t: (i, 0, 0)),
                  pl.BlockSpec((1, d), lambda i, t: (0, 0))],
        out_specs=pl.BlockSpec((1, tm, d), lambda i, t: (i, t, 0)),
        out_shape=jax.ShapeDtypeStruct((b, s, d), F32),
        compiler_params=_cparams(("parallel", "arbitrary")),
        name="combine",
    )(x1, yk, gates_t, gt_f, g_final.reshape(1, d))


def _dft_tables(rows, gd):
    seq = rows * GRID_W
    n = np.arange(rows)
    ang1 = 2.0 * np.pi * np.outer(n, n) / rows
    d2 = np.concatenate([np.cos(ang1), -np.sin(ang1)], axis=0)
    k1 = np.arange(rows)[:, None, None]
    k2 = np.arange(GRID_W)[None, :, None]
    n2 = np.arange(GRID_W)[None, None, :]
    ang2 = 2.0 * np.pi * ((n2 * (k1 + rows * k2)) % seq) / seq
    ec, es = np.cos(ang2), np.sin(ang2)
    etab = np.concatenate([np.concatenate([ec, es], axis=2),
                           np.concatenate([-es, ec], axis=2)], axis=1)
    c = np.arange(gd)
    angc = 2.0 * np.pi * np.outer(c, c) / gd
    scale = 1.0 / np.sqrt(seq * gd)
    return (jnp.asarray(d2, BF16), jnp.asarray(etab, BF16),
            jnp.asarray(np.cos(angc) * scale, F32), jnp.asarray(np.sin(angc) * scale, F32))


def _block_diag(w):
    h, i, o = w.shape
    eye = jnp.eye(h, dtype=w.dtype)
    return (eye[:, None, :, None] * w[:, :, None, :]).reshape(h * i, h * o)


def kernel(x, c, ctx, c_ctx, w_mod, b_mod, g_norm_mix, g_norm_ffn, w_in, w_fourier, conv_w, conv_b,
           rg_w_a, rg_b_a, rg_w_x, rg_b_x, rg_lam, g_out_fourier, g_out_rg, w_out, w_router, b_router,
           w_gate_up, b_gate_up, w_down, b_down, g_final):
    assert w_mod.shape[0] == 1, "single-layer stack only"
    b, s, d = x.shape
    df = w_fourier.shape[1] * w_fourier.shape[2]
    dr = conv_w.shape[2]
    gd = w_fourier.shape[2]
    rows = s // GRID_W
    t = b * s
    ne = w_router.shape[2]

    mrows = -(-(b + 1) // SUBLANES) * SUBLANES
    cond = jnp.zeros((mrows, d), F32).at[:b].set(c).at[b].set(c_ctx)
    mod = _adaln(cond, w_mod[0], b_mod[0])
    sh_m, sc_m, gt_m, sh_f, sc_f, gt_f = [mod[:b, k * d:(k + 1) * d].reshape(b, 1, d) for k in range(N_MOD)]
    csh_m = mod[b:b + 1, 0:d].reshape(1, 1, d)
    csc_m = mod[b:b + 1, d:2 * d].reshape(1, 1, d)

    tm = min(s, TOKEN_TILE)
    d2, etab, cmat, smat = _dft_tables(rows, gd)
    jmat = jnp.asarray(np.eye(GRID_W)[::-1].copy(), BF16)

    w_in_bf = w_in[0].astype(BF16)
    f, xs, gg = _stage_b(x, sh_m, sc_m, g_norm_mix[0], w_in_bf, jmat, df, dr, tm=tm)
    xr_ctx = _stage_b_ctx(ctx, csh_m, csc_m, g_norm_mix[0], w_in_bf[:, df:df + dr])

    wcat = jnp.stack([jnp.concatenate([_block_diag(rg_w_a[0, dd]), _block_diag(rg_w_x[0, dd])], axis=1)
                      for dd in range(2)]).astype(BF16)
    bcat = jnp.concatenate([rg_b_a[0], rg_b_x[0]], axis=1).reshape(2, 1, 2 * dr)
    lam = rg_lam[0].reshape(2, 1, dr)
    h0 = jnp.zeros((b, SUBLANES, dr), F32)
    _, hfin_ctx = _rg_scan(xr_ctx, h0, conv_w[0], conv_b[0], wcat, bcat, lam, tc=ctx.shape[1])
    hs, _ = _rg_scan(xs, hfin_ctx, conv_w[0], conv_b[0], wcat, bcat, lam, tc=min(s, SCAN_CHUNK))

    cw, sw = _fold_fourier(cmat, smat, w_fourier[0])
    wcs = jnp.concatenate([cw, sw], axis=1).astype(BF16)
    y = _fourier_stage1(f, d2, tl=min(GRID_W * df, 16384))
    fn = _fourier_stage2(y.reshape(b, 2, rows, GRID_W, df), etab, wcs, g_out_fourier[0],
                         kb=min(rows, 16))
    fn = fn.reshape(b, s, df)

    x1, h2, idx, gates = _stage_m(fn, hs, gg, x, gt_m, sh_f, sc_f, g_out_rg[0], g_norm_ffn[0],
                                  w_out[0].astype(BF16), w_router[0].T, b_router[0], jmat, tm=tm)

    tl = min(t, RANK_TILE)
    tri = jnp.asarray(np.triu(np.ones((tl, tl))), BF16)
    rank, cnt = _dispatch_ranks(idx, tri, min(t, RANK_STEP))
    counts = cnt[:, 0]
    tmm = MOE_ROW_TILE
    padded = (counts + tmm - 1) // tmm * tmm
    pad_end = jnp.cumsum(padded)
    pad_start = pad_end - padded
    eids = jnp.arange(ne, dtype=jnp.int32)
    dest = rank + jnp.sum(jnp.where(idx[:, :, None] == eids, pad_start, 0), axis=-1)
    n_blocks = -(-(t * TOP_K) // tmm) + ne
    cap = n_blocks * tmm
    n_used = (pad_end[-1] // tmm).astype(jnp.int32).reshape(1)
    blk_start = jnp.arange(n_blocks, dtype=jnp.int32) * tmm
    blk_expert = jnp.sum(blk_start[:, None] >= pad_end[None, :], axis=1).astype(jnp.int32)
    last_expert = jnp.sum(pad_end[-1] - tmm >= pad_end).astype(jnp.int32)
    blk_expert = jnp.minimum(blk_expert, last_expert)
    sel = blk_expert[:, None] == eids
    blk_first = jnp.sum(jnp.where(sel, pad_start, 0), axis=1)
    blk_count = jnp.sum(jnp.where(sel, counts, 0), axis=1)
    blk_valid = jnp.clip(blk_count - (blk_start - blk_first), 0, tmm).astype(jnp.int32)

    na = TOP_K * t
    inv = _sc_scatter_ids(dest.reshape(-1), cap)[:, 0].reshape(n_blocks, tmm)
    live = jnp.arange(tmm, dtype=jnp.int32)[None, :] < blk_valid[:, None]
    spread = jnp.arange(cap, dtype=jnp.int32).reshape(n_blocks, tmm) % t
    src_tok = jnp.where(live, inv % t, spread).reshape(-1)
    dst_row = jnp.where(live, inv, na + spread).reshape(-1)

    h2_rows = h2.reshape(t, d // 2)
    y_all = jax.empty_ref(jax.ShapeDtypeStruct(((TOP_K + 1) * t, d // 2), jnp.int32))
    unit = n_blocks // sum(MOE_PIPE)
    assert unit * sum(MOE_PIPE) == n_blocks
    blk0 = 0
    for parts in MOE_PIPE:
        nq = parts * unit
        blocks = slice(blk0, blk0 + nq)
        rows = slice(blk0 * tmm, (blk0 + nq) * tmm)
        x_q = _sc_gather(h2_rows, src_tok[rows])
        nu_q = jnp.clip(n_used - blk0, 0, nq).astype(jnp.int32)
        blk0 += nq
        y_q = _moe_experts(blk_expert[blocks], blk_valid[blocks], nu_q, x_q,
                           w_gate_up[0], b_gate_up[0], w_down[0], b_down[0], tmm)
        _sc_scatter(y_q, dst_row[rows], out_ref=y_all)
    yk = y_all[...].reshape(TOP_K + 1, b, s, d // 2)
    return _combine(x1, yk, gates.T.reshape(b, s, TOP_K), gt_f, g_final, tm)
```

```python
import functools

import numpy as np
import jax
import jax.numpy as jnp
from jax import lax
from jax.experimental import pallas as pl
from jax.experimental.pallas import tpu as pltpu
from jax.experimental.pallas import tpu_sc as plsc

GRID_W = 64
FOURIER_GROUPS = 4
RG_HEADS = 8
CONV_W = 4
CONV_PAD_LO = 2
RG_C = 8.0
N_EXPERTS = 32
TOP_K = 4
SWIGLU_LIMIT = 7.0
SWIGLU_ALPHA = 1.702
N_MOD = 6
EPS = 1e-6

LANES = 128
SUBLANES = 8
VMEM_LIMIT_BYTES = 56 * 1024 * 1024
TOKEN_TILE = 1024
RANK_TILE = 512
RANK_STEP = 2048
SCAN_CHUNK = 512
MOE_ROW_TILE = 512
MOE_FF_CHUNK = 512
MOE_PIPE = (1, 3, 3, 1)

F32 = jnp.float32
BF16 = jnp.bfloat16


def _cparams(sem):
    return pltpu.CompilerParams(dimension_semantics=sem, vmem_limit_bytes=VMEM_LIMIT_BYTES)


def _split_bf16(a):
    hi = a.astype(BF16)
    lo = (a - hi.astype(F32)).astype(BF16)
    return hi, lo


def _dot3(a, b):
    ah, al = _split_bf16(a)
    bh, bl = _split_bf16(b)
    out = jnp.dot(ah, bh, preferred_element_type=F32)
    out += jnp.dot(ah, bl, preferred_element_type=F32)
    out += jnp.dot(al, bh, preferred_element_type=F32)
    return out


def _dot3_nt(a, b):
    dn = (((1,), (1,)), ((), ()))
    ah, al = _split_bf16(a)
    bh, bl = _split_bf16(b)
    out = lax.dot_general(ah, bh, dn, preferred_element_type=F32)
    out += lax.dot_general(ah, bl, dn, preferred_element_type=F32)
    out += lax.dot_general(al, bh, dn, preferred_element_type=F32)
    return out


def _gelu_tanh(x):
    return 0.5 * x * (1.0 + jnp.tanh(0.7978845608028654 * (x + 0.044715 * (x * x * x))))


def _rms(x, g):
    return x * lax.rsqrt(jnp.mean(x * x, axis=-1, keepdims=True) + EPS) * g


def _pack_halves(v):
    h = v.shape[1] // 2
    hi = lax.bitcast_convert_type(v[:, :h].astype(BF16).astype(F32), jnp.uint32)
    lo = lax.bitcast_convert_type(v[:, h:].astype(BF16).astype(F32), jnp.uint32)
    return lax.bitcast_convert_type(hi | (lo >> 16), jnp.int32)


def _unpack_halves(w):
    u = lax.bitcast_convert_type(w, jnp.uint32)
    hi = lax.bitcast_convert_type(u & jnp.uint32(0xFFFF0000), F32)
    lo = lax.bitcast_convert_type(u << 16, F32)
    return hi, lo


def _adaln_kernel(c_ref, w_ref, b_ref, o_ref):
    s = c_ref[...]
    s = s * jax.nn.sigmoid(s)
    o_ref[...] = _dot3(s, w_ref[...]) + b_ref[...]


def _adaln(cond, w_mod, b_mod):
    m, d = cond.shape
    n = w_mod.shape[1]
    tn = n // N_MOD
    return pl.pallas_call(
        _adaln_kernel,
        grid=(n // tn,),
        in_specs=[pl.BlockSpec((m, d), lambda i: (0, 0)),
                  pl.BlockSpec((d, tn), lambda i: (0, i)),
                  pl.BlockSpec((1, tn), lambda i: (0, i))],
        out_specs=pl.BlockSpec((m, tn), lambda i: (0, i)),
        out_shape=jax.ShapeDtypeStruct((m, n), F32),
        compiler_params=_cparams(("arbitrary",)),
        name="adaln",
    )(cond, w_mod, b_mod.reshape(1, n))


def _fold_kernel(c_ref, s_ref, w_ref, cw_ref, sw_ref):
    w = w_ref[0]
    cw_ref[0] = _dot3(c_ref[...], w)
    sw_ref[0] = _dot3(s_ref[...], w)


def _fold_fourier(cmat, smat, w_f):
    g, gd, _ = w_f.shape
    spec_m = pl.BlockSpec((gd, gd), lambda i: (0, 0))
    spec_w = pl.BlockSpec((1, gd, gd), lambda i: (i, 0, 0))
    return pl.pallas_call(
        _fold_kernel,
        grid=(g,),
        in_specs=[spec_m, spec_m, spec_w],
        out_specs=[spec_w, spec_w],
        out_shape=[jax.ShapeDtypeStruct((g, gd, gd), F32)] * 2,
        compiler_params=_cparams(("arbitrary",)),
        name="fold_fourier",
    )(cmat, smat, w_f)


def _seg_pitch(seg_len):
    n8 = seg_len // SUBLANES
    return SUBLANES * (n8 + 1 - n8 % 2)


def _pitched_store(v, buf, blk0):
    pitch = _seg_pitch(GRID_W)
    for r in range(v.shape[0] // GRID_W):
        for c in range(v.shape[1] // LANES):
            buf[c, (blk0 + r) * pitch:(blk0 + r) * pitch + GRID_W, :] = (
                v[r * GRID_W:(r + 1) * GRID_W, c * LANES:(c + 1) * LANES])


def _pitched_gather(buf, nb, store):
    pitch = _seg_pitch(GRID_W)
    for pos in range(GRID_W):
        for c in range(buf.shape[0]):
            store(pos, c, buf[c, pl.ds(pos, nb, stride=pitch), :])


def _stage_b_kernel(x_ref, sh_ref, sc_ref, g_ref, w_ref, j_ref, f_ref, xs_ref, gg_ref, fbuf, *, df, dr):
    tm = x_ref.shape[1]
    h = _rms(x_ref[0], g_ref[...]) * (1.0 + sc_ref[0]) + sh_ref[0]
    hb = h.astype(BF16)
    gr = jnp.dot(hb, w_ref[:, df + dr:], preferred_element_type=F32)
    gg_ref[0] = _gelu_tanh(gr).astype(BF16)
    _pitched_store(jnp.dot(hb, w_ref[:, :df], preferred_element_type=F32), fbuf, 0)

    def store_f(pos, c, tile):
        f_ref[0, :, pos * df + c * LANES:pos * df + (c + 1) * LANES] = tile.astype(BF16)
    _pitched_gather(fbuf, tm // GRID_W, store_f)
    xr = jnp.dot(hb, w_ref[:, df:df + dr], preferred_element_type=F32).astype(BF16)
    for r in range(tm // GRID_W):
        blk = xr[r * GRID_W:(r + 1) * GRID_W]
        if r % 2 == 1:
            blk = jnp.dot(j_ref[...], blk, preferred_element_type=F32).astype(BF16)
        xs_ref[0, r * GRID_W:(r + 1) * GRID_W, :] = blk


def _stage_b(x, shift, scale, g, w_in_bf, jmat, df, dr, tm):
    b, s, d = x.shape
    n = w_in_bf.shape[1]
    vec = pl.BlockSpec((1, 1, d), lambda i, t: (i, 0, 0))
    out = pl.BlockSpec((1, tm, df), lambda i, t: (i, t, 0))
    nb = tm // GRID_W
    tok = jax.ShapeDtypeStruct((b, s, df), BF16)
    return pl.pallas_call(
        functools.partial(_stage_b_kernel, df=df, dr=dr),
        grid=(b, s // tm),
        in_specs=[pl.BlockSpec((1, tm, d), lambda i, t: (i, t, 0)), vec, vec,
                  pl.BlockSpec((1, d), lambda i, t: (0, 0)),
                  pl.BlockSpec((d, n), lambda i, t: (0, 0)),
                  pl.BlockSpec((GRID_W, GRID_W), lambda i, t: (0, 0))],
        out_specs=[pl.BlockSpec((1, nb, GRID_W * df), lambda i, t: (i, t, 0)), out, out],
        out_shape=[jax.ShapeDtypeStruct((b, s // GRID_W, GRID_W * df), BF16), tok, tok],
        scratch_shapes=[pltpu.VMEM((df // LANES, nb * _seg_pitch(GRID_W), LANES), F32)],
        compiler_params=_cparams(("parallel", "arbitrary")),
        name="stage_b",
    )(x, shift, scale, g.reshape(1, d), w_in_bf, jmat)


def _stage_b_ctx_kernel(x_ref, sh_ref, sc_ref, g_ref, w_ref, xr_ref):
    h = _rms(x_ref[0], g_ref[...]) * (1.0 + sc_ref[0]) + sh_ref[0]
    xr_ref[0] = jnp.dot(h.astype(BF16), w_ref[...], preferred_element_type=F32).astype(BF16)


def _stage_b_ctx(ctx, shift, scale, g, w_xr_bf):
    b, s, d = ctx.shape
    dr = w_xr_bf.shape[1]
    vec = pl.BlockSpec((1, 1, d), lambda i: (0, 0, 0))
    return pl.pallas_call(
        _stage_b_ctx_kernel,
        grid=(b,),
        in_specs=[pl.BlockSpec((1, s, d), lambda i: (i, 0, 0)), vec, vec,
                  pl.BlockSpec((1, d), lambda i: (0, 0)),
                  pl.BlockSpec((d, dr), lambda i: (0, 0))],
        out_specs=pl.BlockSpec((1, s, dr), lambda i: (i, 0, 0)),
        out_shape=jax.ShapeDtypeStruct((b, s, dr), BF16),
        compiler_params=_cparams(("arbitrary",)),
        name="stage_b_ctx",
    )(ctx, shift, scale, g.reshape(1, d), w_xr_bf)


HALO = 16


def _sigmoid(x):
    return 0.5 * jnp.tanh(0.5 * x) + 0.5


def _rg_kernel(xs_ref, h0_ref, cw_ref, cb_ref, w_ref, b_ref, lam_ref, pm_ref, pmt_ref, out_ref, hfin_ref,
               hf_s, xc_s, a_s, u_s, hl_s, p_s, c_s, hc_s, *, tc, nchunk, seq, dr):
    p = pl.program_id(1)
    j = pl.program_id(2)
    cidx = jnp.where(p == 0, j, nchunk - 1 - j)
    start = pl.multiple_of(cidx * tc, tc)
    nseg = SUBLANES
    sl = tc // nseg
    sub = lax.broadcasted_iota(jnp.int32, (nseg, dr), 0)

    @pl.when(p == 0)
    def _():
        xp = jnp.dot(pm_ref[...], xs_ref[0, pl.ds(start, tc), :], preferred_element_type=F32)
        pstart = pl.multiple_of(jnp.maximum(start - HALO, 0), HALO)
        nstart = pl.multiple_of(jnp.minimum(start + tc, seq - HALO), HALO)
        prev = xs_ref[0, pl.ds(pstart, HALO), :].astype(F32)
        nxt = xs_ref[0, pl.ds(nstart, HALO), :].astype(F32)
        prev = jnp.where(cidx > 0, prev, 0.0)
        nxt = jnp.where(cidx < nchunk - 1, nxt, 0.0)
        tm2 = jnp.where(sub == 0, prev[HALO - 2:HALO - 1], pltpu.roll(xp[(sl - 2) * nseg:(sl - 1) * nseg], 1, 0))
        tm1 = jnp.where(sub == 0, prev[HALO - 1:HALO], pltpu.roll(xp[(sl - 1) * nseg:sl * nseg], 1, 0))
        tp1 = jnp.where(sub == nseg - 1, nxt[0:1], pltpu.roll(xp[0:nseg], nseg - 1, 0))
        ext = jnp.concatenate([tm2, tm1, xp, tp1], axis=0)
        xc = cb_ref[...] + cw_ref[0:1, :] * ext[0:tc]
        for k in range(1, CONV_W):
            xc = xc + cw_ref[k:k + 1, :] * ext[k * nseg:k * nseg + tc]
        xc_s[pl.ds(start, tc), :] = xc

    xc = xc_s[pl.ds(start, tc), :]
    gates = jnp.dot(xc.astype(BF16), w_ref[p], preferred_element_type=F32) + b_ref[p]
    i = _sigmoid(gates[:, dr:])
    half_c = (-0.5 * RG_C) * jax.nn.softplus(-lam_ref[p])
    log_a = half_c * jnp.tanh(0.5 * gates[:, :dr]) + half_c
    a = jnp.exp(log_a)
    a_s[...] = a
    w = -jnp.tanh(log_a) * (1.0 + a * a)
    u_s[...] = jnp.where(w > 0.0, w * lax.rsqrt(w), 0.0) * (i * xc)

    @pl.when(jnp.logical_and(p == 0, j == 0))
    def _():
        hfin_ref[...] = jnp.zeros_like(hfin_ref)

    @pl.when(j == 0)
    def _():
        hc_s[0:1, :] = h0_ref[0, pl.ds(p, 1), :]

    def segment_scan(reverse):
        def body(q, carry):
            t = (sl - 1 - q) if reverse else q
            rows = pl.ds(pl.multiple_of(t * nseg, nseg), nseg)
            h, pr = carry
            av = a_s[rows, :]
            h = av * h + u_s[rows, :]
            pr = av * pr
            hl_s[rows, :] = h
            p_s[rows, :] = pr
            return h, pr
        h_end, p_end = lax.fori_loop(0, sl, body, (jnp.zeros((nseg, dr), F32), jnp.ones((nseg, dr), F32)),
                                     unroll=4)
        carry = hc_s[0:1, :]
        for g in (range(nseg - 1, -1, -1) if reverse else range(nseg)):
            c_s[g:g + 1, :] = carry
            carry = h_end[g:g + 1, :] + p_end[g:g + 1, :] * carry
        hc_s[0:1, :] = carry
        return c_s[...]

    def corrected(cin):
        h = hl_s[...].reshape(sl, nseg, dr) + p_s[...].reshape(sl, nseg, dr) * cin[None]
        return h.reshape(tc, dr)

    @pl.when(p == 0)
    def _():
        hf_s[pl.ds(start, tc), :] = corrected(segment_scan(False))

    @pl.when(p == 1)
    def _():
        tot = corrected(segment_scan(True)) + hf_s[pl.ds(start, tc), :]
        out_ref[0] = jnp.dot(pmt_ref[...], tot.astype(BF16), preferred_element_type=F32).astype(BF16)

    @pl.when(j == nchunk - 1)
    def _():
        hfin_ref[0, pl.ds(p, 1), :] = hc_s[0:1, :]


def _rg_scan(xs, h0, conv_w, conv_b, wcat, bcat, lam, tc):
    b, s, dr = xs.shape
    nchunk = s // tc
    last = nchunk - 1
    sl = tc // SUBLANES
    src = (np.arange(tc) % SUBLANES) * sl + np.arange(tc) // SUBLANES
    pm = np.zeros((tc, tc), np.float32)
    pm[np.arange(tc), src] = 1.0
    chunk_buf = pltpu.VMEM((tc, dr), F32)
    full2 = lambda shape: pl.BlockSpec(shape, lambda i, p, j: (0,) * len(shape))
    return pl.pallas_call(
        functools.partial(_rg_kernel, tc=tc, nchunk=nchunk, seq=s, dr=dr),
        grid=(b, 2, nchunk),
        in_specs=[pl.BlockSpec((1, s, dr), lambda i, p, j: (i, 0, 0)),
                  pl.BlockSpec((1, SUBLANES, dr), lambda i, p, j: (i, 0, 0)),
                  full2((CONV_W, dr)), full2((1, dr)),
                  full2((2, dr, 2 * dr)), full2((2, 1, 2 * dr)), full2((2, 1, dr)),
                  full2((tc, tc)), full2((tc, tc))],
        out_specs=[pl.BlockSpec((1, tc, dr), lambda i, p, j: (i, jnp.where(p == 0, last, last - j), 0)),
                   pl.BlockSpec((1, SUBLANES, dr), lambda i, p, j: (i, 0, 0))],
        out_shape=[jax.ShapeDtypeStruct((b, s, dr), BF16),
                   jax.ShapeDtypeStruct((b, SUBLANES, dr), F32)],
        scratch_shapes=[pltpu.VMEM((s, dr), F32), pltpu.VMEM((s, dr), F32),
                        chunk_buf, chunk_buf, chunk_buf, chunk_buf,
                        pltpu.VMEM((SUBLANES, dr), F32), pltpu.VMEM((SUBLANES, dr), F32)],
        compiler_params=_cparams(("arbitrary", "arbitrary", "arbitrary")),
        name="rg_scan",
    )(xs, h0, conv_w, conv_b.reshape(1, dr), wcat, bcat, lam, jnp.asarray(pm, BF16), jnp.asarray(pm.T, BF16))


def _f1_kernel(d_ref, x_ref, y_ref):
    y_ref[0] = jnp.dot(d_ref[...], x_ref[0], preferred_element_type=F32).astype(BF16)


def _fourier_stage1(fv, d2, tl):
    b, r, n = fv.shape
    return pl.pallas_call(
        _f1_kernel,
        grid=(b, n // tl),
        in_specs=[pl.BlockSpec((2 * r, r), lambda i, l: (0, 0)),
                  pl.BlockSpec((1, r, tl), lambda i, l: (i, 0, l))],
        out_specs=pl.BlockSpec((1, 2 * r, tl), lambda i, l: (i, 0, l)),
        out_shape=jax.ShapeDtypeStruct((b, 2 * r, n), BF16),
        compiler_params=_cparams(("parallel", "arbitrary")),
        name="fourier_stage1",
    )(d2, fv)


def _f2_kernel(y_ref, e_ref, wcs_ref, g_ref, o_ref, obuf, *, kb, df):
    zr, zi = [], []
    for q in range(kb):
        yk = jnp.concatenate([y_ref[0, 0, q], y_ref[0, 1, q]], axis=0)
        z = jnp.dot(e_ref[q], yk, preferred_element_type=F32)
        zr.append(z[:GRID_W])
        zi.append(z[GRID_W:])
    zr = jnp.concatenate(zr, axis=0).astype(BF16)
    zi = jnp.concatenate(zi, axis=0).astype(BF16)
    gd = wcs_ref.shape[2]
    o = jnp.concatenate(
        [jnp.dot(jnp.concatenate([zr[:, g * gd:(g + 1) * gd], zi[:, g * gd:(g + 1) * gd]], axis=1), wcs_ref[g],
                 preferred_element_type=F32) for g in range(df // gd)], axis=1)
    on = _rms(o, g_ref[...])

    def store_o(pos, c, tile):
        o_ref[0, pos, :, c * LANES:(c + 1) * LANES] = tile.astype(BF16)
    _pitched_store(on, obuf, 0)
    _pitched_gather(obuf, kb, store_o)


def _fourier_stage2(y5, etab, wcs, g, kb):
    b, _, r, w, df = y5.shape
    return pl.pallas_call(
        functools.partial(_f2_kernel, kb=kb, df=df),
        grid=(b, r // kb),
        in_specs=[pl.BlockSpec((1, 2, kb, w, df), lambda i, k: (i, 0, k, 0, 0)),
                  pl.BlockSpec((kb, 2 * w, 2 * w), lambda i, k: (k, 0, 0)),
                  pl.BlockSpec(wcs.shape, lambda i, k: (0, 0, 0)),
                  pl.BlockSpec((1, df), lambda i, k: (0, 0))],
        out_specs=pl.BlockSpec((1, w, kb, df), lambda i, k: (i, 0, k, 0)),
        out_shape=jax.ShapeDtypeStruct((b, w, r, df), BF16),
        scratch_shapes=[pltpu.VMEM((df // LANES, kb * _seg_pitch(GRID_W), LANES), F32)],
        compiler_params=_cparams(("parallel", "arbitrary")),
        name="fourier_stage2",
    )(y5, etab, wcs, g.reshape(1, df))


def _stage_m_kernel(fn_ref, hs_ref, gg_ref, x_ref, gtm_ref, shf_ref, scf_ref, gr_ref, gffn_ref,
                    wo_ref, wr_ref, br_ref, j_ref, x1_ref, h2_ref, idx_ref, gate_ref, *, df):
    tm = x_ref.shape[1]
    hs = hs_ref[0]
    blocks = []
    for r in range(tm // GRID_W):
        blk = hs[r * GRID_W:(r + 1) * GRID_W]
        if r % 2 == 1:
            blk = jnp.dot(j_ref[...], blk, preferred_element_type=F32)
        blocks.append(blk.astype(F32))
    rg = jnp.concatenate(blocks, axis=0) * gg_ref[0].astype(F32)
    rgn = _rms(rg, gr_ref[...]).astype(BF16)
    mix = jnp.dot(fn_ref[0], wo_ref[:df, :], preferred_element_type=F32)
    mix += jnp.dot(rgn, wo_ref[df:, :], preferred_element_type=F32)
    x1 = x_ref[0] + gtm_ref[0] * mix
    x1_ref[0] = x1
    h2 = _rms(x1, gffn_ref[...]) * (1.0 + scf_ref[0]) + shf_ref[0]
    h2_ref[0] = _pack_halves(h2)

    logits = _dot3_nt(wr_ref[...], h2) + br_ref[...]
    eidx = lax.broadcasted_iota(jnp.int32, logits.shape, 0)
    vals, idxs = [], []
    for _ in range(TOP_K):
        m = jnp.max(logits, axis=0, keepdims=True)
        sel = jnp.min(jnp.where(logits == m, eidx, N_EXPERTS), axis=0, keepdims=True)
        vals.append(m)
        idxs.append(sel)
        logits = jnp.where(eidx == sel, -jnp.inf, logits)
    ex = [jnp.exp(v - vals[0]) for v in vals]
    den = ex[0] + ex[1] + ex[2] + ex[3]
    for k in range(TOP_K):
        gate_ref[k:k + 1, :] = ex[k] / den
        idx_ref[k:k + 1, :] = idxs[k]


def _stage_m(fn, hs, gg, x, gt_m, sh_f, sc_f, g_out_r, g_ffn, w_out_bf, w_router_t, b_router, jmat, tm):
    b, s, d = x.shape
    df = fn.shape[2]
    dr = hs.shape[2]
    nt = s // tm
    ne = w_router_t.shape[0]
    vec = pl.BlockSpec((1, 1, d), lambda i, t: (i, 0, 0))
    half = lambda dd: pl.BlockSpec((1, tm, dd), lambda i, t: (i, t, 0))
    full = lambda shape: pl.BlockSpec(shape, lambda i, t: (0,) * len(shape))
    tok = pl.BlockSpec((TOP_K, tm), lambda i, t: (0, i * nt + t))
    return pl.pallas_call(
        functools.partial(_stage_m_kernel, df=df),
        grid=(b, nt),
        in_specs=[half(df), half(dr), half(dr), half(d), vec, vec, vec,
                  full((1, dr)), full((1, d)), full((d, d)), full((ne, d)), full((ne, 1)),
                  full((GRID_W, GRID_W))],
        out_specs=[half(d), half(d // 2), tok, tok],
        out_shape=[jax.ShapeDtypeStruct((b, s, d), F32), jax.ShapeDtypeStruct((b, s, d // 2), jnp.int32),
                   jax.ShapeDtypeStruct((TOP_K, b * s), jnp.int32),
                   jax.ShapeDtypeStruct((TOP_K, b * s), F32)],
        compiler_params=_cparams(("parallel", "arbitrary")),
        name="stage_m",
    )(fn, hs, gg, x, gt_m, sh_f, sc_f, g_out_r.reshape(1, dr), g_ffn.reshape(1, d), w_out_bf,
      w_router_t, b_router.reshape(ne, 1), jmat)


def _rank_kernel(idx_ref, tri_ref, rank_ref, cnt_ref, carry_s):
    c = pl.program_id(0)

    @pl.when(c == 0)
    def _():
        carry_s[...] = jnp.zeros_like(carry_s)

    l = tri_ref.shape[0]
    eidx = lax.broadcasted_iota(jnp.int32, (N_EXPERTS, l), 0)
    for sub in range(idx_ref.shape[1] // l):
        lanes = slice(sub * l, (sub + 1) * l)
        for k in range(TOP_K):
            onehot = eidx == idx_ref[k:k + 1, lanes]
            oh = jnp.where(onehot, 1.0, 0.0)
            prefix = jnp.dot(oh.astype(BF16), tri_ref[...], preferred_element_type=F32)
            carry = carry_s[:, 0:1]
            rank = jnp.sum(jnp.where(onehot, prefix - 1.0 + carry, 0.0), axis=0, keepdims=True)
            rank_ref[k:k + 1, lanes] = rank.astype(jnp.int32)
            carry_s[...] = carry_s[...] + jnp.sum(oh, axis=1, keepdims=True)
    cnt_ref[...] = carry_s[...].astype(jnp.int32)


def _dispatch_ranks(idx, tri, tl):
    k, t = idx.shape
    return pl.pallas_call(
        _rank_kernel,
        grid=(t // tl,),
        in_specs=[pl.BlockSpec((k, tl), lambda c: (0, c)),
                  pl.BlockSpec(tri.shape, lambda c: (0, 0))],
        out_specs=[pl.BlockSpec((k, tl), lambda c: (0, c)),
                   pl.BlockSpec((N_EXPERTS, LANES), lambda c: (0, 0))],
        out_shape=[jax.ShapeDtypeStruct((k, t), jnp.int32),
                   jax.ShapeDtypeStruct((N_EXPERTS, LANES), jnp.int32)],
        scratch_shapes=[pltpu.VMEM((N_EXPERTS, LANES), F32)],
        compiler_params=_cparams(("arbitrary",)),
        name="dispatch_ranks",
    )(idx, tri)


def _moe_kernel(be_ref, bv_ref, nu_ref, x_ref, wgu_ref, bgu_ref, wd_ref, bd_ref, o_ref, wgu_s, wd_s, *, dff):
    i = pl.program_id(0)
    h = x_ref.shape[1]

    @pl.when(jnp.logical_or(i == 0, be_ref[i] != be_ref[jnp.maximum(i - 1, 0)]))
    def _():
        wgu_s[...] = wgu_ref[0].astype(BF16)
        wd_s[...] = wd_ref[0].astype(BF16)

    def swiglu(nrows):
        rows = lax.broadcasted_iota(jnp.int32, (nrows, h), 0)
        xw = jnp.where(rows < bv_ref[i], x_ref[:nrows, :], 0)
        xa, xb = _unpack_halves(xw)
        xa = xa.astype(BF16)
        xb = xb.astype(BF16)
        acc = None
        for c in range(dff // MOE_FF_CHUNK):
            gs = slice(c * MOE_FF_CHUNK, (c + 1) * MOE_FF_CHUNK)
            us = slice(dff + c * MOE_FF_CHUNK, dff + (c + 1) * MOE_FF_CHUNK)
            g = jnp.dot(xa, wgu_s[:h, gs], preferred_element_type=F32)
            g += jnp.dot(xb, wgu_s[h:, gs], preferred_element_type=F32)
            u = jnp.dot(xa, wgu_s[:h, us], preferred_element_type=F32)
            u += jnp.dot(xb, wgu_s[h:, us], preferred_element_type=F32)
            gt = jnp.minimum(g + bgu_ref[0, :, gs], SWIGLU_LIMIT)
            up = jnp.clip(u + bgu_ref[0, :, us], -SWIGLU_LIMIT, SWIGLU_LIMIT)
            act = (up + 1.0) * (gt * _sigmoid(SWIGLU_ALPHA * gt))
            part = jnp.dot(act.astype(BF16), wd_s[gs, :], preferred_element_type=F32)
            acc = part if acc is None else acc + part
        o_ref[:nrows, :] = _pack_halves(acc + bd_ref[0])

    tmm = x_ref.shape[0]
    used = i < nu_ref[0]

    @pl.when(jnp.logical_and(used, bv_ref[i] > tmm // 2))
    def _():
        swiglu(tmm)

    @pl.when(jnp.logical_and(used, bv_ref[i] <= tmm // 2))
    def _():
        swiglu(tmm // 2)


def _moe_experts(blk_expert, blk_valid, n_used, xs, wgu, bgu, wd, bd, tmm):
    cap, h = xs.shape
    ne, d, dff2 = wgu.shape
    dff = dff2 // 2
    row_blk = lambda i, be, bv, nu: (jnp.maximum(jnp.minimum(i, nu[0] - 1), 0), 0)
    wsel = lambda i, be, bv, nu: (be[i], 0, 0)
    grid_spec = pltpu.PrefetchScalarGridSpec(
        num_scalar_prefetch=3,
        grid=(cap // tmm,),
        in_specs=[pl.BlockSpec((tmm, h), row_blk),
                  pl.BlockSpec((1, d, dff2), wsel),
                  pl.BlockSpec((1, 1, dff2), wsel),
                  pl.BlockSpec((1, dff, d), wsel),
                  pl.BlockSpec((1, 1, d), wsel)],
        out_specs=pl.BlockSpec((tmm, h), row_blk),
        scratch_shapes=[pltpu.VMEM((d, dff2), BF16), pltpu.VMEM((dff, d), BF16)],
    )
    return pl.pallas_call(
        functools.partial(_moe_kernel, dff=dff),
        grid_spec=grid_spec,
        out_shape=jax.ShapeDtypeStruct((cap, h), jnp.int32),
        compiler_params=_cparams(("arbitrary",)),
        name="moe_experts",
    )(blk_expert, blk_valid, n_used, xs, wgu, bgu.reshape(ne, 1, dff2), wd, bd.reshape(ne, 1, d))


SC_CHUNK = 64
SC_ID_CHUNK = 128


def _sc_workers():
    info = plsc.get_sparse_core_info()
    return info.num_cores, info.num_subcores


def _sc_scatter(rows, dest, out_rows=None, out_ref=None):
    n, w = rows.shape
    nc, ns = _sc_workers()
    per_w = n // (nc * ns)
    assert per_w % SC_CHUNK == 0
    mesh = plsc.VectorSubcoreMesh(core_axis_name="c", subcore_axis_name="s")
    out_type = () if out_ref is not None else jax.ShapeDtypeStruct((out_rows, w), jnp.int32)

    @functools.partial(
        pl.kernel, mesh=mesh, out_type=out_type,
        scratch_types=[pltpu.VMEM((SC_CHUNK,), jnp.int32),
                       pltpu.VMEM((SC_CHUNK, w), jnp.int32),
                       pltpu.SemaphoreType.DMA],
    )
    def scatter_rows(rows_hbm, dest_hbm, out_hbm, idx_v, rows_v, sem):
        base = (lax.axis_index("s") * nc + lax.axis_index("c")) * per_w

        @pl.loop(0, per_w // SC_CHUNK)
        def _(j):
            off = pl.multiple_of(base + j * SC_CHUNK, SC_CHUNK)
            pltpu.sync_copy(rows_hbm.at[pl.ds(off, SC_CHUNK)], rows_v)
            pltpu.sync_copy(dest_hbm.at[pl.ds(off, SC_CHUNK)], idx_v)
            pltpu.async_copy(rows_v, out_hbm.at[idx_v], sem).wait()

    if out_ref is not None:
        scatter_rows(rows, dest, out_ref)
        return None
    return scatter_rows(rows, dest)


def _sc_scatter_ids(dest, out_rows):
    n = dest.shape[0]
    info = plsc.get_sparse_core_info()
    nc, ns, nl = info.num_cores, info.num_subcores, info.num_lanes
    per_w = n // (nc * ns)
    chunk = SC_ID_CHUNK
    assert per_w % (2 * chunk) == 0
    mesh = plsc.VectorSubcoreMesh(core_axis_name="c", subcore_axis_name="s")
    idx_t = pltpu.VMEM((chunk,), jnp.int32)
    rows_t = pltpu.VMEM((chunk, LANES), jnp.int32)

    @functools.partial(
        pl.kernel, mesh=mesh,
        out_type=jax.ShapeDtypeStruct((out_rows, LANES), jnp.int32),
        scratch_types=[idx_t, idx_t, rows_t, rows_t, pltpu.SemaphoreType.DMA, pltpu.SemaphoreType.DMA],
    )
    def scatter_ids(dest_hbm, out_hbm, idx_a, idx_b, rows_a, rows_b, sem_a, sem_b):
        base = (lax.axis_index("s") * nc + lax.axis_index("c")) * per_w

        def start(off, idx_v, rows_v, sem):
            for r in range(chunk):
                rows_v[r, pl.ds(0, nl)] = jnp.zeros((nl,), jnp.int32) + (off + r)
            pltpu.sync_copy(dest_hbm.at[pl.ds(off, chunk)], idx_v)
            return pltpu.async_copy(rows_v, out_hbm.at[idx_v], sem)

        @pl.loop(0, per_w // (2 * chunk))
        def _(j):
            off = pl.multiple_of(base + j * (2 * chunk), 2 * chunk)
            copy_a = start(off, idx_a, rows_a, sem_a)
            copy_b = start(off + chunk, idx_b, rows_b, sem_b)
            copy_a.wait()
            copy_b.wait()

    return scatter_ids(dest)


def _sc_gather(table, idx):
    n = idx.shape[0]
    w = table.shape[1]
    nc, ns = _sc_workers()
    per_w = n // (nc * ns)
    assert per_w % SC_CHUNK == 0
    mesh = plsc.VectorSubcoreMesh(core_axis_name="c", subcore_axis_name="s")

    @functools.partial(
        pl.kernel, mesh=mesh,
        out_type=jax.ShapeDtypeStruct((n, w), jnp.int32),
        scratch_types=[pltpu.VMEM((SC_CHUNK,), jnp.int32),
                       pltpu.VMEM((SC_CHUNK, w), jnp.int32),
                       pltpu.SemaphoreType.DMA],
    )
    def gather_rows(table_hbm, idx_hbm, out_hbm, idx_v, rows_v, sem):
        base = (lax.axis_index("s") * nc + lax.axis_index("c")) * per_w

        @pl.loop(0, per_w // SC_CHUNK)
        def _(j):
            off = pl.multiple_of(base + j * SC_CHUNK, SC_CHUNK)
            pltpu.sync_copy(idx_hbm.at[pl.ds(off, SC_CHUNK)], idx_v)
            pltpu.async_copy(table_hbm.at[idx_v], rows_v, sem).wait()
            pltpu.sync_copy(rows_v, out_hbm.at[pl.ds(off, SC_CHUNK)])

    return gather_rows(table, idx)


def _combine_kernel(x1_ref, y_ref, gate_ref, gtf_ref, g_ref, o_ref):
    h = y_ref.shape[3]
    gates = gate_ref[0]
    moe_a = moe_b = None
    for k in range(TOP_K):
        ya, yb = _unpack_halves(y_ref[k, 0])
        gk = gates[:, k:k + 1]
        moe_a = gk * ya if k == 0 else moe_a + gk * ya
        moe_b = gk * yb if k == 0 else moe_b + gk * yb
    za = x1_ref[0, :, :h] + gtf_ref[0, :, :h] * moe_a
    zb = x1_ref[0, :, h:] + gtf_ref[0, :, h:] * moe_b
    ms = (jnp.sum(za * za, axis=-1, keepdims=True) + jnp.sum(zb * zb, axis=-1, keepdims=True)) / (2 * h)
    inv = lax.rsqrt(ms + EPS)
    o_ref[0, :, :h] = za * inv * g_ref[:, :h]
    o_ref[0, :, h:] = zb * inv * g_ref[:, h:]


def _combine(x1, yk, gates_t, gt_f, g_final, tm):
    b, s, d = x1.shape
    h = yk.shape[3]
    return pl.pallas_call(
        _combine_kernel,
        grid=(b, s // tm),
        in_specs=[pl.BlockSpec((1, tm, d), lambda i, t: (i, t, 0)),
                  pl.BlockSpec((TOP_K, 1, tm, h), lambda i, t: (0, i, t, 0)),
                  pl.BlockSpec((1, tm, TOP_K), lambda i, t: (i, t, 0)),
                  pl.BlockSpec((1, 1, d), lambda i, t: (i, 0, 0)),
                  pl.BlockSpec((1, d), lambda i, t: (0, 0))],
        out_specs=pl.BlockSpec((1, tm, d), lambda i, t: (i, t, 0)),
        out_shape=jax.ShapeDtypeStruct((b, s, d), F32),
        compiler_params=_cparams(("parallel", "arbitrary")),
        name="combine",
    )(x1, yk, gates_t, gt_f, g_final.reshape(1, d))


def _dft_tables(rows, gd):
    seq = rows * GRID_W
    n = np.arange(rows)
    ang1 = 2.0 * np.pi * np.outer(n, n) / rows
    d2 = np.concatenate([np.cos(ang1), -np.sin(ang1)], axis=0)
    k1 = np.arange(rows)[:, None, None]
    k2 = np.arange(GRID_W)[None, :, None]
    n2 = np.arange(GRID_W)[None, None, :]
    ang2 = 2.0 * np.pi * ((n2 * (k1 + rows * k2)) % seq) / seq
    ec, es = np.cos(ang2), np.sin(ang2)
    etab = np.concatenate([np.concatenate([ec, es], axis=2),
                           np.concatenate([-es, ec], axis=2)], axis=1)
    c = np.arange(gd)
    angc = 2.0 * np.pi * np.outer(c, c) / gd
    scale = 1.0 / np.sqrt(seq * gd)
    return (jnp.asarray(d2, BF16), jnp.asarray(etab, BF16),
            jnp.asarray(np.cos(angc) * scale, F32), jnp.asarray(np.sin(angc) * scale, F32))


def _block_diag(w):
    h, i, o = w.shape
    eye = jnp.eye(h, dtype=w.dtype)
    return (eye[:, None, :, None] * w[:, :, None, :]).reshape(h * i, h * o)


def kernel(x, c, ctx, c_ctx, w_mod, b_mod, g_norm_mix, g_norm_ffn, w_in, w_fourier, conv_w, conv_b,
           rg_w_a, rg_b_a, rg_w_x, rg_b_x, rg_lam, g_out_fourier, g_out_rg, w_out, w_router, b_router,
           w_gate_up, b_gate_up, w_down, b_down, g_final):
    assert w_mod.shape[0] == 1, "single-layer stack only"
    b, s, d = x.shape
    df = w_fourier.shape[1] * w_fourier.shape[2]
    dr = conv_w.shape[2]
    gd = w_fourier.shape[2]
    rows = s // GRID_W
    t = b * s
    ne = w_router.shape[2]

    mrows = -(-(b + 1) // SUBLANES) * SUBLANES
    cond = jnp.zeros((mrows, d), F32).at[:b].set(c).at[b].set(c_ctx)
    mod = _adaln(cond, w_mod[0], b_mod[0])
    sh_m, sc_m, gt_m, sh_f, sc_f, gt_f = [mod[:b, k * d:(k + 1) * d].reshape(b, 1, d) for k in range(N_MOD)]
    csh_m = mod[b:b + 1, 0:d].reshape(1, 1, d)
    csc_m = mod[b:b + 1, d:2 * d].reshape(1, 1, d)

    tm = min(s, TOKEN_TILE)
    d2, etab, cmat, smat = _dft_tables(rows, gd)
    jmat = jnp.asarray(np.eye(GRID_W)[::-1].copy(), BF16)

    w_in_bf = w_in[0].astype(BF16)
    f, xs, gg = _stage_b(x, sh_m, sc_m, g_norm_mix[0], w_in_bf, jmat, df, dr, tm=tm)
    xr_ctx = _stage_b_ctx(ctx, csh_m, csc_m, g_norm_mix[0], w_in_bf[:, df:df + dr])

    wcat = jnp.stack([jnp.concatenate([_block_diag(rg_w_a[0, dd]), _block_diag(rg_w_x[0, dd])], axis=1)
                      for dd in range(2)]).astype(BF16)
    bcat = jnp.concatenate([rg_b_a[0], rg_b_x[0]], axis=1).reshape(2, 1, 2 * dr)
    lam = rg_lam[0].reshape(2, 1, dr)
    h0 = jnp.zeros((b, SUBLANES, dr), F32)
    _, hfin_ctx = _rg_scan(xr_ctx, h0, conv_w[0], conv_b[0], wcat, bcat, lam, tc=ctx.shape[1])
    hs, _ = _rg_scan(xs, hfin_ctx, conv_w[0], conv_b[0], wcat, bcat, lam, tc=min(s, SCAN_CHUNK))

    cw, sw = _fold_fourier(cmat, smat, w_fourier[0])
    wcs = jnp.concatenate([cw, sw], axis=1).astype(BF16)
    y = _fourier_stage1(f, d2, tl=min(GRID_W * df, 16384))
    fn = _fourier_stage2(y.reshape(b, 2, rows, GRID_W, df), etab, wcs, g_out_fourier[0],
                         kb=min(rows, 16))
    fn = fn.reshape(b, s, df)

    x1, h2, idx, gates = _stage_m(fn, hs, gg, x, gt_m, sh_f, sc_f, g_out_rg[0], g_norm_ffn[0],
                                  w_out[0].astype(BF16), w_router[0].T, b_router[0], jmat, tm=tm)

    tl = min(t, RANK_TILE)
    tri = jnp.asarray(np.triu(np.ones((tl, tl))), BF16)
    rank, cnt = _dispatch_ranks(idx, tri, min(t, RANK_STEP))
    counts = cnt[:, 0]
    tmm = MOE_ROW_TILE
    padded = (counts + tmm - 1) // tmm * tmm
    pad_end = jnp.cumsum(padded)
    pad_start = pad_end - padded
    eids = jnp.arange(ne, dtype=jnp.int32)
    dest = rank + jnp.sum(jnp.where(idx[:, :, None] == eids, pad_start, 0), axis=-1)
    n_blocks = -(-(t * TOP_K) // tmm) + ne
    cap = n_blocks * tmm
    n_used = (pad_end[-1] // tmm).astype(jnp.int32).reshape(1)
    blk_start = jnp.arange(n_blocks, dtype=jnp.int32) * tmm
    blk_expert = jnp.sum(blk_start[:, None] >= pad_end[None, :], axis=1).astype(jnp.int32)
    last_expert = jnp.sum(pad_end[-1] - tmm >= pad_end).astype(jnp.int32)
    blk_expert = jnp.minimum(blk_expert, last_expert)
    sel = blk_expert[:, None] == eids
    blk_first = jnp.sum(jnp.where(sel, pad_start, 0), axis=1)
    blk_count = jnp.sum(jnp.where(sel, counts, 0), axis=1)
    blk_valid = jnp.clip(blk_count - (blk_start - blk_first), 0, tmm).astype(jnp.int32)

    na = TOP_K * t
    inv = _sc_scatter_ids(dest.reshape(-1), cap)[:, 0].reshape(n_blocks, tmm)
    live = jnp.arange(tmm, dtype=jnp.int32)[None, :] < blk_valid[:, None]
    spread = jnp.arange(cap, dtype=jnp.int32).reshape(n_blocks, tmm) % t
    src_tok = jnp.where(live, inv % t, spread).reshape(-1)
    dst_row = jnp.where(live, inv, na + spread).reshape(-1)

    h2_rows = h2.reshape(t, d // 2)
    y_all = jax.empty_ref(jax.ShapeDtypeStruct(((TOP_K + 1) * t, d // 2), jnp.int32))
    unit = n_blocks // sum(MOE_PIPE)
    assert unit * sum(MOE_PIPE) == n_blocks
    blk0 = 0
    for parts in MOE_PIPE:
        nq = parts * unit
        blocks = slice(blk0, blk0 + nq)
        rows = slice(blk0 * tmm, (blk0 + nq) * tmm)
        x_q = _sc_gather(h2_rows, src_tok[rows])
        nu_q = jnp.clip(n_used - blk0, 0, nq).astype(jnp.int32)
        blk0 += nq
        y_q = _moe_experts(blk_expert[blocks], blk_valid[blocks], nu_q, x_q,
                           w_gate_up[0], b_gate_up[0], w_down[0], b_down[0], tmm)
        _sc_scatter(y_q, dst_row[rows], out_ref=y_all)
    yk = y_all[...].reshape(TOP_K + 1, b, s, d // 2)
    return _combine(x1, yk, gates.T.reshape(b, s, TOP_K), gt_f, g_final, tm)
```

```python
import functools

import numpy as np
import jax
import jax.numpy as jnp
from jax import lax
from jax.experimental import pallas as pl
from jax.experimental.pallas import tpu as pltpu
from jax.experimental.pallas import tpu_sc as plsc

GRID_W = 64
CONV_W = 4
CONV_PAD_LO = 2
RG_C = 8.0
N_EXPERTS = 32
TOP_K = 4
SWIGLU_LIMIT = 7.0
SWIGLU_ALPHA = 1.702
N_MOD = 6
EPS = 1e-6

LANES = 128
SUBLANES = 8
VMEM_LIMIT_BYTES = 56 * 1024 * 1024
TOKEN_TILE = 1024
RANK_TILE = 512
RANK_STEP = 2048
SCAN_CHUNK = 512
MOE_ROW_TILE = 512
MOE_FF_CHUNK = 512
MOE_PIPE = (1, 3, 3, 1)

F32 = jnp.float32
BF16 = jnp.bfloat16


def _cparams(sem):
    return pltpu.CompilerParams(dimension_semantics=sem, vmem_limit_bytes=VMEM_LIMIT_BYTES)


def _split_bf16(a):
    hi = a.astype(BF16)
    lo = (a - hi.astype(F32)).astype(BF16)
    return hi, lo


def _dot3(a, b):
    ah, al = _split_bf16(a)
    bh, bl = _split_bf16(b)
    out = jnp.dot(ah, bh, preferred_element_type=F32)
    out += jnp.dot(ah, bl, preferred_element_type=F32)
    out += jnp.dot(al, bh, preferred_element_type=F32)
    return out


def _dot3_nt(a, b):
    dn = (((1,), (1,)), ((), ()))
    ah, al = _split_bf16(a)
    bh, bl = _split_bf16(b)
    out = lax.dot_general(ah, bh, dn, preferred_element_type=F32)
    out += lax.dot_general(ah, bl, dn, preferred_element_type=F32)
    out += lax.dot_general(al, bh, dn, preferred_element_type=F32)
    return out


def _gelu_tanh(x):
    return 0.5 * x * (1.0 + jnp.tanh(0.7978845608028654 * (x + 0.044715 * (x * x * x))))


def _rms(x, g):
    return x * lax.rsqrt(jnp.mean(x * x, axis=-1, keepdims=True) + EPS) * g


def _pack_halves(v):
    h = v.shape[1] // 2
    hi = lax.bitcast_convert_type(v[:, :h].astype(BF16).astype(F32), jnp.uint32)
    lo = lax.bitcast_convert_type(v[:, h:].astype(BF16).astype(F32), jnp.uint32)
    return lax.bitcast_convert_type(hi | (lo >> 16), jnp.int32)


def _unpack_halves(w):
    u = lax.bitcast_convert_type(w, jnp.uint32)
    hi = lax.bitcast_convert_type(u & jnp.uint32(0xFFFF0000), F32)
    lo = lax.bitcast_convert_type(u << 16, F32)
    return hi, lo


def _adaln_kernel(c_ref, w_ref, b_ref, o_ref):
    s = c_ref[...]
    s = s * jax.nn.sigmoid(s)
    o_ref[...] = _dot3(s, w_ref[...]) + b_ref[...]


def _adaln(cond, w_mod, b_mod):
    m, d = cond.shape
    n = w_mod.shape[1]
    tn = n // N_MOD
    return pl.pallas_call(
        _adaln_kernel,
        grid=(n // tn,),
        in_specs=[pl.BlockSpec((m, d), lambda i: (0, 0)),
                  pl.BlockSpec((d, tn), lambda i: (0, i)),
                  pl.BlockSpec((1, tn), lambda i: (0, i))],
        out_specs=pl.BlockSpec((m, tn), lambda i: (0, i)),
        out_shape=jax.ShapeDtypeStruct((m, n), F32),
        compiler_params=_cparams(("arbitrary",)),
        name="adaln",
    )(cond, w_mod, b_mod.reshape(1, n))


def _fold_kernel(c_ref, s_ref, w_ref, cw_ref, sw_ref):
    w = w_ref[0]
    cw_ref[0] = _dot3(c_ref[...], w)
    sw_ref[0] = _dot3(s_ref[...], w)


def _fold_fourier(cmat, smat, w_f):
    g, gd, _ = w_f.shape
    spec_m = pl.BlockSpec((gd, gd), lambda i: (0, 0))
    spec_w = pl.BlockSpec((1, gd, gd), lambda i: (i, 0, 0))
    return pl.pallas_call(
        _fold_kernel,
        grid=(g,),
        in_specs=[spec_m, spec_m, spec_w],
        out_specs=[spec_w, spec_w],
        out_shape=[jax.ShapeDtypeStruct((g, gd, gd), F32)] * 2,
        compiler_params=_cparams(("arbitrary",)),
        name="fold_fourier",
    )(cmat, smat, w_f)


def _seg_pitch(seg_len):
    n8 = seg_len // SUBLANES
    return SUBLANES * (n8 + 1 - n8 % 2)


def _pitched_store(v, buf, blk0):
    pitch = _seg_pitch(GRID_W)
    for r in range(v.shape[0] // GRID_W):
        for c in range(v.shape[1] // LANES):
            buf[c, (blk0 + r) * pitch:(blk0 + r) * pitch + GRID_W, :] = (
                v[r * GRID_W:(r + 1) * GRID_W, c * LANES:(c + 1) * LANES])


def _pitched_gather(buf, nb, store):
    pitch = _seg_pitch(GRID_W)
    for pos in range(GRID_W):
        for c in range(buf.shape[0]):
            store(pos, c, buf[c, pl.ds(pos, nb, stride=pitch), :])


def _stage_b_kernel(x_ref, sh_ref, sc_ref, g_ref, w_ref, j_ref, f_ref, xs_ref, gg_ref, fbuf, *, df, dr):
    tm = x_ref.shape[1]
    h = _rms(x_ref[0], g_ref[...]) * (1.0 + sc_ref[0]) + sh_ref[0]
    hb = h.astype(BF16)
    gr = jnp.dot(hb, w_ref[:, df + dr:], preferred_element_type=F32)
    gg_ref[0] = _gelu_tanh(gr).astype(BF16)
    _pitched_store(jnp.dot(hb, w_ref[:, :df], preferred_element_type=F32), fbuf, 0)

    def store_f(pos, c, tile):
        f_ref[0, :, pos * df + c * LANES:pos * df + (c + 1) * LANES] = tile.astype(BF16)
    _pitched_gather(fbuf, tm // GRID_W, store_f)
    xr = jnp.dot(hb, w_ref[:, df:df + dr], preferred_element_type=F32).astype(BF16)
    for r in range(tm // GRID_W):
        blk = xr[r * GRID_W:(r + 1) * GRID_W]
        if r % 2 == 1:
            blk = jnp.dot(j_ref[...], blk, preferred_element_type=F32).astype(BF16)
        xs_ref[0, r * GRID_W:(r + 1) * GRID_W, :] = blk


def _stage_b(x, shift, scale, g, w_in_bf, jmat, df, dr, tm):
    b, s, d = x.shape
    n = w_in_bf.shape[1]
    vec = pl.BlockSpec((1, 1, d), lambda i, t: (i, 0, 0))
    out = pl.BlockSpec((1, tm, df), lambda i, t: (i, t, 0))
    nb = tm // GRID_W
    tok = jax.ShapeDtypeStruct((b, s, df), BF16)
    return pl.pallas_call(
        functools.partial(_stage_b_kernel, df=df, dr=dr),
        grid=(b, s // tm),
        in_specs=[pl.BlockSpec((1, tm, d), lambda i, t: (i, t, 0)), vec, vec,
                  pl.BlockSpec((1, d), lambda i, t: (0, 0)),
                  pl.BlockSpec((d, n), lambda i, t: (0, 0)),
                  pl.BlockSpec((GRID_W, GRID_W), lambda i, t: (0, 0))],
        out_specs=[pl.BlockSpec((1, nb, GRID_W * df), lambda i, t: (i, t, 0)), out, out],
        out_shape=[jax.ShapeDtypeStruct((b, s // GRID_W, GRID_W * df), BF16), tok, tok],
        scratch_shapes=[pltpu.VMEM((df // LANES, nb * _seg_pitch(GRID_W), LANES), F32)],
        compiler_params=_cparams(("parallel", "arbitrary")),
        name="stage_b",
    )(x, shift, scale, g.reshape(1, d), w_in_bf, jmat)


def _stage_b_ctx_kernel(x_ref, sh_ref, sc_ref, g_ref, w_ref, xr_ref):
    h = _rms(x_ref[0], g_ref[...]) * (1.0 + sc_ref[0]) + sh_ref[0]
    xr_ref[0] = jnp.dot(h.astype(BF16), w_ref[...], preferred_element_type=F32).astype(BF16)


def _stage_b_ctx(ctx, shift, scale, g, w_xr_bf):
    b, s, d = ctx.shape
    dr = w_xr_bf.shape[1]
    vec = pl.BlockSpec((1, 1, d), lambda i: (0, 0, 0))
    return pl.pallas_call(
        _stage_b_ctx_kernel,
        grid=(b,),
        in_specs=[pl.BlockSpec((1, s, d), lambda i: (i, 0, 0)), vec, vec,
                  pl.BlockSpec((1, d), lambda i: (0, 0)),
                  pl.BlockSpec((d, dr), lambda i: (0, 0))],
        out_specs=pl.BlockSpec((1, s, dr), lambda i: (i, 0, 0)),
        out_shape=jax.ShapeDtypeStruct((b, s, dr), BF16),
        compiler_params=_cparams(("arbitrary",)),
        name="stage_b_ctx",
    )(ctx, shift, scale, g.reshape(1, d), w_xr_bf)


HALO = 16


def _sigmoid(x):
    return 0.5 * jnp.tanh(0.5 * x) + 0.5


def _rg_kernel(xs_ref, h0_ref, cw_ref, cb_ref, w_ref, b_ref, lam_ref, pm_ref, pmt_ref, out_ref, hfin_ref,
               hf_s, xc_s, a_s, u_s, hl_s, p_s, c_s, hc_s, *, tc, nchunk, seq, dr):
    p = pl.program_id(1)
    j = pl.program_id(2)
    cidx = jnp.where(p == 0, j, nchunk - 1 - j)
    start = pl.multiple_of(cidx * tc, tc)
    nseg = SUBLANES
    sl = tc // nseg
    sub = lax.broadcasted_iota(jnp.int32, (nseg, dr), 0)

    @pl.when(p == 0)
    def _():
        xp = jnp.dot(pm_ref[...], xs_ref[0, pl.ds(start, tc), :], preferred_element_type=F32)
        pstart = pl.multiple_of(jnp.maximum(start - HALO, 0), HALO)
        nstart = pl.multiple_of(jnp.minimum(start + tc, seq - HALO), HALO)
        prev = xs_ref[0, pl.ds(pstart, HALO), :].astype(F32)
        nxt = xs_ref[0, pl.ds(nstart, HALO), :].astype(F32)
        prev = jnp.where(cidx > 0, prev, 0.0)
        nxt = jnp.where(cidx < nchunk - 1, nxt, 0.0)
        tm2 = jnp.where(sub == 0, prev[HALO - 2:HALO - 1], pltpu.roll(xp[(sl - 2) * nseg:(sl - 1) * nseg], 1, 0))
        tm1 = jnp.where(sub == 0, prev[HALO - 1:HALO], pltpu.roll(xp[(sl - 1) * nseg:sl * nseg], 1, 0))
        tp1 = jnp.where(sub == nseg - 1, nxt[0:1], pltpu.roll(xp[0:nseg], nseg - 1, 0))
        ext = jnp.concatenate([tm2, tm1, xp, tp1], axis=0)
        xc = cb_ref[...] + cw_ref[0:1, :] * ext[0:tc]
        for k in range(1, CONV_W):
            xc = xc + cw_ref[k:k + 1, :] * ext[k * nseg:k * nseg + tc]
        xc_s[pl.ds(start, tc), :] = xc

    xc = xc_s[pl.ds(start, tc), :]
    gates = jnp.dot(xc.astype(BF16), w_ref[p], preferred_element_type=F32) + b_ref[p]
    i = _sigmoid(gates[:, dr:])
    half_c = (-0.5 * RG_C) * jax.nn.softplus(-lam_ref[p])
    log_a = half_c * jnp.tanh(0.5 * gates[:, :dr]) + half_c
    a = jnp.exp(log_a)
    a_s[...] = a
    w = -jnp.tanh(log_a) * (1.0 + a * a)
    u_s[...] = jnp.where(w > 0.0, w * lax.rsqrt(w), 0.0) * (i * xc)

    @pl.when(jnp.logical_and(p == 0, j == 0))
    def _():
        hfin_ref[...] = jnp.zeros_like(hfin_ref)

    @pl.when(j == 0)
    def _():
        hc_s[0:1, :] = h0_ref[0, pl.ds(p, 1), :]

    def segment_scan(reverse):
        def body(q, carry):
            t = (sl - 1 - q) if reverse else q
            rows = pl.ds(pl.multiple_of(t * nseg, nseg), nseg)
            h, pr = carry
            av = a_s[rows, :]
            h = av * h + u_s[rows, :]
            pr = av * pr
            hl_s[rows, :] = h
            p_s[rows, :] = pr
            return h, pr
        h_end, p_end = lax.fori_loop(0, sl, body, (jnp.zeros((nseg, dr), F32), jnp.ones((nseg, dr), F32)),
                                     unroll=4)
        carry = hc_s[0:1, :]
        for g in (range(nseg - 1, -1, -1) if reverse else range(nseg)):
            c_s[g:g + 1, :] = carry
            carry = h_end[g:g + 1, :] + p_end[g:g + 1, :] * carry
        hc_s[0:1, :] = carry
        return c_s[...]

    def corrected(cin):
        h = hl_s[...].reshape(sl, nseg, dr) + p_s[...].reshape(sl, nseg, dr) * cin[None]
        return h.reshape(tc, dr)

    @pl.when(p == 0)
    def _():
        hf_s[pl.ds(start, tc), :] = corrected(segment_scan(False))

    @pl.when(p == 1)
    def _():
        tot = corrected(segment_scan(True)) + hf_s[pl.ds(start, tc), :]
        out_ref[0] = jnp.dot(pmt_ref[...], tot.astype(BF16), preferred_element_type=F32).astype(BF16)

    @pl.when(j == nchunk - 1)
    def _():
        hfin_ref[0, pl.ds(p, 1), :] = hc_s[0:1, :]


def _rg_scan(xs, h0, conv_w, conv_b, wcat, bcat, lam, tc):
    b, s, dr = xs.shape
    nchunk = s // tc
    last = nchunk - 1
    sl = tc // SUBLANES
    src = (np.arange(tc) % SUBLANES) * sl + np.arange(tc) // SUBLANES
    pm = np.zeros((tc, tc), np.float32)
    pm[np.arange(tc), src] = 1.0
    chunk_buf = pltpu.VMEM((tc, dr), F32)
    full2 = lambda shape: pl.BlockSpec(shape, lambda i, p, j: (0,) * len(shape))
    return pl.pallas_call(
        functools.partial(_rg_kernel, tc=tc, nchunk=nchunk, seq=s, dr=dr),
        grid=(b, 2, nchunk),
        in_specs=[pl.BlockSpec((1, s, dr), lambda i, p, j: (i, 0, 0)),
                  pl.BlockSpec((1, SUBLANES, dr), lambda i, p, j: (i, 0, 0)),
                  full2((CONV_W, dr)), full2((1, dr)),
                  full2((2, dr, 2 * dr)), full2((2, 1, 2 * dr)), full2((2, 1, dr)),
                  full2((tc, tc)), full2((tc, tc))],
        out_specs=[pl.BlockSpec((1, tc, dr), lambda i, p, j: (i, jnp.where(p == 0, last, last - j), 0)),
                   pl.BlockSpec((1, SUBLANES, dr), lambda i, p, j: (i, 0, 0))],
        out_shape=[jax.ShapeDtypeStruct((b, s, dr), BF16),
                   jax.ShapeDtypeStruct((b, SUBLANES, dr), F32)],
        scratch_shapes=[pltpu.VMEM((s, dr), F32), pltpu.VMEM((s, dr), F32),
                        chunk_buf, chunk_buf, chunk_buf, chunk_buf,
                        pltpu.VMEM((SUBLANES, dr), F32), pltpu.VMEM((SUBLANES, dr), F32)],
        compiler_params=_cparams(("arbitrary", "arbitrary", "arbitrary")),
        name="rg_scan",
    )(xs, h0, conv_w, conv_b.reshape(1, dr), wcat, bcat, lam, jnp.asarray(pm, BF16), jnp.asarray(pm.T, BF16))


def _f1_kernel(d_ref, x_ref, y_ref):
    y_ref[0] = jnp.dot(d_ref[...], x_ref[0], preferred_element_type=F32).astype(BF16)


def _fourier_stage1(fv, d2, tl):
    b, r, n = fv.shape
    return pl.pallas_call(
        _f1_kernel,
        grid=(b, n // tl),
        in_specs=[pl.BlockSpec((2 * r, r), lambda i, l: (0, 0)),
                  pl.BlockSpec((1, r, tl), lambda i, l: (i, 0, l))],
        out_specs=pl.BlockSpec((1, 2 * r, tl), lambda i, l: (i, 0, l)),
        out_shape=jax.ShapeDtypeStruct((b, 2 * r, n), BF16),
        compiler_params=_cparams(("parallel", "arbitrary")),
        name="fourier_stage1",
    )(d2, fv)


def _f2_kernel(y_ref, e_ref, wcs_ref, g_ref, o_ref, obuf, *, kb, df):
    zr, zi = [], []
    for q in range(kb):
        yk = jnp.concatenate([y_ref[0, 0, q], y_ref[0, 1, q]], axis=0)
        z = jnp.dot(e_ref[q], yk, preferred_element_type=F32)
        zr.append(z[:GRID_W])
        zi.append(z[GRID_W:])
    zr = jnp.concatenate(zr, axis=0).astype(BF16)
    zi = jnp.concatenate(zi, axis=0).astype(BF16)
    gd = wcs_ref.shape[2]
    o = jnp.concatenate(
        [jnp.dot(jnp.concatenate([zr[:, g * gd:(g + 1) * gd], zi[:, g * gd:(g + 1) * gd]], axis=1), wcs_ref[g],
                 preferred_element_type=F32) for g in range(df // gd)], axis=1)
    on = _rms(o, g_ref[...])

    def store_o(pos, c, tile):
        o_ref[0, pos, :, c * LANES:(c + 1) * LANES] = tile.astype(BF16)
    _pitched_store(on, obuf, 0)
    _pitched_gather(obuf, kb, store_o)


def _fourier_stage2(y5, etab, wcs, g, kb):
    b, _, r, w, df = y5.shape
    return pl.pallas_call(
        functools.partial(_f2_kernel, kb=kb, df=df),
        grid=(b, r // kb),
        in_specs=[pl.BlockSpec((1, 2, kb, w, df), lambda i, k: (i, 0, k, 0, 0)),
                  pl.BlockSpec((kb, 2 * w, 2 * w), lambda i, k: (k, 0, 0)),
                  pl.BlockSpec(wcs.shape, lambda i, k: (0, 0, 0)),
                  pl.BlockSpec((1, df), lambda i, k: (0, 0))],
        out_specs=pl.BlockSpec((1, w, kb, df), lambda i, k: (i, 0, k, 0)),
        out_shape=jax.ShapeDtypeStruct((b, w, r, df), BF16),
        scratch_shapes=[pltpu.VMEM((df // LANES, kb * _seg_pitch(GRID_W), LANES), F32)],
        compiler_params=_cparams(("parallel", "arbitrary")),
        name="fourier_stage2",
    )(y5, etab, wcs, g.reshape(1, df))


def _stage_m_kernel(fn_ref, hs_ref, gg_ref, x_ref, gtm_ref, shf_ref, scf_ref, gr_ref, gffn_ref,
                    wo_ref, wr_ref, br_ref, j_ref, x1_ref, h2_ref, idx_ref, gate_ref, *, df):
    tm = x_ref.shape[1]
    hs = hs_ref[0]
    blocks = []
    for r in range(tm // GRID_W):
        blk = hs[r * GRID_W:(r + 1) * GRID_W]
        if r % 2 == 1:
            blk = jnp.dot(j_ref[...], blk, preferred_element_type=F32)
        blocks.append(blk.astype(F32))
    rg = jnp.concatenate(blocks, axis=0) * gg_ref[0].astype(F32)
    rgn = _rms(rg, gr_ref[...]).astype(BF16)
    mix = jnp.dot(fn_ref[0], wo_ref[:df, :], preferred_element_type=F32)
    mix += jnp.dot(rgn, wo_ref[df:, :], preferred_element_type=F32)
    x1 = x_ref[0] + gtm_ref[0] * mix
    x1_ref[0] = x1
    h2 = _rms(x1, gffn_ref[...]) * (1.0 + scf_ref[0]) + shf_ref[0]
    h2_ref[0] = _pack_halves(h2)

    logits = _dot3_nt(wr_ref[...], h2) + br_ref[...]
    eidx = lax.broadcasted_iota(jnp.int32, logits.shape, 0)
    vals, idxs = [], []
    for _ in range(TOP_K):
        m = jnp.max(logits, axis=0, keepdims=True)
        sel = jnp.min(jnp.where(logits == m, eidx, N_EXPERTS), axis=0, keepdims=True)
        vals.append(m)
        idxs.append(sel)
        logits = jnp.where(eidx == sel, -jnp.inf, logits)
    ex = [jnp.exp(v - vals[0]) for v in vals]
    den = ex[0] + ex[1] + ex[2] + ex[3]
    for k in range(TOP_K):
        gate_ref[k:k + 1, :] = ex[k] / den
        idx_ref[k:k + 1, :] = idxs[k]


def _stage_m(fn, hs, gg, x, gt_m, sh_f, sc_f, g_out_r, g_ffn, w_out_bf, w_router_t, b_router, jmat, tm):
    b, s, d = x.shape
    df = fn.shape[2]
    dr = hs.shape[2]
    nt = s // tm
    ne = w_router_t.shape[0]
    vec = pl.BlockSpec((1, 1, d), lambda i, t: (i, 0, 0))
    half = lambda dd: pl.BlockSpec((1, tm, dd), lambda i, t: (i, t, 0))
    full = lambda shape: pl.BlockSpec(shape, lambda i, t: (0,) * len(shape))
    tok = pl.BlockSpec((TOP_K, tm), lambda i, t: (0, i * nt + t))
    return pl.pallas_call(
        functools.partial(_stage_m_kernel, df=df),
        grid=(b, nt),
        in_specs=[half(df), half(dr), half(dr), half(d), vec, vec, vec,
                  full((1, dr)), full((1, d)), full((d, d)), full((ne, d)), full((ne, 1)),
                  full((GRID_W, GRID_W))],
        out_specs=[half(d), half(d // 2), tok, tok],
        out_shape=[jax.ShapeDtypeStruct((b, s, d), F32), jax.ShapeDtypeStruct((b, s, d // 2), jnp.int32),
                   jax.ShapeDtypeStruct((TOP_K, b * s), jnp.int32),
                   jax.ShapeDtypeStruct((TOP_K, b * s), F32)],
        compiler_params=_cparams(("parallel", "arbitrary")),
        name="stage_m",
    )(fn, hs, gg, x, gt_m, sh_f, sc_f, g_out_r.reshape(1, dr), g_ffn.reshape(1, d), w_out_bf,
      w_router_t, b_router.reshape(ne, 1), jmat)


def _rank_kernel(idx_ref, tri_ref, rank_ref, cnt_ref, carry_s):
    c = pl.program_id(0)

    @pl.when(c == 0)
    def _():
        carry_s[...] = jnp.zeros_like(carry_s)

    l = tri_ref.shape[0]
    eidx = lax.broadcasted_iota(jnp.int32, (N_EXPERTS, l), 0)
    for sub in range(idx_ref.shape[1] // l):
        lanes = slice(sub * l, (sub + 1) * l)
        for k in range(TOP_K):
            onehot = eidx == idx_ref[k:k + 1, lanes]
            oh = jnp.where(onehot, 1.0, 0.0)
            prefix = jnp.dot(oh.astype(BF16), tri_ref[...], preferred_element_type=F32)
            carry = carry_s[:, 0:1]
            rank = jnp.sum(jnp.where(onehot, prefix - 1.0 + carry, 0.0), axis=0, keepdims=True)
            rank_ref[k:k + 1, lanes] = rank.astype(jnp.int32)
            carry_s[...] = carry_s[...] + jnp.sum(oh, axis=1, keepdims=True)
    cnt_ref[...] = carry_s[...].astype(jnp.int32)


def _dispatch_ranks(idx, tri, tl):
    k, t = idx.shape
    return pl.pallas_call(
        _rank_kernel,
        grid=(t // tl,),
        in_specs=[pl.BlockSpec((k, tl), lambda c: (0, c)),
                  pl.BlockSpec(tri.shape, lambda c: (0, 0))],
        out_specs=[pl.BlockSpec((k, tl), lambda c: (0, c)),
                   pl.BlockSpec((N_EXPERTS, LANES), lambda c: (0, 0))],
        out_shape=[jax.ShapeDtypeStruct((k, t), jnp.int32),
                   jax.ShapeDtypeStruct((N_EXPERTS, LANES), jnp.int32)],
        scratch_shapes=[pltpu.VMEM((N_EXPERTS, LANES), F32)],
        compiler_params=_cparams(("arbitrary",)),
        name="dispatch_ranks",
    )(idx, tri)


def _moe_kernel(be_ref, bv_ref, nu_ref, x_ref, wgu_ref, bgu_ref, wd_ref, bd_ref, o_ref, wgu_s, wd_s, *, dff):
    i = pl.program_id(0)
    h = x_ref.shape[1]

    @pl.when(jnp.logical_or(i == 0, be_ref[i] != be_ref[jnp.maximum(i - 1, 0)]))
    def _():
        wgu_s[...] = wgu_ref[0].astype(BF16)
        wd_s[...] = wd_ref[0].astype(BF16)

    @pl.when(i < nu_ref[0])
    def _():
        rows = lax.broadcasted_iota(jnp.int32, x_ref.shape, 0)
        xw = jnp.where(rows < bv_ref[i], x_ref[...], 0)
        xa, xb = _unpack_halves(xw)
        xa = xa.astype(BF16)
        xb = xb.astype(BF16)
        acc = None
        for c in range(dff // MOE_FF_CHUNK):
            gs = slice(c * MOE_FF_CHUNK, (c + 1) * MOE_FF_CHUNK)
            us = slice(dff + c * MOE_FF_CHUNK, dff + (c + 1) * MOE_FF_CHUNK)
            g = jnp.dot(xa, wgu_s[:h, gs], preferred_element_type=F32)
            g += jnp.dot(xb, wgu_s[h:, gs], preferred_element_type=F32)
            u = jnp.dot(xa, wgu_s[:h, us], preferred_element_type=F32)
            u += jnp.dot(xb, wgu_s[h:, us], preferred_element_type=F32)
            gt = jnp.minimum(g + bgu_ref[0, :, gs], SWIGLU_LIMIT)
            up = jnp.clip(u + bgu_ref[0, :, us], -SWIGLU_LIMIT, SWIGLU_LIMIT)
            act = (up + 1.0) * (gt * _sigmoid(SWIGLU_ALPHA * gt))
            part = jnp.dot(act.astype(BF16), wd_s[gs, :], preferred_element_type=F32)
            acc = part if acc is None else acc + part
        o_ref[...] = _pack_halves(acc + bd_ref[0])


def _moe_experts(blk_expert, blk_valid, n_used, xs, wgu, bgu, wd, bd, tmm):
    cap, h = xs.shape
    ne, d, dff2 = wgu.shape
    dff = dff2 // 2
    row_blk = lambda i, be, bv, nu: (jnp.maximum(jnp.minimum(i, nu[0] - 1), 0), 0)
    wsel = lambda i, be, bv, nu: (be[i], 0, 0)
    grid_spec = pltpu.PrefetchScalarGridSpec(
        num_scalar_prefetch=3,
        grid=(cap // tmm,),
        in_specs=[pl.BlockSpec((tmm, h), row_blk),
                  pl.BlockSpec((1, d, dff2), wsel),
                  pl.BlockSpec((1, 1, dff2), wsel),
                  pl.BlockSpec((1, dff, d), wsel),
                  pl.BlockSpec((1, 1, d), wsel)],
        out_specs=pl.BlockSpec((tmm, h), row_blk),
        scratch_shapes=[pltpu.VMEM((d, dff2), BF16), pltpu.VMEM((dff, d), BF16)],
    )
    return pl.pallas_call(
        functools.partial(_moe_kernel, dff=dff),
        grid_spec=grid_spec,
        out_shape=jax.ShapeDtypeStruct((cap, h), jnp.int32),
        compiler_params=_cparams(("arbitrary",)),
        name="moe_experts",
    )(blk_expert, blk_valid, n_used, xs, wgu, bgu.reshape(ne, 1, dff2), wd, bd.reshape(ne, 1, d))


SC_CHUNK = 64
SC_ID_CHUNK = 128


def _sc_workers():
    info = plsc.get_sparse_core_info()
    return info.num_cores, info.num_subcores


def _sc_scatter_into(out_ref, rows, dest):
    n, w = rows.shape
    nc, ns = _sc_workers()
    per_w = n // (nc * ns)
    assert per_w % SC_CHUNK == 0
    mesh = plsc.VectorSubcoreMesh(core_axis_name="c", subcore_axis_name="s")

    @functools.partial(
        pl.kernel, mesh=mesh, out_type=(),
        scratch_types=[pltpu.VMEM((SC_CHUNK,), jnp.int32),
                       pltpu.VMEM((SC_CHUNK, w), jnp.int32),
                       pltpu.SemaphoreType.DMA],
    )
    def scatter_rows(rows_hbm, dest_hbm, out_hbm, idx_v, rows_v, sem):
        base = (lax.axis_index("s") * nc + lax.axis_index("c")) * per_w

        @pl.loop(0, per_w // SC_CHUNK)
        def _(j):
            off = pl.multiple_of(base + j * SC_CHUNK, SC_CHUNK)
            pltpu.sync_copy(rows_hbm.at[pl.ds(off, SC_CHUNK)], rows_v)
            pltpu.sync_copy(dest_hbm.at[pl.ds(off, SC_CHUNK)], idx_v)
            pltpu.async_copy(rows_v, out_hbm.at[idx_v], sem).wait()

    scatter_rows(rows, dest, out_ref)


def _sc_scatter_ids(dest, out_rows):
    n = dest.shape[0]
    info = plsc.get_sparse_core_info()
    nc, ns, nl = info.num_cores, info.num_subcores, info.num_lanes
    per_w = n // (nc * ns)
    chunk = SC_ID_CHUNK
    assert per_w % (2 * chunk) == 0
    mesh = plsc.VectorSubcoreMesh(core_axis_name="c", subcore_axis_name="s")
    idx_t = pltpu.VMEM((chunk,), jnp.int32)
    rows_t = pltpu.VMEM((chunk, LANES), jnp.int32)

    @functools.partial(
        pl.kernel, mesh=mesh,
        out_type=jax.ShapeDtypeStruct((out_rows, LANES), jnp.int32),
        scratch_types=[idx_t, idx_t, rows_t, rows_t, pltpu.SemaphoreType.DMA, pltpu.SemaphoreType.DMA],
    )
    def scatter_ids(dest_hbm, out_hbm, idx_a, idx_b, rows_a, rows_b, sem_a, sem_b):
        base = (lax.axis_index("s") * nc + lax.axis_index("c")) * per_w

        def start(off, idx_v, rows_v, sem):
            for r in range(chunk):
                rows_v[r, pl.ds(0, nl)] = jnp.zeros((nl,), jnp.int32) + (off + r)
            pltpu.sync_copy(dest_hbm.at[pl.ds(off, chunk)], idx_v)
            return pltpu.async_copy(rows_v, out_hbm.at[idx_v], sem)

        @pl.loop(0, per_w // (2 * chunk))
        def _(j):
            off = pl.multiple_of(base + j * (2 * chunk), 2 * chunk)
            copy_a = start(off, idx_a, rows_a, sem_a)
            copy_b = start(off + chunk, idx_b, rows_b, sem_b)
            copy_a.wait()
            copy_b.wait()

    return scatter_ids(dest)


def _sc_gather(table, idx):
    n = idx.shape[0]
    w = table.shape[1]
    nc, ns = _sc_workers()
    per_w = n // (nc * ns)
    assert per_w % SC_CHUNK == 0
    mesh = plsc.VectorSubcoreMesh(core_axis_name="c", subcore_axis_name="s")

    @functools.partial(
        pl.kernel, mesh=mesh,
        out_type=jax.ShapeDtypeStruct((n, w), jnp.int32),
        scratch_types=[pltpu.VMEM((SC_CHUNK,), jnp.int32),
                       pltpu.VMEM((SC_CHUNK, w), jnp.int32),
                       pltpu.SemaphoreType.DMA],
    )
    def gather_rows(table_hbm, idx_hbm, out_hbm, idx_v, rows_v, sem):
        base = (lax.axis_index("s") * nc + lax.axis_index("c")) * per_w

        @pl.loop(0, per_w // SC_CHUNK)
        def _(j):
            off = pl.multiple_of(base + j * SC_CHUNK, SC_CHUNK)
            pltpu.sync_copy(idx_hbm.at[pl.ds(off, SC_CHUNK)], idx_v)
            pltpu.async_copy(table_hbm.at[idx_v], rows_v, sem).wait()
            pltpu.sync_copy(rows_v, out_hbm.at[pl.ds(off, SC_CHUNK)])

    return gather_rows(table, idx)


def _combine_kernel(x1_ref, y_ref, gate_ref, gtf_ref, g_ref, o_ref):
    h = y_ref.shape[3]
    gates = gate_ref[0]
    moe_a = moe_b = None
    for k in range(TOP_K):
        ya, yb = _unpack_halves(y_ref[k, 0])
        gk = gates[:, k:k + 1]
        moe_a = gk * ya if k == 0 else moe_a + gk * ya
        moe_b = gk * yb if k == 0 else moe_b + gk * yb
    za = x1_ref[0, :, :h] + gtf_ref[0, :, :h] * moe_a
    zb = x1_ref[0, :, h:] + gtf_ref[0, :, h:] * moe_b
    ms = (jnp.sum(za * za, axis=-1, keepdims=True) + jnp.sum(zb * zb, axis=-1, keepdims=True)) / (2 * h)
    inv = lax.rsqrt(ms + EPS)
    o_ref[0, :, :h] = za * inv * g_ref[:, :h]
    o_ref[0, :, h:] = zb * inv * g_ref[:, h:]


def _combine(x1, yk, gates_t, gt_f, g_final, tm):
    b, s, d = x1.shape
    h = yk.shape[3]
    return pl.pallas_call(
        _combine_kernel,
        grid=(b, s // tm),
        in_specs=[pl.BlockSpec((1, tm, d), lambda i, t: (i, t, 0)),
                  pl.BlockSpec((TOP_K, 1, tm, h), lambda i, t: (0, i, t, 0)),
                  pl.BlockSpec((1, tm, TOP_K), lambda i, t: (i, t, 0)),
                  pl.BlockSpec((1, 1, d), lambda i, t: (i, 0, 0)),
                  pl.BlockSpec((1, d), lambda i, t: (0, 0))],
        out_specs=pl.BlockSpec((1, tm, d), lambda i, t: (i, t, 0)),
        out_shape=jax.ShapeDtypeStruct((b, s, d), F32),
        compiler_params=_cparams(("parallel", "arbitrary")),
        name="combine",
    )(x1, yk, gates_t, gt_f, g_final.reshape(1, d))


def _dft_tables(rows, gd):
    seq = rows * GRID_W
    n = np.arange(rows)
    ang1 = 2.0 * np.pi * np.outer(n, n) / rows
    d2 = np.concatenate([np.cos(ang1), -np.sin(ang1)], axis=0)
    k1 = np.arange(rows)[:, None, None]
    k2 = np.arange(GRID_W)[None, :, None]
    n2 = np.arange(GRID_W)[None, None, :]
    ang2 = 2.0 * np.pi * ((n2 * (k1 + rows * k2)) % seq) / seq
    ec, es = np.cos(ang2), np.sin(ang2)
    etab = np.concatenate([np.concatenate([ec, es], axis=2),
                           np.concatenate([-es, ec], axis=2)], axis=1)
    c = np.arange(gd)
    angc = 2.0 * np.pi * np.outer(c, c) / gd
    scale = 1.0 / np.sqrt(seq * gd)
    return (jnp.asarray(d2, BF16), jnp.asarray(etab, BF16),
            jnp.asarray(np.cos(angc) * scale, F32), jnp.asarray(np.sin(angc) * scale, F32))


def _block_diag(w):
    h, i, o = w.shape
    eye = jnp.eye(h, dtype=w.dtype)
    return (eye[:, None, :, None] * w[:, :, None, :]).reshape(h * i, h * o)


def kernel(x, c, ctx, c_ctx, w_mod, b_mod, g_norm_mix, g_norm_ffn, w_in, w_fourier, conv_w, conv_b,
           rg_w_a, rg_b_a, rg_w_x, rg_b_x, rg_lam, g_out_fourier, g_out_rg, w_out, w_router, b_router,
           w_gate_up, b_gate_up, w_down, b_down, g_final):
    assert w_mod.shape[0] == 1, "single-layer stack only"
    b, s, d = x.shape
    df = w_fourier.shape[1] * w_fourier.shape[2]
    dr = conv_w.shape[2]
    gd = w_fourier.shape[2]
    rows = s // GRID_W
    t = b * s
    ne = w_router.shape[2]

    mrows = -(-(b + 1) // SUBLANES) * SUBLANES
    cond = jnp.zeros((mrows, d), F32).at[:b].set(c).at[b].set(c_ctx)
    mod = _adaln(cond, w_mod[0], b_mod[0])
    sh_m, sc_m, gt_m, sh_f, sc_f, gt_f = [mod[:b, k * d:(k + 1) * d].reshape(b, 1, d) for k in range(N_MOD)]
    csh_m = mod[b:b + 1, 0:d].reshape(1, 1, d)
    csc_m = mod[b:b + 1, d:2 * d].reshape(1, 1, d)

    tm = min(s, TOKEN_TILE)
    d2, etab, cmat, smat = _dft_tables(rows, gd)
    jmat = jnp.asarray(np.eye(GRID_W)[::-1].copy(), BF16)

    w_in_bf = w_in[0].astype(BF16)
    f, xs, gg = _stage_b(x, sh_m, sc_m, g_norm_mix[0], w_in_bf, jmat, df, dr, tm=tm)
    xr_ctx = _stage_b_ctx(ctx, csh_m, csc_m, g_norm_mix[0], w_in_bf[:, df:df + dr])

    wcat = jnp.stack([jnp.concatenate([_block_diag(rg_w_a[0, dd]), _block_diag(rg_w_x[0, dd])], axis=1)
                      for dd in range(2)]).astype(BF16)
    bcat = jnp.concatenate([rg_b_a[0], rg_b_x[0]], axis=1).reshape(2, 1, 2 * dr)
    lam = rg_lam[0].reshape(2, 1, dr)
    h0 = jnp.zeros((b, SUBLANES, dr), F32)
    _, hfin_ctx = _rg_scan(xr_ctx, h0, conv_w[0], conv_b[0], wcat, bcat, lam, tc=ctx.shape[1])
    hs, _ = _rg_scan(xs, hfin_ctx, conv_w[0], conv_b[0], wcat, bcat, lam, tc=min(s, SCAN_CHUNK))

    cw, sw = _fold_fourier(cmat, smat, w_fourier[0])
    wcs = jnp.concatenate([cw, sw], axis=1).astype(BF16)
    y = _fourier_stage1(f, d2, tl=min(GRID_W * df, 16384))
    fn = _fourier_stage2(y.reshape(b, 2, rows, GRID_W, df), etab, wcs, g_out_fourier[0],
                         kb=min(rows, 16))
    fn = fn.reshape(b, s, df)

    x1, h2, idx, gates = _stage_m(fn, hs, gg, x, gt_m, sh_f, sc_f, g_out_rg[0], g_norm_ffn[0],
                                  w_out[0].astype(BF16), w_router[0].T, b_router[0], jmat, tm=tm)

    tl = min(t, RANK_TILE)
    tri = jnp.asarray(np.triu(np.ones((tl, tl))), BF16)
    rank, cnt = _dispatch_ranks(idx, tri, min(t, RANK_STEP))
    counts = cnt[:, 0]
    tmm = MOE_ROW_TILE
    padded = (counts + tmm - 1) // tmm * tmm
    pad_end = jnp.cumsum(padded)
    pad_start = pad_end - padded
    eids = jnp.arange(ne, dtype=jnp.int32)
    dest = rank + jnp.sum(jnp.where(idx[:, :, None] == eids, pad_start, 0), axis=-1)
    n_blocks = -(-(t * TOP_K) // tmm) + ne
    cap = n_blocks * tmm
    n_used = (pad_end[-1] // tmm).astype(jnp.int32).reshape(1)
    blk_start = jnp.arange(n_blocks, dtype=jnp.int32) * tmm
    blk_expert = jnp.sum(blk_start[:, None] >= pad_end[None, :], axis=1).astype(jnp.int32)
    last_expert = jnp.sum(pad_end[-1] - tmm >= pad_end).astype(jnp.int32)
    blk_expert = jnp.minimum(blk_expert, last_expert)
    sel = blk_expert[:, None] == eids
    blk_first = jnp.sum(jnp.where(sel, pad_start, 0), axis=1)
    blk_count = jnp.sum(jnp.where(sel, counts, 0), axis=1)
    blk_valid = jnp.clip(blk_count - (blk_start - blk_first), 0, tmm).astype(jnp.int32)

    na = TOP_K * t
    inv = _sc_scatter_ids(dest.reshape(-1), cap)[:, 0].reshape(n_blocks, tmm)
    live = jnp.arange(tmm, dtype=jnp.int32)[None, :] < blk_valid[:, None]
    spread = jnp.arange(cap, dtype=jnp.int32).reshape(n_blocks, tmm) % t
    src_tok = jnp.where(live, inv % t, spread).reshape(-1)
    dst_row = jnp.where(live, inv, na + spread).reshape(-1)

    h2_rows = h2.reshape(t, d // 2)
    y_all = jax.empty_ref(jax.ShapeDtypeStruct(((TOP_K + 1) * t, d // 2), jnp.int32))
    unit = n_blocks // sum(MOE_PIPE)
    assert unit * sum(MOE_PIPE) == n_blocks
    blk0 = 0
    for parts in MOE_PIPE:
        nq = parts * unit
        blocks = slice(blk0, blk0 + nq)
        rows = slice(blk0 * tmm, (blk0 + nq) * tmm)
        x_q = _sc_gather(h2_rows, src_tok[rows])
        nu_q = jnp.clip(n_used - blk0, 0, nq).astype(jnp.int32)
        blk0 += nq
        y_q = _moe_experts(blk_expert[blocks], blk_valid[blocks], nu_q, x_q,
                           w_gate_up[0], b_gate_up[0], w_down[0], b_down[0], tmm)
        _sc_scatter_into(y_all, y_q, dst_row[rows])
    yk = y_all[...].reshape(TOP_K + 1, b, s, d // 2)
    return _combine(x1, yk, gates.T.reshape(b, s, TOP_K), gt_f, g_final, tm)
```

```python
import functools

import numpy as np
import jax
import jax.numpy as jnp
from jax import lax
from jax.experimental import pallas as pl
from jax.experimental.pallas import tpu as pltpu
from jax.experimental.pallas import tpu_sc as plsc

GRID_W = 64
CONV_W = 4
CONV_PAD_LO = 2
RG_C = 8.0
N_EXPERTS = 32
TOP_K = 4
SWIGLU_LIMIT = 7.0
SWIGLU_ALPHA = 1.702
N_MOD = 6
EPS = 1e-6

LANES = 128
SUBLANES = 8
VMEM_LIMIT_BYTES = 56 * 1024 * 1024
TOKEN_TILE = 1024
RANK_TILE = 512
RANK_STEP = 2048
SCAN_CHUNK = 512
MOE_ROW_TILE = 512
MOE_FF_CHUNK = 512
MOE_PIPE = (1, 3, 3, 1)

F32 = jnp.float32
BF16 = jnp.bfloat16


def _cparams(sem):
    return pltpu.CompilerParams(dimension_semantics=sem, vmem_limit_bytes=VMEM_LIMIT_BYTES)


def _split_bf16(a):
    hi = a.astype(BF16)
    lo = (a - hi.astype(F32)).astype(BF16)
    return hi, lo


def _dot3(a, b):
    ah, al = _split_bf16(a)
    bh, bl = _split_bf16(b)
    out = jnp.dot(ah, bh, preferred_element_type=F32)
    out += jnp.dot(ah, bl, preferred_element_type=F32)
    out += jnp.dot(al, bh, preferred_element_type=F32)
    return out


def _dot3_nt(a, b):
    dn = (((1,), (1,)), ((), ()))
    m = a.shape[0]
    ah, al = _split_bf16(a)
    bh, bl = _split_bf16(b)
    both = lax.dot_general(jnp.concatenate([ah, al], axis=0), bh, dn, preferred_element_type=F32)
    return both[:m] + both[m:] + lax.dot_general(ah, bl, dn, preferred_element_type=F32)


def _gelu_tanh(x):
    return 0.5 * x * (1.0 + jnp.tanh(0.7978845608028654 * (x + 0.044715 * (x * x * x))))


def _rms(x, g):
    return x * lax.rsqrt(jnp.mean(x * x, axis=-1, keepdims=True) + EPS) * g


def _pack_halves(v):
    h = v.shape[1] // 2
    hi = lax.bitcast_convert_type(v[:, :h].astype(BF16).astype(F32), jnp.uint32)
    lo = lax.bitcast_convert_type(v[:, h:].astype(BF16).astype(F32), jnp.uint32)
    return lax.bitcast_convert_type(hi | (lo >> 16), jnp.int32)


def _unpack_halves(w):
    u = lax.bitcast_convert_type(w, jnp.uint32)
    hi = lax.bitcast_convert_type(u & jnp.uint32(0xFFFF0000), F32)
    lo = lax.bitcast_convert_type(u << 16, F32)
    return hi, lo


def _adaln_kernel(c_ref, w_ref, b_ref, o_ref):
    s = c_ref[...]
    s = s * jax.nn.sigmoid(s)
    o_ref[...] = _dot3(s, w_ref[...]) + b_ref[...]


def _adaln(cond, w_mod, b_mod):
    m, d = cond.shape
    n = w_mod.shape[1]
    tn = n // N_MOD
    return pl.pallas_call(
        _adaln_kernel,
        grid=(n // tn,),
        in_specs=[pl.BlockSpec((m, d), lambda i: (0, 0)),
                  pl.BlockSpec((d, tn), lambda i: (0, i)),
                  pl.BlockSpec((1, tn), lambda i: (0, i))],
        out_specs=pl.BlockSpec((m, tn), lambda i: (0, i)),
        out_shape=jax.ShapeDtypeStruct((m, n), F32),
        compiler_params=_cparams(("arbitrary",)),
        name="adaln",
    )(cond, w_mod, b_mod.reshape(1, n))


def _fold_kernel(c_ref, s_ref, w_ref, cw_ref, sw_ref):
    w = w_ref[0]
    cw_ref[0] = _dot3(c_ref[...], w)
    sw_ref[0] = _dot3(s_ref[...], w)


def _fold_fourier(cmat, smat, w_f):
    g, gd, _ = w_f.shape
    spec_m = pl.BlockSpec((gd, gd), lambda i: (0, 0))
    spec_w = pl.BlockSpec((1, gd, gd), lambda i: (i, 0, 0))
    return pl.pallas_call(
        _fold_kernel,
        grid=(g,),
        in_specs=[spec_m, spec_m, spec_w],
        out_specs=[spec_w, spec_w],
        out_shape=[jax.ShapeDtypeStruct((g, gd, gd), F32)] * 2,
        compiler_params=_cparams(("arbitrary",)),
        name="fold_fourier",
    )(cmat, smat, w_f)


def _seg_pitch(seg_len):
    n8 = seg_len // SUBLANES
    return SUBLANES * (n8 + 1 - n8 % 2)


def _pitched_store(v, buf, blk0):
    pitch = _seg_pitch(GRID_W)
    for r in range(v.shape[0] // GRID_W):
        for c in range(v.shape[1] // LANES):
            buf[c, (blk0 + r) * pitch:(blk0 + r) * pitch + GRID_W, :] = (
                v[r * GRID_W:(r + 1) * GRID_W, c * LANES:(c + 1) * LANES])


def _pitched_gather(buf, nb, store):
    pitch = _seg_pitch(GRID_W)
    for pos in range(GRID_W):
        for c in range(buf.shape[0]):
            store(pos, c, buf[c, pl.ds(pos, nb, stride=pitch), :])


def _stage_b_kernel(x_ref, sh_ref, sc_ref, g_ref, w_ref, j_ref, f_ref, xs_ref, gg_ref, fbuf, *, df, dr):
    tm = x_ref.shape[1]
    h = _rms(x_ref[0], g_ref[...]) * (1.0 + sc_ref[0]) + sh_ref[0]
    hb = h.astype(BF16)
    gr = jnp.dot(hb, w_ref[:, df + dr:], preferred_element_type=F32)
    gg_ref[0] = _gelu_tanh(gr).astype(BF16)
    _pitched_store(jnp.dot(hb, w_ref[:, :df], preferred_element_type=F32), fbuf, 0)

    def store_f(pos, c, tile):
        f_ref[0, :, pos * df + c * LANES:pos * df + (c + 1) * LANES] = tile.astype(BF16)
    _pitched_gather(fbuf, tm // GRID_W, store_f)
    xr = jnp.dot(hb, w_ref[:, df:df + dr], preferred_element_type=F32).astype(BF16)
    for r in range(tm // GRID_W):
        blk = xr[r * GRID_W:(r + 1) * GRID_W]
        if r % 2 == 1:
            blk = jnp.dot(j_ref[...], blk, preferred_element_type=F32).astype(BF16)
        xs_ref[0, r * GRID_W:(r + 1) * GRID_W, :] = blk


def _stage_b(x, shift, scale, g, w_in_bf, jmat, df, dr, tm):
    b, s, d = x.shape
    n = w_in_bf.shape[1]
    vec = pl.BlockSpec((1, 1, d), lambda i, t: (i, 0, 0))
    out = pl.BlockSpec((1, tm, df), lambda i, t: (i, t, 0))
    nb = tm // GRID_W
    tok = jax.ShapeDtypeStruct((b, s, df), BF16)
    return pl.pallas_call(
        functools.partial(_stage_b_kernel, df=df, dr=dr),
        grid=(b, s // tm),
        in_specs=[pl.BlockSpec((1, tm, d), lambda i, t: (i, t, 0)), vec, vec,
                  pl.BlockSpec((1, d), lambda i, t: (0, 0)),
                  pl.BlockSpec((d, n), lambda i, t: (0, 0)),
                  pl.BlockSpec((GRID_W, GRID_W), lambda i, t: (0, 0))],
        out_specs=[pl.BlockSpec((1, nb, GRID_W * df), lambda i, t: (i, t, 0)), out, out],
        out_shape=[jax.ShapeDtypeStruct((b, s // GRID_W, GRID_W * df), BF16), tok, tok],
        scratch_shapes=[pltpu.VMEM((df // LANES, nb * _seg_pitch(GRID_W), LANES), F32)],
        compiler_params=_cparams(("parallel", "arbitrary")),
        name="stage_b",
    )(x, shift, scale, g.reshape(1, d), w_in_bf, jmat)


def _stage_b_ctx_kernel(x_ref, sh_ref, sc_ref, g_ref, w_ref, xr_ref):
    h = _rms(x_ref[0], g_ref[...]) * (1.0 + sc_ref[0]) + sh_ref[0]
    xr_ref[0] = jnp.dot(h.astype(BF16), w_ref[...], preferred_element_type=F32).astype(BF16)


def _stage_b_ctx(ctx, shift, scale, g, w_xr_bf):
    b, s, d = ctx.shape
    dr = w_xr_bf.shape[1]
    vec = pl.BlockSpec((1, 1, d), lambda i: (0, 0, 0))
    return pl.pallas_call(
        _stage_b_ctx_kernel,
        grid=(b,),
        in_specs=[pl.BlockSpec((1, s, d), lambda i: (i, 0, 0)), vec, vec,
                  pl.BlockSpec((1, d), lambda i: (0, 0)),
                  pl.BlockSpec((d, dr), lambda i: (0, 0))],
        out_specs=pl.BlockSpec((1, s, dr), lambda i: (i, 0, 0)),
        out_shape=jax.ShapeDtypeStruct((b, s, dr), BF16),
        compiler_params=_cparams(("arbitrary",)),
        name="stage_b_ctx",
    )(ctx, shift, scale, g.reshape(1, d), w_xr_bf)


HALO = 16


def _sigmoid(x):
    return 0.5 * jnp.tanh(0.5 * x) + 0.5


def _rg_kernel(xs_ref, h0_ref, cw_ref, cb_ref, w_ref, b_ref, lam_ref, pm_ref, pmt_ref, out_ref, hfin_ref,
               hf_s, xc_s, a_s, u_s, hl_s, p_s, c_s, hc_s, *, tc, nchunk, seq, dr):
    p = pl.program_id(1)
    j = pl.program_id(2)
    cidx = jnp.where(p == 0, j, nchunk - 1 - j)
    start = pl.multiple_of(cidx * tc, tc)
    nseg = SUBLANES
    sl = tc // nseg
    sub = lax.broadcasted_iota(jnp.int32, (nseg, dr), 0)

    @pl.when(p == 0)
    def _():
        xp = jnp.dot(pm_ref[...], xs_ref[0, pl.ds(start, tc), :], preferred_element_type=F32)
        pstart = pl.multiple_of(jnp.maximum(start - HALO, 0), HALO)
        nstart = pl.multiple_of(jnp.minimum(start + tc, seq - HALO), HALO)
        prev = xs_ref[0, pl.ds(pstart, HALO), :].astype(F32)
        nxt = xs_ref[0, pl.ds(nstart, HALO), :].astype(F32)
        prev = jnp.where(cidx > 0, prev, 0.0)
        nxt = jnp.where(cidx < nchunk - 1, nxt, 0.0)
        tm2 = jnp.where(sub == 0, prev[HALO - 2:HALO - 1], pltpu.roll(xp[(sl - 2) * nseg:(sl - 1) * nseg], 1, 0))
        tm1 = jnp.where(sub == 0, prev[HALO - 1:HALO], pltpu.roll(xp[(sl - 1) * nseg:sl * nseg], 1, 0))
        tp1 = jnp.where(sub == nseg - 1, nxt[0:1], pltpu.roll(xp[0:nseg], nseg - 1, 0))
        ext = jnp.concatenate([tm2, tm1, xp, tp1], axis=0)
        xc = cb_ref[...] + cw_ref[0:1, :] * ext[0:tc]
        for k in range(1, CONV_W):
            xc = xc + cw_ref[k:k + 1, :] * ext[k * nseg:k * nseg + tc]
        xc_s[pl.ds(start, tc), :] = xc

    xc = xc_s[pl.ds(start, tc), :]
    gates = jnp.dot(xc.astype(BF16), w_ref[p], preferred_element_type=F32) + b_ref[p]
    i = _sigmoid(gates[:, dr:])
    half_c = (-0.5 * RG_C) * jax.nn.softplus(-lam_ref[p])
    log_a = half_c * jnp.tanh(0.5 * gates[:, :dr]) + half_c
    a = jnp.exp(log_a)
    a_s[...] = a
    w = -jnp.tanh(log_a) * (1.0 + a * a)
    u_s[...] = jnp.where(w > 0.0, w * lax.rsqrt(w), 0.0) * (i * xc)

    @pl.when(jnp.logical_and(p == 0, j == 0))
    def _():
        hfin_ref[...] = jnp.zeros_like(hfin_ref)

    @pl.when(j == 0)
    def _():
        hc_s[0:1, :] = h0_ref[0, pl.ds(p, 1), :]

    def segment_scan(reverse):
        def body(q, carry):
            t = (sl - 1 - q) if reverse else q
            rows = pl.ds(pl.multiple_of(t * nseg, nseg), nseg)
            h, pr = carry
            av = a_s[rows, :]
            h = av * h + u_s[rows, :]
            pr = av * pr
            hl_s[rows, :] = h
            p_s[rows, :] = pr
            return h, pr
        h_end, p_end = lax.fori_loop(0, sl, body, (jnp.zeros((nseg, dr), F32), jnp.ones((nseg, dr), F32)),
                                     unroll=4)
        carry = hc_s[0:1, :]
        for g in (range(nseg - 1, -1, -1) if reverse else range(nseg)):
            c_s[g:g + 1, :] = carry
            carry = h_end[g:g + 1, :] + p_end[g:g + 1, :] * carry
        hc_s[0:1, :] = carry
        return c_s[...]

    def corrected(cin):
        h = hl_s[...].reshape(sl, nseg, dr) + p_s[...].reshape(sl, nseg, dr) * cin[None]
        return h.reshape(tc, dr)

    @pl.when(p == 0)
    def _():
        hf_s[pl.ds(start, tc), :] = corrected(segment_scan(False))

    @pl.when(p == 1)
    def _():
        tot = corrected(segment_scan(True)) + hf_s[pl.ds(start, tc), :]
        out_ref[0] = jnp.dot(pmt_ref[...], tot.astype(BF16), preferred_element_type=F32).astype(BF16)

    @pl.when(j == nchunk - 1)
    def _():
        hfin_ref[0, pl.ds(p, 1), :] = hc_s[0:1, :]


def _rg_scan(xs, h0, conv_w, conv_b, wcat, bcat, lam, tc):
    b, s, dr = xs.shape
    nchunk = s // tc
    last = nchunk - 1
    sl = tc // SUBLANES
    src = (np.arange(tc) % SUBLANES) * sl + np.arange(tc) // SUBLANES
    pm = np.zeros((tc, tc), np.float32)
    pm[np.arange(tc), src] = 1.0
    chunk_buf = pltpu.VMEM((tc, dr), F32)
    full2 = lambda shape: pl.BlockSpec(shape, lambda i, p, j: (0,) * len(shape))
    return pl.pallas_call(
        functools.partial(_rg_kernel, tc=tc, nchunk=nchunk, seq=s, dr=dr),
        grid=(b, 2, nchunk),
        in_specs=[pl.BlockSpec((1, s, dr), lambda i, p, j: (i, 0, 0)),
                  pl.BlockSpec((1, SUBLANES, dr), lambda i, p, j: (i, 0, 0)),
                  full2((CONV_W, dr)), full2((1, dr)),
                  full2((2, dr, 2 * dr)), full2((2, 1, 2 * dr)), full2((2, 1, dr)),
                  full2((tc, tc)), full2((tc, tc))],
        out_specs=[pl.BlockSpec((1, tc, dr), lambda i, p, j: (i, jnp.where(p == 0, last, last - j), 0)),
                   pl.BlockSpec((1, SUBLANES, dr), lambda i, p, j: (i, 0, 0))],
        out_shape=[jax.ShapeDtypeStruct((b, s, dr), BF16),
                   jax.ShapeDtypeStruct((b, SUBLANES, dr), F32)],
        scratch_shapes=[pltpu.VMEM((s, dr), F32), pltpu.VMEM((s, dr), F32),
                        chunk_buf, chunk_buf, chunk_buf, chunk_buf,
                        pltpu.VMEM((SUBLANES, dr), F32), pltpu.VMEM((SUBLANES, dr), F32)],
        compiler_params=_cparams(("arbitrary", "arbitrary", "arbitrary")),
        name="rg_scan",
    )(xs, h0, conv_w, conv_b.reshape(1, dr), wcat, bcat, lam, jnp.asarray(pm, BF16), jnp.asarray(pm.T, BF16))


def _f1_kernel(d_ref, x_ref, y_ref):
    y_ref[0] = jnp.dot(d_ref[...], x_ref[0], preferred_element_type=F32).astype(BF16)


def _fourier_stage1(fv, d2, tl):
    b, r, n = fv.shape
    return pl.pallas_call(
        _f1_kernel,
        grid=(b, n // tl),
        in_specs=[pl.BlockSpec((2 * r, r), lambda i, l: (0, 0)),
                  pl.BlockSpec((1, r, tl), lambda i, l: (i, 0, l))],
        out_specs=pl.BlockSpec((1, 2 * r, tl), lambda i, l: (i, 0, l)),
        out_shape=jax.ShapeDtypeStruct((b, 2 * r, n), BF16),
        compiler_params=_cparams(("parallel", "arbitrary")),
        name="fourier_stage1",
    )(d2, fv)


def _f2_kernel(y_ref, e_ref, wcs_ref, g_ref, o_ref, obuf, *, kb, df):
    zr, zi = [], []
    for q in range(kb):
        yk = jnp.concatenate([y_ref[0, 0, q], y_ref[0, 1, q]], axis=0)
        z = jnp.dot(e_ref[q], yk, preferred_element_type=F32)
        zr.append(z[:GRID_W])
        zi.append(z[GRID_W:])
    zr = jnp.concatenate(zr, axis=0).astype(BF16)
    zi = jnp.concatenate(zi, axis=0).astype(BF16)
    gd = wcs_ref.shape[2]
    o = jnp.concatenate(
        [jnp.dot(jnp.concatenate([zr[:, g * gd:(g + 1) * gd], zi[:, g * gd:(g + 1) * gd]], axis=1), wcs_ref[g],
                 preferred_element_type=F32) for g in range(df // gd)], axis=1)
    on = _rms(o, g_ref[...])

    def store_o(pos, c, tile):
        o_ref[0, pos, :, c * LANES:(c + 1) * LANES] = tile.astype(BF16)
    _pitched_store(on, obuf, 0)
    _pitched_gather(obuf, kb, store_o)


def _fourier_stage2(y5, etab, wcs, g, kb):
    b, _, r, w, df = y5.shape
    return pl.pallas_call(
        functools.partial(_f2_kernel, kb=kb, df=df),
        grid=(b, r // kb),
        in_specs=[pl.BlockSpec((1, 2, kb, w, df), lambda i, k: (i, 0, k, 0, 0)),
                  pl.BlockSpec((kb, 2 * w, 2 * w), lambda i, k: (k, 0, 0)),
                  pl.BlockSpec(wcs.shape, lambda i, k: (0, 0, 0)),
                  pl.BlockSpec((1, df), lambda i, k: (0, 0))],
        out_specs=pl.BlockSpec((1, w, kb, df), lambda i, k: (i, 0, k, 0)),
        out_shape=jax.ShapeDtypeStruct((b, w, r, df), BF16),
        scratch_shapes=[pltpu.VMEM((df // LANES, kb * _seg_pitch(GRID_W), LANES), F32)],
        compiler_params=_cparams(("parallel", "arbitrary")),
        name="fourier_stage2",
    )(y5, etab, wcs, g.reshape(1, df))


def _stage_m_kernel(fn_ref, hs_ref, gg_ref, x_ref, gtm_ref, shf_ref, scf_ref, gr_ref, gffn_ref,
                    wo_ref, wr_ref, br_ref, j_ref, x1_ref, h2_ref, idx_ref, gate_ref, *, df):
    tm = x_ref.shape[1]
    hs = hs_ref[0]
    blocks = []
    for r in range(tm // GRID_W):
        blk = hs[r * GRID_W:(r + 1) * GRID_W]
        if r % 2 == 1:
            blk = jnp.dot(j_ref[...], blk, preferred_element_type=F32)
        blocks.append(blk.astype(F32))
    rg = jnp.concatenate(blocks, axis=0) * gg_ref[0].astype(F32)
    rgn = _rms(rg, gr_ref[...]).astype(BF16)
    mix = jnp.dot(fn_ref[0], wo_ref[:df, :], preferred_element_type=F32)
    mix += jnp.dot(rgn, wo_ref[df:, :], preferred_element_type=F32)
    x1 = x_ref[0] + gtm_ref[0] * mix
    x1_ref[0] = x1
    h2 = _rms(x1, gffn_ref[...]) * (1.0 + scf_ref[0]) + shf_ref[0]
    h2_ref[0] = _pack_halves(h2)

    logits = _dot3_nt(wr_ref[...], h2) + br_ref[...]
    eidx = lax.broadcasted_iota(jnp.int32, logits.shape, 0)
    vals, idxs = [], []
    for _ in range(TOP_K):
        m = jnp.max(logits, axis=0, keepdims=True)
        sel = jnp.min(jnp.where(logits == m, eidx, N_EXPERTS), axis=0, keepdims=True)
        vals.append(m)
        idxs.append(sel)
        logits = jnp.where(eidx == sel, -jnp.inf, logits)
    ex = [jnp.exp(v - vals[0]) for v in vals]
    den = ex[0] + ex[1] + ex[2] + ex[3]
    for k in range(TOP_K):
        gate_ref[k:k + 1, :] = ex[k] / den
        idx_ref[k:k + 1, :] = idxs[k]


def _stage_m(fn, hs, gg, x, gt_m, sh_f, sc_f, g_out_r, g_ffn, w_out_bf, w_router_t, b_router, jmat, tm):
    b, s, d = x.shape
    df = fn.shape[2]
    dr = hs.shape[2]
    nt = s // tm
    ne = w_router_t.shape[0]
    vec = pl.BlockSpec((1, 1, d), lambda i, t: (i, 0, 0))
    half = lambda dd: pl.BlockSpec((1, tm, dd), lambda i, t: (i, t, 0))
    full = lambda shape: pl.BlockSpec(shape, lambda i, t: (0,) * len(shape))
    tok = pl.BlockSpec((TOP_K, tm), lambda i, t: (0, i * nt + t))
    return pl.pallas_call(
        functools.partial(_stage_m_kernel, df=df),
        grid=(b, nt),
        in_specs=[half(df), half(dr), half(dr), half(d), vec, vec, vec,
                  full((1, dr)), full((1, d)), full((d, d)), full((ne, d)), full((ne, 1)),
                  full((GRID_W, GRID_W))],
        out_specs=[half(d), half(d // 2), tok, tok],
        out_shape=[jax.ShapeDtypeStruct((b, s, d), F32), jax.ShapeDtypeStruct((b, s, d // 2), jnp.int32),
                   jax.ShapeDtypeStruct((TOP_K, b * s), jnp.int32),
                   jax.ShapeDtypeStruct((TOP_K, b * s), F32)],
        compiler_params=_cparams(("parallel", "arbitrary")),
        name="stage_m",
    )(fn, hs, gg, x, gt_m, sh_f, sc_f, g_out_r.reshape(1, dr), g_ffn.reshape(1, d), w_out_bf,
      w_router_t, b_router.reshape(ne, 1), jmat)


def _rank_kernel(idx_ref, tri_ref, rank_ref, cnt_ref, carry_s):
    c = pl.program_id(0)

    @pl.when(c == 0)
    def _():
        carry_s[...] = jnp.zeros_like(carry_s)

    l = tri_ref.shape[0]
    eidx = lax.broadcasted_iota(jnp.int32, (N_EXPERTS, l), 0)
    for sub in range(idx_ref.shape[1] // l):
        lanes = slice(sub * l, (sub + 1) * l)
        for k in range(TOP_K):
            onehot = eidx == idx_ref[k:k + 1, lanes]
            oh = jnp.where(onehot, 1.0, 0.0)
            prefix = jnp.dot(oh.astype(BF16), tri_ref[...], preferred_element_type=F32)
            carry = carry_s[:, 0:1]
            rank = jnp.sum(jnp.where(onehot, prefix - 1.0 + carry, 0.0), axis=0, keepdims=True)
            rank_ref[k:k + 1, lanes] = rank.astype(jnp.int32)
            carry_s[...] = carry_s[...] + jnp.sum(oh, axis=1, keepdims=True)
    cnt_ref[...] = carry_s[...].astype(jnp.int32)


def _dispatch_ranks(idx, tri, tl):
    k, t = idx.shape
    return pl.pallas_call(
        _rank_kernel,
        grid=(t // tl,),
        in_specs=[pl.BlockSpec((k, tl), lambda c: (0, c)),
                  pl.BlockSpec(tri.shape, lambda c: (0, 0))],
        out_specs=[pl.BlockSpec((k, tl), lambda c: (0, c)),
                   pl.BlockSpec((N_EXPERTS, LANES), lambda c: (0, 0))],
        out_shape=[jax.ShapeDtypeStruct((k, t), jnp.int32),
                   jax.ShapeDtypeStruct((N_EXPERTS, LANES), jnp.int32)],
        scratch_shapes=[pltpu.VMEM((N_EXPERTS, LANES), F32)],
        compiler_params=_cparams(("arbitrary",)),
        name="dispatch_ranks",
    )(idx, tri)


def _moe_kernel(be_ref, bv_ref, nu_ref, x_ref, wgu_ref, bgu_ref, wd_ref, bd_ref, o_ref, wgu_s, wd_s, *, dff):
    i = pl.program_id(0)
    h = x_ref.shape[1]

    @pl.when(jnp.logical_or(i == 0, be_ref[i] != be_ref[jnp.maximum(i - 1, 0)]))
    def _():
        wgu_s[...] = wgu_ref[0].astype(BF16)
        wd_s[...] = wd_ref[0].astype(BF16)

    @pl.when(i < nu_ref[0])
    def _():
        rows = lax.broadcasted_iota(jnp.int32, x_ref.shape, 0)
        xw = jnp.where(rows < bv_ref[i], x_ref[...], 0)
        xa, xb = _unpack_halves(xw)
        xa = xa.astype(BF16)
        xb = xb.astype(BF16)
        acc = None
        for c in range(dff // MOE_FF_CHUNK):
            gs = slice(c * MOE_FF_CHUNK, (c + 1) * MOE_FF_CHUNK)
            us = slice(dff + c * MOE_FF_CHUNK, dff + (c + 1) * MOE_FF_CHUNK)
            g = jnp.dot(xa, wgu_s[:h, gs], preferred_element_type=F32)
            g += jnp.dot(xb, wgu_s[h:, gs], preferred_element_type=F32)
            u = jnp.dot(xa, wgu_s[:h, us], preferred_element_type=F32)
            u += jnp.dot(xb, wgu_s[h:, us], preferred_element_type=F32)
            gt = jnp.minimum(g + bgu_ref[0, :, gs], SWIGLU_LIMIT)
            up = jnp.clip(u + bgu_ref[0, :, us], -SWIGLU_LIMIT, SWIGLU_LIMIT)
            act = (up + 1.0) * (gt * _sigmoid(SWIGLU_ALPHA * gt))
            part = jnp.dot(act.astype(BF16), wd_s[gs, :], preferred_element_type=F32)
            acc = part if acc is None else acc + part
        o_ref[...] = _pack_halves(acc + bd_ref[0])


def _moe_experts(blk_expert, blk_valid, n_used, xs, wgu, bgu, wd, bd, tmm):
    cap, h = xs.shape
    ne, d, dff2 = wgu.shape
    dff = dff2 // 2
    row_blk = lambda i, be, bv, nu: (jnp.maximum(jnp.minimum(i, nu[0] - 1), 0), 0)
    wsel = lambda i, be, bv, nu: (be[i], 0, 0)
    grid_spec = pltpu.PrefetchScalarGridSpec(
        num_scalar_prefetch=3,
        grid=(cap // tmm,),
        in_specs=[pl.BlockSpec((tmm, h), row_blk),
                  pl.BlockSpec((1, d, dff2), wsel),
                  pl.BlockSpec((1, 1, dff2), wsel),
                  pl.BlockSpec((1, dff, d), wsel),
                  pl.BlockSpec((1, 1, d), wsel)],
        out_specs=pl.BlockSpec((tmm, h), row_blk),
        scratch_shapes=[pltpu.VMEM((d, dff2), BF16), pltpu.VMEM((dff, d), BF16)],
    )
    return pl.pallas_call(
        functools.partial(_moe_kernel, dff=dff),
        grid_spec=grid_spec,
        out_shape=jax.ShapeDtypeStruct((cap, h), jnp.int32),
        compiler_params=_cparams(("arbitrary",)),
        name="moe_experts",
    )(blk_expert, blk_valid, n_used, xs, wgu, bgu.reshape(ne, 1, dff2), wd, bd.reshape(ne, 1, d))


SC_CHUNK = 64
SC_ID_CHUNK = 128


def _sc_workers():
    info = plsc.get_sparse_core_info()
    return info.num_cores, info.num_subcores


def _sc_scatter_into(out_ref, rows, dest):
    n, w = rows.shape
    nc, ns = _sc_workers()
    per_w = n // (nc * ns)
    assert per_w % SC_CHUNK == 0
    mesh = plsc.VectorSubcoreMesh(core_axis_name="c", subcore_axis_name="s")

    @functools.partial(
        pl.kernel, mesh=mesh, out_type=(),
        scratch_types=[pltpu.VMEM((SC_CHUNK,), jnp.int32),
                       pltpu.VMEM((SC_CHUNK, w), jnp.int32),
                       pltpu.SemaphoreType.DMA],
    )
    def scatter_rows(rows_hbm, dest_hbm, out_hbm, idx_v, rows_v, sem):
        base = (lax.axis_index("s") * nc + lax.axis_index("c")) * per_w

        @pl.loop(0, per_w // SC_CHUNK)
        def _(j):
            off = pl.multiple_of(base + j * SC_CHUNK, SC_CHUNK)
            pltpu.sync_copy(rows_hbm.at[pl.ds(off, SC_CHUNK)], rows_v)
            pltpu.sync_copy(dest_hbm.at[pl.ds(off, SC_CHUNK)], idx_v)
            pltpu.async_copy(rows_v, out_hbm.at[idx_v], sem).wait()

    scatter_rows(rows, dest, out_ref)


def _sc_scatter_ids(dest, out_rows):
    n = dest.shape[0]
    info = plsc.get_sparse_core_info()
    nc, ns, nl = info.num_cores, info.num_subcores, info.num_lanes
    per_w = n // (nc * ns)
    chunk = SC_ID_CHUNK
    assert per_w % (2 * chunk) == 0
    mesh = plsc.VectorSubcoreMesh(core_axis_name="c", subcore_axis_name="s")
    idx_t = pltpu.VMEM((chunk,), jnp.int32)
    rows_t = pltpu.VMEM((chunk, LANES), jnp.int32)

    @functools.partial(
        pl.kernel, mesh=mesh,
        out_type=jax.ShapeDtypeStruct((out_rows, LANES), jnp.int32),
        scratch_types=[idx_t, idx_t, rows_t, rows_t, pltpu.SemaphoreType.DMA, pltpu.SemaphoreType.DMA],
    )
    def scatter_ids(dest_hbm, out_hbm, idx_a, idx_b, rows_a, rows_b, sem_a, sem_b):
        base = (lax.axis_index("s") * nc + lax.axis_index("c")) * per_w

        def start(off, idx_v, rows_v, sem):
            for r in range(chunk):
                rows_v[r, pl.ds(0, nl)] = jnp.zeros((nl,), jnp.int32) + (off + r)
            pltpu.sync_copy(dest_hbm.at[pl.ds(off, chunk)], idx_v)
            return pltpu.async_copy(rows_v, out_hbm.at[idx_v], sem)

        @pl.loop(0, per_w // (2 * chunk))
        def _(j):
            off = pl.multiple_of(base + j * (2 * chunk), 2 * chunk)
            copy_a = start(off, idx_a, rows_a, sem_a)
            copy_b = start(off + chunk, idx_b, rows_b, sem_b)
            copy_a.wait()
            copy_b.wait()

    return scatter_ids(dest)


def _sc_gather(table, idx):
    n = idx.shape[0]
    w = table.shape[1]
    nc, ns = _sc_workers()
    per_w = n // (nc * ns)
    assert per_w % SC_CHUNK == 0
    mesh = plsc.VectorSubcoreMesh(core_axis_name="c", subcore_axis_name="s")

    @functools.partial(
        pl.kernel, mesh=mesh,
        out_type=jax.ShapeDtypeStruct((n, w), jnp.int32),
        scratch_types=[pltpu.VMEM((SC_CHUNK,), jnp.int32),
                       pltpu.VMEM((SC_CHUNK, w), jnp.int32),
                       pltpu.SemaphoreType.DMA],
    )
    def gather_rows(table_hbm, idx_hbm, out_hbm, idx_v, rows_v, sem):
        base = (lax.axis_index("s") * nc + lax.axis_index("c")) * per_w

        @pl.loop(0, per_w // SC_CHUNK)
        def _(j):
            off = pl.multiple_of(base + j * SC_CHUNK, SC_CHUNK)
            pltpu.sync_copy(idx_hbm.at[pl.ds(off, SC_CHUNK)], idx_v)
            pltpu.async_copy(table_hbm.at[idx_v], rows_v, sem).wait()
            pltpu.sync_copy(rows_v, out_hbm.at[pl.ds(off, SC_CHUNK)])

    return gather_rows(table, idx)


def _combine_kernel(x1_ref, y_ref, gate_ref, gtf_ref, g_ref, o_ref):
    h = y_ref.shape[3]
    gates = gate_ref[0]
    moe_a = moe_b = None
    for k in range(TOP_K):
        ya, yb = _unpack_halves(y_ref[k, 0])
        gk = gates[:, k:k + 1]
        moe_a = gk * ya if k == 0 else moe_a + gk * ya
        moe_b = gk * yb if k == 0 else moe_b + gk * yb
    za = x1_ref[0, :, :h] + gtf_ref[0, :, :h] * moe_a
    zb = x1_ref[0, :, h:] + gtf_ref[0, :, h:] * moe_b
    ms = (jnp.sum(za * za, axis=-1, keepdims=True) + jnp.sum(zb * zb, axis=-1, keepdims=True)) / (2 * h)
    inv = lax.rsqrt(ms + EPS)
    o_ref[0, :, :h] = za * inv * g_ref[:, :h]
    o_ref[0, :, h:] = zb * inv * g_ref[:, h:]


def _combine(x1, yk, gates_t, gt_f, g_final, tm):
    b, s, d = x1.shape
    h = yk.shape[3]
    return pl.pallas_call(
        _combine_kernel,
        grid=(b, s // tm),
        in_specs=[pl.BlockSpec((1, tm, d), lambda i, t: (i, t, 0)),
                  pl.BlockSpec((TOP_K, 1, tm, h), lambda i, t: (0, i, t, 0)),
                  pl.BlockSpec((1, tm, TOP_K), lambda i, t: (i, t, 0)),
                  pl.BlockSpec((1, 1, d), lambda i, t: (i, 0, 0)),
                  pl.BlockSpec((1, d), lambda i, t: (0, 0))],
        out_specs=pl.BlockSpec((1, tm, d), lambda i, t: (i, t, 0)),
        out_shape=jax.ShapeDtypeStruct((b, s, d), F32),
        compiler_params=_cparams(("parallel", "arbitrary")),
        name="combine",
    )(x1, yk, gates_t, gt_f, g_final.reshape(1, d))


def _dft_tables(rows, gd):
    seq = rows * GRID_W
    n = np.arange(rows)
    ang1 = 2.0 * np.pi * np.outer(n, n) / rows
    d2 = np.concatenate([np.cos(ang1), -np.sin(ang1)], axis=0)
    k1 = np.arange(rows)[:, None, None]
    k2 = np.arange(GRID_W)[None, :, None]
    n2 = np.arange(GRID_W)[None, None, :]
    ang2 = 2.0 * np.pi * ((n2 * (k1 + rows * k2)) % seq) / seq
    ec, es = np.cos(ang2), np.sin(ang2)
    etab = np.concatenate([np.concatenate([ec, es], axis=2),
                           np.concatenate([-es, ec], axis=2)], axis=1)
    c = np.arange(gd)
    angc = 2.0 * np.pi * np.outer(c, c) / gd
    scale = 1.0 / np.sqrt(seq * gd)
    return (jnp.asarray(d2, BF16), jnp.asarray(etab, BF16),
            jnp.asarray(np.cos(angc) * scale, F32), jnp.asarray(np.sin(angc) * scale, F32))


def _block_diag(w):
    h, i, o = w.shape
    eye = jnp.eye(h, dtype=w.dtype)
    return (eye[:, None, :, None] * w[:, :, None, :]).reshape(h * i, h * o)


def kernel(x, c, ctx, c_ctx, w_mod, b_mod, g_norm_mix, g_norm_ffn, w_in, w_fourier, conv_w, conv_b,
           rg_w_a, rg_b_a, rg_w_x, rg_b_x, rg_lam, g_out_fourier, g_out_rg, w_out, w_router, b_router,
           w_gate_up, b_gate_up, w_down, b_down, g_final):
    assert w_mod.shape[0] == 1, "single-layer stack only"
    b, s, d = x.shape
    df = w_fourier.shape[1] * w_fourier.shape[2]
    dr = conv_w.shape[2]
    gd = w_fourier.shape[2]
    rows = s // GRID_W
    t = b * s
    ne = w_router.shape[2]

    mrows = -(-(b + 1) // SUBLANES) * SUBLANES
    cond = jnp.zeros((mrows, d), F32).at[:b].set(c).at[b].set(c_ctx)
    mod = _adaln(cond, w_mod[0], b_mod[0])
    sh_m, sc_m, gt_m, sh_f, sc_f, gt_f = [mod[:b, k * d:(k + 1) * d].reshape(b, 1, d) for k in range(N_MOD)]
    csh_m = mod[b:b + 1, 0:d].reshape(1, 1, d)
    csc_m = mod[b:b + 1, d:2 * d].reshape(1, 1, d)

    tm = min(s, TOKEN_TILE)
    d2, etab, cmat, smat = _dft_tables(rows, gd)
    jmat = jnp.asarray(np.eye(GRID_W)[::-1].copy(), BF16)

    w_in_bf = w_in[0].astype(BF16)
    f, xs, gg = _stage_b(x, sh_m, sc_m, g_norm_mix[0], w_in_bf, jmat, df, dr, tm=tm)
    xr_ctx = _stage_b_ctx(ctx, csh_m, csc_m, g_norm_mix[0], w_in_bf[:, df:df + dr])

    wcat = jnp.stack([jnp.concatenate([_block_diag(rg_w_a[0, dd]), _block_diag(rg_w_x[0, dd])], axis=1)
                      for dd in range(2)]).astype(BF16)
    bcat = jnp.concatenate([rg_b_a[0], rg_b_x[0]], axis=1).reshape(2, 1, 2 * dr)
    lam = rg_lam[0].reshape(2, 1, dr)
    h0 = jnp.zeros((b, SUBLANES, dr), F32)
    _, hfin_ctx = _rg_scan(xr_ctx, h0, conv_w[0], conv_b[0], wcat, bcat, lam, tc=ctx.shape[1])
    hs, _ = _rg_scan(xs, hfin_ctx, conv_w[0], conv_b[0], wcat, bcat, lam, tc=min(s, SCAN_CHUNK))

    cw, sw = _fold_fourier(cmat, smat, w_fourier[0])
    wcs = jnp.concatenate([cw, sw], axis=1).astype(BF16)
    y = _fourier_stage1(f, d2, tl=min(GRID_W * df, 16384))
    fn = _fourier_stage2(y.reshape(b, 2, rows, GRID_W, df), etab, wcs, g_out_fourier[0],
                         kb=min(rows, 16))
    fn = fn.reshape(b, s, df)

    x1, h2, idx, gates = _stage_m(fn, hs, gg, x, gt_m, sh_f, sc_f, g_out_rg[0], g_norm_ffn[0],
                                  w_out[0].astype(BF16), w_router[0].T, b_router[0], jmat, tm=tm)

    tl = min(t, RANK_TILE)
    tri = jnp.asarray(np.triu(np.ones((tl, tl))), BF16)
    rank, cnt = _dispatch_ranks(idx, tri, min(t, RANK_STEP))
    counts = cnt[:, 0]
    tmm = MOE_ROW_TILE
    padded = (counts + tmm - 1) // tmm * tmm
    pad_end = jnp.cumsum(padded)
    pad_start = pad_end - padded
    eids = jnp.arange(ne, dtype=jnp.int32)
    dest = rank + jnp.sum(jnp.where(idx[:, :, None] == eids, pad_start, 0), axis=-1)
    n_blocks = -(-(t * TOP_K) // tmm) + ne
    cap = n_blocks * tmm
    n_used = (pad_end[-1] // tmm).astype(jnp.int32).reshape(1)
    blk_start = jnp.arange(n_blocks, dtype=jnp.int32) * tmm
    blk_expert = jnp.sum(blk_start[:, None] >= pad_end[None, :], axis=1).astype(jnp.int32)
    last_expert = jnp.sum(pad_end[-1] - tmm >= pad_end).astype(jnp.int32)
    blk_expert = jnp.minimum(blk_expert, last_expert)
    sel = blk_expert[:, None] == eids
    blk_first = jnp.sum(jnp.where(sel, pad_start, 0), axis=1)
    blk_count = jnp.sum(jnp.where(sel, counts, 0), axis=1)
    blk_valid = jnp.clip(blk_count - (blk_start - blk_first), 0, tmm).astype(jnp.int32)

    na = TOP_K * t
    inv = _sc_scatter_ids(dest.reshape(-1), cap)[:, 0].reshape(n_blocks, tmm)
    live = jnp.arange(tmm, dtype=jnp.int32)[None, :] < blk_valid[:, None]
    spread = jnp.arange(cap, dtype=jnp.int32).reshape(n_blocks, tmm) % t
    src_tok = jnp.where(live, inv % t, spread).reshape(-1)
    dst_row = jnp.where(live, inv, na + spread).reshape(-1)

    h2_rows = h2.reshape(t, d // 2)
    y_all = jax.empty_ref(jax.ShapeDtypeStruct(((TOP_K + 1) * t, d // 2), jnp.int32))
    unit = n_blocks // sum(MOE_PIPE)
    assert unit * sum(MOE_PIPE) == n_blocks
    blk0 = 0
    for parts in MOE_PIPE:
        nq = parts * unit
        blocks = slice(blk0, blk0 + nq)
        rows = slice(blk0 * tmm, (blk0 + nq) * tmm)
        x_q = _sc_gather(h2_rows, src_tok[rows])
        nu_q = jnp.clip(n_used - blk0, 0, nq).astype(jnp.int32)
        blk0 += nq
        y_q = _moe_experts(blk_expert[blocks], blk_valid[blocks], nu_q, x_q,
                           w_gate_up[0], b_gate_up[0], w_down[0], b_down[0], tmm)
        _sc_scatter_into(y_all, y_q, dst_row[rows])
    yk = y_all[...].reshape(TOP_K + 1, b, s, d // 2)
    return _combine(x1, yk, gates.T.reshape(b, s, TOP_K), gt_f, g_final, tm)
```

```python
import functools

import numpy as np
import jax
import jax.numpy as jnp
from jax import lax
from jax.experimental import pallas as pl
from jax.experimental.pallas import tpu as pltpu
from jax.experimental.pallas import tpu_sc as plsc

GRID_W = 64
CONV_W = 4
CONV_PAD_LO = 2
RG_C = 8.0
N_EXPERTS = 32
TOP_K = 4
SWIGLU_LIMIT = 7.0
SWIGLU_ALPHA = 1.702
N_MOD = 6
EPS = 1e-6

LANES = 128
SUBLANES = 8
VMEM_LIMIT_BYTES = 56 * 1024 * 1024
TOKEN_TILE = 1024
RANK_TILE = 512
RANK_STEP = 4096
SCAN_CHUNK = 512
MOE_ROW_TILE = 512
MOE_FF_CHUNK = 512
MOE_PIPE = (1, 3, 3, 1)

F32 = jnp.float32
BF16 = jnp.bfloat16


def _cparams(sem):
    return pltpu.CompilerParams(dimension_semantics=sem, vmem_limit_bytes=VMEM_LIMIT_BYTES)


def _split_bf16(a):
    hi = a.astype(BF16)
    lo = (a - hi.astype(F32)).astype(BF16)
    return hi, lo


def _dot3(a, b):
    ah, al = _split_bf16(a)
    bh, bl = _split_bf16(b)
    out = jnp.dot(ah, bh, preferred_element_type=F32)
    out += jnp.dot(ah, bl, preferred_element_type=F32)
    out += jnp.dot(al, bh, preferred_element_type=F32)
    return out


def _dot3_nt(a, b):
    dn = (((1,), (1,)), ((), ()))
    m = a.shape[0]
    ah, al = _split_bf16(a)
    bh, bl = _split_bf16(b)
    both = lax.dot_general(jnp.concatenate([ah, al], axis=0), bh, dn, preferred_element_type=F32)
    return both[:m] + both[m:] + lax.dot_general(ah, bl, dn, preferred_element_type=F32)


def _gelu_tanh(x):
    return 0.5 * x * (1.0 + jnp.tanh(0.7978845608028654 * (x + 0.044715 * (x * x * x))))


def _rms(x, g):
    return x * lax.rsqrt(jnp.mean(x * x, axis=-1, keepdims=True) + EPS) * g


def _pack_halves(v):
    h = v.shape[1] // 2
    hi = lax.bitcast_convert_type(v[:, :h].astype(BF16).astype(F32), jnp.uint32)
    lo = lax.bitcast_convert_type(v[:, h:].astype(BF16).astype(F32), jnp.uint32)
    return lax.bitcast_convert_type(hi | (lo >> 16), jnp.int32)


def _unpack_halves(w):
    u = lax.bitcast_convert_type(w, jnp.uint32)
    hi = lax.bitcast_convert_type(u & jnp.uint32(0xFFFF0000), F32)
    lo = lax.bitcast_convert_type(u << 16, F32)
    return hi, lo


def _adaln_kernel(c_ref, w_ref, b_ref, o_ref):
    s = c_ref[...]
    s = s * jax.nn.sigmoid(s)
    o_ref[...] = _dot3(s, w_ref[...]) + b_ref[...]


def _adaln(cond, w_mod, b_mod):
    m, d = cond.shape
    n = w_mod.shape[1]
    tn = n // N_MOD
    return pl.pallas_call(
        _adaln_kernel,
        grid=(n // tn,),
        in_specs=[pl.BlockSpec((m, d), lambda i: (0, 0)),
                  pl.BlockSpec((d, tn), lambda i: (0, i)),
                  pl.BlockSpec((1, tn), lambda i: (0, i))],
        out_specs=pl.BlockSpec((m, tn), lambda i: (0, i)),
        out_shape=jax.ShapeDtypeStruct((m, n), F32),
        compiler_params=_cparams(("arbitrary",)),
        name="adaln",
    )(cond, w_mod, b_mod.reshape(1, n))


def _fold_kernel(c_ref, s_ref, w_ref, cw_ref, sw_ref):
    w = w_ref[0]
    cw_ref[0] = _dot3(c_ref[...], w)
    sw_ref[0] = _dot3(s_ref[...], w)


def _fold_fourier(cmat, smat, w_f):
    g, gd, _ = w_f.shape
    spec_m = pl.BlockSpec((gd, gd), lambda i: (0, 0))
    spec_w = pl.BlockSpec((1, gd, gd), lambda i: (i, 0, 0))
    return pl.pallas_call(
        _fold_kernel,
        grid=(g,),
        in_specs=[spec_m, spec_m, spec_w],
        out_specs=[spec_w, spec_w],
        out_shape=[jax.ShapeDtypeStruct((g, gd, gd), F32)] * 2,
        compiler_params=_cparams(("arbitrary",)),
        name="fold_fourier",
    )(cmat, smat, w_f)


def _seg_pitch(seg_len):
    n8 = seg_len // SUBLANES
    return SUBLANES * (n8 + 1 - n8 % 2)


def _pitched_store(v, buf, blk0):
    pitch = _seg_pitch(GRID_W)
    for r in range(v.shape[0] // GRID_W):
        for c in range(v.shape[1] // LANES):
            buf[c, (blk0 + r) * pitch:(blk0 + r) * pitch + GRID_W, :] = (
                v[r * GRID_W:(r + 1) * GRID_W, c * LANES:(c + 1) * LANES])


def _pitched_gather(buf, nb, store):
    pitch = _seg_pitch(GRID_W)
    for pos in range(GRID_W):
        for c in range(buf.shape[0]):
            store(pos, c, buf[c, pl.ds(pos, nb, stride=pitch), :])


def _stage_b_kernel(x_ref, sh_ref, sc_ref, g_ref, w_ref, j_ref, f_ref, xs_ref, gg_ref, fbuf, *, df, dr):
    tm = x_ref.shape[1]
    h = _rms(x_ref[0], g_ref[...] * (1.0 + sc_ref[0])) + sh_ref[0]
    hb = h.astype(BF16)
    gr = jnp.dot(hb, w_ref[:, df + dr:], preferred_element_type=F32)
    gg_ref[0] = _gelu_tanh(gr).astype(BF16)
    _pitched_store(jnp.dot(hb, w_ref[:, :df], preferred_element_type=F32), fbuf, 0)

    def store_f(pos, c, tile):
        f_ref[0, :, pos * df + c * LANES:pos * df + (c + 1) * LANES] = tile.astype(BF16)
    _pitched_gather(fbuf, tm // GRID_W, store_f)
    xr = jnp.dot(hb, w_ref[:, df:df + dr], preferred_element_type=F32).astype(BF16)
    for r in range(tm // GRID_W):
        blk = xr[r * GRID_W:(r + 1) * GRID_W]
        if r % 2 == 1:
            blk = jnp.dot(j_ref[...], blk, preferred_element_type=F32).astype(BF16)
        xs_ref[0, r * GRID_W:(r + 1) * GRID_W, :] = blk


def _stage_b(x, shift, scale, g, w_in_bf, jmat, df, dr, tm):
    b, s, d = x.shape
    n = w_in_bf.shape[1]
    vec = pl.BlockSpec((1, 1, d), lambda i, t: (i, 0, 0))
    out = pl.BlockSpec((1, tm, df), lambda i, t: (i, t, 0))
    nb = tm // GRID_W
    tok = jax.ShapeDtypeStruct((b, s, df), BF16)
    return pl.pallas_call(
        functools.partial(_stage_b_kernel, df=df, dr=dr),
        grid=(b, s // tm),
        in_specs=[pl.BlockSpec((1, tm, d), lambda i, t: (i, t, 0)), vec, vec,
                  pl.BlockSpec((1, d), lambda i, t: (0, 0)),
                  pl.BlockSpec((d, n), lambda i, t: (0, 0)),
                  pl.BlockSpec((GRID_W, GRID_W), lambda i, t: (0, 0))],
        out_specs=[pl.BlockSpec((1, nb, GRID_W * df), lambda i, t: (i, t, 0)), out, out],
        out_shape=[jax.ShapeDtypeStruct((b, s // GRID_W, GRID_W * df), BF16), tok, tok],
        scratch_shapes=[pltpu.VMEM((df // LANES, nb * _seg_pitch(GRID_W), LANES), F32)],
        compiler_params=_cparams(("parallel", "arbitrary")),
        name="stage_b",
    )(x, shift, scale, g.reshape(1, d), w_in_bf, jmat)


def _stage_b_ctx_kernel(x_ref, sh_ref, sc_ref, g_ref, w_ref, xr_ref):
    h = _rms(x_ref[0], g_ref[...] * (1.0 + sc_ref[0])) + sh_ref[0]
    xr_ref[0] = jnp.dot(h.astype(BF16), w_ref[...], preferred_element_type=F32).astype(BF16)


def _stage_b_ctx(ctx, shift, scale, g, w_xr_bf):
    b, s, d = ctx.shape
    dr = w_xr_bf.shape[1]
    vec = pl.BlockSpec((1, 1, d), lambda i: (0, 0, 0))
    return pl.pallas_call(
        _stage_b_ctx_kernel,
        grid=(b,),
        in_specs=[pl.BlockSpec((1, s, d), lambda i: (i, 0, 0)), vec, vec,
                  pl.BlockSpec((1, d), lambda i: (0, 0)),
                  pl.BlockSpec((d, dr), lambda i: (0, 0))],
        out_specs=pl.BlockSpec((1, s, dr), lambda i: (i, 0, 0)),
        out_shape=jax.ShapeDtypeStruct((b, s, dr), BF16),
        compiler_params=_cparams(("arbitrary",)),
        name="stage_b_ctx",
    )(ctx, shift, scale, g.reshape(1, d), w_xr_bf)


HALO = 16


def _sigmoid(x):
    return 0.5 * jnp.tanh(0.5 * x) + 0.5


def _rg_kernel(xs_ref, h0_ref, cw_ref, cb_ref, w_ref, b_ref, lam_ref, pm_ref, pmt_ref, out_ref, hfin_ref,
               hf_s, xc_s, a_s, u_s, hl_s, p_s, c_s, hc_s, *, tc, nchunk, seq, dr):
    p = pl.program_id(1)
    j = pl.program_id(2)
    cidx = jnp.where(p == 0, j, nchunk - 1 - j)
    start = pl.multiple_of(cidx * tc, tc)
    nseg = SUBLANES
    sl = tc // nseg
    sub = lax.broadcasted_iota(jnp.int32, (nseg, dr), 0)

    @pl.when(p == 0)
    def _():
        xp = jnp.dot(pm_ref[...], xs_ref[0, pl.ds(start, tc), :], preferred_element_type=F32)
        pstart = pl.multiple_of(jnp.maximum(start - HALO, 0), HALO)
        nstart = pl.multiple_of(jnp.minimum(start + tc, seq - HALO), HALO)
        prev = xs_ref[0, pl.ds(pstart, HALO), :].astype(F32)
        nxt = xs_ref[0, pl.ds(nstart, HALO), :].astype(F32)
        prev = jnp.where(cidx > 0, prev, 0.0)
        nxt = jnp.where(cidx < nchunk - 1, nxt, 0.0)
        tm2 = jnp.where(sub == 0, prev[HALO - 2:HALO - 1], pltpu.roll(xp[(sl - 2) * nseg:(sl - 1) * nseg], 1, 0))
        tm1 = jnp.where(sub == 0, prev[HALO - 1:HALO], pltpu.roll(xp[(sl - 1) * nseg:sl * nseg], 1, 0))
        tp1 = jnp.where(sub == nseg - 1, nxt[0:1], pltpu.roll(xp[0:nseg], nseg - 1, 0))
        ext = jnp.concatenate([tm2, tm1, xp, tp1], axis=0)
        xc = cb_ref[...] + cw_ref[0:1, :] * ext[0:tc]
        for k in range(1, CONV_W):
            xc = xc + cw_ref[k:k + 1, :] * ext[k * nseg:k * nseg + tc]
        xc_s[pl.ds(start, tc), :] = xc

    xc = xc_s[pl.ds(start, tc), :]
    gates = jnp.dot(xc.astype(BF16), w_ref[p], preferred_element_type=F32) + b_ref[p]
    i = _sigmoid(gates[:, dr:])
    half_c = (-0.5 * RG_C) * jax.nn.softplus(-lam_ref[p])
    log_a = half_c * jnp.tanh(0.5 * gates[:, :dr]) + half_c
    a = jnp.exp(log_a)
    a_s[...] = a
    w = -jnp.tanh(log_a) * (1.0 + a * a)
    u_s[...] = jnp.where(w > 0.0, w * lax.rsqrt(w), 0.0) * (i * xc)

    @pl.when(jnp.logical_and(p == 0, j == 0))
    def _():
        hfin_ref[...] = jnp.zeros_like(hfin_ref)

    @pl.when(j == 0)
    def _():
        hc_s[0:1, :] = h0_ref[0, pl.ds(p, 1), :]

    def segment_scan(reverse):
        def body(q, carry):
            t = (sl - 1 - q) if reverse else q
            rows = pl.ds(pl.multiple_of(t * nseg, nseg), nseg)
            h, pr = carry
            av = a_s[rows, :]
            h = av * h + u_s[rows, :]
            pr = av * pr
            hl_s[rows, :] = h
            p_s[rows, :] = pr
            return h, pr
        h_end, p_end = lax.fori_loop(0, sl, body, (jnp.zeros((nseg, dr), F32), jnp.ones((nseg, dr), F32)),
                                     unroll=4)
        carry = hc_s[0:1, :]
        for g in (range(nseg - 1, -1, -1) if reverse else range(nseg)):
            c_s[g:g + 1, :] = carry
            carry = h_end[g:g + 1, :] + p_end[g:g + 1, :] * carry
        hc_s[0:1, :] = carry
        return c_s[...]

    def corrected(cin):
        h = hl_s[...].reshape(sl, nseg, dr) + p_s[...].reshape(sl, nseg, dr) * cin[None]
        return h.reshape(tc, dr)

    @pl.when(p == 0)
    def _():
        hf_s[pl.ds(start, tc), :] = corrected(segment_scan(False))

    @pl.when(p == 1)
    def _():
        tot = corrected(segment_scan(True)) + hf_s[pl.ds(start, tc), :]
        out_ref[0] = jnp.dot(pmt_ref[...], tot.astype(BF16), preferred_element_type=F32).astype(BF16)

    @pl.when(j == nchunk - 1)
    def _():
        hfin_ref[0, pl.ds(p, 1), :] = hc_s[0:1, :]


def _rg_scan(xs, h0, conv_w, conv_b, wcat, bcat, lam, tc):
    b, s, dr = xs.shape
    nchunk = s // tc
    last = nchunk - 1
    sl = tc // SUBLANES
    src = (np.arange(tc) % SUBLANES) * sl + np.arange(tc) // SUBLANES
    pm = np.zeros((tc, tc), np.float32)
    pm[np.arange(tc), src] = 1.0
    chunk_buf = pltpu.VMEM((tc, dr), F32)
    full2 = lambda shape: pl.BlockSpec(shape, lambda i, p, j: (0,) * len(shape))
    return pl.pallas_call(
        functools.partial(_rg_kernel, tc=tc, nchunk=nchunk, seq=s, dr=dr),
        grid=(b, 2, nchunk),
        in_specs=[pl.BlockSpec((1, s, dr), lambda i, p, j: (i, 0, 0)),
                  pl.BlockSpec((1, SUBLANES, dr), lambda i, p, j: (i, 0, 0)),
                  full2((CONV_W, dr)), full2((1, dr)),
                  full2((2, dr, 2 * dr)), full2((2, 1, 2 * dr)), full2((2, 1, dr)),
                  full2((tc, tc)), full2((tc, tc))],
        out_specs=[pl.BlockSpec((1, tc, dr), lambda i, p, j: (i, jnp.where(p == 0, last, last - j), 0)),
                   pl.BlockSpec((1, SUBLANES, dr), lambda i, p, j: (i, 0, 0))],
        out_shape=[jax.ShapeDtypeStruct((b, s, dr), BF16),
                   jax.ShapeDtypeStruct((b, SUBLANES, dr), F32)],
        scratch_shapes=[pltpu.VMEM((s, dr), F32), pltpu.VMEM((s, dr), F32),
                        chunk_buf, chunk_buf, chunk_buf, chunk_buf,
                        pltpu.VMEM((SUBLANES, dr), F32), pltpu.VMEM((SUBLANES, dr), F32)],
        compiler_params=_cparams(("arbitrary", "arbitrary", "arbitrary")),
        name="rg_scan",
    )(xs, h0, conv_w, conv_b.reshape(1, dr), wcat, bcat, lam, jnp.asarray(pm, BF16), jnp.asarray(pm.T, BF16))


def _f1_kernel(d_ref, x_ref, y_ref):
    y_ref[0] = jnp.dot(d_ref[...], x_ref[0], preferred_element_type=F32).astype(BF16)


def _fourier_stage1(fv, d2, tl):
    b, r, n = fv.shape
    return pl.pallas_call(
        _f1_kernel,
        grid=(b, n // tl),
        in_specs=[pl.BlockSpec((2 * r, r), lambda i, l: (0, 0)),
                  pl.BlockSpec((1, r, tl), lambda i, l: (i, 0, l))],
        out_specs=pl.BlockSpec((1, 2 * r, tl), lambda i, l: (i, 0, l)),
        out_shape=jax.ShapeDtypeStruct((b, 2 * r, n), BF16),
        compiler_params=_cparams(("parallel", "arbitrary")),
        name="fourier_stage1",
    )(d2, fv)


def _f2_kernel(y_ref, e_ref, wcs_ref, g_ref, o_ref, obuf, *, kb, df):
    zr, zi = [], []
    for q in range(kb):
        yk = jnp.concatenate([y_ref[0, 0, q], y_ref[0, 1, q]], axis=0)
        z = jnp.dot(e_ref[q], yk, preferred_element_type=F32)
        zr.append(z[:GRID_W])
        zi.append(z[GRID_W:])
    zr = jnp.concatenate(zr, axis=0).astype(BF16)
    zi = jnp.concatenate(zi, axis=0).astype(BF16)
    gd = wcs_ref.shape[2]
    o = jnp.concatenate(
        [jnp.dot(jnp.concatenate([zr[:, g * gd:(g + 1) * gd], zi[:, g * gd:(g + 1) * gd]], axis=1), wcs_ref[g],
                 preferred_element_type=F32) for g in range(df // gd)], axis=1)
    on = _rms(o, g_ref[...])

    def store_o(pos, c, tile):
        o_ref[0, pos, :, c * LANES:(c + 1) * LANES] = tile.astype(BF16)
    _pitched_store(on, obuf, 0)
    _pitched_gather(obuf, kb, store_o)


def _fourier_stage2(y5, etab, wcs, g, kb):
    b, _, r, w, df = y5.shape
    return pl.pallas_call(
        functools.partial(_f2_kernel, kb=kb, df=df),
        grid=(b, r // kb),
        in_specs=[pl.BlockSpec((1, 2, kb, w, df), lambda i, k: (i, 0, k, 0, 0)),
                  pl.BlockSpec((kb, 2 * w, 2 * w), lambda i, k: (k, 0, 0)),
                  pl.BlockSpec(wcs.shape, lambda i, k: (0, 0, 0)),
                  pl.BlockSpec((1, df), lambda i, k: (0, 0))],
        out_specs=pl.BlockSpec((1, w, kb, df), lambda i, k: (i, 0, k, 0)),
        out_shape=jax.ShapeDtypeStruct((b, w, r, df), BF16),
        scratch_shapes=[pltpu.VMEM((df // LANES, kb * _seg_pitch(GRID_W), LANES), F32)],
        compiler_params=_cparams(("parallel", "arbitrary")),
        name="fourier_stage2",
    )(y5, etab, wcs, g.reshape(1, df))


def _stage_m_kernel(fn_ref, hs_ref, gg_ref, x_ref, gtm_ref, shf_ref, scf_ref, gr_ref, gffn_ref,
                    wo_ref, wr_ref, br_ref, j_ref, x1_ref, h2_ref, idx_ref, gate_ref, *, df):
    tm = x_ref.shape[1]
    hs = hs_ref[0]
    blocks = []
    for r in range(tm // GRID_W):
        blk = hs[r * GRID_W:(r + 1) * GRID_W]
        if r % 2 == 1:
            blk = jnp.dot(j_ref[...], blk, preferred_element_type=F32)
        blocks.append(blk.astype(F32))
    rg = jnp.concatenate(blocks, axis=0) * gg_ref[0].astype(F32)
    rgn = _rms(rg, gr_ref[...]).astype(BF16)
    mix = jnp.dot(fn_ref[0], wo_ref[:df, :], preferred_element_type=F32)
    mix += jnp.dot(rgn, wo_ref[df:, :], preferred_element_type=F32)
    x1 = x_ref[0] + gtm_ref[0] * mix
    x1_ref[0] = x1
    h2 = _rms(x1, gffn_ref[...] * (1.0 + scf_ref[0])) + shf_ref[0]
    h2_ref[0] = _pack_halves(h2)

    logits = _dot3_nt(wr_ref[...], h2) + br_ref[...]
    eidx = lax.broadcasted_iota(jnp.int32, logits.shape, 0)
    vals, idxs = [], []
    for _ in range(TOP_K):
        m = jnp.max(logits, axis=0, keepdims=True)
        sel = jnp.min(jnp.where(logits == m, eidx, N_EXPERTS), axis=0, keepdims=True)
        vals.append(m)
        idxs.append(sel)
        logits = jnp.where(eidx == sel, -jnp.inf, logits)
    ex = [jnp.exp(v - vals[0]) for v in vals]
    den = ex[0] + ex[1] + ex[2] + ex[3]
    for k in range(TOP_K):
        gate_ref[k:k + 1, :] = ex[k] / den
        idx_ref[k:k + 1, :] = idxs[k]


def _stage_m(fn, hs, gg, x, gt_m, sh_f, sc_f, g_out_r, g_ffn, w_out_bf, w_router_t, b_router, jmat, tm):
    b, s, d = x.shape
    df = fn.shape[2]
    dr = hs.shape[2]
    nt = s // tm
    ne = w_router_t.shape[0]
    vec = pl.BlockSpec((1, 1, d), lambda i, t: (i, 0, 0))
    half = lambda dd: pl.BlockSpec((1, tm, dd), lambda i, t: (i, t, 0))
    full = lambda shape: pl.BlockSpec(shape, lambda i, t: (0,) * len(shape))
    tok = pl.BlockSpec((TOP_K, tm), lambda i, t: (0, i * nt + t))
    return pl.pallas_call(
        functools.partial(_stage_m_kernel, df=df),
        grid=(b, nt),
        in_specs=[half(df), half(dr), half(dr), half(d), vec, vec, vec,
                  full((1, dr)), full((1, d)), full((d, d)), full((ne, d)), full((ne, 1)),
                  full((GRID_W, GRID_W))],
        out_specs=[half(d), half(d // 2), tok, tok],
        out_shape=[jax.ShapeDtypeStruct((b, s, d), F32), jax.ShapeDtypeStruct((b, s, d // 2), jnp.int32),
                   jax.ShapeDtypeStruct((TOP_K, b * s), jnp.int32),
                   jax.ShapeDtypeStruct((TOP_K, b * s), F32)],
        compiler_params=_cparams(("parallel", "arbitrary")),
        name="stage_m",
    )(fn, hs, gg, x, gt_m, sh_f, sc_f, g_out_r.reshape(1, dr), g_ffn.reshape(1, d), w_out_bf,
      w_router_t, b_router.reshape(ne, 1), jmat)


def _rank_kernel(idx_ref, tri_ref, rank_ref, cnt_ref, carry_s):
    c = pl.program_id(0)

    @pl.when(c == 0)
    def _():
        carry_s[...] = jnp.zeros_like(carry_s)

    l = tri_ref.shape[0]
    eidx = lax.broadcasted_iota(jnp.int32, (N_EXPERTS, l), 0)
    for sub in range(idx_ref.shape[1] // l):
        lanes = slice(sub * l, (sub + 1) * l)
        onehots = [eidx == idx_ref[k:k + 1, lanes] for k in range(TOP_K)]
        ohs = [jnp.where(o, 1.0, 0.0) for o in onehots]
        prefix = jnp.dot(jnp.concatenate(ohs, axis=0).astype(BF16), tri_ref[...], preferred_element_type=F32)
        for k in range(TOP_K):
            carry = carry_s[:, 0:1]
            pk = prefix[k * N_EXPERTS:(k + 1) * N_EXPERTS]
            rank = jnp.sum(jnp.where(onehots[k], pk - 1.0 + carry, 0.0), axis=0, keepdims=True)
            rank_ref[k:k + 1, lanes] = rank.astype(jnp.int32)
            carry_s[...] = carry_s[...] + jnp.sum(ohs[k], axis=1, keepdims=True)
    cnt_ref[...] = carry_s[...].astype(jnp.int32)


def _dispatch_ranks(idx, tri, tl):
    k, t = idx.shape
    return pl.pallas_call(
        _rank_kernel,
        grid=(t // tl,),
        in_specs=[pl.BlockSpec((k, tl), lambda c: (0, c)),
                  pl.BlockSpec(tri.shape, lambda c: (0, 0))],
        out_specs=[pl.BlockSpec((k, tl), lambda c: (0, c)),
                   pl.BlockSpec((N_EXPERTS, LANES), lambda c: (0, 0))],
        out_shape=[jax.ShapeDtypeStruct((k, t), jnp.int32),
                   jax.ShapeDtypeStruct((N_EXPERTS, LANES), jnp.int32)],
        scratch_shapes=[pltpu.VMEM((N_EXPERTS, LANES), F32)],
        compiler_params=_cparams(("arbitrary",)),
        name="dispatch_ranks",
    )(idx, tri)


def _moe_kernel(be_ref, bv_ref, nu_ref, x_ref, wgu_ref, bgu_ref, wd_ref, bd_ref, o_ref, wgu_s, wd_s, *, dff):
    i = pl.program_id(0)
    h = x_ref.shape[1]

    @pl.when(jnp.logical_or(i == 0, be_ref[i] != be_ref[jnp.maximum(i - 1, 0)]))
    def _():
        wgu_s[...] = wgu_ref[0].astype(BF16)
        wd_s[...] = wd_ref[0].astype(BF16)

    @pl.when(i < nu_ref[0])
    def _():
        rows = lax.broadcasted_iota(jnp.int32, x_ref.shape, 0)
        xw = jnp.where(rows < bv_ref[i], x_ref[...], 0)
        xa, xb = _unpack_halves(xw)
        xa = xa.astype(BF16)
        xb = xb.astype(BF16)
        acc = None
        for c in range(dff // MOE_FF_CHUNK):
            gs = slice(c * MOE_FF_CHUNK, (c + 1) * MOE_FF_CHUNK)
            us = slice(dff + c * MOE_FF_CHUNK, dff + (c + 1) * MOE_FF_CHUNK)
            g = jnp.dot(xa, wgu_s[:h, gs], preferred_element_type=F32)
            g += jnp.dot(xb, wgu_s[h:, gs], preferred_element_type=F32)
            u = jnp.dot(xa, wgu_s[:h, us], preferred_element_type=F32)
            u += jnp.dot(xb, wgu_s[h:, us], preferred_element_type=F32)
            gt = jnp.minimum(g + bgu_ref[0, :, gs], SWIGLU_LIMIT)
            up = jnp.clip(u + bgu_ref[0, :, us], -SWIGLU_LIMIT, SWIGLU_LIMIT)
            act = (up + 1.0) * (gt * _sigmoid(SWIGLU_ALPHA * gt))
            part = jnp.dot(act.astype(BF16), wd_s[gs, :], preferred_element_type=F32)
            acc = part if acc is None else acc + part
        o_ref[...] = _pack_halves(acc + bd_ref[0])


def _moe_experts(blk_expert, blk_valid, n_used, xs, wgu, bgu, wd, bd, tmm):
    cap, h = xs.shape
    ne, d, dff2 = wgu.shape
    dff = dff2 // 2
    row_blk = lambda i, be, bv, nu: (jnp.maximum(jnp.minimum(i, nu[0] - 1), 0), 0)
    wsel = lambda i, be, bv, nu: (be[i], 0, 0)
    grid_spec = pltpu.PrefetchScalarGridSpec(
        num_scalar_prefetch=3,
        grid=(cap // tmm,),
        in_specs=[pl.BlockSpec((tmm, h), row_blk),
                  pl.BlockSpec((1, d, dff2), wsel),
                  pl.BlockSpec((1, 1, dff2), wsel),
                  pl.BlockSpec((1, dff, d), wsel),
                  pl.BlockSpec((1, 1, d), wsel)],
        out_specs=pl.BlockSpec((tmm, h), row_blk),
        scratch_shapes=[pltpu.VMEM((d, dff2), BF16), pltpu.VMEM((dff, d), BF16)],
    )
    return pl.pallas_call(
        functools.partial(_moe_kernel, dff=dff),
        grid_spec=grid_spec,
        out_shape=jax.ShapeDtypeStruct((cap, h), jnp.int32),
        compiler_params=_cparams(("arbitrary",)),
        name="moe_experts",
    )(blk_expert, blk_valid, n_used, xs, wgu, bgu.reshape(ne, 1, dff2), wd, bd.reshape(ne, 1, d))


SC_CHUNK = 64
SC_ID_CHUNK = 128


def _sc_workers():
    info = plsc.get_sparse_core_info()
    return info.num_cores, info.num_subcores


def _sc_scatter_into(out_ref, rows, dest):
    n, w = rows.shape
    nc, ns = _sc_workers()
    per_w = n // (nc * ns)
    assert per_w % SC_CHUNK == 0
    mesh = plsc.VectorSubcoreMesh(core_axis_name="c", subcore_axis_name="s")

    @functools.partial(
        pl.kernel, mesh=mesh, out_type=(),
        scratch_types=[pltpu.VMEM((SC_CHUNK,), jnp.int32),
                       pltpu.VMEM((SC_CHUNK, w), jnp.int32),
                       pltpu.SemaphoreType.DMA],
    )
    def scatter_rows(rows_hbm, dest_hbm, out_hbm, idx_v, rows_v, sem):
        base = (lax.axis_index("s") * nc + lax.axis_index("c")) * per_w

        @pl.loop(0, per_w // SC_CHUNK)
        def _(j):
            off = pl.multiple_of(base + j * SC_CHUNK, SC_CHUNK)
            pltpu.sync_copy(rows_hbm.at[pl.ds(off, SC_CHUNK)], rows_v)
            pltpu.sync_copy(dest_hbm.at[pl.ds(off, SC_CHUNK)], idx_v)
            pltpu.async_copy(rows_v, out_hbm.at[idx_v], sem).wait()

    scatter_rows(rows, dest, out_ref)


def _sc_scatter_ids(dest, out_rows):
    n = dest.shape[0]
    info = plsc.get_sparse_core_info()
    nc, ns, nl = info.num_cores, info.num_subcores, info.num_lanes
    per_w = n // (nc * ns)
    chunk = SC_ID_CHUNK
    assert per_w % (2 * chunk) == 0
    mesh = plsc.VectorSubcoreMesh(core_axis_name="c", subcore_axis_name="s")
    idx_t = pltpu.VMEM((chunk,), jnp.int32)
    rows_t = pltpu.VMEM((chunk, LANES), jnp.int32)

    @functools.partial(
        pl.kernel, mesh=mesh,
        out_type=jax.ShapeDtypeStruct((out_rows, LANES), jnp.int32),
        scratch_types=[idx_t, idx_t, rows_t, rows_t, pltpu.SemaphoreType.DMA, pltpu.SemaphoreType.DMA],
    )
    def scatter_ids(dest_hbm, out_hbm, idx_a, idx_b, rows_a, rows_b, sem_a, sem_b):
        base = (lax.axis_index("s") * nc + lax.axis_index("c")) * per_w

        def start(off, idx_v, rows_v, sem):
            for r in range(chunk):
                rows_v[r, pl.ds(0, nl)] = jnp.zeros((nl,), jnp.int32) + (off + r)
            pltpu.sync_copy(dest_hbm.at[pl.ds(off, chunk)], idx_v)
            return pltpu.async_copy(rows_v, out_hbm.at[idx_v], sem)

        @pl.loop(0, per_w // (2 * chunk))
        def _(j):
            off = pl.multiple_of(base + j * (2 * chunk), 2 * chunk)
            copy_a = start(off, idx_a, rows_a, sem_a)
            copy_b = start(off + chunk, idx_b, rows_b, sem_b)
            copy_a.wait()
            copy_b.wait()

    return scatter_ids(dest)


def _sc_gather(table, idx):
    n = idx.shape[0]
    w = table.shape[1]
    nc, ns = _sc_workers()
    per_w = n // (nc * ns)
    assert per_w % SC_CHUNK == 0
    mesh = plsc.VectorSubcoreMesh(core_axis_name="c", subcore_axis_name="s")

    @functools.partial(
        pl.kernel, mesh=mesh,
        out_type=jax.ShapeDtypeStruct((n, w), jnp.int32),
        scratch_types=[pltpu.VMEM((SC_CHUNK,), jnp.int32),
                       pltpu.VMEM((SC_CHUNK, w), jnp.int32),
                       pltpu.SemaphoreType.DMA],
    )
    def gather_rows(table_hbm, idx_hbm, out_hbm, idx_v, rows_v, sem):
        base = (lax.axis_index("s") * nc + lax.axis_index("c")) * per_w

        @pl.loop(0, per_w // SC_CHUNK)
        def _(j):
            off = pl.multiple_of(base + j * SC_CHUNK, SC_CHUNK)
            pltpu.sync_copy(idx_hbm.at[pl.ds(off, SC_CHUNK)], idx_v)
            pltpu.async_copy(table_hbm.at[idx_v], rows_v, sem).wait()
            pltpu.sync_copy(rows_v, out_hbm.at[pl.ds(off, SC_CHUNK)])

    return gather_rows(table, idx)


def _combine_kernel(x1_ref, y_ref, gate_ref, gtf_ref, g_ref, o_ref):
    h = y_ref.shape[3]
    gates = gate_ref[0]
    moe_a = moe_b = None
    for k in range(TOP_K):
        ya, yb = _unpack_halves(y_ref[k, 0])
        gk = gates[:, k:k + 1]
        moe_a = gk * ya if k == 0 else moe_a + gk * ya
        moe_b = gk * yb if k == 0 else moe_b + gk * yb
    za = x1_ref[0, :, :h] + gtf_ref[0, :, :h] * moe_a
    zb = x1_ref[0, :, h:] + gtf_ref[0, :, h:] * moe_b
    ms = (jnp.sum(za * za, axis=-1, keepdims=True) + jnp.sum(zb * zb, axis=-1, keepdims=True)) / (2 * h)
    inv = lax.rsqrt(ms + EPS)
    o_ref[0, :, :h] = za * inv * g_ref[:, :h]
    o_ref[0, :, h:] = zb * inv * g_ref[:, h:]


def _combine(x1, yk, gates_t, gt_f, g_final, tm):
    b, s, d = x1.shape
    h = yk.shape[3]
    return pl.pallas_call(
        _combine_kernel,
        grid=(b, s // tm),
        in_specs=[pl.BlockSpec((1, tm, d), lambda i, t: (i, t, 0)),
                  pl.BlockSpec((TOP_K, 1, tm, h), lambda i, t: (0, i, t, 0)),
                  pl.BlockSpec((1, tm, TOP_K), lambda i, t: (i, t, 0)),
                  pl.BlockSpec((1, 1, d), lambda i, t: (i, 0, 0)),
                  pl.BlockSpec((1, d), lambda i, t: (0, 0))],
        out_specs=pl.BlockSpec((1, tm, d), lambda i, t: (i, t, 0)),
        out_shape=jax.ShapeDtypeStruct((b, s, d), F32),
        compiler_params=_cparams(("parallel", "arbitrary")),
        name="combine",
    )(x1, yk, gates_t, gt_f, g_final.reshape(1, d))


def _dft_tables(rows, gd):
    seq = rows * GRID_W
    n = np.arange(rows)
    ang1 = 2.0 * np.pi * np.outer(n, n) / rows
    d2 = np.concatenate([np.cos(ang1), -np.sin(ang1)], axis=0)
    k1 = np.arange(rows)[:, None, None]
    k2 = np.arange(GRID_W)[None, :, None]
    n2 = np.arange(GRID_W)[None, None, :]
    ang2 = 2.0 * np.pi * ((n2 * (k1 + rows * k2)) % seq) / seq
    ec, es = np.cos(ang2), np.sin(ang2)
    etab = np.concatenate([np.concatenate([ec, es], axis=2),
                           np.concatenate([-es, ec], axis=2)], axis=1)
    c = np.arange(gd)
    angc = 2.0 * np.pi * np.outer(c, c) / gd
    scale = 1.0 / np.sqrt(seq * gd)
    return (jnp.asarray(d2, BF16), jnp.asarray(etab, BF16),
            jnp.asarray(np.cos(angc) * scale, F32), jnp.asarray(np.sin(angc) * scale, F32))


def _block_diag(w):
    h, i, o = w.shape
    eye = jnp.eye(h, dtype=w.dtype)
    return (eye[:, None, :, None] * w[:, :, None, :]).reshape(h * i, h * o)


def kernel(x, c, ctx, c_ctx, w_mod, b_mod, g_norm_mix, g_norm_ffn, w_in, w_fourier, conv_w, conv_b,
           rg_w_a, rg_b_a, rg_w_x, rg_b_x, rg_lam, g_out_fourier, g_out_rg, w_out, w_router, b_router,
           w_gate_up, b_gate_up, w_down, b_down, g_final):
    assert w_mod.shape[0] == 1, "single-layer stack only"
    b, s, d = x.shape
    df = w_fourier.shape[1] * w_fourier.shape[2]
    dr = conv_w.shape[2]
    gd = w_fourier.shape[2]
    rows = s // GRID_W
    t = b * s
    ne = w_router.shape[2]

    mrows = -(-(b + 1) // SUBLANES) * SUBLANES
    cond = jnp.zeros((mrows, d), F32).at[:b].set(c).at[b].set(c_ctx)
    mod = _adaln(cond, w_mod[0], b_mod[0])
    sh_m, sc_m, gt_m, sh_f, sc_f, gt_f = [mod[:b, k * d:(k + 1) * d].reshape(b, 1, d) for k in range(N_MOD)]
    csh_m = mod[b:b + 1, 0:d].reshape(1, 1, d)
    csc_m = mod[b:b + 1, d:2 * d].reshape(1, 1, d)

    tm = min(s, TOKEN_TILE)
    d2, etab, cmat, smat = _dft_tables(rows, gd)
    jmat = jnp.asarray(np.eye(GRID_W)[::-1].copy(), BF16)

    w_in_bf = w_in[0].astype(BF16)
    f, xs, gg = _stage_b(x, sh_m, sc_m, g_norm_mix[0], w_in_bf, jmat, df, dr, tm=tm)
    xr_ctx = _stage_b_ctx(ctx, csh_m, csc_m, g_norm_mix[0], w_in_bf[:, df:df + dr])

    wcat = jnp.stack([jnp.concatenate([_block_diag(rg_w_a[0, dd]), _block_diag(rg_w_x[0, dd])], axis=1)
                      for dd in range(2)]).astype(BF16)
    bcat = jnp.concatenate([rg_b_a[0], rg_b_x[0]], axis=1).reshape(2, 1, 2 * dr)
    lam = rg_lam[0].reshape(2, 1, dr)
    h0 = jnp.zeros((b, SUBLANES, dr), F32)
    _, hfin_ctx = _rg_scan(xr_ctx, h0, conv_w[0], conv_b[0], wcat, bcat, lam, tc=ctx.shape[1])
    hs, _ = _rg_scan(xs, hfin_ctx, conv_w[0], conv_b[0], wcat, bcat, lam, tc=min(s, SCAN_CHUNK))

    cw, sw = _fold_fourier(cmat, smat, w_fourier[0])
    wcs = jnp.concatenate([cw, sw], axis=1).astype(BF16)
    y = _fourier_stage1(f, d2, tl=min(GRID_W * df, 16384))
    fn = _fourier_stage2(y.reshape(b, 2, rows, GRID_W, df), etab, wcs, g_out_fourier[0],
                         kb=min(rows, 16))
    fn = fn.reshape(b, s, df)

    x1, h2, idx, gates = _stage_m(fn, hs, gg, x, gt_m, sh_f, sc_f, g_out_rg[0], g_norm_ffn[0],
                                  w_out[0].astype(BF16), w_router[0].T, b_router[0], jmat, tm=tm)

    tl = min(t, RANK_TILE)
    tri = jnp.asarray(np.triu(np.ones((tl, tl))), BF16)
    rank, cnt = _dispatch_ranks(idx, tri, min(t, RANK_STEP))
    counts = cnt[:, 0]
    tmm = MOE_ROW_TILE
    padded = (counts + tmm - 1) // tmm * tmm
    pad_end = jnp.cumsum(padded)
    pad_start = pad_end - padded
    eids = jnp.arange(ne, dtype=jnp.int32)
    dest = rank + jnp.sum(jnp.where(idx[:, :, None] == eids, pad_start, 0), axis=-1)
    n_blocks = -(-(t * TOP_K) // tmm) + ne
    cap = n_blocks * tmm
    n_used = (pad_end[-1] // tmm).astype(jnp.int32).reshape(1)
    blk_start = jnp.arange(n_blocks, dtype=jnp.int32) * tmm
    blk_expert = jnp.sum(blk_start[:, None] >= pad_end[None, :], axis=1).astype(jnp.int32)
    last_expert = jnp.sum(pad_end[-1] - tmm >= pad_end).astype(jnp.int32)
    blk_expert = jnp.minimum(blk_expert, last_expert)
    sel = blk_expert[:, None] == eids
    blk_first = jnp.sum(jnp.where(sel, pad_start, 0), axis=1)
    blk_count = jnp.sum(jnp.where(sel, counts, 0), axis=1)
    blk_valid = jnp.clip(blk_count - (blk_start - blk_first), 0, tmm).astype(jnp.int32)

    na = TOP_K * t
    inv = _sc_scatter_ids(dest.reshape(-1), cap)[:, 0].reshape(n_blocks, tmm)
    live = jnp.arange(tmm, dtype=jnp.int32)[None, :] < blk_valid[:, None]
    spread = jnp.arange(cap, dtype=jnp.int32).reshape(n_blocks, tmm) % t
    src_tok = jnp.where(live, inv % t, spread).reshape(-1)
    dst_row = jnp.where(live, inv, na + spread).reshape(-1)

    h2_rows = h2.reshape(t, d // 2)
    y_all = jax.empty_ref(jax.ShapeDtypeStruct(((TOP_K + 1) * t, d // 2), jnp.int32))
    unit = n_blocks // sum(MOE_PIPE)
    assert unit * sum(MOE_PIPE) == n_blocks
    blk0 = 0
    for parts in MOE_PIPE:
        nq = parts * unit
        blocks = slice(blk0, blk0 + nq)
        rows = slice(blk0 * tmm, (blk0 + nq) * tmm)
        x_q = _sc_gather(h2_rows, src_tok[rows])
        nu_q = jnp.clip(n_used - blk0, 0, nq).astype(jnp.int32)
        blk0 += nq
        y_q = _moe_experts(blk_expert[blocks], blk_valid[blocks], nu_q, x_q,
                           w_gate_up[0], b_gate_up[0], w_down[0], b_down[0], tmm)
        _sc_scatter_into(y_all, y_q, dst_row[rows])
    yk = y_all[...].reshape(TOP_K + 1, b, s, d // 2)
    return _combine(x1, yk, gates.T.reshape(b, s, TOP_K), gt_f, g_final, tm)
```

```python
import functools

import numpy as np
import jax
import jax.numpy as jnp
from jax import lax
from jax.experimental import pallas as pl
from jax.experimental.pallas import tpu as pltpu
from jax.experimental.pallas import tpu_sc as plsc

GRID_W = 64
CONV_W = 4
CONV_PAD_LO = 2
RG_C = 8.0
N_EXPERTS = 32
TOP_K = 4
SWIGLU_LIMIT = 7.0
SWIGLU_ALPHA = 1.702
N_MOD = 6
EPS = 1e-6

LANES = 128
SUBLANES = 8
VMEM_LIMIT_BYTES = 56 * 1024 * 1024
TOKEN_TILE = 1024
RANK_TILE = 512
RANK_STEP = 4096
SCAN_CHUNK = 512
MOE_ROW_TILE = 512
MOE_FF_CHUNK = 512
MOE_PIPE = (1, 3, 3, 1)

F32 = jnp.float32
BF16 = jnp.bfloat16


def _cparams(sem):
    return pltpu.CompilerParams(dimension_semantics=sem, vmem_limit_bytes=VMEM_LIMIT_BYTES)


def _split_bf16(a):
    hi = a.astype(BF16)
    lo = (a - hi.astype(F32)).astype(BF16)
    return hi, lo


def _dot3(a, b):
    ah, al = _split_bf16(a)
    bh, bl = _split_bf16(b)
    out = jnp.dot(ah, bh, preferred_element_type=F32)
    out += jnp.dot(ah, bl, preferred_element_type=F32)
    out += jnp.dot(al, bh, preferred_element_type=F32)
    return out


def _dot3_nt(a, b):
    dn = (((1,), (1,)), ((), ()))
    m = a.shape[0]
    ah, al = _split_bf16(a)
    bh, bl = _split_bf16(b)
    both = lax.dot_general(jnp.concatenate([ah, al], axis=0), bh, dn, preferred_element_type=F32)
    return both[:m] + both[m:] + lax.dot_general(ah, bl, dn, preferred_element_type=F32)


def _gelu_tanh(x):
    return 0.5 * x * (1.0 + jnp.tanh(0.7978845608028654 * (x + 0.044715 * (x * x * x))))


def _rms(x, g):
    return x * lax.rsqrt(jnp.mean(x * x, axis=-1, keepdims=True) + EPS) * g


def _pack_halves(v):
    h = v.shape[1] // 2
    hi = lax.bitcast_convert_type(v[:, :h].astype(BF16).astype(F32), jnp.uint32)
    lo = lax.bitcast_convert_type(v[:, h:].astype(BF16).astype(F32), jnp.uint32)
    return lax.bitcast_convert_type(hi | (lo >> 16), jnp.int32)


def _unpack_halves(w):
    u = lax.bitcast_convert_type(w, jnp.uint32)
    hi = lax.bitcast_convert_type(u & jnp.uint32(0xFFFF0000), F32)
    lo = lax.bitcast_convert_type(u << 16, F32)
    return hi, lo


def _adaln_kernel(c_ref, w_ref, b_ref, o_ref):
    s = c_ref[...]
    s = s * jax.nn.sigmoid(s)
    o_ref[...] = _dot3(s, w_ref[...]) + b_ref[...]


def _adaln(cond, w_mod, b_mod):
    m, d = cond.shape
    n = w_mod.shape[1]
    tn = n // N_MOD
    return pl.pallas_call(
        _adaln_kernel,
        grid=(n // tn,),
        in_specs=[pl.BlockSpec((m, d), lambda i: (0, 0)),
                  pl.BlockSpec((d, tn), lambda i: (0, i)),
                  pl.BlockSpec((1, tn), lambda i: (0, i))],
        out_specs=pl.BlockSpec((m, tn), lambda i: (0, i)),
        out_shape=jax.ShapeDtypeStruct((m, n), F32),
        compiler_params=_cparams(("arbitrary",)),
        name="adaln",
    )(cond, w_mod, b_mod.reshape(1, n))


def _fold_kernel(c_ref, s_ref, w_ref, cw_ref, sw_ref):
    w = w_ref[0]
    cw_ref[0] = _dot3(c_ref[...], w)
    sw_ref[0] = _dot3(s_ref[...], w)


def _fold_fourier(cmat, smat, w_f):
    g, gd, _ = w_f.shape
    spec_m = pl.BlockSpec((gd, gd), lambda i: (0, 0))
    spec_w = pl.BlockSpec((1, gd, gd), lambda i: (i, 0, 0))
    return pl.pallas_call(
        _fold_kernel,
        grid=(g,),
        in_specs=[spec_m, spec_m, spec_w],
        out_specs=[spec_w, spec_w],
        out_shape=[jax.ShapeDtypeStruct((g, gd, gd), F32)] * 2,
        compiler_params=_cparams(("arbitrary",)),
        name="fold_fourier",
    )(cmat, smat, w_f)


def _seg_pitch(seg_len):
    n8 = seg_len // SUBLANES
    return SUBLANES * (n8 + 1 - n8 % 2)


def _pitched_store(v, buf, blk0):
    pitch = _seg_pitch(GRID_W)
    for r in range(v.shape[0] // GRID_W):
        for c in range(v.shape[1] // LANES):
            buf[c, (blk0 + r) * pitch:(blk0 + r) * pitch + GRID_W, :] = (
                v[r * GRID_W:(r + 1) * GRID_W, c * LANES:(c + 1) * LANES])


def _pitched_gather(buf, nb, store):
    pitch = _seg_pitch(GRID_W)
    for pos in range(GRID_W):
        for c in range(buf.shape[0]):
            store(pos, c, buf[c, pl.ds(pos, nb, stride=pitch), :])


def _stage_b_kernel(x_ref, sh_ref, sc_ref, g_ref, w_ref, j_ref, f_ref, xs_ref, gg_ref, fbuf, *, df, dr):
    tm = x_ref.shape[1]
    h = _rms(x_ref[0], g_ref[...] * (1.0 + sc_ref[0])) + sh_ref[0]
    hb = h.astype(BF16)
    gr = jnp.dot(hb, w_ref[:, df + dr:], preferred_element_type=F32)
    gg_ref[0] = _gelu_tanh(gr).astype(BF16)
    _pitched_store(jnp.dot(hb, w_ref[:, :df], preferred_element_type=F32), fbuf, 0)

    def store_f(pos, c, tile):
        f_ref[0, :, pos * df + c * LANES:pos * df + (c + 1) * LANES] = tile.astype(BF16)
    _pitched_gather(fbuf, tm // GRID_W, store_f)
    xr = jnp.dot(hb, w_ref[:, df:df + dr], preferred_element_type=F32).astype(BF16)
    for r in range(tm // GRID_W):
        blk = xr[r * GRID_W:(r + 1) * GRID_W]
        if r % 2 == 1:
            blk = jnp.dot(j_ref[...], blk, preferred_element_type=F32).astype(BF16)
        xs_ref[0, r * GRID_W:(r + 1) * GRID_W, :] = blk


def _stage_b(x, shift, scale, g, w_in_bf, jmat, df, dr, tm):
    b, s, d = x.shape
    n = w_in_bf.shape[1]
    vec = pl.BlockSpec((1, 1, d), lambda i, t: (i, 0, 0))
    out = pl.BlockSpec((1, tm, df), lambda i, t: (i, t, 0))
    nb = tm // GRID_W
    tok = jax.ShapeDtypeStruct((b, s, df), BF16)
    return pl.pallas_call(
        functools.partial(_stage_b_kernel, df=df, dr=dr),
        grid=(b, s // tm),
        in_specs=[pl.BlockSpec((1, tm, d), lambda i, t: (i, t, 0)), vec, vec,
                  pl.BlockSpec((1, d), lambda i, t: (0, 0)),
                  pl.BlockSpec((d, n), lambda i, t: (0, 0)),
                  pl.BlockSpec((GRID_W, GRID_W), lambda i, t: (0, 0))],
        out_specs=[pl.BlockSpec((1, nb, GRID_W * df), lambda i, t: (i, t, 0)), out, out],
        out_shape=[jax.ShapeDtypeStruct((b, s // GRID_W, GRID_W * df), BF16), tok, tok],
        scratch_shapes=[pltpu.VMEM((df // LANES, nb * _seg_pitch(GRID_W), LANES), F32)],
        compiler_params=_cparams(("parallel", "arbitrary")),
        name="stage_b",
    )(x, shift, scale, g.reshape(1, d), w_in_bf, jmat)


def _stage_b_ctx_kernel(x_ref, sh_ref, sc_ref, g_ref, w_ref, xr_ref):
    h = _rms(x_ref[0], g_ref[...] * (1.0 + sc_ref[0])) + sh_ref[0]
    xr_ref[0] = jnp.dot(h.astype(BF16), w_ref[...], preferred_element_type=F32).astype(BF16)


def _stage_b_ctx(ctx, shift, scale, g, w_xr_bf):
    b, s, d = ctx.shape
    dr = w_xr_bf.shape[1]
    vec = pl.BlockSpec((1, 1, d), lambda i: (0, 0, 0))
    return pl.pallas_call(
        _stage_b_ctx_kernel,
        grid=(b,),
        in_specs=[pl.BlockSpec((1, s, d), lambda i: (i, 0, 0)), vec, vec,
                  pl.BlockSpec((1, d), lambda i: (0, 0)),
                  pl.BlockSpec((d, dr), lambda i: (0, 0))],
        out_specs=pl.BlockSpec((1, s, dr), lambda i: (i, 0, 0)),
        out_shape=jax.ShapeDtypeStruct((b, s, dr), BF16),
        compiler_params=_cparams(("arbitrary",)),
        name="stage_b_ctx",
    )(ctx, shift, scale, g.reshape(1, d), w_xr_bf)


HALO = 16


def _sigmoid(x):
    return 0.5 * jnp.tanh(0.5 * x) + 0.5


def _rg_kernel(xs_ref, h0_ref, cw_ref, cb_ref, w_ref, b_ref, lam_ref, pm_ref, pmt_ref, out_ref, hfin_ref,
               hf_s, xc_s, a_s, u_s, hl_s, p_s, c_s, hc_s, *, tc, nchunk, seq, dr):
    p = pl.program_id(1)
    j = pl.program_id(2)
    cidx = jnp.where(p == 0, j, nchunk - 1 - j)
    start = pl.multiple_of(cidx * tc, tc)
    nseg = SUBLANES
    sl = tc // nseg
    sub = lax.broadcasted_iota(jnp.int32, (nseg, dr), 0)

    @pl.when(p == 0)
    def _():
        xp = jnp.dot(pm_ref[...], xs_ref[0, pl.ds(start, tc), :], preferred_element_type=F32)
        pstart = pl.multiple_of(jnp.maximum(start - HALO, 0), HALO)
        nstart = pl.multiple_of(jnp.minimum(start + tc, seq - HALO), HALO)
        prev = xs_ref[0, pl.ds(pstart, HALO), :].astype(F32)
        nxt = xs_ref[0, pl.ds(nstart, HALO), :].astype(F32)
        prev = jnp.where(cidx > 0, prev, 0.0)
        nxt = jnp.where(cidx < nchunk - 1, nxt, 0.0)
        tm2 = jnp.where(sub == 0, prev[HALO - 2:HALO - 1], pltpu.roll(xp[(sl - 2) * nseg:(sl - 1) * nseg], 1, 0))
        tm1 = jnp.where(sub == 0, prev[HALO - 1:HALO], pltpu.roll(xp[(sl - 1) * nseg:sl * nseg], 1, 0))
        tp1 = jnp.where(sub == nseg - 1, nxt[0:1], pltpu.roll(xp[0:nseg], nseg - 1, 0))
        ext = jnp.concatenate([tm2, tm1, xp, tp1], axis=0)
        xc = cb_ref[...] + cw_ref[0:1, :] * ext[0:tc]
        for k in range(1, CONV_W):
            xc = xc + cw_ref[k:k + 1, :] * ext[k * nseg:k * nseg + tc]
        xc_s[pl.ds(start, tc), :] = xc

    xc = xc_s[pl.ds(start, tc), :]
    gates = jnp.dot(xc.astype(BF16), w_ref[p], preferred_element_type=F32) + b_ref[p]
    i = _sigmoid(gates[:, dr:])
    half_c = (-0.5 * RG_C) * jax.nn.softplus(-lam_ref[p])
    log_a = half_c * jnp.tanh(0.5 * gates[:, :dr]) + half_c
    a = jnp.exp(log_a)
    a_s[...] = a
    w = -jnp.tanh(log_a) * (1.0 + a * a)
    u_s[...] = jnp.where(w > 0.0, w * lax.rsqrt(w), 0.0) * (i * xc)

    @pl.when(jnp.logical_and(p == 0, j == 0))
    def _():
        hfin_ref[...] = jnp.zeros_like(hfin_ref)

    @pl.when(j == 0)
    def _():
        hc_s[0:1, :] = h0_ref[0, pl.ds(p, 1), :]

    def segment_scan(reverse):
        def body(q, carry):
            t = (sl - 1 - q) if reverse else q
            rows = pl.ds(pl.multiple_of(t * nseg, nseg), nseg)
            h, pr = carry
            av = a_s[rows, :]
            h = av * h + u_s[rows, :]
            pr = av * pr
            hl_s[rows, :] = h
            p_s[rows, :] = pr
            return h, pr
        h_end, p_end = lax.fori_loop(0, sl, body, (jnp.zeros((nseg, dr), F32), jnp.ones((nseg, dr), F32)),
                                     unroll=4)
        carry = hc_s[0:1, :]
        for g in (range(nseg - 1, -1, -1) if reverse else range(nseg)):
            c_s[g:g + 1, :] = carry
            carry = h_end[g:g + 1, :] + p_end[g:g + 1, :] * carry
        hc_s[0:1, :] = carry
        return c_s[...]

    def corrected(cin):
        h = hl_s[...].reshape(sl, nseg, dr) + p_s[...].reshape(sl, nseg, dr) * cin[None]
        return h.reshape(tc, dr)

    @pl.when(p == 0)
    def _():
        hf_s[pl.ds(start, tc), :] = corrected(segment_scan(False))

    @pl.when(p == 1)
    def _():
        tot = corrected(segment_scan(True)) + hf_s[pl.ds(start, tc), :]
        out_ref[0] = jnp.dot(pmt_ref[...], tot.astype(BF16), preferred_element_type=F32).astype(BF16)

    @pl.when(j == nchunk - 1)
    def _():
        hfin_ref[0, pl.ds(p, 1), :] = hc_s[0:1, :]


def _rg_scan(xs, h0, conv_w, conv_b, wcat, bcat, lam, tc):
    b, s, dr = xs.shape
    nchunk = s // tc
    last = nchunk - 1
    sl = tc // SUBLANES
    src = (np.arange(tc) % SUBLANES) * sl + np.arange(tc) // SUBLANES
    pm = np.zeros((tc, tc), np.float32)
    pm[np.arange(tc), src] = 1.0
    chunk_buf = pltpu.VMEM((tc, dr), F32)
    full2 = lambda shape: pl.BlockSpec(shape, lambda i, p, j: (0,) * len(shape))
    return pl.pallas_call(
        functools.partial(_rg_kernel, tc=tc, nchunk=nchunk, seq=s, dr=dr),
        grid=(b, 2, nchunk),
        in_specs=[pl.BlockSpec((1, s, dr), lambda i, p, j: (i, 0, 0)),
                  pl.BlockSpec((1, SUBLANES, dr), lambda i, p, j: (i, 0, 0)),
                  full2((CONV_W, dr)), full2((1, dr)),
                  full2((2, dr, 2 * dr)), full2((2, 1, 2 * dr)), full2((2, 1, dr)),
                  full2((tc, tc)), full2((tc, tc))],
        out_specs=[pl.BlockSpec((1, tc, dr), lambda i, p, j: (i, jnp.where(p == 0, last, last - j), 0)),
                   pl.BlockSpec((1, SUBLANES, dr), lambda i, p, j: (i, 0, 0))],
        out_shape=[jax.ShapeDtypeStruct((b, s, dr), BF16),
                   jax.ShapeDtypeStruct((b, SUBLANES, dr), F32)],
        scratch_shapes=[pltpu.VMEM((s, dr), F32), pltpu.VMEM((s, dr), F32),
                        chunk_buf, chunk_buf, chunk_buf, chunk_buf,
                        pltpu.VMEM((SUBLANES, dr), F32), pltpu.VMEM((SUBLANES, dr), F32)],
        compiler_params=_cparams(("arbitrary", "arbitrary", "arbitrary")),
        name="rg_scan",
    )(xs, h0, conv_w, conv_b.reshape(1, dr), wcat, bcat, lam, jnp.asarray(pm, BF16), jnp.asarray(pm.T, BF16))


def _f1_kernel(d_ref, x_ref, y_ref):
    y_ref[0] = jnp.dot(d_ref[...], x_ref[0], preferred_element_type=F32).astype(BF16)


def _fourier_stage1(fv, d2, tl):
    b, r, n = fv.shape
    return pl.pallas_call(
        _f1_kernel,
        grid=(b, n // tl),
        in_specs=[pl.BlockSpec((2 * r, r), lambda i, l: (0, 0)),
                  pl.BlockSpec((1, r, tl), lambda i, l: (i, 0, l))],
        out_specs=pl.BlockSpec((1, 2 * r, tl), lambda i, l: (i, 0, l)),
        out_shape=jax.ShapeDtypeStruct((b, 2 * r, n), BF16),
        compiler_params=_cparams(("parallel", "arbitrary")),
        name="fourier_stage1",
    )(d2, fv)


def _f2_kernel(y_ref, e_ref, wcs_ref, g_ref, o_ref, obuf, *, kb, df):
    zr, zi = [], []
    for q in range(kb):
        yk = jnp.concatenate([y_ref[0, 0, q], y_ref[0, 1, q]], axis=0)
        z = jnp.dot(e_ref[q], yk, preferred_element_type=F32)
        zr.append(z[:GRID_W])
        zi.append(z[GRID_W:])
    zr = jnp.concatenate(zr, axis=0).astype(BF16)
    zi = jnp.concatenate(zi, axis=0).astype(BF16)
    gd = wcs_ref.shape[2]
    o = jnp.concatenate(
        [jnp.dot(jnp.concatenate([zr[:, g * gd:(g + 1) * gd], zi[:, g * gd:(g + 1) * gd]], axis=1), wcs_ref[g],
                 preferred_element_type=F32) for g in range(df // gd)], axis=1)
    on = _rms(o, g_ref[...])

    def store_o(pos, c, tile):
        o_ref[0, pos, :, c * LANES:(c + 1) * LANES] = tile.astype(BF16)
    _pitched_store(on, obuf, 0)
    _pitched_gather(obuf, kb, store_o)


def _fourier_stage2(y5, etab, wcs, g, kb):
    b, _, r, w, df = y5.shape
    return pl.pallas_call(
        functools.partial(_f2_kernel, kb=kb, df=df),
        grid=(b, r // kb),
        in_specs=[pl.BlockSpec((1, 2, kb, w, df), lambda i, k: (i, 0, k, 0, 0)),
                  pl.BlockSpec((kb, 2 * w, 2 * w), lambda i, k: (k, 0, 0)),
                  pl.BlockSpec(wcs.shape, lambda i, k: (0, 0, 0)),
                  pl.BlockSpec((1, df), lambda i, k: (0, 0))],
        out_specs=pl.BlockSpec((1, w, kb, df), lambda i, k: (i, 0, k, 0)),
        out_shape=jax.ShapeDtypeStruct((b, w, r, df), BF16),
        scratch_shapes=[pltpu.VMEM((df // LANES, kb * _seg_pitch(GRID_W), LANES), F32)],
        compiler_params=_cparams(("parallel", "arbitrary")),
        name="fourier_stage2",
    )(y5, etab, wcs, g.reshape(1, df))


def _stage_m_kernel(fn_ref, hs_ref, gg_ref, x_ref, gtm_ref, shf_ref, scf_ref, gr_ref, gffn_ref,
                    wo_ref, wr_ref, br_ref, j_ref, x1_ref, h2_ref, idx_ref, gate_ref, *, df):
    tm = x_ref.shape[1]
    hs = hs_ref[0]
    blocks = []
    for r in range(tm // GRID_W):
        blk = hs[r * GRID_W:(r + 1) * GRID_W]
        if r % 2 == 1:
            blk = jnp.dot(j_ref[...], blk, preferred_element_type=F32)
        blocks.append(blk.astype(F32))
    rg = jnp.concatenate(blocks, axis=0) * gg_ref[0].astype(F32)
    rgn = _rms(rg, gr_ref[...]).astype(BF16)
    mix = jnp.dot(fn_ref[0], wo_ref[:df, :], preferred_element_type=F32)
    mix += jnp.dot(rgn, wo_ref[df:, :], preferred_element_type=F32)
    x1 = x_ref[0] + gtm_ref[0] * mix
    x1_ref[0] = x1
    h2 = _rms(x1, gffn_ref[...] * (1.0 + scf_ref[0])) + shf_ref[0]
    h2_ref[0] = _pack_halves(h2)

    logits = _dot3_nt(wr_ref[...], h2) + br_ref[...]
    eidx = lax.broadcasted_iota(jnp.int32, logits.shape, 0)
    vals, idxs = [], []
    for _ in range(TOP_K):
        m = jnp.max(logits, axis=0, keepdims=True)
        sel = jnp.min(jnp.where(logits == m, eidx, N_EXPERTS), axis=0, keepdims=True)
        vals.append(m)
        idxs.append(sel)
        logits = jnp.where(eidx == sel, -jnp.inf, logits)
    ex = [jnp.exp(v - vals[0]) for v in vals]
    den = ex[0] + ex[1] + ex[2] + ex[3]
    for k in range(TOP_K):
        gate_ref[k:k + 1, :] = ex[k] / den
        idx_ref[k:k + 1, :] = idxs[k]


def _stage_m(fn, hs, gg, x, gt_m, sh_f, sc_f, g_out_r, g_ffn, w_out_bf, w_router_t, b_router, jmat, tm):
    b, s, d = x.shape
    df = fn.shape[2]
    dr = hs.shape[2]
    nt = s // tm
    ne = w_router_t.shape[0]
    vec = pl.BlockSpec((1, 1, d), lambda i, t: (i, 0, 0))
    half = lambda dd: pl.BlockSpec((1, tm, dd), lambda i, t: (i, t, 0))
    full = lambda shape: pl.BlockSpec(shape, lambda i, t: (0,) * len(shape))
    tok = pl.BlockSpec((TOP_K, tm), lambda i, t: (0, i * nt + t))
    return pl.pallas_call(
        functools.partial(_stage_m_kernel, df=df),
        grid=(b, nt),
        in_specs=[half(df), half(dr), half(dr), half(d), vec, vec, vec,
                  full((1, dr)), full((1, d)), full((d, d)), full((ne, d)), full((ne, 1)),
                  full((GRID_W, GRID_W))],
        out_specs=[half(d), half(d // 2), tok, tok],
        out_shape=[jax.ShapeDtypeStruct((b, s, d), F32), jax.ShapeDtypeStruct((b, s, d // 2), jnp.int32),
                   jax.ShapeDtypeStruct((TOP_K, b * s), jnp.int32),
                   jax.ShapeDtypeStruct((TOP_K, b * s), F32)],
        compiler_params=_cparams(("parallel", "arbitrary")),
        name="stage_m",
    )(fn, hs, gg, x, gt_m, sh_f, sc_f, g_out_r.reshape(1, dr), g_ffn.reshape(1, d), w_out_bf,
      w_router_t, b_router.reshape(ne, 1), jmat)


def _rank_kernel(idx_ref, tri_ref, rank_ref, cnt_ref, carry_s):
    c = pl.program_id(0)

    @pl.when(c == 0)
    def _():
        carry_s[...] = jnp.zeros_like(carry_s)

    l = tri_ref.shape[0]
    eidx = lax.broadcasted_iota(jnp.int32, (N_EXPERTS, l), 0)
    for sub in range(idx_ref.shape[1] // l):
        lanes = slice(sub * l, (sub + 1) * l)
        onehots = [eidx == idx_ref[k:k + 1, lanes] for k in range(TOP_K)]
        ohs = [jnp.where(o, 1.0, 0.0) for o in onehots]
        prefix = jnp.dot(jnp.concatenate(ohs, axis=0).astype(BF16), tri_ref[...], preferred_element_type=F32)
        for k in range(TOP_K):
            carry = carry_s[:, 0:1]
            pk = prefix[k * N_EXPERTS:(k + 1) * N_EXPERTS]
            rank = jnp.sum(jnp.where(onehots[k], pk - 1.0 + carry, 0.0), axis=0, keepdims=True)
            rank_ref[k:k + 1, lanes] = rank.astype(jnp.int32)
            carry_s[...] = carry_s[...] + jnp.sum(ohs[k], axis=1, keepdims=True)
    cnt_ref[...] = carry_s[...].astype(jnp.int32)


def _dispatch_ranks(idx, tri, tl):
    k, t = idx.shape
    return pl.pallas_call(
        _rank_kernel,
        grid=(t // tl,),
        in_specs=[pl.BlockSpec((k, tl), lambda c: (0, c)),
                  pl.BlockSpec(tri.shape, lambda c: (0, 0))],
        out_specs=[pl.BlockSpec((k, tl), lambda c: (0, c)),
                   pl.BlockSpec((N_EXPERTS, LANES), lambda c: (0, 0))],
        out_shape=[jax.ShapeDtypeStruct((k, t), jnp.int32),
                   jax.ShapeDtypeStruct((N_EXPERTS, LANES), jnp.int32)],
        scratch_shapes=[pltpu.VMEM((N_EXPERTS, LANES), F32)],
        compiler_params=_cparams(("arbitrary",)),
        name="dispatch_ranks",
    )(idx, tri)


def _moe_kernel(be_ref, bv_ref, nu_ref, x_ref, wgu_ref, bgu_ref, wd_ref, bd_ref, o_ref, wgu_s, wd_s, *, dff):
    i = pl.program_id(0)
    h = x_ref.shape[1]

    @pl.when(jnp.logical_or(i == 0, be_ref[i] != be_ref[jnp.maximum(i - 1, 0)]))
    def _():
        wgu_s[...] = wgu_ref[0].astype(BF16)
        wd_s[...] = wd_ref[0].astype(BF16)

    @pl.when(i < nu_ref[0])
    def _():
        rows = lax.broadcasted_iota(jnp.int32, x_ref.shape, 0)
        xw = jnp.where(rows < bv_ref[i], x_ref[...], 0)
        xa, xb = _unpack_halves(xw)
        xa = xa.astype(BF16)
        xb = xb.astype(BF16)
        acc = None
        for c in range(dff // MOE_FF_CHUNK):
            gs = slice(c * MOE_FF_CHUNK, (c + 1) * MOE_FF_CHUNK)
            us = slice(dff + c * MOE_FF_CHUNK, dff + (c + 1) * MOE_FF_CHUNK)
            g = jnp.dot(xa, wgu_s[:h, gs], preferred_element_type=F32)
            g += jnp.dot(xb, wgu_s[h:, gs], preferred_element_type=F32)
            u = jnp.dot(xa, wgu_s[:h, us], preferred_element_type=F32)
            u += jnp.dot(xb, wgu_s[h:, us], preferred_element_type=F32)
            gt = jnp.minimum(g + bgu_ref[0, :, gs], SWIGLU_LIMIT)
            up = jnp.clip(u + bgu_ref[0, :, us], -SWIGLU_LIMIT, SWIGLU_LIMIT)
            act = (up + 1.0) * (gt * _sigmoid(SWIGLU_ALPHA * gt))
            part = jnp.dot(act.astype(BF16), wd_s[gs, :], preferred_element_type=F32)
            acc = part if acc is None else acc + part
        o_ref[...] = _pack_halves(acc + bd_ref[0])


def _moe_experts(blk_expert, blk_valid, n_used, xs, wgu, bgu, wd, bd, tmm):
    cap, h = xs.shape
    ne, d, dff2 = wgu.shape
    dff = dff2 // 2
    row_blk = lambda i, be, bv, nu: (jnp.maximum(jnp.minimum(i, nu[0] - 1), 0), 0)
    wsel = lambda i, be, bv, nu: (be[i], 0, 0)
    grid_spec = pltpu.PrefetchScalarGridSpec(
        num_scalar_prefetch=3,
        grid=(cap // tmm,),
        in_specs=[pl.BlockSpec((tmm, h), row_blk),
                  pl.BlockSpec((1, d, dff2), wsel),
                  pl.BlockSpec((1, 1, dff2), wsel),
                  pl.BlockSpec((1, dff, d), wsel),
                  pl.BlockSpec((1, 1, d), wsel)],
        out_specs=pl.BlockSpec((tmm, h), row_blk),
        scratch_shapes=[pltpu.VMEM((d, dff2), BF16), pltpu.VMEM((dff, d), BF16)],
    )
    return pl.pallas_call(
        functools.partial(_moe_kernel, dff=dff),
        grid_spec=grid_spec,
        out_shape=jax.ShapeDtypeStruct((cap, h), jnp.int32),
        compiler_params=_cparams(("arbitrary",)),
        name="moe_experts",
    )(blk_expert, blk_valid, n_used, xs, wgu, bgu.reshape(ne, 1, dff2), wd, bd.reshape(ne, 1, d))


SC_CHUNK = 64
SC_ID_CHUNK = 128


def _sc_workers():
    info = plsc.get_sparse_core_info()
    return info.num_cores, info.num_subcores


def _sc_scatter_into(out_ref, rows, dest):
    n, w = rows.shape
    nc, ns = _sc_workers()
    per_w = n // (nc * ns)
    assert per_w % SC_CHUNK == 0
    mesh = plsc.VectorSubcoreMesh(core_axis_name="c", subcore_axis_name="s")

    @functools.partial(
        pl.kernel, mesh=mesh, out_type=(),
        scratch_types=[pltpu.VMEM((SC_CHUNK,), jnp.int32),
                       pltpu.VMEM((SC_CHUNK, w), jnp.int32),
                       pltpu.SemaphoreType.DMA],
    )
    def scatter_rows(rows_hbm, dest_hbm, out_hbm, idx_v, rows_v, sem):
        base = (lax.axis_index("s") * nc + lax.axis_index("c")) * per_w

        @pl.loop(0, per_w // SC_CHUNK)
        def _(j):
            off = pl.multiple_of(base + j * SC_CHUNK, SC_CHUNK)
            pltpu.sync_copy(rows_hbm.at[pl.ds(off, SC_CHUNK)], rows_v)
            pltpu.sync_copy(dest_hbm.at[pl.ds(off, SC_CHUNK)], idx_v)
            pltpu.async_copy(rows_v, out_hbm.at[idx_v], sem).wait()

    scatter_rows(rows, dest, out_ref)


def _sc_scatter_ids(dest, out_rows):
    n = dest.shape[0]
    info = plsc.get_sparse_core_info()
    nc, ns, nl = info.num_cores, info.num_subcores, info.num_lanes
    per_w = n // (nc * ns)
    chunk = SC_ID_CHUNK
    assert per_w % (2 * chunk) == 0
    mesh = plsc.VectorSubcoreMesh(core_axis_name="c", subcore_axis_name="s")
    idx_t = pltpu.VMEM((chunk,), jnp.int32)
    rows_t = pltpu.VMEM((chunk, LANES), jnp.int32)

    @functools.partial(
        pl.kernel, mesh=mesh,
        out_type=jax.ShapeDtypeStruct((out_rows, LANES), jnp.int32),
        scratch_types=[idx_t, idx_t, rows_t, rows_t, pltpu.SemaphoreType.DMA, pltpu.SemaphoreType.DMA],
    )
    def scatter_ids(dest_hbm, out_hbm, idx_a, idx_b, rows_a, rows_b, sem_a, sem_b):
        base = (lax.axis_index("s") * nc + lax.axis_index("c")) * per_w

        def start(off, idx_v, rows_v, sem):
            for r in range(chunk):
                rows_v[r, pl.ds(0, nl)] = jnp.zeros((nl,), jnp.int32) + (off + r)
            pltpu.sync_copy(dest_hbm.at[pl.ds(off, chunk)], idx_v)
            return pltpu.async_copy(rows_v, out_hbm.at[idx_v], sem)

        @pl.loop(0, per_w // (2 * chunk))
        def _(j):
            off = pl.multiple_of(base + j * (2 * chunk), 2 * chunk)
            copy_a = start(off, idx_a, rows_a, sem_a)
            copy_b = start(off + chunk, idx_b, rows_b, sem_b)
            copy_a.wait()
            copy_b.wait()

    return scatter_ids(dest)


def _sc_gather(table, idx):
    n = idx.shape[0]
    w = table.shape[1]
    nc, ns = _sc_workers()
    per_w = n // (nc * ns)
    assert per_w % SC_CHUNK == 0
    mesh = plsc.VectorSubcoreMesh(core_axis_name="c", subcore_axis_name="s")

    @functools.partial(
        pl.kernel, mesh=mesh,
        out_type=jax.ShapeDtypeStruct((n, w), jnp.int32),
        scratch_types=[pltpu.VMEM((SC_CHUNK,), jnp.int32),
                       pltpu.VMEM((SC_CHUNK, w), jnp.int32),
                       pltpu.SemaphoreType.DMA],
    )
    def gather_rows(table_hbm, idx_hbm, out_hbm, idx_v, rows_v, sem):
        base = (lax.axis_index("s") * nc + lax.axis_index("c")) * per_w

        @pl.loop(0, per_w // SC_CHUNK)
        def _(j):
            off = pl.multiple_of(base + j * SC_CHUNK, SC_CHUNK)
            pltpu.sync_copy(idx_hbm.at[pl.ds(off, SC_CHUNK)], idx_v)
            pltpu.async_copy(table_hbm.at[idx_v], rows_v, sem).wait()
            pltpu.sync_copy(rows_v, out_hbm.at[pl.ds(off, SC_CHUNK)])

    return gather_rows(table, idx)


def _combine_kernel(x1_ref, y_ref, gate_ref, gtf_ref, g_ref, o_ref):
    h = y_ref.shape[3]
    gates = gate_ref[0]
    moe_a = moe_b = None
    for k in range(TOP_K):
        ya, yb = _unpack_halves(y_ref[k, 0])
        gk = gates[:, k:k + 1]
        moe_a = gk * ya if k == 0 else moe_a + gk * ya
        moe_b = gk * yb if k == 0 else moe_b + gk * yb
    za = x1_ref[0, :, :h] + gtf_ref[0, :, :h] * moe_a
    zb = x1_ref[0, :, h:] + gtf_ref[0, :, h:] * moe_b
    ms = (jnp.sum(za * za, axis=-1, keepdims=True) + jnp.sum(zb * zb, axis=-1, keepdims=True)) / (2 * h)
    inv = lax.rsqrt(ms + EPS)
    o_ref[0, :, :h] = za * inv * g_ref[:, :h]
    o_ref[0, :, h:] = zb * inv * g_ref[:, h:]


def _combine(x1, yk, gates_t, gt_f, g_final, tm):
    b, s, d = x1.shape
    h = yk.shape[3]
    return pl.pallas_call(
        _combine_kernel,
        grid=(b, s // tm),
        in_specs=[pl.BlockSpec((1, tm, d), lambda i, t: (i, t, 0)),
                  pl.BlockSpec((TOP_K, 1, tm, h), lambda i, t: (0, i, t, 0)),
                  pl.BlockSpec((1, tm, TOP_K), lambda i, t: (i, t, 0)),
                  pl.BlockSpec((1, 1, d), lambda i, t: (i, 0, 0)),
                  pl.BlockSpec((1, d), lambda i, t: (0, 0))],
        out_specs=pl.BlockSpec((1, tm, d), lambda i, t: (i, t, 0)),
        out_shape=jax.ShapeDtypeStruct((b, s, d), F32),
        compiler_params=_cparams(("parallel", "arbitrary")),
        name="combine",
    )(x1, yk, gates_t, gt_f, g_final.reshape(1, d))


def _dft_tables(rows, gd):
    seq = rows * GRID_W
    n = np.arange(rows)
    ang1 = 2.0 * np.pi * np.outer(n, n) / rows
    d2 = np.concatenate([np.cos(ang1), -np.sin(ang1)], axis=0)
    k1 = np.arange(rows)[:, None, None]
    k2 = np.arange(GRID_W)[None, :, None]
    n2 = np.arange(GRID_W)[None, None, :]
    ang2 = 2.0 * np.pi * ((n2 * (k1 + rows * k2)) % seq) / seq
    ec, es = np.cos(ang2), np.sin(ang2)
    etab = np.concatenate([np.concatenate([ec, es], axis=2),
                           np.concatenate([-es, ec], axis=2)], axis=1)
    c = np.arange(gd)
    angc = 2.0 * np.pi * np.outer(c, c) / gd
    scale = 1.0 / np.sqrt(seq * gd)
    return (jnp.asarray(d2, BF16), jnp.asarray(etab, BF16),
            jnp.asarray(np.cos(angc) * scale, F32), jnp.asarray(np.sin(angc) * scale, F32))


def _block_diag(w):
    h, i, o = w.shape
    eye = jnp.eye(h, dtype=w.dtype)
    return (eye[:, None, :, None] * w[:, :, None, :]).reshape(h * i, h * o)


def kernel(x, c, ctx, c_ctx, w_mod, b_mod, g_norm_mix, g_norm_ffn, w_in, w_fourier, conv_w, conv_b,
           rg_w_a, rg_b_a, rg_w_x, rg_b_x, rg_lam, g_out_fourier, g_out_rg, w_out, w_router, b_router,
           w_gate_up, b_gate_up, w_down, b_down, g_final):
    assert w_mod.shape[0] == 1, "single-layer stack only"
    b, s, d = x.shape
    df = w_fourier.shape[1] * w_fourier.shape[2]
    dr = conv_w.shape[2]
    gd = w_fourier.shape[2]
    rows = s // GRID_W
    t = b * s
    ne = w_router.shape[2]

    mrows = -(-(b + 1) // SUBLANES) * SUBLANES
    cond = jnp.zeros((mrows, d), F32).at[:b].set(c).at[b].set(c_ctx)
    mod = _adaln(cond, w_mod[0], b_mod[0])
    sh_m, sc_m, gt_m, sh_f, sc_f, gt_f = [mod[:b, k * d:(k + 1) * d].reshape(b, 1, d) for k in range(N_MOD)]
    csh_m = mod[b:b + 1, 0:d].reshape(1, 1, d)
    csc_m = mod[b:b + 1, d:2 * d].reshape(1, 1, d)

    tm = min(s, TOKEN_TILE)
    d2, etab, cmat, smat = _dft_tables(rows, gd)
    jmat = jnp.asarray(np.eye(GRID_W)[::-1].copy(), BF16)

    w_in_bf = w_in[0].astype(BF16)
    f, xs, gg = _stage_b(x, sh_m, sc_m, g_norm_mix[0], w_in_bf, jmat, df, dr, tm=tm)
    xr_ctx = _stage_b_ctx(ctx, csh_m, csc_m, g_norm_mix[0], w_in_bf[:, df:df + dr])

    wcat = jnp.stack([jnp.concatenate([_block_diag(rg_w_a[0, dd]), _block_diag(rg_w_x[0, dd])], axis=1)
                      for dd in range(2)]).astype(BF16)
    bcat = jnp.concatenate([rg_b_a[0], rg_b_x[0]], axis=1).reshape(2, 1, 2 * dr)
    lam = rg_lam[0].reshape(2, 1, dr)
    h0 = jnp.zeros((b, SUBLANES, dr), F32)
    _, hfin_ctx = _rg_scan(xr_ctx, h0, conv_w[0], conv_b[0], wcat, bcat, lam, tc=ctx.shape[1])
    hs, _ = _rg_scan(xs, hfin_ctx, conv_w[0], conv_b[0], wcat, bcat, lam, tc=min(s, SCAN_CHUNK))

    cw, sw = _fold_fourier(cmat, smat, w_fourier[0])
    wcs = jnp.concatenate([cw, sw], axis=1).astype(BF16)
    y = _fourier_stage1(f, d2, tl=min(GRID_W * df, 16384))
    fn = _fourier_stage2(y.reshape(b, 2, rows, GRID_W, df), etab, wcs, g_out_fourier[0],
                         kb=min(rows, 16))
    fn = fn.reshape(b, s, df)

    x1, h2, idx, gates = _stage_m(fn, hs, gg, x, gt_m, sh_f, sc_f, g_out_rg[0], g_norm_ffn[0],
                                  w_out[0].astype(BF16), w_router[0].T, b_router[0], jmat, tm=tm)

    tl = min(t, RANK_TILE)
    tri = jnp.asarray(np.triu(np.ones((tl, tl))), BF16)
    rank, cnt = _dispatch_ranks(idx, tri, min(t, RANK_STEP))
    counts = cnt[:, 0]
    tmm = MOE_ROW_TILE
    padded = (counts + tmm - 1) // tmm * tmm
    pad_end = jnp.cumsum(padded)
    pad_start = pad_end - padded
    eids = jnp.arange(ne, dtype=jnp.int32)
    dest = rank + jnp.sum(jnp.where(idx[:, :, None] == eids, pad_start, 0), axis=-1)
    n_blocks = -(-(t * TOP_K) // tmm) + ne
    cap = n_blocks * tmm
    n_used = (pad_end[-1] // tmm).astype(jnp.int32).reshape(1)
    blk_start = jnp.arange(n_blocks, dtype=jnp.int32) * tmm
    blk_expert = jnp.sum(blk_start[:, None] >= pad_end[None, :], axis=1).astype(jnp.int32)
    last_expert = jnp.sum(pad_end[-1] - tmm >= pad_end).astype(jnp.int32)
    blk_expert = jnp.minimum(blk_expert, last_expert)
    sel = blk_expert[:, None] == eids
    blk_first = jnp.sum(jnp.where(sel, pad_start, 0), axis=1)
    blk_count = jnp.sum(jnp.where(sel, counts, 0), axis=1)
    blk_valid = jnp.clip(blk_count - (blk_start - blk_first), 0, tmm).astype(jnp.int32)

    na = TOP_K * t
    ids = _sc_scatter_ids(dest.reshape(-1), cap)

    def row_maps(blocks, rows):
        inv = ids[rows, 0].reshape(-1, tmm)
        live = jnp.arange(tmm, dtype=jnp.int32)[None, :] < blk_valid[blocks, None]
        spread = jnp.arange(rows.start, rows.stop, dtype=jnp.int32).reshape(-1, tmm) % t
        src_tok = jnp.where(live, inv % t, spread).reshape(-1)
        dst_row = jnp.where(live, inv, na + spread).reshape(-1)
        return src_tok, dst_row

    h2_rows = h2.reshape(t, d // 2)
    y_all = jax.empty_ref(jax.ShapeDtypeStruct(((TOP_K + 1) * t, d // 2), jnp.int32))
    unit = n_blocks // sum(MOE_PIPE)
    assert unit * sum(MOE_PIPE) == n_blocks
    blk0 = 0
    for parts in MOE_PIPE:
        nq = parts * unit
        blocks = slice(blk0, blk0 + nq)
        rows = slice(blk0 * tmm, (blk0 + nq) * tmm)
        src_tok, dst_row = row_maps(blocks, rows)
        x_q = _sc_gather(h2_rows, src_tok)
        nu_q = jnp.clip(n_used - blk0, 0, nq).astype(jnp.int32)
        blk0 += nq
        y_q = _moe_experts(blk_expert[blocks], blk_valid[blocks], nu_q, x_q,
                           w_gate_up[0], b_gate_up[0], w_down[0], b_down[0], tmm)
        _sc_scatter_into(y_all, y_q, dst_row)
    yk = y_all[...].reshape(TOP_K + 1, b, s, d // 2)
    return _combine(x1, yk, gates.T.reshape(b, s, TOP_K), gt_f, g_final, tm)
```

```python
import functools

import numpy as np
import jax
import jax.numpy as jnp
from jax import lax
from jax.experimental import pallas as pl
from jax.experimental.pallas import tpu as pltpu
from jax.experimental.pallas import tpu_sc as plsc

GRID_W = 64
CONV_W = 4
CONV_PAD_LO = 2
RG_C = 8.0
N_EXPERTS = 32
TOP_K = 4
SWIGLU_LIMIT = 7.0
SWIGLU_ALPHA = 1.702
N_MOD = 6
EPS = 1e-6

LANES = 128
SUBLANES = 8
VMEM_LIMIT_BYTES = 56 * 1024 * 1024
TOKEN_TILE = 1024
RANK_TILE = 512
RANK_STEP = 4096
SCAN_CHUNK = 512
MOE_ROW_TILE = 512
MOE_FF_CHUNK = 512
MOE_PIPE = (3, 11, 7, 3)

F32 = jnp.float32
BF16 = jnp.bfloat16


def _cparams(sem):
    return pltpu.CompilerParams(dimension_semantics=sem, vmem_limit_bytes=VMEM_LIMIT_BYTES)


def _split_bf16(a):
    hi = a.astype(BF16)
    lo = (a - hi.astype(F32)).astype(BF16)
    return hi, lo


def _dot3(a, b):
    ah, al = _split_bf16(a)
    bh, bl = _split_bf16(b)
    out = jnp.dot(ah, bh, preferred_element_type=F32)
    out += jnp.dot(ah, bl, preferred_element_type=F32)
    out += jnp.dot(al, bh, preferred_element_type=F32)
    return out


def _dot3_nt(a, b):
    dn = (((1,), (1,)), ((), ()))
    m = a.shape[0]
    ah, al = _split_bf16(a)
    bh, bl = _split_bf16(b)
    both = lax.dot_general(jnp.concatenate([ah, al], axis=0), bh, dn, preferred_element_type=F32)
    return both[:m] + both[m:] + lax.dot_general(ah, bl, dn, preferred_element_type=F32)


def _gelu_tanh(x):
    return 0.5 * x * (1.0 + jnp.tanh(0.7978845608028654 * (x + 0.044715 * (x * x * x))))


def _rms(x, g):
    return x * lax.rsqrt(jnp.mean(x * x, axis=-1, keepdims=True) + EPS) * g


def _pack_halves(v):
    h = v.shape[1] // 2
    hi = lax.bitcast_convert_type(v[:, :h].astype(BF16).astype(F32), jnp.uint32)
    lo = lax.bitcast_convert_type(v[:, h:].astype(BF16).astype(F32), jnp.uint32)
    return lax.bitcast_convert_type(hi | (lo >> 16), jnp.int32)


def _unpack_halves(w):
    u = lax.bitcast_convert_type(w, jnp.uint32)
    hi = lax.bitcast_convert_type(u & jnp.uint32(0xFFFF0000), F32)
    lo = lax.bitcast_convert_type(u << 16, F32)
    return hi, lo


def _adaln_kernel(c_ref, w_ref, b_ref, o_ref):
    s = c_ref[...]
    s = s * jax.nn.sigmoid(s)
    o_ref[...] = _dot3(s, w_ref[...]) + b_ref[...]


def _adaln(cond, w_mod, b_mod):
    m, d = cond.shape
    n = w_mod.shape[1]
    tn = n // N_MOD
    return pl.pallas_call(
        _adaln_kernel,
        grid=(n // tn,),
        in_specs=[pl.BlockSpec((m, d), lambda i: (0, 0)),
                  pl.BlockSpec((d, tn), lambda i: (0, i)),
                  pl.BlockSpec((1, tn), lambda i: (0, i))],
        out_specs=pl.BlockSpec((m, tn), lambda i: (0, i)),
        out_shape=jax.ShapeDtypeStruct((m, n), F32),
        compiler_params=_cparams(("arbitrary",)),
        name="adaln",
    )(cond, w_mod, b_mod.reshape(1, n))


def _fold_kernel(c_ref, s_ref, w_ref, cw_ref, sw_ref):
    w = w_ref[0]
    cw_ref[0] = _dot3(c_ref[...], w)
    sw_ref[0] = _dot3(s_ref[...], w)


def _fold_fourier(cmat, smat, w_f):
    g, gd, _ = w_f.shape
    spec_m = pl.BlockSpec((gd, gd), lambda i: (0, 0))
    spec_w = pl.BlockSpec((1, gd, gd), lambda i: (i, 0, 0))
    return pl.pallas_call(
        _fold_kernel,
        grid=(g,),
        in_specs=[spec_m, spec_m, spec_w],
        out_specs=[spec_w, spec_w],
        out_shape=[jax.ShapeDtypeStruct((g, gd, gd), F32)] * 2,
        compiler_params=_cparams(("arbitrary",)),
        name="fold_fourier",
    )(cmat, smat, w_f)


def _seg_pitch(seg_len):
    n8 = seg_len // SUBLANES
    return SUBLANES * (n8 + 1 - n8 % 2)


def _pitched_store(v, buf, blk0):
    pitch = _seg_pitch(GRID_W)
    for r in range(v.shape[0] // GRID_W):
        for c in range(v.shape[1] // LANES):
            buf[c, (blk0 + r) * pitch:(blk0 + r) * pitch + GRID_W, :] = (
                v[r * GRID_W:(r + 1) * GRID_W, c * LANES:(c + 1) * LANES])


def _pitched_gather(buf, nb, store):
    pitch = _seg_pitch(GRID_W)
    for pos in range(GRID_W):
        for c in range(buf.shape[0]):
            store(pos, c, buf[c, pl.ds(pos, nb, stride=pitch), :])


def _stage_b_kernel(x_ref, sh_ref, sc_ref, g_ref, w_ref, j_ref, f_ref, xs_ref, gg_ref, fbuf, *, df, dr):
    tm = x_ref.shape[1]
    h = _rms(x_ref[0], g_ref[...] * (1.0 + sc_ref[0])) + sh_ref[0]
    hb = h.astype(BF16)
    gr = jnp.dot(hb, w_ref[:, df + dr:], preferred_element_type=F32)
    gg_ref[0] = _gelu_tanh(gr).astype(BF16)
    _pitched_store(jnp.dot(hb, w_ref[:, :df], preferred_element_type=F32), fbuf, 0)

    def store_f(pos, c, tile):
        f_ref[0, :, pos * df + c * LANES:pos * df + (c + 1) * LANES] = tile.astype(BF16)
    _pitched_gather(fbuf, tm // GRID_W, store_f)
    xr = jnp.dot(hb, w_ref[:, df:df + dr], preferred_element_type=F32).astype(BF16)
    for r in range(tm // GRID_W):
        blk = xr[r * GRID_W:(r + 1) * GRID_W]
        if r % 2 == 1:
            blk = jnp.dot(j_ref[...], blk, preferred_element_type=F32).astype(BF16)
        xs_ref[0, r * GRID_W:(r + 1) * GRID_W, :] = blk


def _stage_b(x, shift, scale, g, w_in_bf, jmat, df, dr, tm):
    b, s, d = x.shape
    n = w_in_bf.shape[1]
    vec = pl.BlockSpec((1, 1, d), lambda i, t: (i, 0, 0))
    out = pl.BlockSpec((1, tm, df), lambda i, t: (i, t, 0))
    nb = tm // GRID_W
    tok = jax.ShapeDtypeStruct((b, s, df), BF16)
    return pl.pallas_call(
        functools.partial(_stage_b_kernel, df=df, dr=dr),
        grid=(b, s // tm),
        in_specs=[pl.BlockSpec((1, tm, d), lambda i, t: (i, t, 0)), vec, vec,
                  pl.BlockSpec((1, d), lambda i, t: (0, 0)),
                  pl.BlockSpec((d, n), lambda i, t: (0, 0)),
                  pl.BlockSpec((GRID_W, GRID_W), lambda i, t: (0, 0))],
        out_specs=[pl.BlockSpec((1, nb, GRID_W * df), lambda i, t: (i, t, 0)), out, out],
        out_shape=[jax.ShapeDtypeStruct((b, s // GRID_W, GRID_W * df), BF16), tok, tok],
        scratch_shapes=[pltpu.VMEM((df // LANES, nb * _seg_pitch(GRID_W), LANES), F32)],
        compiler_params=_cparams(("parallel", "arbitrary")),
        name="stage_b",
    )(x, shift, scale, g.reshape(1, d), w_in_bf, jmat)


def _stage_b_ctx_kernel(x_ref, sh_ref, sc_ref, g_ref, w_ref, xr_ref):
    h = _rms(x_ref[0], g_ref[...] * (1.0 + sc_ref[0])) + sh_ref[0]
    xr_ref[0] = jnp.dot(h.astype(BF16), w_ref[...], preferred_element_type=F32).astype(BF16)


def _stage_b_ctx(ctx, shift, scale, g, w_xr_bf):
    b, s, d = ctx.shape
    dr = w_xr_bf.shape[1]
    vec = pl.BlockSpec((1, 1, d), lambda i: (0, 0, 0))
    return pl.pallas_call(
        _stage_b_ctx_kernel,
        grid=(b,),
        in_specs=[pl.BlockSpec((1, s, d), lambda i: (i, 0, 0)), vec, vec,
                  pl.BlockSpec((1, d), lambda i: (0, 0)),
                  pl.BlockSpec((d, dr), lambda i: (0, 0))],
        out_specs=pl.BlockSpec((1, s, dr), lambda i: (i, 0, 0)),
        out_shape=jax.ShapeDtypeStruct((b, s, dr), BF16),
        compiler_params=_cparams(("arbitrary",)),
        name="stage_b_ctx",
    )(ctx, shift, scale, g.reshape(1, d), w_xr_bf)


HALO = 16


def _sigmoid(x):
    return 0.5 * jnp.tanh(0.5 * x) + 0.5


def _rg_kernel(xs_ref, h0_ref, cw_ref, cb_ref, w_ref, b_ref, lam_ref, pm_ref, pmt_ref, out_ref, hfin_ref,
               hf_s, xc_s, a_s, u_s, hl_s, p_s, c_s, hc_s, *, tc, nchunk, seq, dr):
    p = pl.program_id(1)
    j = pl.program_id(2)
    cidx = jnp.where(p == 0, j, nchunk - 1 - j)
    start = pl.multiple_of(cidx * tc, tc)
    nseg = SUBLANES
    sl = tc // nseg
    sub = lax.broadcasted_iota(jnp.int32, (nseg, dr), 0)

    @pl.when(p == 0)
    def _():
        xp = jnp.dot(pm_ref[...], xs_ref[0, pl.ds(start, tc), :], preferred_element_type=F32)
        pstart = pl.multiple_of(jnp.maximum(start - HALO, 0), HALO)
        nstart = pl.multiple_of(jnp.minimum(start + tc, seq - HALO), HALO)
        prev = xs_ref[0, pl.ds(pstart, HALO), :].astype(F32)
        nxt = xs_ref[0, pl.ds(nstart, HALO), :].astype(F32)
        prev = jnp.where(cidx > 0, prev, 0.0)
        nxt = jnp.where(cidx < nchunk - 1, nxt, 0.0)
        tm2 = jnp.where(sub == 0, prev[HALO - 2:HALO - 1], pltpu.roll(xp[(sl - 2) * nseg:(sl - 1) * nseg], 1, 0))
        tm1 = jnp.where(sub == 0, prev[HALO - 1:HALO], pltpu.roll(xp[(sl - 1) * nseg:sl * nseg], 1, 0))
        tp1 = jnp.where(sub == nseg - 1, nxt[0:1], pltpu.roll(xp[0:nseg], nseg - 1, 0))
        ext = jnp.concatenate([tm2, tm1, xp, tp1], axis=0)
        xc = cb_ref[...] + cw_ref[0:1, :] * ext[0:tc]
        for k in range(1, CONV_W):
            xc = xc + cw_ref[k:k + 1, :] * ext[k * nseg:k * nseg + tc]
        xc_s[pl.ds(start, tc), :] = xc

    xc = xc_s[pl.ds(start, tc), :]
    gates = jnp.dot(xc.astype(BF16), w_ref[p], preferred_element_type=F32) + b_ref[p]
    i = _sigmoid(gates[:, dr:])
    half_c = (-0.5 * RG_C) * jax.nn.softplus(-lam_ref[p])
    log_a = half_c * jnp.tanh(0.5 * gates[:, :dr]) + half_c
    a = jnp.exp(log_a)
    a_s[...] = a
    w = -jnp.tanh(log_a) * (1.0 + a * a)
    u_s[...] = jnp.where(w > 0.0, w * lax.rsqrt(w), 0.0) * (i * xc)

    @pl.when(jnp.logical_and(p == 0, j == 0))
    def _():
        hfin_ref[...] = jnp.zeros_like(hfin_ref)

    @pl.when(j == 0)
    def _():
        hc_s[0:1, :] = h0_ref[0, pl.ds(p, 1), :]

    def segment_scan(reverse):
        def body(q, carry):
            t = (sl - 1 - q) if reverse else q
            rows = pl.ds(pl.multiple_of(t * nseg, nseg), nseg)
            h, pr = carry
            av = a_s[rows, :]
            h = av * h + u_s[rows, :]
            pr = av * pr
            hl_s[rows, :] = h
            p_s[rows, :] = pr
            return h, pr
        h_end, p_end = lax.fori_loop(0, sl, body, (jnp.zeros((nseg, dr), F32), jnp.ones((nseg, dr), F32)),
                                     unroll=4)
        carry = hc_s[0:1, :]
        for g in (range(nseg - 1, -1, -1) if reverse else range(nseg)):
            c_s[g:g + 1, :] = carry
            carry = h_end[g:g + 1, :] + p_end[g:g + 1, :] * carry
        hc_s[0:1, :] = carry
        return c_s[...]

    def corrected(cin):
        h = hl_s[...].reshape(sl, nseg, dr) + p_s[...].reshape(sl, nseg, dr) * cin[None]
        return h.reshape(tc, dr)

    @pl.when(p == 0)
    def _():
        hf_s[pl.ds(start, tc), :] = corrected(segment_scan(False))

    @pl.when(p == 1)
    def _():
        tot = corrected(segment_scan(True)) + hf_s[pl.ds(start, tc), :]
        out_ref[0] = jnp.dot(pmt_ref[...], tot.astype(BF16), preferred_element_type=F32).astype(BF16)

    @pl.when(j == nchunk - 1)
    def _():
        hfin_ref[0, pl.ds(p, 1), :] = hc_s[0:1, :]


def _rg_scan(xs, h0, conv_w, conv_b, wcat, bcat, lam, tc):
    b, s, dr = xs.shape
    nchunk = s // tc
    last = nchunk - 1
    sl = tc // SUBLANES
    src = (np.arange(tc) % SUBLANES) * sl + np.arange(tc) // SUBLANES
    pm = np.zeros((tc, tc), np.float32)
    pm[np.arange(tc), src] = 1.0
    chunk_buf = pltpu.VMEM((tc, dr), F32)
    full2 = lambda shape: pl.BlockSpec(shape, lambda i, p, j: (0,) * len(shape))
    return pl.pallas_call(
        functools.partial(_rg_kernel, tc=tc, nchunk=nchunk, seq=s, dr=dr),
        grid=(b, 2, nchunk),
        in_specs=[pl.BlockSpec((1, s, dr), lambda i, p, j: (i, 0, 0)),
                  pl.BlockSpec((1, SUBLANES, dr), lambda i, p, j: (i, 0, 0)),
                  full2((CONV_W, dr)), full2((1, dr)),
                  full2((2, dr, 2 * dr)), full2((2, 1, 2 * dr)), full2((2, 1, dr)),
                  full2((tc, tc)), full2((tc, tc))],
        out_specs=[pl.BlockSpec((1, tc, dr), lambda i, p, j: (i, jnp.where(p == 0, last, last - j), 0)),
                   pl.BlockSpec((1, SUBLANES, dr), lambda i, p, j: (i, 0, 0))],
        out_shape=[jax.ShapeDtypeStruct((b, s, dr), BF16),
                   jax.ShapeDtypeStruct((b, SUBLANES, dr), F32)],
        scratch_shapes=[pltpu.VMEM((s, dr), F32), pltpu.VMEM((s, dr), F32),
                        chunk_buf, chunk_buf, chunk_buf, chunk_buf,
                        pltpu.VMEM((SUBLANES, dr), F32), pltpu.VMEM((SUBLANES, dr), F32)],
        compiler_params=_cparams(("arbitrary", "arbitrary", "arbitrary")),
        name="rg_scan",
    )(xs, h0, conv_w, conv_b.reshape(1, dr), wcat, bcat, lam, jnp.asarray(pm, BF16), jnp.asarray(pm.T, BF16))


def _f1_kernel(d_ref, x_ref, y_ref):
    y_ref[0] = jnp.dot(d_ref[...], x_ref[0], preferred_element_type=F32).astype(BF16)


def _fourier_stage1(fv, d2, tl):
    b, r, n = fv.shape
    return pl.pallas_call(
        _f1_kernel,
        grid=(b, n // tl),
        in_specs=[pl.BlockSpec((2 * r, r), lambda i, l: (0, 0)),
                  pl.BlockSpec((1, r, tl), lambda i, l: (i, 0, l))],
        out_specs=pl.BlockSpec((1, 2 * r, tl), lambda i, l: (i, 0, l)),
        out_shape=jax.ShapeDtypeStruct((b, 2 * r, n), BF16),
        compiler_params=_cparams(("parallel", "arbitrary")),
        name="fourier_stage1",
    )(d2, fv)


def _f2_kernel(y_ref, e_ref, wcs_ref, g_ref, o_ref, obuf, *, kb, df):
    zr, zi = [], []
    for q in range(kb):
        yk = jnp.concatenate([y_ref[0, 0, q], y_ref[0, 1, q]], axis=0)
        z = jnp.dot(e_ref[q], yk, preferred_element_type=F32)
        zr.append(z[:GRID_W])
        zi.append(z[GRID_W:])
    zr = jnp.concatenate(zr, axis=0).astype(BF16)
    zi = jnp.concatenate(zi, axis=0).astype(BF16)
    gd = wcs_ref.shape[2]
    o = jnp.concatenate(
        [jnp.dot(jnp.concatenate([zr[:, g * gd:(g + 1) * gd], zi[:, g * gd:(g + 1) * gd]], axis=1), wcs_ref[g],
                 preferred_element_type=F32) for g in range(df // gd)], axis=1)
    on = _rms(o, g_ref[...])

    def store_o(pos, c, tile):
        o_ref[0, pos, :, c * LANES:(c + 1) * LANES] = tile.astype(BF16)
    _pitched_store(on, obuf, 0)
    _pitched_gather(obuf, kb, store_o)


def _fourier_stage2(y5, etab, wcs, g, kb):
    b, _, r, w, df = y5.shape
    return pl.pallas_call(
        functools.partial(_f2_kernel, kb=kb, df=df),
        grid=(b, r // kb),
        in_specs=[pl.BlockSpec((1, 2, kb, w, df), lambda i, k: (i, 0, k, 0, 0)),
                  pl.BlockSpec((kb, 2 * w, 2 * w), lambda i, k: (k, 0, 0)),
                  pl.BlockSpec(wcs.shape, lambda i, k: (0, 0, 0)),
                  pl.BlockSpec((1, df), lambda i, k: (0, 0))],
        out_specs=pl.BlockSpec((1, w, kb, df), lambda i, k: (i, 0, k, 0)),
        out_shape=jax.ShapeDtypeStruct((b, w, r, df), BF16),
        scratch_shapes=[pltpu.VMEM((df // LANES, kb * _seg_pitch(GRID_W), LANES), F32)],
        compiler_params=_cparams(("parallel", "arbitrary")),
        name="fourier_stage2",
    )(y5, etab, wcs, g.reshape(1, df))


def _stage_m_kernel(fn_ref, hs_ref, gg_ref, x_ref, gtm_ref, shf_ref, scf_ref, gr_ref, gffn_ref,
                    wo_ref, wr_ref, br_ref, j_ref, x1_ref, h2_ref, idx_ref, gate_ref, *, df):
    tm = x_ref.shape[1]
    hs = hs_ref[0]
    blocks = []
    for r in range(tm // GRID_W):
        blk = hs[r * GRID_W:(r + 1) * GRID_W]
        if r % 2 == 1:
            blk = jnp.dot(j_ref[...], blk, preferred_element_type=F32)
        blocks.append(blk.astype(F32))
    rg = jnp.concatenate(blocks, axis=0) * gg_ref[0].astype(F32)
    rgn = _rms(rg, gr_ref[...]).astype(BF16)
    mix = jnp.dot(fn_ref[0], wo_ref[:df, :], preferred_element_type=F32)
    mix += jnp.dot(rgn, wo_ref[df:, :], preferred_element_type=F32)
    x1 = x_ref[0] + gtm_ref[0] * mix
    x1_ref[0] = x1
    h2 = _rms(x1, gffn_ref[...] * (1.0 + scf_ref[0])) + shf_ref[0]
    h2_ref[0] = _pack_halves(h2)

    logits = _dot3_nt(wr_ref[...], h2) + br_ref[...]
    eidx = lax.broadcasted_iota(jnp.int32, logits.shape, 0)
    vals, idxs = [], []
    for _ in range(TOP_K):
        m = jnp.max(logits, axis=0, keepdims=True)
        sel = jnp.min(jnp.where(logits == m, eidx, N_EXPERTS), axis=0, keepdims=True)
        vals.append(m)
        idxs.append(sel)
        logits = jnp.where(eidx == sel, -jnp.inf, logits)
    ex = [jnp.exp(v - vals[0]) for v in vals]
    den = ex[0] + ex[1] + ex[2] + ex[3]
    for k in range(TOP_K):
        gate_ref[k:k + 1, :] = ex[k] / den
        idx_ref[k:k + 1, :] = idxs[k]


def _stage_m(fn, hs, gg, x, gt_m, sh_f, sc_f, g_out_r, g_ffn, w_out_bf, w_router_t, b_router, jmat, tm):
    b, s, d = x.shape
    df = fn.shape[2]
    dr = hs.shape[2]
    nt = s // tm
    ne = w_router_t.shape[0]
    vec = pl.BlockSpec((1, 1, d), lambda i, t: (i, 0, 0))
    half = lambda dd: pl.BlockSpec((1, tm, dd), lambda i, t: (i, t, 0))
    full = lambda shape: pl.BlockSpec(shape, lambda i, t: (0,) * len(shape))
    tok = pl.BlockSpec((TOP_K, tm), lambda i, t: (0, i * nt + t))
    return pl.pallas_call(
        functools.partial(_stage_m_kernel, df=df),
        grid=(b, nt),
        in_specs=[half(df), half(dr), half(dr), half(d), vec, vec, vec,
                  full((1, dr)), full((1, d)), full((d, d)), full((ne, d)), full((ne, 1)),
                  full((GRID_W, GRID_W))],
        out_specs=[half(d), half(d // 2), tok, tok],
        out_shape=[jax.ShapeDtypeStruct((b, s, d), F32), jax.ShapeDtypeStruct((b, s, d // 2), jnp.int32),
                   jax.ShapeDtypeStruct((TOP_K, b * s), jnp.int32),
                   jax.ShapeDtypeStruct((TOP_K, b * s), F32)],
        compiler_params=_cparams(("parallel", "arbitrary")),
        name="stage_m",
    )(fn, hs, gg, x, gt_m, sh_f, sc_f, g_out_r.reshape(1, dr), g_ffn.reshape(1, d), w_out_bf,
      w_router_t, b_router.reshape(ne, 1), jmat)


def _rank_kernel(idx_ref, tri_ref, rank_ref, cnt_ref, carry_s):
    c = pl.program_id(0)

    @pl.when(c == 0)
    def _():
        carry_s[...] = jnp.zeros_like(carry_s)

    l = tri_ref.shape[0]
    eidx = lax.broadcasted_iota(jnp.int32, (N_EXPERTS, l), 0)
    for sub in range(idx_ref.shape[1] // l):
        lanes = slice(sub * l, (sub + 1) * l)
        onehots = [eidx == idx_ref[k:k + 1, lanes] for k in range(TOP_K)]
        ohs = [jnp.where(o, 1.0, 0.0) for o in onehots]
        prefix = jnp.dot(jnp.concatenate(ohs, axis=0).astype(BF16), tri_ref[...], preferred_element_type=F32)
        for k in range(TOP_K):
            carry = carry_s[:, 0:1]
            pk = prefix[k * N_EXPERTS:(k + 1) * N_EXPERTS]
            rank = jnp.sum(jnp.where(onehots[k], pk - 1.0 + carry, 0.0), axis=0, keepdims=True)
            rank_ref[k:k + 1, lanes] = rank.astype(jnp.int32)
            carry_s[...] = carry_s[...] + jnp.sum(ohs[k], axis=1, keepdims=True)
    cnt_ref[...] = carry_s[...].astype(jnp.int32)


def _dispatch_ranks(idx, tri, tl):
    k, t = idx.shape
    return pl.pallas_call(
        _rank_kernel,
        grid=(t // tl,),
        in_specs=[pl.BlockSpec((k, tl), lambda c: (0, c)),
                  pl.BlockSpec(tri.shape, lambda c: (0, 0))],
        out_specs=[pl.BlockSpec((k, tl), lambda c: (0, c)),
                   pl.BlockSpec((N_EXPERTS, LANES), lambda c: (0, 0))],
        out_shape=[jax.ShapeDtypeStruct((k, t), jnp.int32),
                   jax.ShapeDtypeStruct((N_EXPERTS, LANES), jnp.int32)],
        scratch_shapes=[pltpu.VMEM((N_EXPERTS, LANES), F32)],
        compiler_params=_cparams(("arbitrary",)),
        name="dispatch_ranks",
    )(idx, tri)


def _moe_kernel(be_ref, bv_ref, nu_ref, x_ref, wgu_ref, bgu_ref, wd_ref, bd_ref, o_ref, wgu_s, wd_s, *, dff):
    i = pl.program_id(0)
    h = x_ref.shape[1]

    @pl.when(jnp.logical_or(i == 0, be_ref[i] != be_ref[jnp.maximum(i - 1, 0)]))
    def _():
        wgu_s[...] = wgu_ref[0].astype(BF16)
        wd_s[...] = wd_ref[0].astype(BF16)

    @pl.when(i < nu_ref[0])
    def _():
        rows = lax.broadcasted_iota(jnp.int32, x_ref.shape, 0)
        xw = jnp.where(rows < bv_ref[i], x_ref[...], 0)
        xa, xb = _unpack_halves(xw)
        xa = xa.astype(BF16)
        xb = xb.astype(BF16)
        acc = None
        for c in range(dff // MOE_FF_CHUNK):
            gs = slice(c * MOE_FF_CHUNK, (c + 1) * MOE_FF_CHUNK)
            us = slice(dff + c * MOE_FF_CHUNK, dff + (c + 1) * MOE_FF_CHUNK)
            g = jnp.dot(xa, wgu_s[:h, gs], preferred_element_type=F32)
            g += jnp.dot(xb, wgu_s[h:, gs], preferred_element_type=F32)
            u = jnp.dot(xa, wgu_s[:h, us], preferred_element_type=F32)
            u += jnp.dot(xb, wgu_s[h:, us], preferred_element_type=F32)
            gt = jnp.minimum(g + bgu_ref[0, :, gs], SWIGLU_LIMIT)
            up = jnp.clip(u + bgu_ref[0, :, us], -SWIGLU_LIMIT, SWIGLU_LIMIT)
            act = (up + 1.0) * (gt * _sigmoid(SWIGLU_ALPHA * gt))
            part = jnp.dot(act.astype(BF16), wd_s[gs, :], preferred_element_type=F32)
            acc = part if acc is None else acc + part
        o_ref[...] = _pack_halves(acc + bd_ref[0])


def _moe_experts(blk_expert, blk_valid, n_used, xs, wgu, bgu, wd, bd, tmm):
    cap, h = xs.shape
    ne, d, dff2 = wgu.shape
    dff = dff2 // 2
    row_blk = lambda i, be, bv, nu: (jnp.maximum(jnp.minimum(i, nu[0] - 1), 0), 0)
    wsel = lambda i, be, bv, nu: (be[i], 0, 0)
    grid_spec = pltpu.PrefetchScalarGridSpec(
        num_scalar_prefetch=3,
        grid=(cap // tmm,),
        in_specs=[pl.BlockSpec((tmm, h), row_blk),
                  pl.BlockSpec((1, d, dff2), wsel),
                  pl.BlockSpec((1, 1, dff2), wsel),
                  pl.BlockSpec((1, dff, d), wsel),
                  pl.BlockSpec((1, 1, d), wsel)],
        out_specs=pl.BlockSpec((tmm, h), row_blk),
        scratch_shapes=[pltpu.VMEM((d, dff2), BF16), pltpu.VMEM((dff, d), BF16)],
    )
    return pl.pallas_call(
        functools.partial(_moe_kernel, dff=dff),
        grid_spec=grid_spec,
        out_shape=jax.ShapeDtypeStruct((cap, h), jnp.int32),
        compiler_params=_cparams(("arbitrary",)),
        name="moe_experts",
    )(blk_expert, blk_valid, n_used, xs, wgu, bgu.reshape(ne, 1, dff2), wd, bd.reshape(ne, 1, d))


SC_CHUNK = 64
SC_ID_CHUNK = 128


def _sc_workers():
    info = plsc.get_sparse_core_info()
    return info.num_cores, info.num_subcores


def _sc_scatter_into(out_ref, rows, dest):
    n, w = rows.shape
    nc, ns = _sc_workers()
    per_w = n // (nc * ns)
    assert per_w % SC_CHUNK == 0
    mesh = plsc.VectorSubcoreMesh(core_axis_name="c", subcore_axis_name="s")

    @functools.partial(
        pl.kernel, mesh=mesh, out_type=(),
        scratch_types=[pltpu.VMEM((SC_CHUNK,), jnp.int32),
                       pltpu.VMEM((SC_CHUNK, w), jnp.int32),
                       pltpu.SemaphoreType.DMA],
    )
    def scatter_rows(rows_hbm, dest_hbm, out_hbm, idx_v, rows_v, sem):
        base = (lax.axis_index("s") * nc + lax.axis_index("c")) * per_w

        @pl.loop(0, per_w // SC_CHUNK)
        def _(j):
            off = pl.multiple_of(base + j * SC_CHUNK, SC_CHUNK)
            pltpu.sync_copy(rows_hbm.at[pl.ds(off, SC_CHUNK)], rows_v)
            pltpu.sync_copy(dest_hbm.at[pl.ds(off, SC_CHUNK)], idx_v)
            pltpu.async_copy(rows_v, out_hbm.at[idx_v], sem).wait()

    scatter_rows(rows, dest, out_ref)


def _sc_scatter_ids(dest, out_rows):
    n = dest.shape[0]
    info = plsc.get_sparse_core_info()
    nc, ns, nl = info.num_cores, info.num_subcores, info.num_lanes
    per_w = n // (nc * ns)
    chunk = SC_ID_CHUNK
    assert per_w % (2 * chunk) == 0
    mesh = plsc.VectorSubcoreMesh(core_axis_name="c", subcore_axis_name="s")
    idx_t = pltpu.VMEM((chunk,), jnp.int32)
    rows_t = pltpu.VMEM((chunk, LANES), jnp.int32)

    @functools.partial(
        pl.kernel, mesh=mesh,
        out_type=jax.ShapeDtypeStruct((out_rows, LANES), jnp.int32),
        scratch_types=[idx_t, idx_t, rows_t, rows_t, pltpu.SemaphoreType.DMA, pltpu.SemaphoreType.DMA],
    )
    def scatter_ids(dest_hbm, out_hbm, idx_a, idx_b, rows_a, rows_b, sem_a, sem_b):
        base = (lax.axis_index("s") * nc + lax.axis_index("c")) * per_w

        def start(off, idx_v, rows_v, sem):
            for r in range(chunk):
                rows_v[r, pl.ds(0, nl)] = jnp.zeros((nl,), jnp.int32) + (off + r)
            pltpu.sync_copy(dest_hbm.at[pl.ds(off, chunk)], idx_v)
            return pltpu.async_copy(rows_v, out_hbm.at[idx_v], sem)

        @pl.loop(0, per_w // (2 * chunk))
        def _(j):
            off = pl.multiple_of(base + j * (2 * chunk), 2 * chunk)
            copy_a = start(off, idx_a, rows_a, sem_a)
            copy_b = start(off + chunk, idx_b, rows_b, sem_b)
            copy_a.wait()
            copy_b.wait()

    return scatter_ids(dest)


def _sc_gather(table, idx):
    n = idx.shape[0]
    w = table.shape[1]
    nc, ns = _sc_workers()
    per_w = n // (nc * ns)
    assert per_w % SC_CHUNK == 0
    mesh = plsc.VectorSubcoreMesh(core_axis_name="c", subcore_axis_name="s")

    @functools.partial(
        pl.kernel, mesh=mesh,
        out_type=jax.ShapeDtypeStruct((n, w), jnp.int32),
        scratch_types=[pltpu.VMEM((SC_CHUNK,), jnp.int32),
                       pltpu.VMEM((SC_CHUNK, w), jnp.int32),
                       pltpu.SemaphoreType.DMA],
    )
    def gather_rows(table_hbm, idx_hbm, out_hbm, idx_v, rows_v, sem):
        base = (lax.axis_index("s") * nc + lax.axis_index("c")) * per_w

        @pl.loop(0, per_w // SC_CHUNK)
        def _(j):
            off = pl.multiple_of(base + j * SC_CHUNK, SC_CHUNK)
            pltpu.sync_copy(idx_hbm.at[pl.ds(off, SC_CHUNK)], idx_v)
            pltpu.async_copy(table_hbm.at[idx_v], rows_v, sem).wait()
            pltpu.sync_copy(rows_v, out_hbm.at[pl.ds(off, SC_CHUNK)])

    return gather_rows(table, idx)


def _combine_kernel(x1_ref, y_ref, gate_ref, gtf_ref, g_ref, o_ref):
    h = y_ref.shape[3]
    gates = gate_ref[0]
    moe_a = moe_b = None
    for k in range(TOP_K):
        ya, yb = _unpack_halves(y_ref[k, 0])
        gk = gates[:, k:k + 1]
        moe_a = gk * ya if k == 0 else moe_a + gk * ya
        moe_b = gk * yb if k == 0 else moe_b + gk * yb
    za = x1_ref[0, :, :h] + gtf_ref[0, :, :h] * moe_a
    zb = x1_ref[0, :, h:] + gtf_ref[0, :, h:] * moe_b
    ms = (jnp.sum(za * za, axis=-1, keepdims=True) + jnp.sum(zb * zb, axis=-1, keepdims=True)) / (2 * h)
    inv = lax.rsqrt(ms + EPS)
    o_ref[0, :, :h] = za * inv * g_ref[:, :h]
    o_ref[0, :, h:] = zb * inv * g_ref[:, h:]


def _combine(x1, yk, gates_t, gt_f, g_final, tm):
    b, s, d = x1.shape
    h = yk.shape[3]
    return pl.pallas_call(
        _combine_kernel,
        grid=(b, s // tm),
        in_specs=[pl.BlockSpec((1, tm, d), lambda i, t: (i, t, 0)),
                  pl.BlockSpec((TOP_K, 1, tm, h), lambda i, t: (0, i, t, 0)),
                  pl.BlockSpec((1, tm, TOP_K), lambda i, t: (i, t, 0)),
                  pl.BlockSpec((1, 1, d), lambda i, t: (i, 0, 0)),
                  pl.BlockSpec((1, d), lambda i, t: (0, 0))],
        out_specs=pl.BlockSpec((1, tm, d), lambda i, t: (i, t, 0)),
        out_shape=jax.ShapeDtypeStruct((b, s, d), F32),
        compiler_params=_cparams(("parallel", "arbitrary")),
        name="combine",
    )(x1, yk, gates_t, gt_f, g_final.reshape(1, d))


def _dft_tables(rows, gd):
    seq = rows * GRID_W
    n = np.arange(rows)
    ang1 = 2.0 * np.pi * np.outer(n, n) / rows
    d2 = np.concatenate([np.cos(ang1), -np.sin(ang1)], axis=0)
    k1 = np.arange(rows)[:, None, None]
    k2 = np.arange(GRID_W)[None, :, None]
    n2 = np.arange(GRID_W)[None, None, :]
    ang2 = 2.0 * np.pi * ((n2 * (k1 + rows * k2)) % seq) / seq
    ec, es = np.cos(ang2), np.sin(ang2)
    etab = np.concatenate([np.concatenate([ec, es], axis=2),
                           np.concatenate([-es, ec], axis=2)], axis=1)
    c = np.arange(gd)
    angc = 2.0 * np.pi * np.outer(c, c) / gd
    scale = 1.0 / np.sqrt(seq * gd)
    return (jnp.asarray(d2, BF16), jnp.asarray(etab, BF16),
            jnp.asarray(np.cos(angc) * scale, F32), jnp.asarray(np.sin(angc) * scale, F32))


def _block_diag(w):
    h, i, o = w.shape
    eye = jnp.eye(h, dtype=w.dtype)
    return (eye[:, None, :, None] * w[:, :, None, :]).reshape(h * i, h * o)


def kernel(x, c, ctx, c_ctx, w_mod, b_mod, g_norm_mix, g_norm_ffn, w_in, w_fourier, conv_w, conv_b,
           rg_w_a, rg_b_a, rg_w_x, rg_b_x, rg_lam, g_out_fourier, g_out_rg, w_out, w_router, b_router,
           w_gate_up, b_gate_up, w_down, b_down, g_final):
    assert w_mod.shape[0] == 1, "single-layer stack only"
    b, s, d = x.shape
    df = w_fourier.shape[1] * w_fourier.shape[2]
    dr = conv_w.shape[2]
    gd = w_fourier.shape[2]
    rows = s // GRID_W
    t = b * s
    ne = w_router.shape[2]

    mrows = -(-(b + 1) // SUBLANES) * SUBLANES
    cond = jnp.zeros((mrows, d), F32).at[:b].set(c).at[b].set(c_ctx)
    mod = _adaln(cond, w_mod[0], b_mod[0])
    sh_m, sc_m, gt_m, sh_f, sc_f, gt_f = [mod[:b, k * d:(k + 1) * d].reshape(b, 1, d) for k in range(N_MOD)]
    csh_m = mod[b:b + 1, 0:d].reshape(1, 1, d)
    csc_m = mod[b:b + 1, d:2 * d].reshape(1, 1, d)

    tm = min(s, TOKEN_TILE)
    d2, etab, cmat, smat = _dft_tables(rows, gd)
    jmat = jnp.asarray(np.eye(GRID_W)[::-1].copy(), BF16)

    w_in_bf = w_in[0].astype(BF16)
    f, xs, gg = _stage_b(x, sh_m, sc_m, g_norm_mix[0], w_in_bf, jmat, df, dr, tm=tm)
    xr_ctx = _stage_b_ctx(ctx, csh_m, csc_m, g_norm_mix[0], w_in_bf[:, df:df + dr])

    wcat = jnp.stack([jnp.concatenate([_block_diag(rg_w_a[0, dd]), _block_diag(rg_w_x[0, dd])], axis=1)
                      for dd in range(2)]).astype(BF16)
    bcat = jnp.concatenate([rg_b_a[0], rg_b_x[0]], axis=1).reshape(2, 1, 2 * dr)
    lam = rg_lam[0].reshape(2, 1, dr)
    h0 = jnp.zeros((b, SUBLANES, dr), F32)
    _, hfin_ctx = _rg_scan(xr_ctx, h0, conv_w[0], conv_b[0], wcat, bcat, lam, tc=ctx.shape[1])
    hs, _ = _rg_scan(xs, hfin_ctx, conv_w[0], conv_b[0], wcat, bcat, lam, tc=min(s, SCAN_CHUNK))

    cw, sw = _fold_fourier(cmat, smat, w_fourier[0])
    wcs = jnp.concatenate([cw, sw], axis=1).astype(BF16)
    y = _fourier_stage1(f, d2, tl=min(GRID_W * df, 16384))
    fn = _fourier_stage2(y.reshape(b, 2, rows, GRID_W, df), etab, wcs, g_out_fourier[0],
                         kb=min(rows, 16))
    fn = fn.reshape(b, s, df)

    x1, h2, idx, gates = _stage_m(fn, hs, gg, x, gt_m, sh_f, sc_f, g_out_rg[0], g_norm_ffn[0],
                                  w_out[0].astype(BF16), w_router[0].T, b_router[0], jmat, tm=tm)

    tl = min(t, RANK_TILE)
    tri = jnp.asarray(np.triu(np.ones((tl, tl))), BF16)
    rank, cnt = _dispatch_ranks(idx, tri, min(t, RANK_STEP))
    counts = cnt[:, 0]
    tmm = MOE_ROW_TILE
    padded = (counts + tmm - 1) // tmm * tmm
    pad_end = jnp.cumsum(padded)
    pad_start = pad_end - padded
    eids = jnp.arange(ne, dtype=jnp.int32)
    dest = rank + jnp.sum(jnp.where(idx[:, :, None] == eids, pad_start, 0), axis=-1)
    n_blocks = -(-(t * TOP_K) // tmm) + ne
    cap = n_blocks * tmm
    n_used = (pad_end[-1] // tmm).astype(jnp.int32).reshape(1)
    blk_start = jnp.arange(n_blocks, dtype=jnp.int32) * tmm
    blk_expert = jnp.sum(blk_start[:, None] >= pad_end[None, :], axis=1).astype(jnp.int32)
    last_expert = jnp.sum(pad_end[-1] - tmm >= pad_end).astype(jnp.int32)
    blk_expert = jnp.minimum(blk_expert, last_expert)
    sel = blk_expert[:, None] == eids
    blk_first = jnp.sum(jnp.where(sel, pad_start, 0), axis=1)
    blk_count = jnp.sum(jnp.where(sel, counts, 0), axis=1)
    blk_valid = jnp.clip(blk_count - (blk_start - blk_first), 0, tmm).astype(jnp.int32)

    na = TOP_K * t
    inv = _sc_scatter_ids(dest.reshape(-1), cap)[:, 0].reshape(n_blocks, tmm)
    live = jnp.arange(tmm, dtype=jnp.int32)[None, :] < blk_valid[:, None]
    spread = jnp.arange(cap, dtype=jnp.int32).reshape(n_blocks, tmm) % t
    src_tok = jnp.where(live, inv % t, spread).reshape(-1)
    dst_row = jnp.where(live, inv, na + spread).reshape(-1)

    h2_rows = h2.reshape(t, d // 2)
    y_all = jax.empty_ref(jax.ShapeDtypeStruct(((TOP_K + 1) * t, d // 2), jnp.int32))
    unit = n_blocks // sum(MOE_PIPE)
    assert unit * sum(MOE_PIPE) == n_blocks
    blk0 = 0
    for parts in MOE_PIPE:
        nq = parts * unit
        blocks = slice(blk0, blk0 + nq)
        rows = slice(blk0 * tmm, (blk0 + nq) * tmm)
        x_q = _sc_gather(h2_rows, src_tok[rows])
        nu_q = jnp.clip(n_used - blk0, 0, nq).astype(jnp.int32)
        blk0 += nq
        y_q = _moe_experts(blk_expert[blocks], blk_valid[blocks], nu_q, x_q,
                           w_gate_up[0], b_gate_up[0], w_down[0], b_down[0], tmm)
        _sc_scatter_into(y_all, y_q, dst_row[rows])
    yk = y_all[...].reshape(TOP_K + 1, b, s, d // 2)
    return _combine(x1, yk, gates.T.reshape(b, s, TOP_K), gt_f, g_final, tm)
```

```python
import functools

import numpy as np
import jax
import jax.numpy as jnp
from jax import lax
from jax.experimental import pallas as pl
from jax.experimental.pallas import tpu as pltpu
from jax.experimental.pallas import tpu_sc as plsc

GRID_W = 64
CONV_W = 4
CONV_PAD_LO = 2
RG_C = 8.0
N_EXPERTS = 32
TOP_K = 4
SWIGLU_LIMIT = 7.0
SWIGLU_ALPHA = 1.702
N_MOD = 6
EPS = 1e-6

LANES = 128
SUBLANES = 8
VMEM_LIMIT_BYTES = 56 * 1024 * 1024
TOKEN_TILE = 1024
RANK_TILE = 512
RANK_STEP = 4096
SCAN_CHUNK = 512
MOE_ROW_TILE = 512
MOE_FF_CHUNK = 512
MOE_PIPE = (3, 11, 7, 3)

F32 = jnp.float32
BF16 = jnp.bfloat16


def _cparams(sem):
    return pltpu.CompilerParams(dimension_semantics=sem, vmem_limit_bytes=VMEM_LIMIT_BYTES)


def _split_bf16(a):
    hi = a.astype(BF16)
    lo = (a - hi.astype(F32)).astype(BF16)
    return hi, lo


def _dot3(a, b):
    ah, al = _split_bf16(a)
    bh, bl = _split_bf16(b)
    out = jnp.dot(ah, bh, preferred_element_type=F32)
    out += jnp.dot(ah, bl, preferred_element_type=F32)
    out += jnp.dot(al, bh, preferred_element_type=F32)
    return out


def _dot3_nt(a, b):
    dn = (((1,), (1,)), ((), ()))
    m = a.shape[0]
    ah, al = _split_bf16(a)
    bh, bl = _split_bf16(b)
    both = lax.dot_general(jnp.concatenate([ah, al], axis=0), bh, dn, preferred_element_type=F32)
    return both[:m] + both[m:] + lax.dot_general(ah, bl, dn, preferred_element_type=F32)


def _gelu_tanh(x):
    return 0.5 * x * (1.0 + jnp.tanh(0.7978845608028654 * (x + 0.044715 * (x * x * x))))


def _rms(x, g):
    return x * lax.rsqrt(jnp.mean(x * x, axis=-1, keepdims=True) + EPS) * g


def _pack_halves(v):
    h = v.shape[1] // 2
    hi = lax.bitcast_convert_type(v[:, :h].astype(BF16).astype(F32), jnp.uint32)
    lo = lax.bitcast_convert_type(v[:, h:].astype(BF16).astype(F32), jnp.uint32)
    return lax.bitcast_convert_type(hi | (lo >> 16), jnp.int32)


def _unpack_halves(w):
    u = lax.bitcast_convert_type(w, jnp.uint32)
    hi = lax.bitcast_convert_type(u & jnp.uint32(0xFFFF0000), F32)
    lo = lax.bitcast_convert_type(u << 16, F32)
    return hi, lo


def _adaln_kernel(c_ref, w_ref, b_ref, o_ref):
    s = c_ref[...]
    s = s * jax.nn.sigmoid(s)
    o_ref[...] = _dot3(s, w_ref[...]) + b_ref[...]


def _adaln(cond, w_mod, b_mod):
    m, d = cond.shape
    n = w_mod.shape[1]
    tn = n // N_MOD
    return pl.pallas_call(
        _adaln_kernel,
        grid=(n // tn,),
        in_specs=[pl.BlockSpec((m, d), lambda i: (0, 0)),
                  pl.BlockSpec((d, tn), lambda i: (0, i)),
                  pl.BlockSpec((1, tn), lambda i: (0, i))],
        out_specs=pl.BlockSpec((m, tn), lambda i: (0, i)),
        out_shape=jax.ShapeDtypeStruct((m, n), F32),
        compiler_params=_cparams(("arbitrary",)),
        name="adaln",
    )(cond, w_mod, b_mod.reshape(1, n))


def _fold_kernel(c_ref, s_ref, w_ref, cw_ref, sw_ref):
    w = w_ref[0]
    cw_ref[0] = _dot3(c_ref[...], w)
    sw_ref[0] = _dot3(s_ref[...], w)


def _fold_fourier(cmat, smat, w_f):
    g, gd, _ = w_f.shape
    spec_m = pl.BlockSpec((gd, gd), lambda i: (0, 0))
    spec_w = pl.BlockSpec((1, gd, gd), lambda i: (i, 0, 0))
    return pl.pallas_call(
        _fold_kernel,
        grid=(g,),
        in_specs=[spec_m, spec_m, spec_w],
        out_specs=[spec_w, spec_w],
        out_shape=[jax.ShapeDtypeStruct((g, gd, gd), F32)] * 2,
        compiler_params=_cparams(("arbitrary",)),
        name="fold_fourier",
    )(cmat, smat, w_f)


def _seg_pitch(seg_len):
    n8 = seg_len // SUBLANES
    return SUBLANES * (n8 + 1 - n8 % 2)


def _pitched_store(v, buf, blk0):
    pitch = _seg_pitch(GRID_W)
    for r in range(v.shape[0] // GRID_W):
        for c in range(v.shape[1] // LANES):
            buf[c, (blk0 + r) * pitch:(blk0 + r) * pitch + GRID_W, :] = (
                v[r * GRID_W:(r + 1) * GRID_W, c * LANES:(c + 1) * LANES])


def _pitched_gather(buf, nb, store):
    pitch = _seg_pitch(GRID_W)
    for pos in range(GRID_W):
        for c in range(buf.shape[0]):
            store(pos, c, buf[c, pl.ds(pos, nb, stride=pitch), :])


def _stage_b_kernel(x_ref, sh_ref, sc_ref, g_ref, w_ref, j_ref, f_ref, xs_ref, gg_ref, fbuf, *, df, dr):
    tm = x_ref.shape[1]
    h = _rms(x_ref[0], g_ref[...] * (1.0 + sc_ref[0])) + sh_ref[0]
    hb = h.astype(BF16)
    gr = jnp.dot(hb, w_ref[:, df + dr:], preferred_element_type=F32)
    gg_ref[0] = _gelu_tanh(gr).astype(BF16)
    _pitched_store(jnp.dot(hb, w_ref[:, :df], preferred_element_type=F32), fbuf, 0)

    def store_f(pos, c, tile):
        f_ref[0, :, pos * df + c * LANES:pos * df + (c + 1) * LANES] = tile.astype(BF16)
    _pitched_gather(fbuf, tm // GRID_W, store_f)
    xr = jnp.dot(hb, w_ref[:, df:df + dr], preferred_element_type=F32).astype(BF16)
    for r in range(tm // GRID_W):
        blk = xr[r * GRID_W:(r + 1) * GRID_W]
        if r % 2 == 1:
            blk = jnp.dot(j_ref[...], blk, preferred_element_type=F32).astype(BF16)
        xs_ref[0, r * GRID_W:(r + 1) * GRID_W, :] = blk


def _stage_b(x, shift, scale, g, w_in_bf, jmat, df, dr, tm):
    b, s, d = x.shape
    n = w_in_bf.shape[1]
    vec = pl.BlockSpec((1, 1, d), lambda i, t: (i, 0, 0))
    out = pl.BlockSpec((1, tm, df), lambda i, t: (i, t, 0))
    nb = tm // GRID_W
    tok = jax.ShapeDtypeStruct((b, s, df), BF16)
    return pl.pallas_call(
        functools.partial(_stage_b_kernel, df=df, dr=dr),
        grid=(b, s // tm),
        in_specs=[pl.BlockSpec((1, tm, d), lambda i, t: (i, t, 0)), vec, vec,
                  pl.BlockSpec((1, d), lambda i, t: (0, 0)),
                  pl.BlockSpec((d, n), lambda i, t: (0, 0)),
                  pl.BlockSpec((GRID_W, GRID_W), lambda i, t: (0, 0))],
        out_specs=[pl.BlockSpec((1, nb, GRID_W * df), lambda i, t: (i, t, 0)), out, out],
        out_shape=[jax.ShapeDtypeStruct((b, s // GRID_W, GRID_W * df), BF16), tok, tok],
        scratch_shapes=[pltpu.VMEM((df // LANES, nb * _seg_pitch(GRID_W), LANES), F32)],
        compiler_params=_cparams(("parallel", "arbitrary")),
        name="stage_b",
    )(x, shift, scale, g.reshape(1, d), w_in_bf, jmat)


def _stage_b_ctx_kernel(x_ref, sh_ref, sc_ref, g_ref, w_ref, xr_ref):
    h = _rms(x_ref[0], g_ref[...] * (1.0 + sc_ref[0])) + sh_ref[0]
    xr_ref[0] = jnp.dot(h.astype(BF16), w_ref[...], preferred_element_type=F32).astype(BF16)


def _stage_b_ctx(ctx, shift, scale, g, w_in_bf, df, dr):
    b, s, d = ctx.shape
    assert df % dr == 0
    vec = pl.BlockSpec((1, 1, d), lambda i: (0, 0, 0))
    return pl.pallas_call(
        _stage_b_ctx_kernel,
        grid=(b,),
        in_specs=[pl.BlockSpec((1, s, d), lambda i: (i, 0, 0)), vec, vec,
                  pl.BlockSpec((1, d), lambda i: (0, 0)),
                  pl.BlockSpec((d, dr), lambda i: (0, df // dr))],
        out_specs=pl.BlockSpec((1, s, dr), lambda i: (i, 0, 0)),
        out_shape=jax.ShapeDtypeStruct((b, s, dr), BF16),
        compiler_params=_cparams(("arbitrary",)),
        name="stage_b_ctx",
    )(ctx, shift, scale, g.reshape(1, d), w_in_bf)


HALO = 16


def _sigmoid(x):
    return 0.5 * jnp.tanh(0.5 * x) + 0.5


def _rg_kernel(xs_ref, h0_ref, cw_ref, cb_ref, w_ref, b_ref, lam_ref, pm_ref, pmt_ref, out_ref, hfin_ref,
               hf_s, xc_s, a_s, u_s, hl_s, p_s, c_s, hc_s, *, tc, nchunk, seq, dr):
    p = pl.program_id(1)
    j = pl.program_id(2)
    cidx = jnp.where(p == 0, j, nchunk - 1 - j)
    start = pl.multiple_of(cidx * tc, tc)
    nseg = SUBLANES
    sl = tc // nseg
    sub = lax.broadcasted_iota(jnp.int32, (nseg, dr), 0)

    @pl.when(p == 0)
    def _():
        xp = jnp.dot(pm_ref[...], xs_ref[0, pl.ds(start, tc), :], preferred_element_type=F32)
        pstart = pl.multiple_of(jnp.maximum(start - HALO, 0), HALO)
        nstart = pl.multiple_of(jnp.minimum(start + tc, seq - HALO), HALO)
        prev = xs_ref[0, pl.ds(pstart, HALO), :].astype(F32)
        nxt = xs_ref[0, pl.ds(nstart, HALO), :].astype(F32)
        prev = jnp.where(cidx > 0, prev, 0.0)
        nxt = jnp.where(cidx < nchunk - 1, nxt, 0.0)
        tm2 = jnp.where(sub == 0, prev[HALO - 2:HALO - 1], pltpu.roll(xp[(sl - 2) * nseg:(sl - 1) * nseg], 1, 0))
        tm1 = jnp.where(sub == 0, prev[HALO - 1:HALO], pltpu.roll(xp[(sl - 1) * nseg:sl * nseg], 1, 0))
        tp1 = jnp.where(sub == nseg - 1, nxt[0:1], pltpu.roll(xp[0:nseg], nseg - 1, 0))
        ext = jnp.concatenate([tm2, tm1, xp, tp1], axis=0)
        xc = cb_ref[...] + cw_ref[0:1, :] * ext[0:tc]
        for k in range(1, CONV_W):
            xc = xc + cw_ref[k:k + 1, :] * ext[k * nseg:k * nseg + tc]
        xc_s[pl.ds(start, tc), :] = xc

    xc = xc_s[pl.ds(start, tc), :]
    gates = jnp.dot(xc.astype(BF16), w_ref[p], preferred_element_type=F32) + b_ref[p]
    i = _sigmoid(gates[:, dr:])
    half_c = (-0.5 * RG_C) * jax.nn.softplus(-lam_ref[p])
    log_a = half_c * jnp.tanh(0.5 * gates[:, :dr]) + half_c
    a = jnp.exp(log_a)
    a_s[...] = a
    w = -jnp.tanh(log_a) * (1.0 + a * a)
    u_s[...] = jnp.where(w > 0.0, w * lax.rsqrt(w), 0.0) * (i * xc)

    @pl.when(jnp.logical_and(p == 0, j == 0))
    def _():
        hfin_ref[...] = jnp.zeros_like(hfin_ref)

    @pl.when(j == 0)
    def _():
        hc_s[0:1, :] = h0_ref[0, pl.ds(p, 1), :]

    def segment_scan(reverse):
        def body(q, carry):
            t = (sl - 1 - q) if reverse else q
            rows = pl.ds(pl.multiple_of(t * nseg, nseg), nseg)
            h, pr = carry
            av = a_s[rows, :]
            h = av * h + u_s[rows, :]
            pr = av * pr
            hl_s[rows, :] = h
            p_s[rows, :] = pr
            return h, pr
        h_end, p_end = lax.fori_loop(0, sl, body, (jnp.zeros((nseg, dr), F32), jnp.ones((nseg, dr), F32)),
                                     unroll=4)
        carry = hc_s[0:1, :]
        for g in (range(nseg - 1, -1, -1) if reverse else range(nseg)):
            c_s[g:g + 1, :] = carry
            carry = h_end[g:g + 1, :] + p_end[g:g + 1, :] * carry
        hc_s[0:1, :] = carry
        return c_s[...]

    def corrected(cin):
        h = hl_s[...].reshape(sl, nseg, dr) + p_s[...].reshape(sl, nseg, dr) * cin[None]
        return h.reshape(tc, dr)

    @pl.when(p == 0)
    def _():
        hf_s[pl.ds(start, tc), :] = corrected(segment_scan(False))

    @pl.when(p == 1)
    def _():
        tot = corrected(segment_scan(True)) + hf_s[pl.ds(start, tc), :]
        out_ref[0] = jnp.dot(pmt_ref[...], tot.astype(BF16), preferred_element_type=F32).astype(BF16)

    @pl.when(j == nchunk - 1)
    def _():
        hfin_ref[0, pl.ds(p, 1), :] = hc_s[0:1, :]


def _rg_scan(xs, h0, conv_w, conv_b, wcat, bcat, lam, tc):
    b, s, dr = xs.shape
    nchunk = s // tc
    last = nchunk - 1
    sl = tc // SUBLANES
    src = (np.arange(tc) % SUBLANES) * sl + np.arange(tc) // SUBLANES
    pm = np.zeros((tc, tc), np.float32)
    pm[np.arange(tc), src] = 1.0
    chunk_buf = pltpu.VMEM((tc, dr), F32)
    full2 = lambda shape: pl.BlockSpec(shape, lambda i, p, j: (0,) * len(shape))
    return pl.pallas_call(
        functools.partial(_rg_kernel, tc=tc, nchunk=nchunk, seq=s, dr=dr),
        grid=(b, 2, nchunk),
        in_specs=[pl.BlockSpec((1, s, dr), lambda i, p, j: (i, 0, 0)),
                  pl.BlockSpec((1, SUBLANES, dr), lambda i, p, j: (i, 0, 0)),
                  full2((CONV_W, dr)), full2((1, dr)),
                  full2((2, dr, 2 * dr)), full2((2, 1, 2 * dr)), full2((2, 1, dr)),
                  full2((tc, tc)), full2((tc, tc))],
        out_specs=[pl.BlockSpec((1, tc, dr), lambda i, p, j: (i, jnp.where(p == 0, last, last - j), 0)),
                   pl.BlockSpec((1, SUBLANES, dr), lambda i, p, j: (i, 0, 0))],
        out_shape=[jax.ShapeDtypeStruct((b, s, dr), BF16),
                   jax.ShapeDtypeStruct((b, SUBLANES, dr), F32)],
        scratch_shapes=[pltpu.VMEM((s, dr), F32), pltpu.VMEM((s, dr), F32),
                        chunk_buf, chunk_buf, chunk_buf, chunk_buf,
                        pltpu.VMEM((SUBLANES, dr), F32), pltpu.VMEM((SUBLANES, dr), F32)],
        compiler_params=_cparams(("arbitrary", "arbitrary", "arbitrary")),
        name="rg_scan",
    )(xs, h0, conv_w, conv_b.reshape(1, dr), wcat, bcat, lam, jnp.asarray(pm, BF16), jnp.asarray(pm.T, BF16))


def _f1_kernel(d_ref, x_ref, y_ref):
    y_ref[0] = jnp.dot(d_ref[...], x_ref[0], preferred_element_type=F32).astype(BF16)


def _fourier_stage1(fv, d2, tl):
    b, r, n = fv.shape
    return pl.pallas_call(
        _f1_kernel,
        grid=(b, n // tl),
        in_specs=[pl.BlockSpec((2 * r, r), lambda i, l: (0, 0)),
                  pl.BlockSpec((1, r, tl), lambda i, l: (i, 0, l))],
        out_specs=pl.BlockSpec((1, 2 * r, tl), lambda i, l: (i, 0, l)),
        out_shape=jax.ShapeDtypeStruct((b, 2 * r, n), BF16),
        compiler_params=_cparams(("parallel", "arbitrary")),
        name="fourier_stage1",
    )(d2, fv)


def _f2_kernel(y_ref, e_ref, wcs_ref, g_ref, o_ref, obuf, *, kb, df):
    zr, zi = [], []
    for q in range(kb):
        yk = jnp.concatenate([y_ref[0, 0, q], y_ref[0, 1, q]], axis=0)
        z = jnp.dot(e_ref[q], yk, preferred_element_type=F32)
        zr.append(z[:GRID_W])
        zi.append(z[GRID_W:])
    zr = jnp.concatenate(zr, axis=0).astype(BF16)
    zi = jnp.concatenate(zi, axis=0).astype(BF16)
    gd = wcs_ref.shape[2]
    o = jnp.concatenate(
        [jnp.dot(jnp.concatenate([zr[:, g * gd:(g + 1) * gd], zi[:, g * gd:(g + 1) * gd]], axis=1), wcs_ref[g],
                 preferred_element_type=F32) for g in range(df // gd)], axis=1)
    on = _rms(o, g_ref[...])

    def store_o(pos, c, tile):
        o_ref[0, pos, :, c * LANES:(c + 1) * LANES] = tile.astype(BF16)
    _pitched_store(on, obuf, 0)
    _pitched_gather(obuf, kb, store_o)


def _fourier_stage2(y5, etab, wcs, g, kb):
    b, _, r, w, df = y5.shape
    return pl.pallas_call(
        functools.partial(_f2_kernel, kb=kb, df=df),
        grid=(b, r // kb),
        in_specs=[pl.BlockSpec((1, 2, kb, w, df), lambda i, k: (i, 0, k, 0, 0)),
                  pl.BlockSpec((kb, 2 * w, 2 * w), lambda i, k: (k, 0, 0)),
                  pl.BlockSpec(wcs.shape, lambda i, k: (0, 0, 0)),
                  pl.BlockSpec((1, df), lambda i, k: (0, 0))],
        out_specs=pl.BlockSpec((1, w, kb, df), lambda i, k: (i, 0, k, 0)),
        out_shape=jax.ShapeDtypeStruct((b, w, r, df), BF16),
        scratch_shapes=[pltpu.VMEM((df // LANES, kb * _seg_pitch(GRID_W), LANES), F32)],
        compiler_params=_cparams(("parallel", "arbitrary")),
        name="fourier_stage2",
    )(y5, etab, wcs, g.reshape(1, df))


def _stage_m_kernel(fn_ref, hs_ref, gg_ref, x_ref, gtm_ref, shf_ref, scf_ref, gr_ref, gffn_ref,
                    wo_ref, wr_ref, br_ref, j_ref, x1_ref, h2_ref, idx_ref, gate_ref, *, df):
    tm = x_ref.shape[1]
    hs = hs_ref[0]
    blocks = []
    for r in range(tm // GRID_W):
        blk = hs[r * GRID_W:(r + 1) * GRID_W]
        if r % 2 == 1:
            blk = jnp.dot(j_ref[...], blk, preferred_element_type=F32)
        blocks.append(blk.astype(F32))
    rg = jnp.concatenate(blocks, axis=0) * gg_ref[0].astype(F32)
    rgn = _rms(rg, gr_ref[...]).astype(BF16)
    mix = jnp.dot(fn_ref[0], wo_ref[:df, :], preferred_element_type=F32)
    mix += jnp.dot(rgn, wo_ref[df:, :], preferred_element_type=F32)
    x1 = x_ref[0] + gtm_ref[0] * mix
    x1_ref[0] = x1
    h2 = _rms(x1, gffn_ref[...] * (1.0 + scf_ref[0])) + shf_ref[0]
    h2_ref[0] = _pack_halves(h2)

    logits = _dot3_nt(wr_ref[...], h2) + br_ref[...]
    eidx = lax.broadcasted_iota(jnp.int32, logits.shape, 0)
    vals, idxs = [], []
    for _ in range(TOP_K):
        m = jnp.max(logits, axis=0, keepdims=True)
        sel = jnp.min(jnp.where(logits == m, eidx, N_EXPERTS), axis=0, keepdims=True)
        vals.append(m)
        idxs.append(sel)
        logits = jnp.where(eidx == sel, -jnp.inf, logits)
    ex = [jnp.exp(v - vals[0]) for v in vals]
    den = ex[0] + ex[1] + ex[2] + ex[3]
    for k in range(TOP_K):
        gate_ref[k:k + 1, :] = ex[k] / den
        idx_ref[k:k + 1, :] = idxs[k]


def _stage_m(fn, hs, gg, x, gt_m, sh_f, sc_f, g_out_r, g_ffn, w_out_bf, w_router_t, b_router, jmat, tm):
    b, s, d = x.shape
    df = fn.shape[2]
    dr = hs.shape[2]
    nt = s // tm
    ne = w_router_t.shape[0]
    vec = pl.BlockSpec((1, 1, d), lambda i, t: (i, 0, 0))
    half = lambda dd: pl.BlockSpec((1, tm, dd), lambda i, t: (i, t, 0))
    full = lambda shape: pl.BlockSpec(shape, lambda i, t: (0,) * len(shape))
    tok = pl.BlockSpec((TOP_K, tm), lambda i, t: (0, i * nt + t))
    return pl.pallas_call(
        functools.partial(_stage_m_kernel, df=df),
        grid=(b, nt),
        in_specs=[half(df), half(dr), half(dr), half(d), vec, vec, vec,
                  full((1, dr)), full((1, d)), full((d, d)), full((ne, d)), full((ne, 1)),
                  full((GRID_W, GRID_W))],
        out_specs=[half(d), half(d // 2), tok, tok],
        out_shape=[jax.ShapeDtypeStruct((b, s, d), F32), jax.ShapeDtypeStruct((b, s, d // 2), jnp.int32),
                   jax.ShapeDtypeStruct((TOP_K, b * s), jnp.int32),
                   jax.ShapeDtypeStruct((TOP_K, b * s), F32)],
        compiler_params=_cparams(("parallel", "arbitrary")),
        name="stage_m",
    )(fn, hs, gg, x, gt_m, sh_f, sc_f, g_out_r.reshape(1, dr), g_ffn.reshape(1, d), w_out_bf,
      w_router_t, b_router.reshape(ne, 1), jmat)


def _rank_kernel(idx_ref, tri_ref, rank_ref, cnt_ref, carry_s):
    c = pl.program_id(0)

    @pl.when(c == 0)
    def _():
        carry_s[...] = jnp.zeros_like(carry_s)

    l = tri_ref.shape[0]
    eidx = lax.broadcasted_iota(jnp.int32, (N_EXPERTS, l), 0)
    for sub in range(idx_ref.shape[1] // l):
        lanes = slice(sub * l, (sub + 1) * l)
        onehots = [eidx == idx_ref[k:k + 1, lanes] for k in range(TOP_K)]
        ohs = [jnp.where(o, 1.0, 0.0) for o in onehots]
        prefix = jnp.dot(jnp.concatenate(ohs, axis=0).astype(BF16), tri_ref[...], preferred_element_type=F32)
        for k in range(TOP_K):
            carry = carry_s[:, 0:1]
            pk = prefix[k * N_EXPERTS:(k + 1) * N_EXPERTS]
            rank = jnp.sum(jnp.where(onehots[k], pk - 1.0 + carry, 0.0), axis=0, keepdims=True)
            rank_ref[k:k + 1, lanes] = rank.astype(jnp.int32)
            carry_s[...] = carry_s[...] + jnp.sum(ohs[k], axis=1, keepdims=True)
    cnt_ref[...] = carry_s[...].astype(jnp.int32)


def _dispatch_ranks(idx, tri, tl):
    k, t = idx.shape
    return pl.pallas_call(
        _rank_kernel,
        grid=(t // tl,),
        in_specs=[pl.BlockSpec((k, tl), lambda c: (0, c)),
                  pl.BlockSpec(tri.shape, lambda c: (0, 0))],
        out_specs=[pl.BlockSpec((k, tl), lambda c: (0, c)),
                   pl.BlockSpec((N_EXPERTS, LANES), lambda c: (0, 0))],
        out_shape=[jax.ShapeDtypeStruct((k, t), jnp.int32),
                   jax.ShapeDtypeStruct((N_EXPERTS, LANES), jnp.int32)],
        scratch_shapes=[pltpu.VMEM((N_EXPERTS, LANES), F32)],
        compiler_params=_cparams(("arbitrary",)),
        name="dispatch_ranks",
    )(idx, tri)


def _moe_kernel(be_ref, bv_ref, nu_ref, x_ref, wgu_ref, bgu_ref, wd_ref, bd_ref, o_ref, wgu_s, wd_s, *, dff):
    i = pl.program_id(0)
    h = x_ref.shape[1]

    @pl.when(jnp.logical_or(i == 0, be_ref[i] != be_ref[jnp.maximum(i - 1, 0)]))
    def _():
        wgu_s[...] = wgu_ref[0].astype(BF16)
        wd_s[...] = wd_ref[0].astype(BF16)

    @pl.when(i < nu_ref[0])
    def _():
        rows = lax.broadcasted_iota(jnp.int32, x_ref.shape, 0)
        xw = jnp.where(rows < bv_ref[i], x_ref[...], 0)
        xa, xb = _unpack_halves(xw)
        xa = xa.astype(BF16)
        xb = xb.astype(BF16)
        acc = None
        for c in range(dff // MOE_FF_CHUNK):
            gs = slice(c * MOE_FF_CHUNK, (c + 1) * MOE_FF_CHUNK)
            us = slice(dff + c * MOE_FF_CHUNK, dff + (c + 1) * MOE_FF_CHUNK)
            g = jnp.dot(xa, wgu_s[:h, gs], preferred_element_type=F32)
            g += jnp.dot(xb, wgu_s[h:, gs], preferred_element_type=F32)
            u = jnp.dot(xa, wgu_s[:h, us], preferred_element_type=F32)
            u += jnp.dot(xb, wgu_s[h:, us], preferred_element_type=F32)
            gt = jnp.minimum(g + bgu_ref[0, :, gs], SWIGLU_LIMIT)
            up = jnp.clip(u + bgu_ref[0, :, us], -SWIGLU_LIMIT, SWIGLU_LIMIT)
            act = (up + 1.0) * (gt * _sigmoid(SWIGLU_ALPHA * gt))
            part = jnp.dot(act.astype(BF16), wd_s[gs, :], preferred_element_type=F32)
            acc = part if acc is None else acc + part
        o_ref[...] = _pack_halves(acc + bd_ref[0])


def _moe_experts(blk_expert, blk_valid, n_used, xs, wgu, bgu, wd, bd, tmm):
    cap, h = xs.shape
    ne, d, dff2 = wgu.shape
    dff = dff2 // 2
    row_blk = lambda i, be, bv, nu: (jnp.maximum(jnp.minimum(i, nu[0] - 1), 0), 0)
    wsel = lambda i, be, bv, nu: (be[i], 0, 0)
    grid_spec = pltpu.PrefetchScalarGridSpec(
        num_scalar_prefetch=3,
        grid=(cap // tmm,),
        in_specs=[pl.BlockSpec((tmm, h), row_blk),
                  pl.BlockSpec((1, d, dff2), wsel),
                  pl.BlockSpec((1, 1, dff2), wsel),
                  pl.BlockSpec((1, dff, d), wsel),
                  pl.BlockSpec((1, 1, d), wsel)],
        out_specs=pl.BlockSpec((tmm, h), row_blk),
        scratch_shapes=[pltpu.VMEM((d, dff2), BF16), pltpu.VMEM((dff, d), BF16)],
    )
    return pl.pallas_call(
        functools.partial(_moe_kernel, dff=dff),
        grid_spec=grid_spec,
        out_shape=jax.ShapeDtypeStruct((cap, h), jnp.int32),
        compiler_params=_cparams(("arbitrary",)),
        name="moe_experts",
    )(blk_expert, blk_valid, n_used, xs, wgu, bgu.reshape(ne, 1, dff2), wd, bd.reshape(ne, 1, d))


SC_CHUNK = 64
SC_ID_CHUNK = 128


def _sc_workers():
    info = plsc.get_sparse_core_info()
    return info.num_cores, info.num_subcores


def _sc_scatter_into(out_ref, rows, dest):
    n, w = rows.shape
    nc, ns = _sc_workers()
    per_w = n // (nc * ns)
    assert per_w % SC_CHUNK == 0
    mesh = plsc.VectorSubcoreMesh(core_axis_name="c", subcore_axis_name="s")

    @functools.partial(
        pl.kernel, mesh=mesh, out_type=(),
        scratch_types=[pltpu.VMEM((SC_CHUNK,), jnp.int32),
                       pltpu.VMEM((SC_CHUNK, w), jnp.int32),
                       pltpu.SemaphoreType.DMA],
    )
    def scatter_rows(rows_hbm, dest_hbm, out_hbm, idx_v, rows_v, sem):
        base = (lax.axis_index("s") * nc + lax.axis_index("c")) * per_w

        @pl.loop(0, per_w // SC_CHUNK)
        def _(j):
            off = pl.multiple_of(base + j * SC_CHUNK, SC_CHUNK)
            pltpu.sync_copy(rows_hbm.at[pl.ds(off, SC_CHUNK)], rows_v)
            pltpu.sync_copy(dest_hbm.at[pl.ds(off, SC_CHUNK)], idx_v)
            pltpu.async_copy(rows_v, out_hbm.at[idx_v], sem).wait()

    scatter_rows(rows, dest, out_ref)


def _sc_scatter_ids(dest, out_rows):
    n = dest.shape[0]
    info = plsc.get_sparse_core_info()
    nc, ns, nl = info.num_cores, info.num_subcores, info.num_lanes
    per_w = n // (nc * ns)
    chunk = SC_ID_CHUNK
    assert per_w % (2 * chunk) == 0
    mesh = plsc.VectorSubcoreMesh(core_axis_name="c", subcore_axis_name="s")
    idx_t = pltpu.VMEM((chunk,), jnp.int32)
    rows_t = pltpu.VMEM((chunk, LANES), jnp.int32)

    @functools.partial(
        pl.kernel, mesh=mesh,
        out_type=jax.ShapeDtypeStruct((out_rows, LANES), jnp.int32),
        scratch_types=[idx_t, idx_t, rows_t, rows_t, pltpu.SemaphoreType.DMA, pltpu.SemaphoreType.DMA],
    )
    def scatter_ids(dest_hbm, out_hbm, idx_a, idx_b, rows_a, rows_b, sem_a, sem_b):
        base = (lax.axis_index("s") * nc + lax.axis_index("c")) * per_w

        def start(off, idx_v, rows_v, sem):
            for r in range(chunk):
                rows_v[r, pl.ds(0, nl)] = jnp.zeros((nl,), jnp.int32) + (off + r)
            pltpu.sync_copy(dest_hbm.at[pl.ds(off, chunk)], idx_v)
            return pltpu.async_copy(rows_v, out_hbm.at[idx_v], sem)

        @pl.loop(0, per_w // (2 * chunk))
        def _(j):
            off = pl.multiple_of(base + j * (2 * chunk), 2 * chunk)
            copy_a = start(off, idx_a, rows_a, sem_a)
            copy_b = start(off + chunk, idx_b, rows_b, sem_b)
            copy_a.wait()
            copy_b.wait()

    return scatter_ids(dest)


def _sc_gather(table, idx):
    n = idx.shape[0]
    w = table.shape[1]
    nc, ns = _sc_workers()
    per_w = n // (nc * ns)
    assert per_w % SC_CHUNK == 0
    mesh = plsc.VectorSubcoreMesh(core_axis_name="c", subcore_axis_name="s")

    @functools.partial(
        pl.kernel, mesh=mesh,
        out_type=jax.ShapeDtypeStruct((n, w), jnp.int32),
        scratch_types=[pltpu.VMEM((SC_CHUNK,), jnp.int32),
                       pltpu.VMEM((SC_CHUNK, w), jnp.int32),
                       pltpu.SemaphoreType.DMA],
    )
    def gather_rows(table_hbm, idx_hbm, out_hbm, idx_v, rows_v, sem):
        base = (lax.axis_index("s") * nc + lax.axis_index("c")) * per_w

        @pl.loop(0, per_w // SC_CHUNK)
        def _(j):
            off = pl.multiple_of(base + j * SC_CHUNK, SC_CHUNK)
            pltpu.sync_copy(idx_hbm.at[pl.ds(off, SC_CHUNK)], idx_v)
            pltpu.async_copy(table_hbm.at[idx_v], rows_v, sem).wait()
            pltpu.sync_copy(rows_v, out_hbm.at[pl.ds(off, SC_CHUNK)])

    return gather_rows(table, idx)


def _combine_kernel(x1_ref, y_ref, gate_ref, gtf_ref, g_ref, o_ref):
    h = y_ref.shape[3]
    gates = gate_ref[0]
    moe_a = moe_b = None
    for k in range(TOP_K):
        ya, yb = _unpack_halves(y_ref[k, 0])
        gk = gates[:, k:k + 1]
        moe_a = gk * ya if k == 0 else moe_a + gk * ya
        moe_b = gk * yb if k == 0 else moe_b + gk * yb
    za = x1_ref[0, :, :h] + gtf_ref[0, :, :h] * moe_a
    zb = x1_ref[0, :, h:] + gtf_ref[0, :, h:] * moe_b
    ms = (jnp.sum(za * za, axis=-1, keepdims=True) + jnp.sum(zb * zb, axis=-1, keepdims=True)) / (2 * h)
    inv = lax.rsqrt(ms + EPS)
    o_ref[0, :, :h] = za * inv * g_ref[:, :h]
    o_ref[0, :, h:] = zb * inv * g_ref[:, h:]


def _combine(x1, yk, gates_t, gt_f, g_final, tm):
    b, s, d = x1.shape
    h = yk.shape[3]
    return pl.pallas_call(
        _combine_kernel,
        grid=(b, s // tm),
        in_specs=[pl.BlockSpec((1, tm, d), lambda i, t: (i, t, 0)),
                  pl.BlockSpec((TOP_K, 1, tm, h), lambda i, t: (0, i, t, 0)),
                  pl.BlockSpec((1, tm, TOP_K), lambda i, t: (i, t, 0)),
                  pl.BlockSpec((1, 1, d), lambda i, t: (i, 0, 0)),
                  pl.BlockSpec((1, d), lambda i, t: (0, 0))],
        out_specs=pl.BlockSpec((1, tm, d), lambda i, t: (i, t, 0)),
        out_shape=jax.ShapeDtypeStruct((b, s, d), F32),
        compiler_params=_cparams(("parallel", "arbitrary")),
        name="combine",
    )(x1, yk, gates_t, gt_f, g_final.reshape(1, d))


def _dft_tables(rows, gd):
    seq = rows * GRID_W
    n = np.arange(rows)
    ang1 = 2.0 * np.pi * np.outer(n, n) / rows
    d2 = np.concatenate([np.cos(ang1), -np.sin(ang1)], axis=0)
    k1 = np.arange(rows)[:, None, None]
    k2 = np.arange(GRID_W)[None, :, None]
    n2 = np.arange(GRID_W)[None, None, :]
    ang2 = 2.0 * np.pi * ((n2 * (k1 + rows * k2)) % seq) / seq
    ec, es = np.cos(ang2), np.sin(ang2)
    etab = np.concatenate([np.concatenate([ec, es], axis=2),
                           np.concatenate([-es, ec], axis=2)], axis=1)
    c = np.arange(gd)
    angc = 2.0 * np.pi * np.outer(c, c) / gd
    scale = 1.0 / np.sqrt(seq * gd)
    return (jnp.asarray(d2, BF16), jnp.asarray(etab, BF16),
            jnp.asarray(np.cos(angc) * scale, F32), jnp.asarray(np.sin(angc) * scale, F32))


def _block_diag(w):
    h, i, o = w.shape
    eye = jnp.eye(h, dtype=w.dtype)
    return (eye[:, None, :, None] * w[:, :, None, :]).reshape(h * i, h * o)


def kernel(x, c, ctx, c_ctx, w_mod, b_mod, g_norm_mix, g_norm_ffn, w_in, w_fourier, conv_w, conv_b,
           rg_w_a, rg_b_a, rg_w_x, rg_b_x, rg_lam, g_out_fourier, g_out_rg, w_out, w_router, b_router,
           w_gate_up, b_gate_up, w_down, b_down, g_final):
    assert w_mod.shape[0] == 1, "single-layer stack only"
    b, s, d = x.shape
    df = w_fourier.shape[1] * w_fourier.shape[2]
    dr = conv_w.shape[2]
    gd = w_fourier.shape[2]
    rows = s // GRID_W
    t = b * s
    ne = w_router.shape[2]

    mrows = -(-(b + 1) // SUBLANES) * SUBLANES
    cond = jnp.zeros((mrows, d), F32).at[:b].set(c).at[b].set(c_ctx)
    mod = _adaln(cond, w_mod[0], b_mod[0])
    sh_m, sc_m, gt_m, sh_f, sc_f, gt_f = [mod[:b, k * d:(k + 1) * d].reshape(b, 1, d) for k in range(N_MOD)]
    csh_m = mod[b:b + 1, 0:d].reshape(1, 1, d)
    csc_m = mod[b:b + 1, d:2 * d].reshape(1, 1, d)

    tm = min(s, TOKEN_TILE)
    d2, etab, cmat, smat = _dft_tables(rows, gd)
    jmat = jnp.asarray(np.eye(GRID_W)[::-1].copy(), BF16)

    w_in_bf = w_in[0].astype(BF16)
    f, xs, gg = _stage_b(x, sh_m, sc_m, g_norm_mix[0], w_in_bf, jmat, df, dr, tm=tm)
    xr_ctx = _stage_b_ctx(ctx, csh_m, csc_m, g_norm_mix[0], w_in_bf, df, dr)

    wcat = jnp.stack([jnp.concatenate([_block_diag(rg_w_a[0, dd]), _block_diag(rg_w_x[0, dd])], axis=1)
                      for dd in range(2)]).astype(BF16)
    bcat = jnp.concatenate([rg_b_a[0], rg_b_x[0]], axis=1).reshape(2, 1, 2 * dr)
    lam = rg_lam[0].reshape(2, 1, dr)
    h0 = jnp.zeros((b, SUBLANES, dr), F32)
    _, hfin_ctx = _rg_scan(xr_ctx, h0, conv_w[0], conv_b[0], wcat, bcat, lam, tc=ctx.shape[1])
    hs, _ = _rg_scan(xs, hfin_ctx, conv_w[0], conv_b[0], wcat, bcat, lam, tc=min(s, SCAN_CHUNK))

    cw, sw = _fold_fourier(cmat, smat, w_fourier[0])
    wcs = jnp.concatenate([cw, sw], axis=1).astype(BF16)
    y = _fourier_stage1(f, d2, tl=GRID_W * df)
    fn = _fourier_stage2(y.reshape(b, 2, rows, GRID_W, df), etab, wcs, g_out_fourier[0],
                         kb=min(rows, 32))
    fn = fn.reshape(b, s, df)

    x1, h2, idx, gates = _stage_m(fn, hs, gg, x, gt_m, sh_f, sc_f, g_out_rg[0], g_norm_ffn[0],
                                  w_out[0].astype(BF16), w_router[0].T, b_router[0], jmat, tm=tm)

    tl = min(t, RANK_TILE)
    tri = jnp.asarray(np.triu(np.ones((tl, tl))), BF16)
    rank, cnt = _dispatch_ranks(idx, tri, min(t, RANK_STEP))
    counts = cnt[:, 0]
    tmm = MOE_ROW_TILE
    padded = (counts + tmm - 1) // tmm * tmm
    pad_end = jnp.cumsum(padded)
    pad_start = pad_end - padded
    eids = jnp.arange(ne, dtype=jnp.int32)
    dest = rank + jnp.sum(jnp.where(idx[:, :, None] == eids, pad_start, 0), axis=-1)
    n_blocks = -(-(t * TOP_K) // tmm) + ne
    cap = n_blocks * tmm
    n_used = (pad_end[-1] // tmm).astype(jnp.int32).reshape(1)
    blk_start = jnp.arange(n_blocks, dtype=jnp.int32) * tmm
    blk_expert = jnp.sum(blk_start[:, None] >= pad_end[None, :], axis=1).astype(jnp.int32)
    last_expert = jnp.sum(pad_end[-1] - tmm >= pad_end).astype(jnp.int32)
    blk_expert = jnp.minimum(blk_expert, last_expert)
    sel = blk_expert[:, None] == eids
    blk_first = jnp.sum(jnp.where(sel, pad_start, 0), axis=1)
    blk_count = jnp.sum(jnp.where(sel, counts, 0), axis=1)
    blk_valid = jnp.clip(blk_count - (blk_start - blk_first), 0, tmm).astype(jnp.int32)

    na = TOP_K * t
    inv = _sc_scatter_ids(dest.reshape(-1), cap)[:, 0].reshape(n_blocks, tmm)
    live = jnp.arange(tmm, dtype=jnp.int32)[None, :] < blk_valid[:, None]
    spread = jnp.arange(cap, dtype=jnp.int32).reshape(n_blocks, tmm) % t
    src_tok = jnp.where(live, inv % t, spread).reshape(-1)
    dst_row = jnp.where(live, inv, na + spread).reshape(-1)

    h2_rows = h2.reshape(t, d // 2)
    y_all = jax.empty_ref(jax.ShapeDtypeStruct(((TOP_K + 1) * t, d // 2), jnp.int32))
    unit = n_blocks // sum(MOE_PIPE)
    assert unit * sum(MOE_PIPE) == n_blocks
    blk0 = 0
    for parts in MOE_PIPE:
        nq = parts * unit
        blocks = slice(blk0, blk0 + nq)
        rows = slice(blk0 * tmm, (blk0 + nq) * tmm)
        x_q = _sc_gather(h2_rows, src_tok[rows])
        nu_q = jnp.clip(n_used - blk0, 0, nq).astype(jnp.int32)
        blk0 += nq
        y_q = _moe_experts(blk_expert[blocks], blk_valid[blocks], nu_q, x_q,
                           w_gate_up[0], b_gate_up[0], w_down[0], b_down[0], tmm)
        _sc_scatter_into(y_all, y_q, dst_row[rows])
    yk = y_all[...].reshape(TOP_K + 1, b, s, d // 2)
    return _combine(x1, yk, gates.T.reshape(b, s, TOP_K), gt_f, g_final, tm)
```

```python
import functools

import numpy as np
import jax
import jax.numpy as jnp
from jax import lax
from jax.experimental import pallas as pl
from jax.experimental.pallas import tpu as pltpu
from jax.experimental.pallas import tpu_sc as plsc

GRID_W = 64
CONV_W = 4
CONV_PAD_LO = 2
RG_C = 8.0
N_EXPERTS = 32
TOP_K = 4
SWIGLU_LIMIT = 7.0
SWIGLU_ALPHA = 1.702
N_MOD = 6
EPS = 1e-6

LANES = 128
SUBLANES = 8
VMEM_LIMIT_BYTES = 56 * 1024 * 1024
TOKEN_TILE = 1024
RANK_TILE = 512
RANK_STEP = 4096
SCAN_CHUNK = 512
MOE_ROW_TILE = 512
MOE_FF_CHUNK = 512
MOE_PIPE = (3, 11, 7, 3)

F32 = jnp.float32
BF16 = jnp.bfloat16


def _cparams(sem):
    return pltpu.CompilerParams(dimension_semantics=sem, vmem_limit_bytes=VMEM_LIMIT_BYTES)


def _split_bf16(a):
    hi = a.astype(BF16)
    lo = (a - hi.astype(F32)).astype(BF16)
    return hi, lo


def _dot3(a, b):
    ah, al = _split_bf16(a)
    bh, bl = _split_bf16(b)
    m = a.shape[0]
    both = jnp.dot(jnp.concatenate([ah, al], axis=0), bh, preferred_element_type=F32)
    return both[:m] + both[m:] + jnp.dot(ah, bl, preferred_element_type=F32)


def _dot3_nt(a, b):
    dn = (((1,), (1,)), ((), ()))
    m = a.shape[0]
    ah, al = _split_bf16(a)
    bh, bl = _split_bf16(b)
    both = lax.dot_general(jnp.concatenate([ah, al], axis=0), bh, dn, preferred_element_type=F32)
    return both[:m] + both[m:] + lax.dot_general(ah, bl, dn, preferred_element_type=F32)


def _gelu_tanh(x):
    return 0.5 * x * (1.0 + jnp.tanh(0.7978845608028654 * (x + 0.044715 * (x * x * x))))


def _rms(x, g):
    return x * lax.rsqrt(jnp.mean(x * x, axis=-1, keepdims=True) + EPS) * g


def _pack_halves(v):
    h = v.shape[1] // 2
    hi = lax.bitcast_convert_type(v[:, :h].astype(BF16).astype(F32), jnp.uint32)
    lo = lax.bitcast_convert_type(v[:, h:].astype(BF16).astype(F32), jnp.uint32)
    return lax.bitcast_convert_type(hi | (lo >> 16), jnp.int32)


def _unpack_halves(w):
    u = lax.bitcast_convert_type(w, jnp.uint32)
    hi = lax.bitcast_convert_type(u & jnp.uint32(0xFFFF0000), F32)
    lo = lax.bitcast_convert_type(u << 16, F32)
    return hi, lo


def _adaln_kernel(c_ref, w_ref, b_ref, o_ref):
    s = c_ref[...]
    s = s * jax.nn.sigmoid(s)
    o_ref[...] = _dot3(s, w_ref[...]) + b_ref[...]


def _adaln(cond, w_mod, b_mod):
    m, d = cond.shape
    n = w_mod.shape[1]
    tn = n // N_MOD
    return pl.pallas_call(
        _adaln_kernel,
        grid=(n // tn,),
        in_specs=[pl.BlockSpec((m, d), lambda i: (0, 0)),
                  pl.BlockSpec((d, tn), lambda i: (0, i)),
                  pl.BlockSpec((1, tn), lambda i: (0, i))],
        out_specs=pl.BlockSpec((m, tn), lambda i: (0, i)),
        out_shape=jax.ShapeDtypeStruct((m, n), F32),
        compiler_params=_cparams(("arbitrary",)),
        name="adaln",
    )(cond, w_mod, b_mod.reshape(1, n))


def _fold_kernel(c_ref, s_ref, w_ref, cw_ref, sw_ref):
    w = w_ref[0]
    cw_ref[0] = _dot3(c_ref[...], w)
    sw_ref[0] = _dot3(s_ref[...], w)


def _fold_fourier(cmat, smat, w_f):
    g, gd, _ = w_f.shape
    spec_m = pl.BlockSpec((gd, gd), lambda i: (0, 0))
    spec_w = pl.BlockSpec((1, gd, gd), lambda i: (i, 0, 0))
    return pl.pallas_call(
        _fold_kernel,
        grid=(g,),
        in_specs=[spec_m, spec_m, spec_w],
        out_specs=[spec_w, spec_w],
        out_shape=[jax.ShapeDtypeStruct((g, gd, gd), F32)] * 2,
        compiler_params=_cparams(("arbitrary",)),
        name="fold_fourier",
    )(cmat, smat, w_f)


def _seg_pitch(seg_len):
    n8 = seg_len // SUBLANES
    return SUBLANES * (n8 + 1 - n8 % 2)


def _pitched_store(v, buf, blk0):
    pitch = _seg_pitch(GRID_W)
    for r in range(v.shape[0] // GRID_W):
        for c in range(v.shape[1] // LANES):
            buf[c, (blk0 + r) * pitch:(blk0 + r) * pitch + GRID_W, :] = (
                v[r * GRID_W:(r + 1) * GRID_W, c * LANES:(c + 1) * LANES])


def _pitched_gather(buf, nb, store):
    pitch = _seg_pitch(GRID_W)
    for pos in range(GRID_W):
        for c in range(buf.shape[0]):
            store(pos, c, buf[c, pl.ds(pos, nb, stride=pitch), :])


def _stage_b_kernel(x_ref, sh_ref, sc_ref, g_ref, w_ref, j_ref, f_ref, xs_ref, gg_ref, fbuf, *, df, dr):
    tm = x_ref.shape[1]
    h = _rms(x_ref[0], g_ref[...] * (1.0 + sc_ref[0])) + sh_ref[0]
    hb = h.astype(BF16)
    gr = jnp.dot(hb, w_ref[:, df + dr:], preferred_element_type=F32)
    gg_ref[0] = _gelu_tanh(gr).astype(BF16)
    _pitched_store(jnp.dot(hb, w_ref[:, :df], preferred_element_type=F32), fbuf, 0)

    def store_f(pos, c, tile):
        f_ref[0, :, pos * df + c * LANES:pos * df + (c + 1) * LANES] = tile.astype(BF16)
    _pitched_gather(fbuf, tm // GRID_W, store_f)
    xr = jnp.dot(hb, w_ref[:, df:df + dr], preferred_element_type=F32).astype(BF16)
    for r in range(tm // GRID_W):
        blk = xr[r * GRID_W:(r + 1) * GRID_W]
        if r % 2 == 1:
            blk = jnp.dot(j_ref[...], blk, preferred_element_type=F32).astype(BF16)
        xs_ref[0, r * GRID_W:(r + 1) * GRID_W, :] = blk


def _stage_b(x, shift, scale, g, w_in_bf, jmat, df, dr, tm):
    b, s, d = x.shape
    n = w_in_bf.shape[1]
    vec = pl.BlockSpec((1, 1, d), lambda i, t: (i, 0, 0))
    out = pl.BlockSpec((1, tm, df), lambda i, t: (i, t, 0))
    nb = tm // GRID_W
    tok = jax.ShapeDtypeStruct((b, s, df), BF16)
    return pl.pallas_call(
        functools.partial(_stage_b_kernel, df=df, dr=dr),
        grid=(b, s // tm),
        in_specs=[pl.BlockSpec((1, tm, d), lambda i, t: (i, t, 0)), vec, vec,
                  pl.BlockSpec((1, d), lambda i, t: (0, 0)),
                  pl.BlockSpec((d, n), lambda i, t: (0, 0)),
                  pl.BlockSpec((GRID_W, GRID_W), lambda i, t: (0, 0))],
        out_specs=[pl.BlockSpec((1, nb, GRID_W * df), lambda i, t: (i, t, 0)), out, out],
        out_shape=[jax.ShapeDtypeStruct((b, s // GRID_W, GRID_W * df), BF16), tok, tok],
        scratch_shapes=[pltpu.VMEM((df // LANES, nb * _seg_pitch(GRID_W), LANES), F32)],
        compiler_params=_cparams(("parallel", "arbitrary")),
        name="stage_b",
    )(x, shift, scale, g.reshape(1, d), w_in_bf, jmat)


def _stage_b_ctx_kernel(x_ref, sh_ref, sc_ref, g_ref, w_ref, xr_ref):
    h = _rms(x_ref[0], g_ref[...] * (1.0 + sc_ref[0])) + sh_ref[0]
    xr_ref[0] = jnp.dot(h.astype(BF16), w_ref[...], preferred_element_type=F32).astype(BF16)


def _stage_b_ctx(ctx, shift, scale, g, w_in_bf, df, dr):
    b, s, d = ctx.shape
    assert df % dr == 0
    vec = pl.BlockSpec((1, 1, d), lambda i: (0, 0, 0))
    return pl.pallas_call(
        _stage_b_ctx_kernel,
        grid=(b,),
        in_specs=[pl.BlockSpec((1, s, d), lambda i: (i, 0, 0)), vec, vec,
                  pl.BlockSpec((1, d), lambda i: (0, 0)),
                  pl.BlockSpec((d, dr), lambda i: (0, df // dr))],
        out_specs=pl.BlockSpec((1, s, dr), lambda i: (i, 0, 0)),
        out_shape=jax.ShapeDtypeStruct((b, s, dr), BF16),
        compiler_params=_cparams(("arbitrary",)),
        name="stage_b_ctx",
    )(ctx, shift, scale, g.reshape(1, d), w_in_bf)


HALO = 16


def _sigmoid(x):
    return 0.5 * jnp.tanh(0.5 * x) + 0.5


def _rg_kernel(xs_ref, h0_ref, cw_ref, cb_ref, w_ref, b_ref, lam_ref, pm_ref, pmt_ref, out_ref, hfin_ref,
               hf_s, xc_s, a_s, u_s, hl_s, p_s, c_s, hc_s, *, tc, nchunk, seq, dr):
    p = pl.program_id(1)
    j = pl.program_id(2)
    cidx = jnp.where(p == 0, j, nchunk - 1 - j)
    start = pl.multiple_of(cidx * tc, tc)
    nseg = SUBLANES
    sl = tc // nseg
    sub = lax.broadcasted_iota(jnp.int32, (nseg, dr), 0)

    @pl.when(p == 0)
    def _():
        xp = jnp.dot(pm_ref[...], xs_ref[0, pl.ds(start, tc), :], preferred_element_type=F32)
        pstart = pl.multiple_of(jnp.maximum(start - HALO, 0), HALO)
        nstart = pl.multiple_of(jnp.minimum(start + tc, seq - HALO), HALO)
        prev = xs_ref[0, pl.ds(pstart, HALO), :].astype(F32)
        nxt = xs_ref[0, pl.ds(nstart, HALO), :].astype(F32)
        prev = jnp.where(cidx > 0, prev, 0.0)
        nxt = jnp.where(cidx < nchunk - 1, nxt, 0.0)
        tm2 = jnp.where(sub == 0, prev[HALO - 2:HALO - 1], pltpu.roll(xp[(sl - 2) * nseg:(sl - 1) * nseg], 1, 0))
        tm1 = jnp.where(sub == 0, prev[HALO - 1:HALO], pltpu.roll(xp[(sl - 1) * nseg:sl * nseg], 1, 0))
        tp1 = jnp.where(sub == nseg - 1, nxt[0:1], pltpu.roll(xp[0:nseg], nseg - 1, 0))
        ext = jnp.concatenate([tm2, tm1, xp, tp1], axis=0)
        xc = cb_ref[...] + cw_ref[0:1, :] * ext[0:tc]
        for k in range(1, CONV_W):
            xc = xc + cw_ref[k:k + 1, :] * ext[k * nseg:k * nseg + tc]
        xc_s[pl.ds(start, tc), :] = xc

    xc = xc_s[pl.ds(start, tc), :]
    gates = jnp.dot(xc.astype(BF16), w_ref[p], preferred_element_type=F32) + b_ref[p]
    i = _sigmoid(gates[:, dr:])
    half_c = (-0.5 * RG_C) * jax.nn.softplus(-lam_ref[p])
    log_a = half_c * jnp.tanh(0.5 * gates[:, :dr]) + half_c
    a = jnp.exp(log_a)
    a_s[...] = a
    w = -jnp.tanh(log_a) * (1.0 + a * a)
    u_s[...] = jnp.where(w > 0.0, w * lax.rsqrt(w), 0.0) * (i * xc)

    @pl.when(jnp.logical_and(p == 0, j == 0))
    def _():
        hfin_ref[...] = jnp.zeros_like(hfin_ref)

    @pl.when(j == 0)
    def _():
        hc_s[0:1, :] = h0_ref[0, pl.ds(p, 1), :]

    def segment_scan(reverse):
        def body(q, carry):
            t = (sl - 1 - q) if reverse else q
            rows = pl.ds(pl.multiple_of(t * nseg, nseg), nseg)
            h, pr = carry
            av = a_s[rows, :]
            h = av * h + u_s[rows, :]
            pr = av * pr
            hl_s[rows, :] = h
            p_s[rows, :] = pr
            return h, pr
        h_end, p_end = lax.fori_loop(0, sl, body, (jnp.zeros((nseg, dr), F32), jnp.ones((nseg, dr), F32)),
                                     unroll=4)
        carry = hc_s[0:1, :]
        for g in (range(nseg - 1, -1, -1) if reverse else range(nseg)):
            c_s[g:g + 1, :] = carry
            carry = h_end[g:g + 1, :] + p_end[g:g + 1, :] * carry
        hc_s[0:1, :] = carry
        return c_s[...]

    def corrected(cin):
        h = hl_s[...].reshape(sl, nseg, dr) + p_s[...].reshape(sl, nseg, dr) * cin[None]
        return h.reshape(tc, dr)

    @pl.when(p == 0)
    def _():
        hf_s[pl.ds(start, tc), :] = corrected(segment_scan(False))

    @pl.when(p == 1)
    def _():
        tot = corrected(segment_scan(True)) + hf_s[pl.ds(start, tc), :]
        out_ref[0] = jnp.dot(pmt_ref[...], tot.astype(BF16), preferred_element_type=F32).astype(BF16)

    @pl.when(j == nchunk - 1)
    def _():
        hfin_ref[0, pl.ds(p, 1), :] = hc_s[0:1, :]


def _rg_scan(xs, h0, conv_w, conv_b, wcat, bcat, lam, tc):
    b, s, dr = xs.shape
    nchunk = s // tc
    last = nchunk - 1
    sl = tc // SUBLANES
    src = (np.arange(tc) % SUBLANES) * sl + np.arange(tc) // SUBLANES
    pm = np.zeros((tc, tc), np.float32)
    pm[np.arange(tc), src] = 1.0
    chunk_buf = pltpu.VMEM((tc, dr), F32)
    full2 = lambda shape: pl.BlockSpec(shape, lambda i, p, j: (0,) * len(shape))
    return pl.pallas_call(
        functools.partial(_rg_kernel, tc=tc, nchunk=nchunk, seq=s, dr=dr),
        grid=(b, 2, nchunk),
        in_specs=[pl.BlockSpec((1, s, dr), lambda i, p, j: (i, 0, 0)),
                  pl.BlockSpec((1, SUBLANES, dr), lambda i, p, j: (i, 0, 0)),
                  full2((CONV_W, dr)), full2((1, dr)),
                  full2((2, dr, 2 * dr)), full2((2, 1, 2 * dr)), full2((2, 1, dr)),
                  full2((tc, tc)), full2((tc, tc))],
        out_specs=[pl.BlockSpec((1, tc, dr), lambda i, p, j: (i, jnp.where(p == 0, last, last - j), 0)),
                   pl.BlockSpec((1, SUBLANES, dr), lambda i, p, j: (i, 0, 0))],
        out_shape=[jax.ShapeDtypeStruct((b, s, dr), BF16),
                   jax.ShapeDtypeStruct((b, SUBLANES, dr), F32)],
        scratch_shapes=[pltpu.VMEM((s, dr), F32), pltpu.VMEM((s, dr), F32),
                        chunk_buf, chunk_buf, chunk_buf, chunk_buf,
                        pltpu.VMEM((SUBLANES, dr), F32), pltpu.VMEM((SUBLANES, dr), F32)],
        compiler_params=_cparams(("arbitrary", "arbitrary", "arbitrary")),
        name="rg_scan",
    )(xs, h0, conv_w, conv_b.reshape(1, dr), wcat, bcat, lam, jnp.asarray(pm, BF16), jnp.asarray(pm.T, BF16))


def _f1_kernel(d_ref, x_ref, y_ref):
    y_ref[0] = jnp.dot(d_ref[...], x_ref[0], preferred_element_type=F32).astype(BF16)


def _fourier_stage1(fv, d2, tl):
    b, r, n = fv.shape
    return pl.pallas_call(
        _f1_kernel,
        grid=(b, n // tl),
        in_specs=[pl.BlockSpec((2 * r, r), lambda i, l: (0, 0)),
                  pl.BlockSpec((1, r, tl), lambda i, l: (i, 0, l))],
        out_specs=pl.BlockSpec((1, 2 * r, tl), lambda i, l: (i, 0, l)),
        out_shape=jax.ShapeDtypeStruct((b, 2 * r, n), BF16),
        compiler_params=_cparams(("parallel", "arbitrary")),
        name="fourier_stage1",
    )(d2, fv)


def _f2_kernel(y_ref, e_ref, wcs_ref, g_ref, o_ref, obuf, *, kb, df):
    zr, zi = [], []
    for q in range(kb):
        yk = jnp.concatenate([y_ref[0, 0, q], y_ref[0, 1, q]], axis=0)
        z = jnp.dot(e_ref[q], yk, preferred_element_type=F32)
        zr.append(z[:GRID_W])
        zi.append(z[GRID_W:])
    zr = jnp.concatenate(zr, axis=0).astype(BF16)
    zi = jnp.concatenate(zi, axis=0).astype(BF16)
    gd = wcs_ref.shape[2]
    o = jnp.concatenate(
        [jnp.dot(jnp.concatenate([zr[:, g * gd:(g + 1) * gd], zi[:, g * gd:(g + 1) * gd]], axis=1), wcs_ref[g],
                 preferred_element_type=F32) for g in range(df // gd)], axis=1)
    on = _rms(o, g_ref[...])

    def store_o(pos, c, tile):
        o_ref[0, pos, :, c * LANES:(c + 1) * LANES] = tile.astype(BF16)
    _pitched_store(on, obuf, 0)
    _pitched_gather(obuf, kb, store_o)


def _fourier_stage2(y5, etab, wcs, g, kb):
    b, _, r, w, df = y5.shape
    return pl.pallas_call(
        functools.partial(_f2_kernel, kb=kb, df=df),
        grid=(b, r // kb),
        in_specs=[pl.BlockSpec((1, 2, kb, w, df), lambda i, k: (i, 0, k, 0, 0)),
                  pl.BlockSpec((kb, 2 * w, 2 * w), lambda i, k: (k, 0, 0)),
                  pl.BlockSpec(wcs.shape, lambda i, k: (0, 0, 0)),
                  pl.BlockSpec((1, df), lambda i, k: (0, 0))],
        out_specs=pl.BlockSpec((1, w, kb, df), lambda i, k: (i, 0, k, 0)),
        out_shape=jax.ShapeDtypeStruct((b, w, r, df), BF16),
        scratch_shapes=[pltpu.VMEM((df // LANES, kb * _seg_pitch(GRID_W), LANES), F32)],
        compiler_params=_cparams(("parallel", "arbitrary")),
        name="fourier_stage2",
    )(y5, etab, wcs, g.reshape(1, df))


def _stage_m_kernel(fn_ref, hs_ref, gg_ref, x_ref, gtm_ref, shf_ref, scf_ref, gr_ref, gffn_ref,
                    wo_ref, wr_ref, br_ref, j_ref, x1_ref, h2_ref, idx_ref, gate_ref, *, df):
    tm = x_ref.shape[1]
    hs = hs_ref[0]
    blocks = []
    for r in range(tm // GRID_W):
        blk = hs[r * GRID_W:(r + 1) * GRID_W]
        if r % 2 == 1:
            blk = jnp.dot(j_ref[...], blk, preferred_element_type=F32)
        blocks.append(blk.astype(F32))
    rg = jnp.concatenate(blocks, axis=0) * gg_ref[0].astype(F32)
    rgn = _rms(rg, gr_ref[...]).astype(BF16)
    mix = jnp.dot(fn_ref[0], wo_ref[:df, :], preferred_element_type=F32)
    mix += jnp.dot(rgn, wo_ref[df:, :], preferred_element_type=F32)
    x1 = x_ref[0] + gtm_ref[0] * mix
    x1_ref[0] = x1
    h2 = _rms(x1, gffn_ref[...] * (1.0 + scf_ref[0])) + shf_ref[0]
    h2_ref[0] = _pack_halves(h2)

    logits = _dot3_nt(wr_ref[...], h2) + br_ref[...]
    eidx = lax.broadcasted_iota(jnp.int32, logits.shape, 0)
    vals, idxs = [], []
    for _ in range(TOP_K):
        m = jnp.max(logits, axis=0, keepdims=True)
        sel = jnp.min(jnp.where(logits == m, eidx, N_EXPERTS), axis=0, keepdims=True)
        vals.append(m)
        idxs.append(sel)
        logits = jnp.where(eidx == sel, -jnp.inf, logits)
    ex = [jnp.exp(v - vals[0]) for v in vals]
    den = ex[0] + ex[1] + ex[2] + ex[3]
    for k in range(TOP_K):
        gate_ref[k:k + 1, :] = ex[k] / den
        idx_ref[k:k + 1, :] = idxs[k]


def _stage_m(fn, hs, gg, x, gt_m, sh_f, sc_f, g_out_r, g_ffn, w_out_bf, w_router_t, b_router, jmat, tm):
    b, s, d = x.shape
    df = fn.shape[2]
    dr = hs.shape[2]
    nt = s // tm
    ne = w_router_t.shape[0]
    vec = pl.BlockSpec((1, 1, d), lambda i, t: (i, 0, 0))
    half = lambda dd: pl.BlockSpec((1, tm, dd), lambda i, t: (i, t, 0))
    full = lambda shape: pl.BlockSpec(shape, lambda i, t: (0,) * len(shape))
    tok = pl.BlockSpec((TOP_K, tm), lambda i, t: (0, i * nt + t))
    return pl.pallas_call(
        functools.partial(_stage_m_kernel, df=df),
        grid=(b, nt),
        in_specs=[half(df), half(dr), half(dr), half(d), vec, vec, vec,
                  full((1, dr)), full((1, d)), full((d, d)), full((ne, d)), full((ne, 1)),
                  full((GRID_W, GRID_W))],
        out_specs=[half(d), half(d // 2), tok, tok],
        out_shape=[jax.ShapeDtypeStruct((b, s, d), F32), jax.ShapeDtypeStruct((b, s, d // 2), jnp.int32),
                   jax.ShapeDtypeStruct((TOP_K, b * s), jnp.int32),
                   jax.ShapeDtypeStruct((TOP_K, b * s), F32)],
        compiler_params=_cparams(("parallel", "arbitrary")),
        name="stage_m",
    )(fn, hs, gg, x, gt_m, sh_f, sc_f, g_out_r.reshape(1, dr), g_ffn.reshape(1, d), w_out_bf,
      w_router_t, b_router.reshape(ne, 1), jmat)


def _rank_kernel(idx_ref, tri_ref, rank_ref, cnt_ref, carry_s):
    c = pl.program_id(0)

    @pl.when(c == 0)
    def _():
        carry_s[...] = jnp.zeros_like(carry_s)

    l = tri_ref.shape[0]
    eidx = lax.broadcasted_iota(jnp.int32, (N_EXPERTS, l), 0)
    for sub in range(idx_ref.shape[1] // l):
        lanes = slice(sub * l, (sub + 1) * l)
        onehots = [eidx == idx_ref[k:k + 1, lanes] for k in range(TOP_K)]
        ohs = [jnp.where(o, 1.0, 0.0) for o in onehots]
        prefix = jnp.dot(jnp.concatenate(ohs, axis=0).astype(BF16), tri_ref[...], preferred_element_type=F32)
        for k in range(TOP_K):
            carry = carry_s[:, 0:1]
            pk = prefix[k * N_EXPERTS:(k + 1) * N_EXPERTS]
            rank = jnp.sum(jnp.where(onehots[k], pk - 1.0 + carry, 0.0), axis=0, keepdims=True)
            rank_ref[k:k + 1, lanes] = rank.astype(jnp.int32)
            carry_s[...] = carry_s[...] + jnp.sum(ohs[k], axis=1, keepdims=True)
    cnt_ref[...] = carry_s[...].astype(jnp.int32)


def _dispatch_ranks(idx, tri, tl):
    k, t = idx.shape
    return pl.pallas_call(
        _rank_kernel,
        grid=(t // tl,),
        in_specs=[pl.BlockSpec((k, tl), lambda c: (0, c)),
                  pl.BlockSpec(tri.shape, lambda c: (0, 0))],
        out_specs=[pl.BlockSpec((k, tl), lambda c: (0, c)),
                   pl.BlockSpec((N_EXPERTS, LANES), lambda c: (0, 0))],
        out_shape=[jax.ShapeDtypeStruct((k, t), jnp.int32),
                   jax.ShapeDtypeStruct((N_EXPERTS, LANES), jnp.int32)],
        scratch_shapes=[pltpu.VMEM((N_EXPERTS, LANES), F32)],
        compiler_params=_cparams(("arbitrary",)),
        name="dispatch_ranks",
    )(idx, tri)


def _moe_kernel(be_ref, bv_ref, nu_ref, x_ref, wgu_ref, bgu_ref, wd_ref, bd_ref, o_ref, wgu_s, wd_s, *, dff):
    i = pl.program_id(0)
    h = x_ref.shape[1]

    @pl.when(jnp.logical_or(i == 0, be_ref[i] != be_ref[jnp.maximum(i - 1, 0)]))
    def _():
        wgu_s[...] = wgu_ref[0].astype(BF16)
        wd_s[...] = wd_ref[0].astype(BF16)

    @pl.when(i < nu_ref[0])
    def _():
        rows = lax.broadcasted_iota(jnp.int32, x_ref.shape, 0)
        xw = jnp.where(rows < bv_ref[i], x_ref[...], 0)
        xa, xb = _unpack_halves(xw)
        xa = xa.astype(BF16)
        xb = xb.astype(BF16)
        acc = None
        for c in range(dff // MOE_FF_CHUNK):
            gs = slice(c * MOE_FF_CHUNK, (c + 1) * MOE_FF_CHUNK)
            us = slice(dff + c * MOE_FF_CHUNK, dff + (c + 1) * MOE_FF_CHUNK)
            g = jnp.dot(xa, wgu_s[:h, gs], preferred_element_type=F32)
            g += jnp.dot(xb, wgu_s[h:, gs], preferred_element_type=F32)
            u = jnp.dot(xa, wgu_s[:h, us], preferred_element_type=F32)
            u += jnp.dot(xb, wgu_s[h:, us], preferred_element_type=F32)
            gt = jnp.minimum(g + bgu_ref[0, :, gs], SWIGLU_LIMIT)
            up = jnp.clip(u + bgu_ref[0, :, us], -SWIGLU_LIMIT, SWIGLU_LIMIT)
            act = (up + 1.0) * (gt * _sigmoid(SWIGLU_ALPHA * gt))
            part = jnp.dot(act.astype(BF16), wd_s[gs, :], preferred_element_type=F32)
            acc = part if acc is None else acc + part
        o_ref[...] = _pack_halves(acc + bd_ref[0])


def _moe_experts(blk_expert, blk_valid, n_used, xs, wgu, bgu, wd, bd, tmm):
    cap, h = xs.shape
    ne, d, dff2 = wgu.shape
    dff = dff2 // 2
    row_blk = lambda i, be, bv, nu: (jnp.maximum(jnp.minimum(i, nu[0] - 1), 0), 0)
    wsel = lambda i, be, bv, nu: (be[i], 0, 0)
    grid_spec = pltpu.PrefetchScalarGridSpec(
        num_scalar_prefetch=3,
        grid=(cap // tmm,),
        in_specs=[pl.BlockSpec((tmm, h), row_blk),
                  pl.BlockSpec((1, d, dff2), wsel),
                  pl.BlockSpec((1, 1, dff2), wsel),
                  pl.BlockSpec((1, dff, d), wsel),
                  pl.BlockSpec((1, 1, d), wsel)],
        out_specs=pl.BlockSpec((tmm, h), row_blk),
        scratch_shapes=[pltpu.VMEM((d, dff2), BF16), pltpu.VMEM((dff, d), BF16)],
    )
    return pl.pallas_call(
        functools.partial(_moe_kernel, dff=dff),
        grid_spec=grid_spec,
        out_shape=jax.ShapeDtypeStruct((cap, h), jnp.int32),
        compiler_params=_cparams(("arbitrary",)),
        name="moe_experts",
    )(blk_expert, blk_valid, n_used, xs, wgu, bgu.reshape(ne, 1, dff2), wd, bd.reshape(ne, 1, d))


SC_CHUNK = 64
SC_ID_CHUNK = 128
SC_ID_STREAMS = 4


def _sc_workers():
    info = plsc.get_sparse_core_info()
    return info.num_cores, info.num_subcores


def _sc_scatter_into(out_ref, rows, dest):
    n, w = rows.shape
    nc, ns = _sc_workers()
    per_w = n // (nc * ns)
    assert per_w % SC_CHUNK == 0
    mesh = plsc.VectorSubcoreMesh(core_axis_name="c", subcore_axis_name="s")

    @functools.partial(
        pl.kernel, mesh=mesh, out_type=(),
        scratch_types=[pltpu.VMEM((SC_CHUNK,), jnp.int32),
                       pltpu.VMEM((SC_CHUNK, w), jnp.int32),
                       pltpu.SemaphoreType.DMA],
    )
    def scatter_rows(rows_hbm, dest_hbm, out_hbm, idx_v, rows_v, sem):
        base = (lax.axis_index("s") * nc + lax.axis_index("c")) * per_w

        @pl.loop(0, per_w // SC_CHUNK)
        def _(j):
            off = pl.multiple_of(base + j * SC_CHUNK, SC_CHUNK)
            pltpu.sync_copy(rows_hbm.at[pl.ds(off, SC_CHUNK)], rows_v)
            pltpu.sync_copy(dest_hbm.at[pl.ds(off, SC_CHUNK)], idx_v)
            pltpu.async_copy(rows_v, out_hbm.at[idx_v], sem).wait()

    scatter_rows(rows, dest, out_ref)


def _sc_scatter_ids(dest, out_rows):
    n = dest.shape[0]
    info = plsc.get_sparse_core_info()
    nc, ns, nl = info.num_cores, info.num_subcores, info.num_lanes
    per_w = n // (nc * ns)
    chunk = SC_ID_CHUNK
    nfly = SC_ID_STREAMS
    assert per_w % (nfly * chunk) == 0
    mesh = plsc.VectorSubcoreMesh(core_axis_name="c", subcore_axis_name="s")
    idx_t = pltpu.VMEM((chunk,), jnp.int32)
    rows_t = pltpu.VMEM((chunk, LANES), jnp.int32)

    @functools.partial(
        pl.kernel, mesh=mesh,
        out_type=jax.ShapeDtypeStruct((out_rows, LANES), jnp.int32),
        scratch_types=[idx_t] * nfly + [rows_t] * nfly + [pltpu.SemaphoreType.DMA] * nfly,
    )
    def scatter_ids(dest_hbm, out_hbm, *scratch):
        idx_bufs, row_bufs, sems = scratch[:nfly], scratch[nfly:2 * nfly], scratch[2 * nfly:]
        base = (lax.axis_index("s") * nc + lax.axis_index("c")) * per_w

        def start(off, idx_v, rows_v, sem):
            for r in range(chunk):
                rows_v[r, pl.ds(0, nl)] = jnp.zeros((nl,), jnp.int32) + (off + r)
            pltpu.sync_copy(dest_hbm.at[pl.ds(off, chunk)], idx_v)
            return pltpu.async_copy(rows_v, out_hbm.at[idx_v], sem)

        @pl.loop(0, per_w // (nfly * chunk))
        def _(j):
            off = pl.multiple_of(base + j * (nfly * chunk), nfly * chunk)
            copies = [start(off + q * chunk, idx_bufs[q], row_bufs[q], sems[q]) for q in range(nfly)]
            for cp in copies:
                cp.wait()

    return scatter_ids(dest)


def _sc_gather(table, idx):
    n = idx.shape[0]
    w = table.shape[1]
    nc, ns = _sc_workers()
    per_w = n // (nc * ns)
    assert per_w % SC_CHUNK == 0
    mesh = plsc.VectorSubcoreMesh(core_axis_name="c", subcore_axis_name="s")

    @functools.partial(
        pl.kernel, mesh=mesh,
        out_type=jax.ShapeDtypeStruct((n, w), jnp.int32),
        scratch_types=[pltpu.VMEM((SC_CHUNK,), jnp.int32),
                       pltpu.VMEM((SC_CHUNK, w), jnp.int32),
                       pltpu.SemaphoreType.DMA],
    )
    def gather_rows(table_hbm, idx_hbm, out_hbm, idx_v, rows_v, sem):
        base = (lax.axis_index("s") * nc + lax.axis_index("c")) * per_w

        @pl.loop(0, per_w // SC_CHUNK)
        def _(j):
            off = pl.multiple_of(base + j * SC_CHUNK, SC_CHUNK)
            pltpu.sync_copy(idx_hbm.at[pl.ds(off, SC_CHUNK)], idx_v)
            pltpu.async_copy(table_hbm.at[idx_v], rows_v, sem).wait()
            pltpu.sync_copy(rows_v, out_hbm.at[pl.ds(off, SC_CHUNK)])

    return gather_rows(table, idx)


def _combine_kernel(x1_ref, y_ref, gate_ref, gtf_ref, g_ref, o_ref):
    h = y_ref.shape[3]
    gates = gate_ref[0]
    moe_a = moe_b = None
    for k in range(TOP_K):
        ya, yb = _unpack_halves(y_ref[k, 0])
        gk = gates[:, k:k + 1]
        moe_a = gk * ya if k == 0 else moe_a + gk * ya
        moe_b = gk * yb if k == 0 else moe_b + gk * yb
    za = x1_ref[0, :, :h] + gtf_ref[0, :, :h] * moe_a
    zb = x1_ref[0, :, h:] + gtf_ref[0, :, h:] * moe_b
    ms = (jnp.sum(za * za, axis=-1, keepdims=True) + jnp.sum(zb * zb, axis=-1, keepdims=True)) / (2 * h)
    inv = lax.rsqrt(ms + EPS)
    o_ref[0, :, :h] = za * inv * g_ref[:, :h]
    o_ref[0, :, h:] = zb * inv * g_ref[:, h:]


def _combine(x1, yk, gates_t, gt_f, g_final, tm):
    b, s, d = x1.shape
    h = yk.shape[3]
    return pl.pallas_call(
        _combine_kernel,
        grid=(b, s // tm),
        in_specs=[pl.BlockSpec((1, tm, d), lambda i, t: (i, t, 0)),
                  pl.BlockSpec((TOP_K, 1, tm, h), lambda i, t: (0, i, t, 0)),
                  pl.BlockSpec((1, tm, TOP_K), lambda i, t: (i, t, 0)),
                  pl.BlockSpec((1, 1, d), lambda i, t: (i, 0, 0)),
                  pl.BlockSpec((1, d), lambda i, t: (0, 0))],
        out_specs=pl.BlockSpec((1, tm, d), lambda i, t: (i, t, 0)),
        out_shape=jax.ShapeDtypeStruct((b, s, d), F32),
        compiler_params=_cparams(("parallel", "arbitrary")),
        name="combine",
    )(x1, yk, gates_t, gt_f, g_final.reshape(1, d))


def _dft_tables(rows, gd):
    seq = rows * GRID_W
    n = np.arange(rows)
    ang1 = 2.0 * np.pi * np.outer(n, n) / rows
    d2 = np.concatenate([np.cos(ang1), -np.sin(ang1)], axis=0)
    k1 = np.arange(rows)[:, None, None]
    k2 = np.arange(GRID_W)[None, :, None]
    n2 = np.arange(GRID_W)[None, None, :]
    ang2 = 2.0 * np.pi * ((n2 * (k1 + rows * k2)) % seq) / seq
    ec, es = np.cos(ang2), np.sin(ang2)
    etab = np.concatenate([np.concatenate([ec, es], axis=2),
                           np.concatenate([-es, ec], axis=2)], axis=1)
    c = np.arange(gd)
    angc = 2.0 * np.pi * np.outer(c, c) / gd
    scale = 1.0 / np.sqrt(seq * gd)
    return (jnp.asarray(d2, BF16), jnp.asarray(etab, BF16),
            jnp.asarray(np.cos(angc) * scale, F32), jnp.asarray(np.sin(angc) * scale, F32))


def _block_diag(w):
    h, i, o = w.shape
    eye = jnp.eye(h, dtype=w.dtype)
    return (eye[:, None, :, None] * w[:, :, None, :]).reshape(h * i, h * o)


def kernel(x, c, ctx, c_ctx, w_mod, b_mod, g_norm_mix, g_norm_ffn, w_in, w_fourier, conv_w, conv_b,
           rg_w_a, rg_b_a, rg_w_x, rg_b_x, rg_lam, g_out_fourier, g_out_rg, w_out, w_router, b_router,
           w_gate_up, b_gate_up, w_down, b_down, g_final):
    assert w_mod.shape[0] == 1, "single-layer stack only"
    b, s, d = x.shape
    df = w_fourier.shape[1] * w_fourier.shape[2]
    dr = conv_w.shape[2]
    gd = w_fourier.shape[2]
    rows = s // GRID_W
    t = b * s
    ne = w_router.shape[2]

    mrows = -(-(b + 1) // SUBLANES) * SUBLANES
    cond = jnp.zeros((mrows, d), F32).at[:b].set(c).at[b].set(c_ctx)
    mod = _adaln(cond, w_mod[0], b_mod[0])
    sh_m, sc_m, gt_m, sh_f, sc_f, gt_f = [mod[:b, k * d:(k + 1) * d].reshape(b, 1, d) for k in range(N_MOD)]
    csh_m = mod[b:b + 1, 0:d].reshape(1, 1, d)
    csc_m = mod[b:b + 1, d:2 * d].reshape(1, 1, d)

    tm = min(s, TOKEN_TILE)
    d2, etab, cmat, smat = _dft_tables(rows, gd)
    jmat = jnp.asarray(np.eye(GRID_W)[::-1].copy(), BF16)

    w_in_bf = w_in[0].astype(BF16)
    f, xs, gg = _stage_b(x, sh_m, sc_m, g_norm_mix[0], w_in_bf, jmat, df, dr, tm=tm)
    xr_ctx = _stage_b_ctx(ctx, csh_m, csc_m, g_norm_mix[0], w_in_bf, df, dr)

    wcat = jnp.stack([jnp.concatenate([_block_diag(rg_w_a[0, dd]), _block_diag(rg_w_x[0, dd])], axis=1)
                      for dd in range(2)]).astype(BF16)
    bcat = jnp.concatenate([rg_b_a[0], rg_b_x[0]], axis=1).reshape(2, 1, 2 * dr)
    lam = rg_lam[0].reshape(2, 1, dr)
    h0 = jnp.zeros((b, SUBLANES, dr), F32)
    _, hfin_ctx = _rg_scan(xr_ctx, h0, conv_w[0], conv_b[0], wcat, bcat, lam, tc=ctx.shape[1])
    hs, _ = _rg_scan(xs, hfin_ctx, conv_w[0], conv_b[0], wcat, bcat, lam, tc=min(s, SCAN_CHUNK))

    cw, sw = _fold_fourier(cmat, smat, w_fourier[0])
    wcs = jnp.concatenate([cw, sw], axis=1).astype(BF16)
    y = _fourier_stage1(f, d2, tl=GRID_W * df)
    fn = _fourier_stage2(y.reshape(b, 2, rows, GRID_W, df), etab, wcs, g_out_fourier[0],
                         kb=min(rows, 32))
    fn = fn.reshape(b, s, df)

    x1, h2, idx, gates = _stage_m(fn, hs, gg, x, gt_m, sh_f, sc_f, g_out_rg[0], g_norm_ffn[0],
                                  w_out[0].astype(BF16), w_router[0].T, b_router[0], jmat, tm=tm)

    tl = min(t, RANK_TILE)
    tri = jnp.asarray(np.triu(np.ones((tl, tl))), BF16)
    rank, cnt = _dispatch_ranks(idx, tri, min(t, RANK_STEP))
    counts = cnt[:, 0]
    tmm = MOE_ROW_TILE
    padded = (counts + tmm - 1) // tmm * tmm
    pad_end = jnp.cumsum(padded)
    pad_start = pad_end - padded
    eids = jnp.arange(ne, dtype=jnp.int32)
    dest = rank + jnp.sum(jnp.where(idx[:, :, None] == eids, pad_start, 0), axis=-1)
    n_blocks = -(-(t * TOP_K) // tmm) + ne
    cap = n_blocks * tmm
    n_used = (pad_end[-1] // tmm).astype(jnp.int32).reshape(1)
    blk_start = jnp.arange(n_blocks, dtype=jnp.int32) * tmm
    blk_expert = jnp.sum(blk_start[:, None] >= pad_end[None, :], axis=1).astype(jnp.int32)
    last_expert = jnp.sum(pad_end[-1] - tmm >= pad_end).astype(jnp.int32)
    blk_expert = jnp.minimum(blk_expert, last_expert)
    sel = blk_expert[:, None] == eids
    blk_first = jnp.sum(jnp.where(sel, pad_start, 0), axis=1)
    blk_count = jnp.sum(jnp.where(sel, counts, 0), axis=1)
    blk_valid = jnp.clip(blk_count - (blk_start - blk_first), 0, tmm).astype(jnp.int32)

    na = TOP_K * t
    inv = _sc_scatter_ids(dest.reshape(-1), cap)[:, 0].reshape(n_blocks, tmm)
    live = jnp.arange(tmm, dtype=jnp.int32)[None, :] < blk_valid[:, None]
    spread = jnp.arange(cap, dtype=jnp.int32).reshape(n_blocks, tmm) % t
    src_tok = jnp.where(live, inv % t, spread).reshape(-1)
    dst_row = jnp.where(live, inv, na + spread).reshape(-1)

    h2_rows = h2.reshape(t, d // 2)
    y_all = jax.empty_ref(jax.ShapeDtypeStruct(((TOP_K + 1) * t, d // 2), jnp.int32))
    unit = n_blocks // sum(MOE_PIPE)
    assert unit * sum(MOE_PIPE) == n_blocks
    blk0 = 0
    for parts in MOE_PIPE:
        nq = parts * unit
        blocks = slice(blk0, blk0 + nq)
        rows = slice(blk0 * tmm, (blk0 + nq) * tmm)
        x_q = _sc_gather(h2_rows, src_tok[rows])
        nu_q = jnp.clip(n_used - blk0, 0, nq).astype(jnp.int32)
        blk0 += nq
        y_q = _moe_experts(blk_expert[blocks], blk_valid[blocks], nu_q, x_q,
                           w_gate_up[0], b_gate_up[0], w_down[0], b_down[0], tmm)
        _sc_scatter_into(y_all, y_q, dst_row[rows])
    yk = y_all[...].reshape(TOP_K + 1, b, s, d // 2)
    return _combine(x1, yk, gates.T.reshape(b, s, TOP_K), gt_f, g_final, tm)
```

```python
import functools

import numpy as np
import jax
import jax.numpy as jnp
from jax import lax
from jax.experimental import pallas as pl
from jax.experimental.pallas import tpu as pltpu
from jax.experimental.pallas import tpu_sc as plsc

GRID_W = 64
CONV_W = 4
CONV_PAD_LO = 2
RG_C = 8.0
N_EXPERTS = 32
TOP_K = 4
SWIGLU_LIMIT = 7.0
SWIGLU_ALPHA = 1.702
N_MOD = 6
EPS = 1e-6

LANES = 128
SUBLANES = 8
VMEM_LIMIT_BYTES = 56 * 1024 * 1024
TOKEN_TILE = 1024
RANK_TILE = 512
RANK_STEP = 4096
SCAN_CHUNK = 512
MOE_ROW_TILE = 512
MOE_FF_CHUNK = 512
MOE_PIPE = (1, 4, 10, 6, 3)

F32 = jnp.float32
BF16 = jnp.bfloat16


def _cparams(sem):
    return pltpu.CompilerParams(dimension_semantics=sem, vmem_limit_bytes=VMEM_LIMIT_BYTES)


def _split_bf16(a):
    hi = a.astype(BF16)
    lo = (a - hi.astype(F32)).astype(BF16)
    return hi, lo


def _dot3(a, b):
    ah, al = _split_bf16(a)
    bh, bl = _split_bf16(b)
    out = jnp.dot(ah, bh, preferred_element_type=F32)
    out += jnp.dot(ah, bl, preferred_element_type=F32)
    out += jnp.dot(al, bh, preferred_element_type=F32)
    return out


def _dot3_nt(a, b):
    dn = (((1,), (1,)), ((), ()))
    m = a.shape[0]
    ah, al = _split_bf16(a)
    bh, bl = _split_bf16(b)
    both = lax.dot_general(jnp.concatenate([ah, al], axis=0), bh, dn, preferred_element_type=F32)
    return both[:m] + both[m:] + lax.dot_general(ah, bl, dn, preferred_element_type=F32)


def _gelu_tanh(x):
    return 0.5 * x * (1.0 + jnp.tanh(0.7978845608028654 * (x + 0.044715 * (x * x * x))))


def _rms(x, g):
    return x * lax.rsqrt(jnp.mean(x * x, axis=-1, keepdims=True) + EPS) * g


def _pack_halves(v):
    h = v.shape[1] // 2
    hi = lax.bitcast_convert_type(v[:, :h].astype(BF16).astype(F32), jnp.uint32)
    lo = lax.bitcast_convert_type(v[:, h:].astype(BF16).astype(F32), jnp.uint32)
    return lax.bitcast_convert_type(hi | (lo >> 16), jnp.int32)


def _unpack_halves(w):
    u = lax.bitcast_convert_type(w, jnp.uint32)
    hi = lax.bitcast_convert_type(u & jnp.uint32(0xFFFF0000), F32)
    lo = lax.bitcast_convert_type(u << 16, F32)
    return hi, lo


def _adaln_kernel(c_ref, w_ref, b_ref, o_ref):
    s = c_ref[...]
    s = s * jax.nn.sigmoid(s)
    o_ref[...] = _dot3(s, w_ref[...]) + b_ref[...]


def _adaln(cond, w_mod, b_mod):
    m, d = cond.shape
    n = w_mod.shape[1]
    tn = n // N_MOD
    return pl.pallas_call(
        _adaln_kernel,
        grid=(n // tn,),
        in_specs=[pl.BlockSpec((m, d), lambda i: (0, 0)),
                  pl.BlockSpec((d, tn), lambda i: (0, i)),
                  pl.BlockSpec((1, tn), lambda i: (0, i))],
        out_specs=pl.BlockSpec((m, tn), lambda i: (0, i)),
        out_shape=jax.ShapeDtypeStruct((m, n), F32),
        compiler_params=_cparams(("arbitrary",)),
        name="adaln",
    )(cond, w_mod, b_mod.reshape(1, n))


def _fold_kernel(c_ref, s_ref, w_ref, cw_ref, sw_ref):
    w = w_ref[0]
    cw_ref[0] = _dot3(c_ref[...], w)
    sw_ref[0] = _dot3(s_ref[...], w)


def _fold_fourier(cmat, smat, w_f):
    g, gd, _ = w_f.shape
    spec_m = pl.BlockSpec((gd, gd), lambda i: (0, 0))
    spec_w = pl.BlockSpec((1, gd, gd), lambda i: (i, 0, 0))
    return pl.pallas_call(
        _fold_kernel,
        grid=(g,),
        in_specs=[spec_m, spec_m, spec_w],
        out_specs=[spec_w, spec_w],
        out_shape=[jax.ShapeDtypeStruct((g, gd, gd), F32)] * 2,
        compiler_params=_cparams(("arbitrary",)),
        name="fold_fourier",
    )(cmat, smat, w_f)


def _seg_pitch(seg_len):
    n8 = seg_len // SUBLANES
    return SUBLANES * (n8 + 1 - n8 % 2)


def _pitched_store(v, buf, blk0):
    pitch = _seg_pitch(GRID_W)
    for r in range(v.shape[0] // GRID_W):
        for c in range(v.shape[1] // LANES):
            buf[c, (blk0 + r) * pitch:(blk0 + r) * pitch + GRID_W, :] = (
                v[r * GRID_W:(r + 1) * GRID_W, c * LANES:(c + 1) * LANES])


def _pitched_gather(buf, nb, store):
    pitch = _seg_pitch(GRID_W)
    for pos in range(GRID_W):
        for c in range(buf.shape[0]):
            store(pos, c, buf[c, pl.ds(pos, nb, stride=pitch), :])


def _stage_b_kernel(x_ref, sh_ref, sc_ref, g_ref, w_ref, j_ref, f_ref, xs_ref, gg_ref, fbuf, *, df, dr):
    tm = x_ref.shape[1]
    h = _rms(x_ref[0], g_ref[...] * (1.0 + sc_ref[0])) + sh_ref[0]
    hb = h.astype(BF16)
    gr = jnp.dot(hb, w_ref[:, df + dr:], preferred_element_type=F32)
    gg_ref[0] = _gelu_tanh(gr).astype(BF16)
    _pitched_store(jnp.dot(hb, w_ref[:, :df], preferred_element_type=F32), fbuf, 0)

    def store_f(pos, c, tile):
        f_ref[0, :, pos * df + c * LANES:pos * df + (c + 1) * LANES] = tile.astype(BF16)
    _pitched_gather(fbuf, tm // GRID_W, store_f)
    xr = jnp.dot(hb, w_ref[:, df:df + dr], preferred_element_type=F32).astype(BF16)
    for r in range(tm // GRID_W):
        blk = xr[r * GRID_W:(r + 1) * GRID_W]
        if r % 2 == 1:
            blk = jnp.dot(j_ref[...], blk, preferred_element_type=F32).astype(BF16)
        xs_ref[0, r * GRID_W:(r + 1) * GRID_W, :] = blk


def _stage_b(x, shift, scale, g, w_in_bf, jmat, df, dr, tm):
    b, s, d = x.shape
    n = w_in_bf.shape[1]
    vec = pl.BlockSpec((1, 1, d), lambda i, t: (i, 0, 0))
    out = pl.BlockSpec((1, tm, df), lambda i, t: (i, t, 0))
    nb = tm // GRID_W
    tok = jax.ShapeDtypeStruct((b, s, df), BF16)
    return pl.pallas_call(
        functools.partial(_stage_b_kernel, df=df, dr=dr),
        grid=(b, s // tm),
        in_specs=[pl.BlockSpec((1, tm, d), lambda i, t: (i, t, 0)), vec, vec,
                  pl.BlockSpec((1, d), lambda i, t: (0, 0)),
                  pl.BlockSpec((d, n), lambda i, t: (0, 0)),
                  pl.BlockSpec((GRID_W, GRID_W), lambda i, t: (0, 0))],
        out_specs=[pl.BlockSpec((1, nb, GRID_W * df), lambda i, t: (i, t, 0)), out, out],
        out_shape=[jax.ShapeDtypeStruct((b, s // GRID_W, GRID_W * df), BF16), tok, tok],
        scratch_shapes=[pltpu.VMEM((df // LANES, nb * _seg_pitch(GRID_W), LANES), F32)],
        compiler_params=_cparams(("parallel", "arbitrary")),
        name="stage_b",
    )(x, shift, scale, g.reshape(1, d), w_in_bf, jmat)


def _stage_b_ctx_kernel(x_ref, sh_ref, sc_ref, g_ref, w_ref, xr_ref):
    h = _rms(x_ref[0], g_ref[...] * (1.0 + sc_ref[0])) + sh_ref[0]
    xr_ref[0] = jnp.dot(h.astype(BF16), w_ref[...], preferred_element_type=F32).astype(BF16)


def _stage_b_ctx(ctx, shift, scale, g, w_in_bf, df, dr):
    b, s, d = ctx.shape
    assert df % dr == 0
    vec = pl.BlockSpec((1, 1, d), lambda i: (0, 0, 0))
    return pl.pallas_call(
        _stage_b_ctx_kernel,
        grid=(b,),
        in_specs=[pl.BlockSpec((1, s, d), lambda i: (i, 0, 0)), vec, vec,
                  pl.BlockSpec((1, d), lambda i: (0, 0)),
                  pl.BlockSpec((d, dr), lambda i: (0, df // dr))],
        out_specs=pl.BlockSpec((1, s, dr), lambda i: (i, 0, 0)),
        out_shape=jax.ShapeDtypeStruct((b, s, dr), BF16),
        compiler_params=_cparams(("arbitrary",)),
        name="stage_b_ctx",
    )(ctx, shift, scale, g.reshape(1, d), w_in_bf)


HALO = 16


def _sigmoid(x):
    return 0.5 * jnp.tanh(0.5 * x) + 0.5


def _rg_kernel(xs_ref, h0_ref, cw_ref, cb_ref, w_ref, b_ref, lam_ref, pm_ref, pmt_ref, out_ref, hfin_ref,
               hf_s, xc_s, a_s, u_s, hl_s, p_s, c_s, hc_s, *, tc, nchunk, seq, dr):
    p = pl.program_id(1)
    j = pl.program_id(2)
    cidx = jnp.where(p == 0, j, nchunk - 1 - j)
    start = pl.multiple_of(cidx * tc, tc)
    nseg = SUBLANES
    sl = tc // nseg
    sub = lax.broadcasted_iota(jnp.int32, (nseg, dr), 0)

    @pl.when(p == 0)
    def _():
        xp = jnp.dot(pm_ref[...], xs_ref[0, pl.ds(start, tc), :], preferred_element_type=F32)
        pstart = pl.multiple_of(jnp.maximum(start - HALO, 0), HALO)
        nstart = pl.multiple_of(jnp.minimum(start + tc, seq - HALO), HALO)
        prev = xs_ref[0, pl.ds(pstart, HALO), :].astype(F32)
        nxt = xs_ref[0, pl.ds(nstart, HALO), :].astype(F32)
        prev = jnp.where(cidx > 0, prev, 0.0)
        nxt = jnp.where(cidx < nchunk - 1, nxt, 0.0)
        tm2 = jnp.where(sub == 0, prev[HALO - 2:HALO - 1], pltpu.roll(xp[(sl - 2) * nseg:(sl - 1) * nseg], 1, 0))
        tm1 = jnp.where(sub == 0, prev[HALO - 1:HALO], pltpu.roll(xp[(sl - 1) * nseg:sl * nseg], 1, 0))
        tp1 = jnp.where(sub == nseg - 1, nxt[0:1], pltpu.roll(xp[0:nseg], nseg - 1, 0))
        ext = jnp.concatenate([tm2, tm1, xp, tp1], axis=0)
        xc = cb_ref[...] + cw_ref[0:1, :] * ext[0:tc]
        for k in range(1, CONV_W):
            xc = xc + cw_ref[k:k + 1, :] * ext[k * nseg:k * nseg + tc]
        xc_s[pl.ds(start, tc), :] = xc

    xc = xc_s[pl.ds(start, tc), :]
    gates = jnp.dot(xc.astype(BF16), w_ref[p], preferred_element_type=F32) + b_ref[p]
    i = _sigmoid(gates[:, dr:])
    half_c = (-0.5 * RG_C) * jax.nn.softplus(-lam_ref[p])
    log_a = half_c * jnp.tanh(0.5 * gates[:, :dr]) + half_c
    a = jnp.exp(log_a)
    a_s[...] = a
    w = -jnp.tanh(log_a) * (1.0 + a * a)
    u_s[...] = jnp.where(w > 0.0, w * lax.rsqrt(w), 0.0) * (i * xc)

    @pl.when(jnp.logical_and(p == 0, j == 0))
    def _():
        hfin_ref[...] = jnp.zeros_like(hfin_ref)

    @pl.when(j == 0)
    def _():
        hc_s[0:1, :] = h0_ref[0, pl.ds(p, 1), :]

    def segment_scan(reverse):
        def body(q, carry):
            t = (sl - 1 - q) if reverse else q
            rows = pl.ds(pl.multiple_of(t * nseg, nseg), nseg)
            h, pr = carry
            av = a_s[rows, :]
            h = av * h + u_s[rows, :]
            pr = av * pr
            hl_s[rows, :] = h
            p_s[rows, :] = pr
            return h, pr
        h_end, p_end = lax.fori_loop(0, sl, body, (jnp.zeros((nseg, dr), F32), jnp.ones((nseg, dr), F32)),
                                     unroll=4)
        carry = hc_s[0:1, :]
        for g in (range(nseg - 1, -1, -1) if reverse else range(nseg)):
            c_s[g:g + 1, :] = carry
            carry = h_end[g:g + 1, :] + p_end[g:g + 1, :] * carry
        hc_s[0:1, :] = carry
        return c_s[...]

    def corrected(cin):
        h = hl_s[...].reshape(sl, nseg, dr) + p_s[...].reshape(sl, nseg, dr) * cin[None]
        return h.reshape(tc, dr)

    @pl.when(p == 0)
    def _():
        hf_s[pl.ds(start, tc), :] = corrected(segment_scan(False))

    @pl.when(p == 1)
    def _():
        tot = corrected(segment_scan(True)) + hf_s[pl.ds(start, tc), :]
        out_ref[0] = jnp.dot(pmt_ref[...], tot.astype(BF16), preferred_element_type=F32).astype(BF16)

    @pl.when(j == nchunk - 1)
    def _():
        hfin_ref[0, pl.ds(p, 1), :] = hc_s[0:1, :]


def _rg_scan(xs, h0, conv_w, conv_b, wcat, bcat, lam, tc):
    b, s, dr = xs.shape
    nchunk = s // tc
    last = nchunk - 1
    sl = tc // SUBLANES
    src = (np.arange(tc) % SUBLANES) * sl + np.arange(tc) // SUBLANES
    pm = np.zeros((tc, tc), np.float32)
    pm[np.arange(tc), src] = 1.0
    chunk_buf = pltpu.VMEM((tc, dr), F32)
    full2 = lambda shape: pl.BlockSpec(shape, lambda i, p, j: (0,) * len(shape))
    return pl.pallas_call(
        functools.partial(_rg_kernel, tc=tc, nchunk=nchunk, seq=s, dr=dr),
        grid=(b, 2, nchunk),
        in_specs=[pl.BlockSpec((1, s, dr), lambda i, p, j: (i, 0, 0)),
                  pl.BlockSpec((1, SUBLANES, dr), lambda i, p, j: (i, 0, 0)),
                  full2((CONV_W, dr)), full2((1, dr)),
                  full2((2, dr, 2 * dr)), full2((2, 1, 2 * dr)), full2((2, 1, dr)),
                  full2((tc, tc)), full2((tc, tc))],
        out_specs=[pl.BlockSpec((1, tc, dr), lambda i, p, j: (i, jnp.where(p == 0, last, last - j), 0)),
                   pl.BlockSpec((1, SUBLANES, dr), lambda i, p, j: (i, 0, 0))],
        out_shape=[jax.ShapeDtypeStruct((b, s, dr), BF16),
                   jax.ShapeDtypeStruct((b, SUBLANES, dr), F32)],
        scratch_shapes=[pltpu.VMEM((s, dr), F32), pltpu.VMEM((s, dr), F32),
                        chunk_buf, chunk_buf, chunk_buf, chunk_buf,
                        pltpu.VMEM((SUBLANES, dr), F32), pltpu.VMEM((SUBLANES, dr), F32)],
        compiler_params=_cparams(("arbitrary", "arbitrary", "arbitrary")),
        name="rg_scan",
    )(xs, h0, conv_w, conv_b.reshape(1, dr), wcat, bcat, lam, jnp.asarray(pm, BF16), jnp.asarray(pm.T, BF16))


def _f1_kernel(d_ref, x_ref, y_ref):
    y_ref[0] = jnp.dot(d_ref[...], x_ref[0], preferred_element_type=F32).astype(BF16)


def _fourier_stage1(fv, d2, tl):
    b, r, n = fv.shape
    return pl.pallas_call(
        _f1_kernel,
        grid=(b, n // tl),
        in_specs=[pl.BlockSpec((2 * r, r), lambda i, l: (0, 0)),
                  pl.BlockSpec((1, r, tl), lambda i, l: (i, 0, l))],
        out_specs=pl.BlockSpec((1, 2 * r, tl), lambda i, l: (i, 0, l)),
        out_shape=jax.ShapeDtypeStruct((b, 2 * r, n), BF16),
        compiler_params=_cparams(("parallel", "arbitrary")),
        name="fourier_stage1",
    )(d2, fv)


def _f2_kernel(y_ref, e_ref, wcs_ref, g_ref, o_ref, obuf, *, kb, df):
    zr, zi = [], []
    for q in range(kb):
        yk = jnp.concatenate([y_ref[0, 0, q], y_ref[0, 1, q]], axis=0)
        z = jnp.dot(e_ref[q], yk, preferred_element_type=F32)
        zr.append(z[:GRID_W])
        zi.append(z[GRID_W:])
    zr = jnp.concatenate(zr, axis=0).astype(BF16)
    zi = jnp.concatenate(zi, axis=0).astype(BF16)
    gd = wcs_ref.shape[2]
    o = jnp.concatenate(
        [jnp.dot(jnp.concatenate([zr[:, g * gd:(g + 1) * gd], zi[:, g * gd:(g + 1) * gd]], axis=1), wcs_ref[g],
                 preferred_element_type=F32) for g in range(df // gd)], axis=1)
    on = _rms(o, g_ref[...])

    def store_o(pos, c, tile):
        o_ref[0, pos, :, c * LANES:(c + 1) * LANES] = tile.astype(BF16)
    _pitched_store(on, obuf, 0)
    _pitched_gather(obuf, kb, store_o)


def _fourier_stage2(y5, etab, wcs, g, kb):
    b, _, r, w, df = y5.shape
    return pl.pallas_call(
        functools.partial(_f2_kernel, kb=kb, df=df),
        grid=(b, r // kb),
        in_specs=[pl.BlockSpec((1, 2, kb, w, df), lambda i, k: (i, 0, k, 0, 0)),
                  pl.BlockSpec((kb, 2 * w, 2 * w), lambda i, k: (k, 0, 0)),
                  pl.BlockSpec(wcs.shape, lambda i, k: (0, 0, 0)),
                  pl.BlockSpec((1, df), lambda i, k: (0, 0))],
        out_specs=pl.BlockSpec((1, w, kb, df), lambda i, k: (i, 0, k, 0)),
        out_shape=jax.ShapeDtypeStruct((b, w, r, df), BF16),
        scratch_shapes=[pltpu.VMEM((df // LANES, kb * _seg_pitch(GRID_W), LANES), F32)],
        compiler_params=_cparams(("parallel", "arbitrary")),
        name="fourier_stage2",
    )(y5, etab, wcs, g.reshape(1, df))


def _stage_m_kernel(fn_ref, hs_ref, gg_ref, x_ref, gtm_ref, shf_ref, scf_ref, gr_ref, gffn_ref,
                    wo_ref, wr_ref, br_ref, j_ref, x1_ref, h2_ref, idx_ref, gate_ref, *, df):
    tm = x_ref.shape[1]
    hs = hs_ref[0]
    blocks = []
    for r in range(tm // GRID_W):
        blk = hs[r * GRID_W:(r + 1) * GRID_W]
        if r % 2 == 1:
            blk = jnp.dot(j_ref[...], blk, preferred_element_type=F32)
        blocks.append(blk.astype(F32))
    rg = jnp.concatenate(blocks, axis=0) * gg_ref[0].astype(F32)
    rgn = _rms(rg, gr_ref[...]).astype(BF16)
    mix = jnp.dot(fn_ref[0], wo_ref[:df, :], preferred_element_type=F32)
    mix += jnp.dot(rgn, wo_ref[df:, :], preferred_element_type=F32)
    x1 = x_ref[0] + gtm_ref[0] * mix
    x1_ref[0] = x1
    h2 = _rms(x1, gffn_ref[...] * (1.0 + scf_ref[0])) + shf_ref[0]
    h2_ref[0] = _pack_halves(h2)

    logits = _dot3_nt(wr_ref[...], h2) + br_ref[...]
    eidx = lax.broadcasted_iota(jnp.int32, logits.shape, 0)
    vals, idxs = [], []
    for _ in range(TOP_K):
        m = jnp.max(logits, axis=0, keepdims=True)
        sel = jnp.min(jnp.where(logits == m, eidx, N_EXPERTS), axis=0, keepdims=True)
        vals.append(m)
        idxs.append(sel)
        logits = jnp.where(eidx == sel, -jnp.inf, logits)
    ex = [jnp.exp(v - vals[0]) for v in vals]
    den = ex[0] + ex[1] + ex[2] + ex[3]
    for k in range(TOP_K):
        gate_ref[k:k + 1, :] = ex[k] / den
        idx_ref[k:k + 1, :] = idxs[k]


def _stage_m(fn, hs, gg, x, gt_m, sh_f, sc_f, g_out_r, g_ffn, w_out_bf, w_router_t, b_router, jmat, tm):
    b, s, d = x.shape
    df = fn.shape[2]
    dr = hs.shape[2]
    nt = s // tm
    ne = w_router_t.shape[0]
    vec = pl.BlockSpec((1, 1, d), lambda i, t: (i, 0, 0))
    half = lambda dd: pl.BlockSpec((1, tm, dd), lambda i, t: (i, t, 0))
    full = lambda shape: pl.BlockSpec(shape, lambda i, t: (0,) * len(shape))
    tok = pl.BlockSpec((TOP_K, tm), lambda i, t: (0, i * nt + t))
    return pl.pallas_call(
        functools.partial(_stage_m_kernel, df=df),
        grid=(b, nt),
        in_specs=[half(df), half(dr), half(dr), half(d), vec, vec, vec,
                  full((1, dr)), full((1, d)), full((d, d)), full((ne, d)), full((ne, 1)),
                  full((GRID_W, GRID_W))],
        out_specs=[half(d), half(d // 2), tok, tok],
        out_shape=[jax.ShapeDtypeStruct((b, s, d), F32), jax.ShapeDtypeStruct((b, s, d // 2), jnp.int32),
                   jax.ShapeDtypeStruct((TOP_K, b * s), jnp.int32),
                   jax.ShapeDtypeStruct((TOP_K, b * s), F32)],
        compiler_params=_cparams(("parallel", "arbitrary")),
        name="stage_m",
    )(fn, hs, gg, x, gt_m, sh_f, sc_f, g_out_r.reshape(1, dr), g_ffn.reshape(1, d), w_out_bf,
      w_router_t, b_router.reshape(ne, 1), jmat)


def _rank_kernel(idx_ref, tri_ref, rank_ref, cnt_ref, carry_s):
    c = pl.program_id(0)

    @pl.when(c == 0)
    def _():
        carry_s[...] = jnp.zeros_like(carry_s)

    l = tri_ref.shape[0]
    eidx = lax.broadcasted_iota(jnp.int32, (N_EXPERTS, l), 0)
    for sub in range(idx_ref.shape[1] // l):
        lanes = slice(sub * l, (sub + 1) * l)
        onehots = [eidx == idx_ref[k:k + 1, lanes] for k in range(TOP_K)]
        ohs = [jnp.where(o, 1.0, 0.0) for o in onehots]
        prefix = jnp.dot(jnp.concatenate(ohs, axis=0).astype(BF16), tri_ref[...], preferred_element_type=F32)
        for k in range(TOP_K):
            carry = carry_s[:, 0:1]
            pk = prefix[k * N_EXPERTS:(k + 1) * N_EXPERTS]
            rank = jnp.sum(jnp.where(onehots[k], pk - 1.0 + carry, 0.0), axis=0, keepdims=True)
            rank_ref[k:k + 1, lanes] = rank.astype(jnp.int32)
            carry_s[...] = carry_s[...] + jnp.sum(ohs[k], axis=1, keepdims=True)
    cnt_ref[...] = carry_s[...].astype(jnp.int32)


def _dispatch_ranks(idx, tri, tl):
    k, t = idx.shape
    return pl.pallas_call(
        _rank_kernel,
        grid=(t // tl,),
        in_specs=[pl.BlockSpec((k, tl), lambda c: (0, c)),
                  pl.BlockSpec(tri.shape, lambda c: (0, 0))],
        out_specs=[pl.BlockSpec((k, tl), lambda c: (0, c)),
                   pl.BlockSpec((N_EXPERTS, LANES), lambda c: (0, 0))],
        out_shape=[jax.ShapeDtypeStruct((k, t), jnp.int32),
                   jax.ShapeDtypeStruct((N_EXPERTS, LANES), jnp.int32)],
        scratch_shapes=[pltpu.VMEM((N_EXPERTS, LANES), F32)],
        compiler_params=_cparams(("arbitrary",)),
        name="dispatch_ranks",
    )(idx, tri)


def _moe_kernel(be_ref, bv_ref, nu_ref, x_ref, wgu_ref, bgu_ref, wd_ref, bd_ref, o_ref, wgu_s, wd_s, *, dff):
    i = pl.program_id(0)
    h = x_ref.shape[1]

    @pl.when(jnp.logical_or(i == 0, be_ref[i] != be_ref[jnp.maximum(i - 1, 0)]))
    def _():
        wgu_s[...] = wgu_ref[0].astype(BF16)
        wd_s[...] = wd_ref[0].astype(BF16)

    @pl.when(i < nu_ref[0])
    def _():
        rows = lax.broadcasted_iota(jnp.int32, x_ref.shape, 0)
        xw = jnp.where(rows < bv_ref[i], x_ref[...], 0)
        xa, xb = _unpack_halves(xw)
        xa = xa.astype(BF16)
        xb = xb.astype(BF16)
        acc = None
        for c in range(dff // MOE_FF_CHUNK):
            gs = slice(c * MOE_FF_CHUNK, (c + 1) * MOE_FF_CHUNK)
            us = slice(dff + c * MOE_FF_CHUNK, dff + (c + 1) * MOE_FF_CHUNK)
            g = jnp.dot(xa, wgu_s[:h, gs], preferred_element_type=F32)
            g += jnp.dot(xb, wgu_s[h:, gs], preferred_element_type=F32)
            u = jnp.dot(xa, wgu_s[:h, us], preferred_element_type=F32)
            u += jnp.dot(xb, wgu_s[h:, us], preferred_element_type=F32)
            gt = jnp.minimum(g + bgu_ref[0, :, gs], SWIGLU_LIMIT)
            up = jnp.clip(u + bgu_ref[0, :, us], -SWIGLU_LIMIT, SWIGLU_LIMIT)
            act = (up + 1.0) * (gt * _sigmoid(SWIGLU_ALPHA * gt))
            part = jnp.dot(act.astype(BF16), wd_s[gs, :], preferred_element_type=F32)
            acc = part if acc is None else acc + part
        o_ref[...] = _pack_halves(acc + bd_ref[0])


def _moe_experts(blk_expert, blk_valid, n_used, xs, wgu, bgu, wd, bd, tmm):
    cap, h = xs.shape
    ne, d, dff2 = wgu.shape
    dff = dff2 // 2
    row_blk = lambda i, be, bv, nu: (jnp.maximum(jnp.minimum(i, nu[0] - 1), 0), 0)
    wsel = lambda i, be, bv, nu: (be[i], 0, 0)
    grid_spec = pltpu.PrefetchScalarGridSpec(
        num_scalar_prefetch=3,
        grid=(cap // tmm,),
        in_specs=[pl.BlockSpec((tmm, h), row_blk),
                  pl.BlockSpec((1, d, dff2), wsel),
                  pl.BlockSpec((1, 1, dff2), wsel),
                  pl.BlockSpec((1, dff, d), wsel),
                  pl.BlockSpec((1, 1, d), wsel)],
        out_specs=pl.BlockSpec((tmm, h), row_blk),
        scratch_shapes=[pltpu.VMEM((d, dff2), BF16), pltpu.VMEM((dff, d), BF16)],
    )
    return pl.pallas_call(
        functools.partial(_moe_kernel, dff=dff),
        grid_spec=grid_spec,
        out_shape=jax.ShapeDtypeStruct((cap, h), jnp.int32),
        compiler_params=_cparams(("arbitrary",)),
        name="moe_experts",
    )(blk_expert, blk_valid, n_used, xs, wgu, bgu.reshape(ne, 1, dff2), wd, bd.reshape(ne, 1, d))


SC_CHUNK = 64
SC_ID_CHUNK = 128


def _sc_workers():
    info = plsc.get_sparse_core_info()
    return info.num_cores, info.num_subcores


def _sc_scatter_into(out_ref, rows, dest):
    n, w = rows.shape
    nc, ns = _sc_workers()
    per_w = n // (nc * ns)
    assert per_w % SC_CHUNK == 0
    mesh = plsc.VectorSubcoreMesh(core_axis_name="c", subcore_axis_name="s")

    @functools.partial(
        pl.kernel, mesh=mesh, out_type=(),
        scratch_types=[pltpu.VMEM((SC_CHUNK,), jnp.int32),
                       pltpu.VMEM((SC_CHUNK, w), jnp.int32),
                       pltpu.SemaphoreType.DMA],
    )
    def scatter_rows(rows_hbm, dest_hbm, out_hbm, idx_v, rows_v, sem):
        base = (lax.axis_index("s") * nc + lax.axis_index("c")) * per_w

        @pl.loop(0, per_w // SC_CHUNK)
        def _(j):
            off = pl.multiple_of(base + j * SC_CHUNK, SC_CHUNK)
            pltpu.sync_copy(rows_hbm.at[pl.ds(off, SC_CHUNK)], rows_v)
            pltpu.sync_copy(dest_hbm.at[pl.ds(off, SC_CHUNK)], idx_v)
            pltpu.async_copy(rows_v, out_hbm.at[idx_v], sem).wait()

    scatter_rows(rows, dest, out_ref)


def _sc_scatter_ids(dest, out_rows):
    n = dest.shape[0]
    info = plsc.get_sparse_core_info()
    nc, ns, nl = info.num_cores, info.num_subcores, info.num_lanes
    per_w = n // (nc * ns)
    chunk = SC_ID_CHUNK
    assert per_w % (2 * chunk) == 0
    mesh = plsc.VectorSubcoreMesh(core_axis_name="c", subcore_axis_name="s")
    idx_t = pltpu.VMEM((chunk,), jnp.int32)
    rows_t = pltpu.VMEM((chunk, LANES), jnp.int32)

    @functools.partial(
        pl.kernel, mesh=mesh,
        out_type=jax.ShapeDtypeStruct((out_rows, LANES), jnp.int32),
        scratch_types=[idx_t, idx_t, rows_t, rows_t, pltpu.SemaphoreType.DMA, pltpu.SemaphoreType.DMA],
    )
    def scatter_ids(dest_hbm, out_hbm, idx_a, idx_b, rows_a, rows_b, sem_a, sem_b):
        base = (lax.axis_index("s") * nc + lax.axis_index("c")) * per_w

        def start(off, idx_v, rows_v, sem):
            for r in range(chunk):
                rows_v[r, pl.ds(0, nl)] = jnp.zeros((nl,), jnp.int32) + (off + r)
            pltpu.sync_copy(dest_hbm.at[pl.ds(off, chunk)], idx_v)
            return pltpu.async_copy(rows_v, out_hbm.at[idx_v], sem)

        @pl.loop(0, per_w // (2 * chunk))
        def _(j):
            off = pl.multiple_of(base + j * (2 * chunk), 2 * chunk)
            copy_a = start(off, idx_a, rows_a, sem_a)
            copy_b = start(off + chunk, idx_b, rows_b, sem_b)
            copy_a.wait()
            copy_b.wait()

    return scatter_ids(dest)


def _sc_gather(table, idx):
    n = idx.shape[0]
    w = table.shape[1]
    nc, ns = _sc_workers()
    per_w = n // (nc * ns)
    assert per_w % SC_CHUNK == 0
    mesh = plsc.VectorSubcoreMesh(core_axis_name="c", subcore_axis_name="s")

    @functools.partial(
        pl.kernel, mesh=mesh,
        out_type=jax.ShapeDtypeStruct((n, w), jnp.int32),
        scratch_types=[pltpu.VMEM((SC_CHUNK,), jnp.int32),
                       pltpu.VMEM((SC_CHUNK, w), jnp.int32),
                       pltpu.SemaphoreType.DMA],
    )
    def gather_rows(table_hbm, idx_hbm, out_hbm, idx_v, rows_v, sem):
        base = (lax.axis_index("s") * nc + lax.axis_index("c")) * per_w

        @pl.loop(0, per_w // SC_CHUNK)
        def _(j):
            off = pl.multiple_of(base + j * SC_CHUNK, SC_CHUNK)
            pltpu.sync_copy(idx_hbm.at[pl.ds(off, SC_CHUNK)], idx_v)
            pltpu.async_copy(table_hbm.at[idx_v], rows_v, sem).wait()
            pltpu.sync_copy(rows_v, out_hbm.at[pl.ds(off, SC_CHUNK)])

    return gather_rows(table, idx)


def _combine_kernel(x1_ref, y_ref, gate_ref, gtf_ref, g_ref, o_ref):
    h = y_ref.shape[3]
    gates = gate_ref[0]
    moe_a = moe_b = None
    for k in range(TOP_K):
        ya, yb = _unpack_halves(y_ref[k, 0])
        gk = gates[:, k:k + 1]
        moe_a = gk * ya if k == 0 else moe_a + gk * ya
        moe_b = gk * yb if k == 0 else moe_b + gk * yb
    za = x1_ref[0, :, :h] + gtf_ref[0, :, :h] * moe_a
    zb = x1_ref[0, :, h:] + gtf_ref[0, :, h:] * moe_b
    ms = (jnp.sum(za * za, axis=-1, keepdims=True) + jnp.sum(zb * zb, axis=-1, keepdims=True)) / (2 * h)
    inv = lax.rsqrt(ms + EPS)
    o_ref[0, :, :h] = za * inv * g_ref[:, :h]
    o_ref[0, :, h:] = zb * inv * g_ref[:, h:]


def _combine(x1, yk, gates_t, gt_f, g_final, tm):
    b, s, d = x1.shape
    h = yk.shape[3]
    return pl.pallas_call(
        _combine_kernel,
        grid=(b, s // tm),
        in_specs=[pl.BlockSpec((1, tm, d), lambda i, t: (i, t, 0)),
                  pl.BlockSpec((TOP_K, 1, tm, h), lambda i, t: (0, i, t, 0)),
                  pl.BlockSpec((1, tm, TOP_K), lambda i, t: (i, t, 0)),
                  pl.BlockSpec((1, 1, d), lambda i, t: (i, 0, 0)),
                  pl.BlockSpec((1, d), lambda i, t: (0, 0))],
        out_specs=pl.BlockSpec((1, tm, d), lambda i, t: (i, t, 0)),
        out_shape=jax.ShapeDtypeStruct((b, s, d), F32),
        compiler_params=_cparams(("parallel", "arbitrary")),
        name="combine",
    )(x1, yk, gates_t, gt_f, g_final.reshape(1, d))


def _dft_tables(rows, gd):
    seq = rows * GRID_W
    n = np.arange(rows)
    ang1 = 2.0 * np.pi * np.outer(n, n) / rows
    d2 = np.concatenate([np.cos(ang1), -np.sin(ang1)], axis=0)
    k1 = np.arange(rows)[:, None, None]
    k2 = np.arange(GRID_W)[None, :, None]
    n2 = np.arange(GRID_W)[None, None, :]
    ang2 = 2.0 * np.pi * ((n2 * (k1 + rows * k2)) % seq) / seq
    ec, es = np.cos(ang2), np.sin(ang2)
    etab = np.concatenate([np.concatenate([ec, es], axis=2),
                           np.concatenate([-es, ec], axis=2)], axis=1)
    c = np.arange(gd)
    angc = 2.0 * np.pi * np.outer(c, c) / gd
    scale = 1.0 / np.sqrt(seq * gd)
    return (jnp.asarray(d2, BF16), jnp.asarray(etab, BF16),
            jnp.asarray(np.cos(angc) * scale, F32), jnp.asarray(np.sin(angc) * scale, F32))


def _block_diag(w):
    h, i, o = w.shape
    eye = jnp.eye(h, dtype=w.dtype)
    return (eye[:, None, :, None] * w[:, :, None, :]).reshape(h * i, h * o)


def kernel(x, c, ctx, c_ctx, w_mod, b_mod, g_norm_mix, g_norm_ffn, w_in, w_fourier, conv_w, conv_b,
           rg_w_a, rg_b_a, rg_w_x, rg_b_x, rg_lam, g_out_fourier, g_out_rg, w_out, w_router, b_router,
           w_gate_up, b_gate_up, w_down, b_down, g_final):
    assert w_mod.shape[0] == 1, "single-layer stack only"
    b, s, d = x.shape
    df = w_fourier.shape[1] * w_fourier.shape[2]
    dr = conv_w.shape[2]
    gd = w_fourier.shape[2]
    rows = s // GRID_W
    t = b * s
    ne = w_router.shape[2]

    mrows = -(-(b + 1) // SUBLANES) * SUBLANES
    cond = jnp.zeros((mrows, d), F32).at[:b].set(c).at[b].set(c_ctx)
    mod = _adaln(cond, w_mod[0], b_mod[0])
    sh_m, sc_m, gt_m, sh_f, sc_f, gt_f = [mod[:b, k * d:(k + 1) * d].reshape(b, 1, d) for k in range(N_MOD)]
    csh_m = mod[b:b + 1, 0:d].reshape(1, 1, d)
    csc_m = mod[b:b + 1, d:2 * d].reshape(1, 1, d)

    tm = min(s, TOKEN_TILE)
    d2, etab, cmat, smat = _dft_tables(rows, gd)
    jmat = jnp.asarray(np.eye(GRID_W)[::-1].copy(), BF16)

    w_in_bf = w_in[0].astype(BF16)
    f, xs, gg = _stage_b(x, sh_m, sc_m, g_norm_mix[0], w_in_bf, jmat, df, dr, tm=tm)
    xr_ctx = _stage_b_ctx(ctx, csh_m, csc_m, g_norm_mix[0], w_in_bf, df, dr)

    wcat = jnp.stack([jnp.concatenate([_block_diag(rg_w_a[0, dd]), _block_diag(rg_w_x[0, dd])], axis=1)
                      for dd in range(2)]).astype(BF16)
    bcat = jnp.concatenate([rg_b_a[0], rg_b_x[0]], axis=1).reshape(2, 1, 2 * dr)
    lam = rg_lam[0].reshape(2, 1, dr)
    h0 = jnp.zeros((b, SUBLANES, dr), F32)
    _, hfin_ctx = _rg_scan(xr_ctx, h0, conv_w[0], conv_b[0], wcat, bcat, lam, tc=ctx.shape[1])
    hs, _ = _rg_scan(xs, hfin_ctx, conv_w[0], conv_b[0], wcat, bcat, lam, tc=min(s, SCAN_CHUNK))

    cw, sw = _fold_fourier(cmat, smat, w_fourier[0])
    wcs = jnp.concatenate([cw, sw], axis=1).astype(BF16)
    y = _fourier_stage1(f, d2, tl=GRID_W * df)
    fn = _fourier_stage2(y.reshape(b, 2, rows, GRID_W, df), etab, wcs, g_out_fourier[0],
                         kb=min(rows, 32))
    fn = fn.reshape(b, s, df)

    x1, h2, idx, gates = _stage_m(fn, hs, gg, x, gt_m, sh_f, sc_f, g_out_rg[0], g_norm_ffn[0],
                                  w_out[0].astype(BF16), w_router[0].T, b_router[0], jmat, tm=tm)

    tl = min(t, RANK_TILE)
    tri = jnp.asarray(np.triu(np.ones((tl, tl))), BF16)
    rank, cnt = _dispatch_ranks(idx, tri, min(t, RANK_STEP))
    counts = cnt[:, 0]
    tmm = MOE_ROW_TILE
    padded = (counts + tmm - 1) // tmm * tmm
    pad_end = jnp.cumsum(padded)
    pad_start = pad_end - padded
    eids = jnp.arange(ne, dtype=jnp.int32)
    dest = rank + jnp.sum(jnp.where(idx[:, :, None] == eids, pad_start, 0), axis=-1)
    n_blocks = -(-(t * TOP_K) // tmm) + ne
    cap = n_blocks * tmm
    n_used = (pad_end[-1] // tmm).astype(jnp.int32).reshape(1)
    blk_start = jnp.arange(n_blocks, dtype=jnp.int32) * tmm
    blk_expert = jnp.sum(blk_start[:, None] >= pad_end[None, :], axis=1).astype(jnp.int32)
    last_expert = jnp.sum(pad_end[-1] - tmm >= pad_end).astype(jnp.int32)
    blk_expert = jnp.minimum(blk_expert, last_expert)
    sel = blk_expert[:, None] == eids
    blk_first = jnp.sum(jnp.where(sel, pad_start, 0), axis=1)
    blk_count = jnp.sum(jnp.where(sel, counts, 0), axis=1)
    blk_valid = jnp.clip(blk_count - (blk_start - blk_first), 0, tmm).astype(jnp.int32)

    na = TOP_K * t
    inv = _sc_scatter_ids(dest.reshape(-1), cap)[:, 0].reshape(n_blocks, tmm)
    live = jnp.arange(tmm, dtype=jnp.int32)[None, :] < blk_valid[:, None]
    spread = jnp.arange(cap, dtype=jnp.int32).reshape(n_blocks, tmm) % t
    src_tok = jnp.where(live, inv % t, spread).reshape(-1)
    dst_row = jnp.where(live, inv, na + spread).reshape(-1)

    h2_rows = h2.reshape(t, d // 2)
    y_all = jax.empty_ref(jax.ShapeDtypeStruct(((TOP_K + 1) * t, d // 2), jnp.int32))
    unit = n_blocks // sum(MOE_PIPE)
    assert unit * sum(MOE_PIPE) == n_blocks
    blk0 = 0
    for parts in MOE_PIPE:
        nq = parts * unit
        blocks = slice(blk0, blk0 + nq)
        rows = slice(blk0 * tmm, (blk0 + nq) * tmm)
        x_q = _sc_gather(h2_rows, src_tok[rows])
        nu_q = jnp.clip(n_used - blk0, 0, nq).astype(jnp.int32)
        blk0 += nq
        y_q = _moe_experts(blk_expert[blocks], blk_valid[blocks], nu_q, x_q,
                           w_gate_up[0], b_gate_up[0], w_down[0], b_down[0], tmm)
        _sc_scatter_into(y_all, y_q, dst_row[rows])
    yk = y_all[...].reshape(TOP_K + 1, b, s, d // 2)
    return _combine(x1, yk, gates.T.reshape(b, s, TOP_K), gt_f, g_final, tm)
```

```python
import functools

import numpy as np
import jax
import jax.numpy as jnp
from jax import lax
from jax.experimental import pallas as pl
from jax.experimental.pallas import tpu as pltpu
from jax.experimental.pallas import tpu_sc as plsc

GRID_W = 64
CONV_W = 4
CONV_PAD_LO = 2
RG_C = 8.0
N_EXPERTS = 32
TOP_K = 4
SWIGLU_LIMIT = 7.0
SWIGLU_ALPHA = 1.702
N_MOD = 6
EPS = 1e-6

LANES = 128
SUBLANES = 8
VMEM_LIMIT_BYTES = 56 * 1024 * 1024
TOKEN_TILE = 1024
RANK_TILE = 512
SCAN_CHUNK = 512
MOE_ROW_TILE = 512
MOE_FF_CHUNK = 512
MOE_PIPE = (3, 11, 7, 3)

F32 = jnp.float32
BF16 = jnp.bfloat16


def _cparams(sem):
    return pltpu.CompilerParams(dimension_semantics=sem, vmem_limit_bytes=VMEM_LIMIT_BYTES)


def _split_bf16(a):
    hi = a.astype(BF16)
    lo = (a - hi.astype(F32)).astype(BF16)
    return hi, lo


def _dot3(a, b):
    ah, al = _split_bf16(a)
    bh, bl = _split_bf16(b)
    out = jnp.dot(ah, bh, preferred_element_type=F32)
    out += jnp.dot(ah, bl, preferred_element_type=F32)
    out += jnp.dot(al, bh, preferred_element_type=F32)
    return out


def _dot3_nt(a, b):
    dn = (((1,), (1,)), ((), ()))
    m = a.shape[0]
    ah, al = _split_bf16(a)
    bh, bl = _split_bf16(b)
    both = lax.dot_general(jnp.concatenate([ah, al], axis=0), bh, dn, preferred_element_type=F32)
    return both[:m] + both[m:] + lax.dot_general(ah, bl, dn, preferred_element_type=F32)


def _gelu_tanh(x):
    return 0.5 * x * (1.0 + jnp.tanh(0.7978845608028654 * (x + 0.044715 * (x * x * x))))


def _rms(x, g):
    return x * lax.rsqrt(jnp.mean(x * x, axis=-1, keepdims=True) + EPS) * g


def _pack_halves(v):
    h = v.shape[1] // 2
    hi = lax.bitcast_convert_type(v[:, :h].astype(BF16).astype(F32), jnp.uint32)
    lo = lax.bitcast_convert_type(v[:, h:].astype(BF16).astype(F32), jnp.uint32)
    return lax.bitcast_convert_type(hi | (lo >> 16), jnp.int32)


def _unpack_halves(w):
    u = lax.bitcast_convert_type(w, jnp.uint32)
    hi = lax.bitcast_convert_type(u & jnp.uint32(0xFFFF0000), F32)
    lo = lax.bitcast_convert_type(u << 16, F32)
    return hi, lo


def _adaln_kernel(c_ref, w_ref, b_ref, o_ref):
    s = c_ref[...]
    s = s * jax.nn.sigmoid(s)
    o_ref[...] = _dot3(s, w_ref[...]) + b_ref[...]


def _adaln(cond, w_mod, b_mod):
    m, d = cond.shape
    n = w_mod.shape[1]
    tn = n // N_MOD
    return pl.pallas_call(
        _adaln_kernel,
        grid=(n // tn,),
        in_specs=[pl.BlockSpec((m, d), lambda i: (0, 0)),
                  pl.BlockSpec((d, tn), lambda i: (0, i)),
                  pl.BlockSpec((1, tn), lambda i: (0, i))],
        out_specs=pl.BlockSpec((m, tn), lambda i: (0, i)),
        out_shape=jax.ShapeDtypeStruct((m, n), F32),
        compiler_params=_cparams(("arbitrary",)),
        name="adaln",
    )(cond, w_mod, b_mod.reshape(1, n))


def _fold_kernel(c_ref, s_ref, w_ref, cw_ref, sw_ref):
    w = w_ref[0]
    cw_ref[0] = _dot3(c_ref[...], w)
    sw_ref[0] = _dot3(s_ref[...], w)


def _fold_fourier(cmat, smat, w_f):
    g, gd, _ = w_f.shape
    spec_m = pl.BlockSpec((gd, gd), lambda i: (0, 0))
    spec_w = pl.BlockSpec((1, gd, gd), lambda i: (i, 0, 0))
    return pl.pallas_call(
        _fold_kernel,
        grid=(g,),
        in_specs=[spec_m, spec_m, spec_w],
        out_specs=[spec_w, spec_w],
        out_shape=[jax.ShapeDtypeStruct((g, gd, gd), F32)] * 2,
        compiler_params=_cparams(("arbitrary",)),
        name="fold_fourier",
    )(cmat, smat, w_f)


def _seg_pitch(seg_len):
    n8 = seg_len // SUBLANES
    return SUBLANES * (n8 + 1 - n8 % 2)


def _pitched_store(v, buf, blk0):
    pitch = _seg_pitch(GRID_W)
    for r in range(v.shape[0] // GRID_W):
        for c in range(v.shape[1] // LANES):
            buf[c, (blk0 + r) * pitch:(blk0 + r) * pitch + GRID_W, :] = (
                v[r * GRID_W:(r + 1) * GRID_W, c * LANES:(c + 1) * LANES])


def _pitched_gather(buf, nb, store):
    pitch = _seg_pitch(GRID_W)
    for pos in range(GRID_W):
        for c in range(buf.shape[0]):
            store(pos, c, buf[c, pl.ds(pos, nb, stride=pitch), :])


def _stage_b_kernel(x_ref, sh_ref, sc_ref, g_ref, w_ref, j_ref, f_ref, xs_ref, gg_ref, fbuf, *, df, dr):
    tm = x_ref.shape[1]
    h = _rms(x_ref[0], g_ref[...] * (1.0 + sc_ref[0])) + sh_ref[0]
    hb = h.astype(BF16)
    gr = jnp.dot(hb, w_ref[:, df + dr:], preferred_element_type=F32)
    gg_ref[0] = _gelu_tanh(gr).astype(BF16)
    _pitched_store(jnp.dot(hb, w_ref[:, :df], preferred_element_type=F32), fbuf, 0)

    def store_f(pos, c, tile):
        f_ref[0, :, pos * df + c * LANES:pos * df + (c + 1) * LANES] = tile.astype(BF16)
    _pitched_gather(fbuf, tm // GRID_W, store_f)
    xr = jnp.dot(hb, w_ref[:, df:df + dr], preferred_element_type=F32).astype(BF16)
    for r in range(tm // GRID_W):
        blk = xr[r * GRID_W:(r + 1) * GRID_W]
        if r % 2 == 1:
            blk = jnp.dot(j_ref[...], blk, preferred_element_type=F32).astype(BF16)
        xs_ref[0, r * GRID_W:(r + 1) * GRID_W, :] = blk


def _stage_b(x, shift, scale, g, w_in_bf, jmat, df, dr, tm):
    b, s, d = x.shape
    n = w_in_bf.shape[1]
    vec = pl.BlockSpec((1, 1, d), lambda i, t: (i, 0, 0))
    out = pl.BlockSpec((1, tm, df), lambda i, t: (i, t, 0))
    nb = tm // GRID_W
    tok = jax.ShapeDtypeStruct((b, s, df), BF16)
    return pl.pallas_call(
        functools.partial(_stage_b_kernel, df=df, dr=dr),
        grid=(b, s // tm),
        in_specs=[pl.BlockSpec((1, tm, d), lambda i, t: (i, t, 0)), vec, vec,
                  pl.BlockSpec((1, d), lambda i, t: (0, 0)),
                  pl.BlockSpec((d, n), lambda i, t: (0, 0)),
                  pl.BlockSpec((GRID_W, GRID_W), lambda i, t: (0, 0))],
        out_specs=[pl.BlockSpec((1, nb, GRID_W * df), lambda i, t: (i, t, 0)), out, out],
        out_shape=[jax.ShapeDtypeStruct((b, s // GRID_W, GRID_W * df), BF16), tok, tok],
        scratch_shapes=[pltpu.VMEM((df // LANES, nb * _seg_pitch(GRID_W), LANES), F32)],
        compiler_params=_cparams(("parallel", "arbitrary")),
        name="stage_b",
    )(x, shift, scale, g.reshape(1, d), w_in_bf, jmat)


def _stage_b_ctx_kernel(x_ref, sh_ref, sc_ref, g_ref, w_ref, xr_ref):
    h = _rms(x_ref[0], g_ref[...] * (1.0 + sc_ref[0])) + sh_ref[0]
    xr_ref[0] = jnp.dot(h.astype(BF16), w_ref[...], preferred_element_type=F32).astype(BF16)


def _stage_b_ctx(ctx, shift, scale, g, w_in_bf, df, dr):
    b, s, d = ctx.shape
    assert df % dr == 0
    vec = pl.BlockSpec((1, 1, d), lambda i: (0, 0, 0))
    return pl.pallas_call(
        _stage_b_ctx_kernel,
        grid=(b,),
        in_specs=[pl.BlockSpec((1, s, d), lambda i: (i, 0, 0)), vec, vec,
                  pl.BlockSpec((1, d), lambda i: (0, 0)),
                  pl.BlockSpec((d, dr), lambda i: (0, df // dr))],
        out_specs=pl.BlockSpec((1, s, dr), lambda i: (i, 0, 0)),
        out_shape=jax.ShapeDtypeStruct((b, s, dr), BF16),
        compiler_params=_cparams(("arbitrary",)),
        name="stage_b_ctx",
    )(ctx, shift, scale, g.reshape(1, d), w_in_bf)


HALO = 16


def _sigmoid(x):
    return 0.5 * jnp.tanh(0.5 * x) + 0.5


def _rg_kernel(xs_ref, h0_ref, cw_ref, cb_ref, w_ref, b_ref, lam_ref, pm_ref, pmt_ref, out_ref, hfin_ref,
               hf_s, xc_s, a_s, u_s, hl_s, p_s, c_s, hc_s, *, tc, nchunk, seq, dr):
    p = pl.program_id(1)
    j = pl.program_id(2)
    cidx = jnp.where(p == 0, j, nchunk - 1 - j)
    start = pl.multiple_of(cidx * tc, tc)
    nseg = SUBLANES
    sl = tc // nseg
    sub = lax.broadcasted_iota(jnp.int32, (nseg, dr), 0)

    @pl.when(p == 0)
    def _():
        xp = jnp.dot(pm_ref[...], xs_ref[0, pl.ds(start, tc), :], preferred_element_type=F32)
        pstart = pl.multiple_of(jnp.maximum(start - HALO, 0), HALO)
        nstart = pl.multiple_of(jnp.minimum(start + tc, seq - HALO), HALO)
        prev = xs_ref[0, pl.ds(pstart, HALO), :].astype(F32)
        nxt = xs_ref[0, pl.ds(nstart, HALO), :].astype(F32)
        prev = jnp.where(cidx > 0, prev, 0.0)
        nxt = jnp.where(cidx < nchunk - 1, nxt, 0.0)
        tm2 = jnp.where(sub == 0, prev[HALO - 2:HALO - 1], pltpu.roll(xp[(sl - 2) * nseg:(sl - 1) * nseg], 1, 0))
        tm1 = jnp.where(sub == 0, prev[HALO - 1:HALO], pltpu.roll(xp[(sl - 1) * nseg:sl * nseg], 1, 0))
        tp1 = jnp.where(sub == nseg - 1, nxt[0:1], pltpu.roll(xp[0:nseg], nseg - 1, 0))
        ext = jnp.concatenate([tm2, tm1, xp, tp1], axis=0)
        xc = cb_ref[...] + cw_ref[0:1, :] * ext[0:tc]
        for k in range(1, CONV_W):
            xc = xc + cw_ref[k:k + 1, :] * ext[k * nseg:k * nseg + tc]
        xc_s[pl.ds(start, tc), :] = xc

    xc = xc_s[pl.ds(start, tc), :]
    gates = jnp.dot(xc.astype(BF16), w_ref[p], preferred_element_type=F32) + b_ref[p]
    i = _sigmoid(gates[:, dr:])
    half_c = (-0.5 * RG_C) * jax.nn.softplus(-lam_ref[p])
    log_a = half_c * jnp.tanh(0.5 * gates[:, :dr]) + half_c
    a = jnp.exp(log_a)
    a_s[...] = a
    w = -jnp.tanh(log_a) * (1.0 + a * a)
    u_s[...] = jnp.where(w > 0.0, w * lax.rsqrt(w), 0.0) * (i * xc)

    @pl.when(jnp.logical_and(p == 0, j == 0))
    def _():
        hfin_ref[...] = jnp.zeros_like(hfin_ref)

    @pl.when(j == 0)
    def _():
        hc_s[0:1, :] = h0_ref[0, pl.ds(p, 1), :]

    def segment_scan(reverse):
        def body(q, carry):
            t = (sl - 1 - q) if reverse else q
            rows = pl.ds(pl.multiple_of(t * nseg, nseg), nseg)
            h, pr = carry
            av = a_s[rows, :]
            h = av * h + u_s[rows, :]
            pr = av * pr
            hl_s[rows, :] = h
            p_s[rows, :] = pr
            return h, pr
        h_end, p_end = lax.fori_loop(0, sl, body, (jnp.zeros((nseg, dr), F32), jnp.ones((nseg, dr), F32)),
                                     unroll=4)
        carry = hc_s[0:1, :]
        for g in (range(nseg - 1, -1, -1) if reverse else range(nseg)):
            c_s[g:g + 1, :] = carry
            carry = h_end[g:g + 1, :] + p_end[g:g + 1, :] * carry
        hc_s[0:1, :] = carry
        return c_s[...]

    def corrected(cin):
        h = hl_s[...].reshape(sl, nseg, dr) + p_s[...].reshape(sl, nseg, dr) * cin[None]
        return h.reshape(tc, dr)

    @pl.when(p == 0)
    def _():
        hf_s[pl.ds(start, tc), :] = corrected(segment_scan(False))

    @pl.when(p == 1)
    def _():
        tot = corrected(segment_scan(True)) + hf_s[pl.ds(start, tc), :]
        out_ref[0] = jnp.dot(pmt_ref[...], tot.astype(BF16), preferred_element_type=F32).astype(BF16)

    @pl.when(j == nchunk - 1)
    def _():
        hfin_ref[0, pl.ds(p, 1), :] = hc_s[0:1, :]


def _rg_scan(xs, h0, conv_w, conv_b, wcat, bcat, lam, tc):
    b, s, dr = xs.shape
    nchunk = s // tc
    last = nchunk - 1
    sl = tc // SUBLANES
    src = (np.arange(tc) % SUBLANES) * sl + np.arange(tc) // SUBLANES
    pm = np.zeros((tc, tc), np.float32)
    pm[np.arange(tc), src] = 1.0
    chunk_buf = pltpu.VMEM((tc, dr), F32)
    full2 = lambda shape: pl.BlockSpec(shape, lambda i, p, j: (0,) * len(shape))
    return pl.pallas_call(
        functools.partial(_rg_kernel, tc=tc, nchunk=nchunk, seq=s, dr=dr),
        grid=(b, 2, nchunk),
        in_specs=[pl.BlockSpec((1, s, dr), lambda i, p, j: (i, 0, 0)),
                  pl.BlockSpec((1, SUBLANES, dr), lambda i, p, j: (i, 0, 0)),
                  full2((CONV_W, dr)), full2((1, dr)),
                  full2((2, dr, 2 * dr)), full2((2, 1, 2 * dr)), full2((2, 1, dr)),
                  full2((tc, tc)), full2((tc, tc))],
        out_specs=[pl.BlockSpec((1, tc, dr), lambda i, p, j: (i, jnp.where(p == 0, last, last - j), 0)),
                   pl.BlockSpec((1, SUBLANES, dr), lambda i, p, j: (i, 0, 0))],
        out_shape=[jax.ShapeDtypeStruct((b, s, dr), BF16),
                   jax.ShapeDtypeStruct((b, SUBLANES, dr), F32)],
        scratch_shapes=[pltpu.VMEM((s, dr), F32), pltpu.VMEM((s, dr), F32),
                        chunk_buf, chunk_buf, chunk_buf, chunk_buf,
                        pltpu.VMEM((SUBLANES, dr), F32), pltpu.VMEM((SUBLANES, dr), F32)],
        compiler_params=_cparams(("arbitrary", "arbitrary", "arbitrary")),
        name="rg_scan",
    )(xs, h0, conv_w, conv_b.reshape(1, dr), wcat, bcat, lam, jnp.asarray(pm, BF16), jnp.asarray(pm.T, BF16))


def _f1_kernel(d_ref, x_ref, y_ref):
    y_ref[0] = jnp.dot(d_ref[...], x_ref[0], preferred_element_type=F32).astype(BF16)


def _fourier_stage1(fv, d2, tl):
    b, r, n = fv.shape
    return pl.pallas_call(
        _f1_kernel,
        grid=(b, n // tl),
        in_specs=[pl.BlockSpec((2 * r, r), lambda i, l: (0, 0)),
                  pl.BlockSpec((1, r, tl), lambda i, l: (i, 0, l))],
        out_specs=pl.BlockSpec((1, 2 * r, tl), lambda i, l: (i, 0, l)),
        out_shape=jax.ShapeDtypeStruct((b, 2 * r, n), BF16),
        compiler_params=_cparams(("parallel", "arbitrary")),
        name="fourier_stage1",
    )(d2, fv)


def _f2_kernel(y_ref, e_ref, wcs_ref, g_ref, o_ref, obuf, *, kb, df):
    zr, zi = [], []
    for q in range(kb):
        yk = jnp.concatenate([y_ref[0, 0, q], y_ref[0, 1, q]], axis=0)
        z = jnp.dot(e_ref[q], yk, preferred_element_type=F32)
        zr.append(z[:GRID_W])
        zi.append(z[GRID_W:])
    zr = jnp.concatenate(zr, axis=0).astype(BF16)
    zi = jnp.concatenate(zi, axis=0).astype(BF16)
    gd = wcs_ref.shape[2]
    o = jnp.concatenate(
        [jnp.dot(jnp.concatenate([zr[:, g * gd:(g + 1) * gd], zi[:, g * gd:(g + 1) * gd]], axis=1), wcs_ref[g],
                 preferred_element_type=F32) for g in range(df // gd)], axis=1)
    on = _rms(o, g_ref[...])

    def store_o(pos, c, tile):
        o_ref[0, pos, :, c * LANES:(c + 1) * LANES] = tile.astype(BF16)
    _pitched_store(on, obuf, 0)
    _pitched_gather(obuf, kb, store_o)


def _fourier_stage2(y5, etab, wcs, g, kb):
    b, _, r, w, df = y5.shape
    return pl.pallas_call(
        functools.partial(_f2_kernel, kb=kb, df=df),
        grid=(b, r // kb),
        in_specs=[pl.BlockSpec((1, 2, kb, w, df), lambda i, k: (i, 0, k, 0, 0)),
                  pl.BlockSpec((kb, 2 * w, 2 * w), lambda i, k: (k, 0, 0)),
                  pl.BlockSpec(wcs.shape, lambda i, k: (0, 0, 0)),
                  pl.BlockSpec((1, df), lambda i, k: (0, 0))],
        out_specs=pl.BlockSpec((1, w, kb, df), lambda i, k: (i, 0, k, 0)),
        out_shape=jax.ShapeDtypeStruct((b, w, r, df), BF16),
        scratch_shapes=[pltpu.VMEM((df // LANES, kb * _seg_pitch(GRID_W), LANES), F32)],
        compiler_params=_cparams(("parallel", "arbitrary")),
        name="fourier_stage2",
    )(y5, etab, wcs, g.reshape(1, df))


def _stage_m_kernel(fn_ref, hs_ref, gg_ref, x_ref, gtm_ref, shf_ref, scf_ref, gr_ref, gffn_ref,
                    wo_ref, wr_ref, br_ref, j_ref, tri_ref, x1_ref, h2_ref, idx_ref, gate_ref, rank_ref, cnt_ref,
                    carry_s, *, df):
    tm = x_ref.shape[1]
    hs = hs_ref[0]
    blocks = []
    for r in range(tm // GRID_W):
        blk = hs[r * GRID_W:(r + 1) * GRID_W]
        if r % 2 == 1:
            blk = jnp.dot(j_ref[...], blk, preferred_element_type=F32)
        blocks.append(blk.astype(F32))
    rg = jnp.concatenate(blocks, axis=0) * gg_ref[0].astype(F32)
    rgn = _rms(rg, gr_ref[...]).astype(BF16)
    mix = jnp.dot(fn_ref[0], wo_ref[:df, :], preferred_element_type=F32)
    mix += jnp.dot(rgn, wo_ref[df:, :], preferred_element_type=F32)
    x1 = x_ref[0] + gtm_ref[0] * mix
    x1_ref[0] = x1
    h2 = _rms(x1, gffn_ref[...] * (1.0 + scf_ref[0])) + shf_ref[0]
    h2_ref[0] = _pack_halves(h2)

    logits = _dot3_nt(wr_ref[...], h2) + br_ref[...]
    eidx = lax.broadcasted_iota(jnp.int32, logits.shape, 0)
    vals, idxs = [], []
    for _ in range(TOP_K):
        m = jnp.max(logits, axis=0, keepdims=True)
        sel = jnp.min(jnp.where(logits == m, eidx, N_EXPERTS), axis=0, keepdims=True)
        vals.append(m)
        idxs.append(sel)
        logits = jnp.where(eidx == sel, -jnp.inf, logits)
    ex = [jnp.exp(v - vals[0]) for v in vals]
    den = ex[0] + ex[1] + ex[2] + ex[3]
    for k in range(TOP_K):
        gate_ref[k:k + 1, :] = ex[k] / den
        idx_ref[k:k + 1, :] = idxs[k]

    @pl.when(jnp.logical_and(pl.program_id(0) == 0, pl.program_id(1) == 0))
    def _():
        carry_s[...] = jnp.zeros_like(carry_s)

    l = tri_ref.shape[0]
    e_piece = lax.broadcasted_iota(jnp.int32, (N_EXPERTS, l), 0)
    for sub in range(tm // l):
        lanes = slice(sub * l, (sub + 1) * l)
        onehots = [e_piece == idxs[k][:, lanes] for k in range(TOP_K)]
        ohs = [jnp.where(o, 1.0, 0.0) for o in onehots]
        prefix = jnp.dot(jnp.concatenate(ohs, axis=0).astype(BF16), tri_ref[...], preferred_element_type=F32)
        for k in range(TOP_K):
            carry = carry_s[:, 0:1]
            pk = prefix[k * N_EXPERTS:(k + 1) * N_EXPERTS]
            rank = jnp.sum(jnp.where(onehots[k], pk - 1.0 + carry, 0.0), axis=0, keepdims=True)
            rank_ref[k:k + 1, lanes] = rank.astype(jnp.int32)
            carry_s[...] = carry_s[...] + jnp.sum(ohs[k], axis=1, keepdims=True)
    cnt_ref[...] = carry_s[...].astype(jnp.int32)


def _stage_m(fn, hs, gg, x, gt_m, sh_f, sc_f, g_out_r, g_ffn, w_out_bf, w_router_t, b_router, jmat, tri, tm):
    b, s, d = x.shape
    df = fn.shape[2]
    dr = hs.shape[2]
    nt = s // tm
    ne = w_router_t.shape[0]
    vec = pl.BlockSpec((1, 1, d), lambda i, t: (i, 0, 0))
    half = lambda dd: pl.BlockSpec((1, tm, dd), lambda i, t: (i, t, 0))
    full = lambda shape: pl.BlockSpec(shape, lambda i, t: (0,) * len(shape))
    tok = pl.BlockSpec((TOP_K, tm), lambda i, t: (0, i * nt + t))
    return pl.pallas_call(
        functools.partial(_stage_m_kernel, df=df),
        grid=(b, nt),
        in_specs=[half(df), half(dr), half(dr), half(d), vec, vec, vec,
                  full((1, dr)), full((1, d)), full((d, d)), full((ne, d)), full((ne, 1)),
                  full((GRID_W, GRID_W)), full(tri.shape)],
        out_specs=[half(d), half(d // 2), tok, tok, tok, full((N_EXPERTS, LANES))],
        out_shape=[jax.ShapeDtypeStruct((b, s, d), F32), jax.ShapeDtypeStruct((b, s, d // 2), jnp.int32),
                   jax.ShapeDtypeStruct((TOP_K, b * s), jnp.int32),
                   jax.ShapeDtypeStruct((TOP_K, b * s), F32),
                   jax.ShapeDtypeStruct((TOP_K, b * s), jnp.int32),
                   jax.ShapeDtypeStruct((N_EXPERTS, LANES), jnp.int32)],
        scratch_shapes=[pltpu.VMEM((N_EXPERTS, LANES), F32)],
        compiler_params=_cparams(("arbitrary", "arbitrary")),
        name="stage_m",
    )(fn, hs, gg, x, gt_m, sh_f, sc_f, g_out_r.reshape(1, dr), g_ffn.reshape(1, d), w_out_bf,
      w_router_t, b_router.reshape(ne, 1), jmat, tri)


def _moe_kernel(be_ref, bv_ref, nu_ref, x_ref, wgu_ref, bgu_ref, wd_ref, bd_ref, o_ref, wgu_s, wd_s, *, dff):
    i = pl.program_id(0)
    h = x_ref.shape[1]

    @pl.when(jnp.logical_or(i == 0, be_ref[i] != be_ref[jnp.maximum(i - 1, 0)]))
    def _():
        wgu_s[...] = wgu_ref[0].astype(BF16)
        wd_s[...] = wd_ref[0].astype(BF16)

    @pl.when(i < nu_ref[0])
    def _():
        rows = lax.broadcasted_iota(jnp.int32, x_ref.shape, 0)
        xw = jnp.where(rows < bv_ref[i], x_ref[...], 0)
        xa, xb = _unpack_halves(xw)
        xa = xa.astype(BF16)
        xb = xb.astype(BF16)
        acc = None
        for c in range(dff // MOE_FF_CHUNK):
            gs = slice(c * MOE_FF_CHUNK, (c + 1) * MOE_FF_CHUNK)
            us = slice(dff + c * MOE_FF_CHUNK, dff + (c + 1) * MOE_FF_CHUNK)
            g = jnp.dot(xa, wgu_s[:h, gs], preferred_element_type=F32)
            g += jnp.dot(xb, wgu_s[h:, gs], preferred_element_type=F32)
            u = jnp.dot(xa, wgu_s[:h, us], preferred_element_type=F32)
            u += jnp.dot(xb, wgu_s[h:, us], preferred_element_type=F32)
            gt = jnp.minimum(g + bgu_ref[0, :, gs], SWIGLU_LIMIT)
            up = jnp.clip(u + bgu_ref[0, :, us], -SWIGLU_LIMIT, SWIGLU_LIMIT)
            act = (up + 1.0) * (gt * _sigmoid(SWIGLU_ALPHA * gt))
            part = jnp.dot(act.astype(BF16), wd_s[gs, :], preferred_element_type=F32)
            acc = part if acc is None else acc + part
        o_ref[...] = _pack_halves(acc + bd_ref[0])


def _moe_experts(blk_expert, blk_valid, n_used, xs, wgu, bgu, wd, bd, tmm):
    cap, h = xs.shape
    ne, d, dff2 = wgu.shape
    dff = dff2 // 2
    row_blk = lambda i, be, bv, nu: (jnp.maximum(jnp.minimum(i, nu[0] - 1), 0), 0)
    wsel = lambda i, be, bv, nu: (be[i], 0, 0)
    grid_spec = pltpu.PrefetchScalarGridSpec(
        num_scalar_prefetch=3,
        grid=(cap // tmm,),
        in_specs=[pl.BlockSpec((tmm, h), row_blk),
                  pl.BlockSpec((1, d, dff2), wsel),
                  pl.BlockSpec((1, 1, dff2), wsel),
                  pl.BlockSpec((1, dff, d), wsel),
                  pl.BlockSpec((1, 1, d), wsel)],
        out_specs=pl.BlockSpec((tmm, h), row_blk),
        scratch_shapes=[pltpu.VMEM((d, dff2), BF16), pltpu.VMEM((dff, d), BF16)],
    )
    return pl.pallas_call(
        functools.partial(_moe_kernel, dff=dff),
        grid_spec=grid_spec,
        out_shape=jax.ShapeDtypeStruct((cap, h), jnp.int32),
        compiler_params=_cparams(("arbitrary",)),
        name="moe_experts",
    )(blk_expert, blk_valid, n_used, xs, wgu, bgu.reshape(ne, 1, dff2), wd, bd.reshape(ne, 1, d))


SC_CHUNK = 64
SC_ID_CHUNK = 128


def _sc_workers():
    info = plsc.get_sparse_core_info()
    return info.num_cores, info.num_subcores


def _sc_scatter_into(out_ref, rows, dest):
    n, w = rows.shape
    nc, ns = _sc_workers()
    per_w = n // (nc * ns)
    assert per_w % SC_CHUNK == 0
    mesh = plsc.VectorSubcoreMesh(core_axis_name="c", subcore_axis_name="s")

    @functools.partial(
        pl.kernel, mesh=mesh, out_type=(),
        scratch_types=[pltpu.VMEM((SC_CHUNK,), jnp.int32),
                       pltpu.VMEM((SC_CHUNK, w), jnp.int32),
                       pltpu.SemaphoreType.DMA],
    )
    def scatter_rows(rows_hbm, dest_hbm, out_hbm, idx_v, rows_v, sem):
        base = (lax.axis_index("s") * nc + lax.axis_index("c")) * per_w

        @pl.loop(0, per_w // SC_CHUNK)
        def _(j):
            off = pl.multiple_of(base + j * SC_CHUNK, SC_CHUNK)
            pltpu.sync_copy(rows_hbm.at[pl.ds(off, SC_CHUNK)], rows_v)
            pltpu.sync_copy(dest_hbm.at[pl.ds(off, SC_CHUNK)], idx_v)
            pltpu.async_copy(rows_v, out_hbm.at[idx_v], sem).wait()

    scatter_rows(rows, dest, out_ref)


def _sc_scatter_ids(dest, out_rows):
    n = dest.shape[0]
    info = plsc.get_sparse_core_info()
    nc, ns, nl = info.num_cores, info.num_subcores, info.num_lanes
    per_w = n // (nc * ns)
    chunk = SC_ID_CHUNK
    assert per_w % (2 * chunk) == 0
    mesh = plsc.VectorSubcoreMesh(core_axis_name="c", subcore_axis_name="s")
    idx_t = pltpu.VMEM((chunk,), jnp.int32)
    rows_t = pltpu.VMEM((chunk, LANES), jnp.int32)

    @functools.partial(
        pl.kernel, mesh=mesh,
        out_type=jax.ShapeDtypeStruct((out_rows, LANES), jnp.int32),
        scratch_types=[idx_t, idx_t, rows_t, rows_t, pltpu.SemaphoreType.DMA, pltpu.SemaphoreType.DMA],
    )
    def scatter_ids(dest_hbm, out_hbm, idx_a, idx_b, rows_a, rows_b, sem_a, sem_b):
        base = (lax.axis_index("s") * nc + lax.axis_index("c")) * per_w

        def start(off, idx_v, rows_v, sem):
            for r in range(chunk):
                rows_v[r, pl.ds(0, nl)] = jnp.zeros((nl,), jnp.int32) + (off + r)
            pltpu.sync_copy(dest_hbm.at[pl.ds(off, chunk)], idx_v)
            return pltpu.async_copy(rows_v, out_hbm.at[idx_v], sem)

        @pl.loop(0, per_w // (2 * chunk))
        def _(j):
            off = pl.multiple_of(base + j * (2 * chunk), 2 * chunk)
            copy_a = start(off, idx_a, rows_a, sem_a)
            copy_b = start(off + chunk, idx_b, rows_b, sem_b)
            copy_a.wait()
            copy_b.wait()

    return scatter_ids(dest)


def _sc_gather(table, idx):
    n = idx.shape[0]
    w = table.shape[1]
    nc, ns = _sc_workers()
    per_w = n // (nc * ns)
    assert per_w % SC_CHUNK == 0
    mesh = plsc.VectorSubcoreMesh(core_axis_name="c", subcore_axis_name="s")

    @functools.partial(
        pl.kernel, mesh=mesh,
        out_type=jax.ShapeDtypeStruct((n, w), jnp.int32),
        scratch_types=[pltpu.VMEM((SC_CHUNK,), jnp.int32),
                       pltpu.VMEM((SC_CHUNK, w), jnp.int32),
                       pltpu.SemaphoreType.DMA],
    )
    def gather_rows(table_hbm, idx_hbm, out_hbm, idx_v, rows_v, sem):
        base = (lax.axis_index("s") * nc + lax.axis_index("c")) * per_w

        @pl.loop(0, per_w // SC_CHUNK)
        def _(j):
            off = pl.multiple_of(base + j * SC_CHUNK, SC_CHUNK)
            pltpu.sync_copy(idx_hbm.at[pl.ds(off, SC_CHUNK)], idx_v)
            pltpu.async_copy(table_hbm.at[idx_v], rows_v, sem).wait()
            pltpu.sync_copy(rows_v, out_hbm.at[pl.ds(off, SC_CHUNK)])

    return gather_rows(table, idx)


def _combine_kernel(x1_ref, y_ref, gate_ref, gtf_ref, g_ref, o_ref):
    h = y_ref.shape[3]
    gates = gate_ref[0]
    moe_a = moe_b = None
    for k in range(TOP_K):
        ya, yb = _unpack_halves(y_ref[k, 0])
        gk = gates[:, k:k + 1]
        moe_a = gk * ya if k == 0 else moe_a + gk * ya
        moe_b = gk * yb if k == 0 else moe_b + gk * yb
    za = x1_ref[0, :, :h] + gtf_ref[0, :, :h] * moe_a
    zb = x1_ref[0, :, h:] + gtf_ref[0, :, h:] * moe_b
    ms = (jnp.sum(za * za, axis=-1, keepdims=True) + jnp.sum(zb * zb, axis=-1, keepdims=True)) / (2 * h)
    inv = lax.rsqrt(ms + EPS)
    o_ref[0, :, :h] = za * inv * g_ref[:, :h]
    o_ref[0, :, h:] = zb * inv * g_ref[:, h:]


def _combine(x1, yk, gates_t, gt_f, g_final, tm):
    b, s, d = x1.shape
    h = yk.shape[3]
    return pl.pallas_call(
        _combine_kernel,
        grid=(b, s // tm),
        in_specs=[pl.BlockSpec((1, tm, d), lambda i, t: (i, t, 0)),
                  pl.BlockSpec((TOP_K, 1, tm, h), lambda i, t: (0, i, t, 0)),
                  pl.BlockSpec((1, tm, TOP_K), lambda i, t: (i, t, 0)),
                  pl.BlockSpec((1, 1, d), lambda i, t: (i, 0, 0)),
                  pl.BlockSpec((1, d), lambda i, t: (0, 0))],
        out_specs=pl.BlockSpec((1, tm, d), lambda i, t: (i, t, 0)),
        out_shape=jax.ShapeDtypeStruct((b, s, d), F32),
        compiler_params=_cparams(("parallel", "arbitrary")),
        name="combine",
    )(x1, yk, gates_t, gt_f, g_final.reshape(1, d))


def _dft_tables(rows, gd):
    seq = rows * GRID_W
    n = np.arange(rows)
    ang1 = 2.0 * np.pi * np.outer(n, n) / rows
    d2 = np.concatenate([np.cos(ang1), -np.sin(ang1)], axis=0)
    k1 = np.arange(rows)[:, None, None]
    k2 = np.arange(GRID_W)[None, :, None]
    n2 = np.arange(GRID_W)[None, None, :]
    ang2 = 2.0 * np.pi * ((n2 * (k1 + rows * k2)) % seq) / seq
    ec, es = np.cos(ang2), np.sin(ang2)
    etab = np.concatenate([np.concatenate([ec, es], axis=2),
                           np.concatenate([-es, ec], axis=2)], axis=1)
    c = np.arange(gd)
    angc = 2.0 * np.pi * np.outer(c, c) / gd
    scale = 1.0 / np.sqrt(seq * gd)
    return (jnp.asarray(d2, BF16), jnp.asarray(etab, BF16),
            jnp.asarray(np.cos(angc) * scale, F32), jnp.asarray(np.sin(angc) * scale, F32))


def _block_diag(w):
    h, i, o = w.shape
    eye = jnp.eye(h, dtype=w.dtype)
    return (eye[:, None, :, None] * w[:, :, None, :]).reshape(h * i, h * o)


def kernel(x, c, ctx, c_ctx, w_mod, b_mod, g_norm_mix, g_norm_ffn, w_in, w_fourier, conv_w, conv_b,
           rg_w_a, rg_b_a, rg_w_x, rg_b_x, rg_lam, g_out_fourier, g_out_rg, w_out, w_router, b_router,
           w_gate_up, b_gate_up, w_down, b_down, g_final):
    assert w_mod.shape[0] == 1, "single-layer stack only"
    b, s, d = x.shape
    df = w_fourier.shape[1] * w_fourier.shape[2]
    dr = conv_w.shape[2]
    gd = w_fourier.shape[2]
    rows = s // GRID_W
    t = b * s
    ne = w_router.shape[2]

    mrows = -(-(b + 1) // SUBLANES) * SUBLANES
    cond = jnp.zeros((mrows, d), F32).at[:b].set(c).at[b].set(c_ctx)
    mod = _adaln(cond, w_mod[0], b_mod[0])
    sh_m, sc_m, gt_m, sh_f, sc_f, gt_f = [mod[:b, k * d:(k + 1) * d].reshape(b, 1, d) for k in range(N_MOD)]
    csh_m = mod[b:b + 1, 0:d].reshape(1, 1, d)
    csc_m = mod[b:b + 1, d:2 * d].reshape(1, 1, d)

    tm = min(s, TOKEN_TILE)
    d2, etab, cmat, smat = _dft_tables(rows, gd)
    jmat = jnp.asarray(np.eye(GRID_W)[::-1].copy(), BF16)

    w_in_bf = w_in[0].astype(BF16)
    f, xs, gg = _stage_b(x, sh_m, sc_m, g_norm_mix[0], w_in_bf, jmat, df, dr, tm=tm)
    xr_ctx = _stage_b_ctx(ctx, csh_m, csc_m, g_norm_mix[0], w_in_bf, df, dr)

    wcat = jnp.stack([jnp.concatenate([_block_diag(rg_w_a[0, dd]), _block_diag(rg_w_x[0, dd])], axis=1)
                      for dd in range(2)]).astype(BF16)
    bcat = jnp.concatenate([rg_b_a[0], rg_b_x[0]], axis=1).reshape(2, 1, 2 * dr)
    lam = rg_lam[0].reshape(2, 1, dr)
    h0 = jnp.zeros((b, SUBLANES, dr), F32)
    _, hfin_ctx = _rg_scan(xr_ctx, h0, conv_w[0], conv_b[0], wcat, bcat, lam, tc=ctx.shape[1])
    hs, _ = _rg_scan(xs, hfin_ctx, conv_w[0], conv_b[0], wcat, bcat, lam, tc=min(s, SCAN_CHUNK))

    cw, sw = _fold_fourier(cmat, smat, w_fourier[0])
    wcs = jnp.concatenate([cw, sw], axis=1).astype(BF16)
    y = _fourier_stage1(f, d2, tl=GRID_W * df)
    fn = _fourier_stage2(y.reshape(b, 2, rows, GRID_W, df), etab, wcs, g_out_fourier[0],
                         kb=min(rows, 32))
    fn = fn.reshape(b, s, df)

    tl = min(tm, RANK_TILE)
    tri = jnp.asarray(np.triu(np.ones((tl, tl))), BF16)
    x1, h2, idx, gates, rank, cnt = _stage_m(fn, hs, gg, x, gt_m, sh_f, sc_f, g_out_rg[0], g_norm_ffn[0],
                                             w_out[0].astype(BF16), w_router[0].T, b_router[0], jmat, tri, tm=tm)

    counts = cnt[:, 0]
    tmm = MOE_ROW_TILE
    padded = (counts + tmm - 1) // tmm * tmm
    pad_end = jnp.cumsum(padded)
    pad_start = pad_end - padded
    eids = jnp.arange(ne, dtype=jnp.int32)
    dest = rank + jnp.sum(jnp.where(idx[:, :, None] == eids, pad_start, 0), axis=-1)
    n_blocks = -(-(t * TOP_K) // tmm) + ne
    cap = n_blocks * tmm
    n_used = (pad_end[-1] // tmm).astype(jnp.int32).reshape(1)
    blk_start = jnp.arange(n_blocks, dtype=jnp.int32) * tmm
    blk_expert = jnp.sum(blk_start[:, None] >= pad_end[None, :], axis=1).astype(jnp.int32)
    last_expert = jnp.sum(pad_end[-1] - tmm >= pad_end).astype(jnp.int32)
    blk_expert = jnp.minimum(blk_expert, last_expert)
    sel = blk_expert[:, None] == eids
    blk_first = jnp.sum(jnp.where(sel, pad_start, 0), axis=1)
    blk_count = jnp.sum(jnp.where(sel, counts, 0), axis=1)
    blk_valid = jnp.clip(blk_count - (blk_start - blk_first), 0, tmm).astype(jnp.int32)

    na = TOP_K * t
    inv = _sc_scatter_ids(dest.reshape(-1), cap)[:, 0].reshape(n_blocks, tmm)
    live = jnp.arange(tmm, dtype=jnp.int32)[None, :] < blk_valid[:, None]
    spread = jnp.arange(cap, dtype=jnp.int32).reshape(n_blocks, tmm) % t
    src_tok = jnp.where(live, inv % t, spread).reshape(-1)
    dst_row = jnp.where(live, inv, na + spread).reshape(-1)

    h2_rows = h2.reshape(t, d // 2)
    y_all = jax.empty_ref(jax.ShapeDtypeStruct(((TOP_K + 1) * t, d // 2), jnp.int32))
    unit = n_blocks // sum(MOE_PIPE)
    assert unit * sum(MOE_PIPE) == n_blocks
    blk0 = 0
    for parts in MOE_PIPE:
        nq = parts * unit
        blocks = slice(blk0, blk0 + nq)
        rows = slice(blk0 * tmm, (blk0 + nq) * tmm)
        x_q = _sc_gather(h2_rows, src_tok[rows])
        nu_q = jnp.clip(n_used - blk0, 0, nq).astype(jnp.int32)
        blk0 += nq
        y_q = _moe_experts(blk_expert[blocks], blk_valid[blocks], nu_q, x_q,
                           w_gate_up[0], b_gate_up[0], w_down[0], b_down[0], tmm)
        _sc_scatter_into(y_all, y_q, dst_row[rows])
    yk = y_all[...].reshape(TOP_K + 1, b, s, d // 2)
    return _combine(x1, yk, gates.T.reshape(b, s, TOP_K), gt_f, g_final, tm)
```
